```python
import math
import jax, jax.numpy as jnp
from jax import lax
import numpy as np

D_MODEL = 2048
BATCH = 8
SEQ = 4096
DEPTH = 2

MIX = D_MODEL
D_MLA = MIX // 2
D_CONV = MIX - D_MLA
N_HEADS = 8
NOPE_DIM = 128
ROPE_DIM = 64
V_DIM = D_MLA // N_HEADS
QK_DIM = NOPE_DIM + ROPE_DIM
Q_LORA = 512
KV_LORA = 256
ROPE_THETA = 10000.0
Q_BLOCK = 128
CONV_K = 31
IN_COLS = Q_LORA + KV_LORA + ROPE_DIM + D_MLA + 2 * D_CONV + D_CONV
EPS = 1e-6

kernel_name = "hybrid_mla_conformer_conv_headgroups"


def rms_norm(x, g):
    xf = x.astype(jnp.float32)
    y = xf * lax.rsqrt(jnp.mean(xf * xf, axis=-1, keepdims=True) + EPS)
    return (y * g.astype(jnp.float32)).astype(x.dtype)


def layer_norm(x, g, b):
    xf = x.astype(jnp.float32)
    mu = jnp.mean(xf, axis=-1, keepdims=True)
    var = jnp.mean(jnp.square(xf - mu), axis=-1, keepdims=True)
    y = (xf - mu) * lax.rsqrt(var + EPS)
    return (y * g.astype(jnp.float32) + b.astype(jnp.float32)).astype(x.dtype)


def rope_tables(positions, dtype):
    inv_freq = 1.0 / (ROPE_THETA ** (jnp.arange(0, ROPE_DIM, 2, dtype=jnp.float32) / ROPE_DIM))
    ang = positions.astype(jnp.float32)[..., None] * inv_freq
    return jnp.cos(ang)[:, :, None, :].astype(dtype), jnp.sin(ang)[:, :, None, :].astype(dtype)


def apply_rope(x, cos, sin):
    x1, x2 = jnp.split(x, 2, axis=-1)
    return jnp.concatenate([x1 * cos - x2 * sin, x2 * cos + x1 * sin], axis=-1)


def causal_block_attention(q, k, v):
    b, s, h, dq = q.shape
    nb = s // Q_BLOCK
    scale = 1.0 / math.sqrt(dq)
    qb = q.reshape(b, nb, Q_BLOCK, h, dq).transpose(1, 0, 2, 3, 4)
    key_pos = jnp.arange(s)

    def one_block(args):
        q_i, blk = args
        scores = jnp.einsum('bqhd,bkhd->bhqk', q_i, k).astype(jnp.float32) * scale
        q_pos = blk * Q_BLOCK + jnp.arange(Q_BLOCK)
        mask = key_pos[None, :] <= q_pos[:, None]
        scores = jnp.where(mask[None, None], scores, -jnp.inf)
        p = jax.nn.softmax(scores, axis=-1).astype(v.dtype)
        return jnp.einsum('bhqk,bkhd->bqhd', p, v)

    out = lax.map(one_block, (qb, jnp.arange(nb)))
    return out.transpose(1, 0, 2, 3, 4).reshape(b, s, h, v.shape[-1])


def causal_depthwise_conv(u, w, bias):
    out = lax.conv_general_dilated(
        u, w[:, None, :].astype(u.dtype),
        window_strides=(1,), padding=((CONV_K - 1, 0),),
        dimension_numbers=('NWC', 'WIO', 'NWC'),
        feature_group_count=u.shape[-1])
    return out + bias


def _fwd_setup_inputs(seed: int = 0) -> dict:
    key = jax.random.key(seed)
    ks = jax.random.split(key, 24)
    f32 = jnp.float32

    def w(k, shape, fan_in):
        return jax.random.normal(k, shape, f32) * (fan_in ** -0.5)

    def gain(k, shape):
        return 1.0 + 0.05 * jax.random.normal(k, shape, f32)

    def small(k, shape):
        return 0.02 * jax.random.normal(k, shape, f32)

    x = jax.random.normal(ks[0], (BATCH, SEQ, D_MODEL), f32)
    c = jax.random.normal(ks[1], (BATCH, D_MODEL), f32)
    offsets = jax.random.randint(ks[2], (BATCH, 1), 0, 1024, dtype=jnp.int32)
    positions = offsets + jnp.arange(SEQ, dtype=jnp.int32)[None, :]
    return {
        'x': x,
        'c': c,
        'positions': positions,
        'ada_w': w(ks[3], (DEPTH, D_MODEL, 3 * D_MODEL), D_MODEL),
        'ada_b': small(ks[4], (DEPTH, 3 * D_MODEL)),
        'norm_g': gain(ks[5], (DEPTH, D_MODEL)),
        'w_in': w(ks[6], (DEPTH, D_MODEL, IN_COLS), D_MODEL),
        'q_lat_g': gain(ks[7], (DEPTH, Q_LORA)),
        'w_q_up': w(ks[8], (DEPTH, Q_LORA, N_HEADS * QK_DIM), Q_LORA),
        'kv_lat_g': gain(ks[9], (DEPTH, KV_LORA)),
        'w_kv_up': w(ks[10], (DEPTH, KV_LORA, N_HEADS * (NOPE_DIM + V_DIM)), KV_LORA),
        'q_norm_g': gain(ks[11], (DEPTH, QK_DIM)),
        'k_norm_g': gain(ks[12], (DEPTH, QK_DIM)),
        'glu_b': small(ks[13], (DEPTH, 2 * D_CONV)),
        'dw_w': w(ks[14], (DEPTH, CONV_K, D_CONV), CONV_K),
        'dw_b': small(ks[15], (DEPTH, D_CONV)),
        'conv_ln_g': gain(ks[16], (DEPTH, D_CONV)),
        'conv_ln_b': small(ks[17], (DEPTH, D_CONV)),
        'w_pw': w(ks[18], (DEPTH, D_CONV, D_CONV), D_CONV),
        'b_pw': small(ks[19], (DEPTH, D_CONV)),
        'w_out': w(ks[20], (DEPTH, MIX, D_MODEL), MIX),
    }


def _fwd_reference(x, c, positions, ada_w, ada_b, norm_g, w_in, q_lat_g, w_q_up, kv_lat_g,
              w_kv_up, q_norm_g, k_norm_g, glu_b, dw_w, dw_b, conv_ln_g, conv_ln_b,
              w_pw, b_pw, w_out):
    b, s, _ = x.shape
    cos, sin = rope_tables(positions, x.dtype)
    c_act = jax.nn.silu(c)
    splits = np.cumsum([Q_LORA, KV_LORA, ROPE_DIM, D_MLA, 2 * D_CONV]).tolist()

    for l in range(DEPTH):
        mod = c_act @ ada_w[l] + ada_b[l]
        shift, scale, gate = [m[:, None, :] for m in jnp.split(mod, 3, axis=-1)]
        h = rms_norm(x, norm_g[l]) * (1.0 + scale) + shift

        z = h @ w_in[l]
        q_lat, kv_lat, k_rope, mla_gate, conv_in, conv_gate = jnp.split(z, splits, axis=-1)

        q = (rms_norm(q_lat, q_lat_g[l]) @ w_q_up[l]).reshape(b, s, N_HEADS, QK_DIM)
        kv = (rms_norm(kv_lat, kv_lat_g[l]) @ w_kv_up[l]).reshape(b, s, N_HEADS, NOPE_DIM + V_DIM)
        k_nope, v = kv[..., :NOPE_DIM], kv[..., NOPE_DIM:]
        k_rope_h = jnp.broadcast_to(k_rope[:, :, None, :], (b, s, N_HEADS, ROPE_DIM))
        k = jnp.concatenate([k_nope, k_rope_h], axis=-1)
        q = rms_norm(q, q_norm_g[l])
        k = rms_norm(k, k_norm_g[l])
        q = jnp.concatenate([q[..., :NOPE_DIM], apply_rope(q[..., NOPE_DIM:], cos, sin)], axis=-1)
        k = jnp.concatenate([k[..., :NOPE_DIM], apply_rope(k[..., NOPE_DIM:], cos, sin)], axis=-1)
        attn = causal_block_attention(q, k, v).reshape(b, s, D_MLA)
        mla_out = attn * jax.nn.silu(mla_gate)

        u_val, u_gate = jnp.split(conv_in + glu_b[l], 2, axis=-1)
        u = u_val * jax.nn.sigmoid(u_gate)
        u = causal_depthwise_conv(u, dw_w[l], dw_b[l])
        u = jax.nn.silu(layer_norm(u, conv_ln_g[l], conv_ln_b[l]))
        u = u @ w_pw[l] + b_pw[l]
        conv_out = u * jax.nn.silu(conv_gate)

        y = jnp.concatenate([mla_out, conv_out], axis=-1) @ w_out[l]
        x = x + gate * y
    return x


import jax as _jax
import jax.numpy as _jnp

TWIN_FORMAT = 'train_step'
FWD_PARAMS = ['x', 'c', 'positions', 'ada_w', 'ada_b', 'norm_g', 'w_in', 'q_lat_g', 'w_q_up', 'kv_lat_g', 'w_kv_up', 'q_norm_g', 'k_norm_g', 'glu_b', 'dw_w', 'dw_b', 'conv_ln_g', 'conv_ln_b', 'w_pw', 'b_pw', 'w_out']
TWIN_WEIGHTS = ['ada_w', 'ada_b', 'norm_g', 'w_in', 'q_lat_g', 'w_q_up', 'kv_lat_g', 'w_kv_up', 'q_norm_g', 'k_norm_g', 'glu_b', 'dw_w', 'dw_b', 'conv_ln_g', 'conv_ln_b', 'w_pw', 'b_pw', 'w_out']
TWIN_DIFF_INPUT = 'x'
TWIN_INPUTS = ['x', 'c', 'positions', 'ada_w', 'ada_b', 'norm_g', 'w_in', 'q_lat_g', 'w_q_up', 'kv_lat_g', 'w_kv_up', 'q_norm_g', 'k_norm_g', 'glu_b', 'dw_w', 'dw_b', 'conv_ln_g', 'conv_ln_b', 'w_pw', 'b_pw', 'w_out', 'loss_target', 'm_ada_w', 'm_ada_b', 'm_norm_g', 'm_w_in', 'm_q_lat_g', 'm_w_q_up', 'm_kv_lat_g', 'm_w_kv_up', 'm_q_norm_g', 'm_k_norm_g', 'm_glu_b', 'm_dw_w', 'm_dw_b', 'm_conv_ln_g', 'm_conv_ln_b', 'm_w_pw', 'm_b_pw', 'm_w_out', 'v_ada_w', 'v_ada_b', 'v_norm_g', 'v_w_in', 'v_q_lat_g', 'v_w_q_up', 'v_kv_lat_g', 'v_w_kv_up', 'v_q_norm_g', 'v_k_norm_g', 'v_glu_b', 'v_dw_w', 'v_dw_b', 'v_conv_ln_g', 'v_conv_ln_b', 'v_w_pw', 'v_b_pw', 'v_w_out']
TWIN_OUTPUTS = ['loss', 'grad_x', 'grad_ada_w', 'grad_ada_b', 'grad_norm_g', 'grad_w_in', 'grad_q_lat_g', 'grad_w_q_up', 'grad_kv_lat_g', 'grad_w_kv_up', 'grad_q_norm_g', 'grad_k_norm_g', 'grad_glu_b', 'grad_dw_w', 'grad_dw_b', 'grad_conv_ln_g', 'grad_conv_ln_b', 'grad_w_pw', 'grad_b_pw', 'grad_w_out', 'delta_ada_w', 'delta_ada_b', 'delta_norm_g', 'delta_w_in', 'delta_q_lat_g', 'delta_w_q_up', 'delta_kv_lat_g', 'delta_w_kv_up', 'delta_q_norm_g', 'delta_k_norm_g', 'delta_glu_b', 'delta_dw_w', 'delta_dw_b', 'delta_conv_ln_g', 'delta_conv_ln_b', 'delta_w_pw', 'delta_b_pw', 'delta_w_out', 'new_m_ada_w', 'new_m_ada_b', 'new_m_norm_g', 'new_m_w_in', 'new_m_q_lat_g', 'new_m_w_q_up', 'new_m_kv_lat_g', 'new_m_w_kv_up', 'new_m_q_norm_g', 'new_m_k_norm_g', 'new_m_glu_b', 'new_m_dw_w', 'new_m_dw_b', 'new_m_conv_ln_g', 'new_m_conv_ln_b', 'new_m_w_pw', 'new_m_b_pw', 'new_m_w_out', 'new_v_ada_w', 'new_v_ada_b', 'new_v_norm_g', 'new_v_w_in', 'new_v_q_lat_g', 'new_v_w_q_up', 'new_v_kv_lat_g', 'new_v_w_kv_up', 'new_v_q_norm_g', 'new_v_k_norm_g', 'new_v_glu_b', 'new_v_dw_w', 'new_v_dw_b', 'new_v_conv_ln_g', 'new_v_conv_ln_b', 'new_v_w_pw', 'new_v_b_pw', 'new_v_w_out']
TWIN_LEAF_KINDS = {'loss': 'loss', 'grad_x': 'grad_x', 'grad_ada_w': 'grad_w', 'grad_ada_b': 'grad_w', 'grad_norm_g': 'grad_w', 'grad_w_in': 'grad_w', 'grad_q_lat_g': 'grad_w', 'grad_w_q_up': 'grad_w', 'grad_kv_lat_g': 'grad_w', 'grad_w_kv_up': 'grad_w', 'grad_q_norm_g': 'grad_w', 'grad_k_norm_g': 'grad_w', 'grad_glu_b': 'grad_w', 'grad_dw_w': 'grad_w', 'grad_dw_b': 'grad_w', 'grad_conv_ln_g': 'grad_w', 'grad_conv_ln_b': 'grad_w', 'grad_w_pw': 'grad_w', 'grad_b_pw': 'grad_w', 'grad_w_out': 'grad_w', 'delta_ada_w': 'delta_w', 'delta_ada_b': 'delta_w', 'delta_norm_g': 'delta_w', 'delta_w_in': 'delta_w', 'delta_q_lat_g': 'delta_w', 'delta_w_q_up': 'delta_w', 'delta_kv_lat_g': 'delta_w', 'delta_w_kv_up': 'delta_w', 'delta_q_norm_g': 'delta_w', 'delta_k_norm_g': 'delta_w', 'delta_glu_b': 'delta_w', 'delta_dw_w': 'delta_w', 'delta_dw_b': 'delta_w', 'delta_conv_ln_g': 'delta_w', 'delta_conv_ln_b': 'delta_w', 'delta_w_pw': 'delta_w', 'delta_b_pw': 'delta_w', 'delta_w_out': 'delta_w', 'new_m_ada_w': 'new_m', 'new_m_ada_b': 'new_m', 'new_m_norm_g': 'new_m', 'new_m_w_in': 'new_m', 'new_m_q_lat_g': 'new_m', 'new_m_w_q_up': 'new_m', 'new_m_kv_lat_g': 'new_m', 'new_m_w_kv_up': 'new_m', 'new_m_q_norm_g': 'new_m', 'new_m_k_norm_g': 'new_m', 'new_m_glu_b': 'new_m', 'new_m_dw_w': 'new_m', 'new_m_dw_b': 'new_m', 'new_m_conv_ln_g': 'new_m', 'new_m_conv_ln_b': 'new_m', 'new_m_w_pw': 'new_m', 'new_m_b_pw': 'new_m', 'new_m_w_out': 'new_m', 'new_v_ada_w': 'new_v', 'new_v_ada_b': 'new_v', 'new_v_norm_g': 'new_v', 'new_v_w_in': 'new_v', 'new_v_q_lat_g': 'new_v', 'new_v_w_q_up': 'new_v', 'new_v_kv_lat_g': 'new_v', 'new_v_w_kv_up': 'new_v', 'new_v_q_norm_g': 'new_v', 'new_v_k_norm_g': 'new_v', 'new_v_glu_b': 'new_v', 'new_v_dw_w': 'new_v', 'new_v_dw_b': 'new_v', 'new_v_conv_ln_g': 'new_v', 'new_v_conv_ln_b': 'new_v', 'new_v_w_pw': 'new_v', 'new_v_b_pw': 'new_v', 'new_v_w_out': 'new_v'}


def _forward(args):
    return _fwd_reference(*[args[k] for k in FWD_PARAMS])


def _output_shape():
    def fwd():
        inp = _fwd_setup_inputs(0)
        return _fwd_reference(*[inp[k] for k in FWD_PARAMS])
    out = _jax.eval_shape(fwd)
    return out.shape, out.dtype

N_MICROBATCH = 1
ADAM_LR = 0.001
ADAM_B1 = 0.9
ADAM_B2 = 0.999
ADAM_EPS = 1e-08
ADAM_WD = 0.01
ADAM_STEP = 10
PER_EXAMPLE_BATCH_AXIS = {'x': 0, 'c': 0, 'positions': 0, 'loss_target': 0}
SHARED_INPUTS = []
_WEIGHT_DTYPES = {'ada_w': _jnp.float32, 'ada_b': _jnp.float32, 'norm_g': _jnp.float32, 'w_in': _jnp.float32, 'q_lat_g': _jnp.float32, 'w_q_up': _jnp.float32, 'kv_lat_g': _jnp.float32, 'w_kv_up': _jnp.float32, 'q_norm_g': _jnp.float32, 'k_norm_g': _jnp.float32, 'glu_b': _jnp.float32, 'dw_w': _jnp.float32, 'dw_b': _jnp.float32, 'conv_ln_g': _jnp.float32, 'conv_ln_b': _jnp.float32, 'w_pw': _jnp.float32, 'b_pw': _jnp.float32, 'w_out': _jnp.float32}
MOMENT_SCALE = {'ada_w': 4.425348e-01, 'ada_b': 1.012494e+00, 'norm_g': 7.392273e-01, 'w_in': 3.675972e-01, 'q_lat_g': 2.083999e-02, 'w_q_up': 1.185796e-02, 'kv_lat_g': 3.729560e+00, 'w_kv_up': 3.047889e-01, 'q_norm_g': 9.543319e-02, 'k_norm_g': 9.570910e-02, 'glu_b': 4.371737e-01, 'dw_w': 2.572202e-01, 'dw_b': 8.389993e-01, 'conv_ln_g': 1.985825e+00, 'conv_ln_b': 1.262294e+00, 'w_pw': 2.722149e-01, 'b_pw': 1.102539e+00, 'w_out': 1.459105e-01}


def _to_microbatches(a, axis):
    t = _jnp.moveaxis(a, axis, 0)
    t = t.reshape((N_MICROBATCH, t.shape[0] // N_MICROBATCH) + t.shape[1:])
    return _jnp.moveaxis(t, 1, axis + 1)


def setup_inputs(seed: int = 0) -> dict:
    inp = _fwd_setup_inputs(seed)
    key = _jax.random.fold_in(_jax.random.key(seed), 7919)
    shape, _ = _output_shape()
    out = dict(inp)
    out["loss_target"] = _jax.random.normal(_jax.random.fold_in(key, 0), shape, _jnp.float32)
    for i, name in enumerate(TWIN_WEIGHTS):
        w = inp[name].astype(_jnp.float32)
        if MOMENT_SCALE is None:
            s = _jnp.sqrt(_jnp.mean(_jnp.square(w)) + 1e-30)
        else:
            s = MOMENT_SCALE[name]
        km, kv = _jax.random.split(_jax.random.fold_in(key, i + 1))
        out[name] = w
        out["m_" + name] = s * _jax.random.normal(km, w.shape, _jnp.float32)
        out["v_" + name] = (s * s) * _jax.random.uniform(kv, w.shape, _jnp.float32, 0.5, 1.5)
    if N_MICROBATCH > 1:
        for name, axis in PER_EXAMPLE_BATCH_AXIS.items():
            out[name] = _to_microbatches(out[name], axis)
    return {'x': out['x'], 'c': out['c'], 'positions': out['positions'], 'ada_w': out['ada_w'], 'ada_b': out['ada_b'], 'norm_g': out['norm_g'], 'w_in': out['w_in'], 'q_lat_g': out['q_lat_g'], 'w_q_up': out['w_q_up'], 'kv_lat_g': out['kv_lat_g'], 'w_kv_up': out['w_kv_up'], 'q_norm_g': out['q_norm_g'], 'k_norm_g': out['k_norm_g'], 'glu_b': out['glu_b'], 'dw_w': out['dw_w'], 'dw_b': out['dw_b'], 'conv_ln_g': out['conv_ln_g'], 'conv_ln_b': out['conv_ln_b'], 'w_pw': out['w_pw'], 'b_pw': out['b_pw'], 'w_out': out['w_out'], 'loss_target': out['loss_target'], 'm_ada_w': out['m_ada_w'], 'm_ada_b': out['m_ada_b'], 'm_norm_g': out['m_norm_g'], 'm_w_in': out['m_w_in'], 'm_q_lat_g': out['m_q_lat_g'], 'm_w_q_up': out['m_w_q_up'], 'm_kv_lat_g': out['m_kv_lat_g'], 'm_w_kv_up': out['m_w_kv_up'], 'm_q_norm_g': out['m_q_norm_g'], 'm_k_norm_g': out['m_k_norm_g'], 'm_glu_b': out['m_glu_b'], 'm_dw_w': out['m_dw_w'], 'm_dw_b': out['m_dw_b'], 'm_conv_ln_g': out['m_conv_ln_g'], 'm_conv_ln_b': out['m_conv_ln_b'], 'm_w_pw': out['m_w_pw'], 'm_b_pw': out['m_b_pw'], 'm_w_out': out['m_w_out'], 'v_ada_w': out['v_ada_w'], 'v_ada_b': out['v_ada_b'], 'v_norm_g': out['v_norm_g'], 'v_w_in': out['v_w_in'], 'v_q_lat_g': out['v_q_lat_g'], 'v_w_q_up': out['v_w_q_up'], 'v_kv_lat_g': out['v_kv_lat_g'], 'v_w_kv_up': out['v_w_kv_up'], 'v_q_norm_g': out['v_q_norm_g'], 'v_k_norm_g': out['v_k_norm_g'], 'v_glu_b': out['v_glu_b'], 'v_dw_w': out['v_dw_w'], 'v_dw_b': out['v_dw_b'], 'v_conv_ln_g': out['v_conv_ln_g'], 'v_conv_ln_b': out['v_conv_ln_b'], 'v_w_pw': out['v_w_pw'], 'v_b_pw': out['v_b_pw'], 'v_w_out': out['v_w_out']}


def _loss(weights, diff, rest, loss_target):
    with _jax.named_scope("forward"):
        args = {**rest, TWIN_DIFF_INPUT: diff, **{k: w.astype(_WEIGHT_DTYPES[k]) for k, w in weights.items()}}
        y = _forward(args)
    with _jax.named_scope("loss_head"):
        err = _jnp.square(y.astype(_jnp.float32) - loss_target)
        return 0.5 * _jnp.sum(_jnp.mean(err, axis=-1)) if err.ndim else 0.5 * err


def _adamw(w, g, m, v):
    m = ADAM_B1 * m + (1.0 - ADAM_B1) * g
    v = ADAM_B2 * v + (1.0 - ADAM_B2) * _jnp.square(g)
    m_hat = m / (1.0 - ADAM_B1 ** ADAM_STEP)
    v_hat = v / (1.0 - ADAM_B2 ** ADAM_STEP)
    delta = -ADAM_LR * (m_hat / (_jnp.sqrt(v_hat) + ADAM_EPS) + ADAM_WD * w)
    return delta, m, v


def reference(x, c, positions, ada_w, ada_b, norm_g, w_in, q_lat_g, w_q_up, kv_lat_g, w_kv_up, q_norm_g, k_norm_g, glu_b, dw_w, dw_b, conv_ln_g, conv_ln_b, w_pw, b_pw, w_out, loss_target, m_ada_w, m_ada_b, m_norm_g, m_w_in, m_q_lat_g, m_w_q_up, m_kv_lat_g, m_w_kv_up, m_q_norm_g, m_k_norm_g, m_glu_b, m_dw_w, m_dw_b, m_conv_ln_g, m_conv_ln_b, m_w_pw, m_b_pw, m_w_out, v_ada_w, v_ada_b, v_norm_g, v_w_in, v_q_lat_g, v_w_q_up, v_kv_lat_g, v_w_kv_up, v_q_norm_g, v_k_norm_g, v_glu_b, v_dw_w, v_dw_b, v_conv_ln_g, v_conv_ln_b, v_w_pw, v_b_pw, v_w_out):
    given = dict(x=x, c=c, positions=positions, ada_w=ada_w, ada_b=ada_b, norm_g=norm_g, w_in=w_in, q_lat_g=q_lat_g, w_q_up=w_q_up, kv_lat_g=kv_lat_g, w_kv_up=w_kv_up, q_norm_g=q_norm_g, k_norm_g=k_norm_g, glu_b=glu_b, dw_w=dw_w, dw_b=dw_b, conv_ln_g=conv_ln_g, conv_ln_b=conv_ln_b, w_pw=w_pw, b_pw=b_pw, w_out=w_out, loss_target=loss_target, m_ada_w=m_ada_w, m_ada_b=m_ada_b, m_norm_g=m_norm_g, m_w_in=m_w_in, m_q_lat_g=m_q_lat_g, m_w_q_up=m_w_q_up, m_kv_lat_g=m_kv_lat_g, m_w_kv_up=m_w_kv_up, m_q_norm_g=m_q_norm_g, m_k_norm_g=m_k_norm_g, m_glu_b=m_glu_b, m_dw_w=m_dw_w, m_dw_b=m_dw_b, m_conv_ln_g=m_conv_ln_g, m_conv_ln_b=m_conv_ln_b, m_w_pw=m_w_pw, m_b_pw=m_b_pw, m_w_out=m_w_out, v_ada_w=v_ada_w, v_ada_b=v_ada_b, v_norm_g=v_norm_g, v_w_in=v_w_in, v_q_lat_g=v_q_lat_g, v_w_q_up=v_w_q_up, v_kv_lat_g=v_kv_lat_g, v_w_kv_up=v_w_kv_up, v_q_norm_g=v_q_norm_g, v_k_norm_g=v_k_norm_g, v_glu_b=v_glu_b, v_dw_w=v_dw_w, v_dw_b=v_dw_b, v_conv_ln_g=v_conv_ln_g, v_conv_ln_b=v_conv_ln_b, v_w_pw=v_w_pw, v_b_pw=v_b_pw, v_w_out=v_w_out)
    weights = {n: given[n] for n in TWIN_WEIGHTS}
    shared = {n: given[n] for n in SHARED_INPUTS}
    per_example = {n: given[n] for n in ['x', 'c', 'positions']}
    grad_fn = _jax.value_and_grad(_loss, argnums=(0, 1))

    def one_microbatch(ex, loss_target):
        ex = dict(ex)
        diff = ex.pop(TWIN_DIFF_INPUT)
        return grad_fn(weights, diff, {**shared, **ex}, loss_target)

    if N_MICROBATCH == 1:
        loss, (grad_w, grad_x) = one_microbatch(per_example, given["loss_target"])
    else:
        def body(carry, xs):
            loss_sum, grad_sum = carry
            l_k, (gw_k, gx_k) = one_microbatch(xs[0], xs[1])
            with _jax.named_scope("update"):
                return (loss_sum + l_k, _jax.tree.map(_jnp.add, grad_sum, gw_k)), gx_k

        init = (_jnp.zeros((), _jnp.float32), _jax.tree.map(_jnp.zeros_like, weights))
        (loss, grad_w), grad_x = _jax.lax.scan(body, init, (per_example, given["loss_target"]))
    with _jax.named_scope("update"):
        delta_w, new_m, new_v = {}, {}, {}
        for n in TWIN_WEIGHTS:
            delta_w[n], new_m[n], new_v[n] = _adamw(weights[n], grad_w[n], given["m_" + n], given["v_" + n])
    return (loss, grad_x, *[grad_w[n] for n in TWIN_WEIGHTS], *[delta_w[n] for n in TWIN_WEIGHTS],
            *[new_m[n] for n in TWIN_WEIGHTS], *[new_v[n] for n in TWIN_WEIGHTS])
```

```python
import functools
import math

import jax
import jax.numpy as jnp
from jax import lax
from jax.experimental import pallas as pl
from jax.experimental.pallas import tpu as pltpu

F32, BF16 = jnp.float32, jnp.bfloat16
MESH = pl.DeviceIdType.MESH
ANY = pl.BlockSpec(memory_space=pl.ANY)

N_HEADS, NOPE, ROPE, VD = 8, 128, 64, 128
QK = NOPE + ROPE
QL, KVL = 512, 256
HP = 256
CONV_K, HALO = 31, 32
ROPE_THETA = 10000.0
EPS = 1e-6
ADAM_LR, ADAM_B1, ADAM_B2, ADAM_EPS, ADAM_WD, ADAM_STEP = 0.001, 0.9, 0.999, 1e-08, 0.01, 10
V7X_VMEM_LIMIT = 56 * 1024 * 1024

NT = (((1,), (1,)), ((), ()))
TN = (((0,), (0,)), ((), ()))
NN = (((1,), (0,)), ((), ()))


def _dot(a, b, dims=NN):
    return lax.dot_general(a, b, dims, preferred_element_type=F32)


def _params(*sem):
    return pltpu.CompilerParams(dimension_semantics=sem or None, vmem_limit_bytes=V7X_VMEM_LIMIT)


def _sigmoid(x):
    return 1.0 / (1.0 + jnp.exp(-x))


def _sum0(x):
    return jnp.sum(x, axis=0, keepdims=True)


def _sum1(x):
    return jnp.sum(x, axis=1, keepdims=True)


def _row(n):
    return pl.BlockSpec((1, n), lambda *_: (0, 0))


def _place():
    x, y, c = lax.axis_index("x"), lax.axis_index("y"), lax.axis_index("c")
    chips = [(1 - x, y), (x, 1 - y), (1 - x, 1 - y)]
    return x, y, c, chips


def _allgather8(v, name):
    r, n = v.shape

    def body(v_ref, out_ref, send_sems, recv_sems, local_sem):
        x, y, c, chips = _place()
        me, sibling = (x, y, c), (x, y, 1 - c)

        def slot(px, py, pc):
            return out_ref.at[4 * px + 2 * py + pc]

        def copy(k, block, to, src=None):
            return pltpu.make_async_remote_copy(
                src_ref=slot(*block) if src is None else src, dst_ref=slot(*block),
                send_sem=send_sems.at[k], recv_sem=recv_sems.at[k], device_id=to, device_id_type=MESH)

        mine = pltpu.make_async_copy(v_ref, slot(*me), local_sem)
        mine.start()
        first = [copy(0, me, sibling, src=v_ref)]
        first += [copy(1 + j, me, (*chip, c), src=v_ref) for j, chip in enumerate(chips)]
        for cp in first:
            cp.start()
        passed = [copy(4 + j, (*chip, c), sibling) for j, chip in enumerate(chips)]
        for j, chip in enumerate(chips):
            copy(1 + j, (*chip, c), me).wait_recv()
            passed[j].start()
        copy(0, sibling, me).wait_recv()
        for j, chip in enumerate(chips):
            copy(4 + j, (*chip, 1 - c), me).wait_recv()
        for cp in first + passed:
            cp.wait_send()
        mine.wait()

    return pl.pallas_call(
        body, name=name, out_shape=jax.ShapeDtypeStruct((8, r, n), v.dtype),
        in_specs=[pl.BlockSpec(memory_space=pltpu.VMEM)], out_specs=pl.BlockSpec(memory_space=pltpu.VMEM),
        scratch_shapes=[pltpu.SemaphoreType.DMA((7,)), pltpu.SemaphoreType.DMA((7,)), pltpu.SemaphoreType.DMA],
    )(v)


def _allgather_shards(shards, name):
    ne = len(shards)

    def body(*refs):
        srcs, dsts = refs[:ne], refs[ne:2 * ne]
        send_sems, recv_sems, local_sems = refs[2 * ne:]
        x, y, c, chips = _place()
        sibling = (x, y, 1 - c)
        jme = 2 * x + y
        locals_, sends, fwds = [], [], []
        for e in range(ne):
            half = srcs[e].shape[2] // 2
            own = pl.ds(pl.multiple_of(c * half, 128), half)
            cp = pltpu.make_async_copy(srcs[e], dsts[e].at[:, jme], local_sems.at[e])
            cp.start()
            locals_.append(cp)
            for k, chip in enumerate(chips):
                cp = pltpu.make_async_remote_copy(
                    src_ref=srcs[e].at[:, :, own], dst_ref=dsts[e].at[:, jme, :, own],
                    send_sem=send_sems.at[6 * e + k], recv_sem=recv_sems.at[6 * e + k],
                    device_id=(*chip, c), device_id_type=MESH)
                cp.start()
                sends.append(cp)
        for e in range(ne):
            half = srcs[e].shape[2] // 2
            own = pl.ds(pl.multiple_of(c * half, 128), half)
            for k, (px, py) in enumerate(chips):
                landed = dsts[e].at[:, 2 * px + py, :, own]
                pltpu.make_async_remote_copy(
                    src_ref=landed, dst_ref=landed, send_sem=send_sems.at[6 * e + k],
                    recv_sem=recv_sems.at[6 * e + k], device_id=(px, py, c), device_id_type=MESH).wait_recv()
                cp = pltpu.make_async_remote_copy(
                    src_ref=landed, dst_ref=landed, send_sem=send_sems.at[6 * e + 3 + k],
                    recv_sem=recv_sems.at[6 * e + 3 + k], device_id=sibling, device_id_type=MESH)
                cp.start()
                fwds.append(cp)
        for e in range(ne):
            half = srcs[e].shape[2] // 2
            other = pl.ds(pl.multiple_of((1 - c) * half, 128), half)
            for k, (px, py) in enumerate(chips):
                theirs = dsts[e].at[:, 2 * px + py, :, other]
                pltpu.make_async_remote_copy(
                    src_ref=theirs, dst_ref=theirs, send_sem=send_sems.at[6 * e + 3 + k],
                    recv_sem=recv_sems.at[6 * e + 3 + k], device_id=sibling, device_id_type=MESH).wait_recv()
        for cp in sends + fwds:
            cp.wait_send()
        for cp in locals_:
            cp.wait()

    outs = pl.pallas_call(
        body, name=name,
        out_shape=[jax.ShapeDtypeStruct((s.shape[0], 4) + s.shape[1:], s.dtype) for s in shards],
        in_specs=[ANY] * ne, out_specs=[ANY] * ne,
        scratch_shapes=[pltpu.SemaphoreType.DMA((6 * ne,)), pltpu.SemaphoreType.DMA((6 * ne,)),
                        pltpu.SemaphoreType.DMA((ne,))],
    )(*shards)
    return list(outs)


def _pair_exchange(parts, name):
    ne = len(parts)

    def body(*refs):
        srcs, dsts = refs[:ne], refs[ne:2 * ne]
        send_sems, recv_sems = refs[2 * ne:]
        x, y, c, _ = _place()
        sibling = (x, y, 1 - c)
        cps = []
        for e in range(ne):
            half = srcs[e].shape[2] // 2
            theirs = pl.ds(pl.multiple_of((1 - c) * half, 128), half)
            cp = pltpu.make_async_remote_copy(
                src_ref=srcs[e].at[:, :, theirs], dst_ref=dsts[e], send_sem=send_sems.at[e],
                recv_sem=recv_sems.at[e], device_id=sibling, device_id_type=MESH)
            cp.start()
            cps.append(cp)
        for cp in cps:
            cp.wait()

    outs = pl.pallas_call(
        body, name=name,
        out_shape=[jax.ShapeDtypeStruct(p.shape[:2] + (p.shape[2] // 2,), p.dtype) for p in parts],
        in_specs=[ANY] * ne, out_specs=[ANY] * ne,
        scratch_shapes=[pltpu.SemaphoreType.DMA((ne,)), pltpu.SemaphoreType.DMA((ne,))],
    )(*parts)
    return list(outs)


def _chip_scatter(sums, name):
    ne = len(sums)

    def body(*refs):
        srcs, dsts = refs[:ne], refs[ne:2 * ne]
        send_sems, recv_sems = refs[2 * ne:]
        x, y, c, chips = _place()
        cps = []
        for e in range(ne):
            for k, (px, py) in enumerate(chips):
                cp = pltpu.make_async_remote_copy(
                    src_ref=srcs[e].at[2 * px + py], dst_ref=dsts[e].at[k], send_sem=send_sems.at[3 * e + k],
                    recv_sem=recv_sems.at[3 * e + k], device_id=(px, py, c), device_id_type=MESH)
                cp.start()
                cps.append(cp)
        for cp in cps:
            cp.wait()

    outs = pl.pallas_call(
        body, name=name,
        out_shape=[jax.ShapeDtypeStruct((3,) + s.shape[1:], s.dtype) for s in sums],
        in_specs=[ANY] * ne, out_specs=[ANY] * ne,
        scratch_shapes=[pltpu.SemaphoreType.DMA((3 * ne,)), pltpu.SemaphoreType.DMA((3 * ne,))],
    )(*sums)
    return list(outs)


def _pair_complete(grads, name):
    ne = len(grads)

    def body(*refs):
        srcs, dsts = refs[:ne], refs[ne:2 * ne]
        send_sems, recv_sems = refs[2 * ne:]
        x, y, c, _ = _place()
        sibling = (x, y, 1 - c)
        cps = []
        for e in range(ne):
            half = srcs[e].shape[1] // 2
            own = pl.ds(pl.multiple_of(c * half, 128), half)
            cp = pltpu.make_async_remote_copy(
                src_ref=dsts[e].at[:, own], dst_ref=dsts[e].at[:, own], send_sem=send_sems.at[e],
                recv_sem=recv_sems.at[e], device_id=sibling, device_id_type=MESH)
            cp.start()
            cps.append(cp)
        for e, cp in enumerate(cps):
            half = srcs[e].shape[1] // 2
            other = pl.ds(pl.multiple_of((1 - c) * half, 128), half)
            cp.wait_send()
            pltpu.make_async_remote_copy(
                src_ref=dsts[e].at[:, other], dst_ref=dsts[e].at[:, other], send_sem=send_sems.at[e],
                recv_sem=recv_sems.at[e], device_id=sibling, device_id_type=MESH).wait_recv()

    outs = pl.pallas_call(
        body, name=name, out_shape=[jax.ShapeDtypeStruct(g.shape, g.dtype) for g in grads],
        in_specs=[ANY] * ne, out_specs=[ANY] * ne, input_output_aliases={e: e for e in range(ne)},
        scratch_shapes=[pltpu.SemaphoreType.DMA((ne,)), pltpu.SemaphoreType.DMA((ne,))],
    )(*grads)
    return list(outs)


def _pair_sum(part, theirs, cidx, name):
    _, r, n = part.shape
    half = n // 2

    def body(c_ref, p_ref, t_ref, o_ref):
        o_ref[...] = (p_ref[...] + t_ref[...]).astype(BF16)

    gs = pltpu.PrefetchScalarGridSpec(
        num_scalar_prefetch=1, grid=(4,),
        in_specs=[pl.BlockSpec((1, r, half), lambda j, c: (j, 0, c[0])),
                  pl.BlockSpec((1, r, half), lambda j, c: (j, 0, 0))],
        out_specs=pl.BlockSpec((1, r, half), lambda j, c: (j, 0, 0)))
    return pl.pallas_call(body, name=name, grid_spec=gs, out_shape=jax.ShapeDtypeStruct((4, r, half), BF16),
                          compiler_params=_params("arbitrary"))(cidx, part, theirs)


def _chip_sum(sums, landed, jc, name):
    _, r, half = sums.shape

    def body(jc_ref, s_ref, l_ref, o_ref):
        acc = s_ref[0].astype(F32)
        for k in range(3):
            acc = acc + l_ref[k].astype(F32)
        o_ref[...] = acc

    gs = pltpu.PrefetchScalarGridSpec(
        num_scalar_prefetch=1, grid=(1,),
        in_specs=[pl.BlockSpec((1, r, half), lambda i, jc: (jc[0], 0, 0)),
                  pl.BlockSpec((3, r, half), lambda i, jc: (0, 0, 0))],
        out_specs=pl.BlockSpec((r, half), lambda i, jc: (0, jc[1])))
    return pl.pallas_call(body, name=name, grid_spec=gs, out_shape=jax.ShapeDtypeStruct((r, 2 * half), F32),
                          compiler_params=_params("arbitrary"))(jc, sums, landed)


def _rope_tables(pos):
    s = pos.shape[0]
    lane = jnp.arange(128)
    inv = 1.0 / (ROPE_THETA ** ((2 * (lane % 32)).astype(F32) / ROPE))
    keep = (lane < 64).astype(F32)
    sign = jnp.where(lane < 32, -1.0, 1.0).astype(F32) * keep
    consts = jnp.stack([inv.astype(F32), keep, sign])[:, None, :]

    def body(p_ref, k_ref, c_ref, s_ref):
        ang = p_ref[...].astype(F32) * k_ref[0]
        c_ref[...] = jnp.cos(ang) * k_ref[1]
        s_ref[...] = jnp.sin(ang) * k_ref[2]

    tm = min(s, 1024)
    return pl.pallas_call(
        body, name="rope_tables", grid=(s // tm,),
        in_specs=[pl.BlockSpec((tm, 1), lambda i: (i, 0)), pl.BlockSpec((3, 1, 128), lambda i: (0, 0, 0))],
        out_specs=[pl.BlockSpec((tm, 128), lambda i: (i, 0))] * 2,
        out_shape=[jax.ShapeDtypeStruct((s, 128), F32)] * 2, compiler_params=_params("arbitrary"),
    )(pos, consts)


def _modulation(c_all, ada_w, ada_b_shard):
    nl, d, n = ada_w.shape
    tn = 512

    def body(c_ref, w_ref, b_ref, o_ref):
        cv = c_ref[...]
        act = (cv * _sigmoid(cv)).astype(BF16)
        o_ref[...] = _dot(act, w_ref[...].astype(BF16)) + b_ref[...]

    return pl.pallas_call(
        body, name="modulation", grid=(nl, n // tn),
        in_specs=[pl.BlockSpec((8, d), lambda l, j: (0, 0)), pl.BlockSpec((None, d, tn), lambda l, j: (l, 0, j)),
                  pl.BlockSpec((None, 1, tn), lambda l, j: (l, 0, j))],
        out_specs=pl.BlockSpec((None, 8, tn), lambda l, j: (l, 0, j)),
        out_shape=jax.ShapeDtypeStruct((nl, 8, n), F32), compiler_params=_params("arbitrary", "arbitrary"),
    )(c_all, ada_w, ada_b_shard)


def _loss_head(xf, target):
    s, d = xf.shape
    tm = min(s, 512)

    def body(x_ref, t_ref, l_ref, dx_ref):
        err = x_ref[...] - t_ref[...]
        l_ref[...] = 0.5 * jnp.mean(err * err, axis=1, keepdims=True)
        dx_ref[...] = err * (1.0 / d)

    return pl.pallas_call(
        body, name="loss_head", grid=(s // tm,),
        in_specs=[pl.BlockSpec((tm, d), lambda i: (i, 0))] * 2,
        out_specs=[pl.BlockSpec((tm, 1), lambda i: (i, 0)), pl.BlockSpec((tm, d), lambda i: (i, 0))],
        out_shape=[jax.ShapeDtypeStruct((s, 1), F32), jax.ShapeDtypeStruct((s, d), F32)],
        compiler_params=_params("arbitrary"),
    )(xf, target)


def _transpose_bf16(w):
    nl, d, n = w.shape
    td = 256

    def body(w_ref, o_ref):
        o_ref[...] = w_ref[...].T.astype(BF16)

    return pl.pallas_call(
        body, name="w_in_transpose", grid=(nl, d // td),
        in_specs=[pl.BlockSpec((None, td, n), lambda l, i: (l, i, 0))],
        out_specs=pl.BlockSpec((None, n, td), lambda l, i: (l, 0, i)),
        out_shape=jax.ShapeDtypeStruct((nl, n, d), BF16), compiler_params=_params("arbitrary", "arbitrary"),
    )(w)


W_ROWS = 4992
W_PIECES = ((0, 0, 832), (832, 768, 64), (896, 1856, 2048), (2944, 832, 1024), (3968, 3904, 1024))


def _load_w_in(w_hbm, w_vmem, sems):
    cps = [pltpu.make_async_copy(w_hbm.at[pl.ds(src, n)], w_vmem.at[pl.ds(dst, n)], sems.at[i])
           for i, (dst, src, n) in enumerate(W_PIECES)]
    for cp in cps:
        cp.start()
    for cp in cps:
        cp.wait()


def _inproj_fwd(x, g, scale, shift, w_int, layer, name):
    s, d = x.shape
    tm = min(s, 256)

    def body(x_ref, g_ref, sc_ref, sh_ref, w_hbm, hb_ref, za_ref, zkr_ref, zb_ref, w_vmem, sems):
        @pl.when(pl.program_id(0) == 0)
        def _():
            _load_w_in(w_hbm.at[layer], w_vmem, sems)

        xv = x_ref[...]
        rstd = lax.rsqrt(jnp.mean(xv * xv, axis=1, keepdims=True) + EPS)
        h = (xv * rstd) * g_ref[...] * (1.0 + sc_ref[...]) + sh_ref[...]
        hb = h.astype(BF16)
        hb_ref[...] = hb
        za_ref[...] = _dot(hb, w_vmem[0:768], NT)
        zkr_ref[...] = _dot(hb, w_vmem[768:896], NT)
        zb_ref[...] = _dot(hb, w_vmem[896:W_ROWS], NT)

    return pl.pallas_call(
        body, name=name, grid=(s // tm,),
        in_specs=[pl.BlockSpec((tm, d), lambda i: (i, 0)), _row(d), _row(d), _row(d), ANY],
        out_specs=[pl.BlockSpec((tm, d), lambda i: (i, 0)), pl.BlockSpec((tm, 768), lambda i: (i, 0)),
                   pl.BlockSpec((tm, 128), lambda i: (i, 0)), pl.BlockSpec((tm, 4096), lambda i: (i, 0))],
        out_shape=[jax.ShapeDtypeStruct((s, d), BF16), jax.ShapeDtypeStruct((s, 768), F32),
                   jax.ShapeDtypeStruct((s, 128), F32), jax.ShapeDtypeStruct((s, 4096), F32)],
        scratch_shapes=[pltpu.VMEM((W_ROWS, d), BF16), pltpu.SemaphoreType.DMA((len(W_PIECES),))],
        compiler_params=_params("arbitrary"),
    )(x, g, scale, shift, w_int)


def _rope(yv, cos, sin):
    return yv * cos + pltpu.roll(yv, 32, axis=1) * sin


def _mla_prep_fwd(za, zkr, cos, sin, wq, wkv, gql, gkvl, gq2, gk2, layer, name):
    s = za.shape[0]
    tm = min(s, 256)

    def body(za_ref, zkr_ref, cos_ref, sin_ref, wq_ref, wkv_ref, gql_ref, gkvl_ref, gq_ref, gk_ref, q_ref, k_ref, v_ref):
        zq, zkv = za_ref[:, 0:QL], za_ref[:, QL:QL + KVL]
        qn = (zq * lax.rsqrt(jnp.mean(zq * zq, axis=1, keepdims=True) + EPS) * gql_ref[...]).astype(BF16)
        kvn = (zkv * lax.rsqrt(jnp.mean(zkv * zkv, axis=1, keepdims=True) + EPS) * gkvl_ref[...]).astype(BF16)
        kr = zkr_ref[...]
        kr_ss = 0.5 * _sum1(kr * kr)
        cos, sin = cos_ref[...], sin_ref[...]
        gq, gk = gq_ref[...], gk_ref[...]
        for h in range(N_HEADS):
            qr = _dot(qn, wq_ref[h])
            n, yv = qr[:, :NOPE], qr[:, NOPE:]
            rstd = lax.rsqrt((_sum1(n * n) + 0.5 * _sum1(yv * yv)) * (1.0 / QK) + EPS)
            q_ref[h, :, 0:NOPE] = (n * rstd * gq[:, :NOPE]).astype(BF16)
            q_ref[h, :, NOPE:HP] = _rope(yv * rstd * gq[:, NOPE:], cos, sin).astype(BF16)
            col = (h % 2) * 256
            kvr = _dot(kvn, wkv_ref[h // 2, :, col:col + 256])
            kn, vv = kvr[:, :NOPE], kvr[:, NOPE:]
            rstd = lax.rsqrt((_sum1(kn * kn) + kr_ss) * (1.0 / QK) + EPS)
            k_ref[h, :, 0:NOPE] = (kn * rstd * gk[:, :NOPE]).astype(BF16)
            k_ref[h, :, NOPE:HP] = _rope(kr * rstd * gk[:, NOPE:], cos, sin).astype(BF16)
            v_ref[h] = vv.astype(BF16)

    tile = lambda n: pl.BlockSpec((tm, n), lambda i: (i, 0))
    return pl.pallas_call(
        body, name=name, grid=(s // tm,),
        in_specs=[tile(768), tile(128), tile(128), tile(128),
                  pl.BlockSpec((None, N_HEADS, QL, HP), lambda i: (layer, 0, 0, 0)),
                  pl.BlockSpec((None, 4, KVL, 512), lambda i: (layer, 0, 0, 0)),
                  _row(QL), _row(KVL), _row(HP), _row(HP)],
        out_specs=[pl.BlockSpec((N_HEADS, tm, HP), lambda i: (0, i, 0)), pl.BlockSpec((N_HEADS, tm, HP), lambda i: (0, i, 0)),
                   pl.BlockSpec((N_HEADS, tm, VD), lambda i: (0, i, 0))],
        out_shape=[jax.ShapeDtypeStruct((N_HEADS, s, HP), BF16), jax.ShapeDtypeStruct((N_HEADS, s, HP), BF16),
                   jax.ShapeDtypeStruct((N_HEADS, s, VD), BF16)],
        compiler_params=_params("arbitrary"),
    )(za, zkr, cos, sin, wq, wkv, gql, gkvl, gq2, gk2)


SCORE_SCALE = 1.0 / math.sqrt(QK)
MASKED = -1e30


def _flash_fwd(q, k, v, name):
    s = q.shape[1]
    t = min(s, 512)

    def body(q_ref, k_ref, v_ref, o_ref, lse_ref):
        i = pl.program_id(1)
        qb = q_ref[...]
        row = lax.broadcasted_iota(jnp.int32, (t, t), 0)
        col = lax.broadcasted_iota(jnp.int32, (t, t), 1)

        def step(j, carry, diagonal):
            m, l, acc = carry
            at = pl.ds(pl.multiple_of(j * t, t), t)
            sc = _dot(qb, k_ref[at, :], NT) * SCORE_SCALE
            if diagonal:
                sc = jnp.where(col <= row, sc, MASKED)
            m_new = jnp.maximum(m, jnp.max(sc, axis=1, keepdims=True))
            p = jnp.exp(sc - m_new)
            alpha = jnp.exp(m - m_new)
            return m_new, alpha * l + _sum1(p), alpha * acc + _dot(p.astype(BF16), v_ref[at, :])

        init = (jnp.full((t, 1), MASKED, F32), jnp.zeros((t, 1), F32), jnp.zeros((t, VD), F32))
        carry = lax.fori_loop(0, i, lambda j, cr: step(j, cr, False), init)
        m, l, acc = step(i, carry, True)
        o_ref[...] = acc / l
        lse_ref[...] = m + jnp.log(l)

    return pl.pallas_call(
        body, name=name, grid=(N_HEADS, s // t),
        in_specs=[pl.BlockSpec((None, t, HP), lambda h, i: (h, i, 0)), pl.BlockSpec((None, s, HP), lambda h, i: (h, 0, 0)),
                  pl.BlockSpec((None, s, VD), lambda h, i: (h, 0, 0))],
        out_specs=[pl.BlockSpec((t, VD), lambda h, i: (i, h)), pl.BlockSpec((None, t, 1), lambda h, i: (h, i, 0))],
        out_shape=[jax.ShapeDtypeStruct((s, N_HEADS * VD), F32), jax.ShapeDtypeStruct((N_HEADS, s, 1), F32)],
        compiler_params=_params("arbitrary", "arbitrary"),
    )(q, k, v)


CH, RC = 256, 64


def _glu(val, gate, bias):
    c = val.shape[1]
    return (val + bias[:, :c]) * _sigmoid(gate + bias[:, c:])


def _conv_fwd(zb, glu_b, dw, dwb, lng, lnb, wpw, bpw, layer, name):
    s = zb.shape[0]
    dc = dwb.shape[1]
    tm = min(s, 256)
    hb = tm // HALO

    def body(val_ref, gate_ref, valh_ref, gateh_ref, glub_ref, dw_ref, dwb_ref, lng_ref, lnb_ref, wpw_ref, bpw_ref,
             cv_ref, pw_ref, ubuf):
        i = pl.program_id(0)
        bias = glub_ref[...]
        ubuf[HALO:, :] = _glu(val_ref[...], gate_ref[...], bias)
        uh = _glu(valh_ref[...], gateh_ref[...], bias)
        ubuf[0:HALO, :] = jnp.where(i > 0, uh, 0.0)
        for cc in range(0, dc, CH):
            for r0 in range(0, tm, RC):
                acc = jnp.zeros((RC, CH), F32)
                for j in range(CONV_K):
                    acc = acc + ubuf[pl.ds(r0 + HALO - (CONV_K - 1) + j, RC), cc:cc + CH] * dw_ref[j:j + 1, cc:cc + CH]
                cv_ref[r0:r0 + RC, cc:cc + CH] = acc + dwb_ref[:, cc:cc + CH]
        cv = cv_ref[...]
        dv = cv - jnp.mean(cv, axis=1, keepdims=True)
        yl = dv * lax.rsqrt(jnp.mean(dv * dv, axis=1, keepdims=True) + EPS) * lng_ref[...] + lnb_ref[...]
        act = (yl * _sigmoid(yl)).astype(BF16)
        pw_ref[...] = _dot(act, wpw_ref[...]) + bpw_ref[...]

    return pl.pallas_call(
        body, name=name, grid=(s // tm,),
        in_specs=[pl.BlockSpec((tm, dc), lambda i: (i, 0)), pl.BlockSpec((tm, dc), lambda i: (i, 1)),
                  pl.BlockSpec((HALO, dc), lambda i: (jnp.maximum(i * hb - 1, 0), 0)),
                  pl.BlockSpec((HALO, dc), lambda i: (jnp.maximum(i * hb - 1, 0), 1)),
                  _row(2 * dc), pl.BlockSpec((None, HALO, dc), lambda i: (layer, 0, 0)), _row(dc), _row(dc), _row(dc),
                  pl.BlockSpec((None, dc, dc), lambda i: (layer, 0, 0)), _row(dc)],
        out_specs=[pl.BlockSpec((tm, dc), lambda i: (i, 0))] * 2,
        out_shape=[jax.ShapeDtypeStruct((s, dc), F32)] * 2,
        scratch_shapes=[pltpu.VMEM((tm + HALO, dc), F32)],
        compiler_params=_params("arbitrary"),
    )(zb, zb, zb, zb, glu_b, dw, dwb, lng, lnb, wpw, bpw)


def _silu_parts(z):
    sg = _sigmoid(z)
    return z * sg, sg * (1.0 + z * (1.0 - sg))


def _outproj_fwd(x, o, zb, pw, gate, wout, layer, name):
    s, d = x.shape
    dm = o.shape[1]
    tm = min(s, 256)

    def body(x_ref, o_ref, mg_ref, cg_ref, pw_ref, gate_ref, w_ref, xn_ref, y_ref, mix_ref):
        mg, cg = mg_ref[...], cg_ref[...]
        mix_ref[:, 0:dm] = (o_ref[...] * (mg * _sigmoid(mg))).astype(BF16)
        mix_ref[:, dm:] = (pw_ref[...] * (cg * _sigmoid(cg))).astype(BF16)
        yv = _dot(mix_ref[...], w_ref[...])
        y_ref[...] = yv
        xn_ref[...] = x_ref[...] + gate_ref[...] * yv

    tile = lambda n, j=0: pl.BlockSpec((tm, n), lambda i: (i, j))
    return pl.pallas_call(
        body, name=name, grid=(s // tm,),
        in_specs=[tile(d), tile(dm), tile(dm, 2), tile(dm, 3), tile(dm), _row(d),
                  pl.BlockSpec((None, 2 * dm, d), lambda i: (layer, 0, 0))],
        out_specs=[tile(d), tile(d), tile(2 * dm)],
        out_shape=[jax.ShapeDtypeStruct((s, d), F32), jax.ShapeDtypeStruct((s, d), F32), jax.ShapeDtypeStruct((s, 2 * dm), BF16)],
        compiler_params=_params("arbitrary"),
    )(x, o, zb, zb, pw, gate, wout)


def _grad_tn(a, b, name):
    s, n = a.shape
    m = b.shape[1]
    tn, ts = min(n, 1024), min(s, 512)

    def body(a_ref, b_ref, o_ref):
        @pl.when(pl.program_id(1) == 0)
        def _():
            o_ref[...] = jnp.zeros_like(o_ref)

        o_ref[...] += _dot(a_ref[...].astype(BF16), b_ref[...].astype(BF16), TN)

    return pl.pallas_call(
        body, name=name, grid=(n // tn, s // ts),
        in_specs=[pl.BlockSpec((ts, tn), lambda r, t: (t, r)), pl.BlockSpec((ts, m), lambda r, t: (t, 0))],
        out_specs=pl.BlockSpec((tn, m), lambda r, t: (r, 0)),
        out_shape=jax.ShapeDtypeStruct((n, m), F32), compiler_params=_params("arbitrary", "arbitrary"),
    )(a, b)


def _outproj_bwd(dxo, y, gate, wout, o, zb, pw, layer, name):
    s, d = dxo.shape
    dm = o.shape[1]
    tm = min(s, 256)

    def body(dx_ref, y_ref, gate_ref, w_ref, o_ref, mg_ref, cg_ref, pw_ref,
             dgate_ref, dy_ref, do_ref, delta_ref, dzb_ref, dpw_ref):
        @pl.when(pl.program_id(0) == 0)
        def _():
            dgate_ref[...] = jnp.zeros_like(dgate_ref)

        dx = dx_ref[...]
        dgate_ref[...] += _sum0(dx * y_ref[...])
        dyb = (dx * gate_ref[...]).astype(BF16)
        dy_ref[...] = dyb
        dmix = _dot(dyb, w_ref[...], NT)
        da, db = dmix[:, :dm], dmix[:, dm:]
        ov = o_ref[...]
        silu_m, dsilu_m = _silu_parts(mg_ref[...])
        do = da * silu_m
        do_ref[...] = do.astype(BF16)
        prod = do * ov
        for h in range(N_HEADS):
            delta_ref[h] = _sum1(prod[:, h * VD:(h + 1) * VD])
        dzb_ref[:, 0:dm] = da * ov * dsilu_m
        silu_c, dsilu_c = _silu_parts(cg_ref[...])
        dpw_ref[...] = db * silu_c
        dzb_ref[:, dm:] = db * pw_ref[...] * dsilu_c

    tile = lambda n, j=0: pl.BlockSpec((tm, n), lambda i: (i, j))
    return pl.pallas_call(
        body, name=name, grid=(s // tm,),
        in_specs=[tile(d), tile(d), _row(d), pl.BlockSpec((None, 2 * dm, d), lambda i: (layer, 0, 0)),
                  tile(dm), tile(dm, 2), tile(dm, 3), tile(dm)],
        out_specs=[_row(d), tile(d), tile(dm), pl.BlockSpec((N_HEADS, tm, 1), lambda i: (0, i, 0)), tile(2 * dm, 1), tile(dm)],
        out_shape=[jax.ShapeDtypeStruct((1, d), F32), jax.ShapeDtypeStruct((s, d), BF16), jax.ShapeDtypeStruct((s, dm), BF16),
                   jax.ShapeDtypeStruct((N_HEADS, s, 1), F32), jax.ShapeDtypeStruct((s, 4 * dm), F32),
                   jax.ShapeDtypeStruct((s, dm), F32)],
        compiler_params=_params("arbitrary"),
    )(dxo, y, gate, wout, o, zb, zb, pw)


def _flash_bwd(q, k, v, do, lse, delta, name):
    s = q.shape[1]
    t = min(s, 512)
    nq = s // t

    def body(q_ref, k_ref, v_ref, do_ref, lse_ref, delta_ref, dq_ref, dk_ref, dv_ref):
        j = pl.program_id(1)

        @pl.when(j == 0)
        def _():
            dq_ref[...] = jnp.zeros_like(dq_ref)

        kb, vb = k_ref[...], v_ref[...]
        row = lax.broadcasted_iota(jnp.int32, (t, t), 0)
        col = lax.broadcasted_iota(jnp.int32, (t, t), 1)

        def step(i, carry, diagonal):
            dk, dv = carry
            at = pl.ds(pl.multiple_of(i * t, t), t)
            qi, doi = q_ref[at, :], do_ref[at, :]
            p = jnp.exp(_dot(qi, kb, NT) * SCORE_SCALE - lse_ref[at, :])
            if diagonal:
                p = jnp.where(col <= row, p, 0.0)
            dv = dv + _dot(p.astype(BF16), doi, TN)
            dp = _dot(doi, vb, NT)
            ds = (p * (dp - delta_ref[at, :]) * SCORE_SCALE).astype(BF16)
            dq_ref[at, :] += _dot(ds, kb)
            return dk + _dot(ds, qi, TN), dv

        carry = step(j, (jnp.zeros((t, HP), F32), jnp.zeros((t, VD), F32)), True)
        dk, dv = lax.fori_loop(j + 1, nq, lambda i, cr: step(i, cr, False), carry)
        dk_ref[...] = dk
        dv_ref[...] = dv

    whole = lambda n: pl.BlockSpec((None, s, n), lambda h, j: (h, 0, 0))
    blk = lambda n: pl.BlockSpec((None, t, n), lambda h, j: (h, j, 0))
    return pl.pallas_call(
        body, name=name, grid=(N_HEADS, nq),
        in_specs=[whole(HP), blk(HP), blk(VD), pl.BlockSpec((s, VD), lambda h, j: (0, h)), whole(1), whole(1)],
        out_specs=[whole(HP), blk(HP), blk(VD)],
        out_shape=[jax.ShapeDtypeStruct((N_HEADS, s, HP), F32), jax.ShapeDtypeStruct((N_HEADS, s, HP), F32),
                   jax.ShapeDtypeStruct((N_HEADS, s, VD), F32)],
        compiler_params=_params("arbitrary", "arbitrary"),
    )(q, k, v, do, lse, delta)


def _mla_prep_bwd(dq, dk, dv, za, zkr, cos, sin, wq, wkv, gql, gkvl, gq2, gk2, layer, name):
    s = za.shape[0]
    tm = min(s, 256)

    def norm_bwd(n, yv, rstd, gain, d_n_out, d_y_out):
        tn_, ty = n * rstd, yv * rstd
        dgain_n, dgain_y = _sum0(d_n_out * tn_), _sum0(d_y_out * ty)
        dtn, dty = d_n_out * gain[:, :NOPE], d_y_out * gain[:, NOPE:]
        a = (_sum1(dtn * n) + _sum1(dty * yv)) * (rstd * rstd * rstd * (1.0 / QK))
        return rstd * dtn - n * a, rstd * dty - (0.5 * yv) * a, dgain_n, dgain_y

    def rope_bwd(d_out, cos, sin):
        return d_out * cos + pltpu.roll(d_out * sin, 128 - 32, axis=1)

    def latent_bwd(z, gain, dn):
        rstd = lax.rsqrt(jnp.mean(z * z, axis=1, keepdims=True) + EPS)
        zh = z * rstd
        dzh = dn * gain
        return rstd * (dzh - zh * jnp.mean(dzh * zh, axis=1, keepdims=True)), _sum0(dn * zh)

    def body(dq_ref, dk_ref, dv_ref, za_ref, zkr_ref, cos_ref, sin_ref, wq_ref, wkv_ref, gql_ref, gkvl_ref, gq_ref, gk_ref,
             dza_ref, dzkr_ref, gwq_ref, gwkv_ref, dgql_ref, dgkvl_ref, dgq_ref, dgk_ref):
        @pl.when(pl.program_id(0) == 0)
        def _():
            for r in (gwq_ref, gwkv_ref, dgql_ref, dgkvl_ref, dgq_ref, dgk_ref):
                r[...] = jnp.zeros_like(r)

        zq, zkv = za_ref[:, 0:QL], za_ref[:, QL:QL + KVL]
        qn = (zq * lax.rsqrt(jnp.mean(zq * zq, axis=1, keepdims=True) + EPS) * gql_ref[...]).astype(BF16)
        kvn = (zkv * lax.rsqrt(jnp.mean(zkv * zkv, axis=1, keepdims=True) + EPS) * gkvl_ref[...]).astype(BF16)
        kr = zkr_ref[...]
        kr_ss = 0.5 * _sum1(kr * kr)
        cos, sin = cos_ref[...], sin_ref[...]
        gq, gk = gq_ref[...], gk_ref[...]
        dqn = jnp.zeros((tm, QL), F32)
        dkvn = jnp.zeros((tm, KVL), F32)
        dkr = jnp.zeros((tm, 128), F32)
        for h in range(N_HEADS):
            qr = _dot(qn, wq_ref[h])
            n, yv = qr[:, :NOPE], qr[:, NOPE:]
            rstd = lax.rsqrt((_sum1(n * n) + 0.5 * _sum1(yv * yv)) * (1.0 / QK) + EPS)
            dqh = dq_ref[h]
            dn, dy, dg_n, dg_y = norm_bwd(n, yv, rstd, gq, dqh[:, :NOPE], rope_bwd(dqh[:, NOPE:], cos, sin))
            dgq_ref[:, 0:NOPE] += dg_n
            dgq_ref[:, NOPE:] += dg_y
            dqr = jnp.concatenate([dn, dy], axis=1).astype(BF16)
            gwq_ref[h] += _dot(qn, dqr, TN)
            dqn = dqn + _dot(dqr, wq_ref[h], NT)

            col = (h % 2) * 256
            wkv_h = wkv_ref[h // 2, :, col:col + 256]
            kvr = _dot(kvn, wkv_h)
            kn = kvr[:, :NOPE]
            rstd = lax.rsqrt((_sum1(kn * kn) + kr_ss) * (1.0 / QK) + EPS)
            dkh = dk_ref[h]
            dn, dy, dg_n, dg_y = norm_bwd(kn, kr, rstd, gk, dkh[:, :NOPE], rope_bwd(dkh[:, NOPE:], cos, sin))
            dgk_ref[:, 0:NOPE] += dg_n
            dgk_ref[:, NOPE:] += dg_y
            dkr = dkr + dy
            dkvr = jnp.concatenate([dn, dv_ref[h]], axis=1).astype(BF16)
            gwkv_ref[h // 2, :, col:col + 256] += _dot(kvn, dkvr, TN)
            dkvn = dkvn + _dot(dkvr, wkv_h, NT)

        dzq, dgql = latent_bwd(zq, gql_ref[...], dqn)
        dzkv, dgkvl = latent_bwd(zkv, gkvl_ref[...], dkvn)
        dgql_ref[...] += dgql
        dgkvl_ref[...] += dgkvl
        dza_ref[:, 0:QL] = dzq
        dza_ref[:, QL:] = dzkv
        lane = lax.broadcasted_iota(jnp.int32, (tm, 128), 1)
        dzkr_ref[...] = jnp.where(lane < ROPE, dkr + pltpu.roll(dkr, 64, axis=1), 0.0)

    tile = lambda n: pl.BlockSpec((tm, n), lambda i: (i, 0))
    heads = lambda n: pl.BlockSpec((N_HEADS, tm, n), lambda i: (0, i, 0))
    return pl.pallas_call(
        body, name=name, grid=(s // tm,),
        in_specs=[heads(HP), heads(HP), heads(VD), tile(768), tile(128), tile(128), tile(128),
                  pl.BlockSpec((None, N_HEADS, QL, HP), lambda i: (layer, 0, 0, 0)),
                  pl.BlockSpec((None, 4, KVL, 512), lambda i: (layer, 0, 0, 0)),
                  _row(QL), _row(KVL), _row(HP), _row(HP)],
        out_specs=[tile(768), tile(128), pl.BlockSpec((N_HEADS, QL, HP), lambda i: (0, 0, 0)),
                   pl.BlockSpec((4, KVL, 512), lambda i: (0, 0, 0)), _row(QL), _row(KVL), _row(HP), _row(HP)],
        out_shape=[jax.ShapeDtypeStruct((s, 768), F32), jax.ShapeDtypeStruct((s, 128), F32),
                   jax.ShapeDtypeStruct((N_HEADS, QL, HP), F32), jax.ShapeDtypeStruct((4, KVL, 512), F32),
                   jax.ShapeDtypeStruct((1, QL), F32), jax.ShapeDtypeStruct((1, KVL), F32),
                   jax.ShapeDtypeStruct((1, HP), F32), jax.ShapeDtypeStruct((1, HP), F32)],
        compiler_params=_params("arbitrary"),
    )(dq, dk, dv, za, zkr, cos, sin, wq, wkv, gql, gkvl, gq2, gk2)


def _pointwise_bwd(dpw, cv, lng, lnb, wpw, layer, name):
    s, dc = cv.shape
    tm = min(s, 256)

    def body(dpw_ref, cv_ref, lng_ref, lnb_ref, w_ref, dcv_ref, act_ref, dbpw_ref, dlng_ref, dlnb_ref):
        @pl.when(pl.program_id(0) == 0)
        def _():
            for r in (dbpw_ref, dlng_ref, dlnb_ref):
                r[...] = jnp.zeros_like(r)

        cv = cv_ref[...]
        dv = cv - jnp.mean(cv, axis=1, keepdims=True)
        rstd = lax.rsqrt(jnp.mean(dv * dv, axis=1, keepdims=True) + EPS)
        xh = dv * rstd
        yl = xh * lng_ref[...] + lnb_ref[...]
        silu, dsilu = _silu_parts(yl)
        act_ref[...] = silu.astype(BF16)
        dpw = dpw_ref[...]
        dbpw_ref[...] += _sum0(dpw)
        dyl = _dot(dpw.astype(BF16), w_ref[...], NT) * dsilu
        dlng_ref[...] += _sum0(dyl * xh)
        dlnb_ref[...] += _sum0(dyl)
        dxh = dyl * lng_ref[...]
        dcv_ref[...] = rstd * (dxh - jnp.mean(dxh, axis=1, keepdims=True) - xh * jnp.mean(dxh * xh, axis=1, keepdims=True))

    tile = pl.BlockSpec((tm, dc), lambda i: (i, 0))
    return pl.pallas_call(
        body, name=name, grid=(s // tm,),
        in_specs=[tile, tile, _row(dc), _row(dc), pl.BlockSpec((None, dc, dc), lambda i: (layer, 0, 0))],
        out_specs=[tile, tile, _row(dc), _row(dc), _row(dc)],
        out_shape=[jax.ShapeDtypeStruct((s, dc), F32), jax.ShapeDtypeStruct((s, dc), BF16)] + [jax.ShapeDtypeStruct((1, dc), F32)] * 3,
        compiler_params=_params("arbitrary"),
    )(dpw, cv, lng, lnb, wpw)


def _conv_bwd(dcv, zb, dzb, glu_b, dw, layer, name):
    s, dc = dcv.shape
    tm = min(s, 256)
    hb = tm // HALO
    last = s // tm - 1

    def body(dcv_ref, dcvn_ref, val_ref, gate_ref, valh_ref, gateh_ref, glub_ref, dw_ref, _, dzb_ref, gdw_ref, ddwb_ref, dglub_ref,
             ubuf, dbuf, gacc):
        i = pl.program_id(0)

        @pl.when(i == 0)
        def _():
            gacc[...] = jnp.zeros_like(gacc)
            ddwb_ref[...] = jnp.zeros_like(ddwb_ref)
            dglub_ref[...] = jnp.zeros_like(dglub_ref)

        bias = glub_ref[...]
        ubuf[HALO:, :] = _glu(val_ref[...], gate_ref[...], bias)
        ubuf[0:HALO, :] = jnp.where(i > 0, _glu(valh_ref[...], gateh_ref[...], bias), 0.0)
        dcv = dcv_ref[...]
        dbuf[0:tm, :] = dcv
        dbuf[tm:, :] = jnp.where(i < last, dcvn_ref[...], 0.0)
        ddwb_ref[...] += _sum0(dcv)
        for cc in range(0, dc, CH):
            for r0 in range(0, tm, RC):
                du = jnp.zeros((RC, CH), F32)
                dpiece = dbuf[r0:r0 + RC, cc:cc + CH]
                for j in range(CONV_K):
                    du = du + dbuf[pl.ds(r0 + (CONV_K - 1) - j, RC), cc:cc + CH] * dw_ref[j:j + 1, cc:cc + CH]
                    win = ubuf[pl.ds(r0 + HALO - (CONV_K - 1) + j, RC), cc:cc + CH]
                    gacc[j, :, cc:cc + CH] += (dpiece * win).reshape(RC // 8, 8, CH).sum(axis=0)
                a = val_ref[r0:r0 + RC, cc:cc + CH] + bias[:, cc:cc + CH]
                sg = _sigmoid(gate_ref[r0:r0 + RC, cc:cc + CH] + bias[:, dc + cc:dc + cc + CH])
                dzb_ref[r0:r0 + RC, cc:cc + CH] = du * sg
                dzb_ref[r0:r0 + RC, dc + cc:dc + cc + CH] = du * a * sg * (1.0 - sg)
        dglub_ref[...] += _sum0(dzb_ref[...])

        @pl.when(i == last)
        def _():
            total = jnp.sum(gacc[...], axis=1)
            for cc in range(0, dc, CH):
                gdw_ref[cc // CH] = total[:, cc:cc + CH]

    return pl.pallas_call(
        body, name=name, grid=(s // tm,),
        in_specs=[pl.BlockSpec((tm, dc), lambda i: (i, 0)),
                  pl.BlockSpec((HALO, dc), lambda i: (jnp.minimum((i + 1) * hb, s // HALO - 1), 0)),
                  pl.BlockSpec((tm, dc), lambda i: (i, 0)), pl.BlockSpec((tm, dc), lambda i: (i, 1)),
                  pl.BlockSpec((HALO, dc), lambda i: (jnp.maximum(i * hb - 1, 0), 0)),
                  pl.BlockSpec((HALO, dc), lambda i: (jnp.maximum(i * hb - 1, 0), 1)),
                  _row(2 * dc), pl.BlockSpec((None, HALO, dc), lambda i: (layer, 0, 0)), ANY],
        out_specs=[pl.BlockSpec((tm, 2 * dc), lambda i: (i, 0)), pl.BlockSpec((4, HALO, CH), lambda i: (0, 0, 0)),
                   _row(dc), _row(2 * dc)],
        out_shape=[jax.ShapeDtypeStruct(dzb.shape, F32), jax.ShapeDtypeStruct((4, HALO, CH), F32),
                   jax.ShapeDtypeStruct((1, dc), F32), jax.ShapeDtypeStruct((1, 2 * dc), F32)],
        scratch_shapes=[pltpu.VMEM((tm + HALO, dc), F32), pltpu.VMEM((tm + HALO, dc), F32), pltpu.VMEM((HALO, 8, dc), F32)],
        input_output_aliases={8: 0},
        compiler_params=_params("arbitrary"),
    )(dcv, dcv, zb, zb, zb, zb, glu_b, dw, dzb)


def _inproj_bwd(dza, dzkr, dzb, x, dxo, g, scale, shift, w_int, layer, name):
    s, d = x.shape
    tm = min(s, 128)

    def body(dza_ref, dzkr_ref, dzb_ref, x_ref, dxo_ref, g_ref, sc_ref, sh_ref, w_hbm, dx_ref, dsh_ref, dgg_ref, w_vmem, sems):
        @pl.when(pl.program_id(0) == 0)
        def _():
            _load_w_in(w_hbm.at[layer], w_vmem, sems)
            dsh_ref[...] = jnp.zeros_like(dsh_ref)
            dgg_ref[...] = jnp.zeros_like(dgg_ref)

        dh = _dot(dza_ref[...].astype(BF16), w_vmem[0:768])
        dh = dh + _dot(dzkr_ref[...].astype(BF16), w_vmem[768:896])
        dh = dh + _dot(dzb_ref[...].astype(BF16), w_vmem[896:W_ROWS])
        xv = x_ref[...]
        rstd = lax.rsqrt(jnp.mean(xv * xv, axis=1, keepdims=True) + EPS)
        xh = xv * rstd
        dsh_ref[...] += _sum0(dh)
        dgg_ref[...] += _sum0(dh * xh)
        dxh = dh * (g_ref[...] * (1.0 + sc_ref[...]))
        dx_ref[...] = dxo_ref[...] + rstd * (dxh - xh * jnp.mean(dxh * xh, axis=1, keepdims=True))

    tile = lambda n: pl.BlockSpec((tm, n), lambda i: (i, 0))
    return pl.pallas_call(
        body, name=name, grid=(s // tm,),
        in_specs=[tile(768), tile(128), tile(4096), tile(d), tile(d), _row(d), _row(d), _row(d), ANY],
        out_specs=[tile(d), _row(d), _row(d)],
        out_shape=[jax.ShapeDtypeStruct((s, d), F32), jax.ShapeDtypeStruct((1, d), F32), jax.ShapeDtypeStruct((1, d), F32)],
        scratch_shapes=[pltpu.VMEM((W_ROWS, d), BF16), pltpu.SemaphoreType.DMA((len(W_PIECES),))],
        compiler_params=_params("arbitrary"),
    )(dza, dzkr, dzb, x, dxo, g, scale, shift, w_int)


def _assemble_w_in_grad(ga, gkr, gb, name):
    d = ga.shape[1]
    moves = ((0, 0, 0, 768), (1, 0, 768, 64), (2, 2048, 832, 1024), (2, 0, 1856, 2048), (2, 3072, 3904, 1024))

    def body(ga_ref, gkr_ref, gb_ref, o_ref, sems):
        refs = (ga_ref, gkr_ref, gb_ref)
        cps = [pltpu.make_async_copy(refs[src].at[pl.ds(r0, n)], o_ref.at[pl.ds(to, n)], sems.at[i])
               for i, (src, r0, to, n) in enumerate(moves)]
        for cp in cps:
            cp.start()
        for cp in cps:
            cp.wait()

    return pl.pallas_call(
        body, name=name, in_specs=[ANY] * 3, out_specs=ANY, out_shape=jax.ShapeDtypeStruct((4928, d), F32),
        scratch_shapes=[pltpu.SemaphoreType.DMA((len(moves),))],
    )(ga, gkr, gb)


def _adamw(w, g, m, v):
    m = ADAM_B1 * m + (1.0 - ADAM_B1) * g
    v = ADAM_B2 * v + (1.0 - ADAM_B2) * (g * g)
    m_hat = m / (1.0 - ADAM_B1 ** ADAM_STEP)
    v_hat = v / (1.0 - ADAM_B2 ** ADAM_STEP)
    return -ADAM_LR * (m_hat / (jnp.sqrt(v_hat) + ADAM_EPS) + ADAM_WD * w), m, v


def _adam_update(w, g0, g1, m, v, name, transposed=False):
    _, r, c = w.shape
    tr = r
    while tr * c * 4 > (1 << 21) and tr % 16 == 0:
        tr //= 2

    def body(w_ref, g0_ref, g1_ref, m_ref, v_ref, g_ref, d_ref, mo_ref, vo_ref):
        g = jnp.where(pl.program_id(0) == 0, g0_ref[...], g1_ref[...])
        if transposed:
            g = g.T
        g_ref[...] = g
        d_ref[...], mo_ref[...], vo_ref[...] = _adamw(w_ref[...], g, m_ref[...], v_ref[...])

    big = pl.BlockSpec((None, tr, c), lambda l, i: (l, i, 0))
    one = pl.BlockSpec((c, tr), lambda l, i: (0, i)) if transposed else pl.BlockSpec((tr, c), lambda l, i: (i, 0))
    return pl.pallas_call(
        body, name=name, grid=(2, r // tr), in_specs=[big, one, one, big, big], out_specs=[big] * 4,
        out_shape=[jax.ShapeDtypeStruct(w.shape, F32)] * 4, compiler_params=_params("arbitrary", "arbitrary"),
    )(w, g0, g1, m, v)


def _ada_update(c_all, dmod, w, m, v):
    nl, d, n = w.shape
    tr = 256

    def body(c_ref, dm_ref, w_ref, m_ref, v_ref, g_ref, d_ref, mo_ref, vo_ref):
        cv = c_ref[...]
        act = (cv * _sigmoid(cv)).astype(BF16)
        g = _dot(act, dm_ref[...].astype(BF16), TN)
        g_ref[...] = g
        d_ref[...], mo_ref[...], vo_ref[...] = _adamw(w_ref[...], g, m_ref[...], v_ref[...])

    big = pl.BlockSpec((None, tr, n), lambda l, i: (l, i, 0))
    return pl.pallas_call(
        body, name="ada_w_update", grid=(nl, d // tr),
        in_specs=[pl.BlockSpec((8, tr), lambda l, i: (0, i)), pl.BlockSpec((None, 8, n), lambda l, i: (l, 0, 0)), big, big, big],
        out_specs=[big] * 4, out_shape=[jax.ShapeDtypeStruct(w.shape, F32)] * 4,
        compiler_params=_params("arbitrary", "arbitrary"),
    )(c_all, dmod, w, m, v)


def _small_update(gathered, w, m, v):
    r = w.shape[0]

    def body(ga_ref, w_ref, m_ref, v_ref, g_ref, d_ref, mo_ref, vo_ref):
        g = ga_ref[0]
        for dev in range(1, 8):
            g = g + ga_ref[dev]
        g_ref[...] = g
        d_ref[...], mo_ref[...], vo_ref[...] = _adamw(w_ref[...], g, m_ref[...], v_ref[...])

    return pl.pallas_call(body, name="small_update", out_shape=[jax.ShapeDtypeStruct((r, 128), F32)] * 4,
                          compiler_params=_params())(gathered, w, m, v)


SMALL = (("ada_b", 6144), ("norm_g", 2048), ("q_lat_g", 512), ("kv_lat_g", 256), ("q_norm_g", 256), ("k_norm_g", 256),
         ("glu_b", 2048), ("dw_b", 1024), ("conv_ln_g", 1024), ("conv_ln_b", 1024), ("b_pw", 1024))


def _pack_small(vals):
    cols = []
    for name, width in SMALL:
        a = vals[name]
        if a.shape[1] < width:
            a = jnp.pad(a, ((0, 0), (0, width - a.shape[1])))
        cols.append(a)
    return jnp.concatenate(cols, axis=1).reshape(-1, 128)


def _unpack_small(packed, shapes):
    flat = packed.reshape(2, -1)
    out, at = {}, 0
    for name, width in SMALL:
        out[name] = flat[:, at:at + shapes[name]]
        at += width
    return out


def _dup_gain(g):
    return jnp.concatenate([g, g[NOPE:]])[None, :]


def _undup(g):
    return jnp.concatenate([g[..., :NOPE], g[..., NOPE:NOPE + ROPE] + g[..., NOPE + ROPE:]], axis=-1)


def kernel(x, c, positions, ada_w, ada_b, norm_g, w_in, q_lat_g, w_q_up, kv_lat_g, w_kv_up, q_norm_g, k_norm_g, glu_b, dw_w, dw_b, conv_ln_g, conv_ln_b, w_pw, b_pw, w_out, loss_target, m_ada_w, m_ada_b, m_norm_g, m_w_in, m_q_lat_g, m_w_q_up, m_kv_lat_g, m_w_kv_up, m_q_norm_g, m_k_norm_g, m_glu_b, m_dw_w, m_dw_b, m_conv_ln_g, m_conv_ln_b, m_w_pw, m_b_pw, m_w_out, v_ada_w, v_ada_b, v_norm_g, v_w_in, v_q_lat_g, v_w_q_up, v_kv_lat_g, v_w_kv_up, v_q_norm_g, v_k_norm_g, v_glu_b, v_dw_w, v_dw_b, v_conv_ln_g, v_conv_ln_b, v_w_pw, v_b_pw, v_w_out):
    nl = 2
    s, d = x.shape[1], x.shape[2]
    xi, yi, ci = lax.axis_index("x"), lax.axis_index("y"), lax.axis_index("c")
    shard = 2 * xi + yi
    me = 4 * xi + 2 * yi + ci
    cidx = jnp.reshape(ci, (1,)).astype(jnp.int32)
    jc = jnp.stack([shard, ci]).astype(jnp.int32)
    x0 = x.reshape(s, d)
    target = loss_target.reshape(s, d)

    c_all = _allgather8(c.reshape(8, d // 8), "gather_c").reshape(8, d)
    n_ada = ada_w.shape[2]
    ada_b_shard = lax.dynamic_slice_in_dim(ada_b, shard * n_ada, n_ada, axis=1)[:, None, :]
    mod_shard = _modulation(c_all, ada_w, ada_b_shard)
    mod_all = _allgather8(mod_shard.reshape(nl * 8, n_ada), "gather_mod")
    mod_rows = lax.dynamic_index_in_dim(mod_all.reshape(4, 2, nl, 8, n_ada)[:, 0], me, axis=2, keepdims=False)
    mod_me = jnp.transpose(mod_rows, (1, 0, 2)).reshape(nl, 3, 1, d)

    w_in_t = _transpose_bf16(w_in)
    wq = w_q_up.reshape(nl, QL, 2, QK)
    wq = jnp.concatenate([wq, wq[..., NOPE:]], axis=-1)
    wq = jnp.transpose(wq, (0, 2, 1, 3)).reshape(nl, 2 * QL, HP).astype(BF16)
    dw_pad = jnp.pad(dw_w, ((0, 0), (0, HALO - CONV_K), (0, 0)))
    w_in_t, wq, wkv, dw, wpw, wout = _allgather_shards(
        [w_in_t, wq, w_kv_up.astype(BF16), dw_pad, w_pw.astype(BF16), w_out.astype(BF16)], "gather_weights")
    w_in_t = w_in_t.reshape(nl, 4 * w_in_t.shape[2], d)
    wq = wq.reshape(nl, N_HEADS, QL, HP)
    dw = jnp.transpose(dw, (0, 2, 1, 3)).reshape(nl, HALO, 4 * dw.shape[3])
    wpw = wpw.reshape(nl, 4 * wpw.shape[2], wpw.shape[3])
    wout = wout.reshape(nl, 4 * wout.shape[2], d)

    cos, sin = _rope_tables(positions.reshape(s, 1))
    row = lambda a, l: a[l][None, :]

    saved = []
    xl = x0
    for l in range(nl):
        shift, scale, gate = mod_me[l, 0], mod_me[l, 1], mod_me[l, 2]
        hb, za, zkr, zb = _inproj_fwd(xl, row(norm_g, l), scale, shift, w_in_t, l, f"inproj_fwd{l}")
        gains = (row(q_lat_g, l), row(kv_lat_g, l), _dup_gain(q_norm_g[l]), _dup_gain(k_norm_g[l]))
        q, k, v = _mla_prep_fwd(za, zkr, cos, sin, wq, wkv, *gains, l, f"mla_prep_fwd{l}")
        o, lse = _flash_fwd(q, k, v, f"flash_fwd{l}")
        cv, pw = _conv_fwd(zb, row(glu_b, l), dw, row(dw_b, l), row(conv_ln_g, l), row(conv_ln_b, l), wpw, row(b_pw, l), l, f"conv_fwd{l}")
        xn, yv, mixb = _outproj_fwd(xl, o, zb, pw, gate, wout, l, f"outproj_fwd{l}")
        saved.append(dict(x=xl, hb=hb, za=za, zkr=zkr, zb=zb, q=q, k=k, v=v, o=o, lse=lse, cv=cv, pw=pw, y=yv, mixb=mixb, gains=gains))
        xl = xn

    tok_loss, dx = _loss_head(xl, target)
    loss = lax.psum(jnp.sum(tok_loss), ("x", "y", "c"))

    big = [None] * nl
    small = [None] * nl
    for l in reversed(range(nl)):
        sv = saved[l]
        shift, scale, gate = mod_me[l, 0], mod_me[l, 1], mod_me[l, 2]
        dgate, dyb, do, delta, dzb, dpw = _outproj_bwd(dx, sv["y"], gate, wout, sv["o"], sv["zb"], sv["pw"], l, f"outproj_bwd{l}")
        g_out = _grad_tn(sv["mixb"], dyb, f"grad_w_out{l}")
        dq, dk, dv = _flash_bwd(sv["q"], sv["k"], sv["v"], do, sv["lse"], delta, f"flash_bwd{l}")
        dza, dzkr, g_q, g_kv, dgql, dgkvl, dgq, dgk = _mla_prep_bwd(
            dq, dk, dv, sv["za"], sv["zkr"], cos, sin, wq, wkv, *sv["gains"], l, f"mla_prep_bwd{l}")
        dcv, act, dbpw, dlng, dlnb = _pointwise_bwd(dpw, sv["cv"], row(conv_ln_g, l), row(conv_ln_b, l), wpw, l, f"pointwise_bwd{l}")
        g_pw = _grad_tn(act, dpw, f"grad_w_pw{l}")
        dzb, g_dw, ddwb, dglub = _conv_bwd(dcv, sv["zb"], dzb, row(glu_b, l), dw, l, f"conv_bwd{l}")
        dx, dshift, dgg = _inproj_bwd(dza, dzkr, dzb, sv["x"], dx, row(norm_g, l), scale, shift, w_in_t, l, f"inproj_bwd{l}")
        g_in = _assemble_w_in_grad(_grad_tn(dza, sv["hb"], f"grad_w_in_a{l}"), _grad_tn(dzkr, sv["hb"], f"grad_w_in_kr{l}"),
                                   _grad_tn(dzb, sv["hb"], f"grad_w_in_b{l}"), f"grad_w_in{l}")
        big[l] = [g_in, g_q.reshape(N_HEADS * QL, HP), g_kv.reshape(4 * KVL, 512), g_dw.reshape(4 * HALO, CH), g_pw, g_out]
        small[l] = dict(ada_b=jnp.concatenate([dshift, dgg * row(norm_g, l), dgate], axis=1), norm_g=dgg * (1.0 + scale),
                        q_lat_g=dgql, kv_lat_g=dgkvl, q_norm_g=_undup(dgq), k_norm_g=_undup(dgk), glu_b=dglub, dw_b=ddwb,
                        conv_ln_g=dlng, conv_ln_b=dlnb, b_pw=dbpw)
    grad_x = dx.reshape(x.shape)

    names = [n for n, _ in SMALL]
    mine = _pack_small({n: jnp.concatenate([small[0][n], small[1][n]], axis=0) for n in names})
    gathered = _allgather8(mine, "gather_small")
    weights = dict(ada_b=ada_b, norm_g=norm_g, q_lat_g=q_lat_g, kv_lat_g=kv_lat_g, q_norm_g=q_norm_g, k_norm_g=k_norm_g,
                   glu_b=glu_b, dw_b=dw_b, conv_ln_g=conv_ln_g, conv_ln_b=conv_ln_b, b_pw=b_pw)
    m_small = dict(ada_b=m_ada_b, norm_g=m_norm_g, q_lat_g=m_q_lat_g, kv_lat_g=m_kv_lat_g, q_norm_g=m_q_norm_g, k_norm_g=m_k_norm_g,
                   glu_b=m_glu_b, dw_b=m_dw_b, conv_ln_g=m_conv_ln_g, conv_ln_b=m_conv_ln_b, b_pw=m_b_pw)
    v_small = dict(ada_b=v_ada_b, norm_g=v_norm_g, q_lat_g=v_q_lat_g, kv_lat_g=v_kv_lat_g, q_norm_g=v_q_norm_g, k_norm_g=v_k_norm_g,
                   glu_b=v_glu_b, dw_b=v_dw_b, conv_ln_g=v_conv_ln_g, conv_ln_b=v_conv_ln_b, b_pw=v_b_pw)
    widths = {n: weights[n].shape[1] for n in names}
    v_packed = _pack_small({n: jnp.pad(v_small[n], ((0, 0), (0, dict(SMALL)[n] - widths[n])), constant_values=1.0) for n in names})
    small_out = [_unpack_small(a, widths) for a in _small_update(gathered, _pack_small(weights), _pack_small(m_small), v_packed)]

    ada_rows = gathered.reshape(8, nl, -1)[:, :, :3 * d]
    dmod = lax.dynamic_slice_in_dim(jnp.transpose(ada_rows, (1, 0, 2)), shard * n_ada, n_ada, axis=2)
    ada_out = _ada_update(c_all, dmod, ada_w, m_ada_w, v_ada_w)

    parts = [g.reshape(4, g.shape[0] // 4, g.shape[1]) for l in range(nl) for g in big[l]]
    theirs = _pair_exchange(parts, "pair_exchange")
    chip = [_pair_sum(p, t, cidx, f"pair_sum{e}") for e, (p, t) in enumerate(zip(parts, theirs))]
    landed = _chip_scatter(chip, "chip_scatter")
    halves = [_chip_sum(sm, ld, jc, f"chip_sum{e}") for e, (sm, ld) in enumerate(zip(chip, landed))]
    full = _pair_complete(halves, "pair_complete")
    per_layer = [full[l * 6:(l + 1) * 6] for l in range(nl)]

    def natural_q(g):
        return jnp.transpose(_undup(g.reshape(2, QL, HP)), (1, 0, 2)).reshape(QL, 2 * QK)

    grads = [[per_layer[l][0], natural_q(per_layer[l][1]), per_layer[l][2], per_layer[l][3][:CONV_K], per_layer[l][4], per_layer[l][5]]
             for l in range(nl)]
    sharded = (("w_in", w_in, m_w_in, v_w_in, True), ("w_q_up", w_q_up, m_w_q_up, v_w_q_up, False),
               ("w_kv_up", w_kv_up, m_w_kv_up, v_w_kv_up, False), ("dw_w", dw_w, m_dw_w, v_dw_w, False),
               ("w_pw", w_pw, m_w_pw, v_w_pw, False), ("w_out", w_out, m_w_out, v_w_out, False))
    big_out = {name: _adam_update(w, grads[0][e], grads[1][e], m, v, f"adam_{name}", transposed=tr)
               for e, (name, w, m, v, tr) in enumerate(sharded)}

    order = ["ada_w", "ada_b", "norm_g", "w_in", "q_lat_g", "w_q_up", "kv_lat_g", "w_kv_up", "q_norm_g", "k_norm_g", "glu_b",
             "dw_w", "dw_b", "conv_ln_g", "conv_ln_b", "w_pw", "b_pw", "w_out"]

    def leaf(kind, name):
        if name == "ada_w":
            return ada_out[kind]
        if name in big_out:
            return big_out[name][kind]
        return small_out[kind][name]

    return (loss, grad_x, *[leaf(kind, name) for kind in range(4) for name in order])
```

```python
import functools
import math

import jax
import jax.numpy as jnp
from jax import lax
from jax.experimental import pallas as pl
from jax.experimental.pallas import tpu as pltpu

F32, BF16 = jnp.float32, jnp.bfloat16
MESH = pl.DeviceIdType.MESH
ANY = pl.BlockSpec(memory_space=pl.ANY)

N_HEADS, NOPE, ROPE, VD = 8, 128, 64, 128
QK = NOPE + ROPE
QL, KVL = 512, 256
HP = 256
CONV_K, HALO = 31, 32
ROPE_THETA = 10000.0
EPS = 1e-6
ADAM_LR, ADAM_B1, ADAM_B2, ADAM_EPS, ADAM_WD, ADAM_STEP = 0.001, 0.9, 0.999, 1e-08, 0.01, 10
V7X_VMEM_LIMIT = 56 * 1024 * 1024

NT = (((1,), (1,)), ((), ()))
TN = (((0,), (0,)), ((), ()))
NN = (((1,), (0,)), ((), ()))


def _dot(a, b, dims=NN):
    return lax.dot_general(a, b, dims, preferred_element_type=F32)


def _params(*sem):
    return pltpu.CompilerParams(dimension_semantics=sem or None, vmem_limit_bytes=V7X_VMEM_LIMIT)


def _sigmoid(x):
    return 1.0 / (1.0 + jnp.exp(-x))


def _sum0(x):
    return jnp.sum(x, axis=0, keepdims=True)


def _sum1(x):
    return jnp.sum(x, axis=1, keepdims=True)


def _row(n):
    return pl.BlockSpec((1, n), lambda *_: (0, 0))


def _place():
    x, y, c = lax.axis_index("x"), lax.axis_index("y"), lax.axis_index("c")
    chips = [(1 - x, y), (x, 1 - y), (1 - x, 1 - y)]
    return x, y, c, chips


def _allgather8(v, name):
    r, n = v.shape

    def body(v_ref, out_ref, send_sems, recv_sems, local_sem):
        x, y, c, chips = _place()
        me, sibling = (x, y, c), (x, y, 1 - c)

        def slot(px, py, pc):
            return out_ref.at[4 * px + 2 * py + pc]

        def copy(k, block, to, src=None):
            return pltpu.make_async_remote_copy(
                src_ref=slot(*block) if src is None else src, dst_ref=slot(*block),
                send_sem=send_sems.at[k], recv_sem=recv_sems.at[k], device_id=to, device_id_type=MESH)

        mine = pltpu.make_async_copy(v_ref, slot(*me), local_sem)
        mine.start()
        first = [copy(0, me, sibling, src=v_ref)]
        first += [copy(1 + j, me, (*chip, c), src=v_ref) for j, chip in enumerate(chips)]
        for cp in first:
            cp.start()
        passed = [copy(4 + j, (*chip, c), sibling) for j, chip in enumerate(chips)]
        for j, chip in enumerate(chips):
            copy(1 + j, (*chip, c), me).wait_recv()
            passed[j].start()
        copy(0, sibling, me).wait_recv()
        for j, chip in enumerate(chips):
            copy(4 + j, (*chip, 1 - c), me).wait_recv()
        for cp in first + passed:
            cp.wait_send()
        mine.wait()

    return pl.pallas_call(
        body, name=name, out_shape=jax.ShapeDtypeStruct((8, r, n), v.dtype),
        in_specs=[pl.BlockSpec(memory_space=pltpu.VMEM)], out_specs=pl.BlockSpec(memory_space=pltpu.VMEM),
        scratch_shapes=[pltpu.SemaphoreType.DMA((7,)), pltpu.SemaphoreType.DMA((7,)), pltpu.SemaphoreType.DMA],
    )(v)


def _allgather_shards(shards, name):
    ne = len(shards)
    per = 7

    def body(*refs):
        srcs, dsts = refs[:ne], refs[ne:2 * ne]
        send_sems, recv_sems = refs[2 * ne:]
        x, y, c, chips = _place()
        sibling = (x, y, 1 - c)
        jme = 2 * x + y

        def copy(e, k, src, dst, to):
            return pltpu.make_async_remote_copy(src_ref=src, dst_ref=dst, send_sem=send_sems.at[per * e + k],
                                                recv_sem=recv_sems.at[per * e + k], device_id=to, device_id_type=MESH)

        started = []
        for e in range(ne):
            half = srcs[e].shape[2] // 2
            own = pl.ds(pl.multiple_of(c * half, 128), half)
            for k, chip in enumerate(chips):
                started.append(copy(e, k, srcs[e].at[:, :, own], dsts[e].at[:, jme, :, own], (*chip, c)))
                started[-1].start()
            started.append(copy(e, 6, srcs[e], dsts[e].at[:, jme], sibling))
            started[-1].start()
        for e in range(ne):
            half = srcs[e].shape[2] // 2
            own = pl.ds(pl.multiple_of(c * half, 128), half)
            for k, (px, py) in enumerate(chips):
                landed = dsts[e].at[:, 2 * px + py, :, own]
                copy(e, k, landed, landed, (px, py, c)).wait_recv()
                started.append(copy(e, 3 + k, landed, landed, sibling))
                started[-1].start()
        for e in range(ne):
            half = srcs[e].shape[2] // 2
            other = pl.ds(pl.multiple_of((1 - c) * half, 128), half)
            for k, (px, py) in enumerate(chips):
                theirs = dsts[e].at[:, 2 * px + py, :, other]
                copy(e, 3 + k, theirs, theirs, sibling).wait_recv()
            copy(e, 6, srcs[e], dsts[e].at[:, jme], sibling).wait_recv()
        for cp in started:
            cp.wait_send()

    outs = pl.pallas_call(
        body, name=name,
        out_shape=[jax.ShapeDtypeStruct((s.shape[0], 4) + s.shape[1:], s.dtype) for s in shards],
        in_specs=[ANY] * ne, out_specs=[ANY] * ne,
        scratch_shapes=[pltpu.SemaphoreType.DMA((per * ne,)), pltpu.SemaphoreType.DMA((per * ne,))],
    )(*shards)
    return list(outs)


def _pair_exchange(parts, name):
    ne = len(parts)

    def body(*refs):
        srcs, dsts = refs[:ne], refs[ne:2 * ne]
        send_sems, recv_sems = refs[2 * ne:]
        x, y, c, _ = _place()
        sibling = (x, y, 1 - c)
        cps = []
        for e in range(ne):
            half = srcs[e].shape[2] // 2
            theirs = pl.ds(pl.multiple_of((1 - c) * half, 128), half)
            cp = pltpu.make_async_remote_copy(
                src_ref=srcs[e].at[:, :, theirs], dst_ref=dsts[e], send_sem=send_sems.at[e],
                recv_sem=recv_sems.at[e], device_id=sibling, device_id_type=MESH)
            cp.start()
            cps.append(cp)
        for cp in cps:
            cp.wait()

    outs = pl.pallas_call(
        body, name=name,
        out_shape=[jax.ShapeDtypeStruct(p.shape[:2] + (p.shape[2] // 2,), p.dtype) for p in parts],
        in_specs=[ANY] * ne, out_specs=[ANY] * ne,
        scratch_shapes=[pltpu.SemaphoreType.DMA((ne,)), pltpu.SemaphoreType.DMA((ne,))],
    )(*parts)
    return list(outs)


def _chip_scatter(sums, name):
    ne = len(sums)

    def body(*refs):
        srcs, dsts = refs[:ne], refs[ne:2 * ne]
        send_sems, recv_sems = refs[2 * ne:]
        x, y, c, chips = _place()
        cps = []
        for e in range(ne):
            for k, (px, py) in enumerate(chips):
                cp = pltpu.make_async_remote_copy(
                    src_ref=srcs[e].at[2 * px + py], dst_ref=dsts[e].at[k], send_sem=send_sems.at[3 * e + k],
                    recv_sem=recv_sems.at[3 * e + k], device_id=(px, py, c), device_id_type=MESH)
                cp.start()
                cps.append(cp)
        for cp in cps:
            cp.wait()

    outs = pl.pallas_call(
        body, name=name,
        out_shape=[jax.ShapeDtypeStruct((3,) + s.shape[1:], s.dtype) for s in sums],
        in_specs=[ANY] * ne, out_specs=[ANY] * ne,
        scratch_shapes=[pltpu.SemaphoreType.DMA((3 * ne,)), pltpu.SemaphoreType.DMA((3 * ne,))],
    )(*sums)
    return list(outs)


def _pair_complete(grads, name):
    ne = len(grads)

    def body(*refs):
        srcs, dsts = refs[:ne], refs[ne:2 * ne]
        send_sems, recv_sems = refs[2 * ne:]
        x, y, c, _ = _place()
        sibling = (x, y, 1 - c)
        cps = []
        for e in range(ne):
            half = srcs[e].shape[1] // 2
            own = pl.ds(pl.multiple_of(c * half, 128), half)
            cp = pltpu.make_async_remote_copy(
                src_ref=dsts[e].at[:, own], dst_ref=dsts[e].at[:, own], send_sem=send_sems.at[e],
                recv_sem=recv_sems.at[e], device_id=sibling, device_id_type=MESH)
            cp.start()
            cps.append(cp)
        for e, cp in enumerate(cps):
            half = srcs[e].shape[1] // 2
            other = pl.ds(pl.multiple_of((1 - c) * half, 128), half)
            cp.wait_send()
            pltpu.make_async_remote_copy(
                src_ref=dsts[e].at[:, other], dst_ref=dsts[e].at[:, other], send_sem=send_sems.at[e],
                recv_sem=recv_sems.at[e], device_id=sibling, device_id_type=MESH).wait_recv()

    outs = pl.pallas_call(
        body, name=name, out_shape=[jax.ShapeDtypeStruct(g.shape, g.dtype) for g in grads],
        in_specs=[ANY] * ne, out_specs=[ANY] * ne, input_output_aliases={e: e for e in range(ne)},
        scratch_shapes=[pltpu.SemaphoreType.DMA((ne,)), pltpu.SemaphoreType.DMA((ne,))],
    )(*grads)
    return list(outs)


def _pair_sum(part, theirs, cidx, name):
    _, r, n = part.shape
    half = n // 2

    def body(c_ref, p_ref, t_ref, o_ref):
        o_ref[...] = (p_ref[...] + t_ref[...]).astype(BF16)

    gs = pltpu.PrefetchScalarGridSpec(
        num_scalar_prefetch=1, grid=(4,),
        in_specs=[pl.BlockSpec((1, r, half), lambda j, c: (j, 0, c[0])),
                  pl.BlockSpec((1, r, half), lambda j, c: (j, 0, 0))],
        out_specs=pl.BlockSpec((1, r, half), lambda j, c: (j, 0, 0)))
    return pl.pallas_call(body, name=name, grid_spec=gs, out_shape=jax.ShapeDtypeStruct((4, r, half), BF16),
                          compiler_params=_params("arbitrary"))(cidx, part, theirs)


def _chip_sum(sums, landed, jc, name):
    _, r, half = sums.shape

    def body(jc_ref, s_ref, l_ref, o_ref):
        acc = s_ref[0].astype(F32)
        for k in range(3):
            acc = acc + l_ref[k].astype(F32)
        o_ref[...] = acc

    gs = pltpu.PrefetchScalarGridSpec(
        num_scalar_prefetch=1, grid=(1,),
        in_specs=[pl.BlockSpec((1, r, half), lambda i, jc: (jc[0], 0, 0)),
                  pl.BlockSpec((3, r, half), lambda i, jc: (0, 0, 0))],
        out_specs=pl.BlockSpec((r, half), lambda i, jc: (0, jc[1])))
    return pl.pallas_call(body, name=name, grid_spec=gs, out_shape=jax.ShapeDtypeStruct((r, 2 * half), F32),
                          compiler_params=_params("arbitrary"))(jc, sums, landed)


def _rope_tables(pos):
    s = pos.shape[0]
    lane = jnp.arange(128)
    inv = 1.0 / (ROPE_THETA ** ((2 * (lane % 32)).astype(F32) / ROPE))
    keep = (lane < 64).astype(F32)
    sign = jnp.where(lane < 32, -1.0, 1.0).astype(F32) * keep
    consts = jnp.stack([inv.astype(F32), keep, sign])[:, None, :]

    def body(p_ref, k_ref, c_ref, s_ref):
        ang = p_ref[...].astype(F32) * k_ref[0]
        c_ref[...] = jnp.cos(ang) * k_ref[1]
        s_ref[...] = jnp.sin(ang) * k_ref[2]

    tm = min(s, 1024)
    return pl.pallas_call(
        body, name="rope_tables", grid=(s // tm,),
        in_specs=[pl.BlockSpec((tm, 1), lambda i: (i, 0)), pl.BlockSpec((3, 1, 128), lambda i: (0, 0, 0))],
        out_specs=[pl.BlockSpec((tm, 128), lambda i: (i, 0))] * 2,
        out_shape=[jax.ShapeDtypeStruct((s, 128), F32)] * 2, compiler_params=_params("arbitrary"),
    )(pos, consts)


def _modulation(c_all, ada_w, ada_b_shard):
    nl, d, n = ada_w.shape
    tn = 512

    def body(c_ref, w_ref, b_ref, o_ref):
        cv = c_ref[...]
        act = (cv * _sigmoid(cv)).astype(BF16)
        o_ref[...] = _dot(act, w_ref[...].astype(BF16)) + b_ref[...]

    return pl.pallas_call(
        body, name="modulation", grid=(nl, n // tn),
        in_specs=[pl.BlockSpec((8, d), lambda l, j: (0, 0)), pl.BlockSpec((None, d, tn), lambda l, j: (l, 0, j)),
                  pl.BlockSpec((None, 1, tn), lambda l, j: (l, 0, j))],
        out_specs=pl.BlockSpec((None, 8, tn), lambda l, j: (l, 0, j)),
        out_shape=jax.ShapeDtypeStruct((nl, 8, n), F32), compiler_params=_params("arbitrary", "arbitrary"),
    )(c_all, ada_w, ada_b_shard)


def _loss_head(xf, target):
    s, d = xf.shape
    tm = min(s, 512)

    def body(x_ref, t_ref, l_ref, dx_ref):
        err = x_ref[...] - t_ref[...]
        l_ref[...] = 0.5 * jnp.mean(err * err, axis=1, keepdims=True)
        dx_ref[...] = err * (1.0 / d)

    return pl.pallas_call(
        body, name="loss_head", grid=(s // tm,),
        in_specs=[pl.BlockSpec((tm, d), lambda i: (i, 0))] * 2,
        out_specs=[pl.BlockSpec((tm, 1), lambda i: (i, 0)), pl.BlockSpec((tm, d), lambda i: (i, 0))],
        out_shape=[jax.ShapeDtypeStruct((s, 1), F32), jax.ShapeDtypeStruct((s, d), F32)],
        compiler_params=_params("arbitrary"),
    )(xf, target)


W_ROWS = 4992
W_PIECES = ((0, 0, 832), (832, 768, 64), (896, 1856, 2048), (2944, 832, 1024), (3968, 3904, 1024))


def _load_w_in(w_hbm, w_vmem, sems):
    cps = [pltpu.make_async_copy(w_hbm.at[pl.ds(src, n)], w_vmem.at[pl.ds(dst, n)], sems.at[i])
           for i, (dst, src, n) in enumerate(W_PIECES)]
    for cp in cps:
        cp.start()
    for cp in cps:
        cp.wait()


def _inproj_fwd(x, g, scale, shift, w_int, layer, name):
    s, d = x.shape
    tm = min(s, 256)

    def body(x_ref, g_ref, sc_ref, sh_ref, w_hbm, hb_ref, za_ref, zkr_ref, zb_ref, w_vmem, sems):
        @pl.when(pl.program_id(0) == 0)
        def _():
            _load_w_in(w_hbm.at[layer], w_vmem, sems)

        xv = x_ref[...]
        rstd = lax.rsqrt(jnp.mean(xv * xv, axis=1, keepdims=True) + EPS)
        h = (xv * rstd) * g_ref[...] * (1.0 + sc_ref[...]) + sh_ref[...]
        hb = h.astype(BF16)
        hb_ref[...] = hb
        za_ref[...] = _dot(hb, w_vmem[0:768], NT)
        zkr_ref[...] = _dot(hb, w_vmem[768:896], NT)
        zb_ref[...] = _dot(hb, w_vmem[896:W_ROWS], NT)

    return pl.pallas_call(
        body, name=name, grid=(s // tm,),
        in_specs=[pl.BlockSpec((tm, d), lambda i: (i, 0)), _row(d), _row(d), _row(d), ANY],
        out_specs=[pl.BlockSpec((tm, d), lambda i: (i, 0)), pl.BlockSpec((tm, 768), lambda i: (i, 0)),
                   pl.BlockSpec((tm, 128), lambda i: (i, 0)), pl.BlockSpec((tm, 4096), lambda i: (i, 0))],
        out_shape=[jax.ShapeDtypeStruct((s, d), BF16), jax.ShapeDtypeStruct((s, 768), F32),
                   jax.ShapeDtypeStruct((s, 128), F32), jax.ShapeDtypeStruct((s, 4096), F32)],
        scratch_shapes=[pltpu.VMEM((W_ROWS, d), BF16), pltpu.SemaphoreType.DMA((len(W_PIECES),))],
        compiler_params=_params("arbitrary"),
    )(x, g, scale, shift, w_int)


def _rope(yv, cos, sin):
    return yv * cos + pltpu.roll(yv, 32, axis=1) * sin


def _mla_prep_fwd(za, zkr, cos, sin, wq, wkv, gql, gkvl, gq2, gk2, layer, name):
    s = za.shape[0]
    tm = min(s, 256)

    def body(za_ref, zkr_ref, cos_ref, sin_ref, wq_ref, wkv_ref, gql_ref, gkvl_ref, gq_ref, gk_ref, q_ref, k_ref, v_ref):
        zq, zkv = za_ref[:, 0:QL], za_ref[:, QL:QL + KVL]
        qn = (zq * lax.rsqrt(jnp.mean(zq * zq, axis=1, keepdims=True) + EPS) * gql_ref[...]).astype(BF16)
        kvn = (zkv * lax.rsqrt(jnp.mean(zkv * zkv, axis=1, keepdims=True) + EPS) * gkvl_ref[...]).astype(BF16)
        kr = zkr_ref[...]
        kr_ss = 0.5 * _sum1(kr * kr)
        cos, sin = cos_ref[...], sin_ref[...]
        gq, gk = gq_ref[...], gk_ref[...]
        for h in range(N_HEADS):
            qr = _dot(qn, wq_ref[h])
            n, yv = qr[:, :NOPE], qr[:, NOPE:]
            rstd = lax.rsqrt((_sum1(n * n) + 0.5 * _sum1(yv * yv)) * (1.0 / QK) + EPS)
            q_ref[h, :, 0:NOPE] = (n * rstd * gq[:, :NOPE]).astype(BF16)
            q_ref[h, :, NOPE:HP] = _rope(yv * rstd * gq[:, NOPE:], cos, sin).astype(BF16)
            col = (h % 2) * 256
            kvr = _dot(kvn, wkv_ref[h // 2, :, col:col + 256])
            kn, vv = kvr[:, :NOPE], kvr[:, NOPE:]
            rstd = lax.rsqrt((_sum1(kn * kn) + kr_ss) * (1.0 / QK) + EPS)
            k_ref[h, :, 0:NOPE] = (kn * rstd * gk[:, :NOPE]).astype(BF16)
            k_ref[h, :, NOPE:HP] = _rope(kr * rstd * gk[:, NOPE:], cos, sin).astype(BF16)
            v_ref[h] = vv.astype(BF16)

    tile = lambda n: pl.BlockSpec((tm, n), lambda i: (i, 0))
    return pl.pallas_call(
        body, name=name, grid=(s // tm,),
        in_specs=[tile(768), tile(128), tile(128), tile(128),
                  pl.BlockSpec((None, N_HEADS, QL, HP), lambda i: (layer, 0, 0, 0)),
                  pl.BlockSpec((None, 4, KVL, 512), lambda i: (layer, 0, 0, 0)),
                  _row(QL), _row(KVL), _row(HP), _row(HP)],
        out_specs=[pl.BlockSpec((N_HEADS, tm, HP), lambda i: (0, i, 0)), pl.BlockSpec((N_HEADS, tm, HP), lambda i: (0, i, 0)),
                   pl.BlockSpec((N_HEADS, tm, VD), lambda i: (0, i, 0))],
        out_shape=[jax.ShapeDtypeStruct((N_HEADS, s, HP), BF16), jax.ShapeDtypeStruct((N_HEADS, s, HP), BF16),
                   jax.ShapeDtypeStruct((N_HEADS, s, VD), BF16)],
        compiler_params=_params("arbitrary"),
    )(za, zkr, cos, sin, wq, wkv, gql, gkvl, gq2, gk2)


SCORE_SCALE = 1.0 / math.sqrt(QK)
SCORE_LOG2 = SCORE_SCALE * math.log2(math.e)
MASKED = -1e30


def _flash_fwd(q, k, v, name):
    s = q.shape[1]
    t = min(s, 512)

    def body(q_ref, k_ref, v_ref, o_ref, lse_ref):
        i = pl.program_id(1)
        qb = q_ref[...]
        row = lax.broadcasted_iota(jnp.int32, (t, t), 0)
        col = lax.broadcasted_iota(jnp.int32, (t, t), 1)

        def step(j, carry, diagonal):
            m, l, acc = carry
            at = pl.ds(pl.multiple_of(j * t, t), t)
            sc = _dot(qb, k_ref[at, :], NT) * SCORE_LOG2
            if diagonal:
                sc = jnp.where(col <= row, sc, MASKED)
            m_new = jnp.maximum(m, jnp.max(sc, axis=1, keepdims=True))
            p = jnp.exp2(sc - m_new)
            alpha = jnp.exp2(m - m_new)
            return m_new, alpha * l + _sum1(p), alpha * acc + _dot(p.astype(BF16), v_ref[at, :])

        init = (jnp.full((t, 1), MASKED, F32), jnp.zeros((t, 1), F32), jnp.zeros((t, VD), F32))
        carry = lax.fori_loop(0, i, lambda j, cr: step(j, cr, False), init)
        m, l, acc = step(i, carry, True)
        o_ref[...] = acc / l
        lse_ref[...] = m + jnp.log2(l)

    return pl.pallas_call(
        body, name=name, grid=(N_HEADS, s // t),
        in_specs=[pl.BlockSpec((None, t, HP), lambda h, i: (h, i, 0)), pl.BlockSpec((None, s, HP), lambda h, i: (h, 0, 0)),
                  pl.BlockSpec((None, s, VD), lambda h, i: (h, 0, 0))],
        out_specs=[pl.BlockSpec((t, VD), lambda h, i: (i, h)), pl.BlockSpec((None, t, 1), lambda h, i: (h, i, 0))],
        out_shape=[jax.ShapeDtypeStruct((s, N_HEADS * VD), F32), jax.ShapeDtypeStruct((N_HEADS, s, 1), F32)],
        compiler_params=_params("arbitrary", "arbitrary"),
    )(q, k, v)


CH, RC = 256, 64


def _glu(val, gate, bias):
    c = val.shape[1]
    return (val + bias[:, :c]) * _sigmoid(gate + bias[:, c:])


def _conv_fwd(zb, glu_b, dw, dwb, lng, lnb, wpw, bpw, layer, name):
    s = zb.shape[0]
    dc = dwb.shape[1]
    tm = min(s, 256)
    hb = tm // HALO

    def body(val_ref, gate_ref, valh_ref, gateh_ref, glub_ref, dw_ref, dwb_ref, lng_ref, lnb_ref, wpw_ref, bpw_ref,
             cv_ref, pw_ref, ubuf):
        i = pl.program_id(0)
        bias = glub_ref[...]
        ubuf[HALO:, :] = _glu(val_ref[...], gate_ref[...], bias)
        uh = _glu(valh_ref[...], gateh_ref[...], bias)
        ubuf[0:HALO, :] = jnp.where(i > 0, uh, 0.0)
        for cc in range(0, dc, CH):
            for r0 in range(0, tm, RC):
                acc = jnp.zeros((RC, CH), F32)
                for j in range(CONV_K):
                    acc = acc + ubuf[pl.ds(r0 + HALO - (CONV_K - 1) + j, RC), cc:cc + CH] * dw_ref[j:j + 1, cc:cc + CH]
                cv_ref[r0:r0 + RC, cc:cc + CH] = acc + dwb_ref[:, cc:cc + CH]
        cv = cv_ref[...]
        dv = cv - jnp.mean(cv, axis=1, keepdims=True)
        yl = dv * lax.rsqrt(jnp.mean(dv * dv, axis=1, keepdims=True) + EPS) * lng_ref[...] + lnb_ref[...]
        act = (yl * _sigmoid(yl)).astype(BF16)
        pw_ref[...] = _dot(act, wpw_ref[...]) + bpw_ref[...]

    return pl.pallas_call(
        body, name=name, grid=(s // tm,),
        in_specs=[pl.BlockSpec((tm, dc), lambda i: (i, 0)), pl.BlockSpec((tm, dc), lambda i: (i, 1)),
                  pl.BlockSpec((HALO, dc), lambda i: (jnp.maximum(i * hb - 1, 0), 0)),
                  pl.BlockSpec((HALO, dc), lambda i: (jnp.maximum(i * hb - 1, 0), 1)),
                  _row(2 * dc), pl.BlockSpec((None, HALO, dc), lambda i: (layer, 0, 0)), _row(dc), _row(dc), _row(dc),
                  pl.BlockSpec((None, dc, dc), lambda i: (layer, 0, 0)), _row(dc)],
        out_specs=[pl.BlockSpec((tm, dc), lambda i: (i, 0))] * 2,
        out_shape=[jax.ShapeDtypeStruct((s, dc), F32)] * 2,
        scratch_shapes=[pltpu.VMEM((tm + HALO, dc), F32)],
        compiler_params=_params("arbitrary"),
    )(zb, zb, zb, zb, glu_b, dw, dwb, lng, lnb, wpw, bpw)


def _silu_parts(z):
    sg = _sigmoid(z)
    return z * sg, sg * (1.0 + z * (1.0 - sg))


def _outproj_fwd(x, o, zb, pw, gate, wout, layer, name):
    s, d = x.shape
    dm = o.shape[1]
    tm = min(s, 256)

    def body(x_ref, o_ref, mg_ref, cg_ref, pw_ref, gate_ref, w_ref, xn_ref, y_ref, mix_ref):
        mg, cg = mg_ref[...], cg_ref[...]
        mix_ref[:, 0:dm] = (o_ref[...] * (mg * _sigmoid(mg))).astype(BF16)
        mix_ref[:, dm:] = (pw_ref[...] * (cg * _sigmoid(cg))).astype(BF16)
        yv = _dot(mix_ref[...], w_ref[...])
        y_ref[...] = yv
        xn_ref[...] = x_ref[...] + gate_ref[...] * yv

    tile = lambda n, j=0: pl.BlockSpec((tm, n), lambda i: (i, j))
    return pl.pallas_call(
        body, name=name, grid=(s // tm,),
        in_specs=[tile(d), tile(dm), tile(dm, 2), tile(dm, 3), tile(dm), _row(d),
                  pl.BlockSpec((None, 2 * dm, d), lambda i: (layer, 0, 0))],
        out_specs=[tile(d), tile(d), tile(2 * dm)],
        out_shape=[jax.ShapeDtypeStruct((s, d), F32), jax.ShapeDtypeStruct((s, d), F32), jax.ShapeDtypeStruct((s, 2 * dm), BF16)],
        compiler_params=_params("arbitrary"),
    )(x, o, zb, zb, pw, gate, wout)


def _grad_tn(a, b, name):
    s, n = a.shape
    m = b.shape[1]
    tn, ts = min(n, 1024), min(s, 512)

    def body(a_ref, b_ref, o_ref):
        @pl.when(pl.program_id(1) == 0)
        def _():
            o_ref[...] = jnp.zeros_like(o_ref)

        o_ref[...] += _dot(a_ref[...].astype(BF16), b_ref[...].astype(BF16), TN)

    return pl.pallas_call(
        body, name=name, grid=(n // tn, s // ts),
        in_specs=[pl.BlockSpec((ts, tn), lambda r, t: (t, r)), pl.BlockSpec((ts, m), lambda r, t: (t, 0))],
        out_specs=pl.BlockSpec((tn, m), lambda r, t: (r, 0)),
        out_shape=jax.ShapeDtypeStruct((n, m), F32), compiler_params=_params("arbitrary", "arbitrary"),
    )(a, b)


def _outproj_bwd(dxo, y, gate, wout, o, zb, pw, layer, name):
    s, d = dxo.shape
    dm = o.shape[1]
    tm = min(s, 256)

    def body(dx_ref, y_ref, gate_ref, w_ref, o_ref, mg_ref, cg_ref, pw_ref,
             dgate_ref, dy_ref, do_ref, delta_ref, dzb_ref, dpw_ref):
        @pl.when(pl.program_id(0) == 0)
        def _():
            dgate_ref[...] = jnp.zeros_like(dgate_ref)

        dx = dx_ref[...]
        dgate_ref[...] += _sum0(dx * y_ref[...])
        dyb = (dx * gate_ref[...]).astype(BF16)
        dy_ref[...] = dyb
        dmix = _dot(dyb, w_ref[...], NT)
        da, db = dmix[:, :dm], dmix[:, dm:]
        ov = o_ref[...]
        silu_m, dsilu_m = _silu_parts(mg_ref[...])
        do = da * silu_m
        do_ref[...] = do.astype(BF16)
        prod = do * ov
        for h in range(N_HEADS):
            delta_ref[h] = _sum1(prod[:, h * VD:(h + 1) * VD])
        dzb_ref[:, 0:dm] = da * ov * dsilu_m
        silu_c, dsilu_c = _silu_parts(cg_ref[...])
        dpw_ref[...] = db * silu_c
        dzb_ref[:, dm:] = db * pw_ref[...] * dsilu_c

    tile = lambda n, j=0: pl.BlockSpec((tm, n), lambda i: (i, j))
    return pl.pallas_call(
        body, name=name, grid=(s // tm,),
        in_specs=[tile(d), tile(d), _row(d), pl.BlockSpec((None, 2 * dm, d), lambda i: (layer, 0, 0)),
                  tile(dm), tile(dm, 2), tile(dm, 3), tile(dm)],
        out_specs=[_row(d), tile(d), tile(dm), pl.BlockSpec((N_HEADS, tm, 1), lambda i: (0, i, 0)), tile(2 * dm, 1), tile(dm)],
        out_shape=[jax.ShapeDtypeStruct((1, d), F32), jax.ShapeDtypeStruct((s, d), BF16), jax.ShapeDtypeStruct((s, dm), BF16),
                   jax.ShapeDtypeStruct((N_HEADS, s, 1), F32), jax.ShapeDtypeStruct((s, 4 * dm), F32),
                   jax.ShapeDtypeStruct((s, dm), F32)],
        compiler_params=_params("arbitrary"),
    )(dxo, y, gate, wout, o, zb, zb, pw)


def _flash_bwd(q, k, v, do, lse, delta, name):
    s = q.shape[1]
    t = min(s, 512)
    nq = s // t

    def body(q_ref, k_ref, v_ref, do_ref, lse_ref, delta_ref, dq_ref, dk_ref, dv_ref):
        j = pl.program_id(1)

        @pl.when(j == 0)
        def _():
            dq_ref[...] = jnp.zeros_like(dq_ref)

        kb, vb = k_ref[...], v_ref[...]
        row = lax.broadcasted_iota(jnp.int32, (t, t), 0)
        col = lax.broadcasted_iota(jnp.int32, (t, t), 1)

        def step(i, carry, diagonal):
            dk, dv = carry
            at = pl.ds(pl.multiple_of(i * t, t), t)
            qi, doi = q_ref[at, :], do_ref[at, :]
            p = jnp.exp2(_dot(qi, kb, NT) * SCORE_LOG2 - lse_ref[at, :])
            if diagonal:
                p = jnp.where(col <= row, p, 0.0)
            dv = dv + _dot(p.astype(BF16), doi, TN)
            dp = _dot(doi, vb, NT)
            ds = (p * (dp - delta_ref[at, :]) * SCORE_SCALE).astype(BF16)
            dq_ref[at, :] += _dot(ds, kb)
            return dk + _dot(ds, qi, TN), dv

        carry = step(j, (jnp.zeros((t, HP), F32), jnp.zeros((t, VD), F32)), True)
        dk, dv = lax.fori_loop(j + 1, nq, lambda i, cr: step(i, cr, False), carry)
        dk_ref[...] = dk
        dv_ref[...] = dv

    whole = lambda n: pl.BlockSpec((None, s, n), lambda h, j: (h, 0, 0))
    blk = lambda n: pl.BlockSpec((None, t, n), lambda h, j: (h, j, 0))
    return pl.pallas_call(
        body, name=name, grid=(N_HEADS, nq),
        in_specs=[whole(HP), blk(HP), blk(VD), pl.BlockSpec((s, VD), lambda h, j: (0, h)), whole(1), whole(1)],
        out_specs=[whole(HP), blk(HP), blk(VD)],
        out_shape=[jax.ShapeDtypeStruct((N_HEADS, s, HP), F32), jax.ShapeDtypeStruct((N_HEADS, s, HP), F32),
                   jax.ShapeDtypeStruct((N_HEADS, s, VD), F32)],
        compiler_params=_params("arbitrary", "arbitrary"),
    )(q, k, v, do, lse, delta)


def _mla_prep_bwd(dq, dk, dv, za, zkr, cos, sin, wq, wkv, gql, gkvl, gq2, gk2, layer, name):
    s = za.shape[0]
    tm = min(s, 256)

    def norm_bwd(n, yv, rstd, gain, d_n_out, d_y_out):
        tn_, ty = n * rstd, yv * rstd
        dgain_n, dgain_y = _sum0(d_n_out * tn_), _sum0(d_y_out * ty)
        dtn, dty = d_n_out * gain[:, :NOPE], d_y_out * gain[:, NOPE:]
        a = (_sum1(dtn * n) + _sum1(dty * yv)) * (rstd * rstd * rstd * (1.0 / QK))
        return rstd * dtn - n * a, rstd * dty - (0.5 * yv) * a, dgain_n, dgain_y

    def rope_bwd(d_out, cos, sin):
        return d_out * cos + pltpu.roll(d_out * sin, 128 - 32, axis=1)

    def latent_bwd(z, gain, dn):
        rstd = lax.rsqrt(jnp.mean(z * z, axis=1, keepdims=True) + EPS)
        zh = z * rstd
        dzh = dn * gain
        return rstd * (dzh - zh * jnp.mean(dzh * zh, axis=1, keepdims=True)), _sum0(dn * zh)

    def body(dq_ref, dk_ref, dv_ref, za_ref, zkr_ref, cos_ref, sin_ref, wq_ref, wkv_ref, gql_ref, gkvl_ref, gq_ref, gk_ref,
             dza_ref, dzkr_ref, gwq_ref, gwkv_ref, dgql_ref, dgkvl_ref, dgq_ref, dgk_ref):
        @pl.when(pl.program_id(0) == 0)
        def _():
            for r in (gwq_ref, gwkv_ref, dgql_ref, dgkvl_ref, dgq_ref, dgk_ref):
                r[...] = jnp.zeros_like(r)

        zq, zkv = za_ref[:, 0:QL], za_ref[:, QL:QL + KVL]
        qn = (zq * lax.rsqrt(jnp.mean(zq * zq, axis=1, keepdims=True) + EPS) * gql_ref[...]).astype(BF16)
        kvn = (zkv * lax.rsqrt(jnp.mean(zkv * zkv, axis=1, keepdims=True) + EPS) * gkvl_ref[...]).astype(BF16)
        kr = zkr_ref[...]
        kr_ss = 0.5 * _sum1(kr * kr)
        cos, sin = cos_ref[...], sin_ref[...]
        gq, gk = gq_ref[...], gk_ref[...]
        dqn = jnp.zeros((tm, QL), F32)
        dkvn = jnp.zeros((tm, KVL), F32)
        dkr = jnp.zeros((tm, 128), F32)
        for h in range(N_HEADS):
            qr = _dot(qn, wq_ref[h])
            n, yv = qr[:, :NOPE], qr[:, NOPE:]
            rstd = lax.rsqrt((_sum1(n * n) + 0.5 * _sum1(yv * yv)) * (1.0 / QK) + EPS)
            dqh = dq_ref[h]
            dn, dy, dg_n, dg_y = norm_bwd(n, yv, rstd, gq, dqh[:, :NOPE], rope_bwd(dqh[:, NOPE:], cos, sin))
            dgq_ref[:, 0:NOPE] += dg_n
            dgq_ref[:, NOPE:] += dg_y
            dqr = jnp.concatenate([dn, dy], axis=1).astype(BF16)
            gwq_ref[h] += _dot(qn, dqr, TN)
            dqn = dqn + _dot(dqr, wq_ref[h], NT)

            col = (h % 2) * 256
            wkv_h = wkv_ref[h // 2, :, col:col + 256]
            kvr = _dot(kvn, wkv_h)
            kn = kvr[:, :NOPE]
            rstd = lax.rsqrt((_sum1(kn * kn) + kr_ss) * (1.0 / QK) + EPS)
            dkh = dk_ref[h]
            dn, dy, dg_n, dg_y = norm_bwd(kn, kr, rstd, gk, dkh[:, :NOPE], rope_bwd(dkh[:, NOPE:], cos, sin))
            dgk_ref[:, 0:NOPE] += dg_n
            dgk_ref[:, NOPE:] += dg_y
            dkr = dkr + dy
            dkvr = jnp.concatenate([dn, dv_ref[h]], axis=1).astype(BF16)
            gwkv_ref[h // 2, :, col:col + 256] += _dot(kvn, dkvr, TN)
            dkvn = dkvn + _dot(dkvr, wkv_h, NT)

        dzq, dgql = latent_bwd(zq, gql_ref[...], dqn)
        dzkv, dgkvl = latent_bwd(zkv, gkvl_ref[...], dkvn)
        dgql_ref[...] += dgql
        dgkvl_ref[...] += dgkvl
        dza_ref[:, 0:QL] = dzq
        dza_ref[:, QL:] = dzkv
        lane = lax.broadcasted_iota(jnp.int32, (tm, 128), 1)
        dzkr_ref[...] = jnp.where(lane < ROPE, dkr + pltpu.roll(dkr, 64, axis=1), 0.0)

    tile = lambda n: pl.BlockSpec((tm, n), lambda i: (i, 0))
    heads = lambda n: pl.BlockSpec((N_HEADS, tm, n), lambda i: (0, i, 0))
    return pl.pallas_call(
        body, name=name, grid=(s // tm,),
        in_specs=[heads(HP), heads(HP), heads(VD), tile(768), tile(128), tile(128), tile(128),
                  pl.BlockSpec((None, N_HEADS, QL, HP), lambda i: (layer, 0, 0, 0)),
                  pl.BlockSpec((None, 4, KVL, 512), lambda i: (layer, 0, 0, 0)),
                  _row(QL), _row(KVL), _row(HP), _row(HP)],
        out_specs=[tile(768), tile(128), pl.BlockSpec((N_HEADS, QL, HP), lambda i: (0, 0, 0)),
                   pl.BlockSpec((4, KVL, 512), lambda i: (0, 0, 0)), _row(QL), _row(KVL), _row(HP), _row(HP)],
        out_shape=[jax.ShapeDtypeStruct((s, 768), F32), jax.ShapeDtypeStruct((s, 128), F32),
                   jax.ShapeDtypeStruct((N_HEADS, QL, HP), F32), jax.ShapeDtypeStruct((4, KVL, 512), F32),
                   jax.ShapeDtypeStruct((1, QL), F32), jax.ShapeDtypeStruct((1, KVL), F32),
                   jax.ShapeDtypeStruct((1, HP), F32), jax.ShapeDtypeStruct((1, HP), F32)],
        compiler_params=_params("arbitrary"),
    )(dq, dk, dv, za, zkr, cos, sin, wq, wkv, gql, gkvl, gq2, gk2)


def _pointwise_bwd(dpw, cv, lng, lnb, wpw, layer, name):
    s, dc = cv.shape
    tm = min(s, 256)

    def body(dpw_ref, cv_ref, lng_ref, lnb_ref, w_ref, dcv_ref, act_ref, dbpw_ref, dlng_ref, dlnb_ref):
        @pl.when(pl.program_id(0) == 0)
        def _():
            for r in (dbpw_ref, dlng_ref, dlnb_ref):
                r[...] = jnp.zeros_like(r)

        cv = cv_ref[...]
        dv = cv - jnp.mean(cv, axis=1, keepdims=True)
        rstd = lax.rsqrt(jnp.mean(dv * dv, axis=1, keepdims=True) + EPS)
        xh = dv * rstd
        yl = xh * lng_ref[...] + lnb_ref[...]
        silu, dsilu = _silu_parts(yl)
        act_ref[...] = silu.astype(BF16)
        dpw = dpw_ref[...]
        dbpw_ref[...] += _sum0(dpw)
        dyl = _dot(dpw.astype(BF16), w_ref[...], NT) * dsilu
        dlng_ref[...] += _sum0(dyl * xh)
        dlnb_ref[...] += _sum0(dyl)
        dxh = dyl * lng_ref[...]
        dcv_ref[...] = rstd * (dxh - jnp.mean(dxh, axis=1, keepdims=True) - xh * jnp.mean(dxh * xh, axis=1, keepdims=True))

    tile = pl.BlockSpec((tm, dc), lambda i: (i, 0))
    return pl.pallas_call(
        body, name=name, grid=(s // tm,),
        in_specs=[tile, tile, _row(dc), _row(dc), pl.BlockSpec((None, dc, dc), lambda i: (layer, 0, 0))],
        out_specs=[tile, tile, _row(dc), _row(dc), _row(dc)],
        out_shape=[jax.ShapeDtypeStruct((s, dc), F32), jax.ShapeDtypeStruct((s, dc), BF16)] + [jax.ShapeDtypeStruct((1, dc), F32)] * 3,
        compiler_params=_params("arbitrary"),
    )(dpw, cv, lng, lnb, wpw)


def _conv_bwd(dcv, zb, dzb, glu_b, dw, layer, name):
    s, dc = dcv.shape
    tm = min(s, 256)
    hb = tm // HALO
    last = s // tm - 1

    def body(dcv_ref, dcvn_ref, val_ref, gate_ref, valh_ref, gateh_ref, glub_ref, dw_ref, _, dzb_ref, gdw_ref, ddwb_ref, dglub_ref,
             ubuf, dbuf, gacc):
        i = pl.program_id(0)

        @pl.when(i == 0)
        def _():
            gacc[...] = jnp.zeros_like(gacc)
            ddwb_ref[...] = jnp.zeros_like(ddwb_ref)
            dglub_ref[...] = jnp.zeros_like(dglub_ref)

        bias = glub_ref[...]
        ubuf[HALO:, :] = _glu(val_ref[...], gate_ref[...], bias)
        ubuf[0:HALO, :] = jnp.where(i > 0, _glu(valh_ref[...], gateh_ref[...], bias), 0.0)
        dcv = dcv_ref[...]
        dbuf[0:tm, :] = dcv
        dbuf[tm:, :] = jnp.where(i < last, dcvn_ref[...], 0.0)
        ddwb_ref[...] += _sum0(dcv)
        for cc in range(0, dc, CH):
            for r0 in range(0, tm, RC):
                du = jnp.zeros((RC, CH), F32)
                dpiece = dbuf[r0:r0 + RC, cc:cc + CH]
                for j in range(CONV_K):
                    du = du + dbuf[pl.ds(r0 + (CONV_K - 1) - j, RC), cc:cc + CH] * dw_ref[j:j + 1, cc:cc + CH]
                    win = ubuf[pl.ds(r0 + HALO - (CONV_K - 1) + j, RC), cc:cc + CH]
                    gacc[j, :, cc:cc + CH] += (dpiece * win).reshape(RC // 8, 8, CH).sum(axis=0)
                a = val_ref[r0:r0 + RC, cc:cc + CH] + bias[:, cc:cc + CH]
                sg = _sigmoid(gate_ref[r0:r0 + RC, cc:cc + CH] + bias[:, dc + cc:dc + cc + CH])
                dzb_ref[r0:r0 + RC, cc:cc + CH] = du * sg
                dzb_ref[r0:r0 + RC, dc + cc:dc + cc + CH] = du * a * sg * (1.0 - sg)
        dglub_ref[...] += _sum0(dzb_ref[...])

        @pl.when(i == last)
        def _():
            total = jnp.sum(gacc[...], axis=1)
            for cc in range(0, dc, CH):
                gdw_ref[cc // CH] = total[:, cc:cc + CH]

    return pl.pallas_call(
        body, name=name, grid=(s // tm,),
        in_specs=[pl.BlockSpec((tm, dc), lambda i: (i, 0)),
                  pl.BlockSpec((HALO, dc), lambda i: (jnp.minimum((i + 1) * hb, s // HALO - 1), 0)),
                  pl.BlockSpec((tm, dc), lambda i: (i, 0)), pl.BlockSpec((tm, dc), lambda i: (i, 1)),
                  pl.BlockSpec((HALO, dc), lambda i: (jnp.maximum(i * hb - 1, 0), 0)),
                  pl.BlockSpec((HALO, dc), lambda i: (jnp.maximum(i * hb - 1, 0), 1)),
                  _row(2 * dc), pl.BlockSpec((None, HALO, dc), lambda i: (layer, 0, 0)), ANY],
        out_specs=[pl.BlockSpec((tm, 2 * dc), lambda i: (i, 0)), pl.BlockSpec((4, HALO, CH), lambda i: (0, 0, 0)),
                   _row(dc), _row(2 * dc)],
        out_shape=[jax.ShapeDtypeStruct(dzb.shape, F32), jax.ShapeDtypeStruct((4, HALO, CH), F32),
                   jax.ShapeDtypeStruct((1, dc), F32), jax.ShapeDtypeStruct((1, 2 * dc), F32)],
        scratch_shapes=[pltpu.VMEM((tm + HALO, dc), F32), pltpu.VMEM((tm + HALO, dc), F32), pltpu.VMEM((HALO, 8, dc), F32)],
        input_output_aliases={8: 0},
        compiler_params=_params("arbitrary"),
    )(dcv, dcv, zb, zb, zb, zb, glu_b, dw, dzb)


def _inproj_bwd(dza, dzkr, dzb, x, dxo, g, scale, shift, w_int, layer, name):
    s, d = x.shape
    tm = min(s, 128)

    def body(dza_ref, dzkr_ref, dzb_ref, x_ref, dxo_ref, g_ref, sc_ref, sh_ref, w_hbm, dx_ref, dsh_ref, dgg_ref, w_vmem, sems):
        @pl.when(pl.program_id(0) == 0)
        def _():
            _load_w_in(w_hbm.at[layer], w_vmem, sems)
            dsh_ref[...] = jnp.zeros_like(dsh_ref)
            dgg_ref[...] = jnp.zeros_like(dgg_ref)

        dh = _dot(dza_ref[...].astype(BF16), w_vmem[0:768])
        dh = dh + _dot(dzkr_ref[...].astype(BF16), w_vmem[768:896])
        dh = dh + _dot(dzb_ref[...].astype(BF16), w_vmem[896:W_ROWS])
        xv = x_ref[...]
        rstd = lax.rsqrt(jnp.mean(xv * xv, axis=1, keepdims=True) + EPS)
        xh = xv * rstd
        dsh_ref[...] += _sum0(dh)
        dgg_ref[...] += _sum0(dh * xh)
        dxh = dh * (g_ref[...] * (1.0 + sc_ref[...]))
        dx_ref[...] = dxo_ref[...] + rstd * (dxh - xh * jnp.mean(dxh * xh, axis=1, keepdims=True))

    tile = lambda n: pl.BlockSpec((tm, n), lambda i: (i, 0))
    return pl.pallas_call(
        body, name=name, grid=(s // tm,),
        in_specs=[tile(768), tile(128), tile(4096), tile(d), tile(d), _row(d), _row(d), _row(d), ANY],
        out_specs=[tile(d), _row(d), _row(d)],
        out_shape=[jax.ShapeDtypeStruct((s, d), F32), jax.ShapeDtypeStruct((1, d), F32), jax.ShapeDtypeStruct((1, d), F32)],
        scratch_shapes=[pltpu.VMEM((W_ROWS, d), BF16), pltpu.SemaphoreType.DMA((len(W_PIECES),))],
        compiler_params=_params("arbitrary"),
    )(dza, dzkr, dzb, x, dxo, g, scale, shift, w_int)


def _grad_w_in(dza, dzkr, dzb, hb, name):
    s, d = hb.shape
    ts = min(s, 512)
    nt = s // ts
    tiles = ((0, 768), (768, 64), (1856, 1024), (2880, 1024), (832, 1024), (3904, 1024))

    def body(a_ref, kr_ref, b_ref, h_ref, o_hbm, acc, sem):
        r, t = pl.program_id(0), pl.program_id(1)

        @pl.when(t == 0)
        def _():
            acc[...] = jnp.zeros_like(acc)

        hv = h_ref[...]

        @pl.when(r == 0)
        def _():
            acc[0:768, :] += _dot(a_ref[...].astype(BF16), hv, TN)

        @pl.when(r == 1)
        def _():
            acc[0:128, :] += _dot(kr_ref[...].astype(BF16), hv, TN)

        @pl.when(r >= 2)
        def _():
            acc[...] += _dot(b_ref[...].astype(BF16), hv, TN)

        for tile, (row0, rows) in enumerate(tiles):
            @pl.when((t == nt - 1) & (r == tile))
            def _():
                cp = pltpu.make_async_copy(acc.at[pl.ds(0, rows)], o_hbm.at[pl.ds(row0, rows)], sem)
                cp.start()
                cp.wait()

    return pl.pallas_call(
        body, name=name, grid=(len(tiles), nt),
        in_specs=[pl.BlockSpec((ts, 768), lambda r, t: (jnp.where(r == 0, t, nt - 1), 0)),
                  pl.BlockSpec((ts, 128), lambda r, t: (jnp.where(r == 1, t, jnp.where(r == 0, 0, nt - 1)), 0)),
                  pl.BlockSpec((ts, 1024), lambda r, t: (jnp.where(r >= 2, t, 0), jnp.maximum(r - 2, 0))),
                  pl.BlockSpec((ts, d), lambda r, t: (t, 0))],
        out_specs=ANY, out_shape=jax.ShapeDtypeStruct((4928, d), F32),
        scratch_shapes=[pltpu.VMEM((1024, d), F32), pltpu.SemaphoreType.DMA],
        compiler_params=_params("arbitrary", "arbitrary"),
    )(dza, dzkr, dzb, hb)


def _adamw(w, g, m, v):
    m = ADAM_B1 * m + (1.0 - ADAM_B1) * g
    v = ADAM_B2 * v + (1.0 - ADAM_B2) * (g * g)
    m_hat = m / (1.0 - ADAM_B1 ** ADAM_STEP)
    v_hat = v / (1.0 - ADAM_B2 ** ADAM_STEP)
    return -ADAM_LR * (m_hat / (jnp.sqrt(v_hat) + ADAM_EPS) + ADAM_WD * w), m, v


def _adam_update(w, g0, g1, m, v, name):
    _, r, c = w.shape
    fits = [t for t in range(8, r + 1, 8) if r % t == 0 and t * c * 4 <= (1 << 21)]
    tr = max(fits) if fits else r

    def body(w_ref, g0_ref, g1_ref, m_ref, v_ref, g_ref, d_ref, mo_ref, vo_ref):
        g = jnp.where(pl.program_id(0) == 0, g0_ref[...], g1_ref[...])
        g_ref[...] = g
        d_ref[...], mo_ref[...], vo_ref[...] = _adamw(w_ref[...], g, m_ref[...], v_ref[...])

    big = pl.BlockSpec((None, tr, c), lambda l, i: (l, i, 0))
    one = pl.BlockSpec((tr, c), lambda l, i: (i, 0))
    return pl.pallas_call(
        body, name=name, grid=(2, r // tr), in_specs=[big, one, one, big, big], out_specs=[big] * 4,
        out_shape=[jax.ShapeDtypeStruct(w.shape, F32)] * 4, compiler_params=_params("arbitrary", "arbitrary"),
    )(w, g0, g1, m, v)


def _ada_update(c_all, dmod, w, m, v):
    nl, d, n = w.shape
    tr = 256

    def body(c_ref, dm_ref, w_ref, m_ref, v_ref, g_ref, d_ref, mo_ref, vo_ref):
        cv = c_ref[...]
        act = (cv * _sigmoid(cv)).astype(BF16)
        g = _dot(act, dm_ref[...].astype(BF16), TN)
        g_ref[...] = g
        d_ref[...], mo_ref[...], vo_ref[...] = _adamw(w_ref[...], g, m_ref[...], v_ref[...])

    big = pl.BlockSpec((None, tr, n), lambda l, i: (l, i, 0))
    return pl.pallas_call(
        body, name="ada_w_update", grid=(nl, d // tr),
        in_specs=[pl.BlockSpec((8, tr), lambda l, i: (0, i)), pl.BlockSpec((None, 8, n), lambda l, i: (l, 0, 0)), big, big, big],
        out_specs=[big] * 4, out_shape=[jax.ShapeDtypeStruct(w.shape, F32)] * 4,
        compiler_params=_params("arbitrary", "arbitrary"),
    )(c_all, dmod, w, m, v)


def _small_update(gathered, w, m, v):
    r = w.shape[0]

    def body(ga_ref, w_ref, m_ref, v_ref, g_ref, d_ref, mo_ref, vo_ref):
        g = ga_ref[0]
        for dev in range(1, 8):
            g = g + ga_ref[dev]
        g_ref[...] = g
        d_ref[...], mo_ref[...], vo_ref[...] = _adamw(w_ref[...], g, m_ref[...], v_ref[...])

    return pl.pallas_call(body, name="small_update", out_shape=[jax.ShapeDtypeStruct((r, 128), F32)] * 4,
                          compiler_params=_params())(gathered, w, m, v)


SMALL = (("ada_b", 6144), ("norm_g", 2048), ("q_lat_g", 512), ("kv_lat_g", 256), ("q_norm_g", 256), ("k_norm_g", 256),
         ("glu_b", 2048), ("dw_b", 1024), ("conv_ln_g", 1024), ("conv_ln_b", 1024), ("b_pw", 1024))


def _pack_small(vals):
    cols = []
    for name, width in SMALL:
        a = vals[name]
        if a.shape[1] < width:
            a = jnp.pad(a, ((0, 0), (0, width - a.shape[1])))
        cols.append(a)
    return jnp.concatenate(cols, axis=1).reshape(-1, 128)


def _unpack_small(packed, shapes):
    flat = packed.reshape(2, -1)
    out, at = {}, 0
    for name, width in SMALL:
        out[name] = flat[:, at:at + shapes[name]]
        at += width
    return out


def _dup_gain(g):
    return jnp.concatenate([g, g[NOPE:]])[None, :]


def _undup(g):
    return jnp.concatenate([g[..., :NOPE], g[..., NOPE:NOPE + ROPE] + g[..., NOPE + ROPE:]], axis=-1)


def kernel(x, c, positions, ada_w, ada_b, norm_g, w_in, q_lat_g, w_q_up, kv_lat_g, w_kv_up, q_norm_g, k_norm_g, glu_b, dw_w, dw_b, conv_ln_g, conv_ln_b, w_pw, b_pw, w_out, loss_target, m_ada_w, m_ada_b, m_norm_g, m_w_in, m_q_lat_g, m_w_q_up, m_kv_lat_g, m_w_kv_up, m_q_norm_g, m_k_norm_g, m_glu_b, m_dw_w, m_dw_b, m_conv_ln_g, m_conv_ln_b, m_w_pw, m_b_pw, m_w_out, v_ada_w, v_ada_b, v_norm_g, v_w_in, v_q_lat_g, v_w_q_up, v_kv_lat_g, v_w_kv_up, v_q_norm_g, v_k_norm_g, v_glu_b, v_dw_w, v_dw_b, v_conv_ln_g, v_conv_ln_b, v_w_pw, v_b_pw, v_w_out):
    nl = 2
    s, d = x.shape[1], x.shape[2]
    xi, yi, ci = lax.axis_index("x"), lax.axis_index("y"), lax.axis_index("c")
    shard = 2 * xi + yi
    me = 4 * xi + 2 * yi + ci
    cidx = jnp.reshape(ci, (1,)).astype(jnp.int32)
    jc = jnp.stack([shard, ci]).astype(jnp.int32)
    x0 = x.reshape(s, d)
    target = loss_target.reshape(s, d)

    c_all = _allgather8(c.reshape(8, d // 8), "gather_c").reshape(8, d)
    n_ada = ada_w.shape[2]
    ada_b_shard = lax.dynamic_slice_in_dim(ada_b, shard * n_ada, n_ada, axis=1)[:, None, :]
    mod_shard = _modulation(c_all, ada_w, ada_b_shard)
    mod_all = _allgather8(mod_shard.reshape(nl * 8, n_ada), "gather_mod")
    mod_rows = lax.dynamic_index_in_dim(mod_all.reshape(4, 2, nl, 8, n_ada)[:, 0], me, axis=2, keepdims=False)
    mod_me = jnp.transpose(mod_rows, (1, 0, 2)).reshape(nl, 3, 1, d)

    tr = lambda a: jnp.transpose(a, (0, 2, 1))
    w_in_t = tr(w_in).astype(BF16)
    wq = w_q_up.reshape(nl, QL, 2, QK)
    wq = jnp.concatenate([wq, wq[..., NOPE:]], axis=-1)
    wq = jnp.transpose(wq, (0, 2, 1, 3)).reshape(nl, 2 * QL, HP).astype(BF16)
    dw_pad = jnp.pad(dw_w, ((0, 0), (0, HALO - CONV_K), (0, 0)))
    w_in_t, wq, wkv, dw, wpw, wout = _allgather_shards(
        [w_in_t, wq, w_kv_up.astype(BF16), dw_pad, w_pw.astype(BF16), w_out.astype(BF16)], "gather_weights")
    w_in_t = w_in_t.reshape(nl, 4 * w_in_t.shape[2], d)
    wq = wq.reshape(nl, N_HEADS, QL, HP)
    dw = jnp.transpose(dw, (0, 2, 1, 3)).reshape(nl, HALO, 4 * dw.shape[3])
    wpw = wpw.reshape(nl, 4 * wpw.shape[2], wpw.shape[3])
    wout = wout.reshape(nl, 4 * wout.shape[2], d)

    cos, sin = _rope_tables(positions.reshape(s, 1))
    row = lambda a, l: a[l][None, :]

    saved = []
    xl = x0
    for l in range(nl):
        shift, scale, gate = mod_me[l, 0], mod_me[l, 1], mod_me[l, 2]
        hb, za, zkr, zb = _inproj_fwd(xl, row(norm_g, l), scale, shift, w_in_t, l, f"inproj_fwd{l}")
        gains = (row(q_lat_g, l), row(kv_lat_g, l), _dup_gain(q_norm_g[l]), _dup_gain(k_norm_g[l]))
        q, k, v = _mla_prep_fwd(za, zkr, cos, sin, wq, wkv, *gains, l, f"mla_prep_fwd{l}")
        o, lse = _flash_fwd(q, k, v, f"flash_fwd{l}")
        cv, pw = _conv_fwd(zb, row(glu_b, l), dw, row(dw_b, l), row(conv_ln_g, l), row(conv_ln_b, l), wpw, row(b_pw, l), l, f"conv_fwd{l}")
        xn, yv, mixb = _outproj_fwd(xl, o, zb, pw, gate, wout, l, f"outproj_fwd{l}")
        saved.append(dict(x=xl, hb=hb, za=za, zkr=zkr, zb=zb, q=q, k=k, v=v, o=o, lse=lse, cv=cv, pw=pw, y=yv, mixb=mixb, gains=gains))
        xl = xn

    tok_loss, dx = _loss_head(xl, target)
    loss = lax.psum(jnp.sum(tok_loss), ("x", "y", "c"))

    big = [None] * nl
    small = [None] * nl
    for l in reversed(range(nl)):
        sv = saved[l]
        shift, scale, gate = mod_me[l, 0], mod_me[l, 1], mod_me[l, 2]
        dgate, dyb, do, delta, dzb, dpw = _outproj_bwd(dx, sv["y"], gate, wout, sv["o"], sv["zb"], sv["pw"], l, f"outproj_bwd{l}")
        g_out = _grad_tn(sv["mixb"], dyb, f"grad_w_out{l}")
        dq, dk, dv = _flash_bwd(sv["q"], sv["k"], sv["v"], do, sv["lse"], delta, f"flash_bwd{l}")
        dza, dzkr, g_q, g_kv, dgql, dgkvl, dgq, dgk = _mla_prep_bwd(
            dq, dk, dv, sv["za"], sv["zkr"], cos, sin, wq, wkv, *sv["gains"], l, f"mla_prep_bwd{l}")
        dcv, act, dbpw, dlng, dlnb = _pointwise_bwd(dpw, sv["cv"], row(conv_ln_g, l), row(conv_ln_b, l), wpw, l, f"pointwise_bwd{l}")
        g_pw = _grad_tn(act, dpw, f"grad_w_pw{l}")
        dzb, g_dw, ddwb, dglub = _conv_bwd(dcv, sv["zb"], dzb, row(glu_b, l), dw, l, f"conv_bwd{l}")
        dx, dshift, dgg = _inproj_bwd(dza, dzkr, dzb, sv["x"], dx, row(norm_g, l), scale, shift, w_in_t, l, f"inproj_bwd{l}")
        g_in = _grad_w_in(dza, dzkr, dzb, sv["hb"], f"grad_w_in{l}")
        big[l] = [g_in, g_q.reshape(N_HEADS * QL, HP), g_kv.reshape(4 * KVL, 512), g_dw.reshape(4 * HALO, CH), g_pw, g_out]
        small[l] = dict(ada_b=jnp.concatenate([dshift, dgg * row(norm_g, l), dgate], axis=1), norm_g=dgg * (1.0 + scale),
                        q_lat_g=dgql, kv_lat_g=dgkvl, q_norm_g=_undup(dgq), k_norm_g=_undup(dgk), glu_b=dglub, dw_b=ddwb,
                        conv_ln_g=dlng, conv_ln_b=dlnb, b_pw=dbpw)
    grad_x = dx.reshape(x.shape)

    names = [n for n, _ in SMALL]
    mine = _pack_small({n: jnp.concatenate([small[0][n], small[1][n]], axis=0) for n in names})
    gathered = _allgather8(mine, "gather_small")
    weights = dict(ada_b=ada_b, norm_g=norm_g, q_lat_g=q_lat_g, kv_lat_g=kv_lat_g, q_norm_g=q_norm_g, k_norm_g=k_norm_g,
                   glu_b=glu_b, dw_b=dw_b, conv_ln_g=conv_ln_g, conv_ln_b=conv_ln_b, b_pw=b_pw)
    m_small = dict(ada_b=m_ada_b, norm_g=m_norm_g, q_lat_g=m_q_lat_g, kv_lat_g=m_kv_lat_g, q_norm_g=m_q_norm_g, k_norm_g=m_k_norm_g,
                   glu_b=m_glu_b, dw_b=m_dw_b, conv_ln_g=m_conv_ln_g, conv_ln_b=m_conv_ln_b, b_pw=m_b_pw)
    v_small = dict(ada_b=v_ada_b, norm_g=v_norm_g, q_lat_g=v_q_lat_g, kv_lat_g=v_kv_lat_g, q_norm_g=v_q_norm_g, k_norm_g=v_k_norm_g,
                   glu_b=v_glu_b, dw_b=v_dw_b, conv_ln_g=v_conv_ln_g, conv_ln_b=v_conv_ln_b, b_pw=v_b_pw)
    widths = {n: weights[n].shape[1] for n in names}
    v_packed = _pack_small({n: jnp.pad(v_small[n], ((0, 0), (0, dict(SMALL)[n] - widths[n])), constant_values=1.0) for n in names})
    small_out = [_unpack_small(a, widths) for a in _small_update(gathered, _pack_small(weights), _pack_small(m_small), v_packed)]

    ada_rows = gathered.reshape(8, nl, -1)[:, :, :3 * d]
    dmod = lax.dynamic_slice_in_dim(jnp.transpose(ada_rows, (1, 0, 2)), shard * n_ada, n_ada, axis=2)
    ada_out = _ada_update(c_all, dmod, ada_w, m_ada_w, v_ada_w)

    parts = [g.reshape(4, g.shape[0] // 4, g.shape[1]) for l in range(nl) for g in big[l]]
    theirs = _pair_exchange(parts, "pair_exchange")
    chip = [_pair_sum(p, t, cidx, f"pair_sum{e}") for e, (p, t) in enumerate(zip(parts, theirs))]
    landed = _chip_scatter(chip, "chip_scatter")
    halves = [_chip_sum(sm, ld, jc, f"chip_sum{e}") for e, (sm, ld) in enumerate(zip(chip, landed))]
    full = _pair_complete(halves, "pair_complete")
    per_layer = [full[l * 6:(l + 1) * 6] for l in range(nl)]

    def natural_q(g):
        return jnp.transpose(_undup(g.reshape(2, QL, HP)), (1, 0, 2)).reshape(QL, 2 * QK)

    grads = [[per_layer[l][0], natural_q(per_layer[l][1]), per_layer[l][2], per_layer[l][3][:CONV_K], per_layer[l][4], per_layer[l][5]]
             for l in range(nl)]
    sharded = (("w_in", tr(w_in), tr(m_w_in), tr(v_w_in)), ("w_q_up", w_q_up, m_w_q_up, v_w_q_up),
               ("w_kv_up", w_kv_up, m_w_kv_up, v_w_kv_up), ("dw_w", dw_w, m_dw_w, v_dw_w),
               ("w_pw", w_pw, m_w_pw, v_w_pw), ("w_out", w_out, m_w_out, v_w_out))
    big_out = {name: _adam_update(w, grads[0][e], grads[1][e], m, v, f"adam_{name}") for e, (name, w, m, v) in enumerate(sharded)}
    big_out["w_in"] = [tr(a) for a in big_out["w_in"]]

    order = ["ada_w", "ada_b", "norm_g", "w_in", "q_lat_g", "w_q_up", "kv_lat_g", "w_kv_up", "q_norm_g", "k_norm_g", "glu_b",
             "dw_w", "dw_b", "conv_ln_g", "conv_ln_b", "w_pw", "b_pw", "w_out"]

    def leaf(kind, name):
        if name == "ada_w":
            return ada_out[kind]
        if name in big_out:
            return big_out[name][kind]
        return small_out[kind][name]

    return (loss, grad_x, *[leaf(kind, name) for kind in range(4) for name in order])
```

```python
import functools
import math

import jax
import jax.numpy as jnp
from jax import lax
from jax.experimental import pallas as pl
from jax.experimental.pallas import tpu as pltpu

F32, BF16 = jnp.float32, jnp.bfloat16
MESH = pl.DeviceIdType.MESH
ANY = pl.BlockSpec(memory_space=pl.ANY)

N_HEADS, NOPE, ROPE, VD = 8, 128, 64, 128
QK = NOPE + ROPE
QL, KVL = 512, 256
HP = 256
CONV_K, HALO = 31, 32
ROPE_THETA = 10000.0
EPS = 1e-6
ADAM_LR, ADAM_B1, ADAM_B2, ADAM_EPS, ADAM_WD, ADAM_STEP = 0.001, 0.9, 0.999, 1e-08, 0.01, 10
V7X_VMEM_LIMIT = 56 * 1024 * 1024

NT = (((1,), (1,)), ((), ()))
TN = (((0,), (0,)), ((), ()))
NN = (((1,), (0,)), ((), ()))


def _dot(a, b, dims=NN):
    return lax.dot_general(a, b, dims, preferred_element_type=F32)


def _params(*sem):
    return pltpu.CompilerParams(dimension_semantics=sem or None, vmem_limit_bytes=V7X_VMEM_LIMIT)


def _sigmoid(x):
    return 1.0 / (1.0 + jnp.exp(-x))


def _sum0(x):
    return jnp.sum(x, axis=0, keepdims=True)


def _sum1(x):
    return jnp.sum(x, axis=1, keepdims=True)


def _row(n):
    return pl.BlockSpec((1, n), lambda *_: (0, 0))


def _place():
    x, y, c = lax.axis_index("x"), lax.axis_index("y"), lax.axis_index("c")
    chips = [(1 - x, y), (x, 1 - y), (1 - x, 1 - y)]
    return x, y, c, chips


def _allgather8(v, name):
    r, n = v.shape

    def body(v_ref, out_ref, send_sems, recv_sems, local_sem):
        x, y, c, chips = _place()
        me, sibling = (x, y, c), (x, y, 1 - c)

        def slot(px, py, pc):
            return out_ref.at[4 * px + 2 * py + pc]

        def copy(k, block, to, src=None):
            return pltpu.make_async_remote_copy(
                src_ref=slot(*block) if src is None else src, dst_ref=slot(*block),
                send_sem=send_sems.at[k], recv_sem=recv_sems.at[k], device_id=to, device_id_type=MESH)

        mine = pltpu.make_async_copy(v_ref, slot(*me), local_sem)
        mine.start()
        first = [copy(0, me, sibling, src=v_ref)]
        first += [copy(1 + j, me, (*chip, c), src=v_ref) for j, chip in enumerate(chips)]
        for cp in first:
            cp.start()
        passed = [copy(4 + j, (*chip, c), sibling) for j, chip in enumerate(chips)]
        for j, chip in enumerate(chips):
            copy(1 + j, (*chip, c), me).wait_recv()
            passed[j].start()
        copy(0, sibling, me).wait_recv()
        for j, chip in enumerate(chips):
            copy(4 + j, (*chip, 1 - c), me).wait_recv()
        for cp in first + passed:
            cp.wait_send()
        mine.wait()

    return pl.pallas_call(
        body, name=name, out_shape=jax.ShapeDtypeStruct((8, r, n), v.dtype),
        in_specs=[pl.BlockSpec(memory_space=pltpu.VMEM)], out_specs=pl.BlockSpec(memory_space=pltpu.VMEM),
        scratch_shapes=[pltpu.SemaphoreType.DMA((7,)), pltpu.SemaphoreType.DMA((7,)), pltpu.SemaphoreType.DMA],
    )(v)


class _Carried:
    def __init__(self, operands, results, n_sems, start, finish, aliases=None):
        self.operands, self.results, self.n_sems = operands, results, n_sems
        self.start, self.finish, self.aliases = start, finish, aliases or {}


def _run_alone(carried, name):
    k = len(carried.operands)

    def body(*refs):
        args = (refs[:k], refs[k:k + len(carried.results)], refs[-2], refs[-1])
        carried.start(*args)
        carried.finish(*args)

    outs = pl.pallas_call(
        body, name=name, out_shape=carried.results, in_specs=[ANY] * k, out_specs=[ANY] * len(carried.results),
        input_output_aliases=carried.aliases,
        scratch_shapes=[pltpu.SemaphoreType.DMA((carried.n_sems,)), pltpu.SemaphoreType.DMA((carried.n_sems,))],
    )(*carried.operands)
    return list(outs)


def _call(body, operands, *, name, grid, in_specs, out_specs, out_shape, scratch=(), aliases=None, carried=None):
    params = _params(*(["arbitrary"] * len(grid)))
    n_in, n_out = len(in_specs), len(out_shape)
    if carried is None:
        return pl.pallas_call(body, name=name, grid=grid, in_specs=in_specs, out_specs=out_specs, out_shape=out_shape,
                              scratch_shapes=list(scratch), input_output_aliases=aliases or {}, compiler_params=params)(*operands)
    k_in, k_out = len(carried.operands), len(carried.results)

    def wrapped(*refs):
        ins, outs = refs[:n_in], refs[n_in + k_in:n_in + k_in + n_out]
        comm = (refs[n_in:n_in + k_in], refs[n_in + k_in + n_out:n_in + k_in + n_out + k_out], refs[-2], refs[-1])
        steps = [pl.program_id(a) for a in range(len(grid))]
        first = functools.reduce(jnp.logical_and, [s == 0 for s in steps])
        last = functools.reduce(jnp.logical_and, [s == g - 1 for s, g in zip(steps, grid)])

        @pl.when(first)
        def _():
            carried.start(*comm)

        body(*ins, *outs, *refs[n_in + k_in + n_out + k_out:-2])

        @pl.when(last)
        def _():
            carried.finish(*comm)

    both = dict(aliases or {})
    both.update({n_in + i: n_out + o for i, o in carried.aliases.items()})
    res = pl.pallas_call(
        wrapped, name=name, grid=grid, in_specs=list(in_specs) + [ANY] * k_in, out_specs=list(out_specs) + [ANY] * k_out,
        out_shape=list(out_shape) + list(carried.results), input_output_aliases=both, compiler_params=params,
        scratch_shapes=list(scratch) + [pltpu.SemaphoreType.DMA((carried.n_sems,)), pltpu.SemaphoreType.DMA((carried.n_sems,))],
    )(*operands, *carried.operands)
    return list(res[:n_out]), list(res[n_out:])


def _gather_start(shards):
    ne = len(shards)
    per = 4

    def copies(srcs, dsts, send_sems, recv_sems):
        x, y, c, chips = _place()
        jme = 2 * x + y
        out = []
        for e in range(ne):
            half = srcs[e].shape[2] // 2
            own = pl.ds(pl.multiple_of(c * half, 128), half)
            for k, chip in enumerate(chips):
                out.append(pltpu.make_async_remote_copy(
                    src_ref=srcs[e].at[:, :, own], dst_ref=dsts[e].at[:, jme, :, own], send_sem=send_sems.at[per * e + k],
                    recv_sem=recv_sems.at[per * e + k], device_id=(*chip, c), device_id_type=MESH))
            out.append(pltpu.make_async_remote_copy(
                src_ref=srcs[e], dst_ref=dsts[e].at[:, jme], send_sem=send_sems.at[per * e + 3],
                recv_sem=recv_sems.at[per * e + 3], device_id=(x, y, 1 - c), device_id_type=MESH))
        return out

    def start(*a):
        for cp in copies(*a):
            cp.start()

    def finish(*a):
        for cp in copies(*a):
            cp.wait()

    results = [jax.ShapeDtypeStruct((s.shape[0], 4) + s.shape[1:], s.dtype) for s in shards]
    return _Carried(list(shards), results, per * ne, start, finish)


def _gather_hand_on(bufs):
    ne = len(bufs)

    def copy(e, k, dsts, send_sems, recv_sems, mine):
        x, y, c, chips = _place()
        px, py = chips[k]
        half = dsts[e].shape[3] // 2
        cols = pl.ds(pl.multiple_of((c if mine else 1 - c) * half, 128), half)
        part = dsts[e].at[:, 2 * px + py, :, cols]
        return pltpu.make_async_remote_copy(src_ref=part, dst_ref=part, send_sem=send_sems.at[3 * e + k],
                                            recv_sem=recv_sems.at[3 * e + k], device_id=(x, y, 1 - c), device_id_type=MESH)

    def start(srcs, dsts, send_sems, recv_sems):
        for e in range(ne):
            for k in range(3):
                copy(e, k, dsts, send_sems, recv_sems, True).start()

    def finish(srcs, dsts, send_sems, recv_sems):
        for e in range(ne):
            for k in range(3):
                copy(e, k, dsts, send_sems, recv_sems, True).wait_send()
                copy(e, k, dsts, send_sems, recv_sems, False).wait_recv()

    results = [jax.ShapeDtypeStruct(b.shape, b.dtype) for b in bufs]
    return _Carried(list(bufs), results, 3 * ne, start, finish, aliases={e: e for e in range(ne)})


def _pair_exchange(parts):
    ne = len(parts)

    def copies(srcs, dsts, send_sems, recv_sems):
        x, y, c, _ = _place()
        out = []
        for e in range(ne):
            half = srcs[e].shape[2] // 2
            theirs = pl.ds(pl.multiple_of((1 - c) * half, 128), half)
            out.append(pltpu.make_async_remote_copy(
                src_ref=srcs[e].at[:, :, theirs], dst_ref=dsts[e], send_sem=send_sems.at[e],
                recv_sem=recv_sems.at[e], device_id=(x, y, 1 - c), device_id_type=MESH))
        return out

    def start(*a):
        for cp in copies(*a):
            cp.start()

    def finish(*a):
        for cp in copies(*a):
            cp.wait()

    results = [jax.ShapeDtypeStruct(p.shape[:2] + (p.shape[2] // 2,), p.dtype) for p in parts]
    return _Carried(list(parts), results, ne, start, finish)


def _chip_scatter(sums):
    ne = len(sums)

    def copies(srcs, dsts, send_sems, recv_sems):
        x, y, c, chips = _place()
        return [pltpu.make_async_remote_copy(
                    src_ref=srcs[e].at[2 * px + py], dst_ref=dsts[e].at[k], send_sem=send_sems.at[3 * e + k],
                    recv_sem=recv_sems.at[3 * e + k], device_id=(px, py, c), device_id_type=MESH)
                for e in range(ne) for k, (px, py) in enumerate(chips)]

    def start(*a):
        for cp in copies(*a):
            cp.start()

    def finish(*a):
        for cp in copies(*a):
            cp.wait()

    results = [jax.ShapeDtypeStruct((3,) + s.shape[1:], s.dtype) for s in sums]
    return _Carried(list(sums), results, 3 * ne, start, finish)


def _pair_complete(grads):
    ne = len(grads)

    def copy(e, dsts, send_sems, recv_sems, mine):
        x, y, c, _ = _place()
        half = dsts[e].shape[1] // 2
        cols = pl.ds(pl.multiple_of((c if mine else 1 - c) * half, 128), half)
        return pltpu.make_async_remote_copy(
            src_ref=dsts[e].at[:, cols], dst_ref=dsts[e].at[:, cols], send_sem=send_sems.at[e],
            recv_sem=recv_sems.at[e], device_id=(x, y, 1 - c), device_id_type=MESH)

    def start(srcs, dsts, send_sems, recv_sems):
        for e in range(ne):
            copy(e, dsts, send_sems, recv_sems, True).start()

    def finish(srcs, dsts, send_sems, recv_sems):
        for e in range(ne):
            copy(e, dsts, send_sems, recv_sems, True).wait_send()
            copy(e, dsts, send_sems, recv_sems, False).wait_recv()

    results = [jax.ShapeDtypeStruct(g.shape, g.dtype) for g in grads]
    return _Carried(list(grads), results, ne, start, finish, aliases={e: e for e in range(ne)})


def _pair_sum(part, theirs, cidx, name):
    _, r, n = part.shape
    half = n // 2

    def body(c_ref, p_ref, t_ref, o_ref):
        o_ref[...] = (p_ref[...] + t_ref[...]).astype(BF16)

    gs = pltpu.PrefetchScalarGridSpec(
        num_scalar_prefetch=1, grid=(4,),
        in_specs=[pl.BlockSpec((1, r, half), lambda j, c: (j, 0, c[0])),
                  pl.BlockSpec((1, r, half), lambda j, c: (j, 0, 0))],
        out_specs=pl.BlockSpec((1, r, half), lambda j, c: (j, 0, 0)))
    return pl.pallas_call(body, name=name, grid_spec=gs, out_shape=jax.ShapeDtypeStruct((4, r, half), BF16),
                          compiler_params=_params("arbitrary"))(cidx, part, theirs)


def _chip_sum(sums, landed, jc, name):
    _, r, half = sums.shape

    def body(jc_ref, s_ref, l_ref, o_ref):
        acc = s_ref[0].astype(F32)
        for k in range(3):
            acc = acc + l_ref[k].astype(F32)
        o_ref[...] = acc

    gs = pltpu.PrefetchScalarGridSpec(
        num_scalar_prefetch=1, grid=(1,),
        in_specs=[pl.BlockSpec((1, r, half), lambda i, jc: (jc[0], 0, 0)),
                  pl.BlockSpec((3, r, half), lambda i, jc: (0, 0, 0))],
        out_specs=pl.BlockSpec((r, half), lambda i, jc: (0, jc[1])))
    return pl.pallas_call(body, name=name, grid_spec=gs, out_shape=jax.ShapeDtypeStruct((r, 2 * half), F32),
                          compiler_params=_params("arbitrary"))(jc, sums, landed)


def _rope_tables(pos):
    s = pos.shape[0]
    lane = jnp.arange(128)
    inv = 1.0 / (ROPE_THETA ** ((2 * (lane % 32)).astype(F32) / ROPE))
    keep = (lane < 64).astype(F32)
    sign = jnp.where(lane < 32, -1.0, 1.0).astype(F32) * keep
    consts = jnp.stack([inv.astype(F32), keep, sign])[:, None, :]

    def body(p_ref, k_ref, c_ref, s_ref):
        ang = p_ref[...].astype(F32) * k_ref[0]
        c_ref[...] = jnp.cos(ang) * k_ref[1]
        s_ref[...] = jnp.sin(ang) * k_ref[2]

    tm = min(s, 1024)
    return pl.pallas_call(
        body, name="rope_tables", grid=(s // tm,),
        in_specs=[pl.BlockSpec((tm, 1), lambda i: (i, 0)), pl.BlockSpec((3, 1, 128), lambda i: (0, 0, 0))],
        out_specs=[pl.BlockSpec((tm, 128), lambda i: (i, 0))] * 2,
        out_shape=[jax.ShapeDtypeStruct((s, 128), F32)] * 2, compiler_params=_params("arbitrary"),
    )(pos, consts)


def _modulation(c_all, ada_w, ada_b_shard):
    nl, d, n = ada_w.shape
    tn = 512

    def body(c_ref, w_ref, b_ref, o_ref):
        cv = c_ref[...]
        act = (cv * _sigmoid(cv)).astype(BF16)
        o_ref[...] = _dot(act, w_ref[...].astype(BF16)) + b_ref[...]

    return pl.pallas_call(
        body, name="modulation", grid=(nl, n // tn),
        in_specs=[pl.BlockSpec((8, d), lambda l, j: (0, 0)), pl.BlockSpec((None, d, tn), lambda l, j: (l, 0, j)),
                  pl.BlockSpec((None, 1, tn), lambda l, j: (l, 0, j))],
        out_specs=pl.BlockSpec((None, 8, tn), lambda l, j: (l, 0, j)),
        out_shape=jax.ShapeDtypeStruct((nl, 8, n), F32), compiler_params=_params("arbitrary", "arbitrary"),
    )(c_all, ada_w, ada_b_shard)


def _loss_head(xf, target):
    s, d = xf.shape
    tm = min(s, 512)

    def body(x_ref, t_ref, l_ref, dx_ref):
        err = x_ref[...] - t_ref[...]
        l_ref[...] = 0.5 * jnp.mean(err * err, axis=1, keepdims=True)
        dx_ref[...] = err * (1.0 / d)

    return pl.pallas_call(
        body, name="loss_head", grid=(s // tm,),
        in_specs=[pl.BlockSpec((tm, d), lambda i: (i, 0))] * 2,
        out_specs=[pl.BlockSpec((tm, 1), lambda i: (i, 0)), pl.BlockSpec((tm, d), lambda i: (i, 0))],
        out_shape=[jax.ShapeDtypeStruct((s, 1), F32), jax.ShapeDtypeStruct((s, d), F32)],
        compiler_params=_params("arbitrary"),
    )(xf, target)


W_ROWS = 4992
W_PIECES = ((0, 0, 832), (832, 768, 64), (896, 1856, 2048), (2944, 832, 1024), (3968, 3904, 1024))


def _load_w_in(w_hbm, w_vmem, sems):
    cps = [pltpu.make_async_copy(w_hbm.at[pl.ds(src, n)], w_vmem.at[pl.ds(dst, n)], sems.at[i])
           for i, (dst, src, n) in enumerate(W_PIECES)]
    for cp in cps:
        cp.start()
    for cp in cps:
        cp.wait()


def _inproj_fwd(x, g, scale, shift, w_int, layer, name):
    s, d = x.shape
    tm = min(s, 256)

    def body(x_ref, g_ref, sc_ref, sh_ref, w_hbm, hb_ref, za_ref, zkr_ref, zb_ref, w_vmem, sems):
        @pl.when(pl.program_id(0) == 0)
        def _():
            _load_w_in(w_hbm.at[layer], w_vmem, sems)

        xv = x_ref[...]
        rstd = lax.rsqrt(jnp.mean(xv * xv, axis=1, keepdims=True) + EPS)
        h = (xv * rstd) * g_ref[...] * (1.0 + sc_ref[...]) + sh_ref[...]
        hb = h.astype(BF16)
        hb_ref[...] = hb
        za_ref[...] = _dot(hb, w_vmem[0:768], NT)
        zkr_ref[...] = _dot(hb, w_vmem[768:896], NT)
        zb_ref[...] = _dot(hb, w_vmem[896:W_ROWS], NT)

    return pl.pallas_call(
        body, name=name, grid=(s // tm,),
        in_specs=[pl.BlockSpec((tm, d), lambda i: (i, 0)), _row(d), _row(d), _row(d), ANY],
        out_specs=[pl.BlockSpec((tm, d), lambda i: (i, 0)), pl.BlockSpec((tm, 768), lambda i: (i, 0)),
                   pl.BlockSpec((tm, 128), lambda i: (i, 0)), pl.BlockSpec((tm, 4096), lambda i: (i, 0))],
        out_shape=[jax.ShapeDtypeStruct((s, d), BF16), jax.ShapeDtypeStruct((s, 768), F32),
                   jax.ShapeDtypeStruct((s, 128), F32), jax.ShapeDtypeStruct((s, 4096), F32)],
        scratch_shapes=[pltpu.VMEM((W_ROWS, d), BF16), pltpu.SemaphoreType.DMA((len(W_PIECES),))],
        compiler_params=_params("arbitrary"),
    )(x, g, scale, shift, w_int)


def _rope(yv, cos, sin):
    return yv * cos + pltpu.roll(yv, 32, axis=1) * sin


def _mla_prep_fwd(za, zkr, cos, sin, wq, wkv, gql, gkvl, gq2, gk2, layer, name):
    s = za.shape[0]
    tm = min(s, 256)

    def body(za_ref, zkr_ref, cos_ref, sin_ref, wq_ref, wkv_ref, gql_ref, gkvl_ref, gq_ref, gk_ref, q_ref, k_ref, v_ref):
        zq, zkv = za_ref[:, 0:QL], za_ref[:, QL:QL + KVL]
        qn = (zq * lax.rsqrt(jnp.mean(zq * zq, axis=1, keepdims=True) + EPS) * gql_ref[...]).astype(BF16)
        kvn = (zkv * lax.rsqrt(jnp.mean(zkv * zkv, axis=1, keepdims=True) + EPS) * gkvl_ref[...]).astype(BF16)
        kr = zkr_ref[...]
        kr_ss = 0.5 * _sum1(kr * kr)
        cos, sin = cos_ref[...], sin_ref[...]
        gq, gk = gq_ref[...], gk_ref[...]
        for h in range(N_HEADS):
            qr = _dot(qn, wq_ref[h])
            n, yv = qr[:, :NOPE], qr[:, NOPE:]
            rstd = lax.rsqrt((_sum1(n * n) + 0.5 * _sum1(yv * yv)) * (1.0 / QK) + EPS)
            q_ref[h, :, 0:NOPE] = (n * rstd * gq[:, :NOPE]).astype(BF16)
            q_ref[h, :, NOPE:HP] = _rope(yv * rstd * gq[:, NOPE:], cos, sin).astype(BF16)
            col = (h % 2) * 256
            kvr = _dot(kvn, wkv_ref[h // 2, :, col:col + 256])
            kn, vv = kvr[:, :NOPE], kvr[:, NOPE:]
            rstd = lax.rsqrt((_sum1(kn * kn) + kr_ss) * (1.0 / QK) + EPS)
            k_ref[h, :, 0:NOPE] = (kn * rstd * gk[:, :NOPE]).astype(BF16)
            k_ref[h, :, NOPE:HP] = _rope(kr * rstd * gk[:, NOPE:], cos, sin).astype(BF16)
            v_ref[h] = vv.astype(BF16)

    tile = lambda n: pl.BlockSpec((tm, n), lambda i: (i, 0))
    return pl.pallas_call(
        body, name=name, grid=(s // tm,),
        in_specs=[tile(768), tile(128), tile(128), tile(128),
                  pl.BlockSpec((None, N_HEADS, QL, HP), lambda i: (layer, 0, 0, 0)),
                  pl.BlockSpec((None, 4, KVL, 512), lambda i: (layer, 0, 0, 0)),
                  _row(QL), _row(KVL), _row(HP), _row(HP)],
        out_specs=[pl.BlockSpec((N_HEADS, tm, HP), lambda i: (0, i, 0)), pl.BlockSpec((N_HEADS, tm, HP), lambda i: (0, i, 0)),
                   pl.BlockSpec((N_HEADS, tm, VD), lambda i: (0, i, 0))],
        out_shape=[jax.ShapeDtypeStruct((N_HEADS, s, HP), BF16), jax.ShapeDtypeStruct((N_HEADS, s, HP), BF16),
                   jax.ShapeDtypeStruct((N_HEADS, s, VD), BF16)],
        compiler_params=_params("arbitrary"),
    )(za, zkr, cos, sin, wq, wkv, gql, gkvl, gq2, gk2)


SCORE_SCALE = 1.0 / math.sqrt(QK)
SCORE_LOG2 = SCORE_SCALE * math.log2(math.e)
MASKED = -1e30


def _flash_fwd(q, k, v, name, carried=None):
    s = q.shape[1]
    t = min(s, 512)

    def body(q_ref, k_ref, v_ref, o_ref, lse_ref):
        i = pl.program_id(1)
        qb = q_ref[...]
        row = lax.broadcasted_iota(jnp.int32, (t, t), 0)
        col = lax.broadcasted_iota(jnp.int32, (t, t), 1)

        def step(j, carry, diagonal):
            m, l, acc = carry
            at = pl.ds(pl.multiple_of(j * t, t), t)
            sc = _dot(qb, k_ref[at, :], NT) * SCORE_LOG2
            if diagonal:
                sc = jnp.where(col <= row, sc, MASKED)
            m_new = jnp.maximum(m, jnp.max(sc, axis=1, keepdims=True))
            p = jnp.exp2(sc - m_new)
            alpha = jnp.exp2(m - m_new)
            return m_new, alpha * l + _sum1(p), alpha * acc + _dot(p.astype(BF16), v_ref[at, :])

        init = (jnp.full((t, 1), MASKED, F32), jnp.zeros((t, 1), F32), jnp.zeros((t, VD), F32))
        carry = lax.fori_loop(0, i, lambda j, cr: step(j, cr, False), init)
        m, l, acc = step(i, carry, True)
        o_ref[...] = acc / l
        lse_ref[...] = m + jnp.log2(l)

    return _call(
        body, (q, k, v), name=name, grid=(N_HEADS, s // t), carried=carried,
        in_specs=[pl.BlockSpec((None, t, HP), lambda h, i: (h, i, 0)), pl.BlockSpec((None, s, HP), lambda h, i: (h, 0, 0)),
                  pl.BlockSpec((None, s, VD), lambda h, i: (h, 0, 0))],
        out_specs=[pl.BlockSpec((t, VD), lambda h, i: (i, h)), pl.BlockSpec((None, t, 1), lambda h, i: (h, i, 0))],
        out_shape=[jax.ShapeDtypeStruct((s, N_HEADS * VD), F32), jax.ShapeDtypeStruct((N_HEADS, s, 1), F32)])


CH, RC = 256, 64


PH_ROWS_LESS = 8


def _glu(val, gate, bias):
    c = val.shape[1]
    return (val + bias[:, :c]) * _sigmoid(gate + bias[:, c:])


def _make_phases(buf, phases, cc):
    rows = buf.shape[0] - PH_ROWS_LESS
    for b in range(1, 8):
        phases[b - 1] = buf[pl.ds(b, rows), cc:cc + CH]


def _window(buf, phases, cc, shift, r0):
    a, b = divmod(shift, 8)
    if b == 0:
        return buf[r0 + 8 * a:r0 + 8 * a + RC, cc:cc + CH]
    return phases[b - 1, r0 + 8 * a:r0 + 8 * a + RC, :]


def _conv_fwd(zb, glu_b, dw, dwb, lng, lnb, wpw, bpw, layer, name, carried=None):
    s = zb.shape[0]
    dc = dwb.shape[1]
    tm = min(s, 256)
    hb = tm // HALO

    def body(val_ref, gate_ref, valh_ref, gateh_ref, glub_ref, dw_ref, dwb_ref, lng_ref, lnb_ref, wpw_ref, bpw_ref,
             cv_ref, pw_ref, ubuf, uph):
        i = pl.program_id(0)
        bias = glub_ref[...]
        ubuf[HALO:, :] = _glu(val_ref[...], gate_ref[...], bias)
        uh = _glu(valh_ref[...], gateh_ref[...], bias)
        ubuf[0:HALO, :] = jnp.where(i > 0, uh, 0.0)
        for cc in range(0, dc, CH):
            _make_phases(ubuf, uph, cc)
            for r0 in range(0, tm, RC):
                acc = jnp.zeros((RC, CH), F32)
                for j in range(CONV_K):
                    acc = acc + _window(ubuf, uph, cc, HALO - (CONV_K - 1) + j, r0) * dw_ref[j:j + 1, cc:cc + CH]
                cv_ref[r0:r0 + RC, cc:cc + CH] = acc + dwb_ref[:, cc:cc + CH]
        cv = cv_ref[...]
        dv = cv - jnp.mean(cv, axis=1, keepdims=True)
        yl = dv * lax.rsqrt(jnp.mean(dv * dv, axis=1, keepdims=True) + EPS) * lng_ref[...] + lnb_ref[...]
        act = (yl * _sigmoid(yl)).astype(BF16)
        pw_ref[...] = _dot(act, wpw_ref[...]) + bpw_ref[...]

    return _call(
        body, (zb, zb, zb, zb, glu_b, dw, dwb, lng, lnb, wpw, bpw), name=name, grid=(s // tm,), carried=carried,
        in_specs=[pl.BlockSpec((tm, dc), lambda i: (i, 0)), pl.BlockSpec((tm, dc), lambda i: (i, 1)),
                  pl.BlockSpec((HALO, dc), lambda i: (jnp.maximum(i * hb - 1, 0), 0)),
                  pl.BlockSpec((HALO, dc), lambda i: (jnp.maximum(i * hb - 1, 0), 1)),
                  _row(2 * dc), pl.BlockSpec((None, HALO, dc), lambda i: (layer, 0, 0)), _row(dc), _row(dc), _row(dc),
                  pl.BlockSpec((None, dc, dc), lambda i: (layer, 0, 0)), _row(dc)],
        out_specs=[pl.BlockSpec((tm, dc), lambda i: (i, 0))] * 2,
        out_shape=[jax.ShapeDtypeStruct((s, dc), F32)] * 2,
        scratch=[pltpu.VMEM((tm + HALO, dc), F32), pltpu.VMEM((7, tm + HALO - PH_ROWS_LESS, CH), F32)])


def _silu_parts(z):
    sg = _sigmoid(z)
    return z * sg, sg * (1.0 + z * (1.0 - sg))


def _outproj_fwd(x, o, zb, pw, gate, wout, layer, name):
    s, d = x.shape
    dm = o.shape[1]
    tm = min(s, 256)

    def body(x_ref, o_ref, mg_ref, cg_ref, pw_ref, gate_ref, w_ref, xn_ref, y_ref, mix_ref):
        mg, cg = mg_ref[...], cg_ref[...]
        mix_ref[:, 0:dm] = (o_ref[...] * (mg * _sigmoid(mg))).astype(BF16)
        mix_ref[:, dm:] = (pw_ref[...] * (cg * _sigmoid(cg))).astype(BF16)
        yv = _dot(mix_ref[...], w_ref[...])
        y_ref[...] = yv
        xn_ref[...] = x_ref[...] + gate_ref[...] * yv

    tile = lambda n, j=0: pl.BlockSpec((tm, n), lambda i: (i, j))
    return pl.pallas_call(
        body, name=name, grid=(s // tm,),
        in_specs=[tile(d), tile(dm), tile(dm, 2), tile(dm, 3), tile(dm), _row(d),
                  pl.BlockSpec((None, 2 * dm, d), lambda i: (layer, 0, 0))],
        out_specs=[tile(d), tile(d), tile(2 * dm)],
        out_shape=[jax.ShapeDtypeStruct((s, d), F32), jax.ShapeDtypeStruct((s, d), F32), jax.ShapeDtypeStruct((s, 2 * dm), BF16)],
        compiler_params=_params("arbitrary"),
    )(x, o, zb, zb, pw, gate, wout)


def _grad_tn(a, b, name):
    s, n = a.shape
    m = b.shape[1]
    tn, ts = min(n, 1024), min(s, 512)

    def body(a_ref, b_ref, o_ref):
        @pl.when(pl.program_id(1) == 0)
        def _():
            o_ref[...] = jnp.zeros_like(o_ref)

        o_ref[...] += _dot(a_ref[...].astype(BF16), b_ref[...].astype(BF16), TN)

    return pl.pallas_call(
        body, name=name, grid=(n // tn, s // ts),
        in_specs=[pl.BlockSpec((ts, tn), lambda r, t: (t, r)), pl.BlockSpec((ts, m), lambda r, t: (t, 0))],
        out_specs=pl.BlockSpec((tn, m), lambda r, t: (r, 0)),
        out_shape=jax.ShapeDtypeStruct((n, m), F32), compiler_params=_params("arbitrary", "arbitrary"),
    )(a, b)


def _outproj_bwd(dxo, y, gate, wout, o, zb, pw, layer, name, carried=None):
    s, d = dxo.shape
    dm = o.shape[1]
    tm = min(s, 256)

    def body(dx_ref, y_ref, gate_ref, w_ref, o_ref, mg_ref, cg_ref, pw_ref,
             dgate_ref, dy_ref, do_ref, delta_ref, dzb_ref, dpw_ref):
        @pl.when(pl.program_id(0) == 0)
        def _():
            dgate_ref[...] = jnp.zeros_like(dgate_ref)

        dx = dx_ref[...]
        dgate_ref[...] += _sum0(dx * y_ref[...])
        dyb = (dx * gate_ref[...]).astype(BF16)
        dy_ref[...] = dyb
        dmix = _dot(dyb, w_ref[...], NT)
        da, db = dmix[:, :dm], dmix[:, dm:]
        ov = o_ref[...]
        silu_m, dsilu_m = _silu_parts(mg_ref[...])
        do = da * silu_m
        do_ref[...] = do.astype(BF16)
        prod = do * ov
        for h in range(N_HEADS):
            delta_ref[h] = _sum1(prod[:, h * VD:(h + 1) * VD])
        dzb_ref[:, 0:dm] = da * ov * dsilu_m
        silu_c, dsilu_c = _silu_parts(cg_ref[...])
        dpw_ref[...] = db * silu_c
        dzb_ref[:, dm:] = db * pw_ref[...] * dsilu_c

    tile = lambda n, j=0: pl.BlockSpec((tm, n), lambda i: (i, j))
    return _call(
        body, (dxo, y, gate, wout, o, zb, zb, pw), name=name, grid=(s // tm,), carried=carried,
        in_specs=[tile(d), tile(d), _row(d), pl.BlockSpec((None, 2 * dm, d), lambda i: (layer, 0, 0)),
                  tile(dm), tile(dm, 2), tile(dm, 3), tile(dm)],
        out_specs=[_row(d), tile(d), tile(dm), pl.BlockSpec((N_HEADS, tm, 1), lambda i: (0, i, 0)), tile(2 * dm, 1), tile(dm)],
        out_shape=[jax.ShapeDtypeStruct((1, d), F32), jax.ShapeDtypeStruct((s, d), BF16), jax.ShapeDtypeStruct((s, dm), BF16),
                   jax.ShapeDtypeStruct((N_HEADS, s, 1), F32), jax.ShapeDtypeStruct((s, 4 * dm), F32),
                   jax.ShapeDtypeStruct((s, dm), F32)])


def _flash_bwd(q, k, v, do, lse, delta, name, carried=None):
    s = q.shape[1]
    t = min(s, 512)
    nq = s // t

    def body(q_ref, k_ref, v_ref, do_ref, lse_ref, delta_ref, dq_ref, dk_ref, dv_ref):
        j = pl.program_id(1)

        @pl.when(j == 0)
        def _():
            dq_ref[...] = jnp.zeros_like(dq_ref)

        kb, vb = k_ref[...], v_ref[...]
        row = lax.broadcasted_iota(jnp.int32, (t, t), 0)
        col = lax.broadcasted_iota(jnp.int32, (t, t), 1)

        def step(i, carry, diagonal):
            dk, dv = carry
            at = pl.ds(pl.multiple_of(i * t, t), t)
            qi, doi = q_ref[at, :], do_ref[at, :]
            p = jnp.exp2(_dot(qi, kb, NT) * SCORE_LOG2 - lse_ref[at, :])
            if diagonal:
                p = jnp.where(col <= row, p, 0.0)
            dv = dv + _dot(p.astype(BF16), doi, TN)
            dp = _dot(doi, vb, NT)
            ds = (p * (dp - delta_ref[at, :]) * SCORE_SCALE).astype(BF16)
            dq_ref[at, :] += _dot(ds, kb)
            return dk + _dot(ds, qi, TN), dv

        carry = step(j, (jnp.zeros((t, HP), F32), jnp.zeros((t, VD), F32)), True)
        dk, dv = lax.fori_loop(j + 1, nq, lambda i, cr: step(i, cr, False), carry)
        dk_ref[...] = dk
        dv_ref[...] = dv

    whole = lambda n: pl.BlockSpec((None, s, n), lambda h, j: (h, 0, 0))
    blk = lambda n: pl.BlockSpec((None, t, n), lambda h, j: (h, j, 0))
    return _call(
        body, (q, k, v, do, lse, delta), name=name, grid=(N_HEADS, nq), carried=carried,
        in_specs=[whole(HP), blk(HP), blk(VD), pl.BlockSpec((s, VD), lambda h, j: (0, h)), whole(1), whole(1)],
        out_specs=[whole(HP), blk(HP), blk(VD)],
        out_shape=[jax.ShapeDtypeStruct((N_HEADS, s, HP), F32), jax.ShapeDtypeStruct((N_HEADS, s, HP), F32),
                   jax.ShapeDtypeStruct((N_HEADS, s, VD), F32)])


def _mla_prep_bwd(dq, dk, dv, za, zkr, cos, sin, wq, wkv, gql, gkvl, gq2, gk2, layer, name):
    s = za.shape[0]
    tm = min(s, 512)

    def norm_bwd(n, yv, rstd, gain, d_n_out, d_y_out):
        tn_, ty = n * rstd, yv * rstd
        dgain_n, dgain_y = _sum0(d_n_out * tn_), _sum0(d_y_out * ty)
        dtn, dty = d_n_out * gain[:, :NOPE], d_y_out * gain[:, NOPE:]
        a = (_sum1(dtn * n) + _sum1(dty * yv)) * (rstd * rstd * rstd * (1.0 / QK))
        return rstd * dtn - n * a, rstd * dty - (0.5 * yv) * a, dgain_n, dgain_y

    def rope_bwd(d_out, cos, sin):
        return d_out * cos + pltpu.roll(d_out * sin, 128 - 32, axis=1)

    def latent_bwd(z, gain, dn):
        rstd = lax.rsqrt(jnp.mean(z * z, axis=1, keepdims=True) + EPS)
        zh = z * rstd
        dzh = dn * gain
        return rstd * (dzh - zh * jnp.mean(dzh * zh, axis=1, keepdims=True)), _sum0(dn * zh)

    def body(dq_ref, dk_ref, dv_ref, za_ref, zkr_ref, cos_ref, sin_ref, wq_ref, wkv_ref, gql_ref, gkvl_ref, gq_ref, gk_ref,
             dza_ref, dzkr_ref, gwq_ref, gwkv_ref, dgql_ref, dgkvl_ref, dgq_ref, dgk_ref):
        @pl.when(pl.program_id(0) == 0)
        def _():
            for r in (gwq_ref, gwkv_ref, dgql_ref, dgkvl_ref, dgq_ref, dgk_ref):
                r[...] = jnp.zeros_like(r)

        zq, zkv = za_ref[:, 0:QL], za_ref[:, QL:QL + KVL]
        qf = zq * lax.rsqrt(jnp.mean(zq * zq, axis=1, keepdims=True) + EPS) * gql_ref[...]
        kvf = zkv * lax.rsqrt(jnp.mean(zkv * zkv, axis=1, keepdims=True) + EPS) * gkvl_ref[...]
        qn, kvn = qf.astype(BF16), kvf.astype(BF16)
        qn_t, kvn_t = qf.T.astype(BF16), kvf.T.astype(BF16)
        kr = zkr_ref[...]
        kr_ss = 0.5 * _sum1(kr * kr)
        cos, sin = cos_ref[...], sin_ref[...]
        gq, gk = gq_ref[...], gk_ref[...]
        dqn = jnp.zeros((tm, QL), F32)
        dkvn = jnp.zeros((tm, KVL), F32)
        dkr = jnp.zeros((tm, 128), F32)
        for h in range(N_HEADS):
            qr = _dot(qn, wq_ref[h])
            n, yv = qr[:, :NOPE], qr[:, NOPE:]
            rstd = lax.rsqrt((_sum1(n * n) + 0.5 * _sum1(yv * yv)) * (1.0 / QK) + EPS)
            dqh = dq_ref[h]
            dn, dy, dg_n, dg_y = norm_bwd(n, yv, rstd, gq, dqh[:, :NOPE], rope_bwd(dqh[:, NOPE:], cos, sin))
            dgq_ref[:, 0:NOPE] += dg_n
            dgq_ref[:, NOPE:] += dg_y
            dqr = jnp.concatenate([dn, dy], axis=1).astype(BF16)
            gwq_ref[h] += _dot(qn_t, dqr)
            dqn = dqn + _dot(dqr, wq_ref[h], NT)

            col = (h % 2) * 256
            wkv_h = wkv_ref[h // 2, :, col:col + 256]
            kvr = _dot(kvn, wkv_h)
            kn = kvr[:, :NOPE]
            rstd = lax.rsqrt((_sum1(kn * kn) + kr_ss) * (1.0 / QK) + EPS)
            dkh = dk_ref[h]
            dn, dy, dg_n, dg_y = norm_bwd(kn, kr, rstd, gk, dkh[:, :NOPE], rope_bwd(dkh[:, NOPE:], cos, sin))
            dgk_ref[:, 0:NOPE] += dg_n
            dgk_ref[:, NOPE:] += dg_y
            dkr = dkr + dy
            dkvr = jnp.concatenate([dn, dv_ref[h]], axis=1).astype(BF16)
            gwkv_ref[h // 2, :, col:col + 256] += _dot(kvn_t, dkvr)
            dkvn = dkvn + _dot(dkvr, wkv_h, NT)

        dzq, dgql = latent_bwd(zq, gql_ref[...], dqn)
        dzkv, dgkvl = latent_bwd(zkv, gkvl_ref[...], dkvn)
        dgql_ref[...] += dgql
        dgkvl_ref[...] += dgkvl
        dza_ref[:, 0:QL] = dzq
        dza_ref[:, QL:] = dzkv
        lane = lax.broadcasted_iota(jnp.int32, (tm, 128), 1)
        dzkr_ref[...] = jnp.where(lane < ROPE, dkr + pltpu.roll(dkr, 64, axis=1), 0.0)

    tile = lambda n: pl.BlockSpec((tm, n), lambda i: (i, 0))
    heads = lambda n: pl.BlockSpec((N_HEADS, tm, n), lambda i: (0, i, 0))
    return pl.pallas_call(
        body, name=name, grid=(s // tm,),
        in_specs=[heads(HP), heads(HP), heads(VD), tile(768), tile(128), tile(128), tile(128),
                  pl.BlockSpec((None, N_HEADS, QL, HP), lambda i: (layer, 0, 0, 0)),
                  pl.BlockSpec((None, 4, KVL, 512), lambda i: (layer, 0, 0, 0)),
                  _row(QL), _row(KVL), _row(HP), _row(HP)],
        out_specs=[tile(768), tile(128), pl.BlockSpec((N_HEADS, QL, HP), lambda i: (0, 0, 0)),
                   pl.BlockSpec((4, KVL, 512), lambda i: (0, 0, 0)), _row(QL), _row(KVL), _row(HP), _row(HP)],
        out_shape=[jax.ShapeDtypeStruct((s, 768), F32), jax.ShapeDtypeStruct((s, 128), F32),
                   jax.ShapeDtypeStruct((N_HEADS, QL, HP), F32), jax.ShapeDtypeStruct((4, KVL, 512), F32),
                   jax.ShapeDtypeStruct((1, QL), F32), jax.ShapeDtypeStruct((1, KVL), F32),
                   jax.ShapeDtypeStruct((1, HP), F32), jax.ShapeDtypeStruct((1, HP), F32)],
        compiler_params=_params("arbitrary"),
    )(dq, dk, dv, za, zkr, cos, sin, wq, wkv, gql, gkvl, gq2, gk2)


def _pointwise_bwd(dpw, cv, lng, lnb, wpw, layer, name):
    s, dc = cv.shape
    tm = min(s, 256)

    def body(dpw_ref, cv_ref, lng_ref, lnb_ref, w_ref, dcv_ref, act_ref, dbpw_ref, dlng_ref, dlnb_ref):
        @pl.when(pl.program_id(0) == 0)
        def _():
            for r in (dbpw_ref, dlng_ref, dlnb_ref):
                r[...] = jnp.zeros_like(r)

        cv = cv_ref[...]
        dv = cv - jnp.mean(cv, axis=1, keepdims=True)
        rstd = lax.rsqrt(jnp.mean(dv * dv, axis=1, keepdims=True) + EPS)
        xh = dv * rstd
        yl = xh * lng_ref[...] + lnb_ref[...]
        silu, dsilu = _silu_parts(yl)
        act_ref[...] = silu.astype(BF16)
        dpw = dpw_ref[...]
        dbpw_ref[...] += _sum0(dpw)
        dyl = _dot(dpw.astype(BF16), w_ref[...], NT) * dsilu
        dlng_ref[...] += _sum0(dyl * xh)
        dlnb_ref[...] += _sum0(dyl)
        dxh = dyl * lng_ref[...]
        dcv_ref[...] = rstd * (dxh - jnp.mean(dxh, axis=1, keepdims=True) - xh * jnp.mean(dxh * xh, axis=1, keepdims=True))

    tile = pl.BlockSpec((tm, dc), lambda i: (i, 0))
    return pl.pallas_call(
        body, name=name, grid=(s // tm,),
        in_specs=[tile, tile, _row(dc), _row(dc), pl.BlockSpec((None, dc, dc), lambda i: (layer, 0, 0))],
        out_specs=[tile, tile, _row(dc), _row(dc), _row(dc)],
        out_shape=[jax.ShapeDtypeStruct((s, dc), F32), jax.ShapeDtypeStruct((s, dc), BF16)] + [jax.ShapeDtypeStruct((1, dc), F32)] * 3,
        compiler_params=_params("arbitrary"),
    )(dpw, cv, lng, lnb, wpw)


def _conv_bwd(dcv, zb, dzb, glu_b, dw, layer, name):
    s, dc = dcv.shape
    tm = min(s, 256)
    hb = tm // HALO
    last = s // tm - 1

    def body(dcv_ref, dcvn_ref, val_ref, gate_ref, valh_ref, gateh_ref, glub_ref, dw_ref, _, dzb_ref, gdw_ref, ddwb_ref, dglub_ref,
             ubuf, dbuf, gacc, uph, dph):
        i = pl.program_id(0)

        @pl.when(i == 0)
        def _():
            gacc[...] = jnp.zeros_like(gacc)
            ddwb_ref[...] = jnp.zeros_like(ddwb_ref)
            dglub_ref[...] = jnp.zeros_like(dglub_ref)

        bias = glub_ref[...]
        ubuf[HALO:, :] = _glu(val_ref[...], gate_ref[...], bias)
        ubuf[0:HALO, :] = jnp.where(i > 0, _glu(valh_ref[...], gateh_ref[...], bias), 0.0)
        dcv = dcv_ref[...]
        dbuf[0:tm, :] = dcv
        dbuf[tm:, :] = jnp.where(i < last, dcvn_ref[...], 0.0)
        ddwb_ref[...] += _sum0(dcv)
        for cc in range(0, dc, CH):
            _make_phases(ubuf, uph, cc)
            _make_phases(dbuf, dph, cc)
            for r0 in range(0, tm, RC):
                du = jnp.zeros((RC, CH), F32)
                dpiece = dbuf[r0:r0 + RC, cc:cc + CH]
                for j in range(CONV_K):
                    du = du + _window(dbuf, dph, cc, (CONV_K - 1) - j, r0) * dw_ref[j:j + 1, cc:cc + CH]
                    win = _window(ubuf, uph, cc, HALO - (CONV_K - 1) + j, r0)
                    gacc[j, :, cc:cc + CH] += (dpiece * win).reshape(RC // 8, 8, CH).sum(axis=0)
                a = val_ref[r0:r0 + RC, cc:cc + CH] + bias[:, cc:cc + CH]
                sg = _sigmoid(gate_ref[r0:r0 + RC, cc:cc + CH] + bias[:, dc + cc:dc + cc + CH])
                dzb_ref[r0:r0 + RC, cc:cc + CH] = du * sg
                dzb_ref[r0:r0 + RC, dc + cc:dc + cc + CH] = du * a * sg * (1.0 - sg)
        dglub_ref[...] += _sum0(dzb_ref[...])

        @pl.when(i == last)
        def _():
            total = jnp.sum(gacc[...], axis=1)
            for cc in range(0, dc, CH):
                gdw_ref[cc // CH] = total[:, cc:cc + CH]

    return pl.pallas_call(
        body, name=name, grid=(s // tm,),
        in_specs=[pl.BlockSpec((tm, dc), lambda i: (i, 0)),
                  pl.BlockSpec((HALO, dc), lambda i: (jnp.minimum((i + 1) * hb, s // HALO - 1), 0)),
                  pl.BlockSpec((tm, dc), lambda i: (i, 0)), pl.BlockSpec((tm, dc), lambda i: (i, 1)),
                  pl.BlockSpec((HALO, dc), lambda i: (jnp.maximum(i * hb - 1, 0), 0)),
                  pl.BlockSpec((HALO, dc), lambda i: (jnp.maximum(i * hb - 1, 0), 1)),
                  _row(2 * dc), pl.BlockSpec((None, HALO, dc), lambda i: (layer, 0, 0)), ANY],
        out_specs=[pl.BlockSpec((tm, 2 * dc), lambda i: (i, 0)), pl.BlockSpec((4, HALO, CH), lambda i: (0, 0, 0)),
                   _row(dc), _row(2 * dc)],
        out_shape=[jax.ShapeDtypeStruct(dzb.shape, F32), jax.ShapeDtypeStruct((4, HALO, CH), F32),
                   jax.ShapeDtypeStruct((1, dc), F32), jax.ShapeDtypeStruct((1, 2 * dc), F32)],
        scratch_shapes=[pltpu.VMEM((tm + HALO, dc), F32), pltpu.VMEM((tm + HALO, dc), F32), pltpu.VMEM((HALO, 8, dc), F32),
                        pltpu.VMEM((7, tm + HALO - PH_ROWS_LESS, CH), F32), pltpu.VMEM((7, tm + HALO - PH_ROWS_LESS, CH), F32)],
        input_output_aliases={8: 0},
        compiler_params=_params("arbitrary"),
    )(dcv, dcv, zb, zb, zb, zb, glu_b, dw, dzb)


def _inproj_bwd(dza, dzkr, dzb, x, dxo, g, scale, shift, w_int, layer, name):
    s, d = x.shape
    tm = min(s, 128)

    def body(dza_ref, dzkr_ref, dzb_ref, x_ref, dxo_ref, g_ref, sc_ref, sh_ref, w_hbm, dx_ref, dsh_ref, dgg_ref, w_vmem, sems):
        @pl.when(pl.program_id(0) == 0)
        def _():
            _load_w_in(w_hbm.at[layer], w_vmem, sems)
            dsh_ref[...] = jnp.zeros_like(dsh_ref)
            dgg_ref[...] = jnp.zeros_like(dgg_ref)

        dh = _dot(dza_ref[...].astype(BF16), w_vmem[0:768])
        dh = dh + _dot(dzkr_ref[...].astype(BF16), w_vmem[768:896])
        dh = dh + _dot(dzb_ref[...].astype(BF16), w_vmem[896:W_ROWS])
        xv = x_ref[...]
        rstd = lax.rsqrt(jnp.mean(xv * xv, axis=1, keepdims=True) + EPS)
        xh = xv * rstd
        dsh_ref[...] += _sum0(dh)
        dgg_ref[...] += _sum0(dh * xh)
        dxh = dh * (g_ref[...] * (1.0 + sc_ref[...]))
        dx_ref[...] = dxo_ref[...] + rstd * (dxh - xh * jnp.mean(dxh * xh, axis=1, keepdims=True))

    tile = lambda n: pl.BlockSpec((tm, n), lambda i: (i, 0))
    return pl.pallas_call(
        body, name=name, grid=(s // tm,),
        in_specs=[tile(768), tile(128), tile(4096), tile(d), tile(d), _row(d), _row(d), _row(d), ANY],
        out_specs=[tile(d), _row(d), _row(d)],
        out_shape=[jax.ShapeDtypeStruct((s, d), F32), jax.ShapeDtypeStruct((1, d), F32), jax.ShapeDtypeStruct((1, d), F32)],
        scratch_shapes=[pltpu.VMEM((W_ROWS, d), BF16), pltpu.SemaphoreType.DMA((len(W_PIECES),))],
        compiler_params=_params("arbitrary"),
    )(dza, dzkr, dzb, x, dxo, g, scale, shift, w_int)


def _grad_w_in(dza, dzkr, dzb, hb, name):
    s, d = hb.shape
    ts = min(s, 512)
    nt = s // ts
    tiles = ((0, 768), (768, 64), (1856, 1024), (2880, 1024), (832, 1024), (3904, 1024))

    def body(a_ref, kr_ref, b_ref, h_ref, o_hbm, acc, sem):
        r, t = pl.program_id(0), pl.program_id(1)

        @pl.when(t == 0)
        def _():
            acc[...] = jnp.zeros_like(acc)

        hv = h_ref[...]

        @pl.when(r == 0)
        def _():
            acc[0:768, :] += _dot(a_ref[...].astype(BF16), hv, TN)

        @pl.when(r == 1)
        def _():
            acc[0:128, :] += _dot(kr_ref[...].astype(BF16), hv, TN)

        @pl.when(r >= 2)
        def _():
            acc[...] += _dot(b_ref[...].astype(BF16), hv, TN)

        for tile, (row0, rows) in enumerate(tiles):
            @pl.when((t == nt - 1) & (r == tile))
            def _():
                cp = pltpu.make_async_copy(acc.at[pl.ds(0, rows)], o_hbm.at[pl.ds(row0, rows)], sem)
                cp.start()
                cp.wait()

    return pl.pallas_call(
        body, name=name, grid=(len(tiles), nt),
        in_specs=[pl.BlockSpec((ts, 768), lambda r, t: (jnp.where(r == 0, t, nt - 1), 0)),
                  pl.BlockSpec((ts, 128), lambda r, t: (jnp.where(r == 1, t, jnp.where(r == 0, 0, nt - 1)), 0)),
                  pl.BlockSpec((ts, 1024), lambda r, t: (jnp.where(r >= 2, t, 0), jnp.maximum(r - 2, 0))),
                  pl.BlockSpec((ts, d), lambda r, t: (t, 0))],
        out_specs=ANY, out_shape=jax.ShapeDtypeStruct((4928, d), F32),
        scratch_shapes=[pltpu.VMEM((1024, d), F32), pltpu.SemaphoreType.DMA],
        compiler_params=_params("arbitrary", "arbitrary"),
    )(dza, dzkr, dzb, hb)


def _adamw(w, g, m, v):
    m = ADAM_B1 * m + (1.0 - ADAM_B1) * g
    v = ADAM_B2 * v + (1.0 - ADAM_B2) * (g * g)
    m_hat = m / (1.0 - ADAM_B1 ** ADAM_STEP)
    v_hat = v / (1.0 - ADAM_B2 ** ADAM_STEP)
    return -ADAM_LR * (m_hat / (jnp.sqrt(v_hat) + ADAM_EPS) + ADAM_WD * w), m, v


def _adam_update(w, g0, g1, m, v, name):
    _, r, c = w.shape
    fits = [t for t in range(8, r + 1, 8) if r % t == 0 and t * c * 4 <= (1 << 21)]
    tr = max(fits) if fits else r

    def body(w_ref, g0_ref, g1_ref, m_ref, v_ref, g_ref, d_ref, mo_ref, vo_ref):
        g = jnp.where(pl.program_id(0) == 0, g0_ref[...], g1_ref[...])
        g_ref[...] = g
        d_ref[...], mo_ref[...], vo_ref[...] = _adamw(w_ref[...], g, m_ref[...], v_ref[...])

    big = pl.BlockSpec((None, tr, c), lambda l, i: (l, i, 0))
    one = pl.BlockSpec((tr, c), lambda l, i: (i, 0))
    return pl.pallas_call(
        body, name=name, grid=(2, r // tr), in_specs=[big, one, one, big, big], out_specs=[big] * 4,
        out_shape=[jax.ShapeDtypeStruct(w.shape, F32)] * 4, compiler_params=_params("arbitrary", "arbitrary"),
    )(w, g0, g1, m, v)


def _ada_update(c_all, dmod, w, m, v):
    nl, d, n = w.shape
    tr = 256

    def body(c_ref, dm_ref, w_ref, m_ref, v_ref, g_ref, d_ref, mo_ref, vo_ref):
        cv = c_ref[...]
        act = (cv * _sigmoid(cv)).astype(BF16)
        g = _dot(act, dm_ref[...].astype(BF16), TN)
        g_ref[...] = g
        d_ref[...], mo_ref[...], vo_ref[...] = _adamw(w_ref[...], g, m_ref[...], v_ref[...])

    big = pl.BlockSpec((None, tr, n), lambda l, i: (l, i, 0))
    return pl.pallas_call(
        body, name="ada_w_update", grid=(nl, d // tr),
        in_specs=[pl.BlockSpec((8, tr), lambda l, i: (0, i)), pl.BlockSpec((None, 8, n), lambda l, i: (l, 0, 0)), big, big, big],
        out_specs=[big] * 4, out_shape=[jax.ShapeDtypeStruct(w.shape, F32)] * 4,
        compiler_params=_params("arbitrary", "arbitrary"),
    )(c_all, dmod, w, m, v)


def _small_update(gathered, w, m, v):
    r = w.shape[0]

    def body(ga_ref, w_ref, m_ref, v_ref, g_ref, d_ref, mo_ref, vo_ref):
        g = ga_ref[0]
        for dev in range(1, 8):
            g = g + ga_ref[dev]
        g_ref[...] = g
        d_ref[...], mo_ref[...], vo_ref[...] = _adamw(w_ref[...], g, m_ref[...], v_ref[...])

    return pl.pallas_call(body, name="small_update", out_shape=[jax.ShapeDtypeStruct((r, 128), F32)] * 4,
                          compiler_params=_params())(gathered, w, m, v)


SMALL = (("ada_b", 6144), ("norm_g", 2048), ("q_lat_g", 512), ("kv_lat_g", 256), ("q_norm_g", 256), ("k_norm_g", 256),
         ("glu_b", 2048), ("dw_b", 1024), ("conv_ln_g", 1024), ("conv_ln_b", 1024), ("b_pw", 1024))


def _pack_small(vals):
    cols = []
    for name, width in SMALL:
        a = vals[name]
        if a.shape[1] < width:
            a = jnp.pad(a, ((0, 0), (0, width - a.shape[1])))
        cols.append(a)
    return jnp.concatenate(cols, axis=1).reshape(-1, 128)


def _unpack_small(packed, shapes):
    flat = packed.reshape(2, -1)
    out, at = {}, 0
    for name, width in SMALL:
        out[name] = flat[:, at:at + shapes[name]]
        at += width
    return out


def _dup_gain(g):
    return jnp.concatenate([g, g[NOPE:]])[None, :]


def _undup(g):
    return jnp.concatenate([g[..., :NOPE], g[..., NOPE:NOPE + ROPE] + g[..., NOPE + ROPE:]], axis=-1)


def kernel(x, c, positions, ada_w, ada_b, norm_g, w_in, q_lat_g, w_q_up, kv_lat_g, w_kv_up, q_norm_g, k_norm_g, glu_b, dw_w, dw_b, conv_ln_g, conv_ln_b, w_pw, b_pw, w_out, loss_target, m_ada_w, m_ada_b, m_norm_g, m_w_in, m_q_lat_g, m_w_q_up, m_kv_lat_g, m_w_kv_up, m_q_norm_g, m_k_norm_g, m_glu_b, m_dw_w, m_dw_b, m_conv_ln_g, m_conv_ln_b, m_w_pw, m_b_pw, m_w_out, v_ada_w, v_ada_b, v_norm_g, v_w_in, v_q_lat_g, v_w_q_up, v_kv_lat_g, v_w_kv_up, v_q_norm_g, v_k_norm_g, v_glu_b, v_dw_w, v_dw_b, v_conv_ln_g, v_conv_ln_b, v_w_pw, v_b_pw, v_w_out):
    nl = 2
    s, d = x.shape[1], x.shape[2]
    xi, yi, ci = lax.axis_index("x"), lax.axis_index("y"), lax.axis_index("c")
    shard = 2 * xi + yi
    me = 4 * xi + 2 * yi + ci
    cidx = jnp.reshape(ci, (1,)).astype(jnp.int32)
    jc = jnp.stack([shard, ci]).astype(jnp.int32)
    x0 = x.reshape(s, d)
    target = loss_target.reshape(s, d)

    c_all = _allgather8(c.reshape(8, d // 8), "gather_c").reshape(8, d)
    n_ada = ada_w.shape[2]
    ada_b_shard = lax.dynamic_slice_in_dim(ada_b, shard * n_ada, n_ada, axis=1)[:, None, :]
    mod_shard = _modulation(c_all, ada_w, ada_b_shard)
    mod_all = _allgather8(mod_shard.reshape(nl * 8, n_ada), "gather_mod")
    mod_rows = lax.dynamic_index_in_dim(mod_all.reshape(4, 2, nl, 8, n_ada)[:, 0], me, axis=2, keepdims=False)
    mod_me = jnp.transpose(mod_rows, (1, 0, 2)).reshape(nl, 3, 1, d)

    tr = lambda a: jnp.transpose(a, (0, 2, 1))
    w_in_t = tr(w_in).astype(BF16)
    wq = w_q_up.reshape(nl, QL, 2, QK)
    wq = jnp.concatenate([wq, wq[..., NOPE:]], axis=-1)
    wq = jnp.transpose(wq, (0, 2, 1, 3)).reshape(nl, 2 * QL, HP).astype(BF16)
    dw_pad = jnp.pad(dw_w, ((0, 0), (0, HALO - CONV_K), (0, 0)))
    local = [w_in_t, wq, w_kv_up.astype(BF16), dw_pad, w_pw.astype(BF16), w_out.astype(BF16)]

    def kernel_layouts(bufs):
        w_in_g, wq_g, wkv_g, dw_g, wpw_g, wout_g = bufs
        return dict(w_in=w_in_g.reshape(1, 4 * w_in_g.shape[2], d), wq=wq_g.reshape(1, N_HEADS, QL, HP), wkv=wkv_g,
                    dw=jnp.transpose(dw_g, (0, 2, 1, 3)).reshape(1, HALO, 4 * dw_g.shape[3]),
                    wpw=wpw_g.reshape(1, 4 * wpw_g.shape[2], wpw_g.shape[3]), wout=wout_g.reshape(1, 4 * wout_g.shape[2], d))

    landed = _run_alone(_gather_start([a[0:1] for a in local]), "gather0")
    wts = [kernel_layouts(_run_alone(_gather_hand_on(landed), "gather0_hand_on")), None]
    next_gather = _gather_start([a[1:2] for a in local])

    cos, sin = _rope_tables(positions.reshape(s, 1))
    row = lambda a, l: a[l][None, :]

    saved = []
    xl = x0
    for l in range(nl):
        w = wts[l]
        shift, scale, gate = mod_me[l, 0], mod_me[l, 1], mod_me[l, 2]
        hb, za, zkr, zb = _inproj_fwd(xl, row(norm_g, l), scale, shift, w["w_in"], 0, f"inproj_fwd{l}")
        gains = (row(q_lat_g, l), row(kv_lat_g, l), _dup_gain(q_norm_g[l]), _dup_gain(k_norm_g[l]))
        q, k, v = _mla_prep_fwd(za, zkr, cos, sin, w["wq"], w["wkv"], *gains, 0, f"mla_prep_fwd{l}")
        conv_args = (zb, row(glu_b, l), w["dw"], row(dw_b, l), row(conv_ln_g, l), row(conv_ln_b, l), w["wpw"], row(b_pw, l), 0)
        if l == 0:
            (o, lse), landed = _flash_fwd(q, k, v, f"flash_fwd{l}", carried=next_gather)
            (cv, pw), gathered1 = _conv_fwd(*conv_args, f"conv_fwd{l}", carried=_gather_hand_on(landed))
            wts[1] = kernel_layouts(gathered1)
        else:
            o, lse = _flash_fwd(q, k, v, f"flash_fwd{l}")
            cv, pw = _conv_fwd(*conv_args, f"conv_fwd{l}")
        xn, yv, mixb = _outproj_fwd(xl, o, zb, pw, gate, w["wout"], 0, f"outproj_fwd{l}")
        saved.append(dict(x=xl, hb=hb, za=za, zkr=zkr, zb=zb, q=q, k=k, v=v, o=o, lse=lse, cv=cv, pw=pw, y=yv, mixb=mixb, gains=gains))
        xl = xn

    tok_loss, dx = _loss_head(xl, target)
    loss = lax.psum(jnp.sum(tok_loss), ("x", "y", "c"))

    big = [None] * nl
    small = [None] * nl
    shards_of = lambda gs: [g.reshape(4, g.shape[0] // 4, g.shape[1]) for g in gs]
    pair_sums = lambda l, parts, theirs: [_pair_sum(p, t, cidx, f"pair_sum{l}_{e}") for e, (p, t) in enumerate(zip(parts, theirs))]
    chip_sums = lambda l, sums, landed: [_chip_sum(sm, ld, jc, f"chip_sum{l}_{e}") for e, (sm, ld) in enumerate(zip(sums, landed))]
    halves = [None] * nl
    for l in reversed(range(nl)):
        sv, w = saved[l], wts[l]
        shift, scale, gate = mod_me[l, 0], mod_me[l, 1], mod_me[l, 2]
        out_args = (dx, sv["y"], gate, w["wout"], sv["o"], sv["zb"], sv["pw"], 0, f"outproj_bwd{l}")
        attn_args = (sv["q"], sv["k"], sv["v"])
        if l == 0:
            parts = shards_of(big[1])
            (dgate, dyb, do, delta, dzb, dpw), theirs = _outproj_bwd(*out_args, carried=_pair_exchange(parts))
            sums = pair_sums(1, parts, theirs)
            (dq, dk, dv), landed = _flash_bwd(*attn_args, do, sv["lse"], delta, f"flash_bwd{l}", carried=_chip_scatter(sums))
            halves[1] = chip_sums(1, sums, landed)
        else:
            dgate, dyb, do, delta, dzb, dpw = _outproj_bwd(*out_args)
            dq, dk, dv = _flash_bwd(*attn_args, do, sv["lse"], delta, f"flash_bwd{l}")
        g_out = _grad_tn(sv["mixb"], dyb, f"grad_w_out{l}")
        dza, dzkr, g_q, g_kv, dgql, dgkvl, dgq, dgk = _mla_prep_bwd(
            dq, dk, dv, sv["za"], sv["zkr"], cos, sin, w["wq"], w["wkv"], *sv["gains"], 0, f"mla_prep_bwd{l}")
        dcv, act, dbpw, dlng, dlnb = _pointwise_bwd(dpw, sv["cv"], row(conv_ln_g, l), row(conv_ln_b, l), w["wpw"], 0, f"pointwise_bwd{l}")
        g_pw = _grad_tn(act, dpw, f"grad_w_pw{l}")
        dzb, g_dw, ddwb, dglub = _conv_bwd(dcv, sv["zb"], dzb, row(glu_b, l), w["dw"], 0, f"conv_bwd{l}")
        dx, dshift, dgg = _inproj_bwd(dza, dzkr, dzb, sv["x"], dx, row(norm_g, l), scale, shift, w["w_in"], 0, f"inproj_bwd{l}")
        g_in = _grad_w_in(dza, dzkr, dzb, sv["hb"], f"grad_w_in{l}")
        big[l] = [g_in, g_q.reshape(N_HEADS * QL, HP), g_kv.reshape(4 * KVL, 512), g_dw.reshape(4 * HALO, CH), g_pw, g_out]
        small[l] = dict(ada_b=jnp.concatenate([dshift, dgg * row(norm_g, l), dgate], axis=1), norm_g=dgg * (1.0 + scale),
                        q_lat_g=dgql, kv_lat_g=dgkvl, q_norm_g=_undup(dgq), k_norm_g=_undup(dgk), glu_b=dglub, dw_b=ddwb,
                        conv_ln_g=dlng, conv_ln_b=dlnb, b_pw=dbpw)
    grad_x = dx.reshape(x.shape)

    names = [n for n, _ in SMALL]
    mine = _pack_small({n: jnp.concatenate([small[0][n], small[1][n]], axis=0) for n in names})
    gathered = _allgather8(mine, "gather_small")
    weights = dict(ada_b=ada_b, norm_g=norm_g, q_lat_g=q_lat_g, kv_lat_g=kv_lat_g, q_norm_g=q_norm_g, k_norm_g=k_norm_g,
                   glu_b=glu_b, dw_b=dw_b, conv_ln_g=conv_ln_g, conv_ln_b=conv_ln_b, b_pw=b_pw)
    m_small = dict(ada_b=m_ada_b, norm_g=m_norm_g, q_lat_g=m_q_lat_g, kv_lat_g=m_kv_lat_g, q_norm_g=m_q_norm_g, k_norm_g=m_k_norm_g,
                   glu_b=m_glu_b, dw_b=m_dw_b, conv_ln_g=m_conv_ln_g, conv_ln_b=m_conv_ln_b, b_pw=m_b_pw)
    v_small = dict(ada_b=v_ada_b, norm_g=v_norm_g, q_lat_g=v_q_lat_g, kv_lat_g=v_kv_lat_g, q_norm_g=v_q_norm_g, k_norm_g=v_k_norm_g,
                   glu_b=v_glu_b, dw_b=v_dw_b, conv_ln_g=v_conv_ln_g, conv_ln_b=v_conv_ln_b, b_pw=v_b_pw)
    widths = {n: weights[n].shape[1] for n in names}
    v_packed = _pack_small({n: jnp.pad(v_small[n], ((0, 0), (0, dict(SMALL)[n] - widths[n])), constant_values=1.0) for n in names})
    small_out = [_unpack_small(a, widths) for a in _small_update(gathered, _pack_small(weights), _pack_small(m_small), v_packed)]

    ada_rows = gathered.reshape(8, nl, -1)[:, :, :3 * d]
    dmod = lax.dynamic_slice_in_dim(jnp.transpose(ada_rows, (1, 0, 2)), shard * n_ada, n_ada, axis=2)
    ada_out = _ada_update(c_all, dmod, ada_w, m_ada_w, v_ada_w)

    parts = shards_of(big[0])
    sums = pair_sums(0, parts, _run_alone(_pair_exchange(parts), "pair_exchange0"))
    halves[0] = chip_sums(0, sums, _run_alone(_chip_scatter(sums), "chip_scatter0"))
    full = _run_alone(_pair_complete(halves[0] + halves[1]), "pair_complete")
    per_layer = [full[l * 6:(l + 1) * 6] for l in range(nl)]

    def natural_q(g):
        return jnp.transpose(_undup(g.reshape(2, QL, HP)), (1, 0, 2)).reshape(QL, 2 * QK)

    grads = [[per_layer[l][0], natural_q(per_layer[l][1]), per_layer[l][2], per_layer[l][3][:CONV_K], per_layer[l][4], per_layer[l][5]]
             for l in range(nl)]
    sharded = (("w_in", tr(w_in), tr(m_w_in), tr(v_w_in)), ("w_q_up", w_q_up, m_w_q_up, v_w_q_up),
               ("w_kv_up", w_kv_up, m_w_kv_up, v_w_kv_up), ("dw_w", dw_w, m_dw_w, v_dw_w),
               ("w_pw", w_pw, m_w_pw, v_w_pw), ("w_out", w_out, m_w_out, v_w_out))
    big_out = {name: _adam_update(w, grads[0][e], grads[1][e], m, v, f"adam_{name}") for e, (name, w, m, v) in enumerate(sharded)}
    big_out["w_in"] = [tr(a) for a in big_out["w_in"]]

    order = ["ada_w", "ada_b", "norm_g", "w_in", "q_lat_g", "w_q_up", "kv_lat_g", "w_kv_up", "q_norm_g", "k_norm_g", "glu_b",
             "dw_w", "dw_b", "conv_ln_g", "conv_ln_b", "w_pw", "b_pw", "w_out"]

    def leaf(kind, name):
        if name == "ada_w":
            return ada_out[kind]
        if name in big_out:
            return big_out[name][kind]
        return small_out[kind][name]

    return (loss, grad_x, *[leaf(kind, name) for kind in range(4) for name in order])
```

```python
import functools
import math

import jax
import jax.numpy as jnp
from jax import lax
from jax.experimental import pallas as pl
from jax.experimental.pallas import tpu as pltpu

F32, BF16 = jnp.float32, jnp.bfloat16
MESH = pl.DeviceIdType.MESH
ANY = pl.BlockSpec(memory_space=pl.ANY)

N_HEADS, NOPE, ROPE, VD = 8, 128, 64, 128
QK = NOPE + ROPE
QL, KVL = 512, 256
HP = 256
CONV_K, HALO = 31, 32
ROPE_THETA = 10000.0
EPS = 1e-6
ADAM_LR, ADAM_B1, ADAM_B2, ADAM_EPS, ADAM_WD, ADAM_STEP = 0.001, 0.9, 0.999, 1e-08, 0.01, 10
V7X_VMEM_LIMIT = 56 * 1024 * 1024

NT = (((1,), (1,)), ((), ()))
TN = (((0,), (0,)), ((), ()))
NN = (((1,), (0,)), ((), ()))


def _dot(a, b, dims=NN):
    return lax.dot_general(a, b, dims, preferred_element_type=F32)


def _params(*sem):
    return pltpu.CompilerParams(dimension_semantics=sem or None, vmem_limit_bytes=V7X_VMEM_LIMIT)


def _sigmoid(x):
    return 1.0 / (1.0 + jnp.exp(-x))


def _sum0(x):
    return jnp.sum(x, axis=0, keepdims=True)


def _sum1(x):
    return jnp.sum(x, axis=1, keepdims=True)


def _row(n):
    return pl.BlockSpec((1, n), lambda *_: (0, 0))


def _place():
    x, y, c = lax.axis_index("x"), lax.axis_index("y"), lax.axis_index("c")
    chips = [(1 - x, y), (x, 1 - y), (1 - x, 1 - y)]
    return x, y, c, chips


def _allgather8(v, name):
    r, n = v.shape

    def body(v_ref, out_ref, send_sems, recv_sems, local_sem):
        x, y, c, chips = _place()
        me, sibling = (x, y, c), (x, y, 1 - c)

        def slot(px, py, pc):
            return out_ref.at[4 * px + 2 * py + pc]

        def copy(k, block, to, src=None):
            return pltpu.make_async_remote_copy(
                src_ref=slot(*block) if src is None else src, dst_ref=slot(*block),
                send_sem=send_sems.at[k], recv_sem=recv_sems.at[k], device_id=to, device_id_type=MESH)

        mine = pltpu.make_async_copy(v_ref, slot(*me), local_sem)
        mine.start()
        first = [copy(0, me, sibling, src=v_ref)]
        first += [copy(1 + j, me, (*chip, c), src=v_ref) for j, chip in enumerate(chips)]
        for cp in first:
            cp.start()
        passed = [copy(4 + j, (*chip, c), sibling) for j, chip in enumerate(chips)]
        for j, chip in enumerate(chips):
            copy(1 + j, (*chip, c), me).wait_recv()
            passed[j].start()
        copy(0, sibling, me).wait_recv()
        for j, chip in enumerate(chips):
            copy(4 + j, (*chip, 1 - c), me).wait_recv()
        for cp in first + passed:
            cp.wait_send()
        mine.wait()

    return pl.pallas_call(
        body, name=name, out_shape=jax.ShapeDtypeStruct((8, r, n), v.dtype),
        in_specs=[pl.BlockSpec(memory_space=pltpu.VMEM)], out_specs=pl.BlockSpec(memory_space=pltpu.VMEM),
        scratch_shapes=[pltpu.SemaphoreType.DMA((7,)), pltpu.SemaphoreType.DMA((7,)), pltpu.SemaphoreType.DMA],
    )(v)


class _Carried:
    def __init__(self, operands, results, n_sems, start, finish, aliases=None):
        self.operands, self.results, self.n_sems = operands, results, n_sems
        self.start, self.finish, self.aliases = start, finish, aliases or {}


def _run_alone(carried, name):
    k = len(carried.operands)

    def body(*refs):
        args = (refs[:k], refs[k:k + len(carried.results)], refs[-2], refs[-1])
        carried.start(*args)
        carried.finish(*args)

    outs = pl.pallas_call(
        body, name=name, out_shape=carried.results, in_specs=[ANY] * k, out_specs=[ANY] * len(carried.results),
        input_output_aliases=carried.aliases,
        scratch_shapes=[pltpu.SemaphoreType.DMA((carried.n_sems,)), pltpu.SemaphoreType.DMA((carried.n_sems,))],
    )(*carried.operands)
    return list(outs)


def _call(body, operands, *, name, grid, in_specs, out_specs, out_shape, scratch=(), aliases=None, carried=None):
    params = _params(*(["arbitrary"] * len(grid)))
    n_in, n_out = len(in_specs), len(out_shape)
    if carried is None:
        return pl.pallas_call(body, name=name, grid=grid, in_specs=in_specs, out_specs=out_specs, out_shape=out_shape,
                              scratch_shapes=list(scratch), input_output_aliases=aliases or {}, compiler_params=params)(*operands)
    k_in, k_out = len(carried.operands), len(carried.results)

    def wrapped(*refs):
        ins, outs = refs[:n_in], refs[n_in + k_in:n_in + k_in + n_out]
        comm = (refs[n_in:n_in + k_in], refs[n_in + k_in + n_out:n_in + k_in + n_out + k_out], refs[-2], refs[-1])
        steps = [pl.program_id(a) for a in range(len(grid))]
        first = functools.reduce(jnp.logical_and, [s == 0 for s in steps])
        last = functools.reduce(jnp.logical_and, [s == g - 1 for s, g in zip(steps, grid)])

        @pl.when(first)
        def _():
            carried.start(*comm)

        body(*ins, *outs, *refs[n_in + k_in + n_out + k_out:-2])

        @pl.when(last)
        def _():
            carried.finish(*comm)

    both = dict(aliases or {})
    both.update({n_in + i: n_out + o for i, o in carried.aliases.items()})
    res = pl.pallas_call(
        wrapped, name=name, grid=grid, in_specs=list(in_specs) + [ANY] * k_in, out_specs=list(out_specs) + [ANY] * k_out,
        out_shape=list(out_shape) + list(carried.results), input_output_aliases=both, compiler_params=params,
        scratch_shapes=list(scratch) + [pltpu.SemaphoreType.DMA((carried.n_sems,)), pltpu.SemaphoreType.DMA((carried.n_sems,))],
    )(*operands, *carried.operands)
    return list(res[:n_out]), list(res[n_out:])


def _gather_start(shards):
    ne = len(shards)
    per = 4

    def copies(srcs, dsts, send_sems, recv_sems):
        x, y, c, chips = _place()
        jme = 2 * x + y
        out = []
        for e in range(ne):
            half = srcs[e].shape[2] // 2
            own = pl.ds(pl.multiple_of(c * half, 128), half)
            for k, chip in enumerate(chips):
                out.append(pltpu.make_async_remote_copy(
                    src_ref=srcs[e].at[:, :, own], dst_ref=dsts[e].at[:, jme, :, own], send_sem=send_sems.at[per * e + k],
                    recv_sem=recv_sems.at[per * e + k], device_id=(*chip, c), device_id_type=MESH))
            out.append(pltpu.make_async_remote_copy(
                src_ref=srcs[e], dst_ref=dsts[e].at[:, jme], send_sem=send_sems.at[per * e + 3],
                recv_sem=recv_sems.at[per * e + 3], device_id=(x, y, 1 - c), device_id_type=MESH))
        return out

    def start(*a):
        for cp in copies(*a):
            cp.start()

    def finish(*a):
        for cp in copies(*a):
            cp.wait()

    results = [jax.ShapeDtypeStruct((s.shape[0], 4) + s.shape[1:], s.dtype) for s in shards]
    return _Carried(list(shards), results, per * ne, start, finish)


def _gather_hand_on(bufs):
    ne = len(bufs)

    def copy(e, k, dsts, send_sems, recv_sems, mine):
        x, y, c, chips = _place()
        px, py = chips[k]
        half = dsts[e].shape[3] // 2
        cols = pl.ds(pl.multiple_of((c if mine else 1 - c) * half, 128), half)
        part = dsts[e].at[:, 2 * px + py, :, cols]
        return pltpu.make_async_remote_copy(src_ref=part, dst_ref=part, send_sem=send_sems.at[3 * e + k],
                                            recv_sem=recv_sems.at[3 * e + k], device_id=(x, y, 1 - c), device_id_type=MESH)

    def start(srcs, dsts, send_sems, recv_sems):
        for e in range(ne):
            for k in range(3):
                copy(e, k, dsts, send_sems, recv_sems, True).start()

    def finish(srcs, dsts, send_sems, recv_sems):
        for e in range(ne):
            for k in range(3):
                copy(e, k, dsts, send_sems, recv_sems, True).wait_send()
                copy(e, k, dsts, send_sems, recv_sems, False).wait_recv()

    results = [jax.ShapeDtypeStruct(b.shape, b.dtype) for b in bufs]
    return _Carried(list(bufs), results, 3 * ne, start, finish, aliases={e: e for e in range(ne)})


def _pair_exchange(parts):
    ne = len(parts)

    def copies(srcs, dsts, send_sems, recv_sems):
        x, y, c, _ = _place()
        out = []
        for e in range(ne):
            half = srcs[e].shape[2] // 2
            theirs = pl.ds(pl.multiple_of((1 - c) * half, 128), half)
            out.append(pltpu.make_async_remote_copy(
                src_ref=srcs[e].at[:, :, theirs], dst_ref=dsts[e], send_sem=send_sems.at[e],
                recv_sem=recv_sems.at[e], device_id=(x, y, 1 - c), device_id_type=MESH))
        return out

    def start(*a):
        for cp in copies(*a):
            cp.start()

    def finish(*a):
        for cp in copies(*a):
            cp.wait()

    results = [jax.ShapeDtypeStruct(p.shape[:2] + (p.shape[2] // 2,), p.dtype) for p in parts]
    return _Carried(list(parts), results, ne, start, finish)


def _chip_scatter(sums):
    ne = len(sums)

    def copies(srcs, dsts, send_sems, recv_sems):
        x, y, c, chips = _place()
        return [pltpu.make_async_remote_copy(
                    src_ref=srcs[e].at[2 * px + py], dst_ref=dsts[e].at[k], send_sem=send_sems.at[3 * e + k],
                    recv_sem=recv_sems.at[3 * e + k], device_id=(px, py, c), device_id_type=MESH)
                for e in range(ne) for k, (px, py) in enumerate(chips)]

    def start(*a):
        for cp in copies(*a):
            cp.start()

    def finish(*a):
        for cp in copies(*a):
            cp.wait()

    results = [jax.ShapeDtypeStruct((3,) + s.shape[1:], s.dtype) for s in sums]
    return _Carried(list(sums), results, 3 * ne, start, finish)


def _pair_complete(grads):
    ne = len(grads)

    def copy(e, dsts, send_sems, recv_sems, mine):
        x, y, c, _ = _place()
        half = dsts[e].shape[1] // 2
        cols = pl.ds(pl.multiple_of((c if mine else 1 - c) * half, 128), half)
        return pltpu.make_async_remote_copy(
            src_ref=dsts[e].at[:, cols], dst_ref=dsts[e].at[:, cols], send_sem=send_sems.at[e],
            recv_sem=recv_sems.at[e], device_id=(x, y, 1 - c), device_id_type=MESH)

    def start(srcs, dsts, send_sems, recv_sems):
        for e in range(ne):
            copy(e, dsts, send_sems, recv_sems, True).start()

    def finish(srcs, dsts, send_sems, recv_sems):
        for e in range(ne):
            copy(e, dsts, send_sems, recv_sems, True).wait_send()
            copy(e, dsts, send_sems, recv_sems, False).wait_recv()

    results = [jax.ShapeDtypeStruct(g.shape, g.dtype) for g in grads]
    return _Carried(list(grads), results, ne, start, finish, aliases={e: e for e in range(ne)})


def _pair_sum(part, theirs, cidx, name):
    _, r, n = part.shape
    half = n // 2

    def body(c_ref, p_ref, t_ref, o_ref):
        o_ref[...] = (p_ref[...] + t_ref[...]).astype(BF16)

    gs = pltpu.PrefetchScalarGridSpec(
        num_scalar_prefetch=1, grid=(4,),
        in_specs=[pl.BlockSpec((1, r, half), lambda j, c: (j, 0, c[0])),
                  pl.BlockSpec((1, r, half), lambda j, c: (j, 0, 0))],
        out_specs=pl.BlockSpec((1, r, half), lambda j, c: (j, 0, 0)))
    return pl.pallas_call(body, name=name, grid_spec=gs, out_shape=jax.ShapeDtypeStruct((4, r, half), BF16),
                          compiler_params=_params("arbitrary"))(cidx, part, theirs)


def _chip_sum(sums, landed, jc, name):
    _, r, half = sums.shape

    def body(jc_ref, s_ref, l_ref, o_ref):
        acc = s_ref[0].astype(F32)
        for k in range(3):
            acc = acc + l_ref[k].astype(F32)
        o_ref[...] = acc

    gs = pltpu.PrefetchScalarGridSpec(
        num_scalar_prefetch=1, grid=(1,),
        in_specs=[pl.BlockSpec((1, r, half), lambda i, jc: (jc[0], 0, 0)),
                  pl.BlockSpec((3, r, half), lambda i, jc: (0, 0, 0))],
        out_specs=pl.BlockSpec((r, half), lambda i, jc: (0, jc[1])))
    return pl.pallas_call(body, name=name, grid_spec=gs, out_shape=jax.ShapeDtypeStruct((r, 2 * half), F32),
                          compiler_params=_params("arbitrary"))(jc, sums, landed)


def _rope_tables(pos):
    s = pos.shape[0]
    lane = jnp.arange(128)
    inv = 1.0 / (ROPE_THETA ** ((2 * (lane % 32)).astype(F32) / ROPE))
    keep = (lane < 64).astype(F32)
    sign = jnp.where(lane < 32, -1.0, 1.0).astype(F32) * keep
    consts = jnp.stack([inv.astype(F32), keep, sign])[:, None, :]

    def body(p_ref, k_ref, c_ref, s_ref):
        ang = p_ref[...].astype(F32) * k_ref[0]
        c_ref[...] = jnp.cos(ang) * k_ref[1]
        s_ref[...] = jnp.sin(ang) * k_ref[2]

    tm = min(s, 1024)
    return pl.pallas_call(
        body, name="rope_tables", grid=(s // tm,),
        in_specs=[pl.BlockSpec((tm, 1), lambda i: (i, 0)), pl.BlockSpec((3, 1, 128), lambda i: (0, 0, 0))],
        out_specs=[pl.BlockSpec((tm, 128), lambda i: (i, 0))] * 2,
        out_shape=[jax.ShapeDtypeStruct((s, 128), F32)] * 2, compiler_params=_params("arbitrary"),
    )(pos, consts)


def _modulation(c_all, ada_w, ada_b_shard):
    nl, d, n = ada_w.shape
    tn = 512

    def body(c_ref, w_ref, b_ref, o_ref):
        cv = c_ref[...]
        act = (cv * _sigmoid(cv)).astype(BF16)
        o_ref[...] = _dot(act, w_ref[...].astype(BF16)) + b_ref[...]

    return pl.pallas_call(
        body, name="modulation", grid=(nl, n // tn),
        in_specs=[pl.BlockSpec((8, d), lambda l, j: (0, 0)), pl.BlockSpec((None, d, tn), lambda l, j: (l, 0, j)),
                  pl.BlockSpec((None, 1, tn), lambda l, j: (l, 0, j))],
        out_specs=pl.BlockSpec((None, 8, tn), lambda l, j: (l, 0, j)),
        out_shape=jax.ShapeDtypeStruct((nl, 8, n), F32), compiler_params=_params("arbitrary", "arbitrary"),
    )(c_all, ada_w, ada_b_shard)


def _loss_head(xf, target):
    s, d = xf.shape
    tm = min(s, 512)

    def body(x_ref, t_ref, l_ref, dx_ref):
        err = x_ref[...] - t_ref[...]
        l_ref[...] = 0.5 * jnp.mean(err * err, axis=1, keepdims=True)
        dx_ref[...] = err * (1.0 / d)

    return pl.pallas_call(
        body, name="loss_head", grid=(s // tm,),
        in_specs=[pl.BlockSpec((tm, d), lambda i: (i, 0))] * 2,
        out_specs=[pl.BlockSpec((tm, 1), lambda i: (i, 0)), pl.BlockSpec((tm, d), lambda i: (i, 0))],
        out_shape=[jax.ShapeDtypeStruct((s, 1), F32), jax.ShapeDtypeStruct((s, d), F32)],
        compiler_params=_params("arbitrary"),
    )(xf, target)


W_ROWS = 4992
W_PIECES = ((0, 0, 832), (832, 768, 64), (896, 1856, 2048), (2944, 832, 1024), (3968, 3904, 1024))


def _load_w_in(w_hbm, w_vmem, sems):
    cps = [pltpu.make_async_copy(w_hbm.at[pl.ds(src, n)], w_vmem.at[pl.ds(dst, n)], sems.at[i])
           for i, (dst, src, n) in enumerate(W_PIECES)]
    for cp in cps:
        cp.start()
    for cp in cps:
        cp.wait()


def _inproj_fwd(x, g, scale, shift, w_int, layer, name, carried=None):
    s, d = x.shape
    tm = min(s, 256)

    def body(x_ref, g_ref, sc_ref, sh_ref, w_hbm, hb_ref, za_ref, zkr_ref, zb_ref, w_vmem, sems):
        @pl.when(pl.program_id(0) == 0)
        def _():
            _load_w_in(w_hbm.at[layer], w_vmem, sems)

        xv = x_ref[...]
        rstd = lax.rsqrt(jnp.mean(xv * xv, axis=1, keepdims=True) + EPS)
        h = (xv * rstd) * g_ref[...] * (1.0 + sc_ref[...]) + sh_ref[...]
        hb = h.astype(BF16)
        hb_ref[...] = hb
        za_ref[...] = _dot(hb, w_vmem[0:768], NT)
        zkr_ref[...] = _dot(hb, w_vmem[768:896], NT)
        zb_ref[...] = _dot(hb, w_vmem[896:W_ROWS], NT)

    return _call(
        body, (x, g, scale, shift, w_int), name=name, grid=(s // tm,), carried=carried,
        in_specs=[pl.BlockSpec((tm, d), lambda i: (i, 0)), _row(d), _row(d), _row(d), ANY],
        out_specs=[pl.BlockSpec((tm, d), lambda i: (i, 0)), pl.BlockSpec((tm, 768), lambda i: (i, 0)),
                   pl.BlockSpec((tm, 128), lambda i: (i, 0)), pl.BlockSpec((tm, 4096), lambda i: (i, 0))],
        out_shape=[jax.ShapeDtypeStruct((s, d), BF16), jax.ShapeDtypeStruct((s, 768), F32),
                   jax.ShapeDtypeStruct((s, 128), F32), jax.ShapeDtypeStruct((s, 4096), F32)],
        scratch=[pltpu.VMEM((W_ROWS, d), BF16), pltpu.SemaphoreType.DMA((len(W_PIECES),))])


def _rope(yv, cos, sin):
    return yv * cos + pltpu.roll(yv, 32, axis=1) * sin


def _mla_prep_fwd(za, zkr, cos, sin, wq, wkv, gql, gkvl, gq2, gk2, layer, name):
    s = za.shape[0]
    tm = min(s, 256)

    def body(za_ref, zkr_ref, cos_ref, sin_ref, wq_ref, wkv_ref, gql_ref, gkvl_ref, gq_ref, gk_ref, q_ref, k_ref, v_ref):
        zq, zkv = za_ref[:, 0:QL], za_ref[:, QL:QL + KVL]
        qn = (zq * lax.rsqrt(jnp.mean(zq * zq, axis=1, keepdims=True) + EPS) * gql_ref[...]).astype(BF16)
        kvn = (zkv * lax.rsqrt(jnp.mean(zkv * zkv, axis=1, keepdims=True) + EPS) * gkvl_ref[...]).astype(BF16)
        kr = zkr_ref[...]
        kr_ss = 0.5 * _sum1(kr * kr)
        cos, sin = cos_ref[...], sin_ref[...]
        gq, gk = gq_ref[...] * SCORE_SCALE, gk_ref[...]
        for h in range(N_HEADS):
            qr = _dot(qn, wq_ref[h])
            n, yv = qr[:, :NOPE], qr[:, NOPE:]
            rstd = lax.rsqrt((_sum1(n * n) + 0.5 * _sum1(yv * yv)) * (1.0 / QK) + EPS)
            q_ref[h, :, 0:NOPE] = (n * rstd * gq[:, :NOPE]).astype(BF16)
            q_ref[h, :, NOPE:HP] = _rope(yv * rstd * gq[:, NOPE:], cos, sin).astype(BF16)
            col = (h % 2) * 256
            kvr = _dot(kvn, wkv_ref[h // 2, :, col:col + 256])
            kn, vv = kvr[:, :NOPE], kvr[:, NOPE:]
            rstd = lax.rsqrt((_sum1(kn * kn) + kr_ss) * (1.0 / QK) + EPS)
            k_ref[h, :, 0:NOPE] = (kn * rstd * gk[:, :NOPE]).astype(BF16)
            k_ref[h, :, NOPE:HP] = _rope(kr * rstd * gk[:, NOPE:], cos, sin).astype(BF16)
            v_ref[h] = vv.astype(BF16)

    tile = lambda n: pl.BlockSpec((tm, n), lambda i: (i, 0))
    return pl.pallas_call(
        body, name=name, grid=(s // tm,),
        in_specs=[tile(768), tile(128), tile(128), tile(128),
                  pl.BlockSpec((None, N_HEADS, QL, HP), lambda i: (layer, 0, 0, 0)),
                  pl.BlockSpec((None, 4, KVL, 512), lambda i: (layer, 0, 0, 0)),
                  _row(QL), _row(KVL), _row(HP), _row(HP)],
        out_specs=[pl.BlockSpec((N_HEADS, tm, HP), lambda i: (0, i, 0)), pl.BlockSpec((N_HEADS, tm, HP), lambda i: (0, i, 0)),
                   pl.BlockSpec((N_HEADS, tm, VD), lambda i: (0, i, 0))],
        out_shape=[jax.ShapeDtypeStruct((N_HEADS, s, HP), BF16), jax.ShapeDtypeStruct((N_HEADS, s, HP), BF16),
                   jax.ShapeDtypeStruct((N_HEADS, s, VD), BF16)],
        compiler_params=_params("arbitrary"),
    )(za, zkr, cos, sin, wq, wkv, gql, gkvl, gq2, gk2)


SCORE_SCALE = 1.0 / math.sqrt(QK)
MASKED = -1e30


def _flash_fwd(q, k, v, name, carried=None):
    s = q.shape[1]
    t = min(s, 512)

    def body(q_ref, k_ref, v_ref, o_ref, lse_ref):
        i = pl.program_id(1)
        qb = q_ref[...]
        row = lax.broadcasted_iota(jnp.int32, (t, t), 0)
        col = lax.broadcasted_iota(jnp.int32, (t, t), 1)

        def block(j):
            return pl.ds(pl.multiple_of(j * t, t), t)

        def scores(j):
            return _dot(qb, k_ref[block(j), :], NT)

        def update(j, sc, m, l, acc, diagonal):
            if diagonal:
                sc = jnp.where(col <= row, sc, MASKED)
            m_new = jnp.maximum(m, jnp.max(sc, axis=1, keepdims=True))
            p = jnp.exp(sc - m_new)
            alpha = jnp.exp(m - m_new)
            return m_new, alpha * l + _sum1(p), alpha * acc + _dot(p.astype(BF16), v_ref[block(j), :])

        init = (jnp.full((t, 1), MASKED, F32), jnp.zeros((t, 1), F32), jnp.zeros((t, VD), F32))
        carry = lax.fori_loop(0, i, lambda j, cr: update(j, scores(j), *cr, False), init)
        m, l, acc = update(i, scores(i), *carry, True)
        o_ref[...] = acc / l
        lse_ref[...] = m + jnp.log(l)

    return _call(
        body, (q, k, v), name=name, grid=(N_HEADS, s // t), carried=carried,
        in_specs=[pl.BlockSpec((None, t, HP), lambda h, i: (h, i, 0)), pl.BlockSpec((None, s, HP), lambda h, i: (h, 0, 0)),
                  pl.BlockSpec((None, s, VD), lambda h, i: (h, 0, 0))],
        out_specs=[pl.BlockSpec((t, VD), lambda h, i: (i, h)), pl.BlockSpec((None, t, 1), lambda h, i: (h, i, 0))],
        out_shape=[jax.ShapeDtypeStruct((s, N_HEADS * VD), F32), jax.ShapeDtypeStruct((N_HEADS, s, 1), F32)])


CH, RC = 256, 64


PH_ROWS_LESS = 8


def _glu(val, gate, bias):
    c = val.shape[1]
    return (val + bias[:, :c]) * _sigmoid(gate + bias[:, c:])


def _make_phases(buf, phases, cc):
    rows = buf.shape[0] - PH_ROWS_LESS
    for b in range(1, 8):
        phases[b - 1] = buf[pl.ds(b, rows), cc:cc + CH]


def _window(buf, phases, cc, shift, r0):
    a, b = divmod(shift, 8)
    if b == 0:
        return buf[r0 + 8 * a:r0 + 8 * a + RC, cc:cc + CH]
    return phases[b - 1, r0 + 8 * a:r0 + 8 * a + RC, :]


def _conv_fwd(zb, glu_b, dw, dwb, lng, lnb, wpw, bpw, layer, name, carried=None):
    s = zb.shape[0]
    dc = dwb.shape[1]
    tm = min(s, 256)
    hb = tm // HALO

    def body(val_ref, gate_ref, valh_ref, gateh_ref, glub_ref, dw_ref, dwb_ref, lng_ref, lnb_ref, wpw_ref, bpw_ref,
             cv_ref, pw_ref, ubuf, uph):
        i = pl.program_id(0)
        bias = glub_ref[...]
        ubuf[HALO:, :] = _glu(val_ref[...], gate_ref[...], bias)
        uh = _glu(valh_ref[...], gateh_ref[...], bias)
        ubuf[0:HALO, :] = jnp.where(i > 0, uh, 0.0)
        for cc in range(0, dc, CH):
            _make_phases(ubuf, uph, cc)
            for r0 in range(0, tm, RC):
                acc = jnp.zeros((RC, CH), F32)
                for j in range(CONV_K):
                    acc = acc + _window(ubuf, uph, cc, HALO - (CONV_K - 1) + j, r0) * dw_ref[j:j + 1, cc:cc + CH]
                cv_ref[r0:r0 + RC, cc:cc + CH] = acc + dwb_ref[:, cc:cc + CH]
        cv = cv_ref[...]
        dv = cv - jnp.mean(cv, axis=1, keepdims=True)
        yl = dv * lax.rsqrt(jnp.mean(dv * dv, axis=1, keepdims=True) + EPS) * lng_ref[...] + lnb_ref[...]
        act = (yl * _sigmoid(yl)).astype(BF16)
        pw_ref[...] = _dot(act, wpw_ref[...]) + bpw_ref[...]

    return _call(
        body, (zb, zb, zb, zb, glu_b, dw, dwb, lng, lnb, wpw, bpw), name=name, grid=(s // tm,), carried=carried,
        in_specs=[pl.BlockSpec((tm, dc), lambda i: (i, 0)), pl.BlockSpec((tm, dc), lambda i: (i, 1)),
                  pl.BlockSpec((HALO, dc), lambda i: (jnp.maximum(i * hb - 1, 0), 0)),
                  pl.BlockSpec((HALO, dc), lambda i: (jnp.maximum(i * hb - 1, 0), 1)),
                  _row(2 * dc), pl.BlockSpec((None, HALO, dc), lambda i: (layer, 0, 0)), _row(dc), _row(dc), _row(dc),
                  pl.BlockSpec((None, dc, dc), lambda i: (layer, 0, 0)), _row(dc)],
        out_specs=[pl.BlockSpec((tm, dc), lambda i: (i, 0))] * 2,
        out_shape=[jax.ShapeDtypeStruct((s, dc), F32)] * 2,
        scratch=[pltpu.VMEM((tm + HALO, dc), F32), pltpu.VMEM((7, tm + HALO - PH_ROWS_LESS, CH), F32)])


def _silu_parts(z):
    sg = _sigmoid(z)
    return z * sg, sg * (1.0 + z * (1.0 - sg))


def _outproj_fwd(x, o, zb, pw, gate, wout, layer, name):
    s, d = x.shape
    dm = o.shape[1]
    tm = min(s, 256)

    def body(x_ref, o_ref, mg_ref, cg_ref, pw_ref, gate_ref, w_ref, xn_ref, y_ref, mix_ref):
        mg, cg = mg_ref[...], cg_ref[...]
        mix_ref[:, 0:dm] = (o_ref[...] * (mg * _sigmoid(mg))).astype(BF16)
        mix_ref[:, dm:] = (pw_ref[...] * (cg * _sigmoid(cg))).astype(BF16)
        yv = _dot(mix_ref[...], w_ref[...])
        y_ref[...] = yv
        xn_ref[...] = x_ref[...] + gate_ref[...] * yv

    tile = lambda n, j=0: pl.BlockSpec((tm, n), lambda i: (i, j))
    return pl.pallas_call(
        body, name=name, grid=(s // tm,),
        in_specs=[tile(d), tile(dm), tile(dm, 2), tile(dm, 3), tile(dm), _row(d),
                  pl.BlockSpec((None, 2 * dm, d), lambda i: (layer, 0, 0))],
        out_specs=[tile(d), tile(d), tile(2 * dm)],
        out_shape=[jax.ShapeDtypeStruct((s, d), F32), jax.ShapeDtypeStruct((s, d), F32), jax.ShapeDtypeStruct((s, 2 * dm), BF16)],
        compiler_params=_params("arbitrary"),
    )(x, o, zb, zb, pw, gate, wout)


def _grad_tn(a, b, name):
    s, n = a.shape
    m = b.shape[1]
    tn, ts = min(n, 1024), min(s, 512)

    def body(a_ref, b_ref, o_ref):
        @pl.when(pl.program_id(1) == 0)
        def _():
            o_ref[...] = jnp.zeros_like(o_ref)

        o_ref[...] += _dot(a_ref[...].astype(BF16), b_ref[...].astype(BF16), TN)

    return pl.pallas_call(
        body, name=name, grid=(n // tn, s // ts),
        in_specs=[pl.BlockSpec((ts, tn), lambda r, t: (t, r)), pl.BlockSpec((ts, m), lambda r, t: (t, 0))],
        out_specs=pl.BlockSpec((tn, m), lambda r, t: (r, 0)),
        out_shape=jax.ShapeDtypeStruct((n, m), F32), compiler_params=_params("arbitrary", "arbitrary"),
    )(a, b)


def _outproj_bwd(dxo, y, gate, wout, o, zb, pw, layer, name, carried=None):
    s, d = dxo.shape
    dm = o.shape[1]
    tm = min(s, 256)

    def body(dx_ref, y_ref, gate_ref, w_ref, o_ref, mg_ref, cg_ref, pw_ref,
             dgate_ref, dy_ref, do_ref, delta_ref, dzb_ref, dpw_ref):
        @pl.when(pl.program_id(0) == 0)
        def _():
            dgate_ref[...] = jnp.zeros_like(dgate_ref)

        dx = dx_ref[...]
        dgate_ref[...] += _sum0(dx * y_ref[...])
        dyb = (dx * gate_ref[...]).astype(BF16)
        dy_ref[...] = dyb
        dmix = _dot(dyb, w_ref[...], NT)
        da, db = dmix[:, :dm], dmix[:, dm:]
        ov = o_ref[...]
        silu_m, dsilu_m = _silu_parts(mg_ref[...])
        do = da * silu_m
        do_ref[...] = do.astype(BF16)
        prod = do * ov
        for h in range(N_HEADS):
            delta_ref[h] = _sum1(prod[:, h * VD:(h + 1) * VD])
        dzb_ref[:, 0:dm] = da * ov * dsilu_m
        silu_c, dsilu_c = _silu_parts(cg_ref[...])
        dpw_ref[...] = db * silu_c
        dzb_ref[:, dm:] = db * pw_ref[...] * dsilu_c

    tile = lambda n, j=0: pl.BlockSpec((tm, n), lambda i: (i, j))
    return _call(
        body, (dxo, y, gate, wout, o, zb, zb, pw), name=name, grid=(s // tm,), carried=carried,
        in_specs=[tile(d), tile(d), _row(d), pl.BlockSpec((None, 2 * dm, d), lambda i: (layer, 0, 0)),
                  tile(dm), tile(dm, 2), tile(dm, 3), tile(dm)],
        out_specs=[_row(d), tile(d), tile(dm), pl.BlockSpec((N_HEADS, tm, 1), lambda i: (0, i, 0)), tile(2 * dm, 1), tile(dm)],
        out_shape=[jax.ShapeDtypeStruct((1, d), F32), jax.ShapeDtypeStruct((s, d), BF16), jax.ShapeDtypeStruct((s, dm), BF16),
                   jax.ShapeDtypeStruct((N_HEADS, s, 1), F32), jax.ShapeDtypeStruct((s, 4 * dm), F32),
                   jax.ShapeDtypeStruct((s, dm), F32)])


def _flash_bwd(q, k, v, do, lse, delta, name, carried=None):
    s = q.shape[1]
    t = min(s, 512)
    nq = s // t

    def body(q_ref, k_ref, v_ref, do_ref, lse_ref, delta_ref, dq_ref, dk_ref, dv_ref):
        j = pl.program_id(1)

        @pl.when(j == 0)
        def _():
            dq_ref[...] = jnp.zeros_like(dq_ref)

        kb, vb = k_ref[...], v_ref[...]
        row = lax.broadcasted_iota(jnp.int32, (t, t), 0)
        col = lax.broadcasted_iota(jnp.int32, (t, t), 1)

        def block(i):
            return pl.ds(pl.multiple_of(i * t, t), t)

        def scores(i):
            at = block(i)
            return _dot(q_ref[at, :], kb, NT), _dot(do_ref[at, :], vb, NT)

        def update(i, sc, dp, dk, dv, diagonal):
            at = block(i)
            p = jnp.exp(sc - lse_ref[at, :])
            if diagonal:
                p = jnp.where(col <= row, p, 0.0)
            dv = dv + _dot(p.astype(BF16), do_ref[at, :], TN)
            ds = (p * (dp - delta_ref[at, :])).astype(BF16)
            dq_ref[at, :] += _dot(ds, kb)
            return dk + _dot(ds, q_ref[at, :], TN), dv

        carry = update(j, *scores(j), jnp.zeros((t, HP), F32), jnp.zeros((t, VD), F32), True)
        dk, dv = lax.fori_loop(j + 1, nq, lambda i, cr: update(i, *scores(i), *cr, False), carry)
        dk_ref[...] = dk
        dv_ref[...] = dv

    whole = lambda n: pl.BlockSpec((None, s, n), lambda h, j: (h, 0, 0))
    blk = lambda n: pl.BlockSpec((None, t, n), lambda h, j: (h, j, 0))
    return _call(
        body, (q, k, v, do, lse, delta), name=name, grid=(N_HEADS, nq), carried=carried,
        in_specs=[whole(HP), blk(HP), blk(VD), pl.BlockSpec((s, VD), lambda h, j: (0, h)), whole(1), whole(1)],
        out_specs=[whole(HP), blk(HP), blk(VD)],
        out_shape=[jax.ShapeDtypeStruct((N_HEADS, s, HP), F32), jax.ShapeDtypeStruct((N_HEADS, s, HP), F32),
                   jax.ShapeDtypeStruct((N_HEADS, s, VD), F32)])


def _mla_prep_bwd(dq, dk, dv, za, zkr, cos, sin, wq, wkv, gql, gkvl, gq2, gk2, layer, name):
    s = za.shape[0]
    tm = min(s, 512)

    def norm_bwd(n, yv, rstd, gain, d_n_out, d_y_out):
        tn_, ty = n * rstd, yv * rstd
        dgain_n, dgain_y = _sum0(d_n_out * tn_), _sum0(d_y_out * ty)
        dtn, dty = d_n_out * gain[:, :NOPE], d_y_out * gain[:, NOPE:]
        a = (_sum1(dtn * n) + _sum1(dty * yv)) * (rstd * rstd * rstd * (1.0 / QK))
        return rstd * dtn - n * a, rstd * dty - (0.5 * yv) * a, dgain_n, dgain_y

    def rope_bwd(d_out, cos, sin):
        return d_out * cos + pltpu.roll(d_out * sin, 128 - 32, axis=1)

    def latent_bwd(z, gain, dn):
        rstd = lax.rsqrt(jnp.mean(z * z, axis=1, keepdims=True) + EPS)
        zh = z * rstd
        dzh = dn * gain
        return rstd * (dzh - zh * jnp.mean(dzh * zh, axis=1, keepdims=True)), _sum0(dn * zh)

    def body(dq_ref, dk_ref, dv_ref, za_ref, zkr_ref, cos_ref, sin_ref, wq_ref, wkv_ref, gql_ref, gkvl_ref, gq_ref, gk_ref,
             dza_ref, dzkr_ref, gwq_ref, gwkv_ref, dgql_ref, dgkvl_ref, dgq_ref, dgk_ref):
        @pl.when(pl.program_id(0) == 0)
        def _():
            for r in (gwq_ref, gwkv_ref, dgql_ref, dgkvl_ref, dgq_ref, dgk_ref):
                r[...] = jnp.zeros_like(r)

        zq, zkv = za_ref[:, 0:QL], za_ref[:, QL:QL + KVL]
        qf = zq * lax.rsqrt(jnp.mean(zq * zq, axis=1, keepdims=True) + EPS) * gql_ref[...]
        kvf = zkv * lax.rsqrt(jnp.mean(zkv * zkv, axis=1, keepdims=True) + EPS) * gkvl_ref[...]
        qn, kvn = qf.astype(BF16), kvf.astype(BF16)
        qn_t, kvn_t = qf.T.astype(BF16), kvf.T.astype(BF16)
        kr = zkr_ref[...]
        kr_ss = 0.5 * _sum1(kr * kr)
        cos, sin = cos_ref[...], sin_ref[...]
        gq, gk = gq_ref[...], gk_ref[...]
        dqn = jnp.zeros((tm, QL), F32)
        dkvn = jnp.zeros((tm, KVL), F32)
        dkr = jnp.zeros((tm, 128), F32)
        for h in range(N_HEADS):
            qr = _dot(qn, wq_ref[h])
            n, yv = qr[:, :NOPE], qr[:, NOPE:]
            rstd = lax.rsqrt((_sum1(n * n) + 0.5 * _sum1(yv * yv)) * (1.0 / QK) + EPS)
            dqh = dq_ref[h] * SCORE_SCALE
            dn, dy, dg_n, dg_y = norm_bwd(n, yv, rstd, gq, dqh[:, :NOPE], rope_bwd(dqh[:, NOPE:], cos, sin))
            dgq_ref[:, 0:NOPE] += dg_n
            dgq_ref[:, NOPE:] += dg_y
            dqr = jnp.concatenate([dn, dy], axis=1).astype(BF16)
            gwq_ref[h] += _dot(qn_t, dqr)
            dqn = dqn + _dot(dqr, wq_ref[h], NT)

            col = (h % 2) * 256
            wkv_h = wkv_ref[h // 2, :, col:col + 256]
            kvr = _dot(kvn, wkv_h)
            kn = kvr[:, :NOPE]
            rstd = lax.rsqrt((_sum1(kn * kn) + kr_ss) * (1.0 / QK) + EPS)
            dkh = dk_ref[h]
            dn, dy, dg_n, dg_y = norm_bwd(kn, kr, rstd, gk, dkh[:, :NOPE], rope_bwd(dkh[:, NOPE:], cos, sin))
            dgk_ref[:, 0:NOPE] += dg_n
            dgk_ref[:, NOPE:] += dg_y
            dkr = dkr + dy
            dkvr = jnp.concatenate([dn, dv_ref[h]], axis=1).astype(BF16)
            gwkv_ref[h // 2, :, col:col + 256] += _dot(kvn_t, dkvr)
            dkvn = dkvn + _dot(dkvr, wkv_h, NT)

        dzq, dgql = latent_bwd(zq, gql_ref[...], dqn)
        dzkv, dgkvl = latent_bwd(zkv, gkvl_ref[...], dkvn)
        dgql_ref[...] += dgql
        dgkvl_ref[...] += dgkvl
        dza_ref[:, 0:QL] = dzq
        dza_ref[:, QL:] = dzkv
        lane = lax.broadcasted_iota(jnp.int32, (tm, 128), 1)
        dzkr_ref[...] = jnp.where(lane < ROPE, dkr + pltpu.roll(dkr, 64, axis=1), 0.0)

    tile = lambda n: pl.BlockSpec((tm, n), lambda i: (i, 0))
    heads = lambda n: pl.BlockSpec((N_HEADS, tm, n), lambda i: (0, i, 0))
    return pl.pallas_call(
        body, name=name, grid=(s // tm,),
        in_specs=[heads(HP), heads(HP), heads(VD), tile(768), tile(128), tile(128), tile(128),
                  pl.BlockSpec((None, N_HEADS, QL, HP), lambda i: (layer, 0, 0, 0)),
                  pl.BlockSpec((None, 4, KVL, 512), lambda i: (layer, 0, 0, 0)),
                  _row(QL), _row(KVL), _row(HP), _row(HP)],
        out_specs=[tile(768), tile(128), pl.BlockSpec((N_HEADS, QL, HP), lambda i: (0, 0, 0)),
                   pl.BlockSpec((4, KVL, 512), lambda i: (0, 0, 0)), _row(QL), _row(KVL), _row(HP), _row(HP)],
        out_shape=[jax.ShapeDtypeStruct((s, 768), F32), jax.ShapeDtypeStruct((s, 128), F32),
                   jax.ShapeDtypeStruct((N_HEADS, QL, HP), F32), jax.ShapeDtypeStruct((4, KVL, 512), F32),
                   jax.ShapeDtypeStruct((1, QL), F32), jax.ShapeDtypeStruct((1, KVL), F32),
                   jax.ShapeDtypeStruct((1, HP), F32), jax.ShapeDtypeStruct((1, HP), F32)],
        compiler_params=_params("arbitrary"),
    )(dq, dk, dv, za, zkr, cos, sin, wq, wkv, gql, gkvl, gq2, gk2)


def _pointwise_bwd(dpw, cv, lng, lnb, wpw, layer, name):
    s, dc = cv.shape
    tm = min(s, 256)

    def body(dpw_ref, cv_ref, lng_ref, lnb_ref, w_ref, dcv_ref, act_ref, dbpw_ref, dlng_ref, dlnb_ref):
        @pl.when(pl.program_id(0) == 0)
        def _():
            for r in (dbpw_ref, dlng_ref, dlnb_ref):
                r[...] = jnp.zeros_like(r)

        cv = cv_ref[...]
        dv = cv - jnp.mean(cv, axis=1, keepdims=True)
        rstd = lax.rsqrt(jnp.mean(dv * dv, axis=1, keepdims=True) + EPS)
        xh = dv * rstd
        yl = xh * lng_ref[...] + lnb_ref[...]
        silu, dsilu = _silu_parts(yl)
        act_ref[...] = silu.astype(BF16)
        dpw = dpw_ref[...]
        dbpw_ref[...] += _sum0(dpw)
        dyl = _dot(dpw.astype(BF16), w_ref[...], NT) * dsilu
        dlng_ref[...] += _sum0(dyl * xh)
        dlnb_ref[...] += _sum0(dyl)
        dxh = dyl * lng_ref[...]
        dcv_ref[...] = rstd * (dxh - jnp.mean(dxh, axis=1, keepdims=True) - xh * jnp.mean(dxh * xh, axis=1, keepdims=True))

    tile = pl.BlockSpec((tm, dc), lambda i: (i, 0))
    return pl.pallas_call(
        body, name=name, grid=(s // tm,),
        in_specs=[tile, tile, _row(dc), _row(dc), pl.BlockSpec((None, dc, dc), lambda i: (layer, 0, 0))],
        out_specs=[tile, tile, _row(dc), _row(dc), _row(dc)],
        out_shape=[jax.ShapeDtypeStruct((s, dc), F32), jax.ShapeDtypeStruct((s, dc), BF16)] + [jax.ShapeDtypeStruct((1, dc), F32)] * 3,
        compiler_params=_params("arbitrary"),
    )(dpw, cv, lng, lnb, wpw)


def _conv_bwd(dcv, zb, dzb, glu_b, dw, layer, name):
    s, dc = dcv.shape
    tm = min(s, 256)
    hb = tm // HALO
    last = s // tm - 1

    def body(dcv_ref, dcvn_ref, val_ref, gate_ref, valh_ref, gateh_ref, glub_ref, dw_ref, _, dzb_ref, gdw_ref, ddwb_ref, dglub_ref,
             ubuf, dbuf, gacc, uph, dph):
        i = pl.program_id(0)

        @pl.when(i == 0)
        def _():
            gacc[...] = jnp.zeros_like(gacc)
            ddwb_ref[...] = jnp.zeros_like(ddwb_ref)
            dglub_ref[...] = jnp.zeros_like(dglub_ref)

        bias = glub_ref[...]
        ubuf[HALO:, :] = _glu(val_ref[...], gate_ref[...], bias)
        ubuf[0:HALO, :] = jnp.where(i > 0, _glu(valh_ref[...], gateh_ref[...], bias), 0.0)
        dcv = dcv_ref[...]
        dbuf[0:tm, :] = dcv
        dbuf[tm:, :] = jnp.where(i < last, dcvn_ref[...], 0.0)
        ddwb_ref[...] += _sum0(dcv)
        for cc in range(0, dc, CH):
            _make_phases(ubuf, uph, cc)
            _make_phases(dbuf, dph, cc)
            for r0 in range(0, tm, RC):
                du = jnp.zeros((RC, CH), F32)
                dpiece = dbuf[r0:r0 + RC, cc:cc + CH]
                for j in range(CONV_K):
                    du = du + _window(dbuf, dph, cc, (CONV_K - 1) - j, r0) * dw_ref[j:j + 1, cc:cc + CH]
                    win = _window(ubuf, uph, cc, HALO - (CONV_K - 1) + j, r0)
                    gacc[j, :, cc:cc + CH] += (dpiece * win).reshape(RC // 8, 8, CH).sum(axis=0)
                a = val_ref[r0:r0 + RC, cc:cc + CH] + bias[:, cc:cc + CH]
                sg = _sigmoid(gate_ref[r0:r0 + RC, cc:cc + CH] + bias[:, dc + cc:dc + cc + CH])
                dzb_ref[r0:r0 + RC, cc:cc + CH] = du * sg
                dzb_ref[r0:r0 + RC, dc + cc:dc + cc + CH] = du * a * sg * (1.0 - sg)
        dglub_ref[...] += _sum0(dzb_ref[...])

        @pl.when(i == last)
        def _():
            total = jnp.sum(gacc[...], axis=1)
            for cc in range(0, dc, CH):
                gdw_ref[cc // CH] = total[:, cc:cc + CH]

    return pl.pallas_call(
        body, name=name, grid=(s // tm,),
        in_specs=[pl.BlockSpec((tm, dc), lambda i: (i, 0)),
                  pl.BlockSpec((HALO, dc), lambda i: (jnp.minimum((i + 1) * hb, s // HALO - 1), 0)),
                  pl.BlockSpec((tm, dc), lambda i: (i, 0)), pl.BlockSpec((tm, dc), lambda i: (i, 1)),
                  pl.BlockSpec((HALO, dc), lambda i: (jnp.maximum(i * hb - 1, 0), 0)),
                  pl.BlockSpec((HALO, dc), lambda i: (jnp.maximum(i * hb - 1, 0), 1)),
                  _row(2 * dc), pl.BlockSpec((None, HALO, dc), lambda i: (layer, 0, 0)), ANY],
        out_specs=[pl.BlockSpec((tm, 2 * dc), lambda i: (i, 0)), pl.BlockSpec((4, HALO, CH), lambda i: (0, 0, 0)),
                   _row(dc), _row(2 * dc)],
        out_shape=[jax.ShapeDtypeStruct(dzb.shape, F32), jax.ShapeDtypeStruct((4, HALO, CH), F32),
                   jax.ShapeDtypeStruct((1, dc), F32), jax.ShapeDtypeStruct((1, 2 * dc), F32)],
        scratch_shapes=[pltpu.VMEM((tm + HALO, dc), F32), pltpu.VMEM((tm + HALO, dc), F32), pltpu.VMEM((HALO, 8, dc), F32),
                        pltpu.VMEM((7, tm + HALO - PH_ROWS_LESS, CH), F32), pltpu.VMEM((7, tm + HALO - PH_ROWS_LESS, CH), F32)],
        input_output_aliases={8: 0},
        compiler_params=_params("arbitrary"),
    )(dcv, dcv, zb, zb, zb, zb, glu_b, dw, dzb)


def _inproj_bwd(dza, dzkr, dzb, x, dxo, g, scale, shift, w_int, layer, name, carried=None):
    s, d = x.shape
    tm = min(s, 128)

    def body(dza_ref, dzkr_ref, dzb_ref, x_ref, dxo_ref, g_ref, sc_ref, sh_ref, w_hbm, dx_ref, dsh_ref, dgg_ref, w_vmem, sems):
        @pl.when(pl.program_id(0) == 0)
        def _():
            _load_w_in(w_hbm.at[layer], w_vmem, sems)
            dsh_ref[...] = jnp.zeros_like(dsh_ref)
            dgg_ref[...] = jnp.zeros_like(dgg_ref)

        dh = _dot(dza_ref[...].astype(BF16), w_vmem[0:768])
        dh = dh + _dot(dzkr_ref[...].astype(BF16), w_vmem[768:896])
        dh = dh + _dot(dzb_ref[...].astype(BF16), w_vmem[896:W_ROWS])
        xv = x_ref[...]
        rstd = lax.rsqrt(jnp.mean(xv * xv, axis=1, keepdims=True) + EPS)
        xh = xv * rstd
        dsh_ref[...] += _sum0(dh)
        dgg_ref[...] += _sum0(dh * xh)
        dxh = dh * (g_ref[...] * (1.0 + sc_ref[...]))
        dx_ref[...] = dxo_ref[...] + rstd * (dxh - xh * jnp.mean(dxh * xh, axis=1, keepdims=True))

    tile = lambda n: pl.BlockSpec((tm, n), lambda i: (i, 0))
    return _call(
        body, (dza, dzkr, dzb, x, dxo, g, scale, shift, w_int), name=name, grid=(s // tm,), carried=carried,
        in_specs=[tile(768), tile(128), tile(4096), tile(d), tile(d), _row(d), _row(d), _row(d), ANY],
        out_specs=[tile(d), _row(d), _row(d)],
        out_shape=[jax.ShapeDtypeStruct((s, d), F32), jax.ShapeDtypeStruct((1, d), F32), jax.ShapeDtypeStruct((1, d), F32)],
        scratch=[pltpu.VMEM((W_ROWS, d), BF16), pltpu.SemaphoreType.DMA((len(W_PIECES),))])


def _grad_w_in(dza, dzkr, dzb, hb, name, carried=None):
    s, d = hb.shape
    ts = min(s, 512)
    nt = s // ts
    tiles = ((0, 768), (768, 64), (1856, 1024), (2880, 1024), (832, 1024), (3904, 1024))

    def body(a_ref, kr_ref, b_ref, h_ref, o_hbm, acc, sem):
        r, t = pl.program_id(0), pl.program_id(1)

        @pl.when(t == 0)
        def _():
            acc[...] = jnp.zeros_like(acc)

        hv = h_ref[...]

        @pl.when(r == 0)
        def _():
            acc[0:768, :] += _dot(a_ref[...].astype(BF16), hv, TN)

        @pl.when(r == 1)
        def _():
            acc[0:128, :] += _dot(kr_ref[...].astype(BF16), hv, TN)

        @pl.when(r >= 2)
        def _():
            acc[...] += _dot(b_ref[...].astype(BF16), hv, TN)

        for tile, (row0, rows) in enumerate(tiles):
            @pl.when((t == nt - 1) & (r == tile))
            def _():
                cp = pltpu.make_async_copy(acc.at[pl.ds(0, rows)], o_hbm.at[pl.ds(row0, rows)], sem)
                cp.start()
                cp.wait()

    return _call(
        body, (dza, dzkr, dzb, hb), name=name, grid=(len(tiles), nt), carried=carried,
        in_specs=[pl.BlockSpec((ts, 768), lambda r, t: (jnp.where(r == 0, t, nt - 1), 0)),
                  pl.BlockSpec((ts, 128), lambda r, t: (jnp.where(r == 1, t, jnp.where(r == 0, 0, nt - 1)), 0)),
                  pl.BlockSpec((ts, 1024), lambda r, t: (jnp.where(r >= 2, t, 0), jnp.maximum(r - 2, 0))),
                  pl.BlockSpec((ts, d), lambda r, t: (t, 0))],
        out_specs=[ANY], out_shape=[jax.ShapeDtypeStruct((4928, d), F32)],
        scratch=[pltpu.VMEM((1024, d), F32), pltpu.SemaphoreType.DMA])


def _adamw(w, g, m, v):
    m = ADAM_B1 * m + (1.0 - ADAM_B1) * g
    v = ADAM_B2 * v + (1.0 - ADAM_B2) * (g * g)
    m_hat = m / (1.0 - ADAM_B1 ** ADAM_STEP)
    v_hat = v / (1.0 - ADAM_B2 ** ADAM_STEP)
    return -ADAM_LR * (m_hat / (jnp.sqrt(v_hat) + ADAM_EPS) + ADAM_WD * w), m, v


def _adam_update(w, g0, g1, m, v, name):
    _, r, c = w.shape
    fits = [t for t in range(8, r + 1, 8) if r % t == 0 and t * c * 4 <= (1 << 21)]
    tr = max(fits) if fits else r

    def body(w_ref, g0_ref, g1_ref, m_ref, v_ref, g_ref, d_ref, mo_ref, vo_ref):
        g = jnp.where(pl.program_id(0) == 0, g0_ref[...], g1_ref[...])
        g_ref[...] = g
        d_ref[...], mo_ref[...], vo_ref[...] = _adamw(w_ref[...], g, m_ref[...], v_ref[...])

    big = pl.BlockSpec((None, tr, c), lambda l, i: (l, i, 0))
    one = pl.BlockSpec((tr, c), lambda l, i: (i, 0))
    return pl.pallas_call(
        body, name=name, grid=(2, r // tr), in_specs=[big, one, one, big, big], out_specs=[big] * 4,
        out_shape=[jax.ShapeDtypeStruct(w.shape, F32)] * 4, compiler_params=_params("arbitrary", "arbitrary"),
    )(w, g0, g1, m, v)


def _ada_update(c_all, dmod, w, m, v, carried=None):
    nl, d, n = w.shape
    tr = 256

    def body(c_ref, dm_ref, w_ref, m_ref, v_ref, g_ref, d_ref, mo_ref, vo_ref):
        cv = c_ref[...]
        act = (cv * _sigmoid(cv)).astype(BF16)
        g = _dot(act, dm_ref[...].astype(BF16), TN)
        g_ref[...] = g
        d_ref[...], mo_ref[...], vo_ref[...] = _adamw(w_ref[...], g, m_ref[...], v_ref[...])

    big = pl.BlockSpec((None, tr, n), lambda l, i: (l, i, 0))
    return _call(
        body, (c_all, dmod, w, m, v), name="ada_w_update", grid=(nl, d // tr), carried=carried,
        in_specs=[pl.BlockSpec((8, tr), lambda l, i: (0, i)), pl.BlockSpec((None, 8, n), lambda l, i: (l, 0, 0)), big, big, big],
        out_specs=[big] * 4, out_shape=[jax.ShapeDtypeStruct(w.shape, F32)] * 4)


def _small_update(gathered, w, m, v):
    r = w.shape[0]

    def body(ga_ref, w_ref, m_ref, v_ref, g_ref, d_ref, mo_ref, vo_ref):
        g = ga_ref[0]
        for dev in range(1, 8):
            g = g + ga_ref[dev]
        g_ref[...] = g
        d_ref[...], mo_ref[...], vo_ref[...] = _adamw(w_ref[...], g, m_ref[...], v_ref[...])

    return pl.pallas_call(body, name="small_update", out_shape=[jax.ShapeDtypeStruct((r, 128), F32)] * 4,
                          compiler_params=_params())(gathered, w, m, v)


SMALL = (("ada_b", 6144), ("norm_g", 2048), ("q_lat_g", 512), ("kv_lat_g", 256), ("q_norm_g", 256), ("k_norm_g", 256),
         ("glu_b", 2048), ("dw_b", 1024), ("conv_ln_g", 1024), ("conv_ln_b", 1024), ("b_pw", 1024))


def _pack_small(vals):
    cols = []
    for name, width in SMALL:
        a = vals[name]
        if a.shape[1] < width:
            a = jnp.pad(a, ((0, 0), (0, width - a.shape[1])))
        cols.append(a)
    return jnp.concatenate(cols, axis=1).reshape(-1, 128)


def _unpack_small(packed, shapes):
    flat = packed.reshape(2, -1)
    out, at = {}, 0
    for name, width in SMALL:
        out[name] = flat[:, at:at + shapes[name]]
        at += width
    return out


def _dup_gain(g):
    return jnp.concatenate([g, g[NOPE:]])[None, :]


def _undup(g):
    return jnp.concatenate([g[..., :NOPE], g[..., NOPE:NOPE + ROPE] + g[..., NOPE + ROPE:]], axis=-1)


def kernel(x, c, positions, ada_w, ada_b, norm_g, w_in, q_lat_g, w_q_up, kv_lat_g, w_kv_up, q_norm_g, k_norm_g, glu_b, dw_w, dw_b, conv_ln_g, conv_ln_b, w_pw, b_pw, w_out, loss_target, m_ada_w, m_ada_b, m_norm_g, m_w_in, m_q_lat_g, m_w_q_up, m_kv_lat_g, m_w_kv_up, m_q_norm_g, m_k_norm_g, m_glu_b, m_dw_w, m_dw_b, m_conv_ln_g, m_conv_ln_b, m_w_pw, m_b_pw, m_w_out, v_ada_w, v_ada_b, v_norm_g, v_w_in, v_q_lat_g, v_w_q_up, v_kv_lat_g, v_w_kv_up, v_q_norm_g, v_k_norm_g, v_glu_b, v_dw_w, v_dw_b, v_conv_ln_g, v_conv_ln_b, v_w_pw, v_b_pw, v_w_out):
    nl = 2
    s, d = x.shape[1], x.shape[2]
    xi, yi, ci = lax.axis_index("x"), lax.axis_index("y"), lax.axis_index("c")
    shard = 2 * xi + yi
    me = 4 * xi + 2 * yi + ci
    cidx = jnp.reshape(ci, (1,)).astype(jnp.int32)
    jc = jnp.stack([shard, ci]).astype(jnp.int32)
    x0 = x.reshape(s, d)
    target = loss_target.reshape(s, d)

    c_all = _allgather8(c.reshape(8, d // 8), "gather_c").reshape(8, d)
    n_ada = ada_w.shape[2]
    ada_b_shard = lax.dynamic_slice_in_dim(ada_b, shard * n_ada, n_ada, axis=1)[:, None, :]
    mod_shard = _modulation(c_all, ada_w, ada_b_shard)
    mod_all = _allgather8(mod_shard.reshape(nl * 8, n_ada), "gather_mod")
    mod_rows = lax.dynamic_index_in_dim(mod_all.reshape(4, 2, nl, 8, n_ada)[:, 0], me, axis=2, keepdims=False)
    mod_me = jnp.transpose(mod_rows, (1, 0, 2)).reshape(nl, 3, 1, d)

    tr = lambda a: jnp.transpose(a, (0, 2, 1))
    w_in_t = tr(w_in).astype(BF16)
    wq = w_q_up.reshape(nl, QL, 2, QK)
    wq = jnp.concatenate([wq, wq[..., NOPE:]], axis=-1)
    wq = jnp.transpose(wq, (0, 2, 1, 3)).reshape(nl, 2 * QL, HP).astype(BF16)
    dw_pad = jnp.pad(dw_w, ((0, 0), (0, HALO - CONV_K), (0, 0)))
    local = [w_in_t, wq, w_kv_up.astype(BF16), dw_pad, w_pw.astype(BF16), w_out.astype(BF16)]

    def kernel_layouts(bufs):
        w_in_g, wq_g, wkv_g, dw_g, wpw_g, wout_g = bufs
        return dict(w_in=w_in_g.reshape(1, 4 * w_in_g.shape[2], d), wq=wq_g.reshape(1, N_HEADS, QL, HP), wkv=wkv_g,
                    dw=jnp.transpose(dw_g, (0, 2, 1, 3)).reshape(1, HALO, 4 * dw_g.shape[3]),
                    wpw=wpw_g.reshape(1, 4 * wpw_g.shape[2], wpw_g.shape[3]), wout=wout_g.reshape(1, 4 * wout_g.shape[2], d))

    w_in0 = _run_alone(_gather_hand_on(_run_alone(_gather_start([local[0][0:1]]), "gather_w_in0")), "gather_w_in0_hand_on")
    rest0 = _gather_start([a[0:1] for a in local[1:]])
    next_gather = _gather_start([a[1:2] for a in local])
    wts = [None] * nl

    cos, sin = _rope_tables(positions.reshape(s, 1))
    row = lambda a, l: a[l][None, :]

    saved = []
    xl = x0
    for l in range(nl):
        shift, scale, gate = mod_me[l, 0], mod_me[l, 1], mod_me[l, 2]
        if l == 0:
            (hb, za, zkr, zb), landed = _inproj_fwd(xl, row(norm_g, l), scale, shift, w_in0[0].reshape(1, -1, d), 0,
                                                    f"inproj_fwd{l}", carried=rest0)
            wts[0] = kernel_layouts(w_in0 + _run_alone(_gather_hand_on(landed), "gather_rest0_hand_on"))
        else:
            hb, za, zkr, zb = _inproj_fwd(xl, row(norm_g, l), scale, shift, wts[l]["w_in"], 0, f"inproj_fwd{l}")
        w = wts[l]
        gains =(row(q_lat_g, l), row(kv_lat_g, l), _dup_gain(q_norm_g[l]), _dup_gain(k_norm_g[l]))
        q, k, v = _mla_prep_fwd(za, zkr, cos, sin, w["wq"], w["wkv"], *gains, 0, f"mla_prep_fwd{l}")
        conv_args = (zb, row(glu_b, l), w["dw"], row(dw_b, l), row(conv_ln_g, l), row(conv_ln_b, l), w["wpw"], row(b_pw, l), 0)
        if l == 0:
            (o, lse), landed = _flash_fwd(q, k, v, f"flash_fwd{l}", carried=next_gather)
            (cv, pw), gathered1 = _conv_fwd(*conv_args, f"conv_fwd{l}", carried=_gather_hand_on(landed))
            wts[1] = kernel_layouts(gathered1)
        else:
            o, lse = _flash_fwd(q, k, v, f"flash_fwd{l}")
            cv, pw = _conv_fwd(*conv_args, f"conv_fwd{l}")
        xn, yv, mixb = _outproj_fwd(xl, o, zb, pw, gate, w["wout"], 0, f"outproj_fwd{l}")
        saved.append(dict(x=xl, hb=hb, za=za, zkr=zkr, zb=zb, q=q, k=k, v=v, o=o, lse=lse, cv=cv, pw=pw, y=yv, mixb=mixb, gains=gains))
        xl = xn

    tok_loss, dx = _loss_head(xl, target)
    loss = lax.psum(jnp.sum(tok_loss), ("x", "y", "c"))

    big = [None] * nl
    small = [None] * nl
    shards_of = lambda gs: [g.reshape(4, g.shape[0] // 4, g.shape[1]) for g in gs]
    pair_sums = lambda l, parts, theirs: [_pair_sum(p, t, cidx, f"pair_sum{l}_{e}") for e, (p, t) in enumerate(zip(parts, theirs))]
    chip_sums = lambda l, sums, landed: [_chip_sum(sm, ld, jc, f"chip_sum{l}_{e}") for e, (sm, ld) in enumerate(zip(sums, landed))]
    halves = [None] * nl
    for l in reversed(range(nl)):
        sv, w = saved[l], wts[l]
        shift, scale, gate = mod_me[l, 0], mod_me[l, 1], mod_me[l, 2]
        out_args = (dx, sv["y"], gate, w["wout"], sv["o"], sv["zb"], sv["pw"], 0, f"outproj_bwd{l}")
        attn_args = (sv["q"], sv["k"], sv["v"])
        if l == 0:
            parts = shards_of(big[1])
            (dgate, dyb, do, delta, dzb, dpw), theirs = _outproj_bwd(*out_args, carried=_pair_exchange(parts))
            sums = pair_sums(1, parts, theirs)
            (dq, dk, dv), landed = _flash_bwd(*attn_args, do, sv["lse"], delta, f"flash_bwd{l}", carried=_chip_scatter(sums))
            halves[1] = chip_sums(1, sums, landed)
        else:
            dgate, dyb, do, delta, dzb, dpw = _outproj_bwd(*out_args)
            dq, dk, dv = _flash_bwd(*attn_args, do, sv["lse"], delta, f"flash_bwd{l}")
        g_out = _grad_tn(sv["mixb"], dyb, f"grad_w_out{l}")
        dza, dzkr, g_q, g_kv, dgql, dgkvl, dgq, dgk = _mla_prep_bwd(
            dq, dk, dv, sv["za"], sv["zkr"], cos, sin, w["wq"], w["wkv"], *sv["gains"], 0, f"mla_prep_bwd{l}")
        dcv, act, dbpw, dlng, dlnb = _pointwise_bwd(dpw, sv["cv"], row(conv_ln_g, l), row(conv_ln_b, l), w["wpw"], 0, f"pointwise_bwd{l}")
        g_pw = _grad_tn(act, dpw, f"grad_w_pw{l}")
        dzb, g_dw, ddwb, dglub = _conv_bwd(dcv, sv["zb"], dzb, row(glu_b, l), w["dw"], 0, f"conv_bwd{l}")
        rest = [g_q.reshape(N_HEADS * QL, HP), g_kv.reshape(4 * KVL, 512), g_dw.reshape(4 * HALO, CH), g_pw, g_out]
        in_args = (dza, dzkr, dzb, sv["x"], dx, row(norm_g, l), scale, shift, w["w_in"], 0, f"inproj_bwd{l}")
        if l == 0:
            parts = shards_of(rest)
            (dx, dshift, dgg), theirs = _inproj_bwd(*in_args, carried=_pair_exchange(parts))
            sums = pair_sums("0r", parts, theirs)
            (g_in,), landed = _grad_w_in(dza, dzkr, dzb, sv["hb"], f"grad_w_in{l}", carried=_chip_scatter(sums))
            rest_halves = chip_sums("0r", sums, landed)
        else:
            dx, dshift, dgg = _inproj_bwd(*in_args)
            g_in, = _grad_w_in(dza, dzkr, dzb, sv["hb"], f"grad_w_in{l}")
        big[l] = [g_in] + rest
        small[l] = dict(ada_b=jnp.concatenate([dshift, dgg * row(norm_g, l), dgate], axis=1), norm_g=dgg * (1.0 + scale),
                        q_lat_g=dgql, kv_lat_g=dgkvl, q_norm_g=_undup(dgq), k_norm_g=_undup(dgk), glu_b=dglub, dw_b=ddwb,
                        conv_ln_g=dlng, conv_ln_b=dlnb, b_pw=dbpw)
    grad_x = dx.reshape(x.shape)

    names = [n for n, _ in SMALL]
    mine = _pack_small({n: jnp.concatenate([small[0][n], small[1][n]], axis=0) for n in names})
    gathered = _allgather8(mine, "gather_small")
    weights = dict(ada_b=ada_b, norm_g=norm_g, q_lat_g=q_lat_g, kv_lat_g=kv_lat_g, q_norm_g=q_norm_g, k_norm_g=k_norm_g,
                   glu_b=glu_b, dw_b=dw_b, conv_ln_g=conv_ln_g, conv_ln_b=conv_ln_b, b_pw=b_pw)
    m_small = dict(ada_b=m_ada_b, norm_g=m_norm_g, q_lat_g=m_q_lat_g, kv_lat_g=m_kv_lat_g, q_norm_g=m_q_norm_g, k_norm_g=m_k_norm_g,
                   glu_b=m_glu_b, dw_b=m_dw_b, conv_ln_g=m_conv_ln_g, conv_ln_b=m_conv_ln_b, b_pw=m_b_pw)
    v_small = dict(ada_b=v_ada_b, norm_g=v_norm_g, q_lat_g=v_q_lat_g, kv_lat_g=v_kv_lat_g, q_norm_g=v_q_norm_g, k_norm_g=v_k_norm_g,
                   glu_b=v_glu_b, dw_b=v_dw_b, conv_ln_g=v_conv_ln_g, conv_ln_b=v_conv_ln_b, b_pw=v_b_pw)
    widths = {n: weights[n].shape[1] for n in names}
    v_packed = _pack_small({n: jnp.pad(v_small[n], ((0, 0), (0, dict(SMALL)[n] - widths[n])), constant_values=1.0) for n in names})
    small_out = [_unpack_small(a, widths) for a in _small_update(gathered, _pack_small(weights), _pack_small(m_small), v_packed)]

    ada_rows = gathered.reshape(8, nl, -1)[:, :, :3 * d]
    dmod = lax.dynamic_slice_in_dim(jnp.transpose(ada_rows, (1, 0, 2)), shard * n_ada, n_ada, axis=2)
    parts = shards_of(big[0][:1])
    sums = pair_sums("0i", parts, _run_alone(_pair_exchange(parts), "pair_exchange_w_in0"))
    ada_out, landed = _ada_update(c_all, dmod, ada_w, m_ada_w, v_ada_w, carried=_chip_scatter(sums))

    halves[0] = chip_sums("0i", sums, landed) + rest_halves
    full = _run_alone(_pair_complete(halves[0] + halves[1]), "pair_complete")
    per_layer = [full[l * 6:(l + 1) * 6] for l in range(nl)]

    def natural_q(g):
        return jnp.transpose(_undup(g.reshape(2, QL, HP)), (1, 0, 2)).reshape(QL, 2 * QK)

    grads = [[per_layer[l][0], natural_q(per_layer[l][1]), per_layer[l][2], per_layer[l][3][:CONV_K], per_layer[l][4], per_layer[l][5]]
             for l in range(nl)]
    sharded = (("w_in", tr(w_in), tr(m_w_in), tr(v_w_in)), ("w_q_up", w_q_up, m_w_q_up, v_w_q_up),
               ("w_kv_up", w_kv_up, m_w_kv_up, v_w_kv_up), ("dw_w", dw_w, m_dw_w, v_dw_w),
               ("w_pw", w_pw, m_w_pw, v_w_pw), ("w_out", w_out, m_w_out, v_w_out))
    big_out = {name: _adam_update(w, grads[0][e], grads[1][e], m, v, f"adam_{name}") for e, (name, w, m, v) in enumerate(sharded)}
    big_out["w_in"] = [tr(a) for a in big_out["w_in"]]

    order = ["ada_w", "ada_b", "norm_g", "w_in", "q_lat_g", "w_q_up", "kv_lat_g", "w_kv_up", "q_norm_g", "k_norm_g", "glu_b",
             "dw_w", "dw_b", "conv_ln_g", "conv_ln_b", "w_pw", "b_pw", "w_out"]

    def leaf(kind, name):
        if name == "ada_w":
            return ada_out[kind]
        if name in big_out:
            return big_out[name][kind]
        return small_out[kind][name]

    return (loss, grad_x, *[leaf(kind, name) for kind in range(4) for name in order])
```

```python
import functools
import math

import jax
import jax.numpy as jnp
from jax import lax
from jax.experimental import pallas as pl
from jax.experimental.pallas import tpu as pltpu

F32, BF16 = jnp.float32, jnp.bfloat16
MESH = pl.DeviceIdType.MESH
ANY = pl.BlockSpec(memory_space=pl.ANY)

N_HEADS, NOPE, ROPE, VD = 8, 128, 64, 128
QK = NOPE + ROPE
QL, KVL = 512, 256
HP = 256
CONV_K, HALO = 31, 32
ROPE_THETA = 10000.0
EPS = 1e-6
ADAM_LR, ADAM_B1, ADAM_B2, ADAM_EPS, ADAM_WD, ADAM_STEP = 0.001, 0.9, 0.999, 1e-08, 0.01, 10
V7X_VMEM_LIMIT = 56 * 1024 * 1024

NT = (((1,), (1,)), ((), ()))
TN = (((0,), (0,)), ((), ()))
NN = (((1,), (0,)), ((), ()))


def _dot(a, b, dims=NN):
    return lax.dot_general(a, b, dims, preferred_element_type=F32)


def _params(*sem):
    return pltpu.CompilerParams(dimension_semantics=sem or None, vmem_limit_bytes=V7X_VMEM_LIMIT)


def _sigmoid(x):
    return 1.0 / (1.0 + jnp.exp(-x))


def _sum0(x):
    return jnp.sum(x, axis=0, keepdims=True)


def _sum1(x):
    return jnp.sum(x, axis=1, keepdims=True)


def _row(n):
    return pl.BlockSpec((1, n), lambda *_: (0, 0))


def _place():
    x, y, c = lax.axis_index("x"), lax.axis_index("y"), lax.axis_index("c")
    chips = [(1 - x, y), (x, 1 - y), (1 - x, 1 - y)]
    return x, y, c, chips


def _allgather8(v, name):
    r, n = v.shape

    def body(v_ref, out_ref, send_sems, recv_sems, local_sem):
        x, y, c, chips = _place()
        me, sibling = (x, y, c), (x, y, 1 - c)

        def slot(px, py, pc):
            return out_ref.at[4 * px + 2 * py + pc]

        def copy(k, block, to, src=None):
            return pltpu.make_async_remote_copy(
                src_ref=slot(*block) if src is None else src, dst_ref=slot(*block),
                send_sem=send_sems.at[k], recv_sem=recv_sems.at[k], device_id=to, device_id_type=MESH)

        mine = pltpu.make_async_copy(v_ref, slot(*me), local_sem)
        mine.start()
        first = [copy(0, me, sibling, src=v_ref)]
        first += [copy(1 + j, me, (*chip, c), src=v_ref) for j, chip in enumerate(chips)]
        for cp in first:
            cp.start()
        passed = [copy(4 + j, (*chip, c), sibling) for j, chip in enumerate(chips)]
        for j, chip in enumerate(chips):
            copy(1 + j, (*chip, c), me).wait_recv()
            passed[j].start()
        copy(0, sibling, me).wait_recv()
        for j, chip in enumerate(chips):
            copy(4 + j, (*chip, 1 - c), me).wait_recv()
        for cp in first + passed:
            cp.wait_send()
        mine.wait()

    return pl.pallas_call(
        body, name=name, out_shape=jax.ShapeDtypeStruct((8, r, n), v.dtype),
        in_specs=[pl.BlockSpec(memory_space=pltpu.VMEM)], out_specs=pl.BlockSpec(memory_space=pltpu.VMEM),
        scratch_shapes=[pltpu.SemaphoreType.DMA((7,)), pltpu.SemaphoreType.DMA((7,)), pltpu.SemaphoreType.DMA],
    )(v)


class _Carried:
    def __init__(self, operands, results, n_sems, start, finish, aliases=None):
        self.operands, self.results, self.n_sems = operands, results, n_sems
        self.start, self.finish, self.aliases = start, finish, aliases or {}


def _run_alone(carried, name):
    k = len(carried.operands)

    def body(*refs):
        args = (refs[:k], refs[k:k + len(carried.results)], refs[-2], refs[-1])
        carried.start(*args)
        carried.finish(*args)

    outs = pl.pallas_call(
        body, name=name, out_shape=carried.results, in_specs=[ANY] * k, out_specs=[ANY] * len(carried.results),
        input_output_aliases=carried.aliases,
        scratch_shapes=[pltpu.SemaphoreType.DMA((carried.n_sems,)), pltpu.SemaphoreType.DMA((carried.n_sems,))],
    )(*carried.operands)
    return list(outs)


def _call(body, operands, *, name, grid, in_specs, out_specs, out_shape, scratch=(), aliases=None, carried=None):
    params = _params(*(["arbitrary"] * len(grid)))
    n_in, n_out = len(in_specs), len(out_shape)
    if carried is None:
        return pl.pallas_call(body, name=name, grid=grid, in_specs=in_specs, out_specs=out_specs, out_shape=out_shape,
                              scratch_shapes=list(scratch), input_output_aliases=aliases or {}, compiler_params=params)(*operands)
    k_in, k_out = len(carried.operands), len(carried.results)

    def wrapped(*refs):
        ins, outs = refs[:n_in], refs[n_in + k_in:n_in + k_in + n_out]
        comm = (refs[n_in:n_in + k_in], refs[n_in + k_in + n_out:n_in + k_in + n_out + k_out], refs[-2], refs[-1])
        steps = [pl.program_id(a) for a in range(len(grid))]
        first = functools.reduce(jnp.logical_and, [s == 0 for s in steps])
        last = functools.reduce(jnp.logical_and, [s == g - 1 for s, g in zip(steps, grid)])

        @pl.when(first)
        def _():
            carried.start(*comm)

        body(*ins, *outs, *refs[n_in + k_in + n_out + k_out:-2])

        @pl.when(last)
        def _():
            carried.finish(*comm)

    both = dict(aliases or {})
    both.update({n_in + i: n_out + o for i, o in carried.aliases.items()})
    res = pl.pallas_call(
        wrapped, name=name, grid=grid, in_specs=list(in_specs) + [ANY] * k_in, out_specs=list(out_specs) + [ANY] * k_out,
        out_shape=list(out_shape) + list(carried.results), input_output_aliases=both, compiler_params=params,
        scratch_shapes=list(scratch) + [pltpu.SemaphoreType.DMA((carried.n_sems,)), pltpu.SemaphoreType.DMA((carried.n_sems,))],
    )(*operands, *carried.operands)
    return list(res[:n_out]), list(res[n_out:])


def _gather_start(shards):
    ne = len(shards)
    per = 4

    def copies(srcs, dsts, send_sems, recv_sems):
        x, y, c, chips = _place()
        jme = 2 * x + y
        out = []
        for e in range(ne):
            half = srcs[e].shape[2] // 2
            own = pl.ds(pl.multiple_of(c * half, 128), half)
            for k, chip in enumerate(chips):
                out.append(pltpu.make_async_remote_copy(
                    src_ref=srcs[e].at[:, :, own], dst_ref=dsts[e].at[:, jme, :, own], send_sem=send_sems.at[per * e + k],
                    recv_sem=recv_sems.at[per * e + k], device_id=(*chip, c), device_id_type=MESH))
            out.append(pltpu.make_async_remote_copy(
                src_ref=srcs[e], dst_ref=dsts[e].at[:, jme], send_sem=send_sems.at[per * e + 3],
                recv_sem=recv_sems.at[per * e + 3], device_id=(x, y, 1 - c), device_id_type=MESH))
        return out

    def start(*a):
        for cp in copies(*a):
            cp.start()

    def finish(*a):
        for cp in copies(*a):
            cp.wait()

    results = [jax.ShapeDtypeStruct((s.shape[0], 4) + s.shape[1:], s.dtype) for s in shards]
    return _Carried(list(shards), results, per * ne, start, finish)


def _gather_hand_on(bufs):
    ne = len(bufs)

    def copy(e, k, dsts, send_sems, recv_sems, mine):
        x, y, c, chips = _place()
        px, py = chips[k]
        half = dsts[e].shape[3] // 2
        cols = pl.ds(pl.multiple_of((c if mine else 1 - c) * half, 128), half)
        part = dsts[e].at[:, 2 * px + py, :, cols]
        return pltpu.make_async_remote_copy(src_ref=part, dst_ref=part, send_sem=send_sems.at[3 * e + k],
                                            recv_sem=recv_sems.at[3 * e + k], device_id=(x, y, 1 - c), device_id_type=MESH)

    def start(srcs, dsts, send_sems, recv_sems):
        for e in range(ne):
            for k in range(3):
                copy(e, k, dsts, send_sems, recv_sems, True).start()

    def finish(srcs, dsts, send_sems, recv_sems):
        for e in range(ne):
            for k in range(3):
                copy(e, k, dsts, send_sems, recv_sems, True).wait_send()
                copy(e, k, dsts, send_sems, recv_sems, False).wait_recv()

    results = [jax.ShapeDtypeStruct(b.shape, b.dtype) for b in bufs]
    return _Carried(list(bufs), results, 3 * ne, start, finish, aliases={e: e for e in range(ne)})


def _pair_exchange(parts):
    ne = len(parts)

    def copies(srcs, dsts, send_sems, recv_sems):
        x, y, c, _ = _place()
        out = []
        for e in range(ne):
            half = srcs[e].shape[2] // 2
            theirs = pl.ds(pl.multiple_of((1 - c) * half, 128), half)
            out.append(pltpu.make_async_remote_copy(
                src_ref=srcs[e].at[:, :, theirs], dst_ref=dsts[e], send_sem=send_sems.at[e],
                recv_sem=recv_sems.at[e], device_id=(x, y, 1 - c), device_id_type=MESH))
        return out

    def start(*a):
        for cp in copies(*a):
            cp.start()

    def finish(*a):
        for cp in copies(*a):
            cp.wait()

    results = [jax.ShapeDtypeStruct(p.shape[:2] + (p.shape[2] // 2,), p.dtype) for p in parts]
    return _Carried(list(parts), results, ne, start, finish)


def _chip_scatter(sums):
    ne = len(sums)

    def copies(srcs, dsts, send_sems, recv_sems):
        x, y, c, chips = _place()
        return [pltpu.make_async_remote_copy(
                    src_ref=srcs[e].at[2 * px + py], dst_ref=dsts[e].at[k], send_sem=send_sems.at[3 * e + k],
                    recv_sem=recv_sems.at[3 * e + k], device_id=(px, py, c), device_id_type=MESH)
                for e in range(ne) for k, (px, py) in enumerate(chips)]

    def start(*a):
        for cp in copies(*a):
            cp.start()

    def finish(*a):
        for cp in copies(*a):
            cp.wait()

    results = [jax.ShapeDtypeStruct((3,) + s.shape[1:], s.dtype) for s in sums]
    return _Carried(list(sums), results, 3 * ne, start, finish)


def _pair_complete(grads):
    ne = len(grads)

    def copy(e, dsts, send_sems, recv_sems, mine):
        x, y, c, _ = _place()
        half = dsts[e].shape[1] // 2
        cols = pl.ds(pl.multiple_of((c if mine else 1 - c) * half, 128), half)
        return pltpu.make_async_remote_copy(
            src_ref=dsts[e].at[:, cols], dst_ref=dsts[e].at[:, cols], send_sem=send_sems.at[e],
            recv_sem=recv_sems.at[e], device_id=(x, y, 1 - c), device_id_type=MESH)

    def start(srcs, dsts, send_sems, recv_sems):
        for e in range(ne):
            copy(e, dsts, send_sems, recv_sems, True).start()

    def finish(srcs, dsts, send_sems, recv_sems):
        for e in range(ne):
            copy(e, dsts, send_sems, recv_sems, True).wait_send()
            copy(e, dsts, send_sems, recv_sems, False).wait_recv()

    results = [jax.ShapeDtypeStruct(g.shape, g.dtype) for g in grads]
    return _Carried(list(grads), results, ne, start, finish, aliases={e: e for e in range(ne)})


def _pair_sum(part, theirs, cidx, name):
    _, r, n = part.shape
    half = n // 2

    def body(c_ref, p_ref, t_ref, o_ref):
        o_ref[...] = (p_ref[...] + t_ref[...]).astype(BF16)

    gs = pltpu.PrefetchScalarGridSpec(
        num_scalar_prefetch=1, grid=(4,),
        in_specs=[pl.BlockSpec((1, r, half), lambda j, c: (j, 0, c[0])),
                  pl.BlockSpec((1, r, half), lambda j, c: (j, 0, 0))],
        out_specs=pl.BlockSpec((1, r, half), lambda j, c: (j, 0, 0)))
    return pl.pallas_call(body, name=name, grid_spec=gs, out_shape=jax.ShapeDtypeStruct((4, r, half), BF16),
                          compiler_params=_params("arbitrary"))(cidx, part, theirs)


def _chip_sum(sums, landed, jc, name):
    _, r, half = sums.shape

    def body(jc_ref, s_ref, l_ref, o_ref):
        acc = s_ref[0].astype(F32)
        for k in range(3):
            acc = acc + l_ref[k].astype(F32)
        o_ref[...] = acc

    gs = pltpu.PrefetchScalarGridSpec(
        num_scalar_prefetch=1, grid=(1,),
        in_specs=[pl.BlockSpec((1, r, half), lambda i, jc: (jc[0], 0, 0)),
                  pl.BlockSpec((3, r, half), lambda i, jc: (0, 0, 0))],
        out_specs=pl.BlockSpec((r, half), lambda i, jc: (0, jc[1])))
    return pl.pallas_call(body, name=name, grid_spec=gs, out_shape=jax.ShapeDtypeStruct((r, 2 * half), F32),
                          compiler_params=_params("arbitrary"))(jc, sums, landed)


def _rope_tables(pos):
    s = pos.shape[0]
    lane = jnp.arange(128)
    inv = 1.0 / (ROPE_THETA ** ((2 * (lane % 32)).astype(F32) / ROPE))
    keep = (lane < 64).astype(F32)
    sign = jnp.where(lane < 32, -1.0, 1.0).astype(F32) * keep
    consts = jnp.stack([inv.astype(F32), keep, sign])[:, None, :]

    def body(p_ref, k_ref, c_ref, s_ref):
        ang = p_ref[...].astype(F32) * k_ref[0]
        c_ref[...] = jnp.cos(ang) * k_ref[1]
        s_ref[...] = jnp.sin(ang) * k_ref[2]

    tm = min(s, 1024)
    return pl.pallas_call(
        body, name="rope_tables", grid=(s // tm,),
        in_specs=[pl.BlockSpec((tm, 1), lambda i: (i, 0)), pl.BlockSpec((3, 1, 128), lambda i: (0, 0, 0))],
        out_specs=[pl.BlockSpec((tm, 128), lambda i: (i, 0))] * 2,
        out_shape=[jax.ShapeDtypeStruct((s, 128), F32)] * 2, compiler_params=_params("arbitrary"),
    )(pos, consts)


def _modulation(c_all, ada_w, ada_b_shard):
    nl, d, n = ada_w.shape
    tn = 512

    def body(c_ref, w_ref, b_ref, o_ref):
        cv = c_ref[...]
        act = (cv * _sigmoid(cv)).astype(BF16)
        o_ref[...] = _dot(act, w_ref[...].astype(BF16)) + b_ref[...]

    return pl.pallas_call(
        body, name="modulation", grid=(nl, n // tn),
        in_specs=[pl.BlockSpec((8, d), lambda l, j: (0, 0)), pl.BlockSpec((None, d, tn), lambda l, j: (l, 0, j)),
                  pl.BlockSpec((None, 1, tn), lambda l, j: (l, 0, j))],
        out_specs=pl.BlockSpec((None, 8, tn), lambda l, j: (l, 0, j)),
        out_shape=jax.ShapeDtypeStruct((nl, 8, n), F32), compiler_params=_params("arbitrary", "arbitrary"),
    )(c_all, ada_w, ada_b_shard)


def _loss_head(xf, target):
    s, d = xf.shape
    tm = min(s, 512)

    def body(x_ref, t_ref, l_ref, dx_ref):
        err = x_ref[...] - t_ref[...]
        l_ref[...] = 0.5 * jnp.mean(err * err, axis=1, keepdims=True)
        dx_ref[...] = err * (1.0 / d)

    return pl.pallas_call(
        body, name="loss_head", grid=(s // tm,),
        in_specs=[pl.BlockSpec((tm, d), lambda i: (i, 0))] * 2,
        out_specs=[pl.BlockSpec((tm, 1), lambda i: (i, 0)), pl.BlockSpec((tm, d), lambda i: (i, 0))],
        out_shape=[jax.ShapeDtypeStruct((s, 1), F32), jax.ShapeDtypeStruct((s, d), F32)],
        compiler_params=_params("arbitrary"),
    )(xf, target)


W_ROWS = 4992
W_PIECES = ((0, 0, 832), (832, 768, 64), (896, 1856, 2048), (2944, 832, 1024), (3968, 3904, 1024))


def _load_w_in(w_hbm, w_vmem, sems):
    cps = [pltpu.make_async_copy(w_hbm.at[pl.ds(src, n)], w_vmem.at[pl.ds(dst, n)], sems.at[i])
           for i, (dst, src, n) in enumerate(W_PIECES)]
    for cp in cps:
        cp.start()
    for cp in cps:
        cp.wait()


def _inproj_fwd(x, g, scale, shift, w_int, layer, name, carried=None):
    s, d = x.shape
    tm = min(s, 256)

    def body(x_ref, g_ref, sc_ref, sh_ref, w_hbm, hb_ref, za_ref, zkr_ref, zb_ref, w_vmem, sems):
        @pl.when(pl.program_id(0) == 0)
        def _():
            _load_w_in(w_hbm.at[layer], w_vmem, sems)

        xv = x_ref[...]
        rstd = lax.rsqrt(jnp.mean(xv * xv, axis=1, keepdims=True) + EPS)
        h = (xv * rstd) * g_ref[...] * (1.0 + sc_ref[...]) + sh_ref[...]
        hb = h.astype(BF16)
        hb_ref[...] = hb
        za_ref[...] = _dot(hb, w_vmem[0:768], NT)
        zkr_ref[...] = _dot(hb, w_vmem[768:896], NT)
        zb_ref[...] = _dot(hb, w_vmem[896:W_ROWS], NT)

    return _call(
        body, (x, g, scale, shift, w_int), name=name, grid=(s // tm,), carried=carried,
        in_specs=[pl.BlockSpec((tm, d), lambda i: (i, 0)), _row(d), _row(d), _row(d), ANY],
        out_specs=[pl.BlockSpec((tm, d), lambda i: (i, 0)), pl.BlockSpec((tm, 768), lambda i: (i, 0)),
                   pl.BlockSpec((tm, 128), lambda i: (i, 0)), pl.BlockSpec((tm, 4096), lambda i: (i, 0))],
        out_shape=[jax.ShapeDtypeStruct((s, d), BF16), jax.ShapeDtypeStruct((s, 768), F32),
                   jax.ShapeDtypeStruct((s, 128), F32), jax.ShapeDtypeStruct((s, 4096), F32)],
        scratch=[pltpu.VMEM((W_ROWS, d), BF16), pltpu.SemaphoreType.DMA((len(W_PIECES),))])


def _rope(yv, cos, sin):
    return yv * cos + pltpu.roll(yv, 32, axis=1) * sin


def _mla_prep_fwd(za, zkr, cos, sin, wq, wkv, gql, gkvl, gq2, gk2, layer, name):
    s = za.shape[0]
    tm = min(s, 256)

    def body(za_ref, zkr_ref, cos_ref, sin_ref, wq_ref, wkv_ref, gql_ref, gkvl_ref, gq_ref, gk_ref, q_ref, k_ref, v_ref):
        zq, zkv = za_ref[:, 0:QL], za_ref[:, QL:QL + KVL]
        qn = (zq * lax.rsqrt(jnp.mean(zq * zq, axis=1, keepdims=True) + EPS) * gql_ref[...]).astype(BF16)
        kvn = (zkv * lax.rsqrt(jnp.mean(zkv * zkv, axis=1, keepdims=True) + EPS) * gkvl_ref[...]).astype(BF16)
        kr = zkr_ref[...]
        kr_ss = 0.5 * _sum1(kr * kr)
        cos, sin = cos_ref[...], sin_ref[...]
        gq, gk = gq_ref[...] * SCORE_SCALE, gk_ref[...]
        qr_all, kvr_all = _dot(qn, wq_ref[...]), _dot(kvn, wkv_ref[...])
        for h in range(N_HEADS):
            qr = qr_all[:, h * HP:(h + 1) * HP]
            n, yv = qr[:, :NOPE], qr[:, NOPE:]
            rstd = lax.rsqrt((_sum1(n * n) + 0.5 * _sum1(yv * yv)) * (1.0 / QK) + EPS)
            q_ref[h, :, 0:NOPE] = (n * rstd * gq[:, :NOPE]).astype(BF16)
            q_ref[h, :, NOPE:HP] = _rope(yv * rstd * gq[:, NOPE:], cos, sin).astype(BF16)
            kvr = kvr_all[:, h * HP:(h + 1) * HP]
            kn, vv = kvr[:, :NOPE], kvr[:, NOPE:]
            rstd = lax.rsqrt((_sum1(kn * kn) + kr_ss) * (1.0 / QK) + EPS)
            k_ref[h, :, 0:NOPE] = (kn * rstd * gk[:, :NOPE]).astype(BF16)
            k_ref[h, :, NOPE:HP] = _rope(kr * rstd * gk[:, NOPE:], cos, sin).astype(BF16)
            v_ref[h] = vv.astype(BF16)

    tile = lambda n: pl.BlockSpec((tm, n), lambda i: (i, 0))
    return pl.pallas_call(
        body, name=name, grid=(s // tm,),
        in_specs=[tile(768), tile(128), tile(128), tile(128),
                  pl.BlockSpec((None, QL, N_HEADS * HP), lambda i: (layer, 0, 0)),
                  pl.BlockSpec((None, KVL, N_HEADS * HP), lambda i: (layer, 0, 0)),
                  _row(QL), _row(KVL), _row(HP), _row(HP)],
        out_specs=[pl.BlockSpec((N_HEADS, tm, HP), lambda i: (0, i, 0)), pl.BlockSpec((N_HEADS, tm, HP), lambda i: (0, i, 0)),
                   pl.BlockSpec((N_HEADS, tm, VD), lambda i: (0, i, 0))],
        out_shape=[jax.ShapeDtypeStruct((N_HEADS, s, HP), BF16), jax.ShapeDtypeStruct((N_HEADS, s, HP), BF16),
                   jax.ShapeDtypeStruct((N_HEADS, s, VD), BF16)],
        compiler_params=_params("arbitrary"),
    )(za, zkr, cos, sin, wq, wkv, gql, gkvl, gq2, gk2)


SCORE_SCALE = 1.0 / math.sqrt(QK)
MASKED = -1e30


def _flash_fwd(q, k, v, name, carried=None):
    s = q.shape[1]
    t = min(s, 1024)

    def body(q_ref, k_ref, v_ref, o_ref, lse_ref):
        i = pl.program_id(1)
        qb = q_ref[...]
        row = lax.broadcasted_iota(jnp.int32, (t, t), 0)
        col = lax.broadcasted_iota(jnp.int32, (t, t), 1)

        def block(j):
            return pl.ds(pl.multiple_of(j * t, t), t)

        def scores(j):
            return _dot(qb, k_ref[block(j), :], NT)

        def update(j, sc, m, l, acc, diagonal):
            if diagonal:
                sc = jnp.where(col <= row, sc, MASKED)
            m_new = jnp.maximum(m, jnp.max(sc, axis=1, keepdims=True))
            p = jnp.exp(sc - m_new)
            alpha = jnp.exp(m - m_new)
            return m_new, alpha * l + _sum1(p), alpha * acc + _dot(p.astype(BF16), v_ref[block(j), :])

        init = (jnp.full((t, 1), MASKED, F32), jnp.zeros((t, 1), F32), jnp.zeros((t, VD), F32))
        carry = lax.fori_loop(0, i, lambda j, cr: update(j, scores(j), *cr, False), init)
        m, l, acc = update(i, scores(i), *carry, True)
        o_ref[...] = acc / l
        lse_ref[...] = m + jnp.log(l)

    return _call(
        body, (q, k, v), name=name, grid=(N_HEADS, s // t), carried=carried,
        in_specs=[pl.BlockSpec((None, t, HP), lambda h, i: (h, i, 0)), pl.BlockSpec((None, s, HP), lambda h, i: (h, 0, 0)),
                  pl.BlockSpec((None, s, VD), lambda h, i: (h, 0, 0))],
        out_specs=[pl.BlockSpec((t, VD), lambda h, i: (i, h)), pl.BlockSpec((None, t, 1), lambda h, i: (h, i, 0))],
        out_shape=[jax.ShapeDtypeStruct((s, N_HEADS * VD), F32), jax.ShapeDtypeStruct((N_HEADS, s, 1), F32)])


CH, RC = 256, 64


PH_ROWS_LESS = 8


def _glu(val, gate, bias):
    c = val.shape[1]
    return (val + bias[:, :c]) * _sigmoid(gate + bias[:, c:])


def _make_phases(buf, phases, cc):
    rows = buf.shape[0] - PH_ROWS_LESS
    for b in range(1, 8):
        phases[b - 1] = buf[pl.ds(b, rows), cc:cc + CH]


def _window(buf, phases, cc, shift, r0):
    a, b = divmod(shift, 8)
    if b == 0:
        return buf[r0 + 8 * a:r0 + 8 * a + RC, cc:cc + CH]
    return phases[b - 1, r0 + 8 * a:r0 + 8 * a + RC, :]


def _conv_fwd(zb, glu_b, dw, dwb, lng, lnb, wpw, bpw, layer, name, carried=None):
    s = zb.shape[0]
    dc = dwb.shape[1]
    tm = min(s, 256)
    hb = tm // HALO

    def body(val_ref, gate_ref, valh_ref, gateh_ref, glub_ref, dw_ref, dwb_ref, lng_ref, lnb_ref, wpw_ref, bpw_ref,
             cv_ref, pw_ref, ubuf, uph):
        i = pl.program_id(0)
        bias = glub_ref[...]
        ubuf[HALO:, :] = _glu(val_ref[...], gate_ref[...], bias)
        uh = _glu(valh_ref[...], gateh_ref[...], bias)
        ubuf[0:HALO, :] = jnp.where(i > 0, uh, 0.0)
        for cc in range(0, dc, CH):
            _make_phases(ubuf, uph, cc)
            for r0 in range(0, tm, RC):
                acc = jnp.zeros((RC, CH), F32)
                for j in range(CONV_K):
                    acc = acc + _window(ubuf, uph, cc, HALO - (CONV_K - 1) + j, r0) * dw_ref[j:j + 1, cc:cc + CH]
                cv_ref[r0:r0 + RC, cc:cc + CH] = acc + dwb_ref[:, cc:cc + CH]
        cv = cv_ref[...]
        dv = cv - jnp.mean(cv, axis=1, keepdims=True)
        yl = dv * lax.rsqrt(jnp.mean(dv * dv, axis=1, keepdims=True) + EPS) * lng_ref[...] + lnb_ref[...]
        act = (yl * _sigmoid(yl)).astype(BF16)
        pw_ref[...] = _dot(act, wpw_ref[...]) + bpw_ref[...]

    return _call(
        body, (zb, zb, zb, zb, glu_b, dw, dwb, lng, lnb, wpw, bpw), name=name, grid=(s // tm,), carried=carried,
        in_specs=[pl.BlockSpec((tm, dc), lambda i: (i, 0)), pl.BlockSpec((tm, dc), lambda i: (i, 1)),
                  pl.BlockSpec((HALO, dc), lambda i: (jnp.maximum(i * hb - 1, 0), 0)),
                  pl.BlockSpec((HALO, dc), lambda i: (jnp.maximum(i * hb - 1, 0), 1)),
                  _row(2 * dc), pl.BlockSpec((None, HALO, dc), lambda i: (layer, 0, 0)), _row(dc), _row(dc), _row(dc),
                  pl.BlockSpec((None, dc, dc), lambda i: (layer, 0, 0)), _row(dc)],
        out_specs=[pl.BlockSpec((tm, dc), lambda i: (i, 0))] * 2,
        out_shape=[jax.ShapeDtypeStruct((s, dc), F32)] * 2,
        scratch=[pltpu.VMEM((tm + HALO, dc), F32), pltpu.VMEM((7, tm + HALO - PH_ROWS_LESS, CH), F32)])


def _silu_parts(z):
    sg = _sigmoid(z)
    return z * sg, sg * (1.0 + z * (1.0 - sg))


def _outproj_fwd(x, o, zb, pw, gate, wout, layer, name):
    s, d = x.shape
    dm = o.shape[1]
    tm = min(s, 256)

    def body(x_ref, o_ref, mg_ref, cg_ref, pw_ref, gate_ref, w_ref, xn_ref, y_ref, mix_ref):
        mg, cg = mg_ref[...], cg_ref[...]
        mix_ref[:, 0:dm] = (o_ref[...] * (mg * _sigmoid(mg))).astype(BF16)
        mix_ref[:, dm:] = (pw_ref[...] * (cg * _sigmoid(cg))).astype(BF16)
        yv = _dot(mix_ref[...], w_ref[...])
        y_ref[...] = yv
        xn_ref[...] = x_ref[...] + gate_ref[...] * yv

    tile = lambda n, j=0: pl.BlockSpec((tm, n), lambda i: (i, j))
    return pl.pallas_call(
        body, name=name, grid=(s // tm,),
        in_specs=[tile(d), tile(dm), tile(dm, 2), tile(dm, 3), tile(dm), _row(d),
                  pl.BlockSpec((None, 2 * dm, d), lambda i: (layer, 0, 0))],
        out_specs=[tile(d), tile(d), tile(2 * dm)],
        out_shape=[jax.ShapeDtypeStruct((s, d), F32), jax.ShapeDtypeStruct((s, d), F32), jax.ShapeDtypeStruct((s, 2 * dm), BF16)],
        compiler_params=_params("arbitrary"),
    )(x, o, zb, zb, pw, gate, wout)


def _grad_tn(a, b, name):
    s, n = a.shape
    m = b.shape[1]
    tn, ts = min(n, 1024), min(s, 512)

    def body(a_ref, b_ref, o_ref):
        @pl.when(pl.program_id(1) == 0)
        def _():
            o_ref[...] = jnp.zeros_like(o_ref)

        o_ref[...] += _dot(a_ref[...].astype(BF16), b_ref[...].astype(BF16), TN)

    return pl.pallas_call(
        body, name=name, grid=(n // tn, s // ts),
        in_specs=[pl.BlockSpec((ts, tn), lambda r, t: (t, r)), pl.BlockSpec((ts, m), lambda r, t: (t, 0))],
        out_specs=pl.BlockSpec((tn, m), lambda r, t: (r, 0)),
        out_shape=jax.ShapeDtypeStruct((n, m), F32), compiler_params=_params("arbitrary", "arbitrary"),
    )(a, b)


def _outproj_bwd(dxo, y, gate, wout, o, zb, pw, layer, name, carried=None):
    s, d = dxo.shape
    dm = o.shape[1]
    tm = min(s, 256)

    def body(dx_ref, y_ref, gate_ref, w_ref, o_ref, mg_ref, cg_ref, pw_ref,
             dgate_ref, dy_ref, do_ref, delta_ref, dzb_ref, dpw_ref):
        @pl.when(pl.program_id(0) == 0)
        def _():
            dgate_ref[...] = jnp.zeros_like(dgate_ref)

        dx = dx_ref[...]
        dgate_ref[...] += _sum0(dx * y_ref[...])
        dyb = (dx * gate_ref[...]).astype(BF16)
        dy_ref[...] = dyb
        dmix = _dot(dyb, w_ref[...], NT)
        da, db = dmix[:, :dm], dmix[:, dm:]
        ov = o_ref[...]
        silu_m, dsilu_m = _silu_parts(mg_ref[...])
        do = da * silu_m
        do_ref[...] = do.astype(BF16)
        prod = do * ov
        for h in range(N_HEADS):
            delta_ref[h] = _sum1(prod[:, h * VD:(h + 1) * VD])
        dzb_ref[:, 0:dm] = da * ov * dsilu_m
        silu_c, dsilu_c = _silu_parts(cg_ref[...])
        dpw_ref[...] = db * silu_c
        dzb_ref[:, dm:] = db * pw_ref[...] * dsilu_c

    tile = lambda n, j=0: pl.BlockSpec((tm, n), lambda i: (i, j))
    return _call(
        body, (dxo, y, gate, wout, o, zb, zb, pw), name=name, grid=(s // tm,), carried=carried,
        in_specs=[tile(d), tile(d), _row(d), pl.BlockSpec((None, 2 * dm, d), lambda i: (layer, 0, 0)),
                  tile(dm), tile(dm, 2), tile(dm, 3), tile(dm)],
        out_specs=[_row(d), tile(d), tile(dm), pl.BlockSpec((N_HEADS, tm, 1), lambda i: (0, i, 0)), tile(2 * dm, 1), tile(dm)],
        out_shape=[jax.ShapeDtypeStruct((1, d), F32), jax.ShapeDtypeStruct((s, d), BF16), jax.ShapeDtypeStruct((s, dm), BF16),
                   jax.ShapeDtypeStruct((N_HEADS, s, 1), F32), jax.ShapeDtypeStruct((s, 4 * dm), F32),
                   jax.ShapeDtypeStruct((s, dm), F32)])


def _flash_bwd(q, k, v, do, lse, delta, name, carried=None):
    s = q.shape[1]
    t = min(s, 1024)
    nq = s // t

    def body(q_ref, k_ref, v_ref, do_ref, lse_ref, delta_ref, dq_ref, dk_ref, dv_ref):
        j = pl.program_id(1)

        @pl.when(j == 0)
        def _():
            dq_ref[...] = jnp.zeros_like(dq_ref)

        kb, vb = k_ref[...], v_ref[...]
        row = lax.broadcasted_iota(jnp.int32, (t, t), 0)
        col = lax.broadcasted_iota(jnp.int32, (t, t), 1)

        def block(i):
            return pl.ds(pl.multiple_of(i * t, t), t)

        def scores(i):
            at = block(i)
            return _dot(q_ref[at, :], kb, NT), _dot(do_ref[at, :], vb, NT)

        def update(i, sc, dp, dk, dv, diagonal):
            at = block(i)
            p = jnp.exp(sc - lse_ref[at, :])
            if diagonal:
                p = jnp.where(col <= row, p, 0.0)
            dv = dv + _dot(p.astype(BF16), do_ref[at, :], TN)
            ds = (p * (dp - delta_ref[at, :])).astype(BF16)
            dq_ref[at, :] += _dot(ds, kb)
            return dk + _dot(ds, q_ref[at, :], TN), dv

        carry = update(j, *scores(j), jnp.zeros((t, HP), F32), jnp.zeros((t, VD), F32), True)
        dk, dv = lax.fori_loop(j + 1, nq, lambda i, cr: update(i, *scores(i), *cr, False), carry)
        dk_ref[...] = dk
        dv_ref[...] = dv

    whole = lambda n: pl.BlockSpec((None, s, n), lambda h, j: (h, 0, 0))
    blk = lambda n: pl.BlockSpec((None, t, n), lambda h, j: (h, j, 0))
    return _call(
        body, (q, k, v, do, lse, delta), name=name, grid=(N_HEADS, nq), carried=carried,
        in_specs=[whole(HP), blk(HP), blk(VD), pl.BlockSpec((s, VD), lambda h, j: (0, h)), whole(1), whole(1)],
        out_specs=[whole(HP), blk(HP), blk(VD)],
        out_shape=[jax.ShapeDtypeStruct((N_HEADS, s, HP), F32), jax.ShapeDtypeStruct((N_HEADS, s, HP), F32),
                   jax.ShapeDtypeStruct((N_HEADS, s, VD), F32)])


def _mla_prep_bwd(dq, dk, dv, za, zkr, cos, sin, wq, wkv, gql, gkvl, gq2, gk2, layer, name):
    s = za.shape[0]
    tm = min(s, 512)

    def norm_bwd(n, yv, rstd, gain, d_n_out, d_y_out):
        tn_, ty = n * rstd, yv * rstd
        dgain_n, dgain_y = _sum0(d_n_out * tn_), _sum0(d_y_out * ty)
        dtn, dty = d_n_out * gain[:, :NOPE], d_y_out * gain[:, NOPE:]
        a = (_sum1(dtn * n) + _sum1(dty * yv)) * (rstd * rstd * rstd * (1.0 / QK))
        return rstd * dtn - n * a, rstd * dty - (0.5 * yv) * a, dgain_n, dgain_y

    def rope_bwd(d_out, cos, sin):
        return d_out * cos + pltpu.roll(d_out * sin, 128 - 32, axis=1)

    def latent_bwd(z, gain, dn):
        rstd = lax.rsqrt(jnp.mean(z * z, axis=1, keepdims=True) + EPS)
        zh = z * rstd
        dzh = dn * gain
        return rstd * (dzh - zh * jnp.mean(dzh * zh, axis=1, keepdims=True)), _sum0(dn * zh)

    def body(dq_ref, dk_ref, dv_ref, za_ref, zkr_ref, cos_ref, sin_ref, wq_ref, wkv_ref, gql_ref, gkvl_ref, gq_ref, gk_ref,
             dza_ref, dzkr_ref, gwq_ref, gwkv_ref, dgql_ref, dgkvl_ref, dgq_ref, dgk_ref):
        @pl.when(pl.program_id(0) == 0)
        def _():
            for r in (gwq_ref, gwkv_ref, dgql_ref, dgkvl_ref, dgq_ref, dgk_ref):
                r[...] = jnp.zeros_like(r)

        zq, zkv = za_ref[:, 0:QL], za_ref[:, QL:QL + KVL]
        qf = zq * lax.rsqrt(jnp.mean(zq * zq, axis=1, keepdims=True) + EPS) * gql_ref[...]
        kvf = zkv * lax.rsqrt(jnp.mean(zkv * zkv, axis=1, keepdims=True) + EPS) * gkvl_ref[...]
        qn, kvn = qf.astype(BF16), kvf.astype(BF16)
        qn_t, kvn_t = qf.T.astype(BF16), kvf.T.astype(BF16)
        kr = zkr_ref[...]
        kr_ss = 0.5 * _sum1(kr * kr)
        cos, sin = cos_ref[...], sin_ref[...]
        gq, gk = gq_ref[...], gk_ref[...]
        dkr = jnp.zeros((tm, 128), F32)
        qr_all, kvr_all = _dot(qn, wq_ref[...]), _dot(kvn, wkv_ref[...])
        dqr_all, dkvr_all = [], []
        for h in range(N_HEADS):
            qr = qr_all[:, h * HP:(h + 1) * HP]
            n, yv = qr[:, :NOPE], qr[:, NOPE:]
            rstd = lax.rsqrt((_sum1(n * n) + 0.5 * _sum1(yv * yv)) * (1.0 / QK) + EPS)
            dqh = dq_ref[h] * SCORE_SCALE
            dn, dy, dg_n, dg_y = norm_bwd(n, yv, rstd, gq, dqh[:, :NOPE], rope_bwd(dqh[:, NOPE:], cos, sin))
            dgq_ref[:, 0:NOPE] += dg_n
            dgq_ref[:, NOPE:] += dg_y
            dqr_all += [dn.astype(BF16), dy.astype(BF16)]
            kn = kvr_all[:, h * HP:h * HP + NOPE]
            rstd = lax.rsqrt((_sum1(kn * kn) + kr_ss) * (1.0 / QK) + EPS)
            dkh = dk_ref[h]
            dn, dy, dg_n, dg_y = norm_bwd(kn, kr, rstd, gk, dkh[:, :NOPE], rope_bwd(dkh[:, NOPE:], cos, sin))
            dgk_ref[:, 0:NOPE] += dg_n
            dgk_ref[:, NOPE:] += dg_y
            dkr = dkr + dy
            dkvr_all += [dn.astype(BF16), dv_ref[h].astype(BF16)]

        dqr_all, dkvr_all = jnp.concatenate(dqr_all, axis=1), jnp.concatenate(dkvr_all, axis=1)
        gwq_ref[...] += _dot(qn_t, dqr_all)
        gwkv_ref[...] += _dot(kvn_t, dkvr_all)
        dzq, dgql = latent_bwd(zq, gql_ref[...], _dot(dqr_all, wq_ref[...], NT))
        dzkv, dgkvl = latent_bwd(zkv, gkvl_ref[...], _dot(dkvr_all, wkv_ref[...], NT))
        dgql_ref[...] += dgql
        dgkvl_ref[...] += dgkvl
        dza_ref[:, 0:QL] = dzq
        dza_ref[:, QL:] = dzkv
        lane = lax.broadcasted_iota(jnp.int32, (tm, 128), 1)
        dzkr_ref[...] = jnp.where(lane < ROPE, dkr + pltpu.roll(dkr, 64, axis=1), 0.0)

    tile = lambda n: pl.BlockSpec((tm, n), lambda i: (i, 0))
    heads = lambda n: pl.BlockSpec((N_HEADS, tm, n), lambda i: (0, i, 0))
    return pl.pallas_call(
        body, name=name, grid=(s // tm,),
        in_specs=[heads(HP), heads(HP), heads(VD), tile(768), tile(128), tile(128), tile(128),
                  pl.BlockSpec((None, QL, N_HEADS * HP), lambda i: (layer, 0, 0)),
                  pl.BlockSpec((None, KVL, N_HEADS * HP), lambda i: (layer, 0, 0)),
                  _row(QL), _row(KVL), _row(HP), _row(HP)],
        out_specs=[tile(768), tile(128), pl.BlockSpec((QL, N_HEADS * HP), lambda i: (0, 0)),
                   pl.BlockSpec((KVL, N_HEADS * HP), lambda i: (0, 0)), _row(QL), _row(KVL), _row(HP), _row(HP)],
        out_shape=[jax.ShapeDtypeStruct((s, 768), F32), jax.ShapeDtypeStruct((s, 128), F32),
                   jax.ShapeDtypeStruct((QL, N_HEADS * HP), F32), jax.ShapeDtypeStruct((KVL, N_HEADS * HP), F32),
                   jax.ShapeDtypeStruct((1, QL), F32), jax.ShapeDtypeStruct((1, KVL), F32),
                   jax.ShapeDtypeStruct((1, HP), F32), jax.ShapeDtypeStruct((1, HP), F32)],
        compiler_params=_params("arbitrary"),
    )(dq, dk, dv, za, zkr, cos, sin, wq, wkv, gql, gkvl, gq2, gk2)


def _pointwise_bwd(dpw, cv, lng, lnb, wpw, layer, name):
    s, dc = cv.shape
    tm = min(s, 256)

    def body(dpw_ref, cv_ref, lng_ref, lnb_ref, w_ref, dcv_ref, act_ref, dbpw_ref, dlng_ref, dlnb_ref):
        @pl.when(pl.program_id(0) == 0)
        def _():
            for r in (dbpw_ref, dlng_ref, dlnb_ref):
                r[...] = jnp.zeros_like(r)

        cv = cv_ref[...]
        dv = cv - jnp.mean(cv, axis=1, keepdims=True)
        rstd = lax.rsqrt(jnp.mean(dv * dv, axis=1, keepdims=True) + EPS)
        xh = dv * rstd
        yl = xh * lng_ref[...] + lnb_ref[...]
        silu, dsilu = _silu_parts(yl)
        act_ref[...] = silu.astype(BF16)
        dpw = dpw_ref[...]
        dbpw_ref[...] += _sum0(dpw)
        dyl = _dot(dpw.astype(BF16), w_ref[...], NT) * dsilu
        dlng_ref[...] += _sum0(dyl * xh)
        dlnb_ref[...] += _sum0(dyl)
        dxh = dyl * lng_ref[...]
        dcv_ref[...] = rstd * (dxh - jnp.mean(dxh, axis=1, keepdims=True) - xh * jnp.mean(dxh * xh, axis=1, keepdims=True))

    tile = pl.BlockSpec((tm, dc), lambda i: (i, 0))
    return pl.pallas_call(
        body, name=name, grid=(s // tm,),
        in_specs=[tile, tile, _row(dc), _row(dc), pl.BlockSpec((None, dc, dc), lambda i: (layer, 0, 0))],
        out_specs=[tile, tile, _row(dc), _row(dc), _row(dc)],
        out_shape=[jax.ShapeDtypeStruct((s, dc), F32), jax.ShapeDtypeStruct((s, dc), BF16)] + [jax.ShapeDtypeStruct((1, dc), F32)] * 3,
        compiler_params=_params("arbitrary"),
    )(dpw, cv, lng, lnb, wpw)


def _conv_bwd(dcv, zb, dzb, glu_b, dw, layer, name):
    s, dc = dcv.shape
    tm = min(s, 256)
    hb = tm // HALO
    last = s // tm - 1

    def body(dcv_ref, dcvn_ref, val_ref, gate_ref, valh_ref, gateh_ref, glub_ref, dw_ref, _, dzb_ref, gdw_ref, ddwb_ref, dglub_ref,
             ubuf, dbuf, gacc, uph, dph):
        i = pl.program_id(0)

        @pl.when(i == 0)
        def _():
            gacc[...] = jnp.zeros_like(gacc)
            ddwb_ref[...] = jnp.zeros_like(ddwb_ref)
            dglub_ref[...] = jnp.zeros_like(dglub_ref)

        bias = glub_ref[...]
        ubuf[HALO:, :] = _glu(val_ref[...], gate_ref[...], bias)
        ubuf[0:HALO, :] = jnp.where(i > 0, _glu(valh_ref[...], gateh_ref[...], bias), 0.0)
        dcv = dcv_ref[...]
        dbuf[0:tm, :] = dcv
        dbuf[tm:, :] = jnp.where(i < last, dcvn_ref[...], 0.0)
        ddwb_ref[...] += _sum0(dcv)
        for cc in range(0, dc, CH):
            _make_phases(ubuf, uph, cc)
            _make_phases(dbuf, dph, cc)
            for r0 in range(0, tm, RC):
                du = jnp.zeros((RC, CH), F32)
                dpiece = dbuf[r0:r0 + RC, cc:cc + CH]
                for j in range(CONV_K):
                    du = du + _window(dbuf, dph, cc, (CONV_K - 1) - j, r0) * dw_ref[j:j + 1, cc:cc + CH]
                    win = _window(ubuf, uph, cc, HALO - (CONV_K - 1) + j, r0)
                    gacc[j, :, cc:cc + CH] += (dpiece * win).reshape(RC // 8, 8, CH).sum(axis=0)
                a = val_ref[r0:r0 + RC, cc:cc + CH] + bias[:, cc:cc + CH]
                sg = _sigmoid(gate_ref[r0:r0 + RC, cc:cc + CH] + bias[:, dc + cc:dc + cc + CH])
                dzb_ref[r0:r0 + RC, cc:cc + CH] = du * sg
                dzb_ref[r0:r0 + RC, dc + cc:dc + cc + CH] = du * a * sg * (1.0 - sg)
        dglub_ref[...] += _sum0(dzb_ref[...])

        @pl.when(i == last)
        def _():
            total = jnp.sum(gacc[...], axis=1)
            for cc in range(0, dc, CH):
                gdw_ref[cc // CH] = total[:, cc:cc + CH]

    return pl.pallas_call(
        body, name=name, grid=(s // tm,),
        in_specs=[pl.BlockSpec((tm, dc), lambda i: (i, 0)),
                  pl.BlockSpec((HALO, dc), lambda i: (jnp.minimum((i + 1) * hb, s // HALO - 1), 0)),
                  pl.BlockSpec((tm, dc), lambda i: (i, 0)), pl.BlockSpec((tm, dc), lambda i: (i, 1)),
                  pl.BlockSpec((HALO, dc), lambda i: (jnp.maximum(i * hb - 1, 0), 0)),
                  pl.BlockSpec((HALO, dc), lambda i: (jnp.maximum(i * hb - 1, 0), 1)),
                  _row(2 * dc), pl.BlockSpec((None, HALO, dc), lambda i: (layer, 0, 0)), ANY],
        out_specs=[pl.BlockSpec((tm, 2 * dc), lambda i: (i, 0)), pl.BlockSpec((4, HALO, CH), lambda i: (0, 0, 0)),
                   _row(dc), _row(2 * dc)],
        out_shape=[jax.ShapeDtypeStruct(dzb.shape, F32), jax.ShapeDtypeStruct((4, HALO, CH), F32),
                   jax.ShapeDtypeStruct((1, dc), F32), jax.ShapeDtypeStruct((1, 2 * dc), F32)],
        scratch_shapes=[pltpu.VMEM((tm + HALO, dc), F32), pltpu.VMEM((tm + HALO, dc), F32), pltpu.VMEM((HALO, 8, dc), F32),
                        pltpu.VMEM((7, tm + HALO - PH_ROWS_LESS, CH), F32), pltpu.VMEM((7, tm + HALO - PH_ROWS_LESS, CH), F32)],
        input_output_aliases={8: 0},
        compiler_params=_params("arbitrary"),
    )(dcv, dcv, zb, zb, zb, zb, glu_b, dw, dzb)


def _inproj_bwd(dza, dzkr, dzb, x, dxo, g, scale, shift, w_int, layer, name, carried=None):
    s, d = x.shape
    tm = min(s, 128)

    def body(dza_ref, dzkr_ref, dzb_ref, x_ref, dxo_ref, g_ref, sc_ref, sh_ref, w_hbm, dx_ref, dsh_ref, dgg_ref, w_vmem, sems):
        @pl.when(pl.program_id(0) == 0)
        def _():
            _load_w_in(w_hbm.at[layer], w_vmem, sems)
            dsh_ref[...] = jnp.zeros_like(dsh_ref)
            dgg_ref[...] = jnp.zeros_like(dgg_ref)

        dh = _dot(dza_ref[...].astype(BF16), w_vmem[0:768])
        dh = dh + _dot(dzkr_ref[...].astype(BF16), w_vmem[768:896])
        dh = dh + _dot(dzb_ref[...].astype(BF16), w_vmem[896:W_ROWS])
        xv = x_ref[...]
        rstd = lax.rsqrt(jnp.mean(xv * xv, axis=1, keepdims=True) + EPS)
        xh = xv * rstd
        dsh_ref[...] += _sum0(dh)
        dgg_ref[...] += _sum0(dh * xh)
        dxh = dh * (g_ref[...] * (1.0 + sc_ref[...]))
        dx_ref[...] = dxo_ref[...] + rstd * (dxh - xh * jnp.mean(dxh * xh, axis=1, keepdims=True))

    tile = lambda n: pl.BlockSpec((tm, n), lambda i: (i, 0))
    return _call(
        body, (dza, dzkr, dzb, x, dxo, g, scale, shift, w_int), name=name, grid=(s // tm,), carried=carried,
        in_specs=[tile(768), tile(128), tile(4096), tile(d), tile(d), _row(d), _row(d), _row(d), ANY],
        out_specs=[tile(d), _row(d), _row(d)],
        out_shape=[jax.ShapeDtypeStruct((s, d), F32), jax.ShapeDtypeStruct((1, d), F32), jax.ShapeDtypeStruct((1, d), F32)],
        scratch=[pltpu.VMEM((W_ROWS, d), BF16), pltpu.SemaphoreType.DMA((len(W_PIECES),))])


def _grad_w_in(dza, dzkr, dzb, hb, name, carried=None):
    s, d = hb.shape
    ts = min(s, 512)
    nt = s // ts
    tiles = ((0, 768), (768, 64), (1856, 1024), (2880, 1024), (832, 1024), (3904, 1024))

    def body(a_ref, kr_ref, b_ref, h_ref, o_hbm, acc, sem):
        r, t = pl.program_id(0), pl.program_id(1)

        @pl.when(t == 0)
        def _():
            acc[...] = jnp.zeros_like(acc)

        hv = h_ref[...]

        @pl.when(r == 0)
        def _():
            acc[0:768, :] += _dot(a_ref[...].astype(BF16), hv, TN)

        @pl.when(r == 1)
        def _():
            acc[0:128, :] += _dot(kr_ref[...].astype(BF16), hv, TN)

        @pl.when(r >= 2)
        def _():
            acc[...] += _dot(b_ref[...].astype(BF16), hv, TN)

        for tile, (row0, rows) in enumerate(tiles):
            @pl.when((t == nt - 1) & (r == tile))
            def _():
                cp = pltpu.make_async_copy(acc.at[pl.ds(0, rows)], o_hbm.at[pl.ds(row0, rows)], sem)
                cp.start()
                cp.wait()

    return _call(
        body, (dza, dzkr, dzb, hb), name=name, grid=(len(tiles), nt), carried=carried,
        in_specs=[pl.BlockSpec((ts, 768), lambda r, t: (jnp.where(r == 0, t, nt - 1), 0)),
                  pl.BlockSpec((ts, 128), lambda r, t: (jnp.where(r == 1, t, jnp.where(r == 0, 0, nt - 1)), 0)),
                  pl.BlockSpec((ts, 1024), lambda r, t: (jnp.where(r >= 2, t, 0), jnp.maximum(r - 2, 0))),
                  pl.BlockSpec((ts, d), lambda r, t: (t, 0))],
        out_specs=[ANY], out_shape=[jax.ShapeDtypeStruct((4928, d), F32)],
        scratch=[pltpu.VMEM((1024, d), F32), pltpu.SemaphoreType.DMA])


def _adamw(w, g, m, v):
    m = ADAM_B1 * m + (1.0 - ADAM_B1) * g
    v = ADAM_B2 * v + (1.0 - ADAM_B2) * (g * g)
    m_hat = m / (1.0 - ADAM_B1 ** ADAM_STEP)
    v_hat = v / (1.0 - ADAM_B2 ** ADAM_STEP)
    return -ADAM_LR * (m_hat / (jnp.sqrt(v_hat) + ADAM_EPS) + ADAM_WD * w), m, v


def _adam_update(w, g0, g1, m, v, name):
    _, r, c = w.shape
    fits = [t for t in range(8, r + 1, 8) if r % t == 0 and t * c * 4 <= (1 << 21)]
    tr = max(fits) if fits else r

    def body(w_ref, g0_ref, g1_ref, m_ref, v_ref, g_ref, d_ref, mo_ref, vo_ref):
        g = jnp.where(pl.program_id(0) == 0, g0_ref[...], g1_ref[...])
        g_ref[...] = g
        d_ref[...], mo_ref[...], vo_ref[...] = _adamw(w_ref[...], g, m_ref[...], v_ref[...])

    big = pl.BlockSpec((None, tr, c), lambda l, i: (l, i, 0))
    one = pl.BlockSpec((tr, c), lambda l, i: (i, 0))
    return pl.pallas_call(
        body, name=name, grid=(2, r // tr), in_specs=[big, one, one, big, big], out_specs=[big] * 4,
        out_shape=[jax.ShapeDtypeStruct(w.shape, F32)] * 4, compiler_params=_params("arbitrary", "arbitrary"),
    )(w, g0, g1, m, v)


def _ada_update(c_all, dmod, w, m, v, carried=None):
    nl, d, n = w.shape
    tr = 256

    def body(c_ref, dm_ref, w_ref, m_ref, v_ref, g_ref, d_ref, mo_ref, vo_ref):
        cv = c_ref[...]
        act = (cv * _sigmoid(cv)).astype(BF16)
        g = _dot(act, dm_ref[...].astype(BF16), TN)
        g_ref[...] = g
        d_ref[...], mo_ref[...], vo_ref[...] = _adamw(w_ref[...], g, m_ref[...], v_ref[...])

    big = pl.BlockSpec((None, tr, n), lambda l, i: (l, i, 0))
    return _call(
        body, (c_all, dmod, w, m, v), name="ada_w_update", grid=(nl, d // tr), carried=carried,
        in_specs=[pl.BlockSpec((8, tr), lambda l, i: (0, i)), pl.BlockSpec((None, 8, n), lambda l, i: (l, 0, 0)), big, big, big],
        out_specs=[big] * 4, out_shape=[jax.ShapeDtypeStruct(w.shape, F32)] * 4)


def _small_update(gathered, w, m, v):
    r = w.shape[0]

    def body(ga_ref, w_ref, m_ref, v_ref, g_ref, d_ref, mo_ref, vo_ref):
        g = ga_ref[0]
        for dev in range(1, 8):
            g = g + ga_ref[dev]
        g_ref[...] = g
        d_ref[...], mo_ref[...], vo_ref[...] = _adamw(w_ref[...], g, m_ref[...], v_ref[...])

    return pl.pallas_call(body, name="small_update", out_shape=[jax.ShapeDtypeStruct((r, 128), F32)] * 4,
                          compiler_params=_params())(gathered, w, m, v)


SMALL = (("ada_b", 6144), ("norm_g", 2048), ("q_lat_g", 512), ("kv_lat_g", 256), ("q_norm_g", 256), ("k_norm_g", 256),
         ("glu_b", 2048), ("dw_b", 1024), ("conv_ln_g", 1024), ("conv_ln_b", 1024), ("b_pw", 1024))


def _pack_small(vals):
    cols = []
    for name, width in SMALL:
        a = vals[name]
        if a.shape[1] < width:
            a = jnp.pad(a, ((0, 0), (0, width - a.shape[1])))
        cols.append(a)
    return jnp.concatenate(cols, axis=1).reshape(-1, 128)


def _unpack_small(packed, shapes):
    flat = packed.reshape(2, -1)
    out, at = {}, 0
    for name, width in SMALL:
        out[name] = flat[:, at:at + shapes[name]]
        at += width
    return out


def _dup_gain(g):
    return jnp.concatenate([g, g[NOPE:]])[None, :]


def _undup(g):
    return jnp.concatenate([g[..., :NOPE], g[..., NOPE:NOPE + ROPE] + g[..., NOPE + ROPE:]], axis=-1)


def kernel(x, c, positions, ada_w, ada_b, norm_g, w_in, q_lat_g, w_q_up, kv_lat_g, w_kv_up, q_norm_g, k_norm_g, glu_b, dw_w, dw_b, conv_ln_g, conv_ln_b, w_pw, b_pw, w_out, loss_target, m_ada_w, m_ada_b, m_norm_g, m_w_in, m_q_lat_g, m_w_q_up, m_kv_lat_g, m_w_kv_up, m_q_norm_g, m_k_norm_g, m_glu_b, m_dw_w, m_dw_b, m_conv_ln_g, m_conv_ln_b, m_w_pw, m_b_pw, m_w_out, v_ada_w, v_ada_b, v_norm_g, v_w_in, v_q_lat_g, v_w_q_up, v_kv_lat_g, v_w_kv_up, v_q_norm_g, v_k_norm_g, v_glu_b, v_dw_w, v_dw_b, v_conv_ln_g, v_conv_ln_b, v_w_pw, v_b_pw, v_w_out):
    nl = 2
    s, d = x.shape[1], x.shape[2]
    xi, yi, ci = lax.axis_index("x"), lax.axis_index("y"), lax.axis_index("c")
    shard = 2 * xi + yi
    me = 4 * xi + 2 * yi + ci
    cidx = jnp.reshape(ci, (1,)).astype(jnp.int32)
    jc = jnp.stack([shard, ci]).astype(jnp.int32)
    x0 = x.reshape(s, d)
    target = loss_target.reshape(s, d)

    c_all = _allgather8(c.reshape(8, d // 8), "gather_c").reshape(8, d)
    n_ada = ada_w.shape[2]
    ada_b_shard = lax.dynamic_slice_in_dim(ada_b, shard * n_ada, n_ada, axis=1)[:, None, :]
    mod_shard = _modulation(c_all, ada_w, ada_b_shard)
    mod_all = _allgather8(mod_shard.reshape(nl * 8, n_ada), "gather_mod")
    mod_rows = lax.dynamic_index_in_dim(mod_all.reshape(4, 2, nl, 8, n_ada)[:, 0], me, axis=2, keepdims=False)
    mod_me = jnp.transpose(mod_rows, (1, 0, 2)).reshape(nl, 3, 1, d)

    tr = lambda a: jnp.transpose(a, (0, 2, 1))
    w_in_t = tr(w_in).astype(BF16)
    wq = w_q_up.reshape(nl, QL, 2, QK)
    wq = jnp.concatenate([wq, wq[..., NOPE:]], axis=-1)
    wq = jnp.transpose(wq, (0, 2, 1, 3)).reshape(nl, 2 * QL, HP).astype(BF16)
    dw_pad = jnp.pad(dw_w, ((0, 0), (0, HALO - CONV_K), (0, 0)))
    local = [w_in_t, wq, w_kv_up.astype(BF16), dw_pad, w_pw.astype(BF16), w_out.astype(BF16)]

    def kernel_layouts(bufs):
        w_in_g, wq_g, wkv_g, dw_g, wpw_g, wout_g = bufs
        heads_side_by_side = lambda a, rows: jnp.transpose(a.reshape(1, -1, rows, a.shape[-1]), (0, 2, 1, 3)).reshape(1, rows, -1)
        return dict(w_in=w_in_g.reshape(1, 4 * w_in_g.shape[2], d), wq=heads_side_by_side(wq_g, QL), wkv=heads_side_by_side(wkv_g, KVL),
                    dw=jnp.transpose(dw_g, (0, 2, 1, 3)).reshape(1, HALO, 4 * dw_g.shape[3]),
                    wpw=wpw_g.reshape(1, 4 * wpw_g.shape[2], wpw_g.shape[3]), wout=wout_g.reshape(1, 4 * wout_g.shape[2], d))

    w_in0 = _run_alone(_gather_hand_on(_run_alone(_gather_start([local[0][0:1]]), "gather_w_in0")), "gather_w_in0_hand_on")
    rest0 = _gather_start([a[0:1] for a in local[1:]])
    next_gather = _gather_start([a[1:2] for a in local])
    wts = [None] * nl

    cos, sin = _rope_tables(positions.reshape(s, 1))
    row = lambda a, l: a[l][None, :]

    saved = []
    xl = x0
    for l in range(nl):
        shift, scale, gate = mod_me[l, 0], mod_me[l, 1], mod_me[l, 2]
        if l == 0:
            (hb, za, zkr, zb), landed = _inproj_fwd(xl, row(norm_g, l), scale, shift, w_in0[0].reshape(1, -1, d), 0,
                                                    f"inproj_fwd{l}", carried=rest0)
            wts[0] = kernel_layouts(w_in0 + _run_alone(_gather_hand_on(landed), "gather_rest0_hand_on"))
        else:
            hb, za, zkr, zb = _inproj_fwd(xl, row(norm_g, l), scale, shift, wts[l]["w_in"], 0, f"inproj_fwd{l}")
        w = wts[l]
        gains =(row(q_lat_g, l), row(kv_lat_g, l), _dup_gain(q_norm_g[l]), _dup_gain(k_norm_g[l]))
        q, k, v = _mla_prep_fwd(za, zkr, cos, sin, w["wq"], w["wkv"], *gains, 0, f"mla_prep_fwd{l}")
        conv_args = (zb, row(glu_b, l), w["dw"], row(dw_b, l), row(conv_ln_g, l), row(conv_ln_b, l), w["wpw"], row(b_pw, l), 0)
        if l == 0:
            (o, lse), landed = _flash_fwd(q, k, v, f"flash_fwd{l}", carried=next_gather)
            (cv, pw), gathered1 = _conv_fwd(*conv_args, f"conv_fwd{l}", carried=_gather_hand_on(landed))
            wts[1] = kernel_layouts(gathered1)
        else:
            o, lse = _flash_fwd(q, k, v, f"flash_fwd{l}")
            cv, pw = _conv_fwd(*conv_args, f"conv_fwd{l}")
        xn, yv, mixb = _outproj_fwd(xl, o, zb, pw, gate, w["wout"], 0, f"outproj_fwd{l}")
        saved.append(dict(x=xl, hb=hb, za=za, zkr=zkr, zb=zb, q=q, k=k, v=v, o=o, lse=lse, cv=cv, pw=pw, y=yv, mixb=mixb, gains=gains))
        xl = xn

    tok_loss, dx = _loss_head(xl, target)
    loss = lax.psum(jnp.sum(tok_loss), ("x", "y", "c"))

    big = [None] * nl
    small = [None] * nl
    shards_of = lambda gs: [g.reshape(4, g.shape[0] // 4, g.shape[1]) for g in gs]
    pair_sums = lambda l, parts, theirs: [_pair_sum(p, t, cidx, f"pair_sum{l}_{e}") for e, (p, t) in enumerate(zip(parts, theirs))]
    chip_sums = lambda l, sums, landed: [_chip_sum(sm, ld, jc, f"chip_sum{l}_{e}") for e, (sm, ld) in enumerate(zip(sums, landed))]
    halves = [None] * nl
    for l in reversed(range(nl)):
        sv, w = saved[l], wts[l]
        shift, scale, gate = mod_me[l, 0], mod_me[l, 1], mod_me[l, 2]
        out_args = (dx, sv["y"], gate, w["wout"], sv["o"], sv["zb"], sv["pw"], 0, f"outproj_bwd{l}")
        attn_args = (sv["q"], sv["k"], sv["v"])
        if l == 0:
            parts = shards_of(big[1])
            (dgate, dyb, do, delta, dzb, dpw), theirs = _outproj_bwd(*out_args, carried=_pair_exchange(parts))
            sums = pair_sums(1, parts, theirs)
            (dq, dk, dv), landed = _flash_bwd(*attn_args, do, sv["lse"], delta, f"flash_bwd{l}", carried=_chip_scatter(sums))
            halves[1] = chip_sums(1, sums, landed)
        else:
            dgate, dyb, do, delta, dzb, dpw = _outproj_bwd(*out_args)
            dq, dk, dv = _flash_bwd(*attn_args, do, sv["lse"], delta, f"flash_bwd{l}")
        g_out = _grad_tn(sv["mixb"], dyb, f"grad_w_out{l}")
        dza, dzkr, g_q, g_kv, dgql, dgkvl, dgq, dgk = _mla_prep_bwd(
            dq, dk, dv, sv["za"], sv["zkr"], cos, sin, w["wq"], w["wkv"], *sv["gains"], 0, f"mla_prep_bwd{l}")
        dcv, act, dbpw, dlng, dlnb = _pointwise_bwd(dpw, sv["cv"], row(conv_ln_g, l), row(conv_ln_b, l), w["wpw"], 0, f"pointwise_bwd{l}")
        g_pw = _grad_tn(act, dpw, f"grad_w_pw{l}")
        dzb, g_dw, ddwb, dglub = _conv_bwd(dcv, sv["zb"], dzb, row(glu_b, l), w["dw"], 0, f"conv_bwd{l}")
        by_shard = lambda g, n: jnp.transpose(g.reshape(g.shape[0], -1, n), (1, 0, 2)).reshape(-1, n)
        rest = [by_shard(g_q, HP), by_shard(g_kv, 512), g_dw.reshape(4 * HALO, CH), g_pw, g_out]
        in_args = (dza, dzkr, dzb, sv["x"], dx, row(norm_g, l), scale, shift, w["w_in"], 0, f"inproj_bwd{l}")
        if l == 0:
            parts = shards_of(rest)
            (dx, dshift, dgg), theirs = _inproj_bwd(*in_args, carried=_pair_exchange(parts))
            sums = pair_sums("0r", parts, theirs)
            (g_in,), landed = _grad_w_in(dza, dzkr, dzb, sv["hb"], f"grad_w_in{l}", carried=_chip_scatter(sums))
            rest_halves = chip_sums("0r", sums, landed)
        else:
            dx, dshift, dgg = _inproj_bwd(*in_args)
            g_in, = _grad_w_in(dza, dzkr, dzb, sv["hb"], f"grad_w_in{l}")
        big[l] = [g_in] + rest
        small[l] = dict(ada_b=jnp.concatenate([dshift, dgg * row(norm_g, l), dgate], axis=1), norm_g=dgg * (1.0 + scale),
                        q_lat_g=dgql, kv_lat_g=dgkvl, q_norm_g=_undup(dgq), k_norm_g=_undup(dgk), glu_b=dglub, dw_b=ddwb,
                        conv_ln_g=dlng, conv_ln_b=dlnb, b_pw=dbpw)
    grad_x = dx.reshape(x.shape)

    names = [n for n, _ in SMALL]
    mine = _pack_small({n: jnp.concatenate([small[0][n], small[1][n]], axis=0) for n in names})
    gathered = _allgather8(mine, "gather_small")
    weights = dict(ada_b=ada_b, norm_g=norm_g, q_lat_g=q_lat_g, kv_lat_g=kv_lat_g, q_norm_g=q_norm_g, k_norm_g=k_norm_g,
                   glu_b=glu_b, dw_b=dw_b, conv_ln_g=conv_ln_g, conv_ln_b=conv_ln_b, b_pw=b_pw)
    m_small = dict(ada_b=m_ada_b, norm_g=m_norm_g, q_lat_g=m_q_lat_g, kv_lat_g=m_kv_lat_g, q_norm_g=m_q_norm_g, k_norm_g=m_k_norm_g,
                   glu_b=m_glu_b, dw_b=m_dw_b, conv_ln_g=m_conv_ln_g, conv_ln_b=m_conv_ln_b, b_pw=m_b_pw)
    v_small = dict(ada_b=v_ada_b, norm_g=v_norm_g, q_lat_g=v_q_lat_g, kv_lat_g=v_kv_lat_g, q_norm_g=v_q_norm_g, k_norm_g=v_k_norm_g,
                   glu_b=v_glu_b, dw_b=v_dw_b, conv_ln_g=v_conv_ln_g, conv_ln_b=v_conv_ln_b, b_pw=v_b_pw)
    widths = {n: weights[n].shape[1] for n in names}
    v_packed = _pack_small({n: jnp.pad(v_small[n], ((0, 0), (0, dict(SMALL)[n] - widths[n])), constant_values=1.0) for n in names})
    small_out = [_unpack_small(a, widths) for a in _small_update(gathered, _pack_small(weights), _pack_small(m_small), v_packed)]

    ada_rows = gathered.reshape(8, nl, -1)[:, :, :3 * d]
    dmod = lax.dynamic_slice_in_dim(jnp.transpose(ada_rows, (1, 0, 2)), shard * n_ada, n_ada, axis=2)
    parts = shards_of(big[0][:1])
    sums = pair_sums("0i", parts, _run_alone(_pair_exchange(parts), "pair_exchange_w_in0"))
    ada_out = _ada_update(c_all, dmod, ada_w, m_ada_w, v_ada_w)

    halves[0] = chip_sums("0i", sums, _run_alone(_chip_scatter(sums), "chip_scatter_w_in0")) + rest_halves
    full = _run_alone(_pair_complete(halves[0] + halves[1]), "pair_complete")
    per_layer = [full[l * 6:(l + 1) * 6] for l in range(nl)]

    def natural_q(g):
        return jnp.transpose(_undup(g.reshape(2, QL, HP)), (1, 0, 2)).reshape(QL, 2 * QK)

    grads = [[per_layer[l][0], natural_q(per_layer[l][1]), per_layer[l][2], per_layer[l][3][:CONV_K], per_layer[l][4], per_layer[l][5]]
             for l in range(nl)]
    sharded = (("w_in", tr(w_in), tr(m_w_in), tr(v_w_in)), ("w_q_up", w_q_up, m_w_q_up, v_w_q_up),
               ("w_kv_up", w_kv_up, m_w_kv_up, v_w_kv_up), ("dw_w", dw_w, m_dw_w, v_dw_w),
               ("w_pw", w_pw, m_w_pw, v_w_pw), ("w_out", w_out, m_w_out, v_w_out))
    big_out = {name: _adam_update(w, grads[0][e], grads[1][e], m, v, f"adam_{name}") for e, (name, w, m, v) in enumerate(sharded)}
    big_out["w_in"] = [tr(a) for a in big_out["w_in"]]

    order = ["ada_w", "ada_b", "norm_g", "w_in", "q_lat_g", "w_q_up", "kv_lat_g", "w_kv_up", "q_norm_g", "k_norm_g", "glu_b",
             "dw_w", "dw_b", "conv_ln_g", "conv_ln_b", "w_pw", "b_pw", "w_out"]

    def leaf(kind, name):
        if name == "ada_w":
            return ada_out[kind]
        if name in big_out:
            return big_out[name][kind]
        return small_out[kind][name]

    return (loss, grad_x, *[leaf(kind, name) for kind in range(4) for name in order])
```

```python
import functools
import math

import jax
import jax.numpy as jnp
from jax import lax
from jax.experimental import pallas as pl
from jax.experimental.pallas import tpu as pltpu

F32, BF16 = jnp.float32, jnp.bfloat16
MESH = pl.DeviceIdType.MESH
ANY = pl.BlockSpec(memory_space=pl.ANY)

N_HEADS, NOPE, ROPE, VD = 8, 128, 64, 128
QK = NOPE + ROPE
QL, KVL = 512, 256
HP = 256
CONV_K, HALO = 31, 32
ROPE_THETA = 10000.0
EPS = 1e-6
ADAM_LR, ADAM_B1, ADAM_B2, ADAM_EPS, ADAM_WD, ADAM_STEP = 0.001, 0.9, 0.999, 1e-08, 0.01, 10
V7X_VMEM_LIMIT = 56 * 1024 * 1024

NT = (((1,), (1,)), ((), ()))
TN = (((0,), (0,)), ((), ()))
NN = (((1,), (0,)), ((), ()))


def _dot(a, b, dims=NN):
    return lax.dot_general(a, b, dims, preferred_element_type=F32)


def _params(*sem):
    return pltpu.CompilerParams(dimension_semantics=sem or None, vmem_limit_bytes=V7X_VMEM_LIMIT)


def _sigmoid(x):
    return 1.0 / (1.0 + jnp.exp(-x))


def _sum0(x):
    return jnp.sum(x, axis=0, keepdims=True)


def _sum1(x):
    return jnp.sum(x, axis=1, keepdims=True)


def _row(n):
    return pl.BlockSpec((1, n), lambda *_: (0, 0))


def _place():
    x, y, c = lax.axis_index("x"), lax.axis_index("y"), lax.axis_index("c")
    chips = [(1 - x, y), (x, 1 - y), (1 - x, 1 - y)]
    return x, y, c, chips


def _allgather8(v, name):
    r, n = v.shape

    def body(v_ref, out_ref, send_sems, recv_sems, local_sem):
        x, y, c, chips = _place()
        me, sibling = (x, y, c), (x, y, 1 - c)

        def slot(px, py, pc):
            return out_ref.at[4 * px + 2 * py + pc]

        def copy(k, block, to, src=None):
            return pltpu.make_async_remote_copy(
                src_ref=slot(*block) if src is None else src, dst_ref=slot(*block),
                send_sem=send_sems.at[k], recv_sem=recv_sems.at[k], device_id=to, device_id_type=MESH)

        mine = pltpu.make_async_copy(v_ref, slot(*me), local_sem)
        mine.start()
        first = [copy(0, me, sibling, src=v_ref)]
        first += [copy(1 + j, me, (*chip, c), src=v_ref) for j, chip in enumerate(chips)]
        for cp in first:
            cp.start()
        passed = [copy(4 + j, (*chip, c), sibling) for j, chip in enumerate(chips)]
        for j, chip in enumerate(chips):
            copy(1 + j, (*chip, c), me).wait_recv()
            passed[j].start()
        copy(0, sibling, me).wait_recv()
        for j, chip in enumerate(chips):
            copy(4 + j, (*chip, 1 - c), me).wait_recv()
        for cp in first + passed:
            cp.wait_send()
        mine.wait()

    return pl.pallas_call(
        body, name=name, out_shape=jax.ShapeDtypeStruct((8, r, n), v.dtype),
        in_specs=[pl.BlockSpec(memory_space=pltpu.VMEM)], out_specs=pl.BlockSpec(memory_space=pltpu.VMEM),
        scratch_shapes=[pltpu.SemaphoreType.DMA((7,)), pltpu.SemaphoreType.DMA((7,)), pltpu.SemaphoreType.DMA],
    )(v)


class _Carried:
    def __init__(self, operands, results, n_sems, start, finish, aliases=None):
        self.operands, self.results, self.n_sems = operands, results, n_sems
        self.start, self.finish, self.aliases = start, finish, aliases or {}


def _run_alone(carried, name):
    k = len(carried.operands)

    def body(*refs):
        args = (refs[:k], refs[k:k + len(carried.results)], refs[-2], refs[-1])
        carried.start(*args)
        carried.finish(*args)

    outs = pl.pallas_call(
        body, name=name, out_shape=carried.results, in_specs=[ANY] * k, out_specs=[ANY] * len(carried.results),
        input_output_aliases=carried.aliases,
        scratch_shapes=[pltpu.SemaphoreType.DMA((carried.n_sems,)), pltpu.SemaphoreType.DMA((carried.n_sems,))],
    )(*carried.operands)
    return list(outs)


def _call(body, operands, *, name, grid, in_specs, out_specs, out_shape, scratch=(), aliases=None, carried=None):
    params = _params(*(["arbitrary"] * len(grid)))
    n_in, n_out = len(in_specs), len(out_shape)
    if carried is None:
        return pl.pallas_call(body, name=name, grid=grid, in_specs=in_specs, out_specs=out_specs, out_shape=out_shape,
                              scratch_shapes=list(scratch), input_output_aliases=aliases or {}, compiler_params=params)(*operands)
    k_in, k_out = len(carried.operands), len(carried.results)

    def wrapped(*refs):
        ins, outs = refs[:n_in], refs[n_in + k_in:n_in + k_in + n_out]
        comm = (refs[n_in:n_in + k_in], refs[n_in + k_in + n_out:n_in + k_in + n_out + k_out], refs[-2], refs[-1])
        steps = [pl.program_id(a) for a in range(len(grid))]
        first = functools.reduce(jnp.logical_and, [s == 0 for s in steps])
        last = functools.reduce(jnp.logical_and, [s == g - 1 for s, g in zip(steps, grid)])

        @pl.when(first)
        def _():
            carried.start(*comm)

        body(*ins, *outs, *refs[n_in + k_in + n_out + k_out:-2])

        @pl.when(last)
        def _():
            carried.finish(*comm)

    both = dict(aliases or {})
    both.update({n_in + i: n_out + o for i, o in carried.aliases.items()})
    res = pl.pallas_call(
        wrapped, name=name, grid=grid, in_specs=list(in_specs) + [ANY] * k_in, out_specs=list(out_specs) + [ANY] * k_out,
        out_shape=list(out_shape) + list(carried.results), input_output_aliases=both, compiler_params=params,
        scratch_shapes=list(scratch) + [pltpu.SemaphoreType.DMA((carried.n_sems,)), pltpu.SemaphoreType.DMA((carried.n_sems,))],
    )(*operands, *carried.operands)
    return list(res[:n_out]), list(res[n_out:])


def _gather_start(shards):
    ne = len(shards)
    per = 4

    def copies(srcs, dsts, send_sems, recv_sems):
        x, y, c, chips = _place()
        jme = 2 * x + y
        out = []
        for e in range(ne):
            half = srcs[e].shape[2] // 2
            own = pl.ds(pl.multiple_of(c * half, 128), half)
            for k, chip in enumerate(chips):
                out.append(pltpu.make_async_remote_copy(
                    src_ref=srcs[e].at[:, :, own], dst_ref=dsts[e].at[:, jme, :, own], send_sem=send_sems.at[per * e + k],
                    recv_sem=recv_sems.at[per * e + k], device_id=(*chip, c), device_id_type=MESH))
            out.append(pltpu.make_async_remote_copy(
                src_ref=srcs[e], dst_ref=dsts[e].at[:, jme], send_sem=send_sems.at[per * e + 3],
                recv_sem=recv_sems.at[per * e + 3], device_id=(x, y, 1 - c), device_id_type=MESH))
        return out

    def start(*a):
        for cp in copies(*a):
            cp.start()

    def finish(*a):
        for cp in copies(*a):
            cp.wait()

    results = [jax.ShapeDtypeStruct((s.shape[0], 4) + s.shape[1:], s.dtype) for s in shards]
    return _Carried(list(shards), results, per * ne, start, finish)


def _gather_hand_on(bufs):
    ne = len(bufs)

    def copy(e, k, dsts, send_sems, recv_sems, mine):
        x, y, c, chips = _place()
        px, py = chips[k]
        half = dsts[e].shape[3] // 2
        cols = pl.ds(pl.multiple_of((c if mine else 1 - c) * half, 128), half)
        part = dsts[e].at[:, 2 * px + py, :, cols]
        return pltpu.make_async_remote_copy(src_ref=part, dst_ref=part, send_sem=send_sems.at[3 * e + k],
                                            recv_sem=recv_sems.at[3 * e + k], device_id=(x, y, 1 - c), device_id_type=MESH)

    def start(srcs, dsts, send_sems, recv_sems):
        for e in range(ne):
            for k in range(3):
                copy(e, k, dsts, send_sems, recv_sems, True).start()

    def finish(srcs, dsts, send_sems, recv_sems):
        for e in range(ne):
            for k in range(3):
                copy(e, k, dsts, send_sems, recv_sems, True).wait_send()
                copy(e, k, dsts, send_sems, recv_sems, False).wait_recv()

    results = [jax.ShapeDtypeStruct(b.shape, b.dtype) for b in bufs]
    return _Carried(list(bufs), results, 3 * ne, start, finish, aliases={e: e for e in range(ne)})


def _pair_exchange(parts):
    ne = len(parts)

    def copies(srcs, dsts, send_sems, recv_sems):
        x, y, c, _ = _place()
        out = []
        for e in range(ne):
            half = srcs[e].shape[2] // 2
            theirs = pl.ds(pl.multiple_of((1 - c) * half, 128), half)
            out.append(pltpu.make_async_remote_copy(
                src_ref=srcs[e].at[:, :, theirs], dst_ref=dsts[e], send_sem=send_sems.at[e],
                recv_sem=recv_sems.at[e], device_id=(x, y, 1 - c), device_id_type=MESH))
        return out

    def start(*a):
        for cp in copies(*a):
            cp.start()

    def finish(*a):
        for cp in copies(*a):
            cp.wait()

    results = [jax.ShapeDtypeStruct(p.shape[:2] + (p.shape[2] // 2,), p.dtype) for p in parts]
    return _Carried(list(parts), results, ne, start, finish)


def _chip_scatter(sums):
    ne = len(sums)

    def copies(srcs, dsts, send_sems, recv_sems):
        x, y, c, chips = _place()
        return [pltpu.make_async_remote_copy(
                    src_ref=srcs[e].at[2 * px + py], dst_ref=dsts[e].at[k], send_sem=send_sems.at[3 * e + k],
                    recv_sem=recv_sems.at[3 * e + k], device_id=(px, py, c), device_id_type=MESH)
                for e in range(ne) for k, (px, py) in enumerate(chips)]

    def start(*a):
        for cp in copies(*a):
            cp.start()

    def finish(*a):
        for cp in copies(*a):
            cp.wait()

    results = [jax.ShapeDtypeStruct((3,) + s.shape[1:], s.dtype) for s in sums]
    return _Carried(list(sums), results, 3 * ne, start, finish)


def _pair_complete(grads):
    ne = len(grads)

    def copy(e, dsts, send_sems, recv_sems, mine):
        x, y, c, _ = _place()
        half = dsts[e].shape[1] // 2
        cols = pl.ds(pl.multiple_of((c if mine else 1 - c) * half, 128), half)
        return pltpu.make_async_remote_copy(
            src_ref=dsts[e].at[:, cols], dst_ref=dsts[e].at[:, cols], send_sem=send_sems.at[e],
            recv_sem=recv_sems.at[e], device_id=(x, y, 1 - c), device_id_type=MESH)

    def start(srcs, dsts, send_sems, recv_sems):
        for e in range(ne):
            copy(e, dsts, send_sems, recv_sems, True).start()

    def finish(srcs, dsts, send_sems, recv_sems):
        for e in range(ne):
            copy(e, dsts, send_sems, recv_sems, True).wait_send()
            copy(e, dsts, send_sems, recv_sems, False).wait_recv()

    results = [jax.ShapeDtypeStruct(g.shape, g.dtype) for g in grads]
    return _Carried(list(grads), results, ne, start, finish, aliases={e: e for e in range(ne)})


def _pair_sum(part, theirs, cidx, name):
    _, r, n = part.shape
    half = n // 2

    def body(c_ref, p_ref, t_ref, o_ref):
        o_ref[...] = (p_ref[...] + t_ref[...]).astype(BF16)

    gs = pltpu.PrefetchScalarGridSpec(
        num_scalar_prefetch=1, grid=(4,),
        in_specs=[pl.BlockSpec((1, r, half), lambda j, c: (j, 0, c[0])),
                  pl.BlockSpec((1, r, half), lambda j, c: (j, 0, 0))],
        out_specs=pl.BlockSpec((1, r, half), lambda j, c: (j, 0, 0)))
    return pl.pallas_call(body, name=name, grid_spec=gs, out_shape=jax.ShapeDtypeStruct((4, r, half), BF16),
                          compiler_params=_params("arbitrary"))(cidx, part, theirs)


def _chip_sum(sums, landed, jc, name):
    _, r, half = sums.shape

    def body(jc_ref, s_ref, l_ref, o_ref):
        acc = s_ref[0].astype(F32)
        for k in range(3):
            acc = acc + l_ref[k].astype(F32)
        o_ref[...] = acc

    gs = pltpu.PrefetchScalarGridSpec(
        num_scalar_prefetch=1, grid=(1,),
        in_specs=[pl.BlockSpec((1, r, half), lambda i, jc: (jc[0], 0, 0)),
                  pl.BlockSpec((3, r, half), lambda i, jc: (0, 0, 0))],
        out_specs=pl.BlockSpec((r, half), lambda i, jc: (0, jc[1])))
    return pl.pallas_call(body, name=name, grid_spec=gs, out_shape=jax.ShapeDtypeStruct((r, 2 * half), F32),
                          compiler_params=_params("arbitrary"))(jc, sums, landed)


def _rope_tables(pos):
    s = pos.shape[0]
    lane = jnp.arange(128)
    inv = 1.0 / (ROPE_THETA ** ((2 * (lane % 32)).astype(F32) / ROPE))
    keep = (lane < 64).astype(F32)
    sign = jnp.where(lane < 32, -1.0, 1.0).astype(F32) * keep
    consts = jnp.stack([inv.astype(F32), keep, sign])[:, None, :]

    def body(p_ref, k_ref, c_ref, s_ref):
        ang = p_ref[...].astype(F32) * k_ref[0]
        c_ref[...] = jnp.cos(ang) * k_ref[1]
        s_ref[...] = jnp.sin(ang) * k_ref[2]

    tm = min(s, 1024)
    return pl.pallas_call(
        body, name="rope_tables", grid=(s // tm,),
        in_specs=[pl.BlockSpec((tm, 1), lambda i: (i, 0)), pl.BlockSpec((3, 1, 128), lambda i: (0, 0, 0))],
        out_specs=[pl.BlockSpec((tm, 128), lambda i: (i, 0))] * 2,
        out_shape=[jax.ShapeDtypeStruct((s, 128), F32)] * 2, compiler_params=_params("arbitrary"),
    )(pos, consts)


def _modulation(c_all, ada_w, ada_b_shard):
    nl, d, n = ada_w.shape
    tn = 512

    def body(c_ref, w_ref, b_ref, o_ref):
        cv = c_ref[...]
        act = (cv * _sigmoid(cv)).astype(BF16)
        o_ref[...] = _dot(act, w_ref[...].astype(BF16)) + b_ref[...]

    return pl.pallas_call(
        body, name="modulation", grid=(nl, n // tn),
        in_specs=[pl.BlockSpec((8, d), lambda l, j: (0, 0)), pl.BlockSpec((None, d, tn), lambda l, j: (l, 0, j)),
                  pl.BlockSpec((None, 1, tn), lambda l, j: (l, 0, j))],
        out_specs=pl.BlockSpec((None, 8, tn), lambda l, j: (l, 0, j)),
        out_shape=jax.ShapeDtypeStruct((nl, 8, n), F32), compiler_params=_params("arbitrary", "arbitrary"),
    )(c_all, ada_w, ada_b_shard)


def _loss_head(xf, target):
    s, d = xf.shape
    tm = min(s, 512)

    def body(x_ref, t_ref, l_ref, dx_ref):
        err = x_ref[...] - t_ref[...]
        l_ref[...] = 0.5 * jnp.mean(err * err, axis=1, keepdims=True)
        dx_ref[...] = err * (1.0 / d)

    return pl.pallas_call(
        body, name="loss_head", grid=(s // tm,),
        in_specs=[pl.BlockSpec((tm, d), lambda i: (i, 0))] * 2,
        out_specs=[pl.BlockSpec((tm, 1), lambda i: (i, 0)), pl.BlockSpec((tm, d), lambda i: (i, 0))],
        out_shape=[jax.ShapeDtypeStruct((s, 1), F32), jax.ShapeDtypeStruct((s, d), F32)],
        compiler_params=_params("arbitrary"),
    )(xf, target)


W_ROWS = 4992
W_PIECES = ((0, 0, 832), (832, 768, 64), (896, 1856, 2048), (2944, 832, 1024), (3968, 3904, 1024))


def _load_w_in(w_hbm, w_vmem, sems):
    cps = [pltpu.make_async_copy(w_hbm.at[pl.ds(src, n)], w_vmem.at[pl.ds(dst, n)], sems.at[i])
           for i, (dst, src, n) in enumerate(W_PIECES)]
    for cp in cps:
        cp.start()
    for cp in cps:
        cp.wait()


def _inproj_fwd(x, g, scale, shift, w_int, layer, name, carried=None):
    s, d = x.shape
    tm = min(s, 256)

    def body(x_ref, g_ref, sc_ref, sh_ref, w_hbm, hb_ref, za_ref, zkr_ref, zb_ref, w_vmem, sems):
        @pl.when(pl.program_id(0) == 0)
        def _():
            _load_w_in(w_hbm.at[layer], w_vmem, sems)

        xv = x_ref[...]
        rstd = lax.rsqrt(jnp.mean(xv * xv, axis=1, keepdims=True) + EPS)
        h = (xv * rstd) * g_ref[...] * (1.0 + sc_ref[...]) + sh_ref[...]
        hb = h.astype(BF16)
        hb_ref[...] = hb
        za_ref[...] = _dot(hb, w_vmem[0:768], NT)
        zkr_ref[...] = _dot(hb, w_vmem[768:896], NT)
        zb_ref[...] = _dot(hb, w_vmem[896:W_ROWS], NT)

    return _call(
        body, (x, g, scale, shift, w_int), name=name, grid=(s // tm,), carried=carried,
        in_specs=[pl.BlockSpec((tm, d), lambda i: (i, 0)), _row(d), _row(d), _row(d), ANY],
        out_specs=[pl.BlockSpec((tm, d), lambda i: (i, 0)), pl.BlockSpec((tm, 768), lambda i: (i, 0)),
                   pl.BlockSpec((tm, 128), lambda i: (i, 0)), pl.BlockSpec((tm, 4096), lambda i: (i, 0))],
        out_shape=[jax.ShapeDtypeStruct((s, d), BF16), jax.ShapeDtypeStruct((s, 768), F32),
                   jax.ShapeDtypeStruct((s, 128), F32), jax.ShapeDtypeStruct((s, 4096), F32)],
        scratch=[pltpu.VMEM((W_ROWS, d), BF16), pltpu.SemaphoreType.DMA((len(W_PIECES),))])


def _rope(yv, cos, sin):
    return yv * cos + pltpu.roll(yv, 32, axis=1) * sin


def _mla_prep_fwd(za, zkr, cos, sin, wq, wkv, gql, gkvl, gq2, gk2, layer, name):
    s = za.shape[0]
    tm = min(s, 256)

    def body(za_ref, zkr_ref, cos_ref, sin_ref, wq_ref, wkv_ref, gql_ref, gkvl_ref, gq_ref, gk_ref, q_ref, k_ref, v_ref):
        zq, zkv = za_ref[:, 0:QL], za_ref[:, QL:QL + KVL]
        qn = (zq * lax.rsqrt(jnp.mean(zq * zq, axis=1, keepdims=True) + EPS) * gql_ref[...]).astype(BF16)
        kvn = (zkv * lax.rsqrt(jnp.mean(zkv * zkv, axis=1, keepdims=True) + EPS) * gkvl_ref[...]).astype(BF16)
        kr = zkr_ref[...]
        kr_ss = 0.5 * _sum1(kr * kr)
        cos, sin = cos_ref[...], sin_ref[...]
        gq, gk = gq_ref[...] * SCORE_SCALE, gk_ref[...]
        qr_all, kvr_all = _dot(qn, wq_ref[...]), _dot(kvn, wkv_ref[...])
        for h in range(N_HEADS):
            qr = qr_all[:, h * HP:(h + 1) * HP]
            n, yv = qr[:, :NOPE], qr[:, NOPE:]
            rstd = lax.rsqrt((_sum1(n * n) + 0.5 * _sum1(yv * yv)) * (1.0 / QK) + EPS)
            q_ref[h, :, 0:NOPE] = (n * rstd * gq[:, :NOPE]).astype(BF16)
            q_ref[h, :, NOPE:HP] = _rope(yv * rstd * gq[:, NOPE:], cos, sin).astype(BF16)
            kvr = kvr_all[:, h * HP:(h + 1) * HP]
            kn, vv = kvr[:, :NOPE], kvr[:, NOPE:]
            rstd = lax.rsqrt((_sum1(kn * kn) + kr_ss) * (1.0 / QK) + EPS)
            k_ref[h, :, 0:NOPE] = (kn * rstd * gk[:, :NOPE]).astype(BF16)
            k_ref[h, :, NOPE:HP] = _rope(kr * rstd * gk[:, NOPE:], cos, sin).astype(BF16)
            v_ref[h] = vv.astype(BF16)

    tile = lambda n: pl.BlockSpec((tm, n), lambda i: (i, 0))
    return pl.pallas_call(
        body, name=name, grid=(s // tm,),
        in_specs=[tile(768), tile(128), tile(128), tile(128),
                  pl.BlockSpec((None, QL, N_HEADS * HP), lambda i: (layer, 0, 0)),
                  pl.BlockSpec((None, KVL, N_HEADS * HP), lambda i: (layer, 0, 0)),
                  _row(QL), _row(KVL), _row(HP), _row(HP)],
        out_specs=[pl.BlockSpec((N_HEADS, tm, HP), lambda i: (0, i, 0)), pl.BlockSpec((N_HEADS, tm, HP), lambda i: (0, i, 0)),
                   pl.BlockSpec((N_HEADS, tm, VD), lambda i: (0, i, 0))],
        out_shape=[jax.ShapeDtypeStruct((N_HEADS, s, HP), BF16), jax.ShapeDtypeStruct((N_HEADS, s, HP), BF16),
                   jax.ShapeDtypeStruct((N_HEADS, s, VD), BF16)],
        compiler_params=_params("arbitrary"),
    )(za, zkr, cos, sin, wq, wkv, gql, gkvl, gq2, gk2)


SCORE_SCALE = 1.0 / math.sqrt(QK)
MASKED = -1e30


def _flash_fwd(q, k, v, name, carried=None):
    s = q.shape[1]
    t = min(s, 1024)

    def body(q_ref, k_ref, v_ref, o_ref, lse_ref):
        i = pl.program_id(1)
        qb = q_ref[...]
        row = lax.broadcasted_iota(jnp.int32, (t, t), 0)
        col = lax.broadcasted_iota(jnp.int32, (t, t), 1)

        def block(j):
            return pl.ds(pl.multiple_of(j * t, t), t)

        def scores(j):
            return _dot(qb, k_ref[block(j), :], NT)

        def update(j, sc, m, l, acc, diagonal):
            if diagonal:
                sc = jnp.where(col <= row, sc, MASKED)
            m_new = jnp.maximum(m, jnp.max(sc, axis=1, keepdims=True))
            p = jnp.exp(sc - m_new)
            alpha = jnp.exp(m - m_new)
            return m_new, alpha * l + _sum1(p), alpha * acc + _dot(p.astype(BF16), v_ref[block(j), :])

        init = (jnp.full((t, 1), MASKED, F32), jnp.zeros((t, 1), F32), jnp.zeros((t, VD), F32))
        carry = lax.fori_loop(0, i, lambda j, cr: update(j, scores(j), *cr, False), init)
        m, l, acc = update(i, scores(i), *carry, True)
        o_ref[...] = acc / l
        lse_ref[...] = m + jnp.log(l)

    return _call(
        body, (q, k, v), name=name, grid=(N_HEADS, s // t), carried=carried,
        in_specs=[pl.BlockSpec((None, t, HP), lambda h, i: (h, i, 0)), pl.BlockSpec((None, s, HP), lambda h, i: (h, 0, 0)),
                  pl.BlockSpec((None, s, VD), lambda h, i: (h, 0, 0))],
        out_specs=[pl.BlockSpec((t, VD), lambda h, i: (i, h)), pl.BlockSpec((None, t, 1), lambda h, i: (h, i, 0))],
        out_shape=[jax.ShapeDtypeStruct((s, N_HEADS * VD), F32), jax.ShapeDtypeStruct((N_HEADS, s, 1), F32)])


CH, RC = 256, 64


PH_ROWS_LESS = 8


def _glu(val, gate, bias):
    c = val.shape[1]
    return (val + bias[:, :c]) * _sigmoid(gate + bias[:, c:])


def _make_phases(buf, phases, cc):
    rows = buf.shape[0] - PH_ROWS_LESS
    for b in range(1, 8):
        phases[b - 1] = buf[pl.ds(b, rows), cc:cc + CH]


def _window(buf, phases, cc, shift, r0):
    a, b = divmod(shift, 8)
    if b == 0:
        return buf[r0 + 8 * a:r0 + 8 * a + RC, cc:cc + CH]
    return phases[b - 1, r0 + 8 * a:r0 + 8 * a + RC, :]


def _conv_fwd(zb, glu_b, dw, dwb, lng, lnb, wpw, bpw, layer, name, carried=None):
    s = zb.shape[0]
    dc = dwb.shape[1]
    tm = min(s, 256)
    hb = tm // HALO

    def body(val_ref, gate_ref, valh_ref, gateh_ref, glub_ref, dw_ref, dwb_ref, lng_ref, lnb_ref, wpw_ref, bpw_ref,
             cv_ref, pw_ref, ubuf, uph):
        i = pl.program_id(0)
        bias = glub_ref[...]
        ubuf[HALO:, :] = _glu(val_ref[...], gate_ref[...], bias)
        uh = _glu(valh_ref[...], gateh_ref[...], bias)
        ubuf[0:HALO, :] = jnp.where(i > 0, uh, 0.0)
        for cc in range(0, dc, CH):
            _make_phases(ubuf, uph, cc)
            for r0 in range(0, tm, RC):
                acc = jnp.zeros((RC, CH), F32)
                for j in range(CONV_K):
                    acc = acc + _window(ubuf, uph, cc, HALO - (CONV_K - 1) + j, r0) * dw_ref[j:j + 1, cc:cc + CH]
                cv_ref[r0:r0 + RC, cc:cc + CH] = acc + dwb_ref[:, cc:cc + CH]
        cv = cv_ref[...]
        dv = cv - jnp.mean(cv, axis=1, keepdims=True)
        yl = dv * lax.rsqrt(jnp.mean(dv * dv, axis=1, keepdims=True) + EPS) * lng_ref[...] + lnb_ref[...]
        act = (yl * _sigmoid(yl)).astype(BF16)
        pw_ref[...] = _dot(act, wpw_ref[...]) + bpw_ref[...]

    return _call(
        body, (zb, zb, zb, zb, glu_b, dw, dwb, lng, lnb, wpw, bpw), name=name, grid=(s // tm,), carried=carried,
        in_specs=[pl.BlockSpec((tm, dc), lambda i: (i, 0)), pl.BlockSpec((tm, dc), lambda i: (i, 1)),
                  pl.BlockSpec((HALO, dc), lambda i: (jnp.maximum(i * hb - 1, 0), 0)),
                  pl.BlockSpec((HALO, dc), lambda i: (jnp.maximum(i * hb - 1, 0), 1)),
                  _row(2 * dc), pl.BlockSpec((None, HALO, dc), lambda i: (layer, 0, 0)), _row(dc), _row(dc), _row(dc),
                  pl.BlockSpec((None, dc, dc), lambda i: (layer, 0, 0)), _row(dc)],
        out_specs=[pl.BlockSpec((tm, dc), lambda i: (i, 0))] * 2,
        out_shape=[jax.ShapeDtypeStruct((s, dc), F32)] * 2,
        scratch=[pltpu.VMEM((tm + HALO, dc), F32), pltpu.VMEM((7, tm + HALO - PH_ROWS_LESS, CH), F32)])


def _silu_parts(z):
    sg = _sigmoid(z)
    return z * sg, sg * (1.0 + z * (1.0 - sg))


def _outproj_fwd(x, o, zb, pw, gate, wout, layer, name, carried=None):
    s, d = x.shape
    dm = o.shape[1]
    tm = min(s, 256)

    def body(x_ref, o_ref, mg_ref, cg_ref, pw_ref, gate_ref, w_ref, xn_ref, y_ref, mix_ref):
        mg, cg = mg_ref[...], cg_ref[...]
        mix_ref[:, 0:dm] = (o_ref[...] * (mg * _sigmoid(mg))).astype(BF16)
        mix_ref[:, dm:] = (pw_ref[...] * (cg * _sigmoid(cg))).astype(BF16)
        yv = _dot(mix_ref[...], w_ref[...])
        y_ref[...] = yv
        xn_ref[...] = x_ref[...] + gate_ref[...] * yv

    tile = lambda n, j=0: pl.BlockSpec((tm, n), lambda i: (i, j))
    return _call(
        body, (x, o, zb, zb, pw, gate, wout), name=name, grid=(s // tm,), carried=carried,
        in_specs=[tile(d), tile(dm), tile(dm, 2), tile(dm, 3), tile(dm), _row(d),
                  pl.BlockSpec((None, 2 * dm, d), lambda i: (layer, 0, 0))],
        out_specs=[tile(d), tile(d), tile(2 * dm)],
        out_shape=[jax.ShapeDtypeStruct((s, d), F32), jax.ShapeDtypeStruct((s, d), F32), jax.ShapeDtypeStruct((s, 2 * dm), BF16)])


def _grad_tn(a, b, name):
    s, n = a.shape
    m = b.shape[1]
    tn, ts = min(n, 1024), min(s, 512)

    def body(a_ref, b_ref, o_ref):
        @pl.when(pl.program_id(1) == 0)
        def _():
            o_ref[...] = jnp.zeros_like(o_ref)

        o_ref[...] += _dot(a_ref[...].astype(BF16), b_ref[...].astype(BF16), TN)

    return pl.pallas_call(
        body, name=name, grid=(n // tn, s // ts),
        in_specs=[pl.BlockSpec((ts, tn), lambda r, t: (t, r)), pl.BlockSpec((ts, m), lambda r, t: (t, 0))],
        out_specs=pl.BlockSpec((tn, m), lambda r, t: (r, 0)),
        out_shape=jax.ShapeDtypeStruct((n, m), F32), compiler_params=_params("arbitrary", "arbitrary"),
    )(a, b)


def _outproj_bwd(dxo, y, gate, wout, o, zb, pw, layer, name, carried=None):
    s, d = dxo.shape
    dm = o.shape[1]
    tm = min(s, 256)

    def body(dx_ref, y_ref, gate_ref, w_ref, o_ref, mg_ref, cg_ref, pw_ref,
             dgate_ref, dy_ref, do_ref, delta_ref, dzb_ref, dpw_ref):
        @pl.when(pl.program_id(0) == 0)
        def _():
            dgate_ref[...] = jnp.zeros_like(dgate_ref)

        dx = dx_ref[...]
        dgate_ref[...] += _sum0(dx * y_ref[...])
        dyb = (dx * gate_ref[...]).astype(BF16)
        dy_ref[...] = dyb
        dmix = _dot(dyb, w_ref[...], NT)
        da, db = dmix[:, :dm], dmix[:, dm:]
        ov = o_ref[...]
        silu_m, dsilu_m = _silu_parts(mg_ref[...])
        do = da * silu_m
        do_ref[...] = do.astype(BF16)
        prod = do * ov
        for h in range(N_HEADS):
            delta_ref[h] = _sum1(prod[:, h * VD:(h + 1) * VD])
        dzb_ref[:, 0:dm] = da * ov * dsilu_m
        silu_c, dsilu_c = _silu_parts(cg_ref[...])
        dpw_ref[...] = db * silu_c
        dzb_ref[:, dm:] = db * pw_ref[...] * dsilu_c

    tile = lambda n, j=0: pl.BlockSpec((tm, n), lambda i: (i, j))
    return _call(
        body, (dxo, y, gate, wout, o, zb, zb, pw), name=name, grid=(s // tm,), carried=carried,
        in_specs=[tile(d), tile(d), _row(d), pl.BlockSpec((None, 2 * dm, d), lambda i: (layer, 0, 0)),
                  tile(dm), tile(dm, 2), tile(dm, 3), tile(dm)],
        out_specs=[_row(d), tile(d), tile(dm), pl.BlockSpec((N_HEADS, tm, 1), lambda i: (0, i, 0)), tile(2 * dm, 1), tile(dm)],
        out_shape=[jax.ShapeDtypeStruct((1, d), F32), jax.ShapeDtypeStruct((s, d), BF16), jax.ShapeDtypeStruct((s, dm), BF16),
                   jax.ShapeDtypeStruct((N_HEADS, s, 1), F32), jax.ShapeDtypeStruct((s, 4 * dm), F32),
                   jax.ShapeDtypeStruct((s, dm), F32)])


def _flash_bwd(q, k, v, do, lse, delta, name, carried=None):
    s = q.shape[1]
    t = min(s, 1024)
    nq = s // t

    def body(q_ref, k_ref, v_ref, do_ref, lse_ref, delta_ref, dq_ref, dk_ref, dv_ref):
        j = pl.program_id(1)

        @pl.when(j == 0)
        def _():
            dq_ref[...] = jnp.zeros_like(dq_ref)

        kb, vb = k_ref[...], v_ref[...]
        row = lax.broadcasted_iota(jnp.int32, (t, t), 0)
        col = lax.broadcasted_iota(jnp.int32, (t, t), 1)

        def block(i):
            return pl.ds(pl.multiple_of(i * t, t), t)

        def scores(i):
            at = block(i)
            return _dot(q_ref[at, :], kb, NT), _dot(do_ref[at, :], vb, NT)

        def update(i, sc, dp, dk, dv, diagonal):
            at = block(i)
            p = jnp.exp(sc - lse_ref[at, :])
            if diagonal:
                p = jnp.where(col <= row, p, 0.0)
            dv = dv + _dot(p.astype(BF16), do_ref[at, :], TN)
            ds = (p * (dp - delta_ref[at, :])).astype(BF16)
            dq_ref[at, :] += _dot(ds, kb)
            return dk + _dot(ds, q_ref[at, :], TN), dv

        carry = update(j, *scores(j), jnp.zeros((t, HP), F32), jnp.zeros((t, VD), F32), True)
        dk, dv = lax.fori_loop(j + 1, nq, lambda i, cr: update(i, *scores(i), *cr, False), carry)
        dk_ref[...] = dk
        dv_ref[...] = dv

    whole = lambda n: pl.BlockSpec((None, s, n), lambda h, j: (h, 0, 0))
    blk = lambda n: pl.BlockSpec((None, t, n), lambda h, j: (h, j, 0))
    return _call(
        body, (q, k, v, do, lse, delta), name=name, grid=(N_HEADS, nq), carried=carried,
        in_specs=[whole(HP), blk(HP), blk(VD), pl.BlockSpec((s, VD), lambda h, j: (0, h)), whole(1), whole(1)],
        out_specs=[whole(HP), blk(HP), blk(VD)],
        out_shape=[jax.ShapeDtypeStruct((N_HEADS, s, HP), F32), jax.ShapeDtypeStruct((N_HEADS, s, HP), F32),
                   jax.ShapeDtypeStruct((N_HEADS, s, VD), F32)])


def _mla_prep_bwd(dq, dk, dv, za, zkr, cos, sin, wq, wkv, gql, gkvl, gq2, gk2, layer, name):
    s = za.shape[0]
    tm = min(s, 512)

    def norm_bwd(n, yv, rstd, gain, d_n_out, d_y_out):
        tn_, ty = n * rstd, yv * rstd
        dgain_n, dgain_y = _sum0(d_n_out * tn_), _sum0(d_y_out * ty)
        dtn, dty = d_n_out * gain[:, :NOPE], d_y_out * gain[:, NOPE:]
        a = (_sum1(dtn * n) + _sum1(dty * yv)) * (rstd * rstd * rstd * (1.0 / QK))
        return rstd * dtn - n * a, rstd * dty - (0.5 * yv) * a, dgain_n, dgain_y

    def rope_bwd(d_out, cos, sin):
        return d_out * cos + pltpu.roll(d_out * sin, 128 - 32, axis=1)

    def latent_bwd(z, gain, dn):
        rstd = lax.rsqrt(jnp.mean(z * z, axis=1, keepdims=True) + EPS)
        zh = z * rstd
        dzh = dn * gain
        return rstd * (dzh - zh * jnp.mean(dzh * zh, axis=1, keepdims=True)), _sum0(dn * zh)

    def body(dq_ref, dk_ref, dv_ref, za_ref, zkr_ref, cos_ref, sin_ref, wq_ref, wkv_ref, gql_ref, gkvl_ref, gq_ref, gk_ref,
             dza_ref, dzkr_ref, gwq_ref, gwkv_ref, dgql_ref, dgkvl_ref, dgq_ref, dgk_ref):
        @pl.when(pl.program_id(0) == 0)
        def _():
            for r in (gwq_ref, gwkv_ref, dgql_ref, dgkvl_ref, dgq_ref, dgk_ref):
                r[...] = jnp.zeros_like(r)

        zq, zkv = za_ref[:, 0:QL], za_ref[:, QL:QL + KVL]
        qf = zq * lax.rsqrt(jnp.mean(zq * zq, axis=1, keepdims=True) + EPS) * gql_ref[...]
        kvf = zkv * lax.rsqrt(jnp.mean(zkv * zkv, axis=1, keepdims=True) + EPS) * gkvl_ref[...]
        qn, kvn = qf.astype(BF16), kvf.astype(BF16)
        qn_t, kvn_t = qf.T.astype(BF16), kvf.T.astype(BF16)
        kr = zkr_ref[...]
        kr_ss = 0.5 * _sum1(kr * kr)
        cos, sin = cos_ref[...], sin_ref[...]
        gq, gk = gq_ref[...], gk_ref[...]
        dkr = jnp.zeros((tm, 128), F32)
        qr_all, kvr_all = _dot(qn, wq_ref[...]), _dot(kvn, wkv_ref[...])
        dqr_all, dkvr_all = [], []
        for h in range(N_HEADS):
            qr = qr_all[:, h * HP:(h + 1) * HP]
            n, yv = qr[:, :NOPE], qr[:, NOPE:]
            rstd = lax.rsqrt((_sum1(n * n) + 0.5 * _sum1(yv * yv)) * (1.0 / QK) + EPS)
            dqh = dq_ref[h] * SCORE_SCALE
            dn, dy, dg_n, dg_y = norm_bwd(n, yv, rstd, gq, dqh[:, :NOPE], rope_bwd(dqh[:, NOPE:], cos, sin))
            dgq_ref[:, 0:NOPE] += dg_n
            dgq_ref[:, NOPE:] += dg_y
            dqr_all += [dn.astype(BF16), dy.astype(BF16)]
            kn = kvr_all[:, h * HP:h * HP + NOPE]
            rstd = lax.rsqrt((_sum1(kn * kn) + kr_ss) * (1.0 / QK) + EPS)
            dkh = dk_ref[h]
            dn, dy, dg_n, dg_y = norm_bwd(kn, kr, rstd, gk, dkh[:, :NOPE], rope_bwd(dkh[:, NOPE:], cos, sin))
            dgk_ref[:, 0:NOPE] += dg_n
            dgk_ref[:, NOPE:] += dg_y
            dkr = dkr + dy
            dkvr_all += [dn.astype(BF16), dv_ref[h].astype(BF16)]

        dqr_all, dkvr_all = jnp.concatenate(dqr_all, axis=1), jnp.concatenate(dkvr_all, axis=1)
        gwq_ref[...] += _dot(qn_t, dqr_all)
        gwkv_ref[...] += _dot(kvn_t, dkvr_all)
        dzq, dgql = latent_bwd(zq, gql_ref[...], _dot(dqr_all, wq_ref[...], NT))
        dzkv, dgkvl = latent_bwd(zkv, gkvl_ref[...], _dot(dkvr_all, wkv_ref[...], NT))
        dgql_ref[...] += dgql
        dgkvl_ref[...] += dgkvl
        dza_ref[:, 0:QL] = dzq
        dza_ref[:, QL:] = dzkv
        lane = lax.broadcasted_iota(jnp.int32, (tm, 128), 1)
        dzkr_ref[...] = jnp.where(lane < ROPE, dkr + pltpu.roll(dkr, 64, axis=1), 0.0)

    tile = lambda n: pl.BlockSpec((tm, n), lambda i: (i, 0))
    heads = lambda n: pl.BlockSpec((N_HEADS, tm, n), lambda i: (0, i, 0))
    return pl.pallas_call(
        body, name=name, grid=(s // tm,),
        in_specs=[heads(HP), heads(HP), heads(VD), tile(768), tile(128), tile(128), tile(128),
                  pl.BlockSpec((None, QL, N_HEADS * HP), lambda i: (layer, 0, 0)),
                  pl.BlockSpec((None, KVL, N_HEADS * HP), lambda i: (layer, 0, 0)),
                  _row(QL), _row(KVL), _row(HP), _row(HP)],
        out_specs=[tile(768), tile(128), pl.BlockSpec((QL, N_HEADS * HP), lambda i: (0, 0)),
                   pl.BlockSpec((KVL, N_HEADS * HP), lambda i: (0, 0)), _row(QL), _row(KVL), _row(HP), _row(HP)],
        out_shape=[jax.ShapeDtypeStruct((s, 768), F32), jax.ShapeDtypeStruct((s, 128), F32),
                   jax.ShapeDtypeStruct((QL, N_HEADS * HP), F32), jax.ShapeDtypeStruct((KVL, N_HEADS * HP), F32),
                   jax.ShapeDtypeStruct((1, QL), F32), jax.ShapeDtypeStruct((1, KVL), F32),
                   jax.ShapeDtypeStruct((1, HP), F32), jax.ShapeDtypeStruct((1, HP), F32)],
        compiler_params=_params("arbitrary"),
    )(dq, dk, dv, za, zkr, cos, sin, wq, wkv, gql, gkvl, gq2, gk2)


def _pointwise_bwd(dpw, cv, lng, lnb, wpw, layer, name):
    s, dc = cv.shape
    tm = min(s, 256)

    def body(dpw_ref, cv_ref, lng_ref, lnb_ref, w_ref, dcv_ref, act_ref, dbpw_ref, dlng_ref, dlnb_ref):
        @pl.when(pl.program_id(0) == 0)
        def _():
            for r in (dbpw_ref, dlng_ref, dlnb_ref):
                r[...] = jnp.zeros_like(r)

        cv = cv_ref[...]
        dv = cv - jnp.mean(cv, axis=1, keepdims=True)
        rstd = lax.rsqrt(jnp.mean(dv * dv, axis=1, keepdims=True) + EPS)
        xh = dv * rstd
        yl = xh * lng_ref[...] + lnb_ref[...]
        silu, dsilu = _silu_parts(yl)
        act_ref[...] = silu.astype(BF16)
        dpw = dpw_ref[...]
        dbpw_ref[...] += _sum0(dpw)
        dyl = _dot(dpw.astype(BF16), w_ref[...], NT) * dsilu
        dlng_ref[...] += _sum0(dyl * xh)
        dlnb_ref[...] += _sum0(dyl)
        dxh = dyl * lng_ref[...]
        dcv_ref[...] = rstd * (dxh - jnp.mean(dxh, axis=1, keepdims=True) - xh * jnp.mean(dxh * xh, axis=1, keepdims=True))

    tile = pl.BlockSpec((tm, dc), lambda i: (i, 0))
    return pl.pallas_call(
        body, name=name, grid=(s // tm,),
        in_specs=[tile, tile, _row(dc), _row(dc), pl.BlockSpec((None, dc, dc), lambda i: (layer, 0, 0))],
        out_specs=[tile, tile, _row(dc), _row(dc), _row(dc)],
        out_shape=[jax.ShapeDtypeStruct((s, dc), F32), jax.ShapeDtypeStruct((s, dc), BF16)] + [jax.ShapeDtypeStruct((1, dc), F32)] * 3,
        compiler_params=_params("arbitrary"),
    )(dpw, cv, lng, lnb, wpw)


def _conv_bwd(dcv, zb, dzb, glu_b, dw, layer, name, carried=None):
    s, dc = dcv.shape
    tm = min(s, 256)
    hb = tm // HALO
    last = s // tm - 1

    def body(dcv_ref, dcvn_ref, val_ref, gate_ref, valh_ref, gateh_ref, glub_ref, dw_ref, _, dzb_ref, gdw_ref, ddwb_ref, dglub_ref,
             ubuf, dbuf, gacc, uph, dph):
        i = pl.program_id(0)

        @pl.when(i == 0)
        def _():
            gacc[...] = jnp.zeros_like(gacc)
            ddwb_ref[...] = jnp.zeros_like(ddwb_ref)
            dglub_ref[...] = jnp.zeros_like(dglub_ref)

        bias = glub_ref[...]
        ubuf[HALO:, :] = _glu(val_ref[...], gate_ref[...], bias)
        ubuf[0:HALO, :] = jnp.where(i > 0, _glu(valh_ref[...], gateh_ref[...], bias), 0.0)
        dcv = dcv_ref[...]
        dbuf[0:tm, :] = dcv
        dbuf[tm:, :] = jnp.where(i < last, dcvn_ref[...], 0.0)
        ddwb_ref[...] += _sum0(dcv)
        for cc in range(0, dc, CH):
            _make_phases(ubuf, uph, cc)
            _make_phases(dbuf, dph, cc)
            for r0 in range(0, tm, RC):
                du = jnp.zeros((RC, CH), F32)
                dpiece = dbuf[r0:r0 + RC, cc:cc + CH]
                for j in range(CONV_K):
                    du = du + _window(dbuf, dph, cc, (CONV_K - 1) - j, r0) * dw_ref[j:j + 1, cc:cc + CH]
                    win = _window(ubuf, uph, cc, HALO - (CONV_K - 1) + j, r0)
                    gacc[j, :, cc:cc + CH] += (dpiece * win).reshape(RC // 8, 8, CH).sum(axis=0)
                a = val_ref[r0:r0 + RC, cc:cc + CH] + bias[:, cc:cc + CH]
                sg = _sigmoid(gate_ref[r0:r0 + RC, cc:cc + CH] + bias[:, dc + cc:dc + cc + CH])
                dzb_ref[r0:r0 + RC, cc:cc + CH] = du * sg
                dzb_ref[r0:r0 + RC, dc + cc:dc + cc + CH] = du * a * sg * (1.0 - sg)
        dglub_ref[...] += _sum0(dzb_ref[...])

        @pl.when(i == last)
        def _():
            total = jnp.sum(gacc[...], axis=1)
            for cc in range(0, dc, CH):
                gdw_ref[cc // CH] = total[:, cc:cc + CH]

    return _call(
        body, (dcv, dcv, zb, zb, zb, zb, glu_b, dw, dzb), name=name, grid=(s // tm,), carried=carried, aliases={8: 0},
        in_specs=[pl.BlockSpec((tm, dc), lambda i: (i, 0)),
                  pl.BlockSpec((HALO, dc), lambda i: (jnp.minimum((i + 1) * hb, s // HALO - 1), 0)),
                  pl.BlockSpec((tm, dc), lambda i: (i, 0)), pl.BlockSpec((tm, dc), lambda i: (i, 1)),
                  pl.BlockSpec((HALO, dc), lambda i: (jnp.maximum(i * hb - 1, 0), 0)),
                  pl.BlockSpec((HALO, dc), lambda i: (jnp.maximum(i * hb - 1, 0), 1)),
                  _row(2 * dc), pl.BlockSpec((None, HALO, dc), lambda i: (layer, 0, 0)), ANY],
        out_specs=[pl.BlockSpec((tm, 2 * dc), lambda i: (i, 0)), pl.BlockSpec((4, HALO, CH), lambda i: (0, 0, 0)),
                   _row(dc), _row(2 * dc)],
        out_shape=[jax.ShapeDtypeStruct(dzb.shape, F32), jax.ShapeDtypeStruct((4, HALO, CH), F32),
                   jax.ShapeDtypeStruct((1, dc), F32), jax.ShapeDtypeStruct((1, 2 * dc), F32)],
        scratch=[pltpu.VMEM((tm + HALO, dc), F32), pltpu.VMEM((tm + HALO, dc), F32), pltpu.VMEM((HALO, 8, dc), F32),
                 pltpu.VMEM((7, tm + HALO - PH_ROWS_LESS, CH), F32), pltpu.VMEM((7, tm + HALO - PH_ROWS_LESS, CH), F32)])


def _inproj_bwd(dza, dzkr, dzb, x, dxo, g, scale, shift, w_int, layer, name, carried=None):
    s, d = x.shape
    tm = min(s, 128)

    def body(dza_ref, dzkr_ref, dzb_ref, x_ref, dxo_ref, g_ref, sc_ref, sh_ref, w_hbm, dx_ref, dsh_ref, dgg_ref, w_vmem, sems):
        @pl.when(pl.program_id(0) == 0)
        def _():
            _load_w_in(w_hbm.at[layer], w_vmem, sems)
            dsh_ref[...] = jnp.zeros_like(dsh_ref)
            dgg_ref[...] = jnp.zeros_like(dgg_ref)

        dh = _dot(dza_ref[...].astype(BF16), w_vmem[0:768])
        dh = dh + _dot(dzkr_ref[...].astype(BF16), w_vmem[768:896])
        dh = dh + _dot(dzb_ref[...].astype(BF16), w_vmem[896:W_ROWS])
        xv = x_ref[...]
        rstd = lax.rsqrt(jnp.mean(xv * xv, axis=1, keepdims=True) + EPS)
        xh = xv * rstd
        dsh_ref[...] += _sum0(dh)
        dgg_ref[...] += _sum0(dh * xh)
        dxh = dh * (g_ref[...] * (1.0 + sc_ref[...]))
        dx_ref[...] = dxo_ref[...] + rstd * (dxh - xh * jnp.mean(dxh * xh, axis=1, keepdims=True))

    tile = lambda n: pl.BlockSpec((tm, n), lambda i: (i, 0))
    return _call(
        body, (dza, dzkr, dzb, x, dxo, g, scale, shift, w_int), name=name, grid=(s // tm,), carried=carried,
        in_specs=[tile(768), tile(128), tile(4096), tile(d), tile(d), _row(d), _row(d), _row(d), ANY],
        out_specs=[tile(d), _row(d), _row(d)],
        out_shape=[jax.ShapeDtypeStruct((s, d), F32), jax.ShapeDtypeStruct((1, d), F32), jax.ShapeDtypeStruct((1, d), F32)],
        scratch=[pltpu.VMEM((W_ROWS, d), BF16), pltpu.SemaphoreType.DMA((len(W_PIECES),))])


def _grad_w_in(dza, dzkr, dzb, hb, name, carried=None):
    s, d = hb.shape
    ts = min(s, 512)
    nt = s // ts
    tiles = ((0, 768), (768, 64), (1856, 1024), (2880, 1024), (832, 1024), (3904, 1024))

    def body(a_ref, kr_ref, b_ref, h_ref, o_hbm, acc, sem):
        r, t = pl.program_id(0), pl.program_id(1)

        @pl.when(t == 0)
        def _():
            acc[...] = jnp.zeros_like(acc)

        hv = h_ref[...]

        @pl.when(r == 0)
        def _():
            acc[0:768, :] += _dot(a_ref[...].astype(BF16), hv, TN)

        @pl.when(r == 1)
        def _():
            acc[0:128, :] += _dot(kr_ref[...].astype(BF16), hv, TN)

        @pl.when(r >= 2)
        def _():
            acc[...] += _dot(b_ref[...].astype(BF16), hv, TN)

        for tile, (row0, rows) in enumerate(tiles):
            @pl.when((t == nt - 1) & (r == tile))
            def _():
                cp = pltpu.make_async_copy(acc.at[pl.ds(0, rows)], o_hbm.at[pl.ds(row0, rows)], sem)
                cp.start()
                cp.wait()

    return _call(
        body, (dza, dzkr, dzb, hb), name=name, grid=(len(tiles), nt), carried=carried,
        in_specs=[pl.BlockSpec((ts, 768), lambda r, t: (jnp.where(r == 0, t, nt - 1), 0)),
                  pl.BlockSpec((ts, 128), lambda r, t: (jnp.where(r == 1, t, jnp.where(r == 0, 0, nt - 1)), 0)),
                  pl.BlockSpec((ts, 1024), lambda r, t: (jnp.where(r >= 2, t, 0), jnp.maximum(r - 2, 0))),
                  pl.BlockSpec((ts, d), lambda r, t: (t, 0))],
        out_specs=[ANY], out_shape=[jax.ShapeDtypeStruct((4928, d), F32)],
        scratch=[pltpu.VMEM((1024, d), F32), pltpu.SemaphoreType.DMA])


def _adamw(w, g, m, v):
    m = ADAM_B1 * m + (1.0 - ADAM_B1) * g
    v = ADAM_B2 * v + (1.0 - ADAM_B2) * (g * g)
    m_hat = m / (1.0 - ADAM_B1 ** ADAM_STEP)
    v_hat = v / (1.0 - ADAM_B2 ** ADAM_STEP)
    return -ADAM_LR * (m_hat / (jnp.sqrt(v_hat) + ADAM_EPS) + ADAM_WD * w), m, v


def _adam_update(w, g0, g1, m, v, name):
    _, r, c = w.shape
    fits = [t for t in range(8, r + 1, 8) if r % t == 0 and t * c * 4 <= (1 << 21)]
    tr = max(fits) if fits else r

    def body(w_ref, g0_ref, g1_ref, m_ref, v_ref, g_ref, d_ref, mo_ref, vo_ref):
        g = jnp.where(pl.program_id(0) == 0, g0_ref[...], g1_ref[...])
        g_ref[...] = g
        d_ref[...], mo_ref[...], vo_ref[...] = _adamw(w_ref[...], g, m_ref[...], v_ref[...])

    big = pl.BlockSpec((None, tr, c), lambda l, i: (l, i, 0))
    one = pl.BlockSpec((tr, c), lambda l, i: (i, 0))
    return pl.pallas_call(
        body, name=name, grid=(2, r // tr), in_specs=[big, one, one, big, big], out_specs=[big] * 4,
        out_shape=[jax.ShapeDtypeStruct(w.shape, F32)] * 4, compiler_params=_params("arbitrary", "arbitrary"),
    )(w, g0, g1, m, v)


def _ada_update(c_all, dmod, w, m, v, carried=None):
    nl, d, n = w.shape
    tr = 256

    def body(c_ref, dm_ref, w_ref, m_ref, v_ref, g_ref, d_ref, mo_ref, vo_ref):
        cv = c_ref[...]
        act = (cv * _sigmoid(cv)).astype(BF16)
        g = _dot(act, dm_ref[...].astype(BF16), TN)
        g_ref[...] = g
        d_ref[...], mo_ref[...], vo_ref[...] = _adamw(w_ref[...], g, m_ref[...], v_ref[...])

    big = pl.BlockSpec((None, tr, n), lambda l, i: (l, i, 0))
    return _call(
        body, (c_all, dmod, w, m, v), name="ada_w_update", grid=(nl, d // tr), carried=carried,
        in_specs=[pl.BlockSpec((8, tr), lambda l, i: (0, i)), pl.BlockSpec((None, 8, n), lambda l, i: (l, 0, 0)), big, big, big],
        out_specs=[big] * 4, out_shape=[jax.ShapeDtypeStruct(w.shape, F32)] * 4)


def _small_update(gathered, w, m, v):
    r = w.shape[0]

    def body(ga_ref, w_ref, m_ref, v_ref, g_ref, d_ref, mo_ref, vo_ref):
        g = ga_ref[0]
        for dev in range(1, 8):
            g = g + ga_ref[dev]
        g_ref[...] = g
        d_ref[...], mo_ref[...], vo_ref[...] = _adamw(w_ref[...], g, m_ref[...], v_ref[...])

    return pl.pallas_call(body, name="small_update", out_shape=[jax.ShapeDtypeStruct((r, 128), F32)] * 4,
                          compiler_params=_params())(gathered, w, m, v)


SMALL = (("ada_b", 6144), ("norm_g", 2048), ("q_lat_g", 512), ("kv_lat_g", 256), ("q_norm_g", 256), ("k_norm_g", 256),
         ("glu_b", 2048), ("dw_b", 1024), ("conv_ln_g", 1024), ("conv_ln_b", 1024), ("b_pw", 1024))


def _pack_small(vals):
    cols = []
    for name, width in SMALL:
        a = vals[name]
        if a.shape[1] < width:
            a = jnp.pad(a, ((0, 0), (0, width - a.shape[1])))
        cols.append(a)
    return jnp.concatenate(cols, axis=1).reshape(-1, 128)


def _unpack_small(packed, shapes):
    flat = packed.reshape(2, -1)
    out, at = {}, 0
    for name, width in SMALL:
        out[name] = flat[:, at:at + shapes[name]]
        at += width
    return out


def _dup_gain(g):
    return jnp.concatenate([g, g[NOPE:]])[None, :]


def _undup(g):
    return jnp.concatenate([g[..., :NOPE], g[..., NOPE:NOPE + ROPE] + g[..., NOPE + ROPE:]], axis=-1)


def kernel(x, c, positions, ada_w, ada_b, norm_g, w_in, q_lat_g, w_q_up, kv_lat_g, w_kv_up, q_norm_g, k_norm_g, glu_b, dw_w, dw_b, conv_ln_g, conv_ln_b, w_pw, b_pw, w_out, loss_target, m_ada_w, m_ada_b, m_norm_g, m_w_in, m_q_lat_g, m_w_q_up, m_kv_lat_g, m_w_kv_up, m_q_norm_g, m_k_norm_g, m_glu_b, m_dw_w, m_dw_b, m_conv_ln_g, m_conv_ln_b, m_w_pw, m_b_pw, m_w_out, v_ada_w, v_ada_b, v_norm_g, v_w_in, v_q_lat_g, v_w_q_up, v_kv_lat_g, v_w_kv_up, v_q_norm_g, v_k_norm_g, v_glu_b, v_dw_w, v_dw_b, v_conv_ln_g, v_conv_ln_b, v_w_pw, v_b_pw, v_w_out):
    nl = 2
    s, d = x.shape[1], x.shape[2]
    xi, yi, ci = lax.axis_index("x"), lax.axis_index("y"), lax.axis_index("c")
    shard = 2 * xi + yi
    me = 4 * xi + 2 * yi + ci
    cidx = jnp.reshape(ci, (1,)).astype(jnp.int32)
    jc = jnp.stack([shard, ci]).astype(jnp.int32)
    x0 = x.reshape(s, d)
    target = loss_target.reshape(s, d)

    c_all = _allgather8(c.reshape(8, d // 8), "gather_c").reshape(8, d)
    n_ada = ada_w.shape[2]
    ada_b_shard = lax.dynamic_slice_in_dim(ada_b, shard * n_ada, n_ada, axis=1)[:, None, :]
    mod_shard = _modulation(c_all, ada_w, ada_b_shard)
    mod_all = _allgather8(mod_shard.reshape(nl * 8, n_ada), "gather_mod")
    mod_rows = lax.dynamic_index_in_dim(mod_all.reshape(4, 2, nl, 8, n_ada)[:, 0], me, axis=2, keepdims=False)
    mod_me = jnp.transpose(mod_rows, (1, 0, 2)).reshape(nl, 3, 1, d)

    tr = lambda a: jnp.transpose(a, (0, 2, 1))
    w_in_t = tr(w_in).astype(BF16)
    wq = w_q_up.reshape(nl, QL, 2, QK)
    wq = jnp.concatenate([wq, wq[..., NOPE:]], axis=-1)
    wq = jnp.transpose(wq, (0, 2, 1, 3)).reshape(nl, 2 * QL, HP).astype(BF16)
    dw_pad = jnp.pad(dw_w, ((0, 0), (0, HALO - CONV_K), (0, 0)))
    local = [w_in_t, wq, w_kv_up.astype(BF16), dw_pad, w_pw.astype(BF16), w_out.astype(BF16)]

    def kernel_layouts(bufs):
        w_in_g, wq_g, wkv_g, dw_g, wpw_g, wout_g = bufs
        heads_side_by_side = lambda a, rows: jnp.transpose(a.reshape(1, -1, rows, a.shape[-1]), (0, 2, 1, 3)).reshape(1, rows, -1)
        return dict(w_in=w_in_g.reshape(1, 4 * w_in_g.shape[2], d), wq=heads_side_by_side(wq_g, QL), wkv=heads_side_by_side(wkv_g, KVL),
                    dw=jnp.transpose(dw_g, (0, 2, 1, 3)).reshape(1, HALO, 4 * dw_g.shape[3]),
                    wpw=wpw_g.reshape(1, 4 * wpw_g.shape[2], wpw_g.shape[3]), wout=wout_g.reshape(1, 4 * wout_g.shape[2], d))

    w_in_all = [_run_alone(_gather_hand_on(_run_alone(_gather_start([local[0][0:1]]), "gather_w_in0")), "gather_w_in0_hand_on"), None]
    others_start = [_gather_start([a[l:l + 1] for a in local[1:]]) for l in range(nl)]
    w_in1_start = _gather_start([local[0][1:2]])
    wts = [None] * nl

    cos, sin = _rope_tables(positions.reshape(s, 1))
    row = lambda a, l: a[l][None, :]

    saved = []
    xl = x0
    for l in range(nl):
        shift, scale, gate = mod_me[l, 0], mod_me[l, 1], mod_me[l, 2]
        in_args = (xl, row(norm_g, l), scale, shift, w_in_all[l][0].reshape(1, -1, d), 0, f"inproj_fwd{l}")
        if l == 0:
            (hb, za, zkr, zb), landed = _inproj_fwd(*in_args, carried=others_start[0])
            others = _run_alone(_gather_hand_on(landed), "gather_others0_hand_on")
        else:
            (hb, za, zkr, zb), others = _inproj_fwd(*in_args, carried=_gather_hand_on(others1_landed))
        w = wts[l] = kernel_layouts(w_in_all[l] + others)
        gains = (row(q_lat_g, l), row(kv_lat_g, l), _dup_gain(q_norm_g[l]), _dup_gain(k_norm_g[l]))
        q, k, v = _mla_prep_fwd(za, zkr, cos, sin, w["wq"], w["wkv"], *gains, 0, f"mla_prep_fwd{l}")
        conv_args = (zb, row(glu_b, l), w["dw"], row(dw_b, l), row(conv_ln_g, l), row(conv_ln_b, l), w["wpw"], row(b_pw, l), 0)
        if l == 0:
            (o, lse), landed = _flash_fwd(q, k, v, f"flash_fwd{l}", carried=w_in1_start)
            (cv, pw), w_in_all[1] = _conv_fwd(*conv_args, f"conv_fwd{l}", carried=_gather_hand_on(landed))
            (xn, yv, mixb), others1_landed = _outproj_fwd(xl, o, zb, pw, gate, w["wout"], 0, f"outproj_fwd{l}", carried=others_start[1])
        else:
            o, lse = _flash_fwd(q, k, v, f"flash_fwd{l}")
            cv, pw = _conv_fwd(*conv_args, f"conv_fwd{l}")
            xn, yv, mixb = _outproj_fwd(xl, o, zb, pw, gate, w["wout"], 0, f"outproj_fwd{l}")
        saved.append(dict(x=xl, hb=hb, za=za, zkr=zkr, zb=zb, q=q, k=k, v=v, o=o, lse=lse, cv=cv, pw=pw, y=yv, mixb=mixb, gains=gains))
        xl = xn

    tok_loss, dx = _loss_head(xl, target)
    loss = lax.psum(jnp.sum(tok_loss), ("x", "y", "c"))

    big = [None] * nl
    small = [None] * nl
    shards_of = lambda gs: [g.reshape(4, g.shape[0] // 4, g.shape[1]) for g in gs]
    pair_sums = lambda l, parts, theirs: [_pair_sum(p, t, cidx, f"pair_sum{l}_{e}") for e, (p, t) in enumerate(zip(parts, theirs))]
    chip_sums = lambda l, sums, landed: [_chip_sum(sm, ld, jc, f"chip_sum{l}_{e}") for e, (sm, ld) in enumerate(zip(sums, landed))]
    halves = [None] * nl
    for l in reversed(range(nl)):
        sv, w = saved[l], wts[l]
        shift, scale, gate = mod_me[l, 0], mod_me[l, 1], mod_me[l, 2]
        out_args = (dx, sv["y"], gate, w["wout"], sv["o"], sv["zb"], sv["pw"], 0, f"outproj_bwd{l}")
        attn_args = (sv["q"], sv["k"], sv["v"])
        if l == 0:
            parts = shards_of(big[1])
            (dgate, dyb, do, delta, dzb, dpw), theirs = _outproj_bwd(*out_args, carried=_pair_exchange(parts))
            sums = pair_sums(1, parts, theirs)
            (dq, dk, dv), landed = _flash_bwd(*attn_args, do, sv["lse"], delta, f"flash_bwd{l}", carried=_chip_scatter(sums))
            halves[1] = chip_sums(1, sums, landed)
        else:
            dgate, dyb, do, delta, dzb, dpw = _outproj_bwd(*out_args)
            dq, dk, dv = _flash_bwd(*attn_args, do, sv["lse"], delta, f"flash_bwd{l}")
        g_out = _grad_tn(sv["mixb"], dyb, f"grad_w_out{l}")
        dza, dzkr, g_q, g_kv, dgql, dgkvl, dgq, dgk = _mla_prep_bwd(
            dq, dk, dv, sv["za"], sv["zkr"], cos, sin, w["wq"], w["wkv"], *sv["gains"], 0, f"mla_prep_bwd{l}")
        dcv, act, dbpw, dlng, dlnb = _pointwise_bwd(dpw, sv["cv"], row(conv_ln_g, l), row(conv_ln_b, l), w["wpw"], 0, f"pointwise_bwd{l}")
        g_pw = _grad_tn(act, dpw, f"grad_w_pw{l}")
        by_shard = lambda g, n: jnp.transpose(g.reshape(g.shape[0], -1, n), (1, 0, 2)).reshape(-1, n)
        early = [by_shard(g_q, HP), by_shard(g_kv, 512), g_pw, g_out]
        conv_args = (dcv, sv["zb"], dzb, row(glu_b, l), w["dw"], 0, f"conv_bwd{l}")
        in_args = (sv["x"], dx, row(norm_g, l), scale, shift, w["w_in"], 0, f"inproj_bwd{l}")
        if l == 0:
            parts = shards_of(early)
            (dzb, g_dw, ddwb, dglub), theirs = _conv_bwd(*conv_args, carried=_pair_exchange(parts))
            sums = pair_sums("0e", parts, theirs)
            (g_in,), landed = _grad_w_in(dza, dzkr, dzb, sv["hb"], f"grad_w_in{l}", carried=_chip_scatter(sums))
            e_q, e_kv, e_pw, e_out = chip_sums("0e", sums, landed)
            parts = shards_of([g_in, g_dw.reshape(4 * HALO, CH)])
            sums = pair_sums("0l", parts, _run_alone(_pair_exchange(parts), "pair_exchange_late0"))
            (dx, dshift, dgg), landed = _inproj_bwd(dza, dzkr, dzb, *in_args, carried=_chip_scatter(sums))
            l_in, l_dw = chip_sums("0l", sums, landed)
            halves[0] = [l_in, e_q, e_kv, l_dw, e_pw, e_out]
        else:
            dzb, g_dw, ddwb, dglub = _conv_bwd(*conv_args)
            dx, dshift, dgg = _inproj_bwd(dza, dzkr, dzb, *in_args)
            g_in, = _grad_w_in(dza, dzkr, dzb, sv["hb"], f"grad_w_in{l}")
        big[l] = [g_in, early[0], early[1], g_dw.reshape(4 * HALO, CH), g_pw, g_out]
        small[l] = dict(ada_b=jnp.concatenate([dshift, dgg * row(norm_g, l), dgate], axis=1), norm_g=dgg * (1.0 + scale),
                        q_lat_g=dgql, kv_lat_g=dgkvl, q_norm_g=_undup(dgq), k_norm_g=_undup(dgk), glu_b=dglub, dw_b=ddwb,
                        conv_ln_g=dlng, conv_ln_b=dlnb, b_pw=dbpw)
    grad_x = dx.reshape(x.shape)

    names = [n for n, _ in SMALL]
    mine = _pack_small({n: jnp.concatenate([small[0][n], small[1][n]], axis=0) for n in names})
    gathered = _allgather8(mine, "gather_small")
    weights = dict(ada_b=ada_b, norm_g=norm_g, q_lat_g=q_lat_g, kv_lat_g=kv_lat_g, q_norm_g=q_norm_g, k_norm_g=k_norm_g,
                   glu_b=glu_b, dw_b=dw_b, conv_ln_g=conv_ln_g, conv_ln_b=conv_ln_b, b_pw=b_pw)
    m_small = dict(ada_b=m_ada_b, norm_g=m_norm_g, q_lat_g=m_q_lat_g, kv_lat_g=m_kv_lat_g, q_norm_g=m_q_norm_g, k_norm_g=m_k_norm_g,
                   glu_b=m_glu_b, dw_b=m_dw_b, conv_ln_g=m_conv_ln_g, conv_ln_b=m_conv_ln_b, b_pw=m_b_pw)
    v_small = dict(ada_b=v_ada_b, norm_g=v_norm_g, q_lat_g=v_q_lat_g, kv_lat_g=v_kv_lat_g, q_norm_g=v_q_norm_g, k_norm_g=v_k_norm_g,
                   glu_b=v_glu_b, dw_b=v_dw_b, conv_ln_g=v_conv_ln_g, conv_ln_b=v_conv_ln_b, b_pw=v_b_pw)
    widths = {n: weights[n].shape[1] for n in names}
    v_packed = _pack_small({n: jnp.pad(v_small[n], ((0, 0), (0, dict(SMALL)[n] - widths[n])), constant_values=1.0) for n in names})
    small_out = [_unpack_small(a, widths) for a in _small_update(gathered, _pack_small(weights), _pack_small(m_small), v_packed)]

    ada_rows = gathered.reshape(8, nl, -1)[:, :, :3 * d]
    dmod = lax.dynamic_slice_in_dim(jnp.transpose(ada_rows, (1, 0, 2)), shard * n_ada, n_ada, axis=2)
    ada_out = _ada_update(c_all, dmod, ada_w, m_ada_w, v_ada_w)

    full = _run_alone(_pair_complete(halves[0] + halves[1]), "pair_complete")
    per_layer = [full[l * 6:(l + 1) * 6] for l in range(nl)]

    def natural_q(g):
        return jnp.transpose(_undup(g.reshape(2, QL, HP)), (1, 0, 2)).reshape(QL, 2 * QK)

    grads = [[per_layer[l][0], natural_q(per_layer[l][1]), per_layer[l][2], per_layer[l][3][:CONV_K], per_layer[l][4], per_layer[l][5]]
             for l in range(nl)]
    sharded = (("w_in", tr(w_in), tr(m_w_in), tr(v_w_in)), ("w_q_up", w_q_up, m_w_q_up, v_w_q_up),
               ("w_kv_up", w_kv_up, m_w_kv_up, v_w_kv_up), ("dw_w", dw_w, m_dw_w, v_dw_w),
               ("w_pw", w_pw, m_w_pw, v_w_pw), ("w_out", w_out, m_w_out, v_w_out))
    big_out = {name: _adam_update(w, grads[0][e], grads[1][e], m, v, f"adam_{name}") for e, (name, w, m, v) in enumerate(sharded)}
    big_out["w_in"] = [tr(a) for a in big_out["w_in"]]

    order = ["ada_w", "ada_b", "norm_g", "w_in", "q_lat_g", "w_q_up", "kv_lat_g", "w_kv_up", "q_norm_g", "k_norm_g", "glu_b",
             "dw_w", "dw_b", "conv_ln_g", "conv_ln_b", "w_pw", "b_pw", "w_out"]

    def leaf(kind, name):
        if name == "ada_w":
            return ada_out[kind]
        if name in big_out:
            return big_out[name][kind]
        return small_out[kind][name]

    return (loss, grad_x, *[leaf(kind, name) for kind in range(4) for name in order])
```

```python
import functools
import math

import jax
import jax.numpy as jnp
from jax import lax
from jax.experimental import pallas as pl
from jax.experimental.pallas import tpu as pltpu

F32, BF16 = jnp.float32, jnp.bfloat16
MESH = pl.DeviceIdType.MESH
ANY = pl.BlockSpec(memory_space=pl.ANY)

N_HEADS, NOPE, ROPE, VD = 8, 128, 64, 128
QK = NOPE + ROPE
QL, KVL = 512, 256
HP = 256
CONV_K, HALO = 31, 32
ROPE_THETA = 10000.0
EPS = 1e-6
ADAM_LR, ADAM_B1, ADAM_B2, ADAM_EPS, ADAM_WD, ADAM_STEP = 0.001, 0.9, 0.999, 1e-08, 0.01, 10
V7X_VMEM_LIMIT = 56 * 1024 * 1024

NT = (((1,), (1,)), ((), ()))
TN = (((0,), (0,)), ((), ()))
NN = (((1,), (0,)), ((), ()))


def _dot(a, b, dims=NN):
    return lax.dot_general(a, b, dims, preferred_element_type=F32)


def _params(*sem):
    return pltpu.CompilerParams(dimension_semantics=sem or None, vmem_limit_bytes=V7X_VMEM_LIMIT)


def _sigmoid(x):
    return 1.0 / (1.0 + jnp.exp(-x))


def _sum0(x):
    return jnp.sum(x, axis=0, keepdims=True)


def _sum1(x):
    return jnp.sum(x, axis=1, keepdims=True)


def _row(n):
    return pl.BlockSpec((1, n), lambda *_: (0, 0))


def _place():
    x, y, c = lax.axis_index("x"), lax.axis_index("y"), lax.axis_index("c")
    chips = [(1 - x, y), (x, 1 - y), (1 - x, 1 - y)]
    return x, y, c, chips


def _allgather8(v, name):
    r, n = v.shape

    def body(v_ref, out_ref, send_sems, recv_sems, local_sem):
        x, y, c, chips = _place()
        me, sibling = (x, y, c), (x, y, 1 - c)

        def slot(px, py, pc):
            return out_ref.at[4 * px + 2 * py + pc]

        def copy(k, block, to, src=None):
            return pltpu.make_async_remote_copy(
                src_ref=slot(*block) if src is None else src, dst_ref=slot(*block),
                send_sem=send_sems.at[k], recv_sem=recv_sems.at[k], device_id=to, device_id_type=MESH)

        mine = pltpu.make_async_copy(v_ref, slot(*me), local_sem)
        mine.start()
        first = [copy(0, me, sibling, src=v_ref)]
        first += [copy(1 + j, me, (*chip, c), src=v_ref) for j, chip in enumerate(chips)]
        for cp in first:
            cp.start()
        passed = [copy(4 + j, (*chip, c), sibling) for j, chip in enumerate(chips)]
        for j, chip in enumerate(chips):
            copy(1 + j, (*chip, c), me).wait_recv()
            passed[j].start()
        copy(0, sibling, me).wait_recv()
        for j, chip in enumerate(chips):
            copy(4 + j, (*chip, 1 - c), me).wait_recv()
        for cp in first + passed:
            cp.wait_send()
        mine.wait()

    return pl.pallas_call(
        body, name=name, out_shape=jax.ShapeDtypeStruct((8, r, n), v.dtype),
        in_specs=[pl.BlockSpec(memory_space=pltpu.VMEM)], out_specs=pl.BlockSpec(memory_space=pltpu.VMEM),
        scratch_shapes=[pltpu.SemaphoreType.DMA((7,)), pltpu.SemaphoreType.DMA((7,)), pltpu.SemaphoreType.DMA],
    )(v)


class _Carried:
    def __init__(self, operands, results, n_sems, start, finish, aliases=None):
        self.operands, self.results, self.n_sems = operands, results, n_sems
        self.start, self.finish, self.aliases = start, finish, aliases or {}


def _run_alone(carried, name):
    k = len(carried.operands)

    def body(*refs):
        args = (refs[:k], refs[k:k + len(carried.results)], refs[-2], refs[-1])
        carried.start(*args)
        carried.finish(*args)

    outs = pl.pallas_call(
        body, name=name, out_shape=carried.results, in_specs=[ANY] * k, out_specs=[ANY] * len(carried.results),
        input_output_aliases=carried.aliases,
        scratch_shapes=[pltpu.SemaphoreType.DMA((carried.n_sems,)), pltpu.SemaphoreType.DMA((carried.n_sems,))],
    )(*carried.operands)
    return list(outs)


def _call(body, operands, *, name, grid, in_specs, out_specs, out_shape, scratch=(), aliases=None, carried=None):
    params = _params(*(["arbitrary"] * len(grid)))
    n_in, n_out = len(in_specs), len(out_shape)
    if carried is None:
        return pl.pallas_call(body, name=name, grid=grid, in_specs=in_specs, out_specs=out_specs, out_shape=out_shape,
                              scratch_shapes=list(scratch), input_output_aliases=aliases or {}, compiler_params=params)(*operands)
    k_in, k_out = len(carried.operands), len(carried.results)

    def wrapped(*refs):
        ins, outs = refs[:n_in], refs[n_in + k_in:n_in + k_in + n_out]
        comm = (refs[n_in:n_in + k_in], refs[n_in + k_in + n_out:n_in + k_in + n_out + k_out], refs[-2], refs[-1])
        steps = [pl.program_id(a) for a in range(len(grid))]
        first = functools.reduce(jnp.logical_and, [s == 0 for s in steps])
        last = functools.reduce(jnp.logical_and, [s == g - 1 for s, g in zip(steps, grid)])

        @pl.when(first)
        def _():
            carried.start(*comm)

        body(*ins, *outs, *refs[n_in + k_in + n_out + k_out:-2])

        @pl.when(last)
        def _():
            carried.finish(*comm)

    both = dict(aliases or {})
    both.update({n_in + i: n_out + o for i, o in carried.aliases.items()})
    res = pl.pallas_call(
        wrapped, name=name, grid=grid, in_specs=list(in_specs) + [ANY] * k_in, out_specs=list(out_specs) + [ANY] * k_out,
        out_shape=list(out_shape) + list(carried.results), input_output_aliases=both, compiler_params=params,
        scratch_shapes=list(scratch) + [pltpu.SemaphoreType.DMA((carried.n_sems,)), pltpu.SemaphoreType.DMA((carried.n_sems,))],
    )(*operands, *carried.operands)
    return list(res[:n_out]), list(res[n_out:])


def _gather_start(shards):
    ne = len(shards)
    per = 4

    def copies(srcs, dsts, send_sems, recv_sems):
        x, y, c, chips = _place()
        jme = 2 * x + y
        out = []
        for e in range(ne):
            half = srcs[e].shape[2] // 2
            own = pl.ds(pl.multiple_of(c * half, 128), half)
            for k, chip in enumerate(chips):
                out.append(pltpu.make_async_remote_copy(
                    src_ref=srcs[e].at[:, :, own], dst_ref=dsts[e].at[:, jme, :, own], send_sem=send_sems.at[per * e + k],
                    recv_sem=recv_sems.at[per * e + k], device_id=(*chip, c), device_id_type=MESH))
            out.append(pltpu.make_async_remote_copy(
                src_ref=srcs[e], dst_ref=dsts[e].at[:, jme], send_sem=send_sems.at[per * e + 3],
                recv_sem=recv_sems.at[per * e + 3], device_id=(x, y, 1 - c), device_id_type=MESH))
        return out

    def start(*a):
        for cp in copies(*a):
            cp.start()

    def finish(*a):
        for cp in copies(*a):
            cp.wait()

    results = [jax.ShapeDtypeStruct((s.shape[0], 4) + s.shape[1:], s.dtype) for s in shards]
    return _Carried(list(shards), results, per * ne, start, finish)


def _gather_hand_on(bufs):
    ne = len(bufs)

    def copy(e, k, dsts, send_sems, recv_sems, mine):
        x, y, c, chips = _place()
        px, py = chips[k]
        half = dsts[e].shape[3] // 2
        cols = pl.ds(pl.multiple_of((c if mine else 1 - c) * half, 128), half)
        part = dsts[e].at[:, 2 * px + py, :, cols]
        return pltpu.make_async_remote_copy(src_ref=part, dst_ref=part, send_sem=send_sems.at[3 * e + k],
                                            recv_sem=recv_sems.at[3 * e + k], device_id=(x, y, 1 - c), device_id_type=MESH)

    def start(srcs, dsts, send_sems, recv_sems):
        for e in range(ne):
            for k in range(3):
                copy(e, k, dsts, send_sems, recv_sems, True).start()

    def finish(srcs, dsts, send_sems, recv_sems):
        for e in range(ne):
            for k in range(3):
                copy(e, k, dsts, send_sems, recv_sems, True).wait_send()
                copy(e, k, dsts, send_sems, recv_sems, False).wait_recv()

    results = [jax.ShapeDtypeStruct(b.shape, b.dtype) for b in bufs]
    return _Carried(list(bufs), results, 3 * ne, start, finish, aliases={e: e for e in range(ne)})


def _pair_exchange(parts):
    ne = len(parts)

    def copies(srcs, dsts, send_sems, recv_sems):
        x, y, c, _ = _place()
        out = []
        for e in range(ne):
            half = srcs[e].shape[2] // 2
            theirs = pl.ds(pl.multiple_of((1 - c) * half, 128), half)
            out.append(pltpu.make_async_remote_copy(
                src_ref=srcs[e].at[:, :, theirs], dst_ref=dsts[e], send_sem=send_sems.at[e],
                recv_sem=recv_sems.at[e], device_id=(x, y, 1 - c), device_id_type=MESH))
        return out

    def start(*a):
        for cp in copies(*a):
            cp.start()

    def finish(*a):
        for cp in copies(*a):
            cp.wait()

    results = [jax.ShapeDtypeStruct(p.shape[:2] + (p.shape[2] // 2,), p.dtype) for p in parts]
    return _Carried(list(parts), results, ne, start, finish)


def _chip_scatter(sums):
    ne = len(sums)

    def copies(srcs, dsts, send_sems, recv_sems):
        x, y, c, chips = _place()
        return [pltpu.make_async_remote_copy(
                    src_ref=srcs[e].at[2 * px + py], dst_ref=dsts[e].at[k], send_sem=send_sems.at[3 * e + k],
                    recv_sem=recv_sems.at[3 * e + k], device_id=(px, py, c), device_id_type=MESH)
                for e in range(ne) for k, (px, py) in enumerate(chips)]

    def start(*a):
        for cp in copies(*a):
            cp.start()

    def finish(*a):
        for cp in copies(*a):
            cp.wait()

    results = [jax.ShapeDtypeStruct((3,) + s.shape[1:], s.dtype) for s in sums]
    return _Carried(list(sums), results, 3 * ne, start, finish)


def _pair_complete(grads):
    ne = len(grads)

    def copy(e, dsts, send_sems, recv_sems, mine):
        x, y, c, _ = _place()
        half = dsts[e].shape[1] // 2
        cols = pl.ds(pl.multiple_of((c if mine else 1 - c) * half, 128), half)
        return pltpu.make_async_remote_copy(
            src_ref=dsts[e].at[:, cols], dst_ref=dsts[e].at[:, cols], send_sem=send_sems.at[e],
            recv_sem=recv_sems.at[e], device_id=(x, y, 1 - c), device_id_type=MESH)

    def start(srcs, dsts, send_sems, recv_sems):
        for e in range(ne):
            copy(e, dsts, send_sems, recv_sems, True).start()

    def finish(srcs, dsts, send_sems, recv_sems):
        for e in range(ne):
            copy(e, dsts, send_sems, recv_sems, True).wait_send()
            copy(e, dsts, send_sems, recv_sems, False).wait_recv()

    results = [jax.ShapeDtypeStruct(g.shape, g.dtype) for g in grads]
    return _Carried(list(grads), results, ne, start, finish, aliases={e: e for e in range(ne)})


def _pair_sum(part, theirs, cidx, name):
    _, r, n = part.shape
    half = n // 2

    def body(c_ref, p_ref, t_ref, o_ref):
        o_ref[...] = (p_ref[...].astype(F32) + t_ref[...].astype(F32)).astype(BF16)

    gs = pltpu.PrefetchScalarGridSpec(
        num_scalar_prefetch=1, grid=(4,),
        in_specs=[pl.BlockSpec((1, r, half), lambda j, c: (j, 0, c[0])),
                  pl.BlockSpec((1, r, half), lambda j, c: (j, 0, 0))],
        out_specs=pl.BlockSpec((1, r, half), lambda j, c: (j, 0, 0)))
    return pl.pallas_call(body, name=name, grid_spec=gs, out_shape=jax.ShapeDtypeStruct((4, r, half), BF16),
                          compiler_params=_params("arbitrary"))(cidx, part, theirs)


def _chip_sum(sums, landed, jc, name):
    _, r, half = sums.shape

    def body(jc_ref, s_ref, l_ref, o_ref):
        acc = s_ref[0].astype(F32)
        for k in range(3):
            acc = acc + l_ref[k].astype(F32)
        o_ref[...] = acc

    gs = pltpu.PrefetchScalarGridSpec(
        num_scalar_prefetch=1, grid=(1,),
        in_specs=[pl.BlockSpec((1, r, half), lambda i, jc: (jc[0], 0, 0)),
                  pl.BlockSpec((3, r, half), lambda i, jc: (0, 0, 0))],
        out_specs=pl.BlockSpec((r, half), lambda i, jc: (0, jc[1])))
    return pl.pallas_call(body, name=name, grid_spec=gs, out_shape=jax.ShapeDtypeStruct((r, 2 * half), F32),
                          compiler_params=_params("arbitrary"))(jc, sums, landed)


def _rope_tables(pos):
    s = pos.shape[0]
    lane = jnp.arange(128)
    inv = 1.0 / (ROPE_THETA ** ((2 * (lane % 32)).astype(F32) / ROPE))
    keep = (lane < 64).astype(F32)
    sign = jnp.where(lane < 32, -1.0, 1.0).astype(F32) * keep
    consts = jnp.stack([inv.astype(F32), keep, sign])[:, None, :]

    def body(p_ref, k_ref, c_ref, s_ref):
        ang = p_ref[...].astype(F32) * k_ref[0]
        c_ref[...] = jnp.cos(ang) * k_ref[1]
        s_ref[...] = jnp.sin(ang) * k_ref[2]

    tm = min(s, 1024)
    return pl.pallas_call(
        body, name="rope_tables", grid=(s // tm,),
        in_specs=[pl.BlockSpec((tm, 1), lambda i: (i, 0)), pl.BlockSpec((3, 1, 128), lambda i: (0, 0, 0))],
        out_specs=[pl.BlockSpec((tm, 128), lambda i: (i, 0))] * 2,
        out_shape=[jax.ShapeDtypeStruct((s, 128), F32)] * 2, compiler_params=_params("arbitrary"),
    )(pos, consts)


def _modulation(c_all, ada_w, ada_b_shard):
    nl, d, n = ada_w.shape
    tn = 512

    def body(c_ref, w_ref, b_ref, o_ref):
        cv = c_ref[...]
        act = (cv * _sigmoid(cv)).astype(BF16)
        o_ref[...] = _dot(act, w_ref[...].astype(BF16)) + b_ref[...]

    return pl.pallas_call(
        body, name="modulation", grid=(nl, n // tn),
        in_specs=[pl.BlockSpec((8, d), lambda l, j: (0, 0)), pl.BlockSpec((None, d, tn), lambda l, j: (l, 0, j)),
                  pl.BlockSpec((None, 1, tn), lambda l, j: (l, 0, j))],
        out_specs=pl.BlockSpec((None, 8, tn), lambda l, j: (l, 0, j)),
        out_shape=jax.ShapeDtypeStruct((nl, 8, n), F32), compiler_params=_params("arbitrary", "arbitrary"),
    )(c_all, ada_w, ada_b_shard)


def _loss_head(xf, target):
    s, d = xf.shape
    tm = min(s, 512)

    def body(x_ref, t_ref, l_ref, dx_ref):
        err = x_ref[...] - t_ref[...]
        l_ref[...] = 0.5 * jnp.mean(err * err, axis=1, keepdims=True)
        dx_ref[...] = err * (1.0 / d)

    return pl.pallas_call(
        body, name="loss_head", grid=(s // tm,),
        in_specs=[pl.BlockSpec((tm, d), lambda i: (i, 0))] * 2,
        out_specs=[pl.BlockSpec((tm, 1), lambda i: (i, 0)), pl.BlockSpec((tm, d), lambda i: (i, 0))],
        out_shape=[jax.ShapeDtypeStruct((s, 1), F32), jax.ShapeDtypeStruct((s, d), F32)],
        compiler_params=_params("arbitrary"),
    )(xf, target)


W_ROWS = 4992
W_PIECES = ((0, 0, 832), (832, 768, 64), (896, 1856, 2048), (2944, 832, 1024), (3968, 3904, 1024))


def _load_w_in(w_hbm, w_vmem, sems):
    cps = [pltpu.make_async_copy(w_hbm.at[pl.ds(src, n)], w_vmem.at[pl.ds(dst, n)], sems.at[i])
           for i, (dst, src, n) in enumerate(W_PIECES)]
    for cp in cps:
        cp.start()
    for cp in cps:
        cp.wait()


def _inproj_fwd(x, g, scale, shift, w_int, layer, name, carried=None):
    s, d = x.shape
    tm = min(s, 256)

    def body(x_ref, g_ref, sc_ref, sh_ref, w_hbm, hb_ref, za_ref, zkr_ref, zb_ref, w_vmem, sems):
        @pl.when(pl.program_id(0) == 0)
        def _():
            _load_w_in(w_hbm.at[layer], w_vmem, sems)

        xv = x_ref[...]
        rstd = lax.rsqrt(jnp.mean(xv * xv, axis=1, keepdims=True) + EPS)
        h = (xv * rstd) * g_ref[...] * (1.0 + sc_ref[...]) + sh_ref[...]
        hb = h.astype(BF16)
        hb_ref[...] = hb
        za_ref[...] = _dot(hb, w_vmem[0:768], NT)
        zkr_ref[...] = _dot(hb, w_vmem[768:896], NT)
        zb_ref[...] = _dot(hb, w_vmem[896:W_ROWS], NT)

    return _call(
        body, (x, g, scale, shift, w_int), name=name, grid=(s // tm,), carried=carried,
        in_specs=[pl.BlockSpec((tm, d), lambda i: (i, 0)), _row(d), _row(d), _row(d), ANY],
        out_specs=[pl.BlockSpec((tm, d), lambda i: (i, 0)), pl.BlockSpec((tm, 768), lambda i: (i, 0)),
                   pl.BlockSpec((tm, 128), lambda i: (i, 0)), pl.BlockSpec((tm, 4096), lambda i: (i, 0))],
        out_shape=[jax.ShapeDtypeStruct((s, d), BF16), jax.ShapeDtypeStruct((s, 768), F32),
                   jax.ShapeDtypeStruct((s, 128), F32), jax.ShapeDtypeStruct((s, 4096), F32)],
        scratch=[pltpu.VMEM((W_ROWS, d), BF16), pltpu.SemaphoreType.DMA((len(W_PIECES),))])


def _rope(yv, cos, sin):
    return yv * cos + pltpu.roll(yv, 32, axis=1) * sin


def _mla_prep_fwd(za, zkr, cos, sin, wq, wkv, gql, gkvl, gq2, gk2, layer, name):
    s = za.shape[0]
    tm = min(s, 256)

    def body(za_ref, zkr_ref, cos_ref, sin_ref, wq_ref, wkv_ref, gql_ref, gkvl_ref, gq_ref, gk_ref, q_ref, k_ref, v_ref):
        zq, zkv = za_ref[:, 0:QL], za_ref[:, QL:QL + KVL]
        qn = (zq * lax.rsqrt(jnp.mean(zq * zq, axis=1, keepdims=True) + EPS) * gql_ref[...]).astype(BF16)
        kvn = (zkv * lax.rsqrt(jnp.mean(zkv * zkv, axis=1, keepdims=True) + EPS) * gkvl_ref[...]).astype(BF16)
        kr = zkr_ref[...]
        kr_ss = 0.5 * _sum1(kr * kr)
        cos, sin = cos_ref[...], sin_ref[...]
        gq, gk = gq_ref[...] * SCORE_SCALE, gk_ref[...]
        qr_all, kvr_all = _dot(qn, wq_ref[...]), _dot(kvn, wkv_ref[...])
        for h in range(N_HEADS):
            qr = qr_all[:, h * HP:(h + 1) * HP]
            n, yv = qr[:, :NOPE], qr[:, NOPE:]
            rstd = lax.rsqrt((_sum1(n * n) + 0.5 * _sum1(yv * yv)) * (1.0 / QK) + EPS)
            q_ref[h, :, 0:NOPE] = (n * rstd * gq[:, :NOPE]).astype(BF16)
            q_ref[h, :, NOPE:HP] = _rope(yv * rstd * gq[:, NOPE:], cos, sin).astype(BF16)
            kvr = kvr_all[:, h * HP:(h + 1) * HP]
            kn, vv = kvr[:, :NOPE], kvr[:, NOPE:]
            rstd = lax.rsqrt((_sum1(kn * kn) + kr_ss) * (1.0 / QK) + EPS)
            k_ref[h, :, 0:NOPE] = (kn * rstd * gk[:, :NOPE]).astype(BF16)
            k_ref[h, :, NOPE:HP] = _rope(kr * rstd * gk[:, NOPE:], cos, sin).astype(BF16)
            v_ref[h] = vv.astype(BF16)

    tile = lambda n: pl.BlockSpec((tm, n), lambda i: (i, 0))
    return pl.pallas_call(
        body, name=name, grid=(s // tm,),
        in_specs=[tile(768), tile(128), tile(128), tile(128),
                  pl.BlockSpec((None, QL, N_HEADS * HP), lambda i: (layer, 0, 0)),
                  pl.BlockSpec((None, KVL, N_HEADS * HP), lambda i: (layer, 0, 0)),
                  _row(QL), _row(KVL), _row(HP), _row(HP)],
        out_specs=[pl.BlockSpec((N_HEADS, tm, HP), lambda i: (0, i, 0)), pl.BlockSpec((N_HEADS, tm, HP), lambda i: (0, i, 0)),
                   pl.BlockSpec((N_HEADS, tm, VD), lambda i: (0, i, 0))],
        out_shape=[jax.ShapeDtypeStruct((N_HEADS, s, HP), BF16), jax.ShapeDtypeStruct((N_HEADS, s, HP), BF16),
                   jax.ShapeDtypeStruct((N_HEADS, s, VD), BF16)],
        compiler_params=_params("arbitrary"),
    )(za, zkr, cos, sin, wq, wkv, gql, gkvl, gq2, gk2)


SCORE_SCALE = 1.0 / math.sqrt(QK)
MASKED = -1e30


def _flash_fwd(q, k, v, name, carried=None):
    s = q.shape[1]
    t = min(s, 1024)

    def body(q_ref, k_ref, v_ref, o_ref, lse_ref):
        i = pl.program_id(1)
        qb = q_ref[...]
        row = lax.broadcasted_iota(jnp.int32, (t, t), 0)
        col = lax.broadcasted_iota(jnp.int32, (t, t), 1)

        def block(j):
            return pl.ds(pl.multiple_of(j * t, t), t)

        def scores(j):
            return _dot(qb, k_ref[block(j), :], NT)

        def update(j, sc, m, l, acc, diagonal):
            if diagonal:
                sc = jnp.where(col <= row, sc, MASKED)
            m_new = jnp.maximum(m, jnp.max(sc, axis=1, keepdims=True))
            p = jnp.exp(sc - m_new)
            alpha = jnp.exp(m - m_new)
            return m_new, alpha * l + _sum1(p), alpha * acc + _dot(p.astype(BF16), v_ref[block(j), :])

        init = (jnp.full((t, 1), MASKED, F32), jnp.zeros((t, 1), F32), jnp.zeros((t, VD), F32))
        carry = lax.fori_loop(0, i, lambda j, cr: update(j, scores(j), *cr, False), init)
        m, l, acc = update(i, scores(i), *carry, True)
        o_ref[...] = acc / l
        lse_ref[...] = m + jnp.log(l)

    return _call(
        body, (q, k, v), name=name, grid=(N_HEADS, s // t), carried=carried,
        in_specs=[pl.BlockSpec((None, t, HP), lambda h, i: (h, i, 0)), pl.BlockSpec((None, s, HP), lambda h, i: (h, 0, 0)),
                  pl.BlockSpec((None, s, VD), lambda h, i: (h, 0, 0))],
        out_specs=[pl.BlockSpec((t, VD), lambda h, i: (i, h)), pl.BlockSpec((None, t, 1), lambda h, i: (h, i, 0))],
        out_shape=[jax.ShapeDtypeStruct((s, N_HEADS * VD), F32), jax.ShapeDtypeStruct((N_HEADS, s, 1), F32)])


CH, RC = 256, 64


PH_ROWS_LESS = 8


def _glu(val, gate, bias):
    c = val.shape[1]
    return (val + bias[:, :c]) * _sigmoid(gate + bias[:, c:])


def _make_phases(buf, phases, cc):
    rows = buf.shape[0] - PH_ROWS_LESS
    for b in range(1, 8):
        phases[b - 1] = buf[pl.ds(b, rows), cc:cc + CH]


def _window(buf, phases, cc, shift, r0):
    a, b = divmod(shift, 8)
    if b == 0:
        return buf[r0 + 8 * a:r0 + 8 * a + RC, cc:cc + CH]
    return phases[b - 1, r0 + 8 * a:r0 + 8 * a + RC, :]


def _conv_fwd(zb, glu_b, dw, dwb, lng, lnb, wpw, bpw, layer, name, carried=None):
    s = zb.shape[0]
    dc = dwb.shape[1]
    tm = min(s, 256)
    hb = tm // HALO

    def body(val_ref, gate_ref, valh_ref, gateh_ref, glub_ref, dw_ref, dwb_ref, lng_ref, lnb_ref, wpw_ref, bpw_ref,
             cv_ref, pw_ref, ubuf, uph):
        i = pl.program_id(0)
        bias = glub_ref[...]
        ubuf[HALO:, :] = _glu(val_ref[...], gate_ref[...], bias)
        uh = _glu(valh_ref[...], gateh_ref[...], bias)
        ubuf[0:HALO, :] = jnp.where(i > 0, uh, 0.0)
        for cc in range(0, dc, CH):
            _make_phases(ubuf, uph, cc)
            for r0 in range(0, tm, RC):
                acc = jnp.zeros((RC, CH), F32)
                for j in range(CONV_K):
                    acc = acc + _window(ubuf, uph, cc, HALO - (CONV_K - 1) + j, r0) * dw_ref[j:j + 1, cc:cc + CH]
                cv_ref[r0:r0 + RC, cc:cc + CH] = acc + dwb_ref[:, cc:cc + CH]
        cv = cv_ref[...]
        dv = cv - jnp.mean(cv, axis=1, keepdims=True)
        yl = dv * lax.rsqrt(jnp.mean(dv * dv, axis=1, keepdims=True) + EPS) * lng_ref[...] + lnb_ref[...]
        act = (yl * _sigmoid(yl)).astype(BF16)
        pw_ref[...] = _dot(act, wpw_ref[...]) + bpw_ref[...]

    return _call(
        body, (zb, zb, zb, zb, glu_b, dw, dwb, lng, lnb, wpw, bpw), name=name, grid=(s // tm,), carried=carried,
        in_specs=[pl.BlockSpec((tm, dc), lambda i: (i, 0)), pl.BlockSpec((tm, dc), lambda i: (i, 1)),
                  pl.BlockSpec((HALO, dc), lambda i: (jnp.maximum(i * hb - 1, 0), 0)),
                  pl.BlockSpec((HALO, dc), lambda i: (jnp.maximum(i * hb - 1, 0), 1)),
                  _row(2 * dc), pl.BlockSpec((None, HALO, dc), lambda i: (layer, 0, 0)), _row(dc), _row(dc), _row(dc),
                  pl.BlockSpec((None, dc, dc), lambda i: (layer, 0, 0)), _row(dc)],
        out_specs=[pl.BlockSpec((tm, dc), lambda i: (i, 0))] * 2,
        out_shape=[jax.ShapeDtypeStruct((s, dc), F32)] * 2,
        scratch=[pltpu.VMEM((tm + HALO, dc), F32), pltpu.VMEM((7, tm + HALO - PH_ROWS_LESS, CH), F32)])


def _silu_parts(z):
    sg = _sigmoid(z)
    return z * sg, sg * (1.0 + z * (1.0 - sg))


def _outproj_fwd(x, o, zb, pw, gate, wout, layer, name, carried=None):
    s, d = x.shape
    dm = o.shape[1]
    tm = min(s, 256)

    def body(x_ref, o_ref, mg_ref, cg_ref, pw_ref, gate_ref, w_ref, xn_ref, mix_ref):
        mg, cg = mg_ref[...], cg_ref[...]
        mix_ref[:, 0:dm] = (o_ref[...] * (mg * _sigmoid(mg))).astype(BF16)
        mix_ref[:, dm:] = (pw_ref[...] * (cg * _sigmoid(cg))).astype(BF16)
        xn_ref[...] = x_ref[...] + gate_ref[...] * _dot(mix_ref[...], w_ref[...])

    tile = lambda n, j=0: pl.BlockSpec((tm, n), lambda i: (i, j))
    return _call(
        body, (x, o, zb, zb, pw, gate, wout), name=name, grid=(s // tm,), carried=carried,
        in_specs=[tile(d), tile(dm), tile(dm, 2), tile(dm, 3), tile(dm), _row(d),
                  pl.BlockSpec((None, 2 * dm, d), lambda i: (layer, 0, 0))],
        out_specs=[tile(d), tile(2 * dm)],
        out_shape=[jax.ShapeDtypeStruct((s, d), F32), jax.ShapeDtypeStruct((s, 2 * dm), BF16)])


def _grad_tn(a, b, name, carried=None):
    s, n = a.shape
    m = b.shape[1]
    tn, ts = min(n, 1024), min(s, 512)
    nt = s // ts

    def body(a_ref, b_ref, o_ref, acc):
        t = pl.program_id(1)

        @pl.when(t == 0)
        def _():
            acc[...] = jnp.zeros_like(acc)

        acc[...] += _dot(a_ref[...].astype(BF16), b_ref[...].astype(BF16), TN)

        @pl.when(t == nt - 1)
        def _():
            o_ref[...] = acc[...].astype(BF16)

    return _call(
        body, (a, b), name=name, grid=(n // tn, nt), carried=carried,
        in_specs=[pl.BlockSpec((ts, tn), lambda r, t: (t, r)), pl.BlockSpec((ts, m), lambda r, t: (t, 0))],
        out_specs=[pl.BlockSpec((tn, m), lambda r, t: (r, 0))], out_shape=[jax.ShapeDtypeStruct((n, m), BF16)],
        scratch=[pltpu.VMEM((tn, m), F32)])


def _grad_w_out(mixb, dxb, gate, wout, layer, name, carried=None):
    s, n = mixb.shape
    d = dxb.shape[1]
    tn, ts = min(n, 1024), min(s, 512)
    nt = s // ts

    def body(a_ref, b_ref, gate_ref, w_ref, g_ref, dgate_ref, acc):
        t = pl.program_id(1)

        @pl.when(t == 0)
        def _():
            acc[...] = jnp.zeros_like(acc)

        acc[...] += _dot(a_ref[...], b_ref[...], TN)

        @pl.when(t == nt - 1)
        def _():
            m = acc[...]
            dgate_ref[...] = _sum0(m * w_ref[...].astype(F32))
            g_ref[...] = (m * gate_ref[...]).astype(BF16)

    return _call(
        body, (mixb, dxb, gate, wout), name=name, grid=(n // tn, nt), carried=carried,
        in_specs=[pl.BlockSpec((ts, tn), lambda r, t: (t, r)), pl.BlockSpec((ts, d), lambda r, t: (t, 0)), _row(d),
                  pl.BlockSpec((None, tn, d), lambda r, t: (layer, r, 0))],
        out_specs=[pl.BlockSpec((tn, d), lambda r, t: (r, 0)), pl.BlockSpec((None, 1, d), lambda r, t: (r, 0, 0))],
        out_shape=[jax.ShapeDtypeStruct((n, d), BF16), jax.ShapeDtypeStruct((n // tn, 1, d), F32)],
        scratch=[pltpu.VMEM((tn, d), F32)])


def _outproj_bwd(dxo, gate, wout, o, zb, pw, layer, name, carried=None):
    s, d = dxo.shape
    dm = o.shape[1]
    tm = min(s, 256)

    def body(dx_ref, gate_ref, w_ref, o_ref, mg_ref, cg_ref, pw_ref, dxb_ref, do_ref, delta_ref, dzb_ref, dpw_ref):
        dx = dx_ref[...]
        dxb_ref[...] = dx.astype(BF16)
        dmix = _dot((dx * gate_ref[...]).astype(BF16), w_ref[...], NT)
        da, db = dmix[:, :dm], dmix[:, dm:]
        ov = o_ref[...]
        silu_m, dsilu_m = _silu_parts(mg_ref[...])
        do = da * silu_m
        do_ref[...] = do.astype(BF16)
        prod = do * ov
        for h in range(N_HEADS):
            delta_ref[h] = _sum1(prod[:, h * VD:(h + 1) * VD])
        dzb_ref[:, 0:dm] = da * ov * dsilu_m
        silu_c, dsilu_c = _silu_parts(cg_ref[...])
        dpw_ref[...] = db * silu_c
        dzb_ref[:, dm:] = db * pw_ref[...] * dsilu_c

    tile = lambda n, j=0: pl.BlockSpec((tm, n), lambda i: (i, j))
    return _call(
        body, (dxo, gate, wout, o, zb, zb, pw), name=name, grid=(s // tm,), carried=carried,
        in_specs=[tile(d), _row(d), pl.BlockSpec((None, 2 * dm, d), lambda i: (layer, 0, 0)),
                  tile(dm), tile(dm, 2), tile(dm, 3), tile(dm)],
        out_specs=[tile(d), tile(dm), pl.BlockSpec((N_HEADS, tm, 1), lambda i: (0, i, 0)), tile(2 * dm, 1), tile(dm)],
        out_shape=[jax.ShapeDtypeStruct((s, d), BF16), jax.ShapeDtypeStruct((s, dm), BF16),
                   jax.ShapeDtypeStruct((N_HEADS, s, 1), F32), jax.ShapeDtypeStruct((s, 4 * dm), F32),
                   jax.ShapeDtypeStruct((s, dm), F32)])


def _flash_bwd(q, k, v, do, lse, delta, name, carried=None):
    s = q.shape[1]
    t = min(s, 1024)
    nq = s // t

    def body(q_ref, k_ref, v_ref, do_ref, lse_ref, delta_ref, dq_ref, dk_ref, dv_ref):
        j = pl.program_id(1)

        @pl.when(j == 0)
        def _():
            dq_ref[...] = jnp.zeros_like(dq_ref)

        kb, vb = k_ref[...], v_ref[...]
        row = lax.broadcasted_iota(jnp.int32, (t, t), 0)
        col = lax.broadcasted_iota(jnp.int32, (t, t), 1)

        def block(i):
            return pl.ds(pl.multiple_of(i * t, t), t)

        def scores(i):
            at = block(i)
            return _dot(q_ref[at, :], kb, NT), _dot(do_ref[at, :], vb, NT)

        def update(i, sc, dp, dk, dv, diagonal):
            at = block(i)
            p = jnp.exp(sc - lse_ref[at, :])
            if diagonal:
                p = jnp.where(col <= row, p, 0.0)
            dv = dv + _dot(p.astype(BF16), do_ref[at, :], TN)
            ds = (p * (dp - delta_ref[at, :])).astype(BF16)
            dq_ref[at, :] += _dot(ds, kb)
            return dk + _dot(ds, q_ref[at, :], TN), dv

        carry = update(j, *scores(j), jnp.zeros((t, HP), F32), jnp.zeros((t, VD), F32), True)
        dk, dv = lax.fori_loop(j + 1, nq, lambda i, cr: update(i, *scores(i), *cr, False), carry)
        dk_ref[...] = dk
        dv_ref[...] = dv

    whole = lambda n: pl.BlockSpec((None, s, n), lambda h, j: (h, 0, 0))
    blk = lambda n: pl.BlockSpec((None, t, n), lambda h, j: (h, j, 0))
    return _call(
        body, (q, k, v, do, lse, delta), name=name, grid=(N_HEADS, nq), carried=carried,
        in_specs=[whole(HP), blk(HP), blk(VD), pl.BlockSpec((s, VD), lambda h, j: (0, h)), whole(1), whole(1)],
        out_specs=[whole(HP), blk(HP), blk(VD)],
        out_shape=[jax.ShapeDtypeStruct((N_HEADS, s, HP), F32), jax.ShapeDtypeStruct((N_HEADS, s, HP), F32),
                   jax.ShapeDtypeStruct((N_HEADS, s, VD), F32)])


def _mla_prep_bwd(dq, dk, dv, za, zkr, cos, sin, wq, wkv, gql, gkvl, gq2, gk2, layer, name):
    s = za.shape[0]
    tm = min(s, 512)

    def norm_bwd(n, yv, rstd, gain, d_n_out, d_y_out):
        tn_, ty = n * rstd, yv * rstd
        dgain_n, dgain_y = _sum0(d_n_out * tn_), _sum0(d_y_out * ty)
        dtn, dty = d_n_out * gain[:, :NOPE], d_y_out * gain[:, NOPE:]
        a = (_sum1(dtn * n) + _sum1(dty * yv)) * (rstd * rstd * rstd * (1.0 / QK))
        return rstd * dtn - n * a, rstd * dty - (0.5 * yv) * a, dgain_n, dgain_y

    def rope_bwd(d_out, cos, sin):
        return d_out * cos + pltpu.roll(d_out * sin, 128 - 32, axis=1)

    def latent_bwd(z, gain, dn):
        rstd = lax.rsqrt(jnp.mean(z * z, axis=1, keepdims=True) + EPS)
        zh = z * rstd
        dzh = dn * gain
        return rstd * (dzh - zh * jnp.mean(dzh * zh, axis=1, keepdims=True)), _sum0(dn * zh)

    def body(dq_ref, dk_ref, dv_ref, za_ref, zkr_ref, cos_ref, sin_ref, wq_ref, wkv_ref, gql_ref, gkvl_ref, gq_ref, gk_ref,
             dza_ref, dzkr_ref, gwq_ref, gwkv_ref, dgql_ref, dgkvl_ref, dgq_ref, dgk_ref, gwq_acc, gwkv_acc):
        @pl.when(pl.program_id(0) == 0)
        def _():
            for r in (gwq_acc, gwkv_acc, dgql_ref, dgkvl_ref, dgq_ref, dgk_ref):
                r[...] = jnp.zeros_like(r)

        zq, zkv = za_ref[:, 0:QL], za_ref[:, QL:QL + KVL]
        qf = zq * lax.rsqrt(jnp.mean(zq * zq, axis=1, keepdims=True) + EPS) * gql_ref[...]
        kvf = zkv * lax.rsqrt(jnp.mean(zkv * zkv, axis=1, keepdims=True) + EPS) * gkvl_ref[...]
        qn, kvn = qf.astype(BF16), kvf.astype(BF16)
        qn_t, kvn_t = qf.T.astype(BF16), kvf.T.astype(BF16)
        kr = zkr_ref[...]
        kr_ss = 0.5 * _sum1(kr * kr)
        cos, sin = cos_ref[...], sin_ref[...]
        gq, gk = gq_ref[...], gk_ref[...]
        dkr = jnp.zeros((tm, 128), F32)
        qr_all, kvr_all = _dot(qn, wq_ref[...]), _dot(kvn, wkv_ref[...])
        dqr_all, dkvr_all = [], []
        for h in range(N_HEADS):
            qr = qr_all[:, h * HP:(h + 1) * HP]
            n, yv = qr[:, :NOPE], qr[:, NOPE:]
            rstd = lax.rsqrt((_sum1(n * n) + 0.5 * _sum1(yv * yv)) * (1.0 / QK) + EPS)
            dqh = dq_ref[h] * SCORE_SCALE
            dn, dy, dg_n, dg_y = norm_bwd(n, yv, rstd, gq, dqh[:, :NOPE], rope_bwd(dqh[:, NOPE:], cos, sin))
            dgq_ref[:, 0:NOPE] += dg_n
            dgq_ref[:, NOPE:] += dg_y
            dqr_all += [dn.astype(BF16), dy.astype(BF16)]
            kn = kvr_all[:, h * HP:h * HP + NOPE]
            rstd = lax.rsqrt((_sum1(kn * kn) + kr_ss) * (1.0 / QK) + EPS)
            dkh = dk_ref[h]
            dn, dy, dg_n, dg_y = norm_bwd(kn, kr, rstd, gk, dkh[:, :NOPE], rope_bwd(dkh[:, NOPE:], cos, sin))
            dgk_ref[:, 0:NOPE] += dg_n
            dgk_ref[:, NOPE:] += dg_y
            dkr = dkr + dy
            dkvr_all += [dn.astype(BF16), dv_ref[h].astype(BF16)]

        dqr_all, dkvr_all = jnp.concatenate(dqr_all, axis=1), jnp.concatenate(dkvr_all, axis=1)
        gwq_acc[...] += _dot(qn_t, dqr_all)
        gwkv_acc[...] += _dot(kvn_t, dkvr_all)

        @pl.when(pl.program_id(0) == s // tm - 1)
        def _():
            gwq_ref[...] = gwq_acc[...].astype(BF16)
            gwkv_ref[...] = gwkv_acc[...].astype(BF16)

        dzq, dgql = latent_bwd(zq, gql_ref[...], _dot(dqr_all, wq_ref[...], NT))
        dzkv, dgkvl = latent_bwd(zkv, gkvl_ref[...], _dot(dkvr_all, wkv_ref[...], NT))
        dgql_ref[...] += dgql
        dgkvl_ref[...] += dgkvl
        dza_ref[:, 0:QL] = dzq
        dza_ref[:, QL:] = dzkv
        lane = lax.broadcasted_iota(jnp.int32, (tm, 128), 1)
        dzkr_ref[...] = jnp.where(lane < ROPE, dkr + pltpu.roll(dkr, 64, axis=1), 0.0)

    tile = lambda n: pl.BlockSpec((tm, n), lambda i: (i, 0))
    heads = lambda n: pl.BlockSpec((N_HEADS, tm, n), lambda i: (0, i, 0))
    return pl.pallas_call(
        body, name=name, grid=(s // tm,),
        in_specs=[heads(HP), heads(HP), heads(VD), tile(768), tile(128), tile(128), tile(128),
                  pl.BlockSpec((None, QL, N_HEADS * HP), lambda i: (layer, 0, 0)),
                  pl.BlockSpec((None, KVL, N_HEADS * HP), lambda i: (layer, 0, 0)),
                  _row(QL), _row(KVL), _row(HP), _row(HP)],
        out_specs=[tile(768), tile(128), pl.BlockSpec((QL, N_HEADS * HP), lambda i: (0, 0)),
                   pl.BlockSpec((KVL, N_HEADS * HP), lambda i: (0, 0)), _row(QL), _row(KVL), _row(HP), _row(HP)],
        out_shape=[jax.ShapeDtypeStruct((s, 768), F32), jax.ShapeDtypeStruct((s, 128), F32),
                   jax.ShapeDtypeStruct((QL, N_HEADS * HP), BF16), jax.ShapeDtypeStruct((KVL, N_HEADS * HP), BF16),
                   jax.ShapeDtypeStruct((1, QL), F32), jax.ShapeDtypeStruct((1, KVL), F32),
                   jax.ShapeDtypeStruct((1, HP), F32), jax.ShapeDtypeStruct((1, HP), F32)],
        scratch_shapes=[pltpu.VMEM((QL, N_HEADS * HP), F32), pltpu.VMEM((KVL, N_HEADS * HP), F32)],
        compiler_params=_params("arbitrary"),
    )(dq, dk, dv, za, zkr, cos, sin, wq, wkv, gql, gkvl, gq2, gk2)


def _pointwise_bwd(dpw, cv, lng, lnb, wpw, layer, name):
    s, dc = cv.shape
    tm = min(s, 256)

    def body(dpw_ref, cv_ref, lng_ref, lnb_ref, w_ref, dcv_ref, act_ref, dbpw_ref, dlng_ref, dlnb_ref):
        @pl.when(pl.program_id(0) == 0)
        def _():
            for r in (dbpw_ref, dlng_ref, dlnb_ref):
                r[...] = jnp.zeros_like(r)

        cv = cv_ref[...]
        dv = cv - jnp.mean(cv, axis=1, keepdims=True)
        rstd = lax.rsqrt(jnp.mean(dv * dv, axis=1, keepdims=True) + EPS)
        xh = dv * rstd
        yl = xh * lng_ref[...] + lnb_ref[...]
        silu, dsilu = _silu_parts(yl)
        act_ref[...] = silu.astype(BF16)
        dpw = dpw_ref[...]
        dbpw_ref[...] += _sum0(dpw)
        dyl = _dot(dpw.astype(BF16), w_ref[...], NT) * dsilu
        dlng_ref[...] += _sum0(dyl * xh)
        dlnb_ref[...] += _sum0(dyl)
        dxh = dyl * lng_ref[...]
        dcv_ref[...] = rstd * (dxh - jnp.mean(dxh, axis=1, keepdims=True) - xh * jnp.mean(dxh * xh, axis=1, keepdims=True))

    tile = pl.BlockSpec((tm, dc), lambda i: (i, 0))
    return pl.pallas_call(
        body, name=name, grid=(s // tm,),
        in_specs=[tile, tile, _row(dc), _row(dc), pl.BlockSpec((None, dc, dc), lambda i: (layer, 0, 0))],
        out_specs=[tile, tile, _row(dc), _row(dc), _row(dc)],
        out_shape=[jax.ShapeDtypeStruct((s, dc), F32), jax.ShapeDtypeStruct((s, dc), BF16)] + [jax.ShapeDtypeStruct((1, dc), F32)] * 3,
        compiler_params=_params("arbitrary"),
    )(dpw, cv, lng, lnb, wpw)


def _conv_bwd(dcv, zb, dzb, glu_b, dw, layer, name, carried=None):
    s, dc = dcv.shape
    tm = min(s, 256)
    hb = tm // HALO
    last = s // tm - 1

    def body(dcv_ref, dcvn_ref, val_ref, gate_ref, valh_ref, gateh_ref, glub_ref, dw_ref, _, dzb_ref, gdw_ref, ddwb_ref, dglub_ref,
             ubuf, dbuf, gacc, uph, dph):
        i = pl.program_id(0)

        @pl.when(i == 0)
        def _():
            gacc[...] = jnp.zeros_like(gacc)
            ddwb_ref[...] = jnp.zeros_like(ddwb_ref)
            dglub_ref[...] = jnp.zeros_like(dglub_ref)

        bias = glub_ref[...]
        ubuf[HALO:, :] = _glu(val_ref[...], gate_ref[...], bias)
        ubuf[0:HALO, :] = jnp.where(i > 0, _glu(valh_ref[...], gateh_ref[...], bias), 0.0)
        dcv = dcv_ref[...]
        dbuf[0:tm, :] = dcv
        dbuf[tm:, :] = jnp.where(i < last, dcvn_ref[...], 0.0)
        ddwb_ref[...] += _sum0(dcv)
        for cc in range(0, dc, CH):
            _make_phases(ubuf, uph, cc)
            _make_phases(dbuf, dph, cc)
            for r0 in range(0, tm, RC):
                du = jnp.zeros((RC, CH), F32)
                dpiece = dbuf[r0:r0 + RC, cc:cc + CH]
                for j in range(CONV_K):
                    du = du + _window(dbuf, dph, cc, (CONV_K - 1) - j, r0) * dw_ref[j:j + 1, cc:cc + CH]
                    win = _window(ubuf, uph, cc, HALO - (CONV_K - 1) + j, r0)
                    gacc[j, :, cc:cc + CH] += (dpiece * win).reshape(RC // 8, 8, CH).sum(axis=0)
                a = val_ref[r0:r0 + RC, cc:cc + CH] + bias[:, cc:cc + CH]
                sg = _sigmoid(gate_ref[r0:r0 + RC, cc:cc + CH] + bias[:, dc + cc:dc + cc + CH])
                dzb_ref[r0:r0 + RC, cc:cc + CH] = du * sg
                dzb_ref[r0:r0 + RC, dc + cc:dc + cc + CH] = du * a * sg * (1.0 - sg)
        dglub_ref[...] += _sum0(dzb_ref[...])

        @pl.when(i == last)
        def _():
            total = jnp.sum(gacc[...], axis=1)
            for cc in range(0, dc, CH):
                gdw_ref[cc // CH] = total[:, cc:cc + CH]

    return _call(
        body, (dcv, dcv, zb, zb, zb, zb, glu_b, dw, dzb), name=name, grid=(s // tm,), carried=carried, aliases={8: 0},
        in_specs=[pl.BlockSpec((tm, dc), lambda i: (i, 0)),
                  pl.BlockSpec((HALO, dc), lambda i: (jnp.minimum((i + 1) * hb, s // HALO - 1), 0)),
                  pl.BlockSpec((tm, dc), lambda i: (i, 0)), pl.BlockSpec((tm, dc), lambda i: (i, 1)),
                  pl.BlockSpec((HALO, dc), lambda i: (jnp.maximum(i * hb - 1, 0), 0)),
                  pl.BlockSpec((HALO, dc), lambda i: (jnp.maximum(i * hb - 1, 0), 1)),
                  _row(2 * dc), pl.BlockSpec((None, HALO, dc), lambda i: (layer, 0, 0)), ANY],
        out_specs=[pl.BlockSpec((tm, 2 * dc), lambda i: (i, 0)), pl.BlockSpec((4, HALO, CH), lambda i: (0, 0, 0)),
                   _row(dc), _row(2 * dc)],
        out_shape=[jax.ShapeDtypeStruct(dzb.shape, F32), jax.ShapeDtypeStruct((4, HALO, CH), F32),
                   jax.ShapeDtypeStruct((1, dc), F32), jax.ShapeDtypeStruct((1, 2 * dc), F32)],
        scratch=[pltpu.VMEM((tm + HALO, dc), F32), pltpu.VMEM((tm + HALO, dc), F32), pltpu.VMEM((HALO, 8, dc), F32),
                 pltpu.VMEM((7, tm + HALO - PH_ROWS_LESS, CH), F32), pltpu.VMEM((7, tm + HALO - PH_ROWS_LESS, CH), F32)])


def _inproj_bwd(dza, dzkr, dzb, x, dxo, g, scale, shift, w_int, layer, name, carried=None):
    s, d = x.shape
    tm = min(s, 128)

    def body(dza_ref, dzkr_ref, dzb_ref, x_ref, dxo_ref, g_ref, sc_ref, sh_ref, w_hbm, dx_ref, dsh_ref, dgg_ref, w_vmem, sems):
        @pl.when(pl.program_id(0) == 0)
        def _():
            _load_w_in(w_hbm.at[layer], w_vmem, sems)
            dsh_ref[...] = jnp.zeros_like(dsh_ref)
            dgg_ref[...] = jnp.zeros_like(dgg_ref)

        dh = _dot(dza_ref[...].astype(BF16), w_vmem[0:768])
        dh = dh + _dot(dzkr_ref[...].astype(BF16), w_vmem[768:896])
        dh = dh + _dot(dzb_ref[...].astype(BF16), w_vmem[896:W_ROWS])
        xv = x_ref[...]
        rstd = lax.rsqrt(jnp.mean(xv * xv, axis=1, keepdims=True) + EPS)
        xh = xv * rstd
        dsh_ref[...] += _sum0(dh)
        dgg_ref[...] += _sum0(dh * xh)
        dxh = dh * (g_ref[...] * (1.0 + sc_ref[...]))
        dx_ref[...] = dxo_ref[...] + rstd * (dxh - xh * jnp.mean(dxh * xh, axis=1, keepdims=True))

    tile = lambda n: pl.BlockSpec((tm, n), lambda i: (i, 0))
    return _call(
        body, (dza, dzkr, dzb, x, dxo, g, scale, shift, w_int), name=name, grid=(s // tm,), carried=carried,
        in_specs=[tile(768), tile(128), tile(4096), tile(d), tile(d), _row(d), _row(d), _row(d), ANY],
        out_specs=[tile(d), _row(d), _row(d)],
        out_shape=[jax.ShapeDtypeStruct((s, d), F32), jax.ShapeDtypeStruct((1, d), F32), jax.ShapeDtypeStruct((1, d), F32)],
        scratch=[pltpu.VMEM((W_ROWS, d), BF16), pltpu.SemaphoreType.DMA((len(W_PIECES),))])


def _grad_w_in(dza, dzkr, dzb, hb, name, carried=None):
    s, d = hb.shape
    ts = min(s, 512)
    nt = s // ts
    tiles = ((0, 768), (768, 64), (1856, 1024), (2880, 1024), (832, 1024), (3904, 1024))

    def body(a_ref, kr_ref, b_ref, h_ref, o_hbm, acc, rounded, sem):
        r, t = pl.program_id(0), pl.program_id(1)

        @pl.when(t == 0)
        def _():
            acc[...] = jnp.zeros_like(acc)

        hv = h_ref[...]

        @pl.when(r == 0)
        def _():
            acc[0:768, :] += _dot(a_ref[...].astype(BF16), hv, TN)

        @pl.when(r == 1)
        def _():
            acc[0:128, :] += _dot(kr_ref[...].astype(BF16), hv, TN)

        @pl.when(r >= 2)
        def _():
            acc[...] += _dot(b_ref[...].astype(BF16), hv, TN)

        for tile, (row0, rows) in enumerate(tiles):
            @pl.when((t == nt - 1) & (r == tile))
            def _():
                rounded[0:rows, :] = acc[0:rows, :].astype(BF16)
                cp = pltpu.make_async_copy(rounded.at[pl.ds(0, rows)], o_hbm.at[pl.ds(row0, rows)], sem)
                cp.start()
                cp.wait()

    return _call(
        body, (dza, dzkr, dzb, hb), name=name, grid=(len(tiles), nt), carried=carried,
        in_specs=[pl.BlockSpec((ts, 768), lambda r, t: (jnp.where(r == 0, t, nt - 1), 0)),
                  pl.BlockSpec((ts, 128), lambda r, t: (jnp.where(r == 1, t, jnp.where(r == 0, 0, nt - 1)), 0)),
                  pl.BlockSpec((ts, 1024), lambda r, t: (jnp.where(r >= 2, t, 0), jnp.maximum(r - 2, 0))),
                  pl.BlockSpec((ts, d), lambda r, t: (t, 0))],
        out_specs=[ANY], out_shape=[jax.ShapeDtypeStruct((4928, d), BF16)],
        scratch=[pltpu.VMEM((1024, d), F32), pltpu.VMEM((1024, d), BF16), pltpu.SemaphoreType.DMA])


def _adamw(w, g, m, v):
    m = ADAM_B1 * m + (1.0 - ADAM_B1) * g
    v = ADAM_B2 * v + (1.0 - ADAM_B2) * (g * g)
    m_hat = m / (1.0 - ADAM_B1 ** ADAM_STEP)
    v_hat = v / (1.0 - ADAM_B2 ** ADAM_STEP)
    return -ADAM_LR * (m_hat / (jnp.sqrt(v_hat) + ADAM_EPS) + ADAM_WD * w), m, v


def _adam_update(w, g0, g1, m, v, name):
    _, r, c = w.shape
    fits = [t for t in range(8, r + 1, 8) if r % t == 0 and t * c * 4 <= (1 << 21)]
    tr = max(fits) if fits else r

    def body(w_ref, g0_ref, g1_ref, m_ref, v_ref, g_ref, d_ref, mo_ref, vo_ref):
        g = jnp.where(pl.program_id(0) == 0, g0_ref[...], g1_ref[...])
        g_ref[...] = g
        d_ref[...], mo_ref[...], vo_ref[...] = _adamw(w_ref[...], g, m_ref[...], v_ref[...])

    big = pl.BlockSpec((None, tr, c), lambda l, i: (l, i, 0))
    one = pl.BlockSpec((tr, c), lambda l, i: (i, 0))
    return pl.pallas_call(
        body, name=name, grid=(2, r // tr), in_specs=[big, one, one, big, big], out_specs=[big] * 4,
        out_shape=[jax.ShapeDtypeStruct(w.shape, F32)] * 4, compiler_params=_params("arbitrary", "arbitrary"),
    )(w, g0, g1, m, v)


def _ada_update(c_all, dmod, w, m, v, carried=None):
    nl, d, n = w.shape
    tr = 256

    def body(c_ref, dm_ref, w_ref, m_ref, v_ref, g_ref, d_ref, mo_ref, vo_ref):
        cv = c_ref[...]
        act = (cv * _sigmoid(cv)).astype(BF16)
        g = _dot(act, dm_ref[...].astype(BF16), TN)
        g_ref[...] = g
        d_ref[...], mo_ref[...], vo_ref[...] = _adamw(w_ref[...], g, m_ref[...], v_ref[...])

    big = pl.BlockSpec((None, tr, n), lambda l, i: (l, i, 0))
    return _call(
        body, (c_all, dmod, w, m, v), name="ada_w_update", grid=(nl, d // tr), carried=carried,
        in_specs=[pl.BlockSpec((8, tr), lambda l, i: (0, i)), pl.BlockSpec((None, 8, n), lambda l, i: (l, 0, 0)), big, big, big],
        out_specs=[big] * 4, out_shape=[jax.ShapeDtypeStruct(w.shape, F32)] * 4)


def _small_update(gathered, w, m, v):
    r = w.shape[0]

    def body(ga_ref, w_ref, m_ref, v_ref, g_ref, d_ref, mo_ref, vo_ref):
        g = ga_ref[0]
        for dev in range(1, 8):
            g = g + ga_ref[dev]
        g_ref[...] = g
        d_ref[...], mo_ref[...], vo_ref[...] = _adamw(w_ref[...], g, m_ref[...], v_ref[...])

    return pl.pallas_call(body, name="small_update", out_shape=[jax.ShapeDtypeStruct((r, 128), F32)] * 4,
                          compiler_params=_params())(gathered, w, m, v)


SMALL = (("ada_b", 6144), ("norm_g", 2048), ("q_lat_g", 512), ("kv_lat_g", 256), ("q_norm_g", 256), ("k_norm_g", 256),
         ("glu_b", 2048), ("dw_b", 1024), ("conv_ln_g", 1024), ("conv_ln_b", 1024), ("b_pw", 1024))


def _pack_small(vals):
    cols = []
    for name, width in SMALL:
        a = vals[name]
        if a.shape[1] < width:
            a = jnp.pad(a, ((0, 0), (0, width - a.shape[1])))
        cols.append(a)
    return jnp.concatenate(cols, axis=1).reshape(-1, 128)


def _unpack_small(packed, shapes):
    flat = packed.reshape(2, -1)
    out, at = {}, 0
    for name, width in SMALL:
        out[name] = flat[:, at:at + shapes[name]]
        at += width
    return out


def _dup_gain(g):
    return jnp.concatenate([g, g[NOPE:]])[None, :]


def _undup(g):
    return jnp.concatenate([g[..., :NOPE], g[..., NOPE:NOPE + ROPE] + g[..., NOPE + ROPE:]], axis=-1)


def kernel(x, c, positions, ada_w, ada_b, norm_g, w_in, q_lat_g, w_q_up, kv_lat_g, w_kv_up, q_norm_g, k_norm_g, glu_b, dw_w, dw_b, conv_ln_g, conv_ln_b, w_pw, b_pw, w_out, loss_target, m_ada_w, m_ada_b, m_norm_g, m_w_in, m_q_lat_g, m_w_q_up, m_kv_lat_g, m_w_kv_up, m_q_norm_g, m_k_norm_g, m_glu_b, m_dw_w, m_dw_b, m_conv_ln_g, m_conv_ln_b, m_w_pw, m_b_pw, m_w_out, v_ada_w, v_ada_b, v_norm_g, v_w_in, v_q_lat_g, v_w_q_up, v_kv_lat_g, v_w_kv_up, v_q_norm_g, v_k_norm_g, v_glu_b, v_dw_w, v_dw_b, v_conv_ln_g, v_conv_ln_b, v_w_pw, v_b_pw, v_w_out):
    nl = 2
    s, d = x.shape[1], x.shape[2]
    xi, yi, ci = lax.axis_index("x"), lax.axis_index("y"), lax.axis_index("c")
    shard = 2 * xi + yi
    me = 4 * xi + 2 * yi + ci
    cidx = jnp.reshape(ci, (1,)).astype(jnp.int32)
    jc = jnp.stack([shard, ci]).astype(jnp.int32)
    x0 = x.reshape(s, d)
    target = loss_target.reshape(s, d)

    c_all = _allgather8(c.reshape(8, d // 8), "gather_c").reshape(8, d)
    n_ada = ada_w.shape[2]
    ada_b_shard = lax.dynamic_slice_in_dim(ada_b, shard * n_ada, n_ada, axis=1)[:, None, :]
    mod_shard = _modulation(c_all, ada_w, ada_b_shard)
    mod_all = _allgather8(mod_shard.reshape(nl * 8, n_ada), "gather_mod")
    mod_rows = lax.dynamic_index_in_dim(mod_all.reshape(4, 2, nl, 8, n_ada)[:, 0], me, axis=2, keepdims=False)
    mod_me = jnp.transpose(mod_rows, (1, 0, 2)).reshape(nl, 3, 1, d)

    tr = lambda a: jnp.transpose(a, (0, 2, 1))
    w_in_t = tr(w_in).astype(BF16)
    wq = w_q_up.reshape(nl, QL, 2, QK)
    wq = jnp.concatenate([wq, wq[..., NOPE:]], axis=-1)
    wq = jnp.transpose(wq, (0, 2, 1, 3)).reshape(nl, 2 * QL, HP).astype(BF16)
    dw_pad = jnp.pad(dw_w, ((0, 0), (0, HALO - CONV_K), (0, 0)))
    local = [w_in_t, wq, w_kv_up.astype(BF16), dw_pad, w_pw.astype(BF16), w_out.astype(BF16)]

    def kernel_layouts(bufs):
        w_in_g, wq_g, wkv_g, dw_g, wpw_g, wout_g = bufs
        heads_side_by_side = lambda a, rows: jnp.transpose(a.reshape(1, -1, rows, a.shape[-1]), (0, 2, 1, 3)).reshape(1, rows, -1)
        return dict(w_in=w_in_g.reshape(1, 4 * w_in_g.shape[2], d), wq=heads_side_by_side(wq_g, QL), wkv=heads_side_by_side(wkv_g, KVL),
                    dw=jnp.transpose(dw_g, (0, 2, 1, 3)).reshape(1, HALO, 4 * dw_g.shape[3]),
                    wpw=wpw_g.reshape(1, 4 * wpw_g.shape[2], wpw_g.shape[3]), wout=wout_g.reshape(1, 4 * wout_g.shape[2], d))

    w_in_all = [_run_alone(_gather_hand_on(_run_alone(_gather_start([local[0][0:1]]), "gather_w_in0")), "gather_w_in0_hand_on"), None]
    others_start = [_gather_start([a[l:l + 1] for a in local[1:]]) for l in range(nl)]
    w_in1_start = _gather_start([local[0][1:2]])
    wts = [None] * nl

    cos, sin = _rope_tables(positions.reshape(s, 1))
    row = lambda a, l: a[l][None, :]

    saved = []
    xl = x0
    for l in range(nl):
        shift, scale, gate = mod_me[l, 0], mod_me[l, 1], mod_me[l, 2]
        in_args = (xl, row(norm_g, l), scale, shift, w_in_all[l][0].reshape(1, -1, d), 0, f"inproj_fwd{l}")
        if l == 0:
            (hb, za, zkr, zb), landed = _inproj_fwd(*in_args, carried=others_start[0])
            others = _run_alone(_gather_hand_on(landed), "gather_others0_hand_on")
        else:
            (hb, za, zkr, zb), others = _inproj_fwd(*in_args, carried=_gather_hand_on(others1_landed))
        w = wts[l] = kernel_layouts(w_in_all[l] + others)
        gains = (row(q_lat_g, l), row(kv_lat_g, l), _dup_gain(q_norm_g[l]), _dup_gain(k_norm_g[l]))
        q, k, v = _mla_prep_fwd(za, zkr, cos, sin, w["wq"], w["wkv"], *gains, 0, f"mla_prep_fwd{l}")
        conv_args = (zb, row(glu_b, l), w["dw"], row(dw_b, l), row(conv_ln_g, l), row(conv_ln_b, l), w["wpw"], row(b_pw, l), 0)
        if l == 0:
            (o, lse), landed = _flash_fwd(q, k, v, f"flash_fwd{l}", carried=w_in1_start)
            (cv, pw), w_in_all[1] = _conv_fwd(*conv_args, f"conv_fwd{l}", carried=_gather_hand_on(landed))
            (xn, mixb), others1_landed = _outproj_fwd(xl, o, zb, pw, gate, w["wout"], 0, f"outproj_fwd{l}", carried=others_start[1])
        else:
            o, lse = _flash_fwd(q, k, v, f"flash_fwd{l}")
            cv, pw = _conv_fwd(*conv_args, f"conv_fwd{l}")
            xn, mixb = _outproj_fwd(xl, o, zb, pw, gate, w["wout"], 0, f"outproj_fwd{l}")
        saved.append(dict(x=xl, hb=hb, za=za, zkr=zkr, zb=zb, q=q, k=k, v=v, o=o, lse=lse, cv=cv, pw=pw, mixb=mixb, gains=gains))
        xl = xn

    tok_loss, dx = _loss_head(xl, target)
    loss = lax.psum(jnp.sum(tok_loss), ("x", "y", "c"))

    big = [None] * nl
    small = [None] * nl
    shards_of = lambda gs: [g.reshape(4, g.shape[0] // 4, g.shape[1]) for g in gs]
    pair_sums = lambda l, parts, theirs: [_pair_sum(p, t, cidx, f"pair_sum{l}_{e}") for e, (p, t) in enumerate(zip(parts, theirs))]
    chip_sums = lambda l, sums, landed: [_chip_sum(sm, ld, jc, f"chip_sum{l}_{e}") for e, (sm, ld) in enumerate(zip(sums, landed))]
    halves = [None] * nl
    for l in reversed(range(nl)):
        sv, w = saved[l], wts[l]
        shift, scale, gate = mod_me[l, 0], mod_me[l, 1], mod_me[l, 2]
        dxb, do, delta, dzb, dpw = _outproj_bwd(dx, gate, w["wout"], sv["o"], sv["zb"], sv["pw"], 0, f"outproj_bwd{l}")
        out_args = (sv["mixb"], dxb, gate, w["wout"], 0, f"grad_w_out{l}")
        attn_args = (sv["q"], sv["k"], sv["v"])
        if l == 0:
            parts = shards_of(big[1])
            (g_out, dgate), theirs = _grad_w_out(*out_args, carried=_pair_exchange(parts))
            sums = pair_sums(1, parts, theirs)
            (dq, dk, dv), landed = _flash_bwd(*attn_args, do, sv["lse"], delta, f"flash_bwd{l}", carried=_chip_scatter(sums))
            halves[1] = chip_sums(1, sums, landed)
        else:
            g_out, dgate = _grad_w_out(*out_args)
            dq, dk, dv = _flash_bwd(*attn_args, do, sv["lse"], delta, f"flash_bwd{l}")
        dza, dzkr, g_q, g_kv, dgql, dgkvl, dgq, dgk = _mla_prep_bwd(
            dq, dk, dv, sv["za"], sv["zkr"], cos, sin, w["wq"], w["wkv"], *sv["gains"], 0, f"mla_prep_bwd{l}")
        dcv, act, dbpw, dlng, dlnb = _pointwise_bwd(dpw, sv["cv"], row(conv_ln_g, l), row(conv_ln_b, l), w["wpw"], 0, f"pointwise_bwd{l}")
        g_pw, = _grad_tn(act, dpw, f"grad_w_pw{l}")
        by_shard = lambda g, n: jnp.transpose(g.reshape(g.shape[0], -1, n), (1, 0, 2)).reshape(-1, n)
        early = [by_shard(g_q, HP), by_shard(g_kv, 512), g_pw, g_out]
        conv_args = (dcv, sv["zb"], dzb, row(glu_b, l), w["dw"], 0, f"conv_bwd{l}")
        in_args = (sv["x"], dx, row(norm_g, l), scale, shift, w["w_in"], 0, f"inproj_bwd{l}")
        if l == 0:
            parts = shards_of(early)
            (dzb, g_dw, ddwb, dglub), theirs = _conv_bwd(*conv_args, carried=_pair_exchange(parts))
            sums = pair_sums("0e", parts, theirs)
            (g_in,), landed = _grad_w_in(dza, dzkr, dzb, sv["hb"], f"grad_w_in{l}", carried=_chip_scatter(sums))
            e_q, e_kv, e_pw, e_out = chip_sums("0e", sums, landed)
            parts = shards_of([g_in, g_dw.reshape(4 * HALO, CH)])
            sums = pair_sums("0l", parts, _run_alone(_pair_exchange(parts), "pair_exchange_late0"))
            (dx, dshift, dgg), landed = _inproj_bwd(dza, dzkr, dzb, *in_args, carried=_chip_scatter(sums))
            l_in, l_dw = chip_sums("0l", sums, landed)
            halves[0] = [l_in, e_q, e_kv, l_dw, e_pw, e_out]
        else:
            dzb, g_dw, ddwb, dglub = _conv_bwd(*conv_args)
            dx, dshift, dgg = _inproj_bwd(dza, dzkr, dzb, *in_args)
            g_in, = _grad_w_in(dza, dzkr, dzb, sv["hb"], f"grad_w_in{l}")
        big[l] = [g_in, early[0], early[1], g_dw.reshape(4 * HALO, CH), g_pw, g_out]
        small[l] = dict(ada_b=jnp.concatenate([dshift, dgg * row(norm_g, l), jnp.sum(dgate, axis=0)], axis=1), norm_g=dgg * (1.0 + scale),
                        q_lat_g=dgql, kv_lat_g=dgkvl, q_norm_g=_undup(dgq), k_norm_g=_undup(dgk), glu_b=dglub, dw_b=ddwb,
                        conv_ln_g=dlng, conv_ln_b=dlnb, b_pw=dbpw)
    grad_x = dx.reshape(x.shape)

    names = [n for n, _ in SMALL]
    mine = _pack_small({n: jnp.concatenate([small[0][n], small[1][n]], axis=0) for n in names})
    gathered = _allgather8(mine, "gather_small")
    weights = dict(ada_b=ada_b, norm_g=norm_g, q_lat_g=q_lat_g, kv_lat_g=kv_lat_g, q_norm_g=q_norm_g, k_norm_g=k_norm_g,
                   glu_b=glu_b, dw_b=dw_b, conv_ln_g=conv_ln_g, conv_ln_b=conv_ln_b, b_pw=b_pw)
    m_small = dict(ada_b=m_ada_b, norm_g=m_norm_g, q_lat_g=m_q_lat_g, kv_lat_g=m_kv_lat_g, q_norm_g=m_q_norm_g, k_norm_g=m_k_norm_g,
                   glu_b=m_glu_b, dw_b=m_dw_b, conv_ln_g=m_conv_ln_g, conv_ln_b=m_conv_ln_b, b_pw=m_b_pw)
    v_small = dict(ada_b=v_ada_b, norm_g=v_norm_g, q_lat_g=v_q_lat_g, kv_lat_g=v_kv_lat_g, q_norm_g=v_q_norm_g, k_norm_g=v_k_norm_g,
                   glu_b=v_glu_b, dw_b=v_dw_b, conv_ln_g=v_conv_ln_g, conv_ln_b=v_conv_ln_b, b_pw=v_b_pw)
    widths = {n: weights[n].shape[1] for n in names}
    v_packed = _pack_small({n: jnp.pad(v_small[n], ((0, 0), (0, dict(SMALL)[n] - widths[n])), constant_values=1.0) for n in names})
    small_out = [_unpack_small(a, widths) for a in _small_update(gathered, _pack_small(weights), _pack_small(m_small), v_packed)]

    ada_rows = gathered.reshape(8, nl, -1)[:, :, :3 * d]
    dmod = lax.dynamic_slice_in_dim(jnp.transpose(ada_rows, (1, 0, 2)), shard * n_ada, n_ada, axis=2)
    ada_out = _ada_update(c_all, dmod, ada_w, m_ada_w, v_ada_w)

    full = _run_alone(_pair_complete(halves[0] + halves[1]), "pair_complete")
    per_layer = [full[l * 6:(l + 1) * 6] for l in range(nl)]

    def natural_q(g):
        return jnp.transpose(_undup(g.reshape(2, QL, HP)), (1, 0, 2)).reshape(QL, 2 * QK)

    grads = [[per_layer[l][0], natural_q(per_layer[l][1]), per_layer[l][2], per_layer[l][3][:CONV_K], per_layer[l][4], per_layer[l][5]]
             for l in range(nl)]
    sharded = (("w_in", tr(w_in), tr(m_w_in), tr(v_w_in)), ("w_q_up", w_q_up, m_w_q_up, v_w_q_up),
               ("w_kv_up", w_kv_up, m_w_kv_up, v_w_kv_up), ("dw_w", dw_w, m_dw_w, v_dw_w),
               ("w_pw", w_pw, m_w_pw, v_w_pw), ("w_out", w_out, m_w_out, v_w_out))
    big_out = {name: _adam_update(w, grads[0][e], grads[1][e], m, v, f"adam_{name}") for e, (name, w, m, v) in enumerate(sharded)}
    big_out["w_in"] = [tr(a) for a in big_out["w_in"]]

    order = ["ada_w", "ada_b", "norm_g", "w_in", "q_lat_g", "w_q_up", "kv_lat_g", "w_kv_up", "q_norm_g", "k_norm_g", "glu_b",
             "dw_w", "dw_b", "conv_ln_g", "conv_ln_b", "w_pw", "b_pw", "w_out"]

    def leaf(kind, name):
        if name == "ada_w":
            return ada_out[kind]
        if name in big_out:
            return big_out[name][kind]
        return small_out[kind][name]

    return (loss, grad_x, *[leaf(kind, name) for kind in range(4) for name in order])
```

```python
import functools
import math

import jax
import jax.numpy as jnp
from jax import lax
from jax.experimental import pallas as pl
from jax.experimental.pallas import tpu as pltpu

F32, BF16 = jnp.float32, jnp.bfloat16
MESH = pl.DeviceIdType.MESH
ANY = pl.BlockSpec(memory_space=pl.ANY)

N_HEADS, NOPE, ROPE, VD = 8, 128, 64, 128
QK = NOPE + ROPE
QL, KVL = 512, 256
HP = 256
CONV_K, HALO = 31, 32
ROPE_THETA = 10000.0
EPS = 1e-6
ADAM_LR, ADAM_B1, ADAM_B2, ADAM_EPS, ADAM_WD, ADAM_STEP = 0.001, 0.9, 0.999, 1e-08, 0.01, 10
V7X_VMEM_LIMIT = 56 * 1024 * 1024

NT = (((1,), (1,)), ((), ()))
TN = (((0,), (0,)), ((), ()))
NN = (((1,), (0,)), ((), ()))


def _dot(a, b, dims=NN):
    return lax.dot_general(a, b, dims, preferred_element_type=F32)


def _params(*sem):
    return pltpu.CompilerParams(dimension_semantics=sem or None, vmem_limit_bytes=V7X_VMEM_LIMIT)


def _sigmoid(x):
    return 1.0 / (1.0 + jnp.exp(-x))


def _sum0(x):
    return jnp.sum(x, axis=0, keepdims=True)


def _sum1(x):
    return jnp.sum(x, axis=1, keepdims=True)


def _row(n):
    return pl.BlockSpec((1, n), lambda *_: (0, 0))


def _place():
    x, y, c = lax.axis_index("x"), lax.axis_index("y"), lax.axis_index("c")
    chips = [(1 - x, y), (x, 1 - y), (1 - x, 1 - y)]
    return x, y, c, chips


def _allgather8(v, name):
    r, n = v.shape

    def body(v_ref, out_ref, send_sems, recv_sems, local_sem):
        x, y, c, chips = _place()
        me, sibling = (x, y, c), (x, y, 1 - c)

        def slot(px, py, pc):
            return out_ref.at[4 * px + 2 * py + pc]

        def copy(k, block, to, src=None):
            return pltpu.make_async_remote_copy(
                src_ref=slot(*block) if src is None else src, dst_ref=slot(*block),
                send_sem=send_sems.at[k], recv_sem=recv_sems.at[k], device_id=to, device_id_type=MESH)

        mine = pltpu.make_async_copy(v_ref, slot(*me), local_sem)
        mine.start()
        first = [copy(0, me, sibling, src=v_ref)]
        first += [copy(1 + j, me, (*chip, c), src=v_ref) for j, chip in enumerate(chips)]
        for cp in first:
            cp.start()
        passed = [copy(4 + j, (*chip, c), sibling) for j, chip in enumerate(chips)]
        for j, chip in enumerate(chips):
            copy(1 + j, (*chip, c), me).wait_recv()
            passed[j].start()
        copy(0, sibling, me).wait_recv()
        for j, chip in enumerate(chips):
            copy(4 + j, (*chip, 1 - c), me).wait_recv()
        for cp in first + passed:
            cp.wait_send()
        mine.wait()

    return pl.pallas_call(
        body, name=name, out_shape=jax.ShapeDtypeStruct((8, r, n), v.dtype),
        in_specs=[pl.BlockSpec(memory_space=pltpu.VMEM)], out_specs=pl.BlockSpec(memory_space=pltpu.VMEM),
        scratch_shapes=[pltpu.SemaphoreType.DMA((7,)), pltpu.SemaphoreType.DMA((7,)), pltpu.SemaphoreType.DMA],
    )(v)


class _Carried:
    def __init__(self, operands, results, n_sems, start, finish, aliases=None):
        self.operands, self.results, self.n_sems = operands, results, n_sems
        self.start, self.finish, self.aliases = start, finish, aliases or {}


def _run_alone(carried, name):
    k = len(carried.operands)

    def body(*refs):
        args = (refs[:k], refs[k:k + len(carried.results)], refs[-2], refs[-1])
        carried.start(*args)
        carried.finish(*args)

    outs = pl.pallas_call(
        body, name=name, out_shape=carried.results, in_specs=[ANY] * k, out_specs=[ANY] * len(carried.results),
        input_output_aliases=carried.aliases,
        scratch_shapes=[pltpu.SemaphoreType.DMA((carried.n_sems,)), pltpu.SemaphoreType.DMA((carried.n_sems,))],
    )(*carried.operands)
    return list(outs)


def _call(body, operands, *, name, grid, in_specs, out_specs, out_shape, scratch=(), aliases=None, carried=None):
    params = _params(*(["arbitrary"] * len(grid)))
    n_in, n_out = len(in_specs), len(out_shape)
    if carried is None:
        return pl.pallas_call(body, name=name, grid=grid, in_specs=in_specs, out_specs=out_specs, out_shape=out_shape,
                              scratch_shapes=list(scratch), input_output_aliases=aliases or {}, compiler_params=params)(*operands)
    k_in, k_out = len(carried.operands), len(carried.results)

    def wrapped(*refs):
        ins, outs = refs[:n_in], refs[n_in + k_in:n_in + k_in + n_out]
        comm = (refs[n_in:n_in + k_in], refs[n_in + k_in + n_out:n_in + k_in + n_out + k_out], refs[-2], refs[-1])
        steps = [pl.program_id(a) for a in range(len(grid))]
        first = functools.reduce(jnp.logical_and, [s == 0 for s in steps])
        last = functools.reduce(jnp.logical_and, [s == g - 1 for s, g in zip(steps, grid)])

        @pl.when(first)
        def _():
            carried.start(*comm)

        body(*ins, *outs, *refs[n_in + k_in + n_out + k_out:-2])

        @pl.when(last)
        def _():
            carried.finish(*comm)

    both = dict(aliases or {})
    both.update({n_in + i: n_out + o for i, o in carried.aliases.items()})
    res = pl.pallas_call(
        wrapped, name=name, grid=grid, in_specs=list(in_specs) + [ANY] * k_in, out_specs=list(out_specs) + [ANY] * k_out,
        out_shape=list(out_shape) + list(carried.results), input_output_aliases=both, compiler_params=params,
        scratch_shapes=list(scratch) + [pltpu.SemaphoreType.DMA((carried.n_sems,)), pltpu.SemaphoreType.DMA((carried.n_sems,))],
    )(*operands, *carried.operands)
    return list(res[:n_out]), list(res[n_out:])


def _gather_start(shards):
    ne = len(shards)
    per = 4

    def copies(srcs, dsts, send_sems, recv_sems):
        x, y, c, chips = _place()
        jme = 2 * x + y
        out = []
        for e in range(ne):
            half = srcs[e].shape[2] // 2
            own = pl.ds(pl.multiple_of(c * half, 128), half)
            for k, chip in enumerate(chips):
                out.append(pltpu.make_async_remote_copy(
                    src_ref=srcs[e].at[:, :, own], dst_ref=dsts[e].at[:, jme, :, own], send_sem=send_sems.at[per * e + k],
                    recv_sem=recv_sems.at[per * e + k], device_id=(*chip, c), device_id_type=MESH))
            out.append(pltpu.make_async_remote_copy(
                src_ref=srcs[e], dst_ref=dsts[e].at[:, jme], send_sem=send_sems.at[per * e + 3],
                recv_sem=recv_sems.at[per * e + 3], device_id=(x, y, 1 - c), device_id_type=MESH))
        return out

    def start(*a):
        for cp in copies(*a):
            cp.start()

    def finish(*a):
        for cp in copies(*a):
            cp.wait()

    results = [jax.ShapeDtypeStruct((s.shape[0], 4) + s.shape[1:], s.dtype) for s in shards]
    return _Carried(list(shards), results, per * ne, start, finish)


def _gather_hand_on(bufs):
    ne = len(bufs)

    def copy(e, k, dsts, send_sems, recv_sems, mine):
        x, y, c, chips = _place()
        px, py = chips[k]
        half = dsts[e].shape[3] // 2
        cols = pl.ds(pl.multiple_of((c if mine else 1 - c) * half, 128), half)
        part = dsts[e].at[:, 2 * px + py, :, cols]
        return pltpu.make_async_remote_copy(src_ref=part, dst_ref=part, send_sem=send_sems.at[3 * e + k],
                                            recv_sem=recv_sems.at[3 * e + k], device_id=(x, y, 1 - c), device_id_type=MESH)

    def start(srcs, dsts, send_sems, recv_sems):
        for e in range(ne):
            for k in range(3):
                copy(e, k, dsts, send_sems, recv_sems, True).start()

    def finish(srcs, dsts, send_sems, recv_sems):
        for e in range(ne):
            for k in range(3):
                copy(e, k, dsts, send_sems, recv_sems, True).wait_send()
                copy(e, k, dsts, send_sems, recv_sems, False).wait_recv()

    results = [jax.ShapeDtypeStruct(b.shape, b.dtype) for b in bufs]
    return _Carried(list(bufs), results, 3 * ne, start, finish, aliases={e: e for e in range(ne)})


def _pair_exchange(parts):
    ne = len(parts)

    def copies(srcs, dsts, send_sems, recv_sems):
        x, y, c, _ = _place()
        out = []
        for e in range(ne):
            half = srcs[e].shape[2] // 2
            theirs = pl.ds(pl.multiple_of((1 - c) * half, 128), half)
            out.append(pltpu.make_async_remote_copy(
                src_ref=srcs[e].at[:, :, theirs], dst_ref=dsts[e], send_sem=send_sems.at[e],
                recv_sem=recv_sems.at[e], device_id=(x, y, 1 - c), device_id_type=MESH))
        return out

    def start(*a):
        for cp in copies(*a):
            cp.start()

    def finish(*a):
        for cp in copies(*a):
            cp.wait()

    results = [jax.ShapeDtypeStruct(p.shape[:2] + (p.shape[2] // 2,), p.dtype) for p in parts]
    return _Carried(list(parts), results, ne, start, finish)


def _chip_scatter(sums):
    ne = len(sums)

    def copies(srcs, dsts, send_sems, recv_sems):
        x, y, c, chips = _place()
        return [pltpu.make_async_remote_copy(
                    src_ref=srcs[e].at[2 * px + py], dst_ref=dsts[e].at[k], send_sem=send_sems.at[3 * e + k],
                    recv_sem=recv_sems.at[3 * e + k], device_id=(px, py, c), device_id_type=MESH)
                for e in range(ne) for k, (px, py) in enumerate(chips)]

    def start(*a):
        for cp in copies(*a):
            cp.start()

    def finish(*a):
        for cp in copies(*a):
            cp.wait()

    results = [jax.ShapeDtypeStruct((3,) + s.shape[1:], s.dtype) for s in sums]
    return _Carried(list(sums), results, 3 * ne, start, finish)


def _pair_complete(grads):
    ne = len(grads)

    def copy(e, dsts, send_sems, recv_sems, mine):
        x, y, c, _ = _place()
        half = dsts[e].shape[1] // 2
        cols = pl.ds(pl.multiple_of((c if mine else 1 - c) * half, 128), half)
        return pltpu.make_async_remote_copy(
            src_ref=dsts[e].at[:, cols], dst_ref=dsts[e].at[:, cols], send_sem=send_sems.at[e],
            recv_sem=recv_sems.at[e], device_id=(x, y, 1 - c), device_id_type=MESH)

    def start(srcs, dsts, send_sems, recv_sems):
        for e in range(ne):
            copy(e, dsts, send_sems, recv_sems, True).start()

    def finish(srcs, dsts, send_sems, recv_sems):
        for e in range(ne):
            copy(e, dsts, send_sems, recv_sems, True).wait_send()
            copy(e, dsts, send_sems, recv_sems, False).wait_recv()

    results = [jax.ShapeDtypeStruct(g.shape, g.dtype) for g in grads]
    return _Carried(list(grads), results, ne, start, finish, aliases={e: e for e in range(ne)})


def _pair_sum(part, theirs, cidx, name):
    _, r, n = part.shape
    half = n // 2

    def body(c_ref, p_ref, t_ref, o_ref):
        o_ref[...] = (p_ref[...].astype(F32) + t_ref[...].astype(F32)).astype(BF16)

    gs = pltpu.PrefetchScalarGridSpec(
        num_scalar_prefetch=1, grid=(4,),
        in_specs=[pl.BlockSpec((1, r, half), lambda j, c: (j, 0, c[0])),
                  pl.BlockSpec((1, r, half), lambda j, c: (j, 0, 0))],
        out_specs=pl.BlockSpec((1, r, half), lambda j, c: (j, 0, 0)))
    return pl.pallas_call(body, name=name, grid_spec=gs, out_shape=jax.ShapeDtypeStruct((4, r, half), BF16),
                          compiler_params=_params("arbitrary"))(cidx, part, theirs)


def _chip_sum(sums, landed, jc, name):
    _, r, half = sums.shape

    def body(jc_ref, s_ref, l_ref, o_ref):
        acc = s_ref[0].astype(F32)
        for k in range(3):
            acc = acc + l_ref[k].astype(F32)
        o_ref[...] = acc

    gs = pltpu.PrefetchScalarGridSpec(
        num_scalar_prefetch=1, grid=(1,),
        in_specs=[pl.BlockSpec((1, r, half), lambda i, jc: (jc[0], 0, 0)),
                  pl.BlockSpec((3, r, half), lambda i, jc: (0, 0, 0))],
        out_specs=pl.BlockSpec((r, half), lambda i, jc: (0, jc[1])))
    return pl.pallas_call(body, name=name, grid_spec=gs, out_shape=jax.ShapeDtypeStruct((r, 2 * half), F32),
                          compiler_params=_params("arbitrary"))(jc, sums, landed)


def _rope_tables(pos):
    s = pos.shape[0]
    lane = jnp.arange(128)
    inv = 1.0 / (ROPE_THETA ** ((2 * (lane % 32)).astype(F32) / ROPE))
    keep = (lane < 64).astype(F32)
    sign = jnp.where(lane < 32, -1.0, 1.0).astype(F32) * keep
    consts = jnp.stack([inv.astype(F32), keep, sign])[:, None, :]

    def body(p_ref, k_ref, c_ref, s_ref):
        ang = p_ref[...].astype(F32) * k_ref[0]
        c_ref[...] = jnp.cos(ang) * k_ref[1]
        s_ref[...] = jnp.sin(ang) * k_ref[2]

    tm = min(s, 1024)
    return pl.pallas_call(
        body, name="rope_tables", grid=(s // tm,),
        in_specs=[pl.BlockSpec((tm, 1), lambda i: (i, 0)), pl.BlockSpec((3, 1, 128), lambda i: (0, 0, 0))],
        out_specs=[pl.BlockSpec((tm, 128), lambda i: (i, 0))] * 2,
        out_shape=[jax.ShapeDtypeStruct((s, 128), F32)] * 2, compiler_params=_params("arbitrary"),
    )(pos, consts)


def _modulation(c_all, ada_w, ada_b_shard):
    nl, d, n = ada_w.shape
    tn = 512

    def body(c_ref, w_ref, b_ref, o_ref):
        cv = c_ref[...]
        act = (cv * _sigmoid(cv)).astype(BF16)
        o_ref[...] = _dot(act, w_ref[...].astype(BF16)) + b_ref[...]

    return pl.pallas_call(
        body, name="modulation", grid=(nl, n // tn),
        in_specs=[pl.BlockSpec((8, d), lambda l, j: (0, 0)), pl.BlockSpec((None, d, tn), lambda l, j: (l, 0, j)),
                  pl.BlockSpec((None, 1, tn), lambda l, j: (l, 0, j))],
        out_specs=pl.BlockSpec((None, 8, tn), lambda l, j: (l, 0, j)),
        out_shape=jax.ShapeDtypeStruct((nl, 8, n), F32), compiler_params=_params("arbitrary", "arbitrary"),
    )(c_all, ada_w, ada_b_shard)


W_ROWS = 4992
W_PIECES = ((0, 0, 832), (832, 768, 64), (896, 1856, 2048), (2944, 832, 1024), (3968, 3904, 1024))


def _load_w_in(w_hbm, w_vmem, sems):
    cps = [pltpu.make_async_copy(w_hbm.at[pl.ds(src, n)], w_vmem.at[pl.ds(dst, n)], sems.at[i])
           for i, (dst, src, n) in enumerate(W_PIECES)]
    for cp in cps:
        cp.start()
    for cp in cps:
        cp.wait()


def _inproj_fwd(x, g, scale, shift, w_int, layer, name, carried=None):
    s, d = x.shape
    tm = min(s, 256)

    def body(x_ref, g_ref, sc_ref, sh_ref, w_hbm, hb_ref, za_ref, zkr_ref, zb_ref, w_vmem, sems):
        @pl.when(pl.program_id(0) == 0)
        def _():
            _load_w_in(w_hbm.at[layer], w_vmem, sems)

        xv = x_ref[...]
        rstd = lax.rsqrt(jnp.mean(xv * xv, axis=1, keepdims=True) + EPS)
        h = (xv * rstd) * g_ref[...] * (1.0 + sc_ref[...]) + sh_ref[...]
        hb = h.astype(BF16)
        hb_ref[...] = hb
        za_ref[...] = _dot(hb, w_vmem[0:768], NT)
        zkr_ref[...] = _dot(hb, w_vmem[768:896], NT)
        zb_ref[...] = _dot(hb, w_vmem[896:W_ROWS], NT)

    return _call(
        body, (x, g, scale, shift, w_int), name=name, grid=(s // tm,), carried=carried,
        in_specs=[pl.BlockSpec((tm, d), lambda i: (i, 0)), _row(d), _row(d), _row(d), ANY],
        out_specs=[pl.BlockSpec((tm, d), lambda i: (i, 0)), pl.BlockSpec((tm, 768), lambda i: (i, 0)),
                   pl.BlockSpec((tm, 128), lambda i: (i, 0)), pl.BlockSpec((tm, 4096), lambda i: (i, 0))],
        out_shape=[jax.ShapeDtypeStruct((s, d), BF16), jax.ShapeDtypeStruct((s, 768), F32),
                   jax.ShapeDtypeStruct((s, 128), F32), jax.ShapeDtypeStruct((s, 4096), F32)],
        scratch=[pltpu.VMEM((W_ROWS, d), BF16), pltpu.SemaphoreType.DMA((len(W_PIECES),))])


def _rope(yv, cos, sin):
    return yv * cos + pltpu.roll(yv, 32, axis=1) * sin


def _mla_prep_fwd(za, zkr, cos, sin, wq, wkv, gql, gkvl, gq2, gk2, layer, name, carried=None):
    s = za.shape[0]
    tm = min(s, 256)

    def body(za_ref, zkr_ref, cos_ref, sin_ref, wq_ref, wkv_ref, gql_ref, gkvl_ref, gq_ref, gk_ref, q_ref, k_ref, v_ref):
        zq, zkv = za_ref[:, 0:QL], za_ref[:, QL:QL + KVL]
        qn = (zq * lax.rsqrt(jnp.mean(zq * zq, axis=1, keepdims=True) + EPS) * gql_ref[...]).astype(BF16)
        kvn = (zkv * lax.rsqrt(jnp.mean(zkv * zkv, axis=1, keepdims=True) + EPS) * gkvl_ref[...]).astype(BF16)
        kr = zkr_ref[...]
        kr_ss = 0.5 * _sum1(kr * kr)
        cos, sin = cos_ref[...], sin_ref[...]
        gq, gk = gq_ref[...] * SCORE_SCALE, gk_ref[...]
        qr_all, kvr_all = _dot(qn, wq_ref[...]), _dot(kvn, wkv_ref[...])
        for h in range(N_HEADS):
            qr = qr_all[:, h * HP:(h + 1) * HP]
            n, yv = qr[:, :NOPE], qr[:, NOPE:]
            rstd = lax.rsqrt((_sum1(n * n) + 0.5 * _sum1(yv * yv)) * (1.0 / QK) + EPS)
            q_ref[h, :, 0:NOPE] = (n * rstd * gq[:, :NOPE]).astype(BF16)
            q_ref[h, :, NOPE:HP] = _rope(yv * rstd * gq[:, NOPE:], cos, sin).astype(BF16)
            kvr = kvr_all[:, h * HP:(h + 1) * HP]
            kn, vv = kvr[:, :NOPE], kvr[:, NOPE:]
            rstd = lax.rsqrt((_sum1(kn * kn) + kr_ss) * (1.0 / QK) + EPS)
            k_ref[h, :, 0:NOPE] = (kn * rstd * gk[:, :NOPE]).astype(BF16)
            k_ref[h, :, NOPE:HP] = _rope(kr * rstd * gk[:, NOPE:], cos, sin).astype(BF16)
            v_ref[h] = vv.astype(BF16)

    tile = lambda n: pl.BlockSpec((tm, n), lambda i: (i, 0))
    return _call(
        body, (za, zkr, cos, sin, wq, wkv, gql, gkvl, gq2, gk2), name=name, grid=(s // tm,), carried=carried,
        in_specs=[tile(768), tile(128), tile(128), tile(128),
                  pl.BlockSpec((None, QL, N_HEADS * HP), lambda i: (layer, 0, 0)),
                  pl.BlockSpec((None, KVL, N_HEADS * HP), lambda i: (layer, 0, 0)),
                  _row(QL), _row(KVL), _row(HP), _row(HP)],
        out_specs=[pl.BlockSpec((N_HEADS, tm, HP), lambda i: (0, i, 0)), pl.BlockSpec((N_HEADS, tm, HP), lambda i: (0, i, 0)),
                   pl.BlockSpec((N_HEADS, tm, VD), lambda i: (0, i, 0))],
        out_shape=[jax.ShapeDtypeStruct((N_HEADS, s, HP), BF16), jax.ShapeDtypeStruct((N_HEADS, s, HP), BF16),
                   jax.ShapeDtypeStruct((N_HEADS, s, VD), BF16)])


SCORE_SCALE = 1.0 / math.sqrt(QK)
MASKED = -1e30


def _flash_fwd(q, k, v, name, carried=None):
    s = q.shape[1]
    t = min(s, 1024)

    def body(q_ref, k_ref, v_ref, o_ref, lse_ref):
        i = pl.program_id(1)
        qb = q_ref[...]
        row = lax.broadcasted_iota(jnp.int32, (t, t), 0)
        col = lax.broadcasted_iota(jnp.int32, (t, t), 1)

        def block(j):
            return pl.ds(pl.multiple_of(j * t, t), t)

        def scores(j):
            return _dot(qb, k_ref[block(j), :], NT)

        def update(j, sc, m, l, acc, diagonal):
            if diagonal:
                sc = jnp.where(col <= row, sc, MASKED)
            m_new = jnp.maximum(m, jnp.max(sc, axis=1, keepdims=True))
            p = jnp.exp(sc - m_new)
            alpha = jnp.exp(m - m_new)
            return m_new, alpha * l + _sum1(p), alpha * acc + _dot(p.astype(BF16), v_ref[block(j), :])

        init = (jnp.full((t, 1), MASKED, F32), jnp.zeros((t, 1), F32), jnp.zeros((t, VD), F32))
        carry = lax.fori_loop(0, i, lambda j, cr: update(j, scores(j), *cr, False), init)
        m, l, acc = update(i, scores(i), *carry, True)
        o_ref[...] = acc / l
        lse_ref[...] = m + jnp.log(l)

    return _call(
        body, (q, k, v), name=name, grid=(N_HEADS, s // t), carried=carried,
        in_specs=[pl.BlockSpec((None, t, HP), lambda h, i: (h, i, 0)), pl.BlockSpec((None, s, HP), lambda h, i: (h, 0, 0)),
                  pl.BlockSpec((None, s, VD), lambda h, i: (h, 0, 0))],
        out_specs=[pl.BlockSpec((t, VD), lambda h, i: (i, h)), pl.BlockSpec((None, t, 1), lambda h, i: (h, i, 0))],
        out_shape=[jax.ShapeDtypeStruct((s, N_HEADS * VD), F32), jax.ShapeDtypeStruct((N_HEADS, s, 1), F32)])


CH, RC = 256, 64


PH_ROWS_LESS = 8


def _glu(val, gate, bias):
    c = val.shape[1]
    return (val + bias[:, :c]) * _sigmoid(gate + bias[:, c:])


def _make_phases(buf, phases, cc):
    rows = buf.shape[0] - PH_ROWS_LESS
    for b in range(1, 8):
        phases[b - 1] = buf[pl.ds(b, rows), cc:cc + CH]


def _window(buf, phases, cc, shift, r0):
    a, b = divmod(shift, 8)
    if b == 0:
        return buf[r0 + 8 * a:r0 + 8 * a + RC, cc:cc + CH]
    return phases[b - 1, r0 + 8 * a:r0 + 8 * a + RC, :]


def _conv_fwd(zb, glu_b, dw, dwb, lng, lnb, wpw, bpw, layer, name, carried=None):
    s = zb.shape[0]
    dc = dwb.shape[1]
    tm = min(s, 256)
    hb = tm // HALO

    def body(val_ref, gate_ref, valh_ref, gateh_ref, glub_ref, dw_ref, dwb_ref, lng_ref, lnb_ref, wpw_ref, bpw_ref,
             cv_ref, pw_ref, ubuf, uph):
        i = pl.program_id(0)
        bias = glub_ref[...]
        ubuf[HALO:, :] = _glu(val_ref[...], gate_ref[...], bias)
        uh = _glu(valh_ref[...], gateh_ref[...], bias)
        ubuf[0:HALO, :] = jnp.where(i > 0, uh, 0.0)
        for cc in range(0, dc, CH):
            _make_phases(ubuf, uph, cc)
            for r0 in range(0, tm, RC):
                acc = jnp.zeros((RC, CH), F32)
                for j in range(CONV_K):
                    acc = acc + _window(ubuf, uph, cc, HALO - (CONV_K - 1) + j, r0) * dw_ref[j:j + 1, cc:cc + CH]
                cv_ref[r0:r0 + RC, cc:cc + CH] = acc + dwb_ref[:, cc:cc + CH]
        cv = cv_ref[...]
        dv = cv - jnp.mean(cv, axis=1, keepdims=True)
        yl = dv * lax.rsqrt(jnp.mean(dv * dv, axis=1, keepdims=True) + EPS) * lng_ref[...] + lnb_ref[...]
        act = (yl * _sigmoid(yl)).astype(BF16)
        pw_ref[...] = _dot(act, wpw_ref[...]) + bpw_ref[...]

    return _call(
        body, (zb, zb, zb, zb, glu_b, dw, dwb, lng, lnb, wpw, bpw), name=name, grid=(s // tm,), carried=carried,
        in_specs=[pl.BlockSpec((tm, dc), lambda i: (i, 0)), pl.BlockSpec((tm, dc), lambda i: (i, 1)),
                  pl.BlockSpec((HALO, dc), lambda i: (jnp.maximum(i * hb - 1, 0), 0)),
                  pl.BlockSpec((HALO, dc), lambda i: (jnp.maximum(i * hb - 1, 0), 1)),
                  _row(2 * dc), pl.BlockSpec((None, HALO, dc), lambda i: (layer, 0, 0)), _row(dc), _row(dc), _row(dc),
                  pl.BlockSpec((None, dc, dc), lambda i: (layer, 0, 0)), _row(dc)],
        out_specs=[pl.BlockSpec((tm, dc), lambda i: (i, 0))] * 2,
        out_shape=[jax.ShapeDtypeStruct((s, dc), F32)] * 2,
        scratch=[pltpu.VMEM((tm + HALO, dc), F32), pltpu.VMEM((7, tm + HALO - PH_ROWS_LESS, CH), F32)])


def _silu_parts(z):
    sg = _sigmoid(z)
    return z * sg, sg * (1.0 + z * (1.0 - sg))


def _outproj_fwd(x, o, zb, pw, gate, wout, layer, name, carried=None, target=None):
    s, d = x.shape
    dm = o.shape[1]
    tm = min(s, 256)

    def project(x_ref, o_ref, mg_ref, cg_ref, pw_ref, gate_ref, w_ref, mix_ref):
        mg, cg = mg_ref[...], cg_ref[...]
        mix_ref[:, 0:dm] = (o_ref[...] * (mg * _sigmoid(mg))).astype(BF16)
        mix_ref[:, dm:] = (pw_ref[...] * (cg * _sigmoid(cg))).astype(BF16)
        return x_ref[...] + gate_ref[...] * _dot(mix_ref[...], w_ref[...])

    def body(*refs):
        xn_ref, mix_ref = refs[7:]
        xn_ref[...] = project(*refs[:7], mix_ref)

    def body_with_loss(*refs):
        t_ref, l_ref, dx_ref, mix_ref = refs[7:]
        err = project(*refs[:7], mix_ref) - t_ref[...]
        l_ref[...] = 0.5 * jnp.mean(err * err, axis=1, keepdims=True)
        dx_ref[...] = err * (1.0 / d)

    tile = lambda n, j=0: pl.BlockSpec((tm, n), lambda i: (i, j))
    in_specs = [tile(d), tile(dm), tile(dm, 2), tile(dm, 3), tile(dm), _row(d), pl.BlockSpec((None, 2 * dm, d), lambda i: (layer, 0, 0))]
    if target is None:
        return _call(body, (x, o, zb, zb, pw, gate, wout), name=name, grid=(s // tm,), carried=carried, in_specs=in_specs,
                     out_specs=[tile(d), tile(2 * dm)],
                     out_shape=[jax.ShapeDtypeStruct((s, d), F32), jax.ShapeDtypeStruct((s, 2 * dm), BF16)])
    return _call(body_with_loss, (x, o, zb, zb, pw, gate, wout, target), name=name, grid=(s // tm,), carried=carried,
                 in_specs=in_specs + [tile(d)], out_specs=[tile(1), tile(d), tile(2 * dm)],
                 out_shape=[jax.ShapeDtypeStruct((s, 1), F32), jax.ShapeDtypeStruct((s, d), F32), jax.ShapeDtypeStruct((s, 2 * dm), BF16)])


def _grad_tn(a, b, name, carried=None):
    s, n = a.shape
    m = b.shape[1]
    tn, ts = min(n, 1024), min(s, 512)
    nt = s // ts

    def body(a_ref, b_ref, o_ref, acc):
        t = pl.program_id(1)

        @pl.when(t == 0)
        def _():
            acc[...] = jnp.zeros_like(acc)

        acc[...] += _dot(a_ref[...].astype(BF16), b_ref[...].astype(BF16), TN)

        @pl.when(t == nt - 1)
        def _():
            o_ref[...] = acc[...].astype(BF16)

    return _call(
        body, (a, b), name=name, grid=(n // tn, nt), carried=carried,
        in_specs=[pl.BlockSpec((ts, tn), lambda r, t: (t, r)), pl.BlockSpec((ts, m), lambda r, t: (t, 0))],
        out_specs=[pl.BlockSpec((tn, m), lambda r, t: (r, 0))], out_shape=[jax.ShapeDtypeStruct((n, m), BF16)],
        scratch=[pltpu.VMEM((tn, m), F32)])


def _grad_w_out(mixb, dxb, gate, wout, layer, name, carried=None):
    s, n = mixb.shape
    d = dxb.shape[1]
    tn, ts = min(n, 1024), min(s, 512)
    nt = s // ts

    def body(a_ref, b_ref, gate_ref, w_ref, g_ref, dgate_ref, acc):
        t = pl.program_id(1)

        @pl.when(t == 0)
        def _():
            acc[...] = jnp.zeros_like(acc)

        acc[...] += _dot(a_ref[...], b_ref[...], TN)

        @pl.when(t == nt - 1)
        def _():
            m = acc[...]
            dgate_ref[...] = _sum0(m * w_ref[...].astype(F32))
            g_ref[...] = (m * gate_ref[...]).astype(BF16)

    return _call(
        body, (mixb, dxb, gate, wout), name=name, grid=(n // tn, nt), carried=carried,
        in_specs=[pl.BlockSpec((ts, tn), lambda r, t: (t, r)), pl.BlockSpec((ts, d), lambda r, t: (t, 0)), _row(d),
                  pl.BlockSpec((None, tn, d), lambda r, t: (layer, r, 0))],
        out_specs=[pl.BlockSpec((tn, d), lambda r, t: (r, 0)), pl.BlockSpec((None, 1, d), lambda r, t: (r, 0, 0))],
        out_shape=[jax.ShapeDtypeStruct((n, d), BF16), jax.ShapeDtypeStruct((n // tn, 1, d), F32)],
        scratch=[pltpu.VMEM((tn, d), F32)])


def _outproj_bwd(dxo, gate, wout, o, zb, pw, layer, name, carried=None):
    s, d = dxo.shape
    dm = o.shape[1]
    tm = min(s, 256)

    def body(dx_ref, gate_ref, w_ref, o_ref, mg_ref, cg_ref, pw_ref, dxb_ref, do_ref, delta_ref, dzb_ref, dpw_ref):
        dx = dx_ref[...]
        dxb_ref[...] = dx.astype(BF16)
        dmix = _dot((dx * gate_ref[...]).astype(BF16), w_ref[...], NT)
        da, db = dmix[:, :dm], dmix[:, dm:]
        ov = o_ref[...]
        silu_m, dsilu_m = _silu_parts(mg_ref[...])
        do = da * silu_m
        do_ref[...] = do.astype(BF16)
        prod = do * ov
        for h in range(N_HEADS):
            delta_ref[h] = _sum1(prod[:, h * VD:(h + 1) * VD])
        dzb_ref[:, 0:dm] = da * ov * dsilu_m
        silu_c, dsilu_c = _silu_parts(cg_ref[...])
        dpw_ref[...] = db * silu_c
        dzb_ref[:, dm:] = db * pw_ref[...] * dsilu_c

    tile = lambda n, j=0: pl.BlockSpec((tm, n), lambda i: (i, j))
    return _call(
        body, (dxo, gate, wout, o, zb, zb, pw), name=name, grid=(s // tm,), carried=carried,
        in_specs=[tile(d), _row(d), pl.BlockSpec((None, 2 * dm, d), lambda i: (layer, 0, 0)),
                  tile(dm), tile(dm, 2), tile(dm, 3), tile(dm)],
        out_specs=[tile(d), tile(dm), pl.BlockSpec((N_HEADS, tm, 1), lambda i: (0, i, 0)), tile(2 * dm, 1), tile(dm)],
        out_shape=[jax.ShapeDtypeStruct((s, d), BF16), jax.ShapeDtypeStruct((s, dm), BF16),
                   jax.ShapeDtypeStruct((N_HEADS, s, 1), F32), jax.ShapeDtypeStruct((s, 4 * dm), F32),
                   jax.ShapeDtypeStruct((s, dm), F32)])


def _flash_bwd(q, k, v, do, lse, delta, name, carried=None):
    s = q.shape[1]
    t = min(s, 1024)
    nq = s // t

    def body(q_ref, k_ref, v_ref, do_ref, lse_ref, delta_ref, dq_ref, dk_ref, dv_ref):
        j = pl.program_id(1)

        @pl.when(j == 0)
        def _():
            dq_ref[...] = jnp.zeros_like(dq_ref)

        kb, vb = k_ref[...], v_ref[...]
        row = lax.broadcasted_iota(jnp.int32, (t, t), 0)
        col = lax.broadcasted_iota(jnp.int32, (t, t), 1)

        def block(i):
            return pl.ds(pl.multiple_of(i * t, t), t)

        def scores(i):
            at = block(i)
            return _dot(q_ref[at, :], kb, NT), _dot(do_ref[at, :], vb, NT)

        def update(i, sc, dp, dk, dv, diagonal):
            at = block(i)
            p = jnp.exp(sc - lse_ref[at, :])
            if diagonal:
                p = jnp.where(col <= row, p, 0.0)
            dv = dv + _dot(p.astype(BF16), do_ref[at, :], TN)
            ds = (p * (dp - delta_ref[at, :])).astype(BF16)
            dq_ref[at, :] += _dot(ds, kb)
            return dk + _dot(ds, q_ref[at, :], TN), dv

        carry = update(j, *scores(j), jnp.zeros((t, HP), F32), jnp.zeros((t, VD), F32), True)
        dk, dv = lax.fori_loop(j + 1, nq, lambda i, cr: update(i, *scores(i), *cr, False), carry)
        dk_ref[...] = dk
        dv_ref[...] = dv

    whole = lambda n: pl.BlockSpec((None, s, n), lambda h, j: (h, 0, 0))
    blk = lambda n: pl.BlockSpec((None, t, n), lambda h, j: (h, j, 0))
    return _call(
        body, (q, k, v, do, lse, delta), name=name, grid=(N_HEADS, nq), carried=carried,
        in_specs=[whole(HP), blk(HP), blk(VD), pl.BlockSpec((s, VD), lambda h, j: (0, h)), whole(1), whole(1)],
        out_specs=[whole(HP), blk(HP), blk(VD)],
        out_shape=[jax.ShapeDtypeStruct((N_HEADS, s, HP), F32), jax.ShapeDtypeStruct((N_HEADS, s, HP), F32),
                   jax.ShapeDtypeStruct((N_HEADS, s, VD), F32)])


def _mla_prep_bwd(dq, dk, dv, za, zkr, cos, sin, wq, wkv, gql, gkvl, gq2, gk2, layer, name):
    s = za.shape[0]
    tm = min(s, 512)

    def norm_bwd(n, yv, rstd, gain, d_n_out, d_y_out):
        tn_, ty = n * rstd, yv * rstd
        dgain_n, dgain_y = _sum0(d_n_out * tn_), _sum0(d_y_out * ty)
        dtn, dty = d_n_out * gain[:, :NOPE], d_y_out * gain[:, NOPE:]
        a = (_sum1(dtn * n) + _sum1(dty * yv)) * (rstd * rstd * rstd * (1.0 / QK))
        return rstd * dtn - n * a, rstd * dty - (0.5 * yv) * a, dgain_n, dgain_y

    def rope_bwd(d_out, cos, sin):
        return d_out * cos + pltpu.roll(d_out * sin, 128 - 32, axis=1)

    def latent_bwd(z, gain, dn):
        rstd = lax.rsqrt(jnp.mean(z * z, axis=1, keepdims=True) + EPS)
        zh = z * rstd
        dzh = dn * gain
        return rstd * (dzh - zh * jnp.mean(dzh * zh, axis=1, keepdims=True)), _sum0(dn * zh)

    def body(dq_ref, dk_ref, dv_ref, za_ref, zkr_ref, cos_ref, sin_ref, wq_ref, wkv_ref, gql_ref, gkvl_ref, gq_ref, gk_ref,
             dza_ref, dzkr_ref, gwq_ref, gwkv_ref, dgql_ref, dgkvl_ref, dgq_ref, dgk_ref, gwq_acc, gwkv_acc):
        @pl.when(pl.program_id(0) == 0)
        def _():
            for r in (gwq_acc, gwkv_acc, dgql_ref, dgkvl_ref, dgq_ref, dgk_ref):
                r[...] = jnp.zeros_like(r)

        zq, zkv = za_ref[:, 0:QL], za_ref[:, QL:QL + KVL]
        qf = zq * lax.rsqrt(jnp.mean(zq * zq, axis=1, keepdims=True) + EPS) * gql_ref[...]
        kvf = zkv * lax.rsqrt(jnp.mean(zkv * zkv, axis=1, keepdims=True) + EPS) * gkvl_ref[...]
        qn, kvn = qf.astype(BF16), kvf.astype(BF16)
        qn_t, kvn_t = qf.T.astype(BF16), kvf.T.astype(BF16)
        kr = zkr_ref[...]
        kr_ss = 0.5 * _sum1(kr * kr)
        cos, sin = cos_ref[...], sin_ref[...]
        gq, gk = gq_ref[...], gk_ref[...]
        dkr = jnp.zeros((tm, 128), F32)
        qr_all, kvr_all = _dot(qn, wq_ref[...]), _dot(kvn, wkv_ref[...])
        dqr_all, dkvr_all = [], []
        for h in range(N_HEADS):
            qr = qr_all[:, h * HP:(h + 1) * HP]
            n, yv = qr[:, :NOPE], qr[:, NOPE:]
            rstd = lax.rsqrt((_sum1(n * n) + 0.5 * _sum1(yv * yv)) * (1.0 / QK) + EPS)
            dqh = dq_ref[h] * SCORE_SCALE
            dn, dy, dg_n, dg_y = norm_bwd(n, yv, rstd, gq, dqh[:, :NOPE], rope_bwd(dqh[:, NOPE:], cos, sin))
            dgq_ref[:, 0:NOPE] += dg_n
            dgq_ref[:, NOPE:] += dg_y
            dqr_all += [dn.astype(BF16), dy.astype(BF16)]
            kn = kvr_all[:, h * HP:h * HP + NOPE]
            rstd = lax.rsqrt((_sum1(kn * kn) + kr_ss) * (1.0 / QK) + EPS)
            dkh = dk_ref[h]
            dn, dy, dg_n, dg_y = norm_bwd(kn, kr, rstd, gk, dkh[:, :NOPE], rope_bwd(dkh[:, NOPE:], cos, sin))
            dgk_ref[:, 0:NOPE] += dg_n
            dgk_ref[:, NOPE:] += dg_y
            dkr = dkr + dy
            dkvr_all += [dn.astype(BF16), dv_ref[h].astype(BF16)]

        dqr_all, dkvr_all = jnp.concatenate(dqr_all, axis=1), jnp.concatenate(dkvr_all, axis=1)
        gwq_acc[...] += _dot(qn_t, dqr_all)
        gwkv_acc[...] += _dot(kvn_t, dkvr_all)

        @pl.when(pl.program_id(0) == s // tm - 1)
        def _():
            gwq_ref[...] = gwq_acc[...].astype(BF16)
            gwkv_ref[...] = gwkv_acc[...].astype(BF16)

        dzq, dgql = latent_bwd(zq, gql_ref[...], _dot(dqr_all, wq_ref[...], NT))
        dzkv, dgkvl = latent_bwd(zkv, gkvl_ref[...], _dot(dkvr_all, wkv_ref[...], NT))
        dgql_ref[...] += dgql
        dgkvl_ref[...] += dgkvl
        dza_ref[:, 0:QL] = dzq
        dza_ref[:, QL:] = dzkv
        lane = lax.broadcasted_iota(jnp.int32, (tm, 128), 1)
        dzkr_ref[...] = jnp.where(lane < ROPE, dkr + pltpu.roll(dkr, 64, axis=1), 0.0)

    tile = lambda n: pl.BlockSpec((tm, n), lambda i: (i, 0))
    heads = lambda n: pl.BlockSpec((N_HEADS, tm, n), lambda i: (0, i, 0))
    return pl.pallas_call(
        body, name=name, grid=(s // tm,),
        in_specs=[heads(HP), heads(HP), heads(VD), tile(768), tile(128), tile(128), tile(128),
                  pl.BlockSpec((None, QL, N_HEADS * HP), lambda i: (layer, 0, 0)),
                  pl.BlockSpec((None, KVL, N_HEADS * HP), lambda i: (layer, 0, 0)),
                  _row(QL), _row(KVL), _row(HP), _row(HP)],
        out_specs=[tile(768), tile(128), pl.BlockSpec((QL, N_HEADS * HP), lambda i: (0, 0)),
                   pl.BlockSpec((KVL, N_HEADS * HP), lambda i: (0, 0)), _row(QL), _row(KVL), _row(HP), _row(HP)],
        out_shape=[jax.ShapeDtypeStruct((s, 768), F32), jax.ShapeDtypeStruct((s, 128), F32),
                   jax.ShapeDtypeStruct((QL, N_HEADS * HP), BF16), jax.ShapeDtypeStruct((KVL, N_HEADS * HP), BF16),
                   jax.ShapeDtypeStruct((1, QL), F32), jax.ShapeDtypeStruct((1, KVL), F32),
                   jax.ShapeDtypeStruct((1, HP), F32), jax.ShapeDtypeStruct((1, HP), F32)],
        scratch_shapes=[pltpu.VMEM((QL, N_HEADS * HP), F32), pltpu.VMEM((KVL, N_HEADS * HP), F32)],
        compiler_params=_params("arbitrary"),
    )(dq, dk, dv, za, zkr, cos, sin, wq, wkv, gql, gkvl, gq2, gk2)


def _pointwise_bwd(dpw, cv, lng, lnb, wpw, layer, name):
    s, dc = cv.shape
    tm = min(s, 256)

    def body(dpw_ref, cv_ref, lng_ref, lnb_ref, w_ref, dcv_ref, act_ref, dbpw_ref, dlng_ref, dlnb_ref):
        @pl.when(pl.program_id(0) == 0)
        def _():
            for r in (dbpw_ref, dlng_ref, dlnb_ref):
                r[...] = jnp.zeros_like(r)

        cv = cv_ref[...]
        dv = cv - jnp.mean(cv, axis=1, keepdims=True)
        rstd = lax.rsqrt(jnp.mean(dv * dv, axis=1, keepdims=True) + EPS)
        xh = dv * rstd
        yl = xh * lng_ref[...] + lnb_ref[...]
        silu, dsilu = _silu_parts(yl)
        act_ref[...] = silu.astype(BF16)
        dpw = dpw_ref[...]
        dbpw_ref[...] += _sum0(dpw)
        dyl = _dot(dpw.astype(BF16), w_ref[...], NT) * dsilu
        dlng_ref[...] += _sum0(dyl * xh)
        dlnb_ref[...] += _sum0(dyl)
        dxh = dyl * lng_ref[...]
        dcv_ref[...] = rstd * (dxh - jnp.mean(dxh, axis=1, keepdims=True) - xh * jnp.mean(dxh * xh, axis=1, keepdims=True))

    tile = pl.BlockSpec((tm, dc), lambda i: (i, 0))
    return pl.pallas_call(
        body, name=name, grid=(s // tm,),
        in_specs=[tile, tile, _row(dc), _row(dc), pl.BlockSpec((None, dc, dc), lambda i: (layer, 0, 0))],
        out_specs=[tile, tile, _row(dc), _row(dc), _row(dc)],
        out_shape=[jax.ShapeDtypeStruct((s, dc), F32), jax.ShapeDtypeStruct((s, dc), BF16)] + [jax.ShapeDtypeStruct((1, dc), F32)] * 3,
        compiler_params=_params("arbitrary"),
    )(dpw, cv, lng, lnb, wpw)


def _conv_bwd(dcv, zb, dzb, glu_b, dw, layer, name, carried=None):
    s, dc = dcv.shape
    tm = min(s, 256)
    hb = tm // HALO
    last = s // tm - 1

    def body(dcv_ref, dcvn_ref, val_ref, gate_ref, valh_ref, gateh_ref, glub_ref, dw_ref, _, dzb_ref, gdw_ref, ddwb_ref, dglub_ref,
             ubuf, dbuf, gacc, uph, dph):
        i = pl.program_id(0)

        @pl.when(i == 0)
        def _():
            gacc[...] = jnp.zeros_like(gacc)
            ddwb_ref[...] = jnp.zeros_like(ddwb_ref)
            dglub_ref[...] = jnp.zeros_like(dglub_ref)

        bias = glub_ref[...]
        ubuf[HALO:, :] = _glu(val_ref[...], gate_ref[...], bias)
        ubuf[0:HALO, :] = jnp.where(i > 0, _glu(valh_ref[...], gateh_ref[...], bias), 0.0)
        dcv = dcv_ref[...]
        dbuf[0:tm, :] = dcv
        dbuf[tm:, :] = jnp.where(i < last, dcvn_ref[...], 0.0)
        ddwb_ref[...] += _sum0(dcv)
        for cc in range(0, dc, CH):
            _make_phases(ubuf, uph, cc)
            _make_phases(dbuf, dph, cc)
            for r0 in range(0, tm, RC):
                du = jnp.zeros((RC, CH), F32)
                dpiece = dbuf[r0:r0 + RC, cc:cc + CH]
                for j in range(CONV_K):
                    du = du + _window(dbuf, dph, cc, (CONV_K - 1) - j, r0) * dw_ref[j:j + 1, cc:cc + CH]
                    win = _window(ubuf, uph, cc, HALO - (CONV_K - 1) + j, r0)
                    gacc[j, :, cc:cc + CH] += (dpiece * win).reshape(RC // 8, 8, CH).sum(axis=0)
                a = val_ref[r0:r0 + RC, cc:cc + CH] + bias[:, cc:cc + CH]
                sg = _sigmoid(gate_ref[r0:r0 + RC, cc:cc + CH] + bias[:, dc + cc:dc + cc + CH])
                dzb_ref[r0:r0 + RC, cc:cc + CH] = du * sg
                dzb_ref[r0:r0 + RC, dc + cc:dc + cc + CH] = du * a * sg * (1.0 - sg)
        dglub_ref[...] += _sum0(dzb_ref[...])

        @pl.when(i == last)
        def _():
            total = jnp.sum(gacc[...], axis=1)
            for cc in range(0, dc, CH):
                gdw_ref[cc // CH] = total[:, cc:cc + CH]

    return _call(
        body, (dcv, dcv, zb, zb, zb, zb, glu_b, dw, dzb), name=name, grid=(s // tm,), carried=carried, aliases={8: 0},
        in_specs=[pl.BlockSpec((tm, dc), lambda i: (i, 0)),
                  pl.BlockSpec((HALO, dc), lambda i: (jnp.minimum((i + 1) * hb, s // HALO - 1), 0)),
                  pl.BlockSpec((tm, dc), lambda i: (i, 0)), pl.BlockSpec((tm, dc), lambda i: (i, 1)),
                  pl.BlockSpec((HALO, dc), lambda i: (jnp.maximum(i * hb - 1, 0), 0)),
                  pl.BlockSpec((HALO, dc), lambda i: (jnp.maximum(i * hb - 1, 0), 1)),
                  _row(2 * dc), pl.BlockSpec((None, HALO, dc), lambda i: (layer, 0, 0)), ANY],
        out_specs=[pl.BlockSpec((tm, 2 * dc), lambda i: (i, 0)), pl.BlockSpec((4, HALO, CH), lambda i: (0, 0, 0)),
                   _row(dc), _row(2 * dc)],
        out_shape=[jax.ShapeDtypeStruct(dzb.shape, F32), jax.ShapeDtypeStruct((4, HALO, CH), F32),
                   jax.ShapeDtypeStruct((1, dc), F32), jax.ShapeDtypeStruct((1, 2 * dc), F32)],
        scratch=[pltpu.VMEM((tm + HALO, dc), F32), pltpu.VMEM((tm + HALO, dc), F32), pltpu.VMEM((HALO, 8, dc), F32),
                 pltpu.VMEM((7, tm + HALO - PH_ROWS_LESS, CH), F32), pltpu.VMEM((7, tm + HALO - PH_ROWS_LESS, CH), F32)])


def _inproj_bwd(dza, dzkr, dzb, x, dxo, g, scale, shift, w_int, layer, name, carried=None):
    s, d = x.shape
    tm = min(s, 128)

    def body(dza_ref, dzkr_ref, dzb_ref, x_ref, dxo_ref, g_ref, sc_ref, sh_ref, w_hbm, dx_ref, dsh_ref, dgg_ref, w_vmem, sems):
        @pl.when(pl.program_id(0) == 0)
        def _():
            _load_w_in(w_hbm.at[layer], w_vmem, sems)
            dsh_ref[...] = jnp.zeros_like(dsh_ref)
            dgg_ref[...] = jnp.zeros_like(dgg_ref)

        dh = _dot(dza_ref[...].astype(BF16), w_vmem[0:768])
        dh = dh + _dot(dzkr_ref[...].astype(BF16), w_vmem[768:896])
        dh = dh + _dot(dzb_ref[...].astype(BF16), w_vmem[896:W_ROWS])
        xv = x_ref[...]
        rstd = lax.rsqrt(jnp.mean(xv * xv, axis=1, keepdims=True) + EPS)
        xh = xv * rstd
        dsh_ref[...] += _sum0(dh)
        dgg_ref[...] += _sum0(dh * xh)
        dxh = dh * (g_ref[...] * (1.0 + sc_ref[...]))
        dx_ref[...] = dxo_ref[...] + rstd * (dxh - xh * jnp.mean(dxh * xh, axis=1, keepdims=True))

    tile = lambda n: pl.BlockSpec((tm, n), lambda i: (i, 0))
    return _call(
        body, (dza, dzkr, dzb, x, dxo, g, scale, shift, w_int), name=name, grid=(s // tm,), carried=carried,
        in_specs=[tile(768), tile(128), tile(4096), tile(d), tile(d), _row(d), _row(d), _row(d), ANY],
        out_specs=[tile(d), _row(d), _row(d)],
        out_shape=[jax.ShapeDtypeStruct((s, d), F32), jax.ShapeDtypeStruct((1, d), F32), jax.ShapeDtypeStruct((1, d), F32)],
        scratch=[pltpu.VMEM((W_ROWS, d), BF16), pltpu.SemaphoreType.DMA((len(W_PIECES),))])


def _grad_w_in(dza, dzkr, dzb, hb, name, carried=None):
    s, d = hb.shape
    ts = min(s, 512)
    nt = s // ts
    tiles = ((0, 768), (768, 64), (1856, 1024), (2880, 1024), (832, 1024), (3904, 1024))

    def body(a_ref, kr_ref, b_ref, h_ref, o_hbm, acc, rounded, sem):
        r, t = pl.program_id(0), pl.program_id(1)

        @pl.when(t == 0)
        def _():
            acc[...] = jnp.zeros_like(acc)

        hv = h_ref[...]

        @pl.when(r == 0)
        def _():
            acc[0:768, :] += _dot(a_ref[...].astype(BF16), hv, TN)

        @pl.when(r == 1)
        def _():
            acc[0:128, :] += _dot(kr_ref[...].astype(BF16), hv, TN)

        @pl.when(r >= 2)
        def _():
            acc[...] += _dot(b_ref[...].astype(BF16), hv, TN)

        for tile, (row0, rows) in enumerate(tiles):
            @pl.when((t == nt - 1) & (r == tile))
            def _():
                rounded[0:rows, :] = acc[0:rows, :].astype(BF16)
                cp = pltpu.make_async_copy(rounded.at[pl.ds(0, rows)], o_hbm.at[pl.ds(row0, rows)], sem)
                cp.start()
                cp.wait()

    return _call(
        body, (dza, dzkr, dzb, hb), name=name, grid=(len(tiles), nt), carried=carried,
        in_specs=[pl.BlockSpec((ts, 768), lambda r, t: (jnp.where(r == 0, t, nt - 1), 0)),
                  pl.BlockSpec((ts, 128), lambda r, t: (jnp.where(r == 1, t, jnp.where(r == 0, 0, nt - 1)), 0)),
                  pl.BlockSpec((ts, 1024), lambda r, t: (jnp.where(r >= 2, t, 0), jnp.maximum(r - 2, 0))),
                  pl.BlockSpec((ts, d), lambda r, t: (t, 0))],
        out_specs=[ANY], out_shape=[jax.ShapeDtypeStruct((4928, d), BF16)],
        scratch=[pltpu.VMEM((1024, d), F32), pltpu.VMEM((1024, d), BF16), pltpu.SemaphoreType.DMA])


def _adamw(w, g, m, v):
    m = ADAM_B1 * m + (1.0 - ADAM_B1) * g
    v = ADAM_B2 * v + (1.0 - ADAM_B2) * (g * g)
    m_hat = m / (1.0 - ADAM_B1 ** ADAM_STEP)
    v_hat = v / (1.0 - ADAM_B2 ** ADAM_STEP)
    return -ADAM_LR * (m_hat / (jnp.sqrt(v_hat) + ADAM_EPS) + ADAM_WD * w), m, v


def _adam_update(w, g0, g1, m, v, name):
    _, r, c = w.shape
    fits = [t for t in range(8, r + 1, 8) if r % t == 0 and t * c * 4 <= (1 << 21)]
    tr = max(fits) if fits else r

    def body(w_ref, g0_ref, g1_ref, m_ref, v_ref, g_ref, d_ref, mo_ref, vo_ref):
        g = jnp.where(pl.program_id(0) == 0, g0_ref[...], g1_ref[...])
        g_ref[...] = g
        d_ref[...], mo_ref[...], vo_ref[...] = _adamw(w_ref[...], g, m_ref[...], v_ref[...])

    big = pl.BlockSpec((None, tr, c), lambda l, i: (l, i, 0))
    one = pl.BlockSpec((tr, c), lambda l, i: (i, 0))
    return pl.pallas_call(
        body, name=name, grid=(2, r // tr), in_specs=[big, one, one, big, big], out_specs=[big] * 4,
        out_shape=[jax.ShapeDtypeStruct(w.shape, F32)] * 4, compiler_params=_params("arbitrary", "arbitrary"),
    )(w, g0, g1, m, v)


def _ada_update(c_all, dmod, w, m, v, carried=None):
    nl, d, n = w.shape
    tr = 256

    def body(c_ref, dm_ref, w_ref, m_ref, v_ref, g_ref, d_ref, mo_ref, vo_ref):
        cv = c_ref[...]
        act = (cv * _sigmoid(cv)).astype(BF16)
        g = _dot(act, dm_ref[...].astype(BF16), TN)
        g_ref[...] = g
        d_ref[...], mo_ref[...], vo_ref[...] = _adamw(w_ref[...], g, m_ref[...], v_ref[...])

    big = pl.BlockSpec((None, tr, n), lambda l, i: (l, i, 0))
    return _call(
        body, (c_all, dmod, w, m, v), name="ada_w_update", grid=(nl, d // tr), carried=carried,
        in_specs=[pl.BlockSpec((8, tr), lambda l, i: (0, i)), pl.BlockSpec((None, 8, n), lambda l, i: (l, 0, 0)), big, big, big],
        out_specs=[big] * 4, out_shape=[jax.ShapeDtypeStruct(w.shape, F32)] * 4)


def _small_update(gathered, w, m, v):
    r = w.shape[0]

    def body(ga_ref, w_ref, m_ref, v_ref, g_ref, d_ref, mo_ref, vo_ref):
        g = ga_ref[0]
        for dev in range(1, 8):
            g = g + ga_ref[dev]
        g_ref[...] = g
        d_ref[...], mo_ref[...], vo_ref[...] = _adamw(w_ref[...], g, m_ref[...], v_ref[...])

    return pl.pallas_call(body, name="small_update", out_shape=[jax.ShapeDtypeStruct((r, 128), F32)] * 4,
                          compiler_params=_params())(gathered, w, m, v)


SMALL = (("ada_b", 6144), ("norm_g", 2048), ("q_lat_g", 512), ("kv_lat_g", 256), ("q_norm_g", 256), ("k_norm_g", 256),
         ("glu_b", 2048), ("dw_b", 1024), ("conv_ln_g", 1024), ("conv_ln_b", 1024), ("b_pw", 1024))


def _pack_small(vals):
    cols = []
    for name, width in SMALL:
        a = vals[name]
        if a.shape[1] < width:
            a = jnp.pad(a, ((0, 0), (0, width - a.shape[1])))
        cols.append(a)
    return jnp.concatenate(cols, axis=1).reshape(-1, 128)


def _unpack_small(packed, shapes):
    flat = packed.reshape(2, -1)
    out, at = {}, 0
    for name, width in SMALL:
        out[name] = flat[:, at:at + shapes[name]]
        at += width
    return out


def _dup_gain(g):
    return jnp.concatenate([g, g[NOPE:]])[None, :]


def _undup(g):
    return jnp.concatenate([g[..., :NOPE], g[..., NOPE:NOPE + ROPE] + g[..., NOPE + ROPE:]], axis=-1)


def kernel(x, c, positions, ada_w, ada_b, norm_g, w_in, q_lat_g, w_q_up, kv_lat_g, w_kv_up, q_norm_g, k_norm_g, glu_b, dw_w, dw_b, conv_ln_g, conv_ln_b, w_pw, b_pw, w_out, loss_target, m_ada_w, m_ada_b, m_norm_g, m_w_in, m_q_lat_g, m_w_q_up, m_kv_lat_g, m_w_kv_up, m_q_norm_g, m_k_norm_g, m_glu_b, m_dw_w, m_dw_b, m_conv_ln_g, m_conv_ln_b, m_w_pw, m_b_pw, m_w_out, v_ada_w, v_ada_b, v_norm_g, v_w_in, v_q_lat_g, v_w_q_up, v_kv_lat_g, v_w_kv_up, v_q_norm_g, v_k_norm_g, v_glu_b, v_dw_w, v_dw_b, v_conv_ln_g, v_conv_ln_b, v_w_pw, v_b_pw, v_w_out):
    nl = 2
    s, d = x.shape[1], x.shape[2]
    xi, yi, ci = lax.axis_index("x"), lax.axis_index("y"), lax.axis_index("c")
    shard = 2 * xi + yi
    me = 4 * xi + 2 * yi + ci
    cidx = jnp.reshape(ci, (1,)).astype(jnp.int32)
    jc = jnp.stack([shard, ci]).astype(jnp.int32)
    x0 = x.reshape(s, d)
    target = loss_target.reshape(s, d)

    c_all = _allgather8(c.reshape(8, d // 8), "gather_c").reshape(8, d)
    n_ada = ada_w.shape[2]
    ada_b_shard = lax.dynamic_slice_in_dim(ada_b, shard * n_ada, n_ada, axis=1)[:, None, :]
    mod_shard = _modulation(c_all, ada_w, ada_b_shard)
    mod_all = _allgather8(mod_shard.reshape(nl * 8, n_ada), "gather_mod")
    mod_rows = lax.dynamic_index_in_dim(mod_all.reshape(4, 2, nl, 8, n_ada)[:, 0], me, axis=2, keepdims=False)
    mod_me = jnp.transpose(mod_rows, (1, 0, 2)).reshape(nl, 3, 1, d)

    tr = lambda a: jnp.transpose(a, (0, 2, 1))
    w_in_t = tr(w_in).astype(BF16)
    wq = w_q_up.reshape(nl, QL, 2, QK)
    wq = jnp.concatenate([wq, wq[..., NOPE:]], axis=-1)
    wq = jnp.transpose(wq, (0, 2, 1, 3)).reshape(nl, 2 * QL, HP).astype(BF16)
    dw_pad = jnp.pad(dw_w, ((0, 0), (0, HALO - CONV_K), (0, 0)))
    local = [w_in_t, wq, w_kv_up.astype(BF16), dw_pad, w_pw.astype(BF16), w_out.astype(BF16)]

    def kernel_layouts(bufs):
        w_in_g, wq_g, wkv_g, dw_g, wpw_g, wout_g = bufs
        heads_side_by_side = lambda a, rows: jnp.transpose(a.reshape(1, -1, rows, a.shape[-1]), (0, 2, 1, 3)).reshape(1, rows, -1)
        return dict(w_in=w_in_g.reshape(1, 4 * w_in_g.shape[2], d), wq=heads_side_by_side(wq_g, QL), wkv=heads_side_by_side(wkv_g, KVL),
                    dw=jnp.transpose(dw_g, (0, 2, 1, 3)).reshape(1, HALO, 4 * dw_g.shape[3]),
                    wpw=wpw_g.reshape(1, 4 * wpw_g.shape[2], wpw_g.shape[3]), wout=wout_g.reshape(1, 4 * wout_g.shape[2], d))

    w_in_all = [_run_alone(_gather_hand_on(_run_alone(_gather_start([local[0][0:1]]), "gather_w_in0")), "gather_w_in0_hand_on"), None]
    others0_start = _gather_start([a[0:1] for a in local[1:]])
    mid1_start, w_out1_start = _gather_start([a[1:2] for a in local[1:5]]), _gather_start([local[5][1:2]])
    w_in1_start = _gather_start([local[0][1:2]])
    wts = [None] * nl

    cos, sin = _rope_tables(positions.reshape(s, 1))
    row = lambda a, l: a[l][None, :]

    saved = []
    xl = x0
    for l in range(nl):
        shift, scale, gate = mod_me[l, 0], mod_me[l, 1], mod_me[l, 2]
        in_args = (xl, row(norm_g, l), scale, shift, w_in_all[l][0].reshape(1, -1, d), 0, f"inproj_fwd{l}")
        if l == 0:
            (hb, za, zkr, zb), landed = _inproj_fwd(*in_args, carried=others0_start)
            others = _run_alone(_gather_hand_on(landed), "gather_others0_hand_on")
        else:
            (hb, za, zkr, zb), others = _inproj_fwd(*in_args, carried=_gather_hand_on(mid1_landed + w_out1_landed))
        w = wts[l] = kernel_layouts(w_in_all[l] + others)
        gains = (row(q_lat_g, l), row(kv_lat_g, l), _dup_gain(q_norm_g[l]), _dup_gain(k_norm_g[l]))
        prep_args = (za, zkr, cos, sin, w["wq"], w["wkv"], *gains, 0, f"mla_prep_fwd{l}")
        conv_args = (zb, row(glu_b, l), w["dw"], row(dw_b, l), row(conv_ln_g, l), row(conv_ln_b, l), w["wpw"], row(b_pw, l), 0)
        out_args = (gate, w["wout"], 0, f"outproj_fwd{l}")
        if l == 0:
            (q, k, v), mid1_landed = _mla_prep_fwd(*prep_args, carried=mid1_start)
            (o, lse), landed = _flash_fwd(q, k, v, f"flash_fwd{l}", carried=w_in1_start)
            (cv, pw), w_in_all[1] = _conv_fwd(*conv_args, f"conv_fwd{l}", carried=_gather_hand_on(landed))
            (xn, mixb), w_out1_landed = _outproj_fwd(xl, o, zb, pw, *out_args, carried=w_out1_start)
        else:
            q, k, v = _mla_prep_fwd(*prep_args)
            o, lse = _flash_fwd(q, k, v, f"flash_fwd{l}")
            cv, pw = _conv_fwd(*conv_args, f"conv_fwd{l}")
            tok_loss, dx, mixb = _outproj_fwd(xl, o, zb, pw, *out_args, target=target)
            xn = None
        saved.append(dict(x=xl, hb=hb, za=za, zkr=zkr, zb=zb, q=q, k=k, v=v, o=o, lse=lse, cv=cv, pw=pw, mixb=mixb, gains=gains))
        xl = xn

    loss = lax.psum(jnp.sum(tok_loss), ("x", "y", "c"))

    big = [None] * nl
    small = [None] * nl
    shards_of = lambda gs: [g.reshape(4, g.shape[0] // 4, g.shape[1]) for g in gs]
    pair_sums = lambda l, parts, theirs: [_pair_sum(p, t, cidx, f"pair_sum{l}_{e}") for e, (p, t) in enumerate(zip(parts, theirs))]
    chip_sums = lambda l, sums, landed: [_chip_sum(sm, ld, jc, f"chip_sum{l}_{e}") for e, (sm, ld) in enumerate(zip(sums, landed))]
    halves = [None] * nl
    for l in reversed(range(nl)):
        sv, w = saved[l], wts[l]
        shift, scale, gate = mod_me[l, 0], mod_me[l, 1], mod_me[l, 2]
        dxb, do, delta, dzb, dpw = _outproj_bwd(dx, gate, w["wout"], sv["o"], sv["zb"], sv["pw"], 0, f"outproj_bwd{l}")
        out_args = (sv["mixb"], dxb, gate, w["wout"], 0, f"grad_w_out{l}")
        attn_args = (sv["q"], sv["k"], sv["v"])
        if l == 0:
            parts = shards_of(big[1])
            (g_out, dgate), theirs = _grad_w_out(*out_args, carried=_pair_exchange(parts))
            sums = pair_sums(1, parts, theirs)
            (dq, dk, dv), landed = _flash_bwd(*attn_args, do, sv["lse"], delta, f"flash_bwd{l}", carried=_chip_scatter(sums))
            halves[1] = chip_sums(1, sums, landed)
        else:
            g_out, dgate = _grad_w_out(*out_args)
            dq, dk, dv = _flash_bwd(*attn_args, do, sv["lse"], delta, f"flash_bwd{l}")
        dza, dzkr, g_q, g_kv, dgql, dgkvl, dgq, dgk = _mla_prep_bwd(
            dq, dk, dv, sv["za"], sv["zkr"], cos, sin, w["wq"], w["wkv"], *sv["gains"], 0, f"mla_prep_bwd{l}")
        dcv, act, dbpw, dlng, dlnb = _pointwise_bwd(dpw, sv["cv"], row(conv_ln_g, l), row(conv_ln_b, l), w["wpw"], 0, f"pointwise_bwd{l}")
        g_pw, = _grad_tn(act, dpw, f"grad_w_pw{l}")
        by_shard = lambda g, n: jnp.transpose(g.reshape(g.shape[0], -1, n), (1, 0, 2)).reshape(-1, n)
        early = [by_shard(g_q, HP), by_shard(g_kv, 512), g_pw, g_out]
        conv_args = (dcv, sv["zb"], dzb, row(glu_b, l), w["dw"], 0, f"conv_bwd{l}")
        in_args = (sv["x"], dx, row(norm_g, l), scale, shift, w["w_in"], 0, f"inproj_bwd{l}")
        if l == 0:
            parts = shards_of(early)
            (dzb, g_dw, ddwb, dglub), theirs = _conv_bwd(*conv_args, carried=_pair_exchange(parts))
            sums = pair_sums("0e", parts, theirs)
            (g_in,), landed = _grad_w_in(dza, dzkr, dzb, sv["hb"], f"grad_w_in{l}", carried=_chip_scatter(sums))
            e_q, e_kv, e_pw, e_out = chip_sums("0e", sums, landed)
            parts = shards_of([g_in, g_dw.reshape(4 * HALO, CH)])
            sums = pair_sums("0l", parts, _run_alone(_pair_exchange(parts), "pair_exchange_late0"))
            (dx, dshift, dgg), landed = _inproj_bwd(dza, dzkr, dzb, *in_args, carried=_chip_scatter(sums))
            l_in, l_dw = chip_sums("0l", sums, landed)
            halves[0] = [l_in, e_q, e_kv, l_dw, e_pw, e_out]
        else:
            dzb, g_dw, ddwb, dglub = _conv_bwd(*conv_args)
            dx, dshift, dgg = _inproj_bwd(dza, dzkr, dzb, *in_args)
            g_in, = _grad_w_in(dza, dzkr, dzb, sv["hb"], f"grad_w_in{l}")
        big[l] = [g_in, early[0], early[1], g_dw.reshape(4 * HALO, CH), g_pw, g_out]
        small[l] = dict(ada_b=jnp.concatenate([dshift, dgg * row(norm_g, l), jnp.sum(dgate, axis=0)], axis=1), norm_g=dgg * (1.0 + scale),
                        q_lat_g=dgql, kv_lat_g=dgkvl, q_norm_g=_undup(dgq), k_norm_g=_undup(dgk), glu_b=dglub, dw_b=ddwb,
                        conv_ln_g=dlng, conv_ln_b=dlnb, b_pw=dbpw)
    grad_x = dx.reshape(x.shape)

    names = [n for n, _ in SMALL]
    mine = _pack_small({n: jnp.concatenate([small[0][n], small[1][n]], axis=0) for n in names})
    gathered = _allgather8(mine, "gather_small")
    weights = dict(ada_b=ada_b, norm_g=norm_g, q_lat_g=q_lat_g, kv_lat_g=kv_lat_g, q_norm_g=q_norm_g, k_norm_g=k_norm_g,
                   glu_b=glu_b, dw_b=dw_b, conv_ln_g=conv_ln_g, conv_ln_b=conv_ln_b, b_pw=b_pw)
    m_small = dict(ada_b=m_ada_b, norm_g=m_norm_g, q_lat_g=m_q_lat_g, kv_lat_g=m_kv_lat_g, q_norm_g=m_q_norm_g, k_norm_g=m_k_norm_g,
                   glu_b=m_glu_b, dw_b=m_dw_b, conv_ln_g=m_conv_ln_g, conv_ln_b=m_conv_ln_b, b_pw=m_b_pw)
    v_small = dict(ada_b=v_ada_b, norm_g=v_norm_g, q_lat_g=v_q_lat_g, kv_lat_g=v_kv_lat_g, q_norm_g=v_q_norm_g, k_norm_g=v_k_norm_g,
                   glu_b=v_glu_b, dw_b=v_dw_b, conv_ln_g=v_conv_ln_g, conv_ln_b=v_conv_ln_b, b_pw=v_b_pw)
    widths = {n: weights[n].shape[1] for n in names}
    v_packed = _pack_small({n: jnp.pad(v_small[n], ((0, 0), (0, dict(SMALL)[n] - widths[n])), constant_values=1.0) for n in names})
    small_out = [_unpack_small(a, widths) for a in _small_update(gathered, _pack_small(weights), _pack_small(m_small), v_packed)]

    ada_rows = gathered.reshape(8, nl, -1)[:, :, :3 * d]
    dmod = lax.dynamic_slice_in_dim(jnp.transpose(ada_rows, (1, 0, 2)), shard * n_ada, n_ada, axis=2)
    ada_out = _ada_update(c_all, dmod, ada_w, m_ada_w, v_ada_w)

    full = _run_alone(_pair_complete(halves[0] + halves[1]), "pair_complete")
    per_layer = [full[l * 6:(l + 1) * 6] for l in range(nl)]

    def natural_q(g):
        return jnp.transpose(_undup(g.reshape(2, QL, HP)), (1, 0, 2)).reshape(QL, 2 * QK)

    grads = [[per_layer[l][0], natural_q(per_layer[l][1]), per_layer[l][2], per_layer[l][3][:CONV_K], per_layer[l][4], per_layer[l][5]]
             for l in range(nl)]
    sharded = (("w_in", tr(w_in), tr(m_w_in), tr(v_w_in)), ("w_q_up", w_q_up, m_w_q_up, v_w_q_up),
               ("w_kv_up", w_kv_up, m_w_kv_up, v_w_kv_up), ("dw_w", dw_w, m_dw_w, v_dw_w),
               ("w_pw", w_pw, m_w_pw, v_w_pw), ("w_out", w_out, m_w_out, v_w_out))
    big_out = {name: _adam_update(w, grads[0][e], grads[1][e], m, v, f"adam_{name}") for e, (name, w, m, v) in enumerate(sharded)}
    big_out["w_in"] = [tr(a) for a in big_out["w_in"]]

    order = ["ada_w", "ada_b", "norm_g", "w_in", "q_lat_g", "w_q_up", "kv_lat_g", "w_kv_up", "q_norm_g", "k_norm_g", "glu_b",
             "dw_w", "dw_b", "conv_ln_g", "conv_ln_b", "w_pw", "b_pw", "w_out"]

    def leaf(kind, name):
        if name == "ada_w":
            return ada_out[kind]
        if name in big_out:
            return big_out[name][kind]
        return small_out[kind][name]

    return (loss, grad_x, *[leaf(kind, name) for kind in range(4) for name in order])
```

```python
import functools
import math

import jax
import jax.numpy as jnp
from jax import lax
from jax.experimental import pallas as pl
from jax.experimental.pallas import tpu as pltpu

F32, BF16 = jnp.float32, jnp.bfloat16
MESH = pl.DeviceIdType.MESH
ANY = pl.BlockSpec(memory_space=pl.ANY)

N_HEADS, NOPE, ROPE, VD = 8, 128, 64, 128
QK = NOPE + ROPE
QL, KVL = 512, 256
HP = 256
CONV_K, HALO = 31, 32
ROPE_THETA = 10000.0
EPS = 1e-6
ADAM_LR, ADAM_B1, ADAM_B2, ADAM_EPS, ADAM_WD, ADAM_STEP = 0.001, 0.9, 0.999, 1e-08, 0.01, 10
V7X_VMEM_LIMIT = 56 * 1024 * 1024

NT = (((1,), (1,)), ((), ()))
TN = (((0,), (0,)), ((), ()))
NN = (((1,), (0,)), ((), ()))


def _dot(a, b, dims=NN):
    return lax.dot_general(a, b, dims, preferred_element_type=F32)


def _params(*sem):
    return pltpu.CompilerParams(dimension_semantics=sem or None, vmem_limit_bytes=V7X_VMEM_LIMIT)


def _sigmoid(x):
    return 1.0 / (1.0 + jnp.exp(-x))


def _sum0(x):
    return jnp.sum(x, axis=0, keepdims=True)


def _sum1(x):
    return jnp.sum(x, axis=1, keepdims=True)


def _row(n):
    return pl.BlockSpec((1, n), lambda *_: (0, 0))


def _place():
    x, y, c = lax.axis_index("x"), lax.axis_index("y"), lax.axis_index("c")
    chips = [(1 - x, y), (x, 1 - y), (1 - x, 1 - y)]
    return x, y, c, chips


def _allgather8(v, name):
    r, n = v.shape

    def body(v_ref, out_ref, send_sems, recv_sems, local_sem):
        x, y, c, chips = _place()
        me, sibling = (x, y, c), (x, y, 1 - c)

        def slot(px, py, pc):
            return out_ref.at[4 * px + 2 * py + pc]

        def copy(k, block, to, src=None):
            return pltpu.make_async_remote_copy(
                src_ref=slot(*block) if src is None else src, dst_ref=slot(*block),
                send_sem=send_sems.at[k], recv_sem=recv_sems.at[k], device_id=to, device_id_type=MESH)

        mine = pltpu.make_async_copy(v_ref, slot(*me), local_sem)
        mine.start()
        first = [copy(0, me, sibling, src=v_ref)]
        first += [copy(1 + j, me, (*chip, c), src=v_ref) for j, chip in enumerate(chips)]
        for cp in first:
            cp.start()
        passed = [copy(4 + j, (*chip, c), sibling) for j, chip in enumerate(chips)]
        for j, chip in enumerate(chips):
            copy(1 + j, (*chip, c), me).wait_recv()
            passed[j].start()
        copy(0, sibling, me).wait_recv()
        for j, chip in enumerate(chips):
            copy(4 + j, (*chip, 1 - c), me).wait_recv()
        for cp in first + passed:
            cp.wait_send()
        mine.wait()

    return pl.pallas_call(
        body, name=name, out_shape=jax.ShapeDtypeStruct((8, r, n), v.dtype),
        in_specs=[pl.BlockSpec(memory_space=pltpu.VMEM)], out_specs=pl.BlockSpec(memory_space=pltpu.VMEM),
        scratch_shapes=[pltpu.SemaphoreType.DMA((7,)), pltpu.SemaphoreType.DMA((7,)), pltpu.SemaphoreType.DMA],
    )(v)


class _Carried:
    def __init__(self, operands, results, n_sems, start, finish, aliases=None):
        self.operands, self.results, self.n_sems = operands, results, n_sems
        self.start, self.finish, self.aliases = start, finish, aliases or {}


def _run_alone(carried, name):
    k = len(carried.operands)

    def body(*refs):
        args = (refs[:k], refs[k:k + len(carried.results)], refs[-2], refs[-1])
        carried.start(*args)
        carried.finish(*args)

    outs = pl.pallas_call(
        body, name=name, out_shape=carried.results, in_specs=[ANY] * k, out_specs=[ANY] * len(carried.results),
        input_output_aliases=carried.aliases,
        scratch_shapes=[pltpu.SemaphoreType.DMA((carried.n_sems,)), pltpu.SemaphoreType.DMA((carried.n_sems,))],
    )(*carried.operands)
    return list(outs)


def _call(body, operands, *, name, grid, in_specs, out_specs, out_shape, scratch=(), aliases=None, carried=None):
    params = _params(*(["arbitrary"] * len(grid)))
    n_in, n_out = len(in_specs), len(out_shape)
    if carried is None:
        return pl.pallas_call(body, name=name, grid=grid, in_specs=in_specs, out_specs=out_specs, out_shape=out_shape,
                              scratch_shapes=list(scratch), input_output_aliases=aliases or {}, compiler_params=params)(*operands)
    k_in, k_out = len(carried.operands), len(carried.results)

    def wrapped(*refs):
        ins, outs = refs[:n_in], refs[n_in + k_in:n_in + k_in + n_out]
        comm = (refs[n_in:n_in + k_in], refs[n_in + k_in + n_out:n_in + k_in + n_out + k_out], refs[-2], refs[-1])
        steps = [pl.program_id(a) for a in range(len(grid))]
        first = functools.reduce(jnp.logical_and, [s == 0 for s in steps])
        last = functools.reduce(jnp.logical_and, [s == g - 1 for s, g in zip(steps, grid)])

        @pl.when(first)
        def _():
            carried.start(*comm)

        body(*ins, *outs, *refs[n_in + k_in + n_out + k_out:-2])

        @pl.when(last)
        def _():
            carried.finish(*comm)

    both = dict(aliases or {})
    both.update({n_in + i: n_out + o for i, o in carried.aliases.items()})
    res = pl.pallas_call(
        wrapped, name=name, grid=grid, in_specs=list(in_specs) + [ANY] * k_in, out_specs=list(out_specs) + [ANY] * k_out,
        out_shape=list(out_shape) + list(carried.results), input_output_aliases=both, compiler_params=params,
        scratch_shapes=list(scratch) + [pltpu.SemaphoreType.DMA((carried.n_sems,)), pltpu.SemaphoreType.DMA((carried.n_sems,))],
    )(*operands, *carried.operands)
    return list(res[:n_out]), list(res[n_out:])


def _gather_start(shards):
    ne = len(shards)
    per = 4

    def copies(srcs, dsts, send_sems, recv_sems):
        x, y, c, chips = _place()
        jme = 2 * x + y
        out = []
        for e in range(ne):
            half = srcs[e].shape[2] // 2
            own = pl.ds(pl.multiple_of(c * half, 128), half)
            for k, chip in enumerate(chips):
                out.append(pltpu.make_async_remote_copy(
                    src_ref=srcs[e].at[:, :, own], dst_ref=dsts[e].at[:, jme, :, own], send_sem=send_sems.at[per * e + k],
                    recv_sem=recv_sems.at[per * e + k], device_id=(*chip, c), device_id_type=MESH))
            out.append(pltpu.make_async_remote_copy(
                src_ref=srcs[e], dst_ref=dsts[e].at[:, jme], send_sem=send_sems.at[per * e + 3],
                recv_sem=recv_sems.at[per * e + 3], device_id=(x, y, 1 - c), device_id_type=MESH))
        return out

    def start(*a):
        for cp in copies(*a):
            cp.start()

    def finish(*a):
        for cp in copies(*a):
            cp.wait()

    results = [jax.ShapeDtypeStruct((s.shape[0], 4) + s.shape[1:], s.dtype) for s in shards]
    return _Carried(list(shards), results, per * ne, start, finish)


def _gather_hand_on(bufs):
    ne = len(bufs)

    def copy(e, k, dsts, send_sems, recv_sems, mine):
        x, y, c, chips = _place()
        px, py = chips[k]
        half = dsts[e].shape[3] // 2
        cols = pl.ds(pl.multiple_of((c if mine else 1 - c) * half, 128), half)
        part = dsts[e].at[:, 2 * px + py, :, cols]
        return pltpu.make_async_remote_copy(src_ref=part, dst_ref=part, send_sem=send_sems.at[3 * e + k],
                                            recv_sem=recv_sems.at[3 * e + k], device_id=(x, y, 1 - c), device_id_type=MESH)

    def start(srcs, dsts, send_sems, recv_sems):
        for e in range(ne):
            for k in range(3):
                copy(e, k, dsts, send_sems, recv_sems, True).start()

    def finish(srcs, dsts, send_sems, recv_sems):
        for e in range(ne):
            for k in range(3):
                copy(e, k, dsts, send_sems, recv_sems, True).wait_send()
                copy(e, k, dsts, send_sems, recv_sems, False).wait_recv()

    results = [jax.ShapeDtypeStruct(b.shape, b.dtype) for b in bufs]
    return _Carried(list(bufs), results, 3 * ne, start, finish, aliases={e: e for e in range(ne)})


def _pair_exchange(parts):
    ne = len(parts)

    def copies(srcs, dsts, send_sems, recv_sems):
        x, y, c, _ = _place()
        out = []
        for e in range(ne):
            half = srcs[e].shape[2] // 2
            theirs = pl.ds(pl.multiple_of((1 - c) * half, 128), half)
            out.append(pltpu.make_async_remote_copy(
                src_ref=srcs[e].at[:, :, theirs], dst_ref=dsts[e], send_sem=send_sems.at[e],
                recv_sem=recv_sems.at[e], device_id=(x, y, 1 - c), device_id_type=MESH))
        return out

    def start(*a):
        for cp in copies(*a):
            cp.start()

    def finish(*a):
        for cp in copies(*a):
            cp.wait()

    results = [jax.ShapeDtypeStruct(p.shape[:2] + (p.shape[2] // 2,), p.dtype) for p in parts]
    return _Carried(list(parts), results, ne, start, finish)


def _chip_scatter(sums):
    ne = len(sums)

    def copies(srcs, dsts, send_sems, recv_sems):
        x, y, c, chips = _place()
        return [pltpu.make_async_remote_copy(
                    src_ref=srcs[e].at[2 * px + py], dst_ref=dsts[e].at[k], send_sem=send_sems.at[3 * e + k],
                    recv_sem=recv_sems.at[3 * e + k], device_id=(px, py, c), device_id_type=MESH)
                for e in range(ne) for k, (px, py) in enumerate(chips)]

    def start(*a):
        for cp in copies(*a):
            cp.start()

    def finish(*a):
        for cp in copies(*a):
            cp.wait()

    results = [jax.ShapeDtypeStruct((3,) + s.shape[1:], s.dtype) for s in sums]
    return _Carried(list(sums), results, 3 * ne, start, finish)


def _pair_complete(grads):
    ne = len(grads)

    def copy(e, dsts, send_sems, recv_sems, mine):
        x, y, c, _ = _place()
        half = dsts[e].shape[1] // 2
        cols = pl.ds(pl.multiple_of((c if mine else 1 - c) * half, 128), half)
        return pltpu.make_async_remote_copy(
            src_ref=dsts[e].at[:, cols], dst_ref=dsts[e].at[:, cols], send_sem=send_sems.at[e],
            recv_sem=recv_sems.at[e], device_id=(x, y, 1 - c), device_id_type=MESH)

    def start(srcs, dsts, send_sems, recv_sems):
        for e in range(ne):
            copy(e, dsts, send_sems, recv_sems, True).start()

    def finish(srcs, dsts, send_sems, recv_sems):
        for e in range(ne):
            copy(e, dsts, send_sems, recv_sems, True).wait_send()
            copy(e, dsts, send_sems, recv_sems, False).wait_recv()

    results = [jax.ShapeDtypeStruct(g.shape, g.dtype) for g in grads]
    return _Carried(list(grads), results, ne, start, finish, aliases={e: e for e in range(ne)})


def _pair_sums(parts, theirs, cidx, name):
    ne = len(parts)

    def body(c_ref, *refs):
        for e in range(ne):
            refs[2 * ne + e][...] = (refs[e][...].astype(F32) + refs[ne + e][...].astype(F32)).astype(BF16)

    halves = [(p.shape[1], p.shape[2] // 2) for p in parts]
    gs = pltpu.PrefetchScalarGridSpec(
        num_scalar_prefetch=1, grid=(4,),
        in_specs=[pl.BlockSpec((1, r, h), lambda j, c: (j, 0, c[0])) for r, h in halves]
                 + [pl.BlockSpec((1, r, h), lambda j, c: (j, 0, 0)) for r, h in halves],
        out_specs=[pl.BlockSpec((1, r, h), lambda j, c: (j, 0, 0)) for r, h in halves])
    return list(pl.pallas_call(body, name=name, grid_spec=gs, out_shape=[jax.ShapeDtypeStruct((4, r, h), BF16) for r, h in halves],
                               compiler_params=_params("arbitrary"))(cidx, *parts, *theirs))


def _chip_sums(sums, landed, jc, name):
    ne = len(sums)

    def body(jc_ref, *refs):
        for e in range(ne):
            acc = refs[e][0].astype(F32)
            for k in range(3):
                acc = acc + refs[ne + e][k].astype(F32)
            refs[2 * ne + e][...] = acc

    halves = [sm.shape[1:] for sm in sums]
    gs = pltpu.PrefetchScalarGridSpec(
        num_scalar_prefetch=1, grid=(1,),
        in_specs=[pl.BlockSpec((1, r, h), lambda i, jc: (jc[0], 0, 0)) for r, h in halves]
                 + [pl.BlockSpec((3, r, h), lambda i, jc: (0, 0, 0)) for r, h in halves],
        out_specs=[pl.BlockSpec((r, h), lambda i, jc: (0, jc[1])) for r, h in halves])
    return list(pl.pallas_call(body, name=name, grid_spec=gs, out_shape=[jax.ShapeDtypeStruct((r, 2 * h), F32) for r, h in halves],
                               compiler_params=_params("arbitrary"))(jc, *sums, *landed))


def _rope_tables(pos):
    s = pos.shape[0]
    lane = jnp.arange(128)
    inv = 1.0 / (ROPE_THETA ** ((2 * (lane % 32)).astype(F32) / ROPE))
    keep = (lane < 64).astype(F32)
    sign = jnp.where(lane < 32, -1.0, 1.0).astype(F32) * keep
    consts = jnp.stack([inv.astype(F32), keep, sign])[:, None, :]

    def body(p_ref, k_ref, c_ref, s_ref):
        ang = p_ref[...].astype(F32) * k_ref[0]
        c_ref[...] = jnp.cos(ang) * k_ref[1]
        s_ref[...] = jnp.sin(ang) * k_ref[2]

    tm = min(s, 1024)
    return pl.pallas_call(
        body, name="rope_tables", grid=(s // tm,),
        in_specs=[pl.BlockSpec((tm, 1), lambda i: (i, 0)), pl.BlockSpec((3, 1, 128), lambda i: (0, 0, 0))],
        out_specs=[pl.BlockSpec((tm, 128), lambda i: (i, 0))] * 2,
        out_shape=[jax.ShapeDtypeStruct((s, 128), F32)] * 2, compiler_params=_params("arbitrary"),
    )(pos, consts)


def _modulation(c_all, ada_w, ada_b_shard):
    nl, d, n = ada_w.shape
    tn = 512

    def body(c_ref, w_ref, b_ref, o_ref):
        cv = c_ref[...]
        act = (cv * _sigmoid(cv)).astype(BF16)
        o_ref[...] = _dot(act, w_ref[...].astype(BF16)) + b_ref[...]

    return pl.pallas_call(
        body, name="modulation", grid=(nl, n // tn),
        in_specs=[pl.BlockSpec((8, d), lambda l, j: (0, 0)), pl.BlockSpec((None, d, tn), lambda l, j: (l, 0, j)),
                  pl.BlockSpec((None, 1, tn), lambda l, j: (l, 0, j))],
        out_specs=pl.BlockSpec((None, 8, tn), lambda l, j: (l, 0, j)),
        out_shape=jax.ShapeDtypeStruct((nl, 8, n), F32), compiler_params=_params("arbitrary", "arbitrary"),
    )(c_all, ada_w, ada_b_shard)


W_ROWS = 4992
W_PIECES = ((0, 0, 832), (832, 768, 64), (896, 1856, 2048), (2944, 832, 1024), (3968, 3904, 1024))


def _load_w_in(w_hbm, w_vmem, sems):
    cps = [pltpu.make_async_copy(w_hbm.at[pl.ds(src, n)], w_vmem.at[pl.ds(dst, n)], sems.at[i])
           for i, (dst, src, n) in enumerate(W_PIECES)]
    for cp in cps:
        cp.start()
    for cp in cps:
        cp.wait()


def _inproj_fwd(x, g, scale, shift, w_int, layer, name, carried=None):
    s, d = x.shape
    tm = min(s, 256)

    def body(x_ref, g_ref, sc_ref, sh_ref, w_hbm, hb_ref, za_ref, zkr_ref, zb_ref, w_vmem, sems):
        @pl.when(pl.program_id(0) == 0)
        def _():
            _load_w_in(w_hbm.at[layer], w_vmem, sems)

        xv = x_ref[...]
        rstd = lax.rsqrt(jnp.mean(xv * xv, axis=1, keepdims=True) + EPS)
        h = (xv * rstd) * g_ref[...] * (1.0 + sc_ref[...]) + sh_ref[...]
        hb = h.astype(BF16)
        hb_ref[...] = hb
        za_ref[...] = _dot(hb, w_vmem[0:768], NT)
        zkr_ref[...] = _dot(hb, w_vmem[768:896], NT)
        zb_ref[...] = _dot(hb, w_vmem[896:W_ROWS], NT)

    return _call(
        body, (x, g, scale, shift, w_int), name=name, grid=(s // tm,), carried=carried,
        in_specs=[pl.BlockSpec((tm, d), lambda i: (i, 0)), _row(d), _row(d), _row(d), ANY],
        out_specs=[pl.BlockSpec((tm, d), lambda i: (i, 0)), pl.BlockSpec((tm, 768), lambda i: (i, 0)),
                   pl.BlockSpec((tm, 128), lambda i: (i, 0)), pl.BlockSpec((tm, 4096), lambda i: (i, 0))],
        out_shape=[jax.ShapeDtypeStruct((s, d), BF16), jax.ShapeDtypeStruct((s, 768), F32),
                   jax.ShapeDtypeStruct((s, 128), F32), jax.ShapeDtypeStruct((s, 4096), F32)],
        scratch=[pltpu.VMEM((W_ROWS, d), BF16), pltpu.SemaphoreType.DMA((len(W_PIECES),))])


def _rope(yv, cos, sin):
    return yv * cos + pltpu.roll(yv, 32, axis=1) * sin


def _mla_prep_fwd(za, zkr, cos, sin, wq, wkv, gql, gkvl, gq2, gk2, layer, name, carried=None):
    s = za.shape[0]
    tm = min(s, 256)

    def body(za_ref, zkr_ref, cos_ref, sin_ref, wq_ref, wkv_ref, gql_ref, gkvl_ref, gq_ref, gk_ref, q_ref, k_ref, v_ref):
        zq, zkv = za_ref[:, 0:QL], za_ref[:, QL:QL + KVL]
        qn = (zq * lax.rsqrt(jnp.mean(zq * zq, axis=1, keepdims=True) + EPS) * gql_ref[...]).astype(BF16)
        kvn = (zkv * lax.rsqrt(jnp.mean(zkv * zkv, axis=1, keepdims=True) + EPS) * gkvl_ref[...]).astype(BF16)
        kr = zkr_ref[...]
        kr_ss = 0.5 * _sum1(kr * kr)
        cos, sin = cos_ref[...], sin_ref[...]
        gq, gk = gq_ref[...] * SCORE_SCALE, gk_ref[...]
        qr_all, kvr_all = _dot(qn, wq_ref[...]), _dot(kvn, wkv_ref[...])
        for h in range(N_HEADS):
            qr = qr_all[:, h * HP:(h + 1) * HP]
            n, yv = qr[:, :NOPE], qr[:, NOPE:]
            rstd = lax.rsqrt((_sum1(n * n) + 0.5 * _sum1(yv * yv)) * (1.0 / QK) + EPS)
            q_ref[h, :, 0:NOPE] = (n * rstd * gq[:, :NOPE]).astype(BF16)
            q_ref[h, :, NOPE:HP] = _rope(yv * rstd * gq[:, NOPE:], cos, sin).astype(BF16)
            kvr = kvr_all[:, h * HP:(h + 1) * HP]
            kn, vv = kvr[:, :NOPE], kvr[:, NOPE:]
            rstd = lax.rsqrt((_sum1(kn * kn) + kr_ss) * (1.0 / QK) + EPS)
            k_ref[h, :, 0:NOPE] = (kn * rstd * gk[:, :NOPE]).astype(BF16)
            k_ref[h, :, NOPE:HP] = _rope(kr * rstd * gk[:, NOPE:], cos, sin).astype(BF16)
            v_ref[h] = vv.astype(BF16)

    tile = lambda n: pl.BlockSpec((tm, n), lambda i: (i, 0))
    return _call(
        body, (za, zkr, cos, sin, wq, wkv, gql, gkvl, gq2, gk2), name=name, grid=(s // tm,), carried=carried,
        in_specs=[tile(768), tile(128), tile(128), tile(128),
                  pl.BlockSpec((None, QL, N_HEADS * HP), lambda i: (layer, 0, 0)),
                  pl.BlockSpec((None, KVL, N_HEADS * HP), lambda i: (layer, 0, 0)),
                  _row(QL), _row(KVL), _row(HP), _row(HP)],
        out_specs=[pl.BlockSpec((N_HEADS, tm, HP), lambda i: (0, i, 0)), pl.BlockSpec((N_HEADS, tm, HP), lambda i: (0, i, 0)),
                   pl.BlockSpec((N_HEADS, tm, VD), lambda i: (0, i, 0))],
        out_shape=[jax.ShapeDtypeStruct((N_HEADS, s, HP), BF16), jax.ShapeDtypeStruct((N_HEADS, s, HP), BF16),
                   jax.ShapeDtypeStruct((N_HEADS, s, VD), BF16)])


SCORE_SCALE = 1.0 / math.sqrt(QK)
MASKED = -1e30


def _flash_fwd(q, k, v, name, carried=None):
    s = q.shape[1]
    t = min(s, 1024)

    def body(q_ref, k_ref, v_ref, o_ref, lse_ref):
        i = pl.program_id(1)
        qb = q_ref[...]
        row = lax.broadcasted_iota(jnp.int32, (t, t), 0)
        col = lax.broadcasted_iota(jnp.int32, (t, t), 1)

        def block(j):
            return pl.ds(pl.multiple_of(j * t, t), t)

        def scores(j):
            return _dot(qb, k_ref[block(j), :], NT)

        def update(j, sc, m, l, acc, diagonal):
            if diagonal:
                sc = jnp.where(col <= row, sc, MASKED)
            m_new = jnp.maximum(m, jnp.max(sc, axis=1, keepdims=True))
            p = jnp.exp(sc - m_new)
            alpha = jnp.exp(m - m_new)
            return m_new, alpha * l + _sum1(p), alpha * acc + _dot(p.astype(BF16), v_ref[block(j), :])

        init = (jnp.full((t, 1), MASKED, F32), jnp.zeros((t, 1), F32), jnp.zeros((t, VD), F32))
        carry = lax.fori_loop(0, i, lambda j, cr: update(j, scores(j), *cr, False), init)
        m, l, acc = update(i, scores(i), *carry, True)
        o_ref[...] = acc / l
        lse_ref[...] = m + jnp.log(l)

    return _call(
        body, (q, k, v), name=name, grid=(N_HEADS, s // t), carried=carried,
        in_specs=[pl.BlockSpec((None, t, HP), lambda h, i: (h, i, 0)), pl.BlockSpec((None, s, HP), lambda h, i: (h, 0, 0)),
                  pl.BlockSpec((None, s, VD), lambda h, i: (h, 0, 0))],
        out_specs=[pl.BlockSpec((t, VD), lambda h, i: (i, h)), pl.BlockSpec((None, t, 1), lambda h, i: (h, i, 0))],
        out_shape=[jax.ShapeDtypeStruct((s, N_HEADS * VD), F32), jax.ShapeDtypeStruct((N_HEADS, s, 1), F32)])


CH, RC = 256, 64


PH_ROWS_LESS = 8


def _glu(val, gate, bias):
    c = val.shape[1]
    return (val + bias[:, :c]) * _sigmoid(gate + bias[:, c:])


def _make_phases(buf, phases, cc):
    rows = buf.shape[0] - PH_ROWS_LESS
    for b in range(1, 8):
        phases[b - 1] = buf[pl.ds(b, rows), cc:cc + CH]


def _window(buf, phases, cc, shift, r0):
    a, b = divmod(shift, 8)
    if b == 0:
        return buf[r0 + 8 * a:r0 + 8 * a + RC, cc:cc + CH]
    return phases[b - 1, r0 + 8 * a:r0 + 8 * a + RC, :]


def _conv_fwd(zb, glu_b, dw, dwb, lng, lnb, wpw, bpw, layer, name, carried=None):
    s = zb.shape[0]
    dc = dwb.shape[1]
    tm = min(s, 256)
    hb = tm // HALO

    def body(val_ref, gate_ref, valh_ref, gateh_ref, glub_ref, dw_ref, dwb_ref, lng_ref, lnb_ref, wpw_ref, bpw_ref,
             cv_ref, pw_ref, ubuf, uph):
        i = pl.program_id(0)
        bias = glub_ref[...]
        ubuf[HALO:, :] = _glu(val_ref[...], gate_ref[...], bias)
        uh = _glu(valh_ref[...], gateh_ref[...], bias)
        ubuf[0:HALO, :] = jnp.where(i > 0, uh, 0.0)
        for cc in range(0, dc, CH):
            _make_phases(ubuf, uph, cc)
            for r0 in range(0, tm, RC):
                acc = jnp.zeros((RC, CH), F32)
                for j in range(CONV_K):
                    acc = acc + _window(ubuf, uph, cc, HALO - (CONV_K - 1) + j, r0) * dw_ref[j:j + 1, cc:cc + CH]
                cv_ref[r0:r0 + RC, cc:cc + CH] = acc + dwb_ref[:, cc:cc + CH]
        cv = cv_ref[...]
        dv = cv - jnp.mean(cv, axis=1, keepdims=True)
        yl = dv * lax.rsqrt(jnp.mean(dv * dv, axis=1, keepdims=True) + EPS) * lng_ref[...] + lnb_ref[...]
        act = (yl * _sigmoid(yl)).astype(BF16)
        pw_ref[...] = _dot(act, wpw_ref[...]) + bpw_ref[...]

    return _call(
        body, (zb, zb, zb, zb, glu_b, dw, dwb, lng, lnb, wpw, bpw), name=name, grid=(s // tm,), carried=carried,
        in_specs=[pl.BlockSpec((tm, dc), lambda i: (i, 0)), pl.BlockSpec((tm, dc), lambda i: (i, 1)),
                  pl.BlockSpec((HALO, dc), lambda i: (jnp.maximum(i * hb - 1, 0), 0)),
                  pl.BlockSpec((HALO, dc), lambda i: (jnp.maximum(i * hb - 1, 0), 1)),
                  _row(2 * dc), pl.BlockSpec((None, HALO, dc), lambda i: (layer, 0, 0)), _row(dc), _row(dc), _row(dc),
                  pl.BlockSpec((None, dc, dc), lambda i: (layer, 0, 0)), _row(dc)],
        out_specs=[pl.BlockSpec((tm, dc), lambda i: (i, 0))] * 2,
        out_shape=[jax.ShapeDtypeStruct((s, dc), F32)] * 2,
        scratch=[pltpu.VMEM((tm + HALO, dc), F32), pltpu.VMEM((7, tm + HALO - PH_ROWS_LESS, CH), F32)])


def _silu_parts(z):
    sg = _sigmoid(z)
    return z * sg, sg * (1.0 + z * (1.0 - sg))


def _outproj_fwd(x, o, zb, pw, gate, wout, layer, name, carried=None, target=None):
    s, d = x.shape
    dm = o.shape[1]
    tm = min(s, 256)

    def project(x_ref, o_ref, mg_ref, cg_ref, pw_ref, gate_ref, w_ref, mix_ref):
        mg, cg = mg_ref[...], cg_ref[...]
        mix_ref[:, 0:dm] = (o_ref[...] * (mg * _sigmoid(mg))).astype(BF16)
        mix_ref[:, dm:] = (pw_ref[...] * (cg * _sigmoid(cg))).astype(BF16)
        return x_ref[...] + gate_ref[...] * _dot(mix_ref[...], w_ref[...])

    def body(*refs):
        xn_ref, mix_ref = refs[7:]
        xn_ref[...] = project(*refs[:7], mix_ref)

    def body_with_loss(*refs):
        t_ref, l_ref, dx_ref, mix_ref = refs[7:]
        err = project(*refs[:7], mix_ref) - t_ref[...]
        l_ref[...] = 0.5 * jnp.mean(err * err, axis=1, keepdims=True)
        dx_ref[...] = err * (1.0 / d)

    tile = lambda n, j=0: pl.BlockSpec((tm, n), lambda i: (i, j))
    in_specs = [tile(d), tile(dm), tile(dm, 2), tile(dm, 3), tile(dm), _row(d), pl.BlockSpec((None, 2 * dm, d), lambda i: (layer, 0, 0))]
    if target is None:
        return _call(body, (x, o, zb, zb, pw, gate, wout), name=name, grid=(s // tm,), carried=carried, in_specs=in_specs,
                     out_specs=[tile(d), tile(2 * dm)],
                     out_shape=[jax.ShapeDtypeStruct((s, d), F32), jax.ShapeDtypeStruct((s, 2 * dm), BF16)])
    return _call(body_with_loss, (x, o, zb, zb, pw, gate, wout, target), name=name, grid=(s // tm,), carried=carried,
                 in_specs=in_specs + [tile(d)], out_specs=[tile(1), tile(d), tile(2 * dm)],
                 out_shape=[jax.ShapeDtypeStruct((s, 1), F32), jax.ShapeDtypeStruct((s, d), F32), jax.ShapeDtypeStruct((s, 2 * dm), BF16)])


def _grad_tn(a, b, name, carried=None):
    s, n = a.shape
    m = b.shape[1]
    tn, ts = min(n, 1024), min(s, 512)
    nt = s // ts

    def body(a_ref, b_ref, o_ref, acc):
        t = pl.program_id(1)

        @pl.when(t == 0)
        def _():
            acc[...] = jnp.zeros_like(acc)

        acc[...] += _dot(a_ref[...].astype(BF16), b_ref[...].astype(BF16), TN)

        @pl.when(t == nt - 1)
        def _():
            o_ref[...] = acc[...].astype(BF16)

    return _call(
        body, (a, b), name=name, grid=(n // tn, nt), carried=carried,
        in_specs=[pl.BlockSpec((ts, tn), lambda r, t: (t, r)), pl.BlockSpec((ts, m), lambda r, t: (t, 0))],
        out_specs=[pl.BlockSpec((tn, m), lambda r, t: (r, 0))], out_shape=[jax.ShapeDtypeStruct((n, m), BF16)],
        scratch=[pltpu.VMEM((tn, m), F32)])


def _grad_w_out(mixb, dxb, gate, wout, layer, name, carried=None):
    s, n = mixb.shape
    d = dxb.shape[1]
    tn, ts = min(n, 1024), min(s, 512)
    nt = s // ts

    def body(a_ref, b_ref, gate_ref, w_ref, g_ref, dgate_ref, acc):
        t = pl.program_id(1)

        @pl.when(t == 0)
        def _():
            acc[...] = jnp.zeros_like(acc)

        acc[...] += _dot(a_ref[...], b_ref[...], TN)

        @pl.when(t == nt - 1)
        def _():
            m = acc[...]
            dgate_ref[...] = _sum0(m * w_ref[...].astype(F32))
            g_ref[...] = (m * gate_ref[...]).astype(BF16)

    return _call(
        body, (mixb, dxb, gate, wout), name=name, grid=(n // tn, nt), carried=carried,
        in_specs=[pl.BlockSpec((ts, tn), lambda r, t: (t, r)), pl.BlockSpec((ts, d), lambda r, t: (t, 0)), _row(d),
                  pl.BlockSpec((None, tn, d), lambda r, t: (layer, r, 0))],
        out_specs=[pl.BlockSpec((tn, d), lambda r, t: (r, 0)), pl.BlockSpec((None, 1, d), lambda r, t: (r, 0, 0))],
        out_shape=[jax.ShapeDtypeStruct((n, d), BF16), jax.ShapeDtypeStruct((n // tn, 1, d), F32)],
        scratch=[pltpu.VMEM((tn, d), F32)])


def _outproj_bwd(dxo, gate, wout, o, zb, pw, layer, name, carried=None):
    s, d = dxo.shape
    dm = o.shape[1]
    tm = min(s, 256)

    def body(dx_ref, gate_ref, w_ref, o_ref, mg_ref, cg_ref, pw_ref, dxb_ref, do_ref, delta_ref, dzb_ref, dpw_ref):
        dx = dx_ref[...]
        dxb_ref[...] = dx.astype(BF16)
        dmix = _dot((dx * gate_ref[...]).astype(BF16), w_ref[...], NT)
        da, db = dmix[:, :dm], dmix[:, dm:]
        ov = o_ref[...]
        silu_m, dsilu_m = _silu_parts(mg_ref[...])
        do = da * silu_m
        do_ref[...] = do.astype(BF16)
        prod = do * ov
        for h in range(N_HEADS):
            delta_ref[h] = _sum1(prod[:, h * VD:(h + 1) * VD])
        dzb_ref[:, 0:dm] = da * ov * dsilu_m
        silu_c, dsilu_c = _silu_parts(cg_ref[...])
        dpw_ref[...] = db * silu_c
        dzb_ref[:, dm:] = db * pw_ref[...] * dsilu_c

    tile = lambda n, j=0: pl.BlockSpec((tm, n), lambda i: (i, j))
    return _call(
        body, (dxo, gate, wout, o, zb, zb, pw), name=name, grid=(s // tm,), carried=carried,
        in_specs=[tile(d), _row(d), pl.BlockSpec((None, 2 * dm, d), lambda i: (layer, 0, 0)),
                  tile(dm), tile(dm, 2), tile(dm, 3), tile(dm)],
        out_specs=[tile(d), tile(dm), pl.BlockSpec((N_HEADS, tm, 1), lambda i: (0, i, 0)), tile(2 * dm, 1), tile(dm)],
        out_shape=[jax.ShapeDtypeStruct((s, d), BF16), jax.ShapeDtypeStruct((s, dm), BF16),
                   jax.ShapeDtypeStruct((N_HEADS, s, 1), F32), jax.ShapeDtypeStruct((s, 4 * dm), F32),
                   jax.ShapeDtypeStruct((s, dm), F32)])


def _flash_bwd(q, k, v, do, lse, delta, name, carried=None):
    s = q.shape[1]
    t = min(s, 1024)
    nq = s // t

    def body(q_ref, k_ref, v_ref, do_ref, lse_ref, delta_ref, dq_ref, dk_ref, dv_ref):
        j = pl.program_id(1)

        @pl.when(j == 0)
        def _():
            dq_ref[...] = jnp.zeros_like(dq_ref)

        kb, vb = k_ref[...], v_ref[...]
        row = lax.broadcasted_iota(jnp.int32, (t, t), 0)
        col = lax.broadcasted_iota(jnp.int32, (t, t), 1)

        def block(i):
            return pl.ds(pl.multiple_of(i * t, t), t)

        def scores(i):
            at = block(i)
            return _dot(q_ref[at, :], kb, NT), _dot(do_ref[at, :], vb, NT)

        def update(i, sc, dp, dk, dv, diagonal):
            at = block(i)
            p = jnp.exp(sc - lse_ref[at, :])
            if diagonal:
                p = jnp.where(col <= row, p, 0.0)
            dv = dv + _dot(p.astype(BF16), do_ref[at, :], TN)
            ds = (p * (dp - delta_ref[at, :])).astype(BF16)
            dq_ref[at, :] += _dot(ds, kb)
            return dk + _dot(ds, q_ref[at, :], TN), dv

        carry = update(j, *scores(j), jnp.zeros((t, HP), F32), jnp.zeros((t, VD), F32), True)
        dk, dv = lax.fori_loop(j + 1, nq, lambda i, cr: update(i, *scores(i), *cr, False), carry)
        dk_ref[...] = dk
        dv_ref[...] = dv

    whole = lambda n: pl.BlockSpec((None, s, n), lambda h, j: (h, 0, 0))
    blk = lambda n: pl.BlockSpec((None, t, n), lambda h, j: (h, j, 0))
    return _call(
        body, (q, k, v, do, lse, delta), name=name, grid=(N_HEADS, nq), carried=carried,
        in_specs=[whole(HP), blk(HP), blk(VD), pl.BlockSpec((s, VD), lambda h, j: (0, h)), whole(1), whole(1)],
        out_specs=[whole(HP), blk(HP), blk(VD)],
        out_shape=[jax.ShapeDtypeStruct((N_HEADS, s, HP), F32), jax.ShapeDtypeStruct((N_HEADS, s, HP), F32),
                   jax.ShapeDtypeStruct((N_HEADS, s, VD), F32)])


def _mla_prep_bwd(dq, dk, dv, za, zkr, cos, sin, wq, wkv, gql, gkvl, gq2, gk2, layer, name):
    s = za.shape[0]
    tm = min(s, 512)

    def norm_bwd(n, yv, rstd, gain, d_n_out, d_y_out):
        tn_, ty = n * rstd, yv * rstd
        dgain_n, dgain_y = _sum0(d_n_out * tn_), _sum0(d_y_out * ty)
        dtn, dty = d_n_out * gain[:, :NOPE], d_y_out * gain[:, NOPE:]
        a = (_sum1(dtn * n) + _sum1(dty * yv)) * (rstd * rstd * rstd * (1.0 / QK))
        return rstd * dtn - n * a, rstd * dty - (0.5 * yv) * a, dgain_n, dgain_y

    def rope_bwd(d_out, cos, sin):
        return d_out * cos + pltpu.roll(d_out * sin, 128 - 32, axis=1)

    def latent_bwd(z, gain, dn):
        rstd = lax.rsqrt(jnp.mean(z * z, axis=1, keepdims=True) + EPS)
        zh = z * rstd
        dzh = dn * gain
        return rstd * (dzh - zh * jnp.mean(dzh * zh, axis=1, keepdims=True)), _sum0(dn * zh)

    def body(dq_ref, dk_ref, dv_ref, za_ref, zkr_ref, cos_ref, sin_ref, wq_ref, wkv_ref, gql_ref, gkvl_ref, gq_ref, gk_ref,
             dza_ref, dzkr_ref, gwq_ref, gwkv_ref, dgql_ref, dgkvl_ref, dgq_ref, dgk_ref, gwq_acc, gwkv_acc):
        @pl.when(pl.program_id(0) == 0)
        def _():
            for r in (gwq_acc, gwkv_acc, dgql_ref, dgkvl_ref, dgq_ref, dgk_ref):
                r[...] = jnp.zeros_like(r)

        zq, zkv = za_ref[:, 0:QL], za_ref[:, QL:QL + KVL]
        qf = zq * lax.rsqrt(jnp.mean(zq * zq, axis=1, keepdims=True) + EPS) * gql_ref[...]
        kvf = zkv * lax.rsqrt(jnp.mean(zkv * zkv, axis=1, keepdims=True) + EPS) * gkvl_ref[...]
        qn, kvn = qf.astype(BF16), kvf.astype(BF16)
        qn_t, kvn_t = qf.T.astype(BF16), kvf.T.astype(BF16)
        kr = zkr_ref[...]
        kr_ss = 0.5 * _sum1(kr * kr)
        cos, sin = cos_ref[...], sin_ref[...]
        gq, gk = gq_ref[...], gk_ref[...]
        dkr = jnp.zeros((tm, 128), F32)
        qr_all, kvr_all = _dot(qn, wq_ref[...]), _dot(kvn, wkv_ref[...])
        dqr_all, dkvr_all = [], []
        for h in range(N_HEADS):
            qr = qr_all[:, h * HP:(h + 1) * HP]
            n, yv = qr[:, :NOPE], qr[:, NOPE:]
            rstd = lax.rsqrt((_sum1(n * n) + 0.5 * _sum1(yv * yv)) * (1.0 / QK) + EPS)
            dqh = dq_ref[h] * SCORE_SCALE
            dn, dy, dg_n, dg_y = norm_bwd(n, yv, rstd, gq, dqh[:, :NOPE], rope_bwd(dqh[:, NOPE:], cos, sin))
            dgq_ref[:, 0:NOPE] += dg_n
            dgq_ref[:, NOPE:] += dg_y
            dqr_all += [dn.astype(BF16), dy.astype(BF16)]
            kn = kvr_all[:, h * HP:h * HP + NOPE]
            rstd = lax.rsqrt((_sum1(kn * kn) + kr_ss) * (1.0 / QK) + EPS)
            dkh = dk_ref[h]
            dn, dy, dg_n, dg_y = norm_bwd(kn, kr, rstd, gk, dkh[:, :NOPE], rope_bwd(dkh[:, NOPE:], cos, sin))
            dgk_ref[:, 0:NOPE] += dg_n
            dgk_ref[:, NOPE:] += dg_y
            dkr = dkr + dy
            dkvr_all += [dn.astype(BF16), dv_ref[h].astype(BF16)]

        dqr_all, dkvr_all = jnp.concatenate(dqr_all, axis=1), jnp.concatenate(dkvr_all, axis=1)
        gwq_acc[...] += _dot(qn_t, dqr_all)
        gwkv_acc[...] += _dot(kvn_t, dkvr_all)

        @pl.when(pl.program_id(0) == s // tm - 1)
        def _():
            gwq_ref[...] = gwq_acc[...].astype(BF16)
            gwkv_ref[...] = gwkv_acc[...].astype(BF16)

        dzq, dgql = latent_bwd(zq, gql_ref[...], _dot(dqr_all, wq_ref[...], NT))
        dzkv, dgkvl = latent_bwd(zkv, gkvl_ref[...], _dot(dkvr_all, wkv_ref[...], NT))
        dgql_ref[...] += dgql
        dgkvl_ref[...] += dgkvl
        dza_ref[:, 0:QL] = dzq
        dza_ref[:, QL:] = dzkv
        lane = lax.broadcasted_iota(jnp.int32, (tm, 128), 1)
        dzkr_ref[...] = jnp.where(lane < ROPE, dkr + pltpu.roll(dkr, 64, axis=1), 0.0)

    tile = lambda n: pl.BlockSpec((tm, n), lambda i: (i, 0))
    heads = lambda n: pl.BlockSpec((N_HEADS, tm, n), lambda i: (0, i, 0))
    return pl.pallas_call(
        body, name=name, grid=(s // tm,),
        in_specs=[heads(HP), heads(HP), heads(VD), tile(768), tile(128), tile(128), tile(128),
                  pl.BlockSpec((None, QL, N_HEADS * HP), lambda i: (layer, 0, 0)),
                  pl.BlockSpec((None, KVL, N_HEADS * HP), lambda i: (layer, 0, 0)),
                  _row(QL), _row(KVL), _row(HP), _row(HP)],
        out_specs=[tile(768), tile(128), pl.BlockSpec((QL, N_HEADS * HP), lambda i: (0, 0)),
                   pl.BlockSpec((KVL, N_HEADS * HP), lambda i: (0, 0)), _row(QL), _row(KVL), _row(HP), _row(HP)],
        out_shape=[jax.ShapeDtypeStruct((s, 768), F32), jax.ShapeDtypeStruct((s, 128), F32),
                   jax.ShapeDtypeStruct((QL, N_HEADS * HP), BF16), jax.ShapeDtypeStruct((KVL, N_HEADS * HP), BF16),
                   jax.ShapeDtypeStruct((1, QL), F32), jax.ShapeDtypeStruct((1, KVL), F32),
                   jax.ShapeDtypeStruct((1, HP), F32), jax.ShapeDtypeStruct((1, HP), F32)],
        scratch_shapes=[pltpu.VMEM((QL, N_HEADS * HP), F32), pltpu.VMEM((KVL, N_HEADS * HP), F32)],
        compiler_params=_params("arbitrary"),
    )(dq, dk, dv, za, zkr, cos, sin, wq, wkv, gql, gkvl, gq2, gk2)


def _pointwise_bwd(dpw, cv, lng, lnb, wpw, layer, name):
    s, dc = cv.shape
    tm = min(s, 256)

    def body(dpw_ref, cv_ref, lng_ref, lnb_ref, w_ref, dcv_ref, act_ref, dbpw_ref, dlng_ref, dlnb_ref):
        @pl.when(pl.program_id(0) == 0)
        def _():
            for r in (dbpw_ref, dlng_ref, dlnb_ref):
                r[...] = jnp.zeros_like(r)

        cv = cv_ref[...]
        dv = cv - jnp.mean(cv, axis=1, keepdims=True)
        rstd = lax.rsqrt(jnp.mean(dv * dv, axis=1, keepdims=True) + EPS)
        xh = dv * rstd
        yl = xh * lng_ref[...] + lnb_ref[...]
        silu, dsilu = _silu_parts(yl)
        act_ref[...] = silu.astype(BF16)
        dpw = dpw_ref[...]
        dbpw_ref[...] += _sum0(dpw)
        dyl = _dot(dpw.astype(BF16), w_ref[...], NT) * dsilu
        dlng_ref[...] += _sum0(dyl * xh)
        dlnb_ref[...] += _sum0(dyl)
        dxh = dyl * lng_ref[...]
        dcv_ref[...] = rstd * (dxh - jnp.mean(dxh, axis=1, keepdims=True) - xh * jnp.mean(dxh * xh, axis=1, keepdims=True))

    tile = pl.BlockSpec((tm, dc), lambda i: (i, 0))
    return pl.pallas_call(
        body, name=name, grid=(s // tm,),
        in_specs=[tile, tile, _row(dc), _row(dc), pl.BlockSpec((None, dc, dc), lambda i: (layer, 0, 0))],
        out_specs=[tile, tile, _row(dc), _row(dc), _row(dc)],
        out_shape=[jax.ShapeDtypeStruct((s, dc), F32), jax.ShapeDtypeStruct((s, dc), BF16)] + [jax.ShapeDtypeStruct((1, dc), F32)] * 3,
        compiler_params=_params("arbitrary"),
    )(dpw, cv, lng, lnb, wpw)


def _conv_bwd(dcv, zb, dzb, glu_b, dw, layer, name, carried=None):
    s, dc = dcv.shape
    tm = min(s, 256)
    hb = tm // HALO
    last = s // tm - 1

    def body(dcv_ref, dcvn_ref, val_ref, gate_ref, valh_ref, gateh_ref, glub_ref, dw_ref, _, dzb_ref, gdw_ref, ddwb_ref, dglub_ref,
             ubuf, dbuf, gacc, uph, dph):
        i = pl.program_id(0)

        @pl.when(i == 0)
        def _():
            gacc[...] = jnp.zeros_like(gacc)
            ddwb_ref[...] = jnp.zeros_like(ddwb_ref)
            dglub_ref[...] = jnp.zeros_like(dglub_ref)

        bias = glub_ref[...]
        ubuf[HALO:, :] = _glu(val_ref[...], gate_ref[...], bias)
        ubuf[0:HALO, :] = jnp.where(i > 0, _glu(valh_ref[...], gateh_ref[...], bias), 0.0)
        dcv = dcv_ref[...]
        dbuf[0:tm, :] = dcv
        dbuf[tm:, :] = jnp.where(i < last, dcvn_ref[...], 0.0)
        ddwb_ref[...] += _sum0(dcv)
        for cc in range(0, dc, CH):
            _make_phases(ubuf, uph, cc)
            _make_phases(dbuf, dph, cc)
            for r0 in range(0, tm, RC):
                du = jnp.zeros((RC, CH), F32)
                dpiece = dbuf[r0:r0 + RC, cc:cc + CH]
                for j in range(CONV_K):
                    du = du + _window(dbuf, dph, cc, (CONV_K - 1) - j, r0) * dw_ref[j:j + 1, cc:cc + CH]
                    win = _window(ubuf, uph, cc, HALO - (CONV_K - 1) + j, r0)
                    gacc[j, :, cc:cc + CH] += (dpiece * win).reshape(RC // 8, 8, CH).sum(axis=0)
                a = val_ref[r0:r0 + RC, cc:cc + CH] + bias[:, cc:cc + CH]
                sg = _sigmoid(gate_ref[r0:r0 + RC, cc:cc + CH] + bias[:, dc + cc:dc + cc + CH])
                dzb_ref[r0:r0 + RC, cc:cc + CH] = du * sg
                dzb_ref[r0:r0 + RC, dc + cc:dc + cc + CH] = du * a * sg * (1.0 - sg)
        dglub_ref[...] += _sum0(dzb_ref[...])

        @pl.when(i == last)
        def _():
            total = jnp.sum(gacc[...], axis=1)
            for cc in range(0, dc, CH):
                gdw_ref[cc // CH] = total[:, cc:cc + CH]

    return _call(
        body, (dcv, dcv, zb, zb, zb, zb, glu_b, dw, dzb), name=name, grid=(s // tm,), carried=carried, aliases={8: 0},
        in_specs=[pl.BlockSpec((tm, dc), lambda i: (i, 0)),
                  pl.BlockSpec((HALO, dc), lambda i: (jnp.minimum((i + 1) * hb, s // HALO - 1), 0)),
                  pl.BlockSpec((tm, dc), lambda i: (i, 0)), pl.BlockSpec((tm, dc), lambda i: (i, 1)),
                  pl.BlockSpec((HALO, dc), lambda i: (jnp.maximum(i * hb - 1, 0), 0)),
                  pl.BlockSpec((HALO, dc), lambda i: (jnp.maximum(i * hb - 1, 0), 1)),
                  _row(2 * dc), pl.BlockSpec((None, HALO, dc), lambda i: (layer, 0, 0)), ANY],
        out_specs=[pl.BlockSpec((tm, 2 * dc), lambda i: (i, 0)), pl.BlockSpec((4, HALO, CH), lambda i: (0, 0, 0)),
                   _row(dc), _row(2 * dc)],
        out_shape=[jax.ShapeDtypeStruct(dzb.shape, F32), jax.ShapeDtypeStruct((4, HALO, CH), F32),
                   jax.ShapeDtypeStruct((1, dc), F32), jax.ShapeDtypeStruct((1, 2 * dc), F32)],
        scratch=[pltpu.VMEM((tm + HALO, dc), F32), pltpu.VMEM((tm + HALO, dc), F32), pltpu.VMEM((HALO, 8, dc), F32),
                 pltpu.VMEM((7, tm + HALO - PH_ROWS_LESS, CH), F32), pltpu.VMEM((7, tm + HALO - PH_ROWS_LESS, CH), F32)])


def _inproj_bwd(dza, dzkr, dzb, x, dxo, g, scale, shift, w_int, layer, name, carried=None):
    s, d = x.shape
    tm = min(s, 128)

    def body(dza_ref, dzkr_ref, dzb_ref, x_ref, dxo_ref, g_ref, sc_ref, sh_ref, w_hbm, dx_ref, dsh_ref, dgg_ref, w_vmem, sems):
        @pl.when(pl.program_id(0) == 0)
        def _():
            _load_w_in(w_hbm.at[layer], w_vmem, sems)
            dsh_ref[...] = jnp.zeros_like(dsh_ref)
            dgg_ref[...] = jnp.zeros_like(dgg_ref)

        dh = _dot(dza_ref[...].astype(BF16), w_vmem[0:768])
        dh = dh + _dot(dzkr_ref[...].astype(BF16), w_vmem[768:896])
        dh = dh + _dot(dzb_ref[...].astype(BF16), w_vmem[896:W_ROWS])
        xv = x_ref[...]
        rstd = lax.rsqrt(jnp.mean(xv * xv, axis=1, keepdims=True) + EPS)
        xh = xv * rstd
        dsh_ref[...] += _sum0(dh)
        dgg_ref[...] += _sum0(dh * xh)
        dxh = dh * (g_ref[...] * (1.0 + sc_ref[...]))
        dx_ref[...] = dxo_ref[...] + rstd * (dxh - xh * jnp.mean(dxh * xh, axis=1, keepdims=True))

    tile = lambda n: pl.BlockSpec((tm, n), lambda i: (i, 0))
    return _call(
        body, (dza, dzkr, dzb, x, dxo, g, scale, shift, w_int), name=name, grid=(s // tm,), carried=carried,
        in_specs=[tile(768), tile(128), tile(4096), tile(d), tile(d), _row(d), _row(d), _row(d), ANY],
        out_specs=[tile(d), _row(d), _row(d)],
        out_shape=[jax.ShapeDtypeStruct((s, d), F32), jax.ShapeDtypeStruct((1, d), F32), jax.ShapeDtypeStruct((1, d), F32)],
        scratch=[pltpu.VMEM((W_ROWS, d), BF16), pltpu.SemaphoreType.DMA((len(W_PIECES),))])


def _grad_w_in(dza, dzkr, dzb, hb, name, carried=None):
    s, d = hb.shape
    ts = min(s, 512)
    nt = s // ts
    tiles = ((0, 768), (768, 64), (1856, 1024), (2880, 1024), (832, 1024), (3904, 1024))

    def body(a_ref, kr_ref, b_ref, h_ref, o_hbm, acc, rounded, sem):
        r, t = pl.program_id(0), pl.program_id(1)

        @pl.when(t == 0)
        def _():
            acc[...] = jnp.zeros_like(acc)

        hv = h_ref[...]

        @pl.when(r == 0)
        def _():
            acc[0:768, :] += _dot(a_ref[...].astype(BF16), hv, TN)

        @pl.when(r == 1)
        def _():
            acc[0:128, :] += _dot(kr_ref[...].astype(BF16), hv, TN)

        @pl.when(r >= 2)
        def _():
            acc[...] += _dot(b_ref[...].astype(BF16), hv, TN)

        for tile, (row0, rows) in enumerate(tiles):
            @pl.when((t == nt - 1) & (r == tile))
            def _():
                rounded[0:rows, :] = acc[0:rows, :].astype(BF16)
                cp = pltpu.make_async_copy(rounded.at[pl.ds(0, rows)], o_hbm.at[pl.ds(row0, rows)], sem)
                cp.start()
                cp.wait()

    return _call(
        body, (dza, dzkr, dzb, hb), name=name, grid=(len(tiles), nt), carried=carried,
        in_specs=[pl.BlockSpec((ts, 768), lambda r, t: (jnp.where(r == 0, t, nt - 1), 0)),
                  pl.BlockSpec((ts, 128), lambda r, t: (jnp.where(r == 1, t, jnp.where(r == 0, 0, nt - 1)), 0)),
                  pl.BlockSpec((ts, 1024), lambda r, t: (jnp.where(r >= 2, t, 0), jnp.maximum(r - 2, 0))),
                  pl.BlockSpec((ts, d), lambda r, t: (t, 0))],
        out_specs=[ANY], out_shape=[jax.ShapeDtypeStruct((4928, d), BF16)],
        scratch=[pltpu.VMEM((1024, d), F32), pltpu.VMEM((1024, d), BF16), pltpu.SemaphoreType.DMA])


def _adamw(w, g, m, v):
    m = ADAM_B1 * m + (1.0 - ADAM_B1) * g
    v = ADAM_B2 * v + (1.0 - ADAM_B2) * (g * g)
    m_hat = m / (1.0 - ADAM_B1 ** ADAM_STEP)
    v_hat = v / (1.0 - ADAM_B2 ** ADAM_STEP)
    return -ADAM_LR * (m_hat / (jnp.sqrt(v_hat) + ADAM_EPS) + ADAM_WD * w), m, v


def _adam_update(w, g0, g1, m, v, name):
    _, r, c = w.shape
    fits = [t for t in range(8, r + 1, 8) if r % t == 0 and t * c * 4 <= (1 << 21)]
    tr = max(fits) if fits else r

    def body(w_ref, g0_ref, g1_ref, m_ref, v_ref, g_ref, d_ref, mo_ref, vo_ref):
        g = jnp.where(pl.program_id(0) == 0, g0_ref[...], g1_ref[...])
        g_ref[...] = g
        d_ref[...], mo_ref[...], vo_ref[...] = _adamw(w_ref[...], g, m_ref[...], v_ref[...])

    big = pl.BlockSpec((None, tr, c), lambda l, i: (l, i, 0))
    one = pl.BlockSpec((tr, c), lambda l, i: (i, 0))
    return pl.pallas_call(
        body, name=name, grid=(2, r // tr), in_specs=[big, one, one, big, big], out_specs=[big] * 4,
        out_shape=[jax.ShapeDtypeStruct(w.shape, F32)] * 4, compiler_params=_params("arbitrary", "arbitrary"),
    )(w, g0, g1, m, v)


def _ada_update(c_all, dmod, w, m, v, carried=None):
    nl, d, n = w.shape
    tr = 256

    def body(c_ref, dm_ref, w_ref, m_ref, v_ref, g_ref, d_ref, mo_ref, vo_ref):
        cv = c_ref[...]
        act = (cv * _sigmoid(cv)).astype(BF16)
        g = _dot(act, dm_ref[...].astype(BF16), TN)
        g_ref[...] = g
        d_ref[...], mo_ref[...], vo_ref[...] = _adamw(w_ref[...], g, m_ref[...], v_ref[...])

    big = pl.BlockSpec((None, tr, n), lambda l, i: (l, i, 0))
    return _call(
        body, (c_all, dmod, w, m, v), name="ada_w_update", grid=(nl, d // tr), carried=carried,
        in_specs=[pl.BlockSpec((8, tr), lambda l, i: (0, i)), pl.BlockSpec((None, 8, n), lambda l, i: (l, 0, 0)), big, big, big],
        out_specs=[big] * 4, out_shape=[jax.ShapeDtypeStruct(w.shape, F32)] * 4)


def _small_update(gathered, w, m, v):
    r = w.shape[0]

    def body(ga_ref, w_ref, m_ref, v_ref, g_ref, d_ref, mo_ref, vo_ref):
        g = ga_ref[0]
        for dev in range(1, 8):
            g = g + ga_ref[dev]
        g_ref[...] = g
        d_ref[...], mo_ref[...], vo_ref[...] = _adamw(w_ref[...], g, m_ref[...], v_ref[...])

    return pl.pallas_call(body, name="small_update", out_shape=[jax.ShapeDtypeStruct((r, 128), F32)] * 4,
                          compiler_params=_params())(gathered, w, m, v)


SMALL = (("ada_b", 6144), ("norm_g", 2048), ("q_lat_g", 512), ("kv_lat_g", 256), ("q_norm_g", 256), ("k_norm_g", 256),
         ("glu_b", 2048), ("dw_b", 1024), ("conv_ln_g", 1024), ("conv_ln_b", 1024), ("b_pw", 1024))


def _pack_small(vals):
    cols = []
    for name, width in SMALL:
        a = vals[name]
        if a.shape[1] < width:
            a = jnp.pad(a, ((0, 0), (0, width - a.shape[1])))
        cols.append(a)
    return jnp.concatenate(cols, axis=1).reshape(-1, 128)


def _unpack_small(packed, shapes):
    flat = packed.reshape(2, -1)
    out, at = {}, 0
    for name, width in SMALL:
        out[name] = flat[:, at:at + shapes[name]]
        at += width
    return out


def _dup_gain(g):
    return jnp.concatenate([g, g[NOPE:]])[None, :]


def _undup(g):
    return jnp.concatenate([g[..., :NOPE], g[..., NOPE:NOPE + ROPE] + g[..., NOPE + ROPE:]], axis=-1)


def kernel(x, c, positions, ada_w, ada_b, norm_g, w_in, q_lat_g, w_q_up, kv_lat_g, w_kv_up, q_norm_g, k_norm_g, glu_b, dw_w, dw_b, conv_ln_g, conv_ln_b, w_pw, b_pw, w_out, loss_target, m_ada_w, m_ada_b, m_norm_g, m_w_in, m_q_lat_g, m_w_q_up, m_kv_lat_g, m_w_kv_up, m_q_norm_g, m_k_norm_g, m_glu_b, m_dw_w, m_dw_b, m_conv_ln_g, m_conv_ln_b, m_w_pw, m_b_pw, m_w_out, v_ada_w, v_ada_b, v_norm_g, v_w_in, v_q_lat_g, v_w_q_up, v_kv_lat_g, v_w_kv_up, v_q_norm_g, v_k_norm_g, v_glu_b, v_dw_w, v_dw_b, v_conv_ln_g, v_conv_ln_b, v_w_pw, v_b_pw, v_w_out):
    nl = 2
    s, d = x.shape[1], x.shape[2]
    xi, yi, ci = lax.axis_index("x"), lax.axis_index("y"), lax.axis_index("c")
    shard = 2 * xi + yi
    me = 4 * xi + 2 * yi + ci
    cidx = jnp.reshape(ci, (1,)).astype(jnp.int32)
    jc = jnp.stack([shard, ci]).astype(jnp.int32)
    x0 = x.reshape(s, d)
    target = loss_target.reshape(s, d)

    c_all = _allgather8(c.reshape(8, d // 8), "gather_c").reshape(8, d)
    n_ada = ada_w.shape[2]
    ada_b_shard = lax.dynamic_slice_in_dim(ada_b, shard * n_ada, n_ada, axis=1)[:, None, :]
    mod_shard = _modulation(c_all, ada_w, ada_b_shard)
    mod_all = _allgather8(mod_shard.reshape(nl * 8, n_ada), "gather_mod")
    mod_rows = lax.dynamic_index_in_dim(mod_all.reshape(4, 2, nl, 8, n_ada)[:, 0], me, axis=2, keepdims=False)
    mod_me = jnp.transpose(mod_rows, (1, 0, 2)).reshape(nl, 3, 1, d)

    tr = lambda a: jnp.transpose(a, (0, 2, 1))
    w_in_t = tr(w_in).astype(BF16)
    wq = w_q_up.reshape(nl, QL, 2, QK)
    wq = jnp.concatenate([wq, wq[..., NOPE:]], axis=-1)
    wq = jnp.transpose(wq, (0, 2, 1, 3)).reshape(nl, 2 * QL, HP).astype(BF16)
    dw_pad = jnp.pad(dw_w, ((0, 0), (0, HALO - CONV_K), (0, 0)))
    local = [w_in_t, wq, w_kv_up.astype(BF16), dw_pad, w_pw.astype(BF16), w_out.astype(BF16)]

    def kernel_layouts(bufs):
        w_in_g, wq_g, wkv_g, dw_g, wpw_g, wout_g = bufs
        heads_side_by_side = lambda a, rows: jnp.transpose(a.reshape(1, -1, rows, a.shape[-1]), (0, 2, 1, 3)).reshape(1, rows, -1)
        return dict(w_in=w_in_g.reshape(1, 4 * w_in_g.shape[2], d), wq=heads_side_by_side(wq_g, QL), wkv=heads_side_by_side(wkv_g, KVL),
                    dw=jnp.transpose(dw_g, (0, 2, 1, 3)).reshape(1, HALO, 4 * dw_g.shape[3]),
                    wpw=wpw_g.reshape(1, 4 * wpw_g.shape[2], wpw_g.shape[3]), wout=wout_g.reshape(1, 4 * wout_g.shape[2], d))

    w_in_all = [_run_alone(_gather_hand_on(_run_alone(_gather_start([local[0][0:1]]), "gather_w_in0")), "gather_w_in0_hand_on"), None]
    others0_start = _gather_start([a[0:1] for a in local[1:]])
    mid1_start, w_out1_start = _gather_start([a[1:2] for a in local[1:5]]), _gather_start([local[5][1:2]])
    w_in1_start = _gather_start([local[0][1:2]])
    wts = [None] * nl

    cos, sin = _rope_tables(positions.reshape(s, 1))
    row = lambda a, l: a[l][None, :]

    saved = []
    xl = x0
    for l in range(nl):
        shift, scale, gate = mod_me[l, 0], mod_me[l, 1], mod_me[l, 2]
        in_args = (xl, row(norm_g, l), scale, shift, w_in_all[l][0].reshape(1, -1, d), 0, f"inproj_fwd{l}")
        if l == 0:
            (hb, za, zkr, zb), landed = _inproj_fwd(*in_args, carried=others0_start)
            others = _run_alone(_gather_hand_on(landed), "gather_others0_hand_on")
        else:
            (hb, za, zkr, zb), others = _inproj_fwd(*in_args, carried=_gather_hand_on(mid1_landed + w_out1_landed))
        w = wts[l] = kernel_layouts(w_in_all[l] + others)
        gains = (row(q_lat_g, l), row(kv_lat_g, l), _dup_gain(q_norm_g[l]), _dup_gain(k_norm_g[l]))
        prep_args = (za, zkr, cos, sin, w["wq"], w["wkv"], *gains, 0, f"mla_prep_fwd{l}")
        conv_args = (zb, row(glu_b, l), w["dw"], row(dw_b, l), row(conv_ln_g, l), row(conv_ln_b, l), w["wpw"], row(b_pw, l), 0)
        out_args = (gate, w["wout"], 0, f"outproj_fwd{l}")
        if l == 0:
            (q, k, v), mid1_landed = _mla_prep_fwd(*prep_args, carried=mid1_start)
            (o, lse), landed = _flash_fwd(q, k, v, f"flash_fwd{l}", carried=w_in1_start)
            (cv, pw), w_in_all[1] = _conv_fwd(*conv_args, f"conv_fwd{l}", carried=_gather_hand_on(landed))
            (xn, mixb), w_out1_landed = _outproj_fwd(xl, o, zb, pw, *out_args, carried=w_out1_start)
        else:
            q, k, v = _mla_prep_fwd(*prep_args)
            o, lse = _flash_fwd(q, k, v, f"flash_fwd{l}")
            cv, pw = _conv_fwd(*conv_args, f"conv_fwd{l}")
            tok_loss, dx, mixb = _outproj_fwd(xl, o, zb, pw, *out_args, target=target)
            xn = None
        saved.append(dict(x=xl, hb=hb, za=za, zkr=zkr, zb=zb, q=q, k=k, v=v, o=o, lse=lse, cv=cv, pw=pw, mixb=mixb, gains=gains))
        xl = xn

    loss = lax.psum(jnp.sum(tok_loss), ("x", "y", "c"))

    big = [None] * nl
    small = [None] * nl
    shards_of = lambda gs: [g.reshape(4, g.shape[0] // 4, g.shape[1]) for g in gs]
    pair_sums = lambda l, parts, theirs: _pair_sums(parts, theirs, cidx, f"pair_sums{l}")
    chip_sums = lambda l, sums, landed: _chip_sums(sums, landed, jc, f"chip_sums{l}")
    halves = [None] * nl
    for l in reversed(range(nl)):
        sv, w = saved[l], wts[l]
        shift, scale, gate = mod_me[l, 0], mod_me[l, 1], mod_me[l, 2]
        dxb, do, delta, dzb, dpw = _outproj_bwd(dx, gate, w["wout"], sv["o"], sv["zb"], sv["pw"], 0, f"outproj_bwd{l}")
        out_args = (sv["mixb"], dxb, gate, w["wout"], 0, f"grad_w_out{l}")
        attn_args = (sv["q"], sv["k"], sv["v"])
        if l == 0:
            parts = shards_of(big[1])
            (g_out, dgate), theirs = _grad_w_out(*out_args, carried=_pair_exchange(parts))
            sums = pair_sums(1, parts, theirs)
            (dq, dk, dv), landed = _flash_bwd(*attn_args, do, sv["lse"], delta, f"flash_bwd{l}", carried=_chip_scatter(sums))
            halves[1] = chip_sums(1, sums, landed)
        else:
            g_out, dgate = _grad_w_out(*out_args)
            dq, dk, dv = _flash_bwd(*attn_args, do, sv["lse"], delta, f"flash_bwd{l}")
        dza, dzkr, g_q, g_kv, dgql, dgkvl, dgq, dgk = _mla_prep_bwd(
            dq, dk, dv, sv["za"], sv["zkr"], cos, sin, w["wq"], w["wkv"], *sv["gains"], 0, f"mla_prep_bwd{l}")
        dcv, act, dbpw, dlng, dlnb = _pointwise_bwd(dpw, sv["cv"], row(conv_ln_g, l), row(conv_ln_b, l), w["wpw"], 0, f"pointwise_bwd{l}")
        g_pw, = _grad_tn(act, dpw, f"grad_w_pw{l}")
        by_shard = lambda g, n: jnp.transpose(g.reshape(g.shape[0], -1, n), (1, 0, 2)).reshape(-1, n)
        early = [by_shard(g_q, HP), by_shard(g_kv, 512), g_pw, g_out]
        conv_args = (dcv, sv["zb"], dzb, row(glu_b, l), w["dw"], 0, f"conv_bwd{l}")
        in_args = (sv["x"], dx, row(norm_g, l), scale, shift, w["w_in"], 0, f"inproj_bwd{l}")
        if l == 0:
            parts = shards_of(early)
            (dzb, g_dw, ddwb, dglub), theirs = _conv_bwd(*conv_args, carried=_pair_exchange(parts))
            sums = pair_sums("0e", parts, theirs)
            (g_in,), landed = _grad_w_in(dza, dzkr, dzb, sv["hb"], f"grad_w_in{l}", carried=_chip_scatter(sums))
            e_q, e_kv, e_pw, e_out = chip_sums("0e", sums, landed)
            parts = shards_of([g_in, g_dw.reshape(4 * HALO, CH)])
            sums = pair_sums("0l", parts, _run_alone(_pair_exchange(parts), "pair_exchange_late0"))
            (dx, dshift, dgg), landed = _inproj_bwd(dza, dzkr, dzb, *in_args, carried=_chip_scatter(sums))
            l_in, l_dw = chip_sums("0l", sums, landed)
            halves[0] = [l_in, e_q, e_kv, l_dw, e_pw, e_out]
        else:
            dzb, g_dw, ddwb, dglub = _conv_bwd(*conv_args)
            dx, dshift, dgg = _inproj_bwd(dza, dzkr, dzb, *in_args)
            g_in, = _grad_w_in(dza, dzkr, dzb, sv["hb"], f"grad_w_in{l}")
        big[l] = [g_in, early[0], early[1], g_dw.reshape(4 * HALO, CH), g_pw, g_out]
        small[l] = dict(ada_b=jnp.concatenate([dshift, dgg * row(norm_g, l), jnp.sum(dgate, axis=0)], axis=1), norm_g=dgg * (1.0 + scale),
                        q_lat_g=dgql, kv_lat_g=dgkvl, q_norm_g=_undup(dgq), k_norm_g=_undup(dgk), glu_b=dglub, dw_b=ddwb,
                        conv_ln_g=dlng, conv_ln_b=dlnb, b_pw=dbpw)
    grad_x = dx.reshape(x.shape)

    names = [n for n, _ in SMALL]
    mine = _pack_small({n: jnp.concatenate([small[0][n], small[1][n]], axis=0) for n in names})
    gathered = _allgather8(mine, "gather_small")
    weights = dict(ada_b=ada_b, norm_g=norm_g, q_lat_g=q_lat_g, kv_lat_g=kv_lat_g, q_norm_g=q_norm_g, k_norm_g=k_norm_g,
                   glu_b=glu_b, dw_b=dw_b, conv_ln_g=conv_ln_g, conv_ln_b=conv_ln_b, b_pw=b_pw)
    m_small = dict(ada_b=m_ada_b, norm_g=m_norm_g, q_lat_g=m_q_lat_g, kv_lat_g=m_kv_lat_g, q_norm_g=m_q_norm_g, k_norm_g=m_k_norm_g,
                   glu_b=m_glu_b, dw_b=m_dw_b, conv_ln_g=m_conv_ln_g, conv_ln_b=m_conv_ln_b, b_pw=m_b_pw)
    v_small = dict(ada_b=v_ada_b, norm_g=v_norm_g, q_lat_g=v_q_lat_g, kv_lat_g=v_kv_lat_g, q_norm_g=v_q_norm_g, k_norm_g=v_k_norm_g,
                   glu_b=v_glu_b, dw_b=v_dw_b, conv_ln_g=v_conv_ln_g, conv_ln_b=v_conv_ln_b, b_pw=v_b_pw)
    widths = {n: weights[n].shape[1] for n in names}
    v_packed = _pack_small({n: jnp.pad(v_small[n], ((0, 0), (0, dict(SMALL)[n] - widths[n])), constant_values=1.0) for n in names})
    small_out = [_unpack_small(a, widths) for a in _small_update(gathered, _pack_small(weights), _pack_small(m_small), v_packed)]

    ada_rows = gathered.reshape(8, nl, -1)[:, :, :3 * d]
    dmod = lax.dynamic_slice_in_dim(jnp.transpose(ada_rows, (1, 0, 2)), shard * n_ada, n_ada, axis=2)
    ada_out = _ada_update(c_all, dmod, ada_w, m_ada_w, v_ada_w)

    full = _run_alone(_pair_complete(halves[0] + halves[1]), "pair_complete")
    per_layer = [full[l * 6:(l + 1) * 6] for l in range(nl)]

    def natural_q(g):
        return jnp.transpose(_undup(g.reshape(2, QL, HP)), (1, 0, 2)).reshape(QL, 2 * QK)

    grads = [[per_layer[l][0], natural_q(per_layer[l][1]), per_layer[l][2], per_layer[l][3][:CONV_K], per_layer[l][4], per_layer[l][5]]
             for l in range(nl)]
    sharded = (("w_in", tr(w_in), tr(m_w_in), tr(v_w_in)), ("w_q_up", w_q_up, m_w_q_up, v_w_q_up),
               ("w_kv_up", w_kv_up, m_w_kv_up, v_w_kv_up), ("dw_w", dw_w, m_dw_w, v_dw_w),
               ("w_pw", w_pw, m_w_pw, v_w_pw), ("w_out", w_out, m_w_out, v_w_out))
    big_out = {name: _adam_update(w, grads[0][e], grads[1][e], m, v, f"adam_{name}") for e, (name, w, m, v) in enumerate(sharded)}
    big_out["w_in"] = [tr(a) for a in big_out["w_in"]]

    order = ["ada_w", "ada_b", "norm_g", "w_in", "q_lat_g", "w_q_up", "kv_lat_g", "w_kv_up", "q_norm_g", "k_norm_g", "glu_b",
             "dw_w", "dw_b", "conv_ln_g", "conv_ln_b", "w_pw", "b_pw", "w_out"]

    def leaf(kind, name):
        if name == "ada_w":
            return ada_out[kind]
        if name in big_out:
            return big_out[name][kind]
        return small_out[kind][name]

    return (loss, grad_x, *[leaf(kind, name) for kind in range(4) for name in order])
```

```python
import functools
import math

import jax
import jax.numpy as jnp
from jax import lax
from jax.experimental import pallas as pl
from jax.experimental.pallas import tpu as pltpu

F32, BF16 = jnp.float32, jnp.bfloat16
MESH = pl.DeviceIdType.MESH
ANY = pl.BlockSpec(memory_space=pl.ANY)

N_HEADS, NOPE, ROPE, VD = 8, 128, 64, 128
QK = NOPE + ROPE
QL, KVL = 512, 256
HP = 256
CONV_K, HALO = 31, 32
ROPE_THETA = 10000.0
EPS = 1e-6
ADAM_LR, ADAM_B1, ADAM_B2, ADAM_EPS, ADAM_WD, ADAM_STEP = 0.001, 0.9, 0.999, 1e-08, 0.01, 10
V7X_VMEM_LIMIT = 56 * 1024 * 1024

NT = (((1,), (1,)), ((), ()))
TN = (((0,), (0,)), ((), ()))
NN = (((1,), (0,)), ((), ()))


def _dot(a, b, dims=NN):
    return lax.dot_general(a, b, dims, preferred_element_type=F32)


def _params(*sem):
    return pltpu.CompilerParams(dimension_semantics=sem or None, vmem_limit_bytes=V7X_VMEM_LIMIT)


def _sigmoid(x):
    return 1.0 / (1.0 + jnp.exp(-x))


def _sum0(x):
    return jnp.sum(x, axis=0, keepdims=True)


def _sum1(x):
    return jnp.sum(x, axis=1, keepdims=True)


def _row(n):
    return pl.BlockSpec((1, n), lambda *_: (0, 0))


def _place():
    x, y, c = lax.axis_index("x"), lax.axis_index("y"), lax.axis_index("c")
    chips = [(1 - x, y), (x, 1 - y), (1 - x, 1 - y)]
    return x, y, c, chips


def _allgather8(v, name):
    r, n = v.shape

    def body(v_ref, out_ref, send_sems, recv_sems, local_sem):
        x, y, c, chips = _place()
        me, sibling = (x, y, c), (x, y, 1 - c)

        def slot(px, py, pc):
            return out_ref.at[4 * px + 2 * py + pc]

        def copy(k, block, to, src=None):
            return pltpu.make_async_remote_copy(
                src_ref=slot(*block) if src is None else src, dst_ref=slot(*block),
                send_sem=send_sems.at[k], recv_sem=recv_sems.at[k], device_id=to, device_id_type=MESH)

        mine = pltpu.make_async_copy(v_ref, slot(*me), local_sem)
        mine.start()
        first = [copy(0, me, sibling, src=v_ref)]
        first += [copy(1 + j, me, (*chip, c), src=v_ref) for j, chip in enumerate(chips)]
        for cp in first:
            cp.start()
        passed = [copy(4 + j, (*chip, c), sibling) for j, chip in enumerate(chips)]
        for j, chip in enumerate(chips):
            copy(1 + j, (*chip, c), me).wait_recv()
            passed[j].start()
        copy(0, sibling, me).wait_recv()
        for j, chip in enumerate(chips):
            copy(4 + j, (*chip, 1 - c), me).wait_recv()
        for cp in first + passed:
            cp.wait_send()
        mine.wait()

    return pl.pallas_call(
        body, name=name, out_shape=jax.ShapeDtypeStruct((8, r, n), v.dtype),
        in_specs=[pl.BlockSpec(memory_space=pltpu.VMEM)], out_specs=pl.BlockSpec(memory_space=pltpu.VMEM),
        scratch_shapes=[pltpu.SemaphoreType.DMA((7,)), pltpu.SemaphoreType.DMA((7,)), pltpu.SemaphoreType.DMA],
    )(v)


class _Carried:
    def __init__(self, operands, results, n_sems, start, finish, aliases=None):
        self.operands, self.results, self.n_sems = operands, results, n_sems
        self.start, self.finish, self.aliases = start, finish, aliases or {}


def _run_alone(carried, name):
    k = len(carried.operands)

    def body(*refs):
        args = (refs[:k], refs[k:k + len(carried.results)], refs[-2], refs[-1])
        carried.start(*args)
        carried.finish(*args)

    outs = pl.pallas_call(
        body, name=name, out_shape=carried.results, in_specs=[ANY] * k, out_specs=[ANY] * len(carried.results),
        input_output_aliases=carried.aliases,
        scratch_shapes=[pltpu.SemaphoreType.DMA((carried.n_sems,)), pltpu.SemaphoreType.DMA((carried.n_sems,))],
    )(*carried.operands)
    return list(outs)


def _call(body, operands, *, name, grid, in_specs, out_specs, out_shape, scratch=(), aliases=None, carried=None):
    params = _params(*(["arbitrary"] * len(grid)))
    n_in, n_out = len(in_specs), len(out_shape)
    if carried is None:
        return pl.pallas_call(body, name=name, grid=grid, in_specs=in_specs, out_specs=out_specs, out_shape=out_shape,
                              scratch_shapes=list(scratch), input_output_aliases=aliases or {}, compiler_params=params)(*operands)
    k_in, k_out = len(carried.operands), len(carried.results)

    def wrapped(*refs):
        ins, outs = refs[:n_in], refs[n_in + k_in:n_in + k_in + n_out]
        comm = (refs[n_in:n_in + k_in], refs[n_in + k_in + n_out:n_in + k_in + n_out + k_out], refs[-2], refs[-1])
        steps = [pl.program_id(a) for a in range(len(grid))]
        first = functools.reduce(jnp.logical_and, [s == 0 for s in steps])
        last = functools.reduce(jnp.logical_and, [s == g - 1 for s, g in zip(steps, grid)])

        @pl.when(first)
        def _():
            carried.start(*comm)

        body(*ins, *outs, *refs[n_in + k_in + n_out + k_out:-2])

        @pl.when(last)
        def _():
            carried.finish(*comm)

    both = dict(aliases or {})
    both.update({n_in + i: n_out + o for i, o in carried.aliases.items()})
    res = pl.pallas_call(
        wrapped, name=name, grid=grid, in_specs=list(in_specs) + [ANY] * k_in, out_specs=list(out_specs) + [ANY] * k_out,
        out_shape=list(out_shape) + list(carried.results), input_output_aliases=both, compiler_params=params,
        scratch_shapes=list(scratch) + [pltpu.SemaphoreType.DMA((carried.n_sems,)), pltpu.SemaphoreType.DMA((carried.n_sems,))],
    )(*operands, *carried.operands)
    return list(res[:n_out]), list(res[n_out:])


def _gather_start(shards):
    ne = len(shards)
    per = 4

    def copies(srcs, dsts, send_sems, recv_sems):
        x, y, c, chips = _place()
        jme = 2 * x + y
        out = []
        for e in range(ne):
            half = srcs[e].shape[2] // 2
            own = pl.ds(pl.multiple_of(c * half, 128), half)
            for k, chip in enumerate(chips):
                out.append(pltpu.make_async_remote_copy(
                    src_ref=srcs[e].at[:, :, own], dst_ref=dsts[e].at[:, jme, :, own], send_sem=send_sems.at[per * e + k],
                    recv_sem=recv_sems.at[per * e + k], device_id=(*chip, c), device_id_type=MESH))
            out.append(pltpu.make_async_remote_copy(
                src_ref=srcs[e], dst_ref=dsts[e].at[:, jme], send_sem=send_sems.at[per * e + 3],
                recv_sem=recv_sems.at[per * e + 3], device_id=(x, y, 1 - c), device_id_type=MESH))
        return out

    def start(*a):
        for cp in copies(*a):
            cp.start()

    def finish(*a):
        for cp in copies(*a):
            cp.wait()

    results = [jax.ShapeDtypeStruct((s.shape[0], 4) + s.shape[1:], s.dtype) for s in shards]
    return _Carried(list(shards), results, per * ne, start, finish)


def _gather_hand_on(bufs):
    ne = len(bufs)

    def copy(e, k, dsts, send_sems, recv_sems, mine):
        x, y, c, chips = _place()
        px, py = chips[k]
        half = dsts[e].shape[3] // 2
        cols = pl.ds(pl.multiple_of((c if mine else 1 - c) * half, 128), half)
        part = dsts[e].at[:, 2 * px + py, :, cols]
        return pltpu.make_async_remote_copy(src_ref=part, dst_ref=part, send_sem=send_sems.at[3 * e + k],
                                            recv_sem=recv_sems.at[3 * e + k], device_id=(x, y, 1 - c), device_id_type=MESH)

    def start(srcs, dsts, send_sems, recv_sems):
        for e in range(ne):
            for k in range(3):
                copy(e, k, dsts, send_sems, recv_sems, True).start()

    def finish(srcs, dsts, send_sems, recv_sems):
        for e in range(ne):
            for k in range(3):
                copy(e, k, dsts, send_sems, recv_sems, True).wait_send()
                copy(e, k, dsts, send_sems, recv_sems, False).wait_recv()

    results = [jax.ShapeDtypeStruct(b.shape, b.dtype) for b in bufs]
    return _Carried(list(bufs), results, 3 * ne, start, finish, aliases={e: e for e in range(ne)})


def _pair_exchange(parts):
    ne = len(parts)

    def copies(srcs, dsts, send_sems, recv_sems):
        x, y, c, _ = _place()
        out = []
        for e in range(ne):
            half = srcs[e].shape[2] // 2
            theirs = pl.ds(pl.multiple_of((1 - c) * half, 128), half)
            out.append(pltpu.make_async_remote_copy(
                src_ref=srcs[e].at[:, :, theirs], dst_ref=dsts[e], send_sem=send_sems.at[e],
                recv_sem=recv_sems.at[e], device_id=(x, y, 1 - c), device_id_type=MESH))
        return out

    def start(*a):
        for cp in copies(*a):
            cp.start()

    def finish(*a):
        for cp in copies(*a):
            cp.wait()

    results = [jax.ShapeDtypeStruct(p.shape[:2] + (p.shape[2] // 2,), p.dtype) for p in parts]
    return _Carried(list(parts), results, ne, start, finish)


def _chip_scatter(sums):
    ne = len(sums)

    def copies(srcs, dsts, send_sems, recv_sems):
        x, y, c, chips = _place()
        return [pltpu.make_async_remote_copy(
                    src_ref=srcs[e].at[2 * px + py], dst_ref=dsts[e].at[k], send_sem=send_sems.at[3 * e + k],
                    recv_sem=recv_sems.at[3 * e + k], device_id=(px, py, c), device_id_type=MESH)
                for e in range(ne) for k, (px, py) in enumerate(chips)]

    def start(*a):
        for cp in copies(*a):
            cp.start()

    def finish(*a):
        for cp in copies(*a):
            cp.wait()

    results = [jax.ShapeDtypeStruct((3,) + s.shape[1:], s.dtype) for s in sums]
    return _Carried(list(sums), results, 3 * ne, start, finish)


def _pair_complete(grads):
    ne = len(grads)

    def copy(e, dsts, send_sems, recv_sems, mine):
        x, y, c, _ = _place()
        half = dsts[e].shape[1] // 2
        cols = pl.ds(pl.multiple_of((c if mine else 1 - c) * half, 128), half)
        return pltpu.make_async_remote_copy(
            src_ref=dsts[e].at[:, cols], dst_ref=dsts[e].at[:, cols], send_sem=send_sems.at[e],
            recv_sem=recv_sems.at[e], device_id=(x, y, 1 - c), device_id_type=MESH)

    def start(srcs, dsts, send_sems, recv_sems):
        for e in range(ne):
            copy(e, dsts, send_sems, recv_sems, True).start()

    def finish(srcs, dsts, send_sems, recv_sems):
        for e in range(ne):
            copy(e, dsts, send_sems, recv_sems, True).wait_send()
            copy(e, dsts, send_sems, recv_sems, False).wait_recv()

    results = [jax.ShapeDtypeStruct(g.shape, g.dtype) for g in grads]
    return _Carried(list(grads), results, ne, start, finish, aliases={e: e for e in range(ne)})


def _pair_sums(parts, theirs, cidx, name):
    ne = len(parts)

    def body(c_ref, *refs):
        for e in range(ne):
            refs[2 * ne + e][...] = (refs[e][...].astype(F32) + refs[ne + e][...].astype(F32)).astype(BF16)

    halves = [(p.shape[1], p.shape[2] // 2) for p in parts]
    gs = pltpu.PrefetchScalarGridSpec(
        num_scalar_prefetch=1, grid=(4,),
        in_specs=[pl.BlockSpec((1, r, h), lambda j, c: (j, 0, c[0])) for r, h in halves]
                 + [pl.BlockSpec((1, r, h), lambda j, c: (j, 0, 0)) for r, h in halves],
        out_specs=[pl.BlockSpec((1, r, h), lambda j, c: (j, 0, 0)) for r, h in halves])
    return list(pl.pallas_call(body, name=name, grid_spec=gs, out_shape=[jax.ShapeDtypeStruct((4, r, h), BF16) for r, h in halves],
                               compiler_params=_params("arbitrary"))(cidx, *parts, *theirs))


def _chip_sums(sums, landed, jc, name):
    ne = len(sums)

    def body(jc_ref, *refs):
        for e in range(ne):
            acc = refs[e][0].astype(F32)
            for k in range(3):
                acc = acc + refs[ne + e][k].astype(F32)
            refs[2 * ne + e][...] = acc

    halves = [sm.shape[1:] for sm in sums]
    gs = pltpu.PrefetchScalarGridSpec(
        num_scalar_prefetch=1, grid=(1,),
        in_specs=[pl.BlockSpec((1, r, h), lambda i, jc: (jc[0], 0, 0)) for r, h in halves]
                 + [pl.BlockSpec((3, r, h), lambda i, jc: (0, 0, 0)) for r, h in halves],
        out_specs=[pl.BlockSpec((r, h), lambda i, jc: (0, jc[1])) for r, h in halves])
    return list(pl.pallas_call(body, name=name, grid_spec=gs, out_shape=[jax.ShapeDtypeStruct((r, 2 * h), F32) for r, h in halves],
                               compiler_params=_params("arbitrary"))(jc, *sums, *landed))


def _rope_tables(pos):
    s = pos.shape[0]
    lane = jnp.arange(128)
    inv = 1.0 / (ROPE_THETA ** ((2 * (lane % 32)).astype(F32) / ROPE))
    keep = (lane < 64).astype(F32)
    sign = jnp.where(lane < 32, -1.0, 1.0).astype(F32) * keep
    consts = jnp.stack([inv.astype(F32), keep, sign])[:, None, :]

    def body(p_ref, k_ref, c_ref, s_ref):
        ang = p_ref[...].astype(F32) * k_ref[0]
        c_ref[...] = jnp.cos(ang) * k_ref[1]
        s_ref[...] = jnp.sin(ang) * k_ref[2]

    tm = min(s, 1024)
    return pl.pallas_call(
        body, name="rope_tables", grid=(s // tm,),
        in_specs=[pl.BlockSpec((tm, 1), lambda i: (i, 0)), pl.BlockSpec((3, 1, 128), lambda i: (0, 0, 0))],
        out_specs=[pl.BlockSpec((tm, 128), lambda i: (i, 0))] * 2,
        out_shape=[jax.ShapeDtypeStruct((s, 128), F32)] * 2, compiler_params=_params("arbitrary"),
    )(pos, consts)


def _modulation(c_all, ada_w, ada_b_shard):
    nl, d, n = ada_w.shape
    tn = 512

    def body(c_ref, w_ref, b_ref, o_ref):
        cv = c_ref[...]
        act = (cv * _sigmoid(cv)).astype(BF16)
        o_ref[...] = _dot(act, w_ref[...].astype(BF16)) + b_ref[...]

    return pl.pallas_call(
        body, name="modulation", grid=(nl, n // tn),
        in_specs=[pl.BlockSpec((8, d), lambda l, j: (0, 0)), pl.BlockSpec((None, d, tn), lambda l, j: (l, 0, j)),
                  pl.BlockSpec((None, 1, tn), lambda l, j: (l, 0, j))],
        out_specs=pl.BlockSpec((None, 8, tn), lambda l, j: (l, 0, j)),
        out_shape=jax.ShapeDtypeStruct((nl, 8, n), F32), compiler_params=_params("arbitrary", "arbitrary"),
    )(c_all, ada_w, ada_b_shard)


W_ROWS = 4992
W_PIECES = ((0, 0, 832), (832, 768, 64), (896, 1856, 2048), (2944, 832, 1024), (3968, 3904, 1024))


def _load_w_in(w_hbm, w_vmem, sems):
    cps = [pltpu.make_async_copy(w_hbm.at[pl.ds(src, n)], w_vmem.at[pl.ds(dst, n)], sems.at[i])
           for i, (dst, src, n) in enumerate(W_PIECES)]
    for cp in cps:
        cp.start()
    for cp in cps:
        cp.wait()


def _inproj_fwd(x, g, scale, shift, w_int, layer, name, carried=None):
    s, d = x.shape
    tm = min(s, 512)

    def body(x_ref, g_ref, sc_ref, sh_ref, w_hbm, hb_ref, za_ref, zkr_ref, zb_ref, w_vmem, sems):
        @pl.when(pl.program_id(0) == 0)
        def _():
            _load_w_in(w_hbm.at[layer], w_vmem, sems)

        xv = x_ref[...]
        rstd = lax.rsqrt(jnp.mean(xv * xv, axis=1, keepdims=True) + EPS)
        h = (xv * rstd) * g_ref[...] * (1.0 + sc_ref[...]) + sh_ref[...]
        hb = h.astype(BF16)
        hb_ref[...] = hb
        za_ref[...] = _dot(hb, w_vmem[0:768], NT)
        zkr_ref[...] = _dot(hb, w_vmem[768:896], NT)
        zb_ref[...] = _dot(hb, w_vmem[896:W_ROWS], NT)

    return _call(
        body, (x, g, scale, shift, w_int), name=name, grid=(s // tm,), carried=carried,
        in_specs=[pl.BlockSpec((tm, d), lambda i: (i, 0)), _row(d), _row(d), _row(d), ANY],
        out_specs=[pl.BlockSpec((tm, d), lambda i: (i, 0)), pl.BlockSpec((tm, 768), lambda i: (i, 0)),
                   pl.BlockSpec((tm, 128), lambda i: (i, 0)), pl.BlockSpec((tm, 4096), lambda i: (i, 0))],
        out_shape=[jax.ShapeDtypeStruct((s, d), BF16), jax.ShapeDtypeStruct((s, 768), F32),
                   jax.ShapeDtypeStruct((s, 128), F32), jax.ShapeDtypeStruct((s, 4096), F32)],
        scratch=[pltpu.VMEM((W_ROWS, d), BF16), pltpu.SemaphoreType.DMA((len(W_PIECES),))])


def _rope(yv, cos, sin):
    return yv * cos + pltpu.roll(yv, 32, axis=1) * sin


def _mla_prep_fwd(za, zkr, cos, sin, wq, wkv, gql, gkvl, gq2, gk2, layer, name, carried=None):
    s = za.shape[0]
    tm = min(s, 256)

    def body(za_ref, zkr_ref, cos_ref, sin_ref, wq_ref, wkv_ref, gql_ref, gkvl_ref, gq_ref, gk_ref, q_ref, k_ref, v_ref):
        zq, zkv = za_ref[:, 0:QL], za_ref[:, QL:QL + KVL]
        qn = (zq * lax.rsqrt(jnp.mean(zq * zq, axis=1, keepdims=True) + EPS) * gql_ref[...]).astype(BF16)
        kvn = (zkv * lax.rsqrt(jnp.mean(zkv * zkv, axis=1, keepdims=True) + EPS) * gkvl_ref[...]).astype(BF16)
        kr = zkr_ref[...]
        kr_ss = 0.5 * _sum1(kr * kr)
        cos, sin = cos_ref[...], sin_ref[...]
        gq, gk = gq_ref[...] * SCORE_SCALE, gk_ref[...]
        qr_all, kvr_all = _dot(qn, wq_ref[...]), _dot(kvn, wkv_ref[...])
        for h in range(N_HEADS):
            qr = qr_all[:, h * HP:(h + 1) * HP]
            n, yv = qr[:, :NOPE], qr[:, NOPE:]
            rstd = lax.rsqrt((_sum1(n * n) + 0.5 * _sum1(yv * yv)) * (1.0 / QK) + EPS)
            q_ref[h, :, 0:NOPE] = (n * rstd * gq[:, :NOPE]).astype(BF16)
            q_ref[h, :, NOPE:HP] = _rope(yv * rstd * gq[:, NOPE:], cos, sin).astype(BF16)
            kvr = kvr_all[:, h * HP:(h + 1) * HP]
            kn, vv = kvr[:, :NOPE], kvr[:, NOPE:]
            rstd = lax.rsqrt((_sum1(kn * kn) + kr_ss) * (1.0 / QK) + EPS)
            k_ref[h, :, 0:NOPE] = (kn * rstd * gk[:, :NOPE]).astype(BF16)
            k_ref[h, :, NOPE:HP] = _rope(kr * rstd * gk[:, NOPE:], cos, sin).astype(BF16)
            v_ref[h] = vv.astype(BF16)

    tile = lambda n: pl.BlockSpec((tm, n), lambda i: (i, 0))
    return _call(
        body, (za, zkr, cos, sin, wq, wkv, gql, gkvl, gq2, gk2), name=name, grid=(s // tm,), carried=carried,
        in_specs=[tile(768), tile(128), tile(128), tile(128),
                  pl.BlockSpec((None, QL, N_HEADS * HP), lambda i: (layer, 0, 0)),
                  pl.BlockSpec((None, KVL, N_HEADS * HP), lambda i: (layer, 0, 0)),
                  _row(QL), _row(KVL), _row(HP), _row(HP)],
        out_specs=[pl.BlockSpec((N_HEADS, tm, HP), lambda i: (0, i, 0)), pl.BlockSpec((N_HEADS, tm, HP), lambda i: (0, i, 0)),
                   pl.BlockSpec((N_HEADS, tm, VD), lambda i: (0, i, 0))],
        out_shape=[jax.ShapeDtypeStruct((N_HEADS, s, HP), BF16), jax.ShapeDtypeStruct((N_HEADS, s, HP), BF16),
                   jax.ShapeDtypeStruct((N_HEADS, s, VD), BF16)])


SCORE_SCALE = 1.0 / math.sqrt(QK)
MASKED = -1e30


def _flash_fwd(q, k, v, name, carried=None):
    s = q.shape[1]
    t = min(s, 1024)

    def body(q_ref, k_ref, v_ref, o_ref, lse_ref):
        i = pl.program_id(1)
        qb = q_ref[...]
        row = lax.broadcasted_iota(jnp.int32, (t, t), 0)
        col = lax.broadcasted_iota(jnp.int32, (t, t), 1)

        def block(j):
            return pl.ds(pl.multiple_of(j * t, t), t)

        def scores(j):
            return _dot(qb, k_ref[block(j), :], NT)

        def update(j, sc, m, l, acc, diagonal):
            if diagonal:
                sc = jnp.where(col <= row, sc, MASKED)
            m_new = jnp.maximum(m, jnp.max(sc, axis=1, keepdims=True))
            p = jnp.exp(sc - m_new)
            alpha = jnp.exp(m - m_new)
            return m_new, alpha * l + _sum1(p), alpha * acc + _dot(p.astype(BF16), v_ref[block(j), :])

        init = (jnp.full((t, 1), MASKED, F32), jnp.zeros((t, 1), F32), jnp.zeros((t, VD), F32))
        carry = lax.fori_loop(0, i, lambda j, cr: update(j, scores(j), *cr, False), init)
        m, l, acc = update(i, scores(i), *carry, True)
        o_ref[...] = acc / l
        lse_ref[...] = m + jnp.log(l)

    return _call(
        body, (q, k, v), name=name, grid=(N_HEADS, s // t), carried=carried,
        in_specs=[pl.BlockSpec((None, t, HP), lambda h, i: (h, i, 0)), pl.BlockSpec((None, s, HP), lambda h, i: (h, 0, 0)),
                  pl.BlockSpec((None, s, VD), lambda h, i: (h, 0, 0))],
        out_specs=[pl.BlockSpec((t, VD), lambda h, i: (i, h)), pl.BlockSpec((None, t, 1), lambda h, i: (h, i, 0))],
        out_shape=[jax.ShapeDtypeStruct((s, N_HEADS * VD), F32), jax.ShapeDtypeStruct((N_HEADS, s, 1), F32)])


CH, RC = 256, 64


PH_ROWS_LESS = 8


def _glu(val, gate, bias):
    c = val.shape[1]
    return (val + bias[:, :c]) * _sigmoid(gate + bias[:, c:])


def _make_phases(buf, phases, cc):
    rows = buf.shape[0] - PH_ROWS_LESS
    for b in range(1, 8):
        phases[b - 1] = buf[pl.ds(b, rows), cc:cc + CH]


def _window(buf, phases, cc, shift, r0):
    a, b = divmod(shift, 8)
    if b == 0:
        return buf[r0 + 8 * a:r0 + 8 * a + RC, cc:cc + CH]
    return phases[b - 1, r0 + 8 * a:r0 + 8 * a + RC, :]


def _conv_fwd(zb, glu_b, dw, dwb, lng, lnb, wpw, bpw, layer, name, carried=None):
    s = zb.shape[0]
    dc = dwb.shape[1]
    tm = min(s, 256)
    hb = tm // HALO

    def body(val_ref, gate_ref, valh_ref, gateh_ref, glub_ref, dw_ref, dwb_ref, lng_ref, lnb_ref, wpw_ref, bpw_ref,
             cv_ref, pw_ref, ubuf, uph):
        i = pl.program_id(0)
        bias = glub_ref[...]
        ubuf[HALO:, :] = _glu(val_ref[...], gate_ref[...], bias)
        uh = _glu(valh_ref[...], gateh_ref[...], bias)
        ubuf[0:HALO, :] = jnp.where(i > 0, uh, 0.0)
        for cc in range(0, dc, CH):
            _make_phases(ubuf, uph, cc)
            for r0 in range(0, tm, RC):
                acc = jnp.zeros((RC, CH), F32)
                for j in range(CONV_K):
                    acc = acc + _window(ubuf, uph, cc, HALO - (CONV_K - 1) + j, r0) * dw_ref[j:j + 1, cc:cc + CH]
                cv_ref[r0:r0 + RC, cc:cc + CH] = acc + dwb_ref[:, cc:cc + CH]
        cv = cv_ref[...]
        dv = cv - jnp.mean(cv, axis=1, keepdims=True)
        yl = dv * lax.rsqrt(jnp.mean(dv * dv, axis=1, keepdims=True) + EPS) * lng_ref[...] + lnb_ref[...]
        act = (yl * _sigmoid(yl)).astype(BF16)
        pw_ref[...] = _dot(act, wpw_ref[...]) + bpw_ref[...]

    return _call(
        body, (zb, zb, zb, zb, glu_b, dw, dwb, lng, lnb, wpw, bpw), name=name, grid=(s // tm,), carried=carried,
        in_specs=[pl.BlockSpec((tm, dc), lambda i: (i, 0)), pl.BlockSpec((tm, dc), lambda i: (i, 1)),
                  pl.BlockSpec((HALO, dc), lambda i: (jnp.maximum(i * hb - 1, 0), 0)),
                  pl.BlockSpec((HALO, dc), lambda i: (jnp.maximum(i * hb - 1, 0), 1)),
                  _row(2 * dc), pl.BlockSpec((None, HALO, dc), lambda i: (layer, 0, 0)), _row(dc), _row(dc), _row(dc),
                  pl.BlockSpec((None, dc, dc), lambda i: (layer, 0, 0)), _row(dc)],
        out_specs=[pl.BlockSpec((tm, dc), lambda i: (i, 0))] * 2,
        out_shape=[jax.ShapeDtypeStruct((s, dc), F32)] * 2,
        scratch=[pltpu.VMEM((tm + HALO, dc), F32), pltpu.VMEM((7, tm + HALO - PH_ROWS_LESS, CH), F32)])


def _silu_parts(z):
    sg = _sigmoid(z)
    return z * sg, sg * (1.0 + z * (1.0 - sg))


def _outproj_fwd(x, o, zb, pw, gate, wout, layer, name, carried=None, target=None):
    s, d = x.shape
    dm = o.shape[1]
    tm = min(s, 256)

    def project(x_ref, o_ref, mg_ref, cg_ref, pw_ref, gate_ref, w_ref, mix_ref):
        mg, cg = mg_ref[...], cg_ref[...]
        mix_ref[:, 0:dm] = (o_ref[...] * (mg * _sigmoid(mg))).astype(BF16)
        mix_ref[:, dm:] = (pw_ref[...] * (cg * _sigmoid(cg))).astype(BF16)
        return x_ref[...] + gate_ref[...] * _dot(mix_ref[...], w_ref[...])

    def body(*refs):
        xn_ref, mix_ref = refs[7:]
        xn_ref[...] = project(*refs[:7], mix_ref)

    def body_with_loss(*refs):
        t_ref, l_ref, dx_ref, mix_ref = refs[7:]
        err = project(*refs[:7], mix_ref) - t_ref[...]
        l_ref[...] = 0.5 * jnp.mean(err * err, axis=1, keepdims=True)
        dx_ref[...] = err * (1.0 / d)

    tile = lambda n, j=0: pl.BlockSpec((tm, n), lambda i: (i, j))
    in_specs = [tile(d), tile(dm), tile(dm, 2), tile(dm, 3), tile(dm), _row(d), pl.BlockSpec((None, 2 * dm, d), lambda i: (layer, 0, 0))]
    if target is None:
        return _call(body, (x, o, zb, zb, pw, gate, wout), name=name, grid=(s // tm,), carried=carried, in_specs=in_specs,
                     out_specs=[tile(d), tile(2 * dm)],
                     out_shape=[jax.ShapeDtypeStruct((s, d), F32), jax.ShapeDtypeStruct((s, 2 * dm), BF16)])
    return _call(body_with_loss, (x, o, zb, zb, pw, gate, wout, target), name=name, grid=(s // tm,), carried=carried,
                 in_specs=in_specs + [tile(d)], out_specs=[tile(1), tile(d), tile(2 * dm)],
                 out_shape=[jax.ShapeDtypeStruct((s, 1), F32), jax.ShapeDtypeStruct((s, d), F32), jax.ShapeDtypeStruct((s, 2 * dm), BF16)])


def _grad_tn(a, b, name, carried=None):
    s, n = a.shape
    m = b.shape[1]
    tn, ts = min(n, 1024), min(s, 512)
    nt = s // ts

    def body(a_ref, b_ref, o_ref, acc):
        t = pl.program_id(1)

        @pl.when(t == 0)
        def _():
            acc[...] = jnp.zeros_like(acc)

        acc[...] += _dot(a_ref[...].astype(BF16), b_ref[...].astype(BF16), TN)

        @pl.when(t == nt - 1)
        def _():
            o_ref[...] = acc[...].astype(BF16)

    return _call(
        body, (a, b), name=name, grid=(n // tn, nt), carried=carried,
        in_specs=[pl.BlockSpec((ts, tn), lambda r, t: (t, r)), pl.BlockSpec((ts, m), lambda r, t: (t, 0))],
        out_specs=[pl.BlockSpec((tn, m), lambda r, t: (r, 0))], out_shape=[jax.ShapeDtypeStruct((n, m), BF16)],
        scratch=[pltpu.VMEM((tn, m), F32)])


def _grad_w_out(mixb, dxb, gate, wout, layer, name, carried=None):
    s, n = mixb.shape
    d = dxb.shape[1]
    tn, ts = min(n, 1024), min(s, 512)
    nt = s // ts

    def body(a_ref, b_ref, gate_ref, w_ref, g_ref, dgate_ref, acc):
        t = pl.program_id(1)

        @pl.when(t == 0)
        def _():
            acc[...] = jnp.zeros_like(acc)

        acc[...] += _dot(a_ref[...], b_ref[...], TN)

        @pl.when(t == nt - 1)
        def _():
            m = acc[...]
            dgate_ref[...] = _sum0(m * w_ref[...].astype(F32))
            g_ref[...] = (m * gate_ref[...]).astype(BF16)

    return _call(
        body, (mixb, dxb, gate, wout), name=name, grid=(n // tn, nt), carried=carried,
        in_specs=[pl.BlockSpec((ts, tn), lambda r, t: (t, r)), pl.BlockSpec((ts, d), lambda r, t: (t, 0)), _row(d),
                  pl.BlockSpec((None, tn, d), lambda r, t: (layer, r, 0))],
        out_specs=[pl.BlockSpec((tn, d), lambda r, t: (r, 0)), pl.BlockSpec((None, 1, d), lambda r, t: (r, 0, 0))],
        out_shape=[jax.ShapeDtypeStruct((n, d), BF16), jax.ShapeDtypeStruct((n // tn, 1, d), F32)],
        scratch=[pltpu.VMEM((tn, d), F32)])


def _outproj_bwd(dxo, gate, wout, o, zb, pw, layer, name, carried=None):
    s, d = dxo.shape
    dm = o.shape[1]
    tm = min(s, 256)

    def body(dx_ref, gate_ref, w_ref, o_ref, mg_ref, cg_ref, pw_ref, dxb_ref, do_ref, delta_ref, dzb_ref, dpw_ref):
        dx = dx_ref[...]
        dxb_ref[...] = dx.astype(BF16)
        dmix = _dot((dx * gate_ref[...]).astype(BF16), w_ref[...], NT)
        da, db = dmix[:, :dm], dmix[:, dm:]
        ov = o_ref[...]
        silu_m, dsilu_m = _silu_parts(mg_ref[...])
        do = da * silu_m
        do_ref[...] = do.astype(BF16)
        prod = do * ov
        for h in range(N_HEADS):
            delta_ref[h] = _sum1(prod[:, h * VD:(h + 1) * VD])
        dzb_ref[:, 0:dm] = da * ov * dsilu_m
        silu_c, dsilu_c = _silu_parts(cg_ref[...])
        dpw_ref[...] = db * silu_c
        dzb_ref[:, dm:] = db * pw_ref[...] * dsilu_c

    tile = lambda n, j=0: pl.BlockSpec((tm, n), lambda i: (i, j))
    return _call(
        body, (dxo, gate, wout, o, zb, zb, pw), name=name, grid=(s // tm,), carried=carried,
        in_specs=[tile(d), _row(d), pl.BlockSpec((None, 2 * dm, d), lambda i: (layer, 0, 0)),
                  tile(dm), tile(dm, 2), tile(dm, 3), tile(dm)],
        out_specs=[tile(d), tile(dm), pl.BlockSpec((N_HEADS, tm, 1), lambda i: (0, i, 0)), tile(2 * dm, 1), tile(dm)],
        out_shape=[jax.ShapeDtypeStruct((s, d), BF16), jax.ShapeDtypeStruct((s, dm), BF16),
                   jax.ShapeDtypeStruct((N_HEADS, s, 1), F32), jax.ShapeDtypeStruct((s, 4 * dm), F32),
                   jax.ShapeDtypeStruct((s, dm), F32)])


def _flash_bwd(q, k, v, do, lse, delta, name, carried=None):
    s = q.shape[1]
    t = min(s, 1024)
    nq = s // t

    def body(q_ref, k_ref, v_ref, do_ref, lse_ref, delta_ref, dq_ref, dk_ref, dv_ref):
        j = pl.program_id(1)

        @pl.when(j == 0)
        def _():
            dq_ref[...] = jnp.zeros_like(dq_ref)

        kb, vb = k_ref[...], v_ref[...]
        row = lax.broadcasted_iota(jnp.int32, (t, t), 0)
        col = lax.broadcasted_iota(jnp.int32, (t, t), 1)

        def block(i):
            return pl.ds(pl.multiple_of(i * t, t), t)

        def scores(i):
            at = block(i)
            return _dot(q_ref[at, :], kb, NT), _dot(do_ref[at, :], vb, NT)

        def update(i, sc, dp, dk, dv, diagonal):
            at = block(i)
            p = jnp.exp(sc - lse_ref[at, :])
            if diagonal:
                p = jnp.where(col <= row, p, 0.0)
            dv = dv + _dot(p.astype(BF16), do_ref[at, :], TN)
            ds = (p * (dp - delta_ref[at, :])).astype(BF16)
            dq_ref[at, :] += _dot(ds, kb)
            return dk + _dot(ds, q_ref[at, :], TN), dv

        carry = update(j, *scores(j), jnp.zeros((t, HP), F32), jnp.zeros((t, VD), F32), True)
        dk, dv = lax.fori_loop(j + 1, nq, lambda i, cr: update(i, *scores(i), *cr, False), carry)
        dk_ref[...] = dk
        dv_ref[...] = dv

    whole = lambda n: pl.BlockSpec((None, s, n), lambda h, j: (h, 0, 0))
    blk = lambda n: pl.BlockSpec((None, t, n), lambda h, j: (h, j, 0))
    return _call(
        body, (q, k, v, do, lse, delta), name=name, grid=(N_HEADS, nq), carried=carried,
        in_specs=[whole(HP), blk(HP), blk(VD), pl.BlockSpec((s, VD), lambda h, j: (0, h)), whole(1), whole(1)],
        out_specs=[whole(HP), blk(HP), blk(VD)],
        out_shape=[jax.ShapeDtypeStruct((N_HEADS, s, HP), F32), jax.ShapeDtypeStruct((N_HEADS, s, HP), F32),
                   jax.ShapeDtypeStruct((N_HEADS, s, VD), F32)])


def _mla_prep_bwd(dq, dk, dv, za, zkr, cos, sin, wq, wkv, gql, gkvl, gq2, gk2, layer, name, carried=None):
    s = za.shape[0]
    tm = min(s, 512)

    def norm_bwd(n, yv, rstd, gain, d_n_out, d_y_out):
        tn_, ty = n * rstd, yv * rstd
        dgain_n, dgain_y = _sum0(d_n_out * tn_), _sum0(d_y_out * ty)
        dtn, dty = d_n_out * gain[:, :NOPE], d_y_out * gain[:, NOPE:]
        a = (_sum1(dtn * n) + _sum1(dty * yv)) * (rstd * rstd * rstd * (1.0 / QK))
        return rstd * dtn - n * a, rstd * dty - (0.5 * yv) * a, dgain_n, dgain_y

    def rope_bwd(d_out, cos, sin):
        return d_out * cos + pltpu.roll(d_out * sin, 128 - 32, axis=1)

    def latent_bwd(z, gain, dn):
        rstd = lax.rsqrt(jnp.mean(z * z, axis=1, keepdims=True) + EPS)
        zh = z * rstd
        dzh = dn * gain
        return rstd * (dzh - zh * jnp.mean(dzh * zh, axis=1, keepdims=True)), _sum0(dn * zh)

    def body(dq_ref, dk_ref, dv_ref, za_ref, zkr_ref, cos_ref, sin_ref, wq_ref, wkv_ref, gql_ref, gkvl_ref, gq_ref, gk_ref,
             dza_ref, dzkr_ref, gwq_ref, gwkv_ref, dgql_ref, dgkvl_ref, dgq_ref, dgk_ref, gwq_acc, gwkv_acc):
        @pl.when(pl.program_id(0) == 0)
        def _():
            for r in (gwq_acc, gwkv_acc, dgql_ref, dgkvl_ref, dgq_ref, dgk_ref):
                r[...] = jnp.zeros_like(r)

        zq, zkv = za_ref[:, 0:QL], za_ref[:, QL:QL + KVL]
        qf = zq * lax.rsqrt(jnp.mean(zq * zq, axis=1, keepdims=True) + EPS) * gql_ref[...]
        kvf = zkv * lax.rsqrt(jnp.mean(zkv * zkv, axis=1, keepdims=True) + EPS) * gkvl_ref[...]
        qn, kvn = qf.astype(BF16), kvf.astype(BF16)
        qn_t, kvn_t = qf.T.astype(BF16), kvf.T.astype(BF16)
        kr = zkr_ref[...]
        kr_ss = 0.5 * _sum1(kr * kr)
        cos, sin = cos_ref[...], sin_ref[...]
        gq, gk = gq_ref[...], gk_ref[...]
        dkr = jnp.zeros((tm, 128), F32)
        qr_all, kvr_all = _dot(qn, wq_ref[...]), _dot(kvn, wkv_ref[...])
        dqr_all, dkvr_all = [], []
        for h in range(N_HEADS):
            qr = qr_all[:, h * HP:(h + 1) * HP]
            n, yv = qr[:, :NOPE], qr[:, NOPE:]
            rstd = lax.rsqrt((_sum1(n * n) + 0.5 * _sum1(yv * yv)) * (1.0 / QK) + EPS)
            dqh = dq_ref[h] * SCORE_SCALE
            dn, dy, dg_n, dg_y = norm_bwd(n, yv, rstd, gq, dqh[:, :NOPE], rope_bwd(dqh[:, NOPE:], cos, sin))
            dgq_ref[:, 0:NOPE] += dg_n
            dgq_ref[:, NOPE:] += dg_y
            dqr_all += [dn.astype(BF16), dy.astype(BF16)]
            kn = kvr_all[:, h * HP:h * HP + NOPE]
            rstd = lax.rsqrt((_sum1(kn * kn) + kr_ss) * (1.0 / QK) + EPS)
            dkh = dk_ref[h]
            dn, dy, dg_n, dg_y = norm_bwd(kn, kr, rstd, gk, dkh[:, :NOPE], rope_bwd(dkh[:, NOPE:], cos, sin))
            dgk_ref[:, 0:NOPE] += dg_n
            dgk_ref[:, NOPE:] += dg_y
            dkr = dkr + dy
            dkvr_all += [dn.astype(BF16), dv_ref[h].astype(BF16)]

        dqr_all, dkvr_all = jnp.concatenate(dqr_all, axis=1), jnp.concatenate(dkvr_all, axis=1)
        gwq_acc[...] += _dot(qn_t, dqr_all)
        gwkv_acc[...] += _dot(kvn_t, dkvr_all)

        @pl.when(pl.program_id(0) == s // tm - 1)
        def _():
            gwq_ref[...] = gwq_acc[...].astype(BF16)
            gwkv_ref[...] = gwkv_acc[...].astype(BF16)

        dzq, dgql = latent_bwd(zq, gql_ref[...], _dot(dqr_all, wq_ref[...], NT))
        dzkv, dgkvl = latent_bwd(zkv, gkvl_ref[...], _dot(dkvr_all, wkv_ref[...], NT))
        dgql_ref[...] += dgql
        dgkvl_ref[...] += dgkvl
        dza_ref[:, 0:QL] = dzq
        dza_ref[:, QL:] = dzkv
        lane = lax.broadcasted_iota(jnp.int32, (tm, 128), 1)
        dzkr_ref[...] = jnp.where(lane < ROPE, dkr + pltpu.roll(dkr, 64, axis=1), 0.0)

    tile = lambda n: pl.BlockSpec((tm, n), lambda i: (i, 0))
    heads = lambda n: pl.BlockSpec((N_HEADS, tm, n), lambda i: (0, i, 0))
    return _call(
        body, (dq, dk, dv, za, zkr, cos, sin, wq, wkv, gql, gkvl, gq2, gk2), name=name, grid=(s // tm,), carried=carried,
        in_specs=[heads(HP), heads(HP), heads(VD), tile(768), tile(128), tile(128), tile(128),
                  pl.BlockSpec((None, QL, N_HEADS * HP), lambda i: (layer, 0, 0)),
                  pl.BlockSpec((None, KVL, N_HEADS * HP), lambda i: (layer, 0, 0)),
                  _row(QL), _row(KVL), _row(HP), _row(HP)],
        out_specs=[tile(768), tile(128), pl.BlockSpec((QL, N_HEADS * HP), lambda i: (0, 0)),
                   pl.BlockSpec((KVL, N_HEADS * HP), lambda i: (0, 0)), _row(QL), _row(KVL), _row(HP), _row(HP)],
        out_shape=[jax.ShapeDtypeStruct((s, 768), F32), jax.ShapeDtypeStruct((s, 128), F32),
                   jax.ShapeDtypeStruct((QL, N_HEADS * HP), BF16), jax.ShapeDtypeStruct((KVL, N_HEADS * HP), BF16),
                   jax.ShapeDtypeStruct((1, QL), F32), jax.ShapeDtypeStruct((1, KVL), F32),
                   jax.ShapeDtypeStruct((1, HP), F32), jax.ShapeDtypeStruct((1, HP), F32)],
        scratch=[pltpu.VMEM((QL, N_HEADS * HP), F32), pltpu.VMEM((KVL, N_HEADS * HP), F32)])


def _pointwise_bwd(dpw, cv, lng, lnb, wpw, layer, name):
    s, dc = cv.shape
    tm = min(s, 256)

    def body(dpw_ref, cv_ref, lng_ref, lnb_ref, w_ref, dcv_ref, act_ref, dbpw_ref, dlng_ref, dlnb_ref):
        @pl.when(pl.program_id(0) == 0)
        def _():
            for r in (dbpw_ref, dlng_ref, dlnb_ref):
                r[...] = jnp.zeros_like(r)

        cv = cv_ref[...]
        dv = cv - jnp.mean(cv, axis=1, keepdims=True)
        rstd = lax.rsqrt(jnp.mean(dv * dv, axis=1, keepdims=True) + EPS)
        xh = dv * rstd
        yl = xh * lng_ref[...] + lnb_ref[...]
        silu, dsilu = _silu_parts(yl)
        act_ref[...] = silu.astype(BF16)
        dpw = dpw_ref[...]
        dbpw_ref[...] += _sum0(dpw)
        dyl = _dot(dpw.astype(BF16), w_ref[...], NT) * dsilu
        dlng_ref[...] += _sum0(dyl * xh)
        dlnb_ref[...] += _sum0(dyl)
        dxh = dyl * lng_ref[...]
        dcv_ref[...] = rstd * (dxh - jnp.mean(dxh, axis=1, keepdims=True) - xh * jnp.mean(dxh * xh, axis=1, keepdims=True))

    tile = pl.BlockSpec((tm, dc), lambda i: (i, 0))
    return pl.pallas_call(
        body, name=name, grid=(s // tm,),
        in_specs=[tile, tile, _row(dc), _row(dc), pl.BlockSpec((None, dc, dc), lambda i: (layer, 0, 0))],
        out_specs=[tile, tile, _row(dc), _row(dc), _row(dc)],
        out_shape=[jax.ShapeDtypeStruct((s, dc), F32), jax.ShapeDtypeStruct((s, dc), BF16)] + [jax.ShapeDtypeStruct((1, dc), F32)] * 3,
        compiler_params=_params("arbitrary"),
    )(dpw, cv, lng, lnb, wpw)


def _conv_bwd(dcv, zb, dzb, glu_b, dw, layer, name, carried=None):
    s, dc = dcv.shape
    tm = min(s, 256)
    hb = tm // HALO
    last = s // tm - 1

    def body(dcv_ref, dcvn_ref, val_ref, gate_ref, valh_ref, gateh_ref, glub_ref, dw_ref, _, dzb_ref, gdw_ref, ddwb_ref, dglub_ref,
             ubuf, dbuf, gacc, uph, dph):
        i = pl.program_id(0)

        @pl.when(i == 0)
        def _():
            gacc[...] = jnp.zeros_like(gacc)
            ddwb_ref[...] = jnp.zeros_like(ddwb_ref)
            dglub_ref[...] = jnp.zeros_like(dglub_ref)

        bias = glub_ref[...]
        ubuf[HALO:, :] = _glu(val_ref[...], gate_ref[...], bias)
        ubuf[0:HALO, :] = jnp.where(i > 0, _glu(valh_ref[...], gateh_ref[...], bias), 0.0)
        dcv = dcv_ref[...]
        dbuf[0:tm, :] = dcv
        dbuf[tm:, :] = jnp.where(i < last, dcvn_ref[...], 0.0)
        ddwb_ref[...] += _sum0(dcv)
        for cc in range(0, dc, CH):
            _make_phases(ubuf, uph, cc)
            _make_phases(dbuf, dph, cc)
            for r0 in range(0, tm, RC):
                du = jnp.zeros((RC, CH), F32)
                dpiece = dbuf[r0:r0 + RC, cc:cc + CH]
                for j in range(CONV_K):
                    du = du + _window(dbuf, dph, cc, (CONV_K - 1) - j, r0) * dw_ref[j:j + 1, cc:cc + CH]
                    win = _window(ubuf, uph, cc, HALO - (CONV_K - 1) + j, r0)
                    gacc[j, :, cc:cc + CH] += (dpiece * win).reshape(RC // 8, 8, CH).sum(axis=0)
                a = val_ref[r0:r0 + RC, cc:cc + CH] + bias[:, cc:cc + CH]
                sg = _sigmoid(gate_ref[r0:r0 + RC, cc:cc + CH] + bias[:, dc + cc:dc + cc + CH])
                dzb_ref[r0:r0 + RC, cc:cc + CH] = du * sg
                dzb_ref[r0:r0 + RC, dc + cc:dc + cc + CH] = du * a * sg * (1.0 - sg)
        dglub_ref[...] += _sum0(dzb_ref[...])

        @pl.when(i == last)
        def _():
            total = jnp.sum(gacc[...], axis=1)
            for cc in range(0, dc, CH):
                gdw_ref[cc // CH] = total[:, cc:cc + CH]

    return _call(
        body, (dcv, dcv, zb, zb, zb, zb, glu_b, dw, dzb), name=name, grid=(s // tm,), carried=carried, aliases={8: 0},
        in_specs=[pl.BlockSpec((tm, dc), lambda i: (i, 0)),
                  pl.BlockSpec((HALO, dc), lambda i: (jnp.minimum((i + 1) * hb, s // HALO - 1), 0)),
                  pl.BlockSpec((tm, dc), lambda i: (i, 0)), pl.BlockSpec((tm, dc), lambda i: (i, 1)),
                  pl.BlockSpec((HALO, dc), lambda i: (jnp.maximum(i * hb - 1, 0), 0)),
                  pl.BlockSpec((HALO, dc), lambda i: (jnp.maximum(i * hb - 1, 0), 1)),
                  _row(2 * dc), pl.BlockSpec((None, HALO, dc), lambda i: (layer, 0, 0)), ANY],
        out_specs=[pl.BlockSpec((tm, 2 * dc), lambda i: (i, 0)), pl.BlockSpec((4, HALO, CH), lambda i: (0, 0, 0)),
                   _row(dc), _row(2 * dc)],
        out_shape=[jax.ShapeDtypeStruct(dzb.shape, F32), jax.ShapeDtypeStruct((4, HALO, CH), F32),
                   jax.ShapeDtypeStruct((1, dc), F32), jax.ShapeDtypeStruct((1, 2 * dc), F32)],
        scratch=[pltpu.VMEM((tm + HALO, dc), F32), pltpu.VMEM((tm + HALO, dc), F32), pltpu.VMEM((HALO, 8, dc), F32),
                 pltpu.VMEM((7, tm + HALO - PH_ROWS_LESS, CH), F32), pltpu.VMEM((7, tm + HALO - PH_ROWS_LESS, CH), F32)])


def _inproj_bwd(dza, dzkr, dzb, x, dxo, g, scale, shift, w_int, layer, name, carried=None):
    s, d = x.shape
    tm = min(s, 256)

    def body(dza_ref, dzkr_ref, dzb_ref, x_ref, dxo_ref, g_ref, sc_ref, sh_ref, w_hbm, dx_ref, dsh_ref, dgg_ref, w_vmem, sems):
        @pl.when(pl.program_id(0) == 0)
        def _():
            _load_w_in(w_hbm.at[layer], w_vmem, sems)
            dsh_ref[...] = jnp.zeros_like(dsh_ref)
            dgg_ref[...] = jnp.zeros_like(dgg_ref)

        dh = _dot(dza_ref[...].astype(BF16), w_vmem[0:768])
        dh = dh + _dot(dzkr_ref[...].astype(BF16), w_vmem[768:896])
        dh = dh + _dot(dzb_ref[...].astype(BF16), w_vmem[896:W_ROWS])
        xv = x_ref[...]
        rstd = lax.rsqrt(jnp.mean(xv * xv, axis=1, keepdims=True) + EPS)
        xh = xv * rstd
        dsh_ref[...] += _sum0(dh)
        dgg_ref[...] += _sum0(dh * xh)
        dxh = dh * (g_ref[...] * (1.0 + sc_ref[...]))
        dx_ref[...] = dxo_ref[...] + rstd * (dxh - xh * jnp.mean(dxh * xh, axis=1, keepdims=True))

    tile = lambda n: pl.BlockSpec((tm, n), lambda i: (i, 0))
    return _call(
        body, (dza, dzkr, dzb, x, dxo, g, scale, shift, w_int), name=name, grid=(s // tm,), carried=carried,
        in_specs=[tile(768), tile(128), tile(4096), tile(d), tile(d), _row(d), _row(d), _row(d), ANY],
        out_specs=[tile(d), _row(d), _row(d)],
        out_shape=[jax.ShapeDtypeStruct((s, d), F32), jax.ShapeDtypeStruct((1, d), F32), jax.ShapeDtypeStruct((1, d), F32)],
        scratch=[pltpu.VMEM((W_ROWS, d), BF16), pltpu.SemaphoreType.DMA((len(W_PIECES),))])


def _grad_w_in(dza, dzkr, dzb, hb, name, carried=None):
    s, d = hb.shape
    ts = min(s, 512)
    nt = s // ts
    tiles = ((0, 768), (768, 64), (1856, 1024), (2880, 1024), (832, 1024), (3904, 1024))

    def body(a_ref, kr_ref, b_ref, h_ref, o_hbm, acc, rounded, sem):
        r, t = pl.program_id(0), pl.program_id(1)

        @pl.when(t == 0)
        def _():
            acc[...] = jnp.zeros_like(acc)

        hv = h_ref[...]

        @pl.when(r == 0)
        def _():
            acc[0:768, :] += _dot(a_ref[...].astype(BF16), hv, TN)

        @pl.when(r == 1)
        def _():
            acc[0:128, :] += _dot(kr_ref[...].astype(BF16), hv, TN)

        @pl.when(r >= 2)
        def _():
            acc[...] += _dot(b_ref[...].astype(BF16), hv, TN)

        for tile, (row0, rows) in enumerate(tiles):
            @pl.when((t == nt - 1) & (r == tile))
            def _():
                rounded[0:rows, :] = acc[0:rows, :].astype(BF16)
                cp = pltpu.make_async_copy(rounded.at[pl.ds(0, rows)], o_hbm.at[pl.ds(row0, rows)], sem)
                cp.start()
                cp.wait()

    return _call(
        body, (dza, dzkr, dzb, hb), name=name, grid=(len(tiles), nt), carried=carried,
        in_specs=[pl.BlockSpec((ts, 768), lambda r, t: (jnp.where(r == 0, t, nt - 1), 0)),
                  pl.BlockSpec((ts, 128), lambda r, t: (jnp.where(r == 1, t, jnp.where(r == 0, 0, nt - 1)), 0)),
                  pl.BlockSpec((ts, 1024), lambda r, t: (jnp.where(r >= 2, t, 0), jnp.maximum(r - 2, 0))),
                  pl.BlockSpec((ts, d), lambda r, t: (t, 0))],
        out_specs=[ANY], out_shape=[jax.ShapeDtypeStruct((4928, d), BF16)],
        scratch=[pltpu.VMEM((1024, d), F32), pltpu.VMEM((1024, d), BF16), pltpu.SemaphoreType.DMA])


def _adamw(w, g, m, v):
    m = ADAM_B1 * m + (1.0 - ADAM_B1) * g
    v = ADAM_B2 * v + (1.0 - ADAM_B2) * (g * g)
    m_hat = m / (1.0 - ADAM_B1 ** ADAM_STEP)
    v_hat = v / (1.0 - ADAM_B2 ** ADAM_STEP)
    return -ADAM_LR * (m_hat / (jnp.sqrt(v_hat) + ADAM_EPS) + ADAM_WD * w), m, v


def _adam_update(w, g0, g1, m, v, name):
    _, r, c = w.shape
    fits = [t for t in range(8, r + 1, 8) if r % t == 0 and t * c * 4 <= (1 << 21)]
    tr = max(fits) if fits else r

    def body(w_ref, g0_ref, g1_ref, m_ref, v_ref, g_ref, d_ref, mo_ref, vo_ref):
        g = jnp.where(pl.program_id(0) == 0, g0_ref[...], g1_ref[...])
        g_ref[...] = g
        d_ref[...], mo_ref[...], vo_ref[...] = _adamw(w_ref[...], g, m_ref[...], v_ref[...])

    big = pl.BlockSpec((None, tr, c), lambda l, i: (l, i, 0))
    one = pl.BlockSpec((tr, c), lambda l, i: (i, 0))
    return pl.pallas_call(
        body, name=name, grid=(2, r // tr), in_specs=[big, one, one, big, big], out_specs=[big] * 4,
        out_shape=[jax.ShapeDtypeStruct(w.shape, F32)] * 4, compiler_params=_params("arbitrary", "arbitrary"),
    )(w, g0, g1, m, v)


def _ada_update(c_all, dmod, w, m, v, carried=None):
    nl, d, n = w.shape
    tr = 256

    def body(c_ref, dm_ref, w_ref, m_ref, v_ref, g_ref, d_ref, mo_ref, vo_ref):
        cv = c_ref[...]
        act = (cv * _sigmoid(cv)).astype(BF16)
        g = _dot(act, dm_ref[...].astype(BF16), TN)
        g_ref[...] = g
        d_ref[...], mo_ref[...], vo_ref[...] = _adamw(w_ref[...], g, m_ref[...], v_ref[...])

    big = pl.BlockSpec((None, tr, n), lambda l, i: (l, i, 0))
    return _call(
        body, (c_all, dmod, w, m, v), name="ada_w_update", grid=(nl, d // tr), carried=carried,
        in_specs=[pl.BlockSpec((8, tr), lambda l, i: (0, i)), pl.BlockSpec((None, 8, n), lambda l, i: (l, 0, 0)), big, big, big],
        out_specs=[big] * 4, out_shape=[jax.ShapeDtypeStruct(w.shape, F32)] * 4)


def _small_update(gathered, w, m, v):
    r = w.shape[0]

    def body(ga_ref, w_ref, m_ref, v_ref, g_ref, d_ref, mo_ref, vo_ref):
        g = ga_ref[0]
        for dev in range(1, 8):
            g = g + ga_ref[dev]
        g_ref[...] = g
        d_ref[...], mo_ref[...], vo_ref[...] = _adamw(w_ref[...], g, m_ref[...], v_ref[...])

    return pl.pallas_call(body, name="small_update", out_shape=[jax.ShapeDtypeStruct((r, 128), F32)] * 4,
                          compiler_params=_params())(gathered, w, m, v)


SMALL = (("ada_b", 6144), ("norm_g", 2048), ("q_lat_g", 512), ("kv_lat_g", 256), ("q_norm_g", 256), ("k_norm_g", 256),
         ("glu_b", 2048), ("dw_b", 1024), ("conv_ln_g", 1024), ("conv_ln_b", 1024), ("b_pw", 1024))


def _pack_small(vals):
    cols = []
    for name, width in SMALL:
        a = vals[name]
        if a.shape[1] < width:
            a = jnp.pad(a, ((0, 0), (0, width - a.shape[1])))
        cols.append(a)
    return jnp.concatenate(cols, axis=1).reshape(-1, 128)


def _unpack_small(packed, shapes):
    flat = packed.reshape(2, -1)
    out, at = {}, 0
    for name, width in SMALL:
        out[name] = flat[:, at:at + shapes[name]]
        at += width
    return out


def _dup_gain(g):
    return jnp.concatenate([g, g[NOPE:]])[None, :]


def _undup(g):
    return jnp.concatenate([g[..., :NOPE], g[..., NOPE:NOPE + ROPE] + g[..., NOPE + ROPE:]], axis=-1)


def kernel(x, c, positions, ada_w, ada_b, norm_g, w_in, q_lat_g, w_q_up, kv_lat_g, w_kv_up, q_norm_g, k_norm_g, glu_b, dw_w, dw_b, conv_ln_g, conv_ln_b, w_pw, b_pw, w_out, loss_target, m_ada_w, m_ada_b, m_norm_g, m_w_in, m_q_lat_g, m_w_q_up, m_kv_lat_g, m_w_kv_up, m_q_norm_g, m_k_norm_g, m_glu_b, m_dw_w, m_dw_b, m_conv_ln_g, m_conv_ln_b, m_w_pw, m_b_pw, m_w_out, v_ada_w, v_ada_b, v_norm_g, v_w_in, v_q_lat_g, v_w_q_up, v_kv_lat_g, v_w_kv_up, v_q_norm_g, v_k_norm_g, v_glu_b, v_dw_w, v_dw_b, v_conv_ln_g, v_conv_ln_b, v_w_pw, v_b_pw, v_w_out):
    nl = 2
    s, d = x.shape[1], x.shape[2]
    xi, yi, ci = lax.axis_index("x"), lax.axis_index("y"), lax.axis_index("c")
    shard = 2 * xi + yi
    me = 4 * xi + 2 * yi + ci
    cidx = jnp.reshape(ci, (1,)).astype(jnp.int32)
    jc = jnp.stack([shard, ci]).astype(jnp.int32)
    x0 = x.reshape(s, d)
    target = loss_target.reshape(s, d)

    c_all = _allgather8(c.reshape(8, d // 8), "gather_c").reshape(8, d)
    n_ada = ada_w.shape[2]
    ada_b_shard = lax.dynamic_slice_in_dim(ada_b, shard * n_ada, n_ada, axis=1)[:, None, :]
    mod_shard = _modulation(c_all, ada_w, ada_b_shard)
    mod_all = _allgather8(mod_shard.reshape(nl * 8, n_ada), "gather_mod")
    mod_rows = lax.dynamic_index_in_dim(mod_all.reshape(4, 2, nl, 8, n_ada)[:, 0], me, axis=2, keepdims=False)
    mod_me = jnp.transpose(mod_rows, (1, 0, 2)).reshape(nl, 3, 1, d)

    tr = lambda a: jnp.transpose(a, (0, 2, 1))
    w_in_t = tr(w_in).astype(BF16)
    wq = w_q_up.reshape(nl, QL, 2, QK)
    wq = jnp.concatenate([wq, wq[..., NOPE:]], axis=-1)
    wq = jnp.transpose(wq, (0, 2, 1, 3)).reshape(nl, 2 * QL, HP).astype(BF16)
    dw_pad = jnp.pad(dw_w, ((0, 0), (0, HALO - CONV_K), (0, 0)))
    local = [w_in_t, wq, w_kv_up.astype(BF16), dw_pad, w_pw.astype(BF16), w_out.astype(BF16)]

    def kernel_layouts(bufs):
        w_in_g, wq_g, wkv_g, dw_g, wpw_g, wout_g = bufs
        heads_side_by_side = lambda a, rows: jnp.transpose(a.reshape(1, -1, rows, a.shape[-1]), (0, 2, 1, 3)).reshape(1, rows, -1)
        return dict(w_in=w_in_g.reshape(1, 4 * w_in_g.shape[2], d), wq=heads_side_by_side(wq_g, QL), wkv=heads_side_by_side(wkv_g, KVL),
                    dw=jnp.transpose(dw_g, (0, 2, 1, 3)).reshape(1, HALO, 4 * dw_g.shape[3]),
                    wpw=wpw_g.reshape(1, 4 * wpw_g.shape[2], wpw_g.shape[3]), wout=wout_g.reshape(1, 4 * wout_g.shape[2], d))

    w_in_all = [_run_alone(_gather_hand_on(_run_alone(_gather_start([local[0][0:1]]), "gather_w_in0")), "gather_w_in0_hand_on"), None]
    others0_start = _gather_start([a[0:1] for a in local[1:]])
    mid1_start, w_out1_start = _gather_start([a[1:2] for a in local[1:5]]), _gather_start([local[5][1:2]])
    w_in1_start = _gather_start([local[0][1:2]])
    wts = [None] * nl

    cos, sin = _rope_tables(positions.reshape(s, 1))
    row = lambda a, l: a[l][None, :]

    saved = []
    xl = x0
    for l in range(nl):
        shift, scale, gate = mod_me[l, 0], mod_me[l, 1], mod_me[l, 2]
        in_args = (xl, row(norm_g, l), scale, shift, w_in_all[l][0].reshape(1, -1, d), 0, f"inproj_fwd{l}")
        if l == 0:
            (hb, za, zkr, zb), landed = _inproj_fwd(*in_args, carried=others0_start)
            others = _run_alone(_gather_hand_on(landed), "gather_others0_hand_on")
        else:
            (hb, za, zkr, zb), others = _inproj_fwd(*in_args, carried=_gather_hand_on(mid1_landed + w_out1_landed))
        w = wts[l] = kernel_layouts(w_in_all[l] + others)
        gains = (row(q_lat_g, l), row(kv_lat_g, l), _dup_gain(q_norm_g[l]), _dup_gain(k_norm_g[l]))
        prep_args = (za, zkr, cos, sin, w["wq"], w["wkv"], *gains, 0, f"mla_prep_fwd{l}")
        conv_args = (zb, row(glu_b, l), w["dw"], row(dw_b, l), row(conv_ln_g, l), row(conv_ln_b, l), w["wpw"], row(b_pw, l), 0)
        out_args = (gate, w["wout"], 0, f"outproj_fwd{l}")
        if l == 0:
            (q, k, v), mid1_landed = _mla_prep_fwd(*prep_args, carried=mid1_start)
            (o, lse), landed = _flash_fwd(q, k, v, f"flash_fwd{l}", carried=w_in1_start)
            (cv, pw), w_in_all[1] = _conv_fwd(*conv_args, f"conv_fwd{l}", carried=_gather_hand_on(landed))
            (xn, mixb), w_out1_landed = _outproj_fwd(xl, o, zb, pw, *out_args, carried=w_out1_start)
        else:
            q, k, v = _mla_prep_fwd(*prep_args)
            o, lse = _flash_fwd(q, k, v, f"flash_fwd{l}")
            cv, pw = _conv_fwd(*conv_args, f"conv_fwd{l}")
            tok_loss, dx, mixb = _outproj_fwd(xl, o, zb, pw, *out_args, target=target)
            xn = None
        saved.append(dict(x=xl, hb=hb, za=za, zkr=zkr, zb=zb, q=q, k=k, v=v, o=o, lse=lse, cv=cv, pw=pw, mixb=mixb, gains=gains))
        xl = xn

    loss = lax.psum(jnp.sum(tok_loss), ("x", "y", "c"))

    big = [None] * nl
    small = [None] * nl
    shards_of = lambda gs: [g.reshape(4, g.shape[0] // 4, g.shape[1]) for g in gs]
    pair_sums = lambda l, parts, theirs: _pair_sums(parts, theirs, cidx, f"pair_sums{l}")
    chip_sums = lambda l, sums, landed: _chip_sums(sums, landed, jc, f"chip_sums{l}")
    halves = [None] * nl
    for l in reversed(range(nl)):
        sv, w = saved[l], wts[l]
        shift, scale, gate = mod_me[l, 0], mod_me[l, 1], mod_me[l, 2]
        dxb, do, delta, dzb, dpw = _outproj_bwd(dx, gate, w["wout"], sv["o"], sv["zb"], sv["pw"], 0, f"outproj_bwd{l}")
        out_args = (sv["mixb"], dxb, gate, w["wout"], 0, f"grad_w_out{l}")
        attn_args = (sv["q"], sv["k"], sv["v"])
        if l == 0:
            parts = shards_of(big[1])
            (g_out, dgate), theirs = _grad_w_out(*out_args, carried=_pair_exchange(parts))
            sums = pair_sums(1, parts, theirs)
            (dq, dk, dv), landed = _flash_bwd(*attn_args, do, sv["lse"], delta, f"flash_bwd{l}", carried=_chip_scatter(sums))
            halves[1] = chip_sums(1, sums, landed)
        else:
            g_out, dgate = _grad_w_out(*out_args)
            dq, dk, dv = _flash_bwd(*attn_args, do, sv["lse"], delta, f"flash_bwd{l}")
        prep_args = (dq, dk, dv, sv["za"], sv["zkr"], cos, sin, w["wq"], w["wkv"], *sv["gains"], 0, f"mla_prep_bwd{l}")
        if l == 0:
            (dza, dzkr, g_q, g_kv, dgql, dgkvl, dgq, dgk), full1 = _mla_prep_bwd(*prep_args, carried=_pair_complete(halves[1]))
        else:
            dza, dzkr, g_q, g_kv, dgql, dgkvl, dgq, dgk = _mla_prep_bwd(*prep_args)
        dcv, act, dbpw, dlng, dlnb = _pointwise_bwd(dpw, sv["cv"], row(conv_ln_g, l), row(conv_ln_b, l), w["wpw"], 0, f"pointwise_bwd{l}")
        g_pw, = _grad_tn(act, dpw, f"grad_w_pw{l}")
        by_shard = lambda g, n: jnp.transpose(g.reshape(g.shape[0], -1, n), (1, 0, 2)).reshape(-1, n)
        early = [by_shard(g_q, HP), by_shard(g_kv, 512), g_pw, g_out]
        conv_args = (dcv, sv["zb"], dzb, row(glu_b, l), w["dw"], 0, f"conv_bwd{l}")
        in_args = (sv["x"], dx, row(norm_g, l), scale, shift, w["w_in"], 0, f"inproj_bwd{l}")
        if l == 0:
            parts = shards_of(early)
            (dzb, g_dw, ddwb, dglub), theirs = _conv_bwd(*conv_args, carried=_pair_exchange(parts))
            sums = pair_sums("0e", parts, theirs)
            (g_in,), landed = _grad_w_in(dza, dzkr, dzb, sv["hb"], f"grad_w_in{l}", carried=_chip_scatter(sums))
            e_q, e_kv, e_pw, e_out = chip_sums("0e", sums, landed)
            parts = shards_of([g_in, g_dw.reshape(4 * HALO, CH)])
            sums = pair_sums("0l", parts, _run_alone(_pair_exchange(parts), "pair_exchange_late0"))
            (dx, dshift, dgg), landed = _inproj_bwd(dza, dzkr, dzb, *in_args, carried=_chip_scatter(sums))
            l_in, l_dw = chip_sums("0l", sums, landed)
            halves[0] = [l_in, e_q, e_kv, l_dw, e_pw, e_out]
        else:
            dzb, g_dw, ddwb, dglub = _conv_bwd(*conv_args)
            dx, dshift, dgg = _inproj_bwd(dza, dzkr, dzb, *in_args)
            g_in, = _grad_w_in(dza, dzkr, dzb, sv["hb"], f"grad_w_in{l}")
        big[l] = [g_in, early[0], early[1], g_dw.reshape(4 * HALO, CH), g_pw, g_out]
        small[l] = dict(ada_b=jnp.concatenate([dshift, dgg * row(norm_g, l), jnp.sum(dgate, axis=0)], axis=1), norm_g=dgg * (1.0 + scale),
                        q_lat_g=dgql, kv_lat_g=dgkvl, q_norm_g=_undup(dgq), k_norm_g=_undup(dgk), glu_b=dglub, dw_b=ddwb,
                        conv_ln_g=dlng, conv_ln_b=dlnb, b_pw=dbpw)
    grad_x = dx.reshape(x.shape)

    names = [n for n, _ in SMALL]
    mine = _pack_small({n: jnp.concatenate([small[0][n], small[1][n]], axis=0) for n in names})
    gathered = _allgather8(mine, "gather_small")
    weights = dict(ada_b=ada_b, norm_g=norm_g, q_lat_g=q_lat_g, kv_lat_g=kv_lat_g, q_norm_g=q_norm_g, k_norm_g=k_norm_g,
                   glu_b=glu_b, dw_b=dw_b, conv_ln_g=conv_ln_g, conv_ln_b=conv_ln_b, b_pw=b_pw)
    m_small = dict(ada_b=m_ada_b, norm_g=m_norm_g, q_lat_g=m_q_lat_g, kv_lat_g=m_kv_lat_g, q_norm_g=m_q_norm_g, k_norm_g=m_k_norm_g,
                   glu_b=m_glu_b, dw_b=m_dw_b, conv_ln_g=m_conv_ln_g, conv_ln_b=m_conv_ln_b, b_pw=m_b_pw)
    v_small = dict(ada_b=v_ada_b, norm_g=v_norm_g, q_lat_g=v_q_lat_g, kv_lat_g=v_kv_lat_g, q_norm_g=v_q_norm_g, k_norm_g=v_k_norm_g,
                   glu_b=v_glu_b, dw_b=v_dw_b, conv_ln_g=v_conv_ln_g, conv_ln_b=v_conv_ln_b, b_pw=v_b_pw)
    widths = {n: weights[n].shape[1] for n in names}
    v_packed = _pack_small({n: jnp.pad(v_small[n], ((0, 0), (0, dict(SMALL)[n] - widths[n])), constant_values=1.0) for n in names})
    small_out = [_unpack_small(a, widths) for a in _small_update(gathered, _pack_small(weights), _pack_small(m_small), v_packed)]

    ada_rows = gathered.reshape(8, nl, -1)[:, :, :3 * d]
    dmod = lax.dynamic_slice_in_dim(jnp.transpose(ada_rows, (1, 0, 2)), shard * n_ada, n_ada, axis=2)
    ada_out = _ada_update(c_all, dmod, ada_w, m_ada_w, v_ada_w)

    per_layer = [_run_alone(_pair_complete(halves[0]), "pair_complete0"), full1]

    def natural_q(g):
        return jnp.transpose(_undup(g.reshape(2, QL, HP)), (1, 0, 2)).reshape(QL, 2 * QK)

    grads = [[per_layer[l][0], natural_q(per_layer[l][1]), per_layer[l][2], per_layer[l][3][:CONV_K], per_layer[l][4], per_layer[l][5]]
             for l in range(nl)]
    sharded = (("w_in", tr(w_in), tr(m_w_in), tr(v_w_in)), ("w_q_up", w_q_up, m_w_q_up, v_w_q_up),
               ("w_kv_up", w_kv_up, m_w_kv_up, v_w_kv_up), ("dw_w", dw_w, m_dw_w, v_dw_w),
               ("w_pw", w_pw, m_w_pw, v_w_pw), ("w_out", w_out, m_w_out, v_w_out))
    big_out = {name: _adam_update(w, grads[0][e], grads[1][e], m, v, f"adam_{name}") for e, (name, w, m, v) in enumerate(sharded)}
    big_out["w_in"] = [tr(a) for a in big_out["w_in"]]

    order = ["ada_w", "ada_b", "norm_g", "w_in", "q_lat_g", "w_q_up", "kv_lat_g", "w_kv_up", "q_norm_g", "k_norm_g", "glu_b",
             "dw_w", "dw_b", "conv_ln_g", "conv_ln_b", "w_pw", "b_pw", "w_out"]

    def leaf(kind, name):
        if name == "ada_w":
            return ada_out[kind]
        if name in big_out:
            return big_out[name][kind]
        return small_out[kind][name]

    return (loss, grad_x, *[leaf(kind, name) for kind in range(4) for name in order])
```

```python
import functools
import math

import jax
import jax.numpy as jnp
from jax import lax
from jax.experimental import pallas as pl
from jax.experimental.pallas import tpu as pltpu

F32, BF16 = jnp.float32, jnp.bfloat16
MESH = pl.DeviceIdType.MESH
ANY = pl.BlockSpec(memory_space=pl.ANY)

N_HEADS, NOPE, ROPE, VD = 8, 128, 64, 128
QK = NOPE + ROPE
QL, KVL = 512, 256
HP = 256
CONV_K, HALO = 31, 32
ROPE_THETA = 10000.0
EPS = 1e-6
ADAM_LR, ADAM_B1, ADAM_B2, ADAM_EPS, ADAM_WD, ADAM_STEP = 0.001, 0.9, 0.999, 1e-08, 0.01, 10
V7X_VMEM_LIMIT = 56 * 1024 * 1024

NT = (((1,), (1,)), ((), ()))
TN = (((0,), (0,)), ((), ()))
NN = (((1,), (0,)), ((), ()))


def _dot(a, b, dims=NN):
    return lax.dot_general(a, b, dims, preferred_element_type=F32)


def _params(*sem):
    return pltpu.CompilerParams(dimension_semantics=sem or None, vmem_limit_bytes=V7X_VMEM_LIMIT)


def _sigmoid(x):
    return 1.0 / (1.0 + jnp.exp(-x))


def _sum0(x):
    return jnp.sum(x, axis=0, keepdims=True)


def _sum1(x):
    return jnp.sum(x, axis=1, keepdims=True)


def _row(n):
    return pl.BlockSpec((1, n), lambda *_: (0, 0))


def _place():
    x, y, c = lax.axis_index("x"), lax.axis_index("y"), lax.axis_index("c")
    chips = [(1 - x, y), (x, 1 - y), (1 - x, 1 - y)]
    return x, y, c, chips


def _allgather8(v, name):
    r, n = v.shape

    def body(v_ref, out_ref, send_sems, recv_sems, local_sem):
        x, y, c, chips = _place()
        me, sibling = (x, y, c), (x, y, 1 - c)

        def slot(px, py, pc):
            return out_ref.at[4 * px + 2 * py + pc]

        def copy(k, block, to, src=None):
            return pltpu.make_async_remote_copy(
                src_ref=slot(*block) if src is None else src, dst_ref=slot(*block),
                send_sem=send_sems.at[k], recv_sem=recv_sems.at[k], device_id=to, device_id_type=MESH)

        mine = pltpu.make_async_copy(v_ref, slot(*me), local_sem)
        mine.start()
        first = [copy(0, me, sibling, src=v_ref)]
        first += [copy(1 + j, me, (*chip, c), src=v_ref) for j, chip in enumerate(chips)]
        for cp in first:
            cp.start()
        passed = [copy(4 + j, (*chip, c), sibling) for j, chip in enumerate(chips)]
        for j, chip in enumerate(chips):
            copy(1 + j, (*chip, c), me).wait_recv()
            passed[j].start()
        copy(0, sibling, me).wait_recv()
        for j, chip in enumerate(chips):
            copy(4 + j, (*chip, 1 - c), me).wait_recv()
        for cp in first + passed:
            cp.wait_send()
        mine.wait()

    return pl.pallas_call(
        body, name=name, out_shape=jax.ShapeDtypeStruct((8, r, n), v.dtype),
        in_specs=[pl.BlockSpec(memory_space=pltpu.VMEM)], out_specs=pl.BlockSpec(memory_space=pltpu.VMEM),
        scratch_shapes=[pltpu.SemaphoreType.DMA((7,)), pltpu.SemaphoreType.DMA((7,)), pltpu.SemaphoreType.DMA],
    )(v)


class _Carried:
    def __init__(self, operands, results, n_sems, start, finish, aliases=None):
        self.operands, self.results, self.n_sems = operands, results, n_sems
        self.start, self.finish, self.aliases = start, finish, aliases or {}


def _run_alone(carried, name):
    k = len(carried.operands)

    def body(*refs):
        args = (refs[:k], refs[k:k + len(carried.results)], refs[-2], refs[-1])
        carried.start(*args)
        carried.finish(*args)

    outs = pl.pallas_call(
        body, name=name, out_shape=carried.results, in_specs=[ANY] * k, out_specs=[ANY] * len(carried.results),
        input_output_aliases=carried.aliases,
        scratch_shapes=[pltpu.SemaphoreType.DMA((carried.n_sems,)), pltpu.SemaphoreType.DMA((carried.n_sems,))],
    )(*carried.operands)
    return list(outs)


def _call(body, operands, *, name, grid, in_specs, out_specs, out_shape, scratch=(), aliases=None, carried=None):
    params = _params(*(["arbitrary"] * len(grid)))
    n_in, n_out = len(in_specs), len(out_shape)
    if carried is None:
        return pl.pallas_call(body, name=name, grid=grid, in_specs=in_specs, out_specs=out_specs, out_shape=out_shape,
                              scratch_shapes=list(scratch), input_output_aliases=aliases or {}, compiler_params=params)(*operands)
    k_in, k_out = len(carried.operands), len(carried.results)

    def wrapped(*refs):
        ins, outs = refs[:n_in], refs[n_in + k_in:n_in + k_in + n_out]
        comm = (refs[n_in:n_in + k_in], refs[n_in + k_in + n_out:n_in + k_in + n_out + k_out], refs[-2], refs[-1])
        steps = [pl.program_id(a) for a in range(len(grid))]
        first = functools.reduce(jnp.logical_and, [s == 0 for s in steps])
        last = functools.reduce(jnp.logical_and, [s == g - 1 for s, g in zip(steps, grid)])

        @pl.when(first)
        def _():
            carried.start(*comm)

        body(*ins, *outs, *refs[n_in + k_in + n_out + k_out:-2])

        @pl.when(last)
        def _():
            carried.finish(*comm)

    both = dict(aliases or {})
    both.update({n_in + i: n_out + o for i, o in carried.aliases.items()})
    res = pl.pallas_call(
        wrapped, name=name, grid=grid, in_specs=list(in_specs) + [ANY] * k_in, out_specs=list(out_specs) + [ANY] * k_out,
        out_shape=list(out_shape) + list(carried.results), input_output_aliases=both, compiler_params=params,
        scratch_shapes=list(scratch) + [pltpu.SemaphoreType.DMA((carried.n_sems,)), pltpu.SemaphoreType.DMA((carried.n_sems,))],
    )(*operands, *carried.operands)
    return list(res[:n_out]), list(res[n_out:])


def _gather_start(shards):
    ne = len(shards)
    per = 4

    def copies(srcs, dsts, send_sems, recv_sems):
        x, y, c, chips = _place()
        jme = 2 * x + y
        out = []
        for e in range(ne):
            half = srcs[e].shape[2] // 2
            own = pl.ds(pl.multiple_of(c * half, 128), half)
            for k, chip in enumerate(chips):
                out.append(pltpu.make_async_remote_copy(
                    src_ref=srcs[e].at[:, :, own], dst_ref=dsts[e].at[:, jme, :, own], send_sem=send_sems.at[per * e + k],
                    recv_sem=recv_sems.at[per * e + k], device_id=(*chip, c), device_id_type=MESH))
            out.append(pltpu.make_async_remote_copy(
                src_ref=srcs[e], dst_ref=dsts[e].at[:, jme], send_sem=send_sems.at[per * e + 3],
                recv_sem=recv_sems.at[per * e + 3], device_id=(x, y, 1 - c), device_id_type=MESH))
        return out

    def start(*a):
        for cp in copies(*a):
            cp.start()

    def finish(*a):
        for cp in copies(*a):
            cp.wait()

    results = [jax.ShapeDtypeStruct((s.shape[0], 4) + s.shape[1:], s.dtype) for s in shards]
    return _Carried(list(shards), results, per * ne, start, finish)


def _gather_hand_on(bufs):
    ne = len(bufs)

    def copy(e, k, dsts, send_sems, recv_sems, mine):
        x, y, c, chips = _place()
        px, py = chips[k]
        half = dsts[e].shape[3] // 2
        cols = pl.ds(pl.multiple_of((c if mine else 1 - c) * half, 128), half)
        part = dsts[e].at[:, 2 * px + py, :, cols]
        return pltpu.make_async_remote_copy(src_ref=part, dst_ref=part, send_sem=send_sems.at[3 * e + k],
                                            recv_sem=recv_sems.at[3 * e + k], device_id=(x, y, 1 - c), device_id_type=MESH)

    def start(srcs, dsts, send_sems, recv_sems):
        for e in range(ne):
            for k in range(3):
                copy(e, k, dsts, send_sems, recv_sems, True).start()

    def finish(srcs, dsts, send_sems, recv_sems):
        for e in range(ne):
            for k in range(3):
                copy(e, k, dsts, send_sems, recv_sems, True).wait_send()
                copy(e, k, dsts, send_sems, recv_sems, False).wait_recv()

    results = [jax.ShapeDtypeStruct(b.shape, b.dtype) for b in bufs]
    return _Carried(list(bufs), results, 3 * ne, start, finish, aliases={e: e for e in range(ne)})


def _pair_exchange(parts):
    ne = len(parts)

    def copies(srcs, dsts, send_sems, recv_sems):
        x, y, c, _ = _place()
        out = []
        for e in range(ne):
            half = srcs[e].shape[2] // 2
            theirs = pl.ds(pl.multiple_of((1 - c) * half, 128), half)
            out.append(pltpu.make_async_remote_copy(
                src_ref=srcs[e].at[:, :, theirs], dst_ref=dsts[e], send_sem=send_sems.at[e],
                recv_sem=recv_sems.at[e], device_id=(x, y, 1 - c), device_id_type=MESH))
        return out

    def start(*a):
        for cp in copies(*a):
            cp.start()

    def finish(*a):
        for cp in copies(*a):
            cp.wait()

    results = [jax.ShapeDtypeStruct(p.shape[:2] + (p.shape[2] // 2,), p.dtype) for p in parts]
    return _Carried(list(parts), results, ne, start, finish)


def _chip_scatter(sums):
    ne = len(sums)

    def copies(srcs, dsts, send_sems, recv_sems):
        x, y, c, chips = _place()
        return [pltpu.make_async_remote_copy(
                    src_ref=srcs[e].at[2 * px + py], dst_ref=dsts[e].at[k], send_sem=send_sems.at[3 * e + k],
                    recv_sem=recv_sems.at[3 * e + k], device_id=(px, py, c), device_id_type=MESH)
                for e in range(ne) for k, (px, py) in enumerate(chips)]

    def start(*a):
        for cp in copies(*a):
            cp.start()

    def finish(*a):
        for cp in copies(*a):
            cp.wait()

    results = [jax.ShapeDtypeStruct((3,) + s.shape[1:], s.dtype) for s in sums]
    return _Carried(list(sums), results, 3 * ne, start, finish)


def _pair_complete(grads):
    ne = len(grads)

    def copy(e, dsts, send_sems, recv_sems, mine):
        x, y, c, _ = _place()
        half = dsts[e].shape[1] // 2
        cols = pl.ds(pl.multiple_of((c if mine else 1 - c) * half, 128), half)
        return pltpu.make_async_remote_copy(
            src_ref=dsts[e].at[:, cols], dst_ref=dsts[e].at[:, cols], send_sem=send_sems.at[e],
            recv_sem=recv_sems.at[e], device_id=(x, y, 1 - c), device_id_type=MESH)

    def start(srcs, dsts, send_sems, recv_sems):
        for e in range(ne):
            copy(e, dsts, send_sems, recv_sems, True).start()

    def finish(srcs, dsts, send_sems, recv_sems):
        for e in range(ne):
            copy(e, dsts, send_sems, recv_sems, True).wait_send()
            copy(e, dsts, send_sems, recv_sems, False).wait_recv()

    results = [jax.ShapeDtypeStruct(g.shape, g.dtype) for g in grads]
    return _Carried(list(grads), results, ne, start, finish, aliases={e: e for e in range(ne)})


def _pair_sums(parts, theirs, cidx, name):
    ne = len(parts)

    def body(c_ref, *refs):
        for e in range(ne):
            refs[2 * ne + e][...] = (refs[e][...].astype(F32) + refs[ne + e][...].astype(F32)).astype(BF16)

    halves = [(p.shape[1], p.shape[2] // 2) for p in parts]
    gs = pltpu.PrefetchScalarGridSpec(
        num_scalar_prefetch=1, grid=(4,),
        in_specs=[pl.BlockSpec((1, r, h), lambda j, c: (j, 0, c[0])) for r, h in halves]
                 + [pl.BlockSpec((1, r, h), lambda j, c: (j, 0, 0)) for r, h in halves],
        out_specs=[pl.BlockSpec((1, r, h), lambda j, c: (j, 0, 0)) for r, h in halves])
    return list(pl.pallas_call(body, name=name, grid_spec=gs, out_shape=[jax.ShapeDtypeStruct((4, r, h), BF16) for r, h in halves],
                               compiler_params=_params("arbitrary"))(cidx, *parts, *theirs))


def _chip_sums(sums, landed, jc, name):
    ne = len(sums)

    def body(jc_ref, *refs):
        for e in range(ne):
            acc = refs[e][0].astype(F32)
            for k in range(3):
                acc = acc + refs[ne + e][k].astype(F32)
            refs[2 * ne + e][...] = acc

    halves = [sm.shape[1:] for sm in sums]
    gs = pltpu.PrefetchScalarGridSpec(
        num_scalar_prefetch=1, grid=(1,),
        in_specs=[pl.BlockSpec((1, r, h), lambda i, jc: (jc[0], 0, 0)) for r, h in halves]
                 + [pl.BlockSpec((3, r, h), lambda i, jc: (0, 0, 0)) for r, h in halves],
        out_specs=[pl.BlockSpec((r, h), lambda i, jc: (0, jc[1])) for r, h in halves])
    return list(pl.pallas_call(body, name=name, grid_spec=gs, out_shape=[jax.ShapeDtypeStruct((r, 2 * h), F32) for r, h in halves],
                               compiler_params=_params("arbitrary"))(jc, *sums, *landed))


def _rope_tables(pos):
    s = pos.shape[0]
    lane = jnp.arange(128)
    inv = 1.0 / (ROPE_THETA ** ((2 * (lane % 32)).astype(F32) / ROPE))
    keep = (lane < 64).astype(F32)
    sign = jnp.where(lane < 32, -1.0, 1.0).astype(F32) * keep
    consts = jnp.stack([inv.astype(F32), keep, sign])[:, None, :]

    def body(p_ref, k_ref, c_ref, s_ref):
        ang = p_ref[...].astype(F32) * k_ref[0]
        c_ref[...] = jnp.cos(ang) * k_ref[1]
        s_ref[...] = jnp.sin(ang) * k_ref[2]

    tm = min(s, 1024)
    return pl.pallas_call(
        body, name="rope_tables", grid=(s // tm,),
        in_specs=[pl.BlockSpec((tm, 1), lambda i: (i, 0)), pl.BlockSpec((3, 1, 128), lambda i: (0, 0, 0))],
        out_specs=[pl.BlockSpec((tm, 128), lambda i: (i, 0))] * 2,
        out_shape=[jax.ShapeDtypeStruct((s, 128), F32)] * 2, compiler_params=_params("arbitrary"),
    )(pos, consts)


def _modulation(c_all, ada_w, ada_b_shard):
    nl, d, n = ada_w.shape
    tn = 512

    def body(c_ref, w_ref, b_ref, o_ref):
        cv = c_ref[...]
        act = (cv * _sigmoid(cv)).astype(BF16)
        o_ref[...] = _dot(act, w_ref[...].astype(BF16)) + b_ref[...]

    return pl.pallas_call(
        body, name="modulation", grid=(nl, n // tn),
        in_specs=[pl.BlockSpec((8, d), lambda l, j: (0, 0)), pl.BlockSpec((None, d, tn), lambda l, j: (l, 0, j)),
                  pl.BlockSpec((None, 1, tn), lambda l, j: (l, 0, j))],
        out_specs=pl.BlockSpec((None, 8, tn), lambda l, j: (l, 0, j)),
        out_shape=jax.ShapeDtypeStruct((nl, 8, n), F32), compiler_params=_params("arbitrary", "arbitrary"),
    )(c_all, ada_w, ada_b_shard)


W_ROWS = 4992
W_PIECES = ((0, 0, 832), (832, 768, 64), (896, 1856, 2048), (2944, 832, 1024), (3968, 3904, 1024))


def _load_w_in(w_hbm, w_vmem, sems):
    cps = [pltpu.make_async_copy(w_hbm.at[pl.ds(src, n)], w_vmem.at[pl.ds(dst, n)], sems.at[i])
           for i, (dst, src, n) in enumerate(W_PIECES)]
    for cp in cps:
        cp.start()
    for cp in cps:
        cp.wait()


def _inproj_fwd(x, g, scale, shift, w_int, layer, name, carried=None):
    s, d = x.shape
    tm = min(s, 256)

    def body(x_ref, g_ref, sc_ref, sh_ref, w_hbm, hb_ref, za_ref, zkr_ref, zb_ref, w_vmem, sems):
        @pl.when(pl.program_id(0) == 0)
        def _():
            _load_w_in(w_hbm.at[layer], w_vmem, sems)

        xv = x_ref[...]
        rstd = lax.rsqrt(jnp.mean(xv * xv, axis=1, keepdims=True) + EPS)
        h = (xv * rstd) * g_ref[...] * (1.0 + sc_ref[...]) + sh_ref[...]
        hb = h.astype(BF16)
        hb_ref[...] = hb
        za_ref[...] = _dot(hb, w_vmem[0:768], NT)
        zkr_ref[...] = _dot(hb, w_vmem[768:896], NT)
        zb_ref[...] = _dot(hb, w_vmem[896:W_ROWS], NT)

    return _call(
        body, (x, g, scale, shift, w_int), name=name, grid=(s // tm,), carried=carried,
        in_specs=[pl.BlockSpec((tm, d), lambda i: (i, 0)), _row(d), _row(d), _row(d), ANY],
        out_specs=[pl.BlockSpec((tm, d), lambda i: (i, 0)), pl.BlockSpec((tm, 768), lambda i: (i, 0)),
                   pl.BlockSpec((tm, 128), lambda i: (i, 0)), pl.BlockSpec((tm, 4096), lambda i: (i, 0))],
        out_shape=[jax.ShapeDtypeStruct((s, d), BF16), jax.ShapeDtypeStruct((s, 768), F32),
                   jax.ShapeDtypeStruct((s, 128), F32), jax.ShapeDtypeStruct((s, 4096), F32)],
        scratch=[pltpu.VMEM((W_ROWS, d), BF16), pltpu.SemaphoreType.DMA((len(W_PIECES),))])


def _rope(yv, cos, sin):
    return yv * cos + pltpu.roll(yv, 32, axis=1) * sin


def _mla_prep_fwd(za, zkr, cos, sin, wq, wkv, gql, gkvl, gq2, gk2, layer, name, carried=None):
    s = za.shape[0]
    tm = min(s, 256)

    def body(za_ref, zkr_ref, cos_ref, sin_ref, wq_ref, wkv_ref, gql_ref, gkvl_ref, gq_ref, gk_ref, q_ref, k_ref, v_ref):
        zq, zkv = za_ref[:, 0:QL], za_ref[:, QL:QL + KVL]
        qn = (zq * lax.rsqrt(jnp.mean(zq * zq, axis=1, keepdims=True) + EPS) * gql_ref[...]).astype(BF16)
        kvn = (zkv * lax.rsqrt(jnp.mean(zkv * zkv, axis=1, keepdims=True) + EPS) * gkvl_ref[...]).astype(BF16)
        kr = zkr_ref[...]
        kr_ss = 0.5 * _sum1(kr * kr)
        cos, sin = cos_ref[...], sin_ref[...]
        gq, gk = gq_ref[...] * SCORE_SCALE, gk_ref[...]
        qr_all, kvr_all = _dot(qn, wq_ref[...]), _dot(kvn, wkv_ref[...])
        for h in range(N_HEADS):
            qr = qr_all[:, h * HP:(h + 1) * HP]
            n, yv = qr[:, :NOPE], qr[:, NOPE:]
            rstd = lax.rsqrt((_sum1(n * n) + 0.5 * _sum1(yv * yv)) * (1.0 / QK) + EPS)
            q_ref[h, :, 0:NOPE] = (n * rstd * gq[:, :NOPE]).astype(BF16)
            q_ref[h, :, NOPE:HP] = _rope(yv * rstd * gq[:, NOPE:], cos, sin).astype(BF16)
            kvr = kvr_all[:, h * HP:(h + 1) * HP]
            kn, vv = kvr[:, :NOPE], kvr[:, NOPE:]
            rstd = lax.rsqrt((_sum1(kn * kn) + kr_ss) * (1.0 / QK) + EPS)
            k_ref[h, :, 0:NOPE] = (kn * rstd * gk[:, :NOPE]).astype(BF16)
            k_ref[h, :, NOPE:HP] = _rope(kr * rstd * gk[:, NOPE:], cos, sin).astype(BF16)
            v_ref[h] = vv.astype(BF16)

    tile = lambda n: pl.BlockSpec((tm, n), lambda i: (i, 0))
    return _call(
        body, (za, zkr, cos, sin, wq, wkv, gql, gkvl, gq2, gk2), name=name, grid=(s // tm,), carried=carried,
        in_specs=[tile(768), tile(128), tile(128), tile(128),
                  pl.BlockSpec((None, QL, N_HEADS * HP), lambda i: (layer, 0, 0)),
                  pl.BlockSpec((None, KVL, N_HEADS * HP), lambda i: (layer, 0, 0)),
                  _row(QL), _row(KVL), _row(HP), _row(HP)],
        out_specs=[pl.BlockSpec((N_HEADS, tm, HP), lambda i: (0, i, 0)), pl.BlockSpec((N_HEADS, tm, HP), lambda i: (0, i, 0)),
                   pl.BlockSpec((N_HEADS, tm, VD), lambda i: (0, i, 0))],
        out_shape=[jax.ShapeDtypeStruct((N_HEADS, s, HP), BF16), jax.ShapeDtypeStruct((N_HEADS, s, HP), BF16),
                   jax.ShapeDtypeStruct((N_HEADS, s, VD), BF16)])


SCORE_SCALE = 1.0 / math.sqrt(QK)
MASKED = -1e30


def _flash_fwd(q, k, v, name, carried=None):
    s = q.shape[1]
    t = min(s, 1024)

    def body(q_ref, k_ref, v_ref, o_ref, lse_ref):
        i = pl.program_id(1)
        qb = q_ref[...]
        row = lax.broadcasted_iota(jnp.int32, (t, t), 0)
        col = lax.broadcasted_iota(jnp.int32, (t, t), 1)

        def block(j):
            return pl.ds(pl.multiple_of(j * t, t), t)

        def scores(j):
            return _dot(qb, k_ref[block(j), :], NT)

        def update(j, sc, m, l, acc, diagonal):
            if diagonal:
                sc = jnp.where(col <= row, sc, MASKED)
            m_new = jnp.maximum(m, jnp.max(sc, axis=1, keepdims=True))
            p = jnp.exp(sc - m_new)
            alpha = jnp.exp(m - m_new)
            return m_new, alpha * l + _sum1(p), alpha * acc + _dot(p.astype(BF16), v_ref[block(j), :])

        init = (jnp.full((t, 1), MASKED, F32), jnp.zeros((t, 1), F32), jnp.zeros((t, VD), F32))
        carry = lax.fori_loop(0, i, lambda j, cr: update(j, scores(j), *cr, False), init)
        m, l, acc = update(i, scores(i), *carry, True)
        o_ref[...] = acc / l
        lse_ref[...] = m + jnp.log(l)

    return _call(
        body, (q, k, v), name=name, grid=(N_HEADS, s // t), carried=carried,
        in_specs=[pl.BlockSpec((None, t, HP), lambda h, i: (h, i, 0)), pl.BlockSpec((None, s, HP), lambda h, i: (h, 0, 0)),
                  pl.BlockSpec((None, s, VD), lambda h, i: (h, 0, 0))],
        out_specs=[pl.BlockSpec((t, VD), lambda h, i: (i, h)), pl.BlockSpec((None, t, 1), lambda h, i: (h, i, 0))],
        out_shape=[jax.ShapeDtypeStruct((s, N_HEADS * VD), F32), jax.ShapeDtypeStruct((N_HEADS, s, 1), F32)])


CH, RC = 128, 64
DW_SHARD = 256


PH_ROWS_LESS = 8


def _glu(val, gate, bias):
    c = val.shape[1]
    return (val + bias[:, :c]) * _sigmoid(gate + bias[:, c:])


def _make_phases(buf, phases, cc):
    rows = buf.shape[0] - PH_ROWS_LESS
    for b in range(1, 8):
        phases[b - 1] = buf[pl.ds(b, rows), cc:cc + CH]


def _window(buf, phases, cc, shift, r0):
    a, b = divmod(shift, 8)
    if b == 0:
        return buf[r0 + 8 * a:r0 + 8 * a + RC, cc:cc + CH]
    return phases[b - 1, r0 + 8 * a:r0 + 8 * a + RC, :]


def _conv_fwd(zb, glu_b, dw, dwb, lng, lnb, wpw, bpw, layer, name, carried=None):
    s = zb.shape[0]
    dc = dwb.shape[1]
    tm = min(s, 256)
    hb = tm // HALO

    def body(val_ref, gate_ref, valh_ref, gateh_ref, glub_ref, dw_ref, dwb_ref, lng_ref, lnb_ref, wpw_ref, bpw_ref,
             cv_ref, pw_ref, ubuf, uph):
        i = pl.program_id(0)
        bias = glub_ref[...]
        ubuf[HALO:, :] = _glu(val_ref[...], gate_ref[...], bias)
        uh = _glu(valh_ref[...], gateh_ref[...], bias)
        ubuf[0:HALO, :] = jnp.where(i > 0, uh, 0.0)
        for cc in range(0, dc, CH):
            _make_phases(ubuf, uph, cc)
            for r0 in range(0, tm, RC):
                acc = jnp.zeros((RC, CH), F32)
                for j in range(CONV_K):
                    acc = acc + _window(ubuf, uph, cc, HALO - (CONV_K - 1) + j, r0) * dw_ref[j:j + 1, cc:cc + CH]
                cv_ref[r0:r0 + RC, cc:cc + CH] = acc + dwb_ref[:, cc:cc + CH]
        cv = cv_ref[...]
        dv = cv - jnp.mean(cv, axis=1, keepdims=True)
        yl = dv * lax.rsqrt(jnp.mean(dv * dv, axis=1, keepdims=True) + EPS) * lng_ref[...] + lnb_ref[...]
        act = (yl * _sigmoid(yl)).astype(BF16)
        pw_ref[...] = _dot(act, wpw_ref[...]) + bpw_ref[...]

    return _call(
        body, (zb, zb, zb, zb, glu_b, dw, dwb, lng, lnb, wpw, bpw), name=name, grid=(s // tm,), carried=carried,
        in_specs=[pl.BlockSpec((tm, dc), lambda i: (i, 0)), pl.BlockSpec((tm, dc), lambda i: (i, 1)),
                  pl.BlockSpec((HALO, dc), lambda i: (jnp.maximum(i * hb - 1, 0), 0)),
                  pl.BlockSpec((HALO, dc), lambda i: (jnp.maximum(i * hb - 1, 0), 1)),
                  _row(2 * dc), pl.BlockSpec((None, HALO, dc), lambda i: (layer, 0, 0)), _row(dc), _row(dc), _row(dc),
                  pl.BlockSpec((None, dc, dc), lambda i: (layer, 0, 0)), _row(dc)],
        out_specs=[pl.BlockSpec((tm, dc), lambda i: (i, 0))] * 2,
        out_shape=[jax.ShapeDtypeStruct((s, dc), F32)] * 2,
        scratch=[pltpu.VMEM((tm + HALO, dc), F32), pltpu.VMEM((7, tm + HALO - PH_ROWS_LESS, CH), F32)])


def _silu_parts(z):
    sg = _sigmoid(z)
    return z * sg, sg * (1.0 + z * (1.0 - sg))


def _outproj_fwd(x, o, zb, pw, gate, wout, layer, name, carried=None, target=None):
    s, d = x.shape
    dm = o.shape[1]
    tm = min(s, 256)

    def project(x_ref, o_ref, mg_ref, cg_ref, pw_ref, gate_ref, w_ref, mix_ref):
        mg, cg = mg_ref[...], cg_ref[...]
        mix_ref[:, 0:dm] = (o_ref[...] * (mg * _sigmoid(mg))).astype(BF16)
        mix_ref[:, dm:] = (pw_ref[...] * (cg * _sigmoid(cg))).astype(BF16)
        return x_ref[...] + gate_ref[...] * _dot(mix_ref[...], w_ref[...])

    def body(*refs):
        xn_ref, mix_ref = refs[7:]
        xn_ref[...] = project(*refs[:7], mix_ref)

    def body_with_loss(*refs):
        t_ref, l_ref, dx_ref, mix_ref = refs[7:]
        err = project(*refs[:7], mix_ref) - t_ref[...]
        l_ref[...] = 0.5 * jnp.mean(err * err, axis=1, keepdims=True)
        dx_ref[...] = err * (1.0 / d)

    tile = lambda n, j=0: pl.BlockSpec((tm, n), lambda i: (i, j))
    in_specs = [tile(d), tile(dm), tile(dm, 2), tile(dm, 3), tile(dm), _row(d), pl.BlockSpec((None, 2 * dm, d), lambda i: (layer, 0, 0))]
    if target is None:
        return _call(body, (x, o, zb, zb, pw, gate, wout), name=name, grid=(s // tm,), carried=carried, in_specs=in_specs,
                     out_specs=[tile(d), tile(2 * dm)],
                     out_shape=[jax.ShapeDtypeStruct((s, d), F32), jax.ShapeDtypeStruct((s, 2 * dm), BF16)])
    return _call(body_with_loss, (x, o, zb, zb, pw, gate, wout, target), name=name, grid=(s // tm,), carried=carried,
                 in_specs=in_specs + [tile(d)], out_specs=[tile(1), tile(d), tile(2 * dm)],
                 out_shape=[jax.ShapeDtypeStruct((s, 1), F32), jax.ShapeDtypeStruct((s, d), F32), jax.ShapeDtypeStruct((s, 2 * dm), BF16)])


def _grad_tn(a, b, name, carried=None):
    s, n = a.shape
    m = b.shape[1]
    tn, ts = min(n, 1024), min(s, 512)
    nt = s // ts

    def body(a_ref, b_ref, o_ref, acc):
        t = pl.program_id(1)

        @pl.when(t == 0)
        def _():
            acc[...] = jnp.zeros_like(acc)

        acc[...] += _dot(a_ref[...].astype(BF16), b_ref[...].astype(BF16), TN)

        @pl.when(t == nt - 1)
        def _():
            o_ref[...] = acc[...].astype(BF16)

    return _call(
        body, (a, b), name=name, grid=(n // tn, nt), carried=carried,
        in_specs=[pl.BlockSpec((ts, tn), lambda r, t: (t, r)), pl.BlockSpec((ts, m), lambda r, t: (t, 0))],
        out_specs=[pl.BlockSpec((tn, m), lambda r, t: (r, 0))], out_shape=[jax.ShapeDtypeStruct((n, m), BF16)],
        scratch=[pltpu.VMEM((tn, m), F32)])


def _grad_w_out(mixb, dxb, gate, wout, layer, name, carried=None):
    s, n = mixb.shape
    d = dxb.shape[1]
    tn, ts = min(n, 1024), min(s, 512)
    nt = s // ts

    def body(a_ref, b_ref, gate_ref, w_ref, g_ref, dgate_ref, acc):
        t = pl.program_id(1)

        @pl.when(t == 0)
        def _():
            acc[...] = jnp.zeros_like(acc)

        acc[...] += _dot(a_ref[...], b_ref[...], TN)

        @pl.when(t == nt - 1)
        def _():
            m = acc[...]
            dgate_ref[...] = _sum0(m * w_ref[...].astype(F32))
            g_ref[...] = (m * gate_ref[...]).astype(BF16)

    return _call(
        body, (mixb, dxb, gate, wout), name=name, grid=(n // tn, nt), carried=carried,
        in_specs=[pl.BlockSpec((ts, tn), lambda r, t: (t, r)), pl.BlockSpec((ts, d), lambda r, t: (t, 0)), _row(d),
                  pl.BlockSpec((None, tn, d), lambda r, t: (layer, r, 0))],
        out_specs=[pl.BlockSpec((tn, d), lambda r, t: (r, 0)), pl.BlockSpec((None, 1, d), lambda r, t: (r, 0, 0))],
        out_shape=[jax.ShapeDtypeStruct((n, d), BF16), jax.ShapeDtypeStruct((n // tn, 1, d), F32)],
        scratch=[pltpu.VMEM((tn, d), F32)])


def _outproj_bwd(dxo, gate, wout, o, zb, pw, layer, name, carried=None):
    s, d = dxo.shape
    dm = o.shape[1]
    tm = min(s, 256)

    def body(dx_ref, gate_ref, w_ref, o_ref, mg_ref, cg_ref, pw_ref, dxb_ref, do_ref, delta_ref, dzb_ref, dpw_ref):
        dx = dx_ref[...]
        dxb_ref[...] = dx.astype(BF16)
        dmix = _dot((dx * gate_ref[...]).astype(BF16), w_ref[...], NT)
        da, db = dmix[:, :dm], dmix[:, dm:]
        ov = o_ref[...]
        silu_m, dsilu_m = _silu_parts(mg_ref[...])
        do = da * silu_m
        do_ref[...] = do.astype(BF16)
        prod = do * ov
        for h in range(N_HEADS):
            delta_ref[h] = _sum1(prod[:, h * VD:(h + 1) * VD])
        dzb_ref[:, 0:dm] = da * ov * dsilu_m
        silu_c, dsilu_c = _silu_parts(cg_ref[...])
        dpw_ref[...] = db * silu_c
        dzb_ref[:, dm:] = db * pw_ref[...] * dsilu_c

    tile = lambda n, j=0: pl.BlockSpec((tm, n), lambda i: (i, j))
    return _call(
        body, (dxo, gate, wout, o, zb, zb, pw), name=name, grid=(s // tm,), carried=carried,
        in_specs=[tile(d), _row(d), pl.BlockSpec((None, 2 * dm, d), lambda i: (layer, 0, 0)),
                  tile(dm), tile(dm, 2), tile(dm, 3), tile(dm)],
        out_specs=[tile(d), tile(dm), pl.BlockSpec((N_HEADS, tm, 1), lambda i: (0, i, 0)), tile(2 * dm, 1), tile(dm)],
        out_shape=[jax.ShapeDtypeStruct((s, d), BF16), jax.ShapeDtypeStruct((s, dm), BF16),
                   jax.ShapeDtypeStruct((N_HEADS, s, 1), F32), jax.ShapeDtypeStruct((s, 4 * dm), F32),
                   jax.ShapeDtypeStruct((s, dm), F32)])


def _flash_bwd(q, k, v, do, lse, delta, name, carried=None):
    s = q.shape[1]
    t = min(s, 1024)
    nq = s // t

    def body(q_ref, k_ref, v_ref, do_ref, lse_ref, delta_ref, dq_ref, dk_ref, dv_ref):
        j = pl.program_id(1)

        @pl.when(j == 0)
        def _():
            dq_ref[...] = jnp.zeros_like(dq_ref)

        kb, vb = k_ref[...], v_ref[...]
        row = lax.broadcasted_iota(jnp.int32, (t, t), 0)
        col = lax.broadcasted_iota(jnp.int32, (t, t), 1)

        def block(i):
            return pl.ds(pl.multiple_of(i * t, t), t)

        def scores(i):
            at = block(i)
            return _dot(q_ref[at, :], kb, NT), _dot(do_ref[at, :], vb, NT)

        def update(i, sc, dp, dk, dv, diagonal):
            at = block(i)
            p = jnp.exp(sc - lse_ref[at, :])
            if diagonal:
                p = jnp.where(col <= row, p, 0.0)
            dv = dv + _dot(p.astype(BF16), do_ref[at, :], TN)
            ds = (p * (dp - delta_ref[at, :])).astype(BF16)
            dq_ref[at, :] += _dot(ds, kb)
            return dk + _dot(ds, q_ref[at, :], TN), dv

        carry = update(j, *scores(j), jnp.zeros((t, HP), F32), jnp.zeros((t, VD), F32), True)
        dk, dv = lax.fori_loop(j + 1, nq, lambda i, cr: update(i, *scores(i), *cr, False), carry)
        dk_ref[...] = dk
        dv_ref[...] = dv.astype(BF16)

    whole = lambda n: pl.BlockSpec((None, s, n), lambda h, j: (h, 0, 0))
    blk = lambda n: pl.BlockSpec((None, t, n), lambda h, j: (h, j, 0))
    return _call(
        body, (q, k, v, do, lse, delta), name=name, grid=(N_HEADS, nq), carried=carried,
        in_specs=[whole(HP), blk(HP), blk(VD), pl.BlockSpec((s, VD), lambda h, j: (0, h)), whole(1), whole(1)],
        out_specs=[whole(HP), blk(HP), blk(VD)],
        out_shape=[jax.ShapeDtypeStruct((N_HEADS, s, HP), F32), jax.ShapeDtypeStruct((N_HEADS, s, HP), F32),
                   jax.ShapeDtypeStruct((N_HEADS, s, VD), BF16)])


def _mla_prep_bwd(dq, dk, dv, za, zkr, cos, sin, wq, wkv, gql, gkvl, gq2, gk2, layer, name):
    s = za.shape[0]
    tm = min(s, 512)

    def norm_bwd(n, yv, rstd, gain, d_n_out, d_y_out):
        tn_, ty = n * rstd, yv * rstd
        dgain_n, dgain_y = _sum0(d_n_out * tn_), _sum0(d_y_out * ty)
        dtn, dty = d_n_out * gain[:, :NOPE], d_y_out * gain[:, NOPE:]
        a = (_sum1(dtn * n) + _sum1(dty * yv)) * (rstd * rstd * rstd * (1.0 / QK))
        return rstd * dtn - n * a, rstd * dty - (0.5 * yv) * a, dgain_n, dgain_y

    def rope_bwd(d_out, cos, sin):
        return d_out * cos + pltpu.roll(d_out * sin, 128 - 32, axis=1)

    def latent_bwd(z, gain, dn):
        rstd = lax.rsqrt(jnp.mean(z * z, axis=1, keepdims=True) + EPS)
        zh = z * rstd
        dzh = dn * gain
        return rstd * (dzh - zh * jnp.mean(dzh * zh, axis=1, keepdims=True)), _sum0(dn * zh)

    def body(dq_ref, dk_ref, dv_ref, za_ref, zkr_ref, cos_ref, sin_ref, wq_ref, wkv_ref, gql_ref, gkvl_ref, gq_ref, gk_ref,
             dza_ref, dzkr_ref, gwq_ref, gwkv_ref, dgql_ref, dgkvl_ref, dgq_ref, dgk_ref, gwq_acc, gwkv_acc):
        @pl.when(pl.program_id(0) == 0)
        def _():
            for r in (gwq_acc, gwkv_acc, dgql_ref, dgkvl_ref, dgq_ref, dgk_ref):
                r[...] = jnp.zeros_like(r)

        zq, zkv = za_ref[:, 0:QL], za_ref[:, QL:QL + KVL]
        qf = zq * lax.rsqrt(jnp.mean(zq * zq, axis=1, keepdims=True) + EPS) * gql_ref[...]
        kvf = zkv * lax.rsqrt(jnp.mean(zkv * zkv, axis=1, keepdims=True) + EPS) * gkvl_ref[...]
        qn, kvn = qf.astype(BF16), kvf.astype(BF16)
        qn_t, kvn_t = qf.T.astype(BF16), kvf.T.astype(BF16)
        kr = zkr_ref[...]
        kr_ss = 0.5 * _sum1(kr * kr)
        cos, sin = cos_ref[...], sin_ref[...]
        gq, gk = gq_ref[...], gk_ref[...]
        dkr = jnp.zeros((tm, 128), F32)
        qr_all, kvr_all = _dot(qn, wq_ref[...]), _dot(kvn, wkv_ref[...])
        dqr_all, dkvr_all = [], []
        for h in range(N_HEADS):
            qr = qr_all[:, h * HP:(h + 1) * HP]
            n, yv = qr[:, :NOPE], qr[:, NOPE:]
            rstd = lax.rsqrt((_sum1(n * n) + 0.5 * _sum1(yv * yv)) * (1.0 / QK) + EPS)
            dqh = dq_ref[h] * SCORE_SCALE
            dn, dy, dg_n, dg_y = norm_bwd(n, yv, rstd, gq, dqh[:, :NOPE], rope_bwd(dqh[:, NOPE:], cos, sin))
            dgq_ref[:, 0:NOPE] += dg_n
            dgq_ref[:, NOPE:] += dg_y
            dqr_all += [dn.astype(BF16), dy.astype(BF16)]
            kn = kvr_all[:, h * HP:h * HP + NOPE]
            rstd = lax.rsqrt((_sum1(kn * kn) + kr_ss) * (1.0 / QK) + EPS)
            dkh = dk_ref[h]
            dn, dy, dg_n, dg_y = norm_bwd(kn, kr, rstd, gk, dkh[:, :NOPE], rope_bwd(dkh[:, NOPE:], cos, sin))
            dgk_ref[:, 0:NOPE] += dg_n
            dgk_ref[:, NOPE:] += dg_y
            dkr = dkr + dy
            dkvr_all += [dn.astype(BF16), dv_ref[h].astype(BF16)]

        dqr_all, dkvr_all = jnp.concatenate(dqr_all, axis=1), jnp.concatenate(dkvr_all, axis=1)
        gwq_acc[...] += _dot(qn_t, dqr_all)
        gwkv_acc[...] += _dot(kvn_t, dkvr_all)

        @pl.when(pl.program_id(0) == s // tm - 1)
        def _():
            gwq_ref[...] = gwq_acc[...].astype(BF16)
            gwkv_ref[...] = gwkv_acc[...].astype(BF16)

        dzq, dgql = latent_bwd(zq, gql_ref[...], _dot(dqr_all, wq_ref[...], NT))
        dzkv, dgkvl = latent_bwd(zkv, gkvl_ref[...], _dot(dkvr_all, wkv_ref[...], NT))
        dgql_ref[...] += dgql
        dgkvl_ref[...] += dgkvl
        dza_ref[:, 0:QL] = dzq
        dza_ref[:, QL:] = dzkv
        lane = lax.broadcasted_iota(jnp.int32, (tm, 128), 1)
        dzkr_ref[...] = jnp.where(lane < ROPE, dkr + pltpu.roll(dkr, 64, axis=1), 0.0)

    tile = lambda n: pl.BlockSpec((tm, n), lambda i: (i, 0))
    heads = lambda n: pl.BlockSpec((N_HEADS, tm, n), lambda i: (0, i, 0))
    return pl.pallas_call(
        body, name=name, grid=(s // tm,),
        in_specs=[heads(HP), heads(HP), heads(VD), tile(768), tile(128), tile(128), tile(128),
                  pl.BlockSpec((None, QL, N_HEADS * HP), lambda i: (layer, 0, 0)),
                  pl.BlockSpec((None, KVL, N_HEADS * HP), lambda i: (layer, 0, 0)),
                  _row(QL), _row(KVL), _row(HP), _row(HP)],
        out_specs=[tile(768), tile(128), pl.BlockSpec((QL, N_HEADS * HP), lambda i: (0, 0)),
                   pl.BlockSpec((KVL, N_HEADS * HP), lambda i: (0, 0)), _row(QL), _row(KVL), _row(HP), _row(HP)],
        out_shape=[jax.ShapeDtypeStruct((s, 768), F32), jax.ShapeDtypeStruct((s, 128), F32),
                   jax.ShapeDtypeStruct((QL, N_HEADS * HP), BF16), jax.ShapeDtypeStruct((KVL, N_HEADS * HP), BF16),
                   jax.ShapeDtypeStruct((1, QL), F32), jax.ShapeDtypeStruct((1, KVL), F32),
                   jax.ShapeDtypeStruct((1, HP), F32), jax.ShapeDtypeStruct((1, HP), F32)],
        scratch_shapes=[pltpu.VMEM((QL, N_HEADS * HP), F32), pltpu.VMEM((KVL, N_HEADS * HP), F32)],
        compiler_params=_params("arbitrary"),
    )(dq, dk, dv, za, zkr, cos, sin, wq, wkv, gql, gkvl, gq2, gk2)


def _pointwise_bwd(dpw, cv, lng, lnb, wpw, layer, name):
    s, dc = cv.shape
    tm = min(s, 256)

    def body(dpw_ref, cv_ref, lng_ref, lnb_ref, w_ref, dcv_ref, act_ref, dbpw_ref, dlng_ref, dlnb_ref):
        @pl.when(pl.program_id(0) == 0)
        def _():
            for r in (dbpw_ref, dlng_ref, dlnb_ref):
                r[...] = jnp.zeros_like(r)

        cv = cv_ref[...]
        dv = cv - jnp.mean(cv, axis=1, keepdims=True)
        rstd = lax.rsqrt(jnp.mean(dv * dv, axis=1, keepdims=True) + EPS)
        xh = dv * rstd
        yl = xh * lng_ref[...] + lnb_ref[...]
        silu, dsilu = _silu_parts(yl)
        act_ref[...] = silu.astype(BF16)
        dpw = dpw_ref[...]
        dbpw_ref[...] += _sum0(dpw)
        dyl = _dot(dpw.astype(BF16), w_ref[...], NT) * dsilu
        dlng_ref[...] += _sum0(dyl * xh)
        dlnb_ref[...] += _sum0(dyl)
        dxh = dyl * lng_ref[...]
        dcv_ref[...] = rstd * (dxh - jnp.mean(dxh, axis=1, keepdims=True) - xh * jnp.mean(dxh * xh, axis=1, keepdims=True))

    tile = pl.BlockSpec((tm, dc), lambda i: (i, 0))
    return pl.pallas_call(
        body, name=name, grid=(s // tm,),
        in_specs=[tile, tile, _row(dc), _row(dc), pl.BlockSpec((None, dc, dc), lambda i: (layer, 0, 0))],
        out_specs=[tile, tile, _row(dc), _row(dc), _row(dc)],
        out_shape=[jax.ShapeDtypeStruct((s, dc), F32), jax.ShapeDtypeStruct((s, dc), BF16)] + [jax.ShapeDtypeStruct((1, dc), F32)] * 3,
        compiler_params=_params("arbitrary"),
    )(dpw, cv, lng, lnb, wpw)


def _conv_bwd(dcv, zb, dzb, glu_b, dw, layer, name, carried=None):
    s, dc = dcv.shape
    tm = min(s, 256)
    hb = tm // HALO
    last = s // tm - 1

    def body(dcv_ref, dcvn_ref, val_ref, gate_ref, valh_ref, gateh_ref, glub_ref, dw_ref, _, dzb_ref, gdw_ref, ddwb_ref, dglub_ref,
             ubuf, dbuf, gacc, uph, dph):
        i = pl.program_id(0)

        @pl.when(i == 0)
        def _():
            gacc[...] = jnp.zeros_like(gacc)
            ddwb_ref[...] = jnp.zeros_like(ddwb_ref)
            dglub_ref[...] = jnp.zeros_like(dglub_ref)

        bias = glub_ref[...]
        ubuf[HALO:, :] = _glu(val_ref[...], gate_ref[...], bias)
        ubuf[0:HALO, :] = jnp.where(i > 0, _glu(valh_ref[...], gateh_ref[...], bias), 0.0)
        dcv = dcv_ref[...]
        dbuf[0:tm, :] = dcv
        dbuf[tm:, :] = jnp.where(i < last, dcvn_ref[...], 0.0)
        ddwb_ref[...] += _sum0(dcv)
        for cc in range(0, dc, CH):
            _make_phases(ubuf, uph, cc)
            _make_phases(dbuf, dph, cc)
            for r0 in range(0, tm, RC):
                du = jnp.zeros((RC, CH), F32)
                dpiece = dbuf[r0:r0 + RC, cc:cc + CH]
                for j in range(CONV_K):
                    du = du + _window(dbuf, dph, cc, (CONV_K - 1) - j, r0) * dw_ref[j:j + 1, cc:cc + CH]
                    win = _window(ubuf, uph, cc, HALO - (CONV_K - 1) + j, r0)
                    gacc[j, :, cc:cc + CH] += (dpiece * win).reshape(RC // 8, 8, CH).sum(axis=0)
                a = val_ref[r0:r0 + RC, cc:cc + CH] + bias[:, cc:cc + CH]
                sg = _sigmoid(gate_ref[r0:r0 + RC, cc:cc + CH] + bias[:, dc + cc:dc + cc + CH])
                dzb_ref[r0:r0 + RC, cc:cc + CH] = du * sg
                dzb_ref[r0:r0 + RC, dc + cc:dc + cc + CH] = du * a * sg * (1.0 - sg)
        dglub_ref[...] += _sum0(dzb_ref[...])

        @pl.when(i == last)
        def _():
            total = jnp.sum(gacc[...], axis=1)
            for cc in range(0, dc, DW_SHARD):
                gdw_ref[cc // DW_SHARD] = total[:, cc:cc + DW_SHARD]

    return _call(
        body, (dcv, dcv, zb, zb, zb, zb, glu_b, dw, dzb), name=name, grid=(s // tm,), carried=carried, aliases={8: 0},
        in_specs=[pl.BlockSpec((tm, dc), lambda i: (i, 0)),
                  pl.BlockSpec((HALO, dc), lambda i: (jnp.minimum((i + 1) * hb, s // HALO - 1), 0)),
                  pl.BlockSpec((tm, dc), lambda i: (i, 0)), pl.BlockSpec((tm, dc), lambda i: (i, 1)),
                  pl.BlockSpec((HALO, dc), lambda i: (jnp.maximum(i * hb - 1, 0), 0)),
                  pl.BlockSpec((HALO, dc), lambda i: (jnp.maximum(i * hb - 1, 0), 1)),
                  _row(2 * dc), pl.BlockSpec((None, HALO, dc), lambda i: (layer, 0, 0)), ANY],
        out_specs=[pl.BlockSpec((tm, 2 * dc), lambda i: (i, 0)), pl.BlockSpec((4, HALO, DW_SHARD), lambda i: (0, 0, 0)),
                   _row(dc), _row(2 * dc)],
        out_shape=[jax.ShapeDtypeStruct(dzb.shape, F32), jax.ShapeDtypeStruct((4, HALO, DW_SHARD), F32),
                   jax.ShapeDtypeStruct((1, dc), F32), jax.ShapeDtypeStruct((1, 2 * dc), F32)],
        scratch=[pltpu.VMEM((tm + HALO, dc), F32), pltpu.VMEM((tm + HALO, dc), F32), pltpu.VMEM((HALO, 8, dc), F32),
                 pltpu.VMEM((7, tm + HALO - PH_ROWS_LESS, CH), F32), pltpu.VMEM((7, tm + HALO - PH_ROWS_LESS, CH), F32)])


def _inproj_bwd(dza, dzkr, dzb, x, dxo, g, scale, shift, w_int, layer, name, carried=None):
    s, d = x.shape
    tm = min(s, 128)

    def body(dza_ref, dzkr_ref, dzb_ref, x_ref, dxo_ref, g_ref, sc_ref, sh_ref, w_hbm, dx_ref, dsh_ref, dgg_ref, w_vmem, sems):
        @pl.when(pl.program_id(0) == 0)
        def _():
            _load_w_in(w_hbm.at[layer], w_vmem, sems)
            dsh_ref[...] = jnp.zeros_like(dsh_ref)
            dgg_ref[...] = jnp.zeros_like(dgg_ref)

        dh = _dot(dza_ref[...].astype(BF16), w_vmem[0:768])
        dh = dh + _dot(dzkr_ref[...].astype(BF16), w_vmem[768:896])
        dh = dh + _dot(dzb_ref[...].astype(BF16), w_vmem[896:W_ROWS])
        xv = x_ref[...]
        rstd = lax.rsqrt(jnp.mean(xv * xv, axis=1, keepdims=True) + EPS)
        xh = xv * rstd
        dsh_ref[...] += _sum0(dh)
        dgg_ref[...] += _sum0(dh * xh)
        dxh = dh * (g_ref[...] * (1.0 + sc_ref[...]))
        dx_ref[...] = dxo_ref[...] + rstd * (dxh - xh * jnp.mean(dxh * xh, axis=1, keepdims=True))

    tile = lambda n: pl.BlockSpec((tm, n), lambda i: (i, 0))
    return _call(
        body, (dza, dzkr, dzb, x, dxo, g, scale, shift, w_int), name=name, grid=(s // tm,), carried=carried,
        in_specs=[tile(768), tile(128), tile(4096), tile(d), tile(d), _row(d), _row(d), _row(d), ANY],
        out_specs=[tile(d), _row(d), _row(d)],
        out_shape=[jax.ShapeDtypeStruct((s, d), F32), jax.ShapeDtypeStruct((1, d), F32), jax.ShapeDtypeStruct((1, d), F32)],
        scratch=[pltpu.VMEM((W_ROWS, d), BF16), pltpu.SemaphoreType.DMA((len(W_PIECES),))])


def _grad_w_in(dza, dzkr, dzb, hb, name, carried=None):
    s, d = hb.shape
    ts = min(s, 512)
    nt = s // ts
    tiles = ((0, 768), (768, 64), (1856, 1024), (2880, 1024), (832, 1024), (3904, 1024))

    def body(a_ref, kr_ref, b_ref, h_ref, o_hbm, acc, rounded, sem):
        r, t = pl.program_id(0), pl.program_id(1)

        @pl.when(t == 0)
        def _():
            acc[...] = jnp.zeros_like(acc)

        hv = h_ref[...]

        @pl.when(r == 0)
        def _():
            acc[0:768, :] += _dot(a_ref[...].astype(BF16), hv, TN)

        @pl.when(r == 1)
        def _():
            acc[0:128, :] += _dot(kr_ref[...].astype(BF16), hv, TN)

        @pl.when(r >= 2)
        def _():
            acc[...] += _dot(b_ref[...].astype(BF16), hv, TN)

        for tile, (row0, rows) in enumerate(tiles):
            @pl.when((t == nt - 1) & (r == tile))
            def _():
                rounded[0:rows, :] = acc[0:rows, :].astype(BF16)
                cp = pltpu.make_async_copy(rounded.at[pl.ds(0, rows)], o_hbm.at[pl.ds(row0, rows)], sem)
                cp.start()
                cp.wait()

    return _call(
        body, (dza, dzkr, dzb, hb), name=name, grid=(len(tiles), nt), carried=carried,
        in_specs=[pl.BlockSpec((ts, 768), lambda r, t: (jnp.where(r == 0, t, nt - 1), 0)),
                  pl.BlockSpec((ts, 128), lambda r, t: (jnp.where(r == 1, t, jnp.where(r == 0, 0, nt - 1)), 0)),
                  pl.BlockSpec((ts, 1024), lambda r, t: (jnp.where(r >= 2, t, 0), jnp.maximum(r - 2, 0))),
                  pl.BlockSpec((ts, d), lambda r, t: (t, 0))],
        out_specs=[ANY], out_shape=[jax.ShapeDtypeStruct((4928, d), BF16)],
        scratch=[pltpu.VMEM((1024, d), F32), pltpu.VMEM((1024, d), BF16), pltpu.SemaphoreType.DMA])


def _adamw(w, g, m, v):
    m = ADAM_B1 * m + (1.0 - ADAM_B1) * g
    v = ADAM_B2 * v + (1.0 - ADAM_B2) * (g * g)
    m_hat = m / (1.0 - ADAM_B1 ** ADAM_STEP)
    v_hat = v / (1.0 - ADAM_B2 ** ADAM_STEP)
    return -ADAM_LR * (m_hat / (jnp.sqrt(v_hat) + ADAM_EPS) + ADAM_WD * w), m, v


def _adam_update(w, g0, g1, m, v, name):
    _, r, c = w.shape
    fits = [t for t in range(8, r + 1, 8) if r % t == 0 and t * c * 4 <= (1 << 21)]
    tr = max(fits) if fits else r

    def body(w_ref, g0_ref, g1_ref, m_ref, v_ref, g_ref, d_ref, mo_ref, vo_ref):
        g = jnp.where(pl.program_id(0) == 0, g0_ref[...], g1_ref[...])
        g_ref[...] = g
        d_ref[...], mo_ref[...], vo_ref[...] = _adamw(w_ref[...], g, m_ref[...], v_ref[...])

    nt = r // tr
    big = pl.BlockSpec((None, tr, c), lambda l, i: (l, i, 0))
    g_of = lambda layer, idle: pl.BlockSpec((tr, c), lambda l, i: (jnp.where(l == layer, i, idle), 0))
    return pl.pallas_call(
        body, name=name, grid=(2, nt), in_specs=[big, g_of(0, nt - 1), g_of(1, 0), big, big], out_specs=[big] * 4,
        out_shape=[jax.ShapeDtypeStruct(w.shape, F32)] * 4, compiler_params=_params("arbitrary", "arbitrary"),
    )(w, g0, g1, m, v)


def _ada_update(c_all, dmod, w, m, v, carried=None):
    nl, d, n = w.shape
    tr = 256

    def body(c_ref, dm_ref, w_ref, m_ref, v_ref, g_ref, d_ref, mo_ref, vo_ref):
        cv = c_ref[...]
        act = (cv * _sigmoid(cv)).astype(BF16)
        g = _dot(act, dm_ref[...].astype(BF16), TN)
        g_ref[...] = g
        d_ref[...], mo_ref[...], vo_ref[...] = _adamw(w_ref[...], g, m_ref[...], v_ref[...])

    big = pl.BlockSpec((None, tr, n), lambda l, i: (l, i, 0))
    return _call(
        body, (c_all, dmod, w, m, v), name="ada_w_update", grid=(nl, d // tr), carried=carried,
        in_specs=[pl.BlockSpec((8, tr), lambda l, i: (0, i)), pl.BlockSpec((None, 8, n), lambda l, i: (l, 0, 0)), big, big, big],
        out_specs=[big] * 4, out_shape=[jax.ShapeDtypeStruct(w.shape, F32)] * 4)


def _small_update(gathered, w, m, v):
    r = w.shape[0]

    def body(ga_ref, w_ref, m_ref, v_ref, g_ref, d_ref, mo_ref, vo_ref):
        g = ga_ref[0]
        for dev in range(1, 8):
            g = g + ga_ref[dev]
        g_ref[...] = g
        d_ref[...], mo_ref[...], vo_ref[...] = _adamw(w_ref[...], g, m_ref[...], v_ref[...])

    return pl.pallas_call(body, name="small_update", out_shape=[jax.ShapeDtypeStruct((r, 128), F32)] * 4,
                          compiler_params=_params())(gathered, w, m, v)


SMALL = (("ada_b", 6144), ("norm_g", 2048), ("q_lat_g", 512), ("kv_lat_g", 256), ("q_norm_g", 256), ("k_norm_g", 256),
         ("glu_b", 2048), ("dw_b", 1024), ("conv_ln_g", 1024), ("conv_ln_b", 1024), ("b_pw", 1024))


def _pack_small(vals):
    cols = []
    for name, width in SMALL:
        a = vals[name]
        if a.shape[1] < width:
            a = jnp.pad(a, ((0, 0), (0, width - a.shape[1])))
        cols.append(a)
    return jnp.concatenate(cols, axis=1).reshape(-1, 128)


def _unpack_small(packed, shapes):
    flat = packed.reshape(2, -1)
    out, at = {}, 0
    for name, width in SMALL:
        out[name] = flat[:, at:at + shapes[name]]
        at += width
    return out


def _dup_gain(g):
    return jnp.concatenate([g, g[NOPE:]])[None, :]


def _undup(g):
    return jnp.concatenate([g[..., :NOPE], g[..., NOPE:NOPE + ROPE] + g[..., NOPE + ROPE:]], axis=-1)


def kernel(x, c, positions, ada_w, ada_b, norm_g, w_in, q_lat_g, w_q_up, kv_lat_g, w_kv_up, q_norm_g, k_norm_g, glu_b, dw_w, dw_b, conv_ln_g, conv_ln_b, w_pw, b_pw, w_out, loss_target, m_ada_w, m_ada_b, m_norm_g, m_w_in, m_q_lat_g, m_w_q_up, m_kv_lat_g, m_w_kv_up, m_q_norm_g, m_k_norm_g, m_glu_b, m_dw_w, m_dw_b, m_conv_ln_g, m_conv_ln_b, m_w_pw, m_b_pw, m_w_out, v_ada_w, v_ada_b, v_norm_g, v_w_in, v_q_lat_g, v_w_q_up, v_kv_lat_g, v_w_kv_up, v_q_norm_g, v_k_norm_g, v_glu_b, v_dw_w, v_dw_b, v_conv_ln_g, v_conv_ln_b, v_w_pw, v_b_pw, v_w_out):
    nl = 2
    s, d = x.shape[1], x.shape[2]
    xi, yi, ci = lax.axis_index("x"), lax.axis_index("y"), lax.axis_index("c")
    shard = 2 * xi + yi
    me = 4 * xi + 2 * yi + ci
    cidx = jnp.reshape(ci, (1,)).astype(jnp.int32)
    jc = jnp.stack([shard, ci]).astype(jnp.int32)
    x0 = x.reshape(s, d)
    target = loss_target.reshape(s, d)

    c_all = _allgather8(c.reshape(8, d // 8), "gather_c").reshape(8, d)
    n_ada = ada_w.shape[2]
    ada_b_shard = lax.dynamic_slice_in_dim(ada_b, shard * n_ada, n_ada, axis=1)[:, None, :]
    mod_shard = _modulation(c_all, ada_w, ada_b_shard)
    mod_all = _allgather8(mod_shard.reshape(nl * 8, n_ada), "gather_mod")
    mod_rows = lax.dynamic_index_in_dim(mod_all.reshape(4, 2, nl, 8, n_ada)[:, 0], me, axis=2, keepdims=False)
    mod_me = jnp.transpose(mod_rows, (1, 0, 2)).reshape(nl, 3, 1, d)

    tr = lambda a: jnp.transpose(a, (0, 2, 1))
    w_in_t = tr(w_in).astype(BF16)
    wq = w_q_up.reshape(nl, QL, 2, QK)
    wq = jnp.concatenate([wq, wq[..., NOPE:]], axis=-1)
    wq = jnp.transpose(wq, (0, 2, 1, 3)).reshape(nl, 2 * QL, HP).astype(BF16)
    dw_pad = jnp.pad(dw_w, ((0, 0), (0, HALO - CONV_K), (0, 0)))
    local = [w_in_t, wq, w_kv_up.astype(BF16), dw_pad, w_pw.astype(BF16), w_out.astype(BF16)]

    def kernel_layouts(bufs):
        w_in_g, wq_g, wkv_g, dw_g, wpw_g, wout_g = bufs
        heads_side_by_side = lambda a, rows: jnp.transpose(a.reshape(1, -1, rows, a.shape[-1]), (0, 2, 1, 3)).reshape(1, rows, -1)
        return dict(w_in=w_in_g.reshape(1, 4 * w_in_g.shape[2], d), wq=heads_side_by_side(wq_g, QL), wkv=heads_side_by_side(wkv_g, KVL),
                    dw=jnp.transpose(dw_g, (0, 2, 1, 3)).reshape(1, HALO, 4 * dw_g.shape[3]),
                    wpw=wpw_g.reshape(1, 4 * wpw_g.shape[2], wpw_g.shape[3]), wout=wout_g.reshape(1, 4 * wout_g.shape[2], d))

    w_in_all = [_run_alone(_gather_hand_on(_run_alone(_gather_start([local[0][0:1]]), "gather_w_in0")), "gather_w_in0_hand_on"), None]
    others0_start = _gather_start([a[0:1] for a in local[1:]])
    mid1_start, w_out1_start = _gather_start([a[1:2] for a in local[1:5]]), _gather_start([local[5][1:2]])
    w_in1_start = _gather_start([local[0][1:2]])
    wts = [None] * nl

    cos, sin = _rope_tables(positions.reshape(s, 1))
    row = lambda a, l: a[l][None, :]

    saved = []
    xl = x0
    for l in range(nl):
        shift, scale, gate = mod_me[l, 0], mod_me[l, 1], mod_me[l, 2]
        in_args = (xl, row(norm_g, l), scale, shift, w_in_all[l][0].reshape(1, -1, d), 0, f"inproj_fwd{l}")
        if l == 0:
            (hb, za, zkr, zb), landed = _inproj_fwd(*in_args, carried=others0_start)
            others = _run_alone(_gather_hand_on(landed), "gather_others0_hand_on")
        else:
            (hb, za, zkr, zb), others = _inproj_fwd(*in_args, carried=_gather_hand_on(mid1_landed + w_out1_landed))
        w = wts[l] = kernel_layouts(w_in_all[l] + others)
        gains = (row(q_lat_g, l), row(kv_lat_g, l), _dup_gain(q_norm_g[l]), _dup_gain(k_norm_g[l]))
        prep_args = (za, zkr, cos, sin, w["wq"], w["wkv"], *gains, 0, f"mla_prep_fwd{l}")
        conv_args = (zb, row(glu_b, l), w["dw"], row(dw_b, l), row(conv_ln_g, l), row(conv_ln_b, l), w["wpw"], row(b_pw, l), 0)
        out_args = (gate, w["wout"], 0, f"outproj_fwd{l}")
        if l == 0:
            (q, k, v), mid1_landed = _mla_prep_fwd(*prep_args, carried=mid1_start)
            (o, lse), landed = _flash_fwd(q, k, v, f"flash_fwd{l}", carried=w_in1_start)
            (cv, pw), w_in_all[1] = _conv_fwd(*conv_args, f"conv_fwd{l}", carried=_gather_hand_on(landed))
            (xn, mixb), w_out1_landed = _outproj_fwd(xl, o, zb, pw, *out_args, carried=w_out1_start)
        else:
            q, k, v = _mla_prep_fwd(*prep_args)
            o, lse = _flash_fwd(q, k, v, f"flash_fwd{l}")
            cv, pw = _conv_fwd(*conv_args, f"conv_fwd{l}")
            tok_loss, dx, mixb = _outproj_fwd(xl, o, zb, pw, *out_args, target=target)
            xn = None
        saved.append(dict(x=xl, hb=hb, za=za, zkr=zkr, zb=zb, q=q, k=k, v=v, o=o, lse=lse, cv=cv, pw=pw, mixb=mixb, gains=gains))
        xl = xn

    loss = lax.psum(jnp.sum(tok_loss), ("x", "y", "c"))

    big = [None] * nl
    small = [None] * nl
    shards_of = lambda gs: [g.reshape(4, g.shape[0] // 4, g.shape[1]) for g in gs]
    pair_sums = lambda l, parts, theirs: _pair_sums(parts, theirs, cidx, f"pair_sums{l}")
    chip_sums = lambda l, sums, landed: _chip_sums(sums, landed, jc, f"chip_sums{l}")
    halves = [None] * nl
    for l in reversed(range(nl)):
        sv, w = saved[l], wts[l]
        shift, scale, gate = mod_me[l, 0], mod_me[l, 1], mod_me[l, 2]
        dxb, do, delta, dzb, dpw = _outproj_bwd(dx, gate, w["wout"], sv["o"], sv["zb"], sv["pw"], 0, f"outproj_bwd{l}")
        out_args = (sv["mixb"], dxb, gate, w["wout"], 0, f"grad_w_out{l}")
        attn_args = (sv["q"], sv["k"], sv["v"])
        if l == 0:
            parts = shards_of(big[1])
            (g_out, dgate), theirs = _grad_w_out(*out_args, carried=_pair_exchange(parts))
            sums = pair_sums(1, parts, theirs)
            (dq, dk, dv), landed = _flash_bwd(*attn_args, do, sv["lse"], delta, f"flash_bwd{l}", carried=_chip_scatter(sums))
            halves[1] = chip_sums(1, sums, landed)
        else:
            g_out, dgate = _grad_w_out(*out_args)
            dq, dk, dv = _flash_bwd(*attn_args, do, sv["lse"], delta, f"flash_bwd{l}")
        dza, dzkr, g_q, g_kv, dgql, dgkvl, dgq, dgk = _mla_prep_bwd(
            dq, dk, dv, sv["za"], sv["zkr"], cos, sin, w["wq"], w["wkv"], *sv["gains"], 0, f"mla_prep_bwd{l}")
        dcv, act, dbpw, dlng, dlnb = _pointwise_bwd(dpw, sv["cv"], row(conv_ln_g, l), row(conv_ln_b, l), w["wpw"], 0, f"pointwise_bwd{l}")
        g_pw, = _grad_tn(act, dpw, f"grad_w_pw{l}")
        by_shard = lambda g, n: jnp.transpose(g.reshape(g.shape[0], -1, n), (1, 0, 2)).reshape(-1, n)
        early = [by_shard(g_q, HP), by_shard(g_kv, 512), g_pw, g_out]
        conv_args = (dcv, sv["zb"], dzb, row(glu_b, l), w["dw"], 0, f"conv_bwd{l}")
        in_args = (sv["x"], dx, row(norm_g, l), scale, shift, w["w_in"], 0, f"inproj_bwd{l}")
        if l == 0:
            parts = shards_of(early)
            (dzb, g_dw, ddwb, dglub), theirs = _conv_bwd(*conv_args, carried=_pair_exchange(parts))
            sums = pair_sums("0e", parts, theirs)
            (g_in,), landed = _grad_w_in(dza, dzkr, dzb, sv["hb"], f"grad_w_in{l}", carried=_chip_scatter(sums))
            e_q, e_kv, e_pw, e_out = chip_sums("0e", sums, landed)
            parts = shards_of([g_in, g_dw.reshape(4 * HALO, DW_SHARD)])
            sums = pair_sums("0l", parts, _run_alone(_pair_exchange(parts), "pair_exchange_late0"))
            (dx, dshift, dgg), landed = _inproj_bwd(dza, dzkr, dzb, *in_args, carried=_chip_scatter(sums))
            l_in, l_dw = chip_sums("0l", sums, landed)
            halves[0] = [l_in, e_q, e_kv, l_dw, e_pw, e_out]
        else:
            dzb, g_dw, ddwb, dglub = _conv_bwd(*conv_args)
            dx, dshift, dgg = _inproj_bwd(dza, dzkr, dzb, *in_args)
            g_in, = _grad_w_in(dza, dzkr, dzb, sv["hb"], f"grad_w_in{l}")
        big[l] = [g_in, early[0], early[1], g_dw.reshape(4 * HALO, DW_SHARD), g_pw, g_out]
        small[l] = dict(ada_b=jnp.concatenate([dshift, dgg * row(norm_g, l), jnp.sum(dgate, axis=0)], axis=1), norm_g=dgg * (1.0 + scale),
                        q_lat_g=dgql, kv_lat_g=dgkvl, q_norm_g=_undup(dgq), k_norm_g=_undup(dgk), glu_b=dglub, dw_b=ddwb,
                        conv_ln_g=dlng, conv_ln_b=dlnb, b_pw=dbpw)
    grad_x = dx.reshape(x.shape)

    names = [n for n, _ in SMALL]
    mine = _pack_small({n: jnp.concatenate([small[0][n], small[1][n]], axis=0) for n in names})
    gathered = _allgather8(mine, "gather_small")
    weights = dict(ada_b=ada_b, norm_g=norm_g, q_lat_g=q_lat_g, kv_lat_g=kv_lat_g, q_norm_g=q_norm_g, k_norm_g=k_norm_g,
                   glu_b=glu_b, dw_b=dw_b, conv_ln_g=conv_ln_g, conv_ln_b=conv_ln_b, b_pw=b_pw)
    m_small = dict(ada_b=m_ada_b, norm_g=m_norm_g, q_lat_g=m_q_lat_g, kv_lat_g=m_kv_lat_g, q_norm_g=m_q_norm_g, k_norm_g=m_k_norm_g,
                   glu_b=m_glu_b, dw_b=m_dw_b, conv_ln_g=m_conv_ln_g, conv_ln_b=m_conv_ln_b, b_pw=m_b_pw)
    v_small = dict(ada_b=v_ada_b, norm_g=v_norm_g, q_lat_g=v_q_lat_g, kv_lat_g=v_kv_lat_g, q_norm_g=v_q_norm_g, k_norm_g=v_k_norm_g,
                   glu_b=v_glu_b, dw_b=v_dw_b, conv_ln_g=v_conv_ln_g, conv_ln_b=v_conv_ln_b, b_pw=v_b_pw)
    widths = {n: weights[n].shape[1] for n in names}
    v_packed = _pack_small({n: jnp.pad(v_small[n], ((0, 0), (0, dict(SMALL)[n] - widths[n])), constant_values=1.0) for n in names})
    small_out = [_unpack_small(a, widths) for a in _small_update(gathered, _pack_small(weights), _pack_small(m_small), v_packed)]

    ada_rows = gathered.reshape(8, nl, -1)[:, :, :3 * d]
    dmod = lax.dynamic_slice_in_dim(jnp.transpose(ada_rows, (1, 0, 2)), shard * n_ada, n_ada, axis=2)
    ada_out = _ada_update(c_all, dmod, ada_w, m_ada_w, v_ada_w)

    full = _run_alone(_pair_complete(halves[0] + halves[1]), "pair_complete")
    per_layer = [full[l * 6:(l + 1) * 6] for l in range(nl)]

    def natural_q(g):
        return jnp.transpose(_undup(g.reshape(2, QL, HP)), (1, 0, 2)).reshape(QL, 2 * QK)

    grads = [[per_layer[l][0], natural_q(per_layer[l][1]), per_layer[l][2], per_layer[l][3][:CONV_K], per_layer[l][4], per_layer[l][5]]
             for l in range(nl)]
    sharded = (("w_in", tr(w_in), tr(m_w_in), tr(v_w_in)), ("w_q_up", w_q_up, m_w_q_up, v_w_q_up),
               ("w_kv_up", w_kv_up, m_w_kv_up, v_w_kv_up), ("dw_w", dw_w, m_dw_w, v_dw_w),
               ("w_pw", w_pw, m_w_pw, v_w_pw), ("w_out", w_out, m_w_out, v_w_out))
    big_out = {name: _adam_update(w, grads[0][e], grads[1][e], m, v, f"adam_{name}") for e, (name, w, m, v) in enumerate(sharded)}
    big_out["w_in"] = [tr(a) for a in big_out["w_in"]]

    order = ["ada_w", "ada_b", "norm_g", "w_in", "q_lat_g", "w_q_up", "kv_lat_g", "w_kv_up", "q_norm_g", "k_norm_g", "glu_b",
             "dw_w", "dw_b", "conv_ln_g", "conv_ln_b", "w_pw", "b_pw", "w_out"]

    def leaf(kind, name):
        if name == "ada_w":
            return ada_out[kind]
        if name in big_out:
            return big_out[name][kind]
        return small_out[kind][name]

    return (loss, grad_x, *[leaf(kind, name) for kind in range(4) for name in order])
```

```python
import functools
import math

import jax
import jax.numpy as jnp
from jax import lax
from jax.experimental import pallas as pl
from jax.experimental.pallas import tpu as pltpu

F32, BF16 = jnp.float32, jnp.bfloat16
MESH = pl.DeviceIdType.MESH
ANY = pl.BlockSpec(memory_space=pl.ANY)

N_HEADS, NOPE, ROPE, VD = 8, 128, 64, 128
QK = NOPE + ROPE
QL, KVL = 512, 256
HP = 256
CONV_K, HALO = 31, 32
ROPE_THETA = 10000.0
EPS = 1e-6
ADAM_LR, ADAM_B1, ADAM_B2, ADAM_EPS, ADAM_WD, ADAM_STEP = 0.001, 0.9, 0.999, 1e-08, 0.01, 10
V7X_VMEM_LIMIT = 56 * 1024 * 1024

NT = (((1,), (1,)), ((), ()))
TN = (((0,), (0,)), ((), ()))
NN = (((1,), (0,)), ((), ()))


def _dot(a, b, dims=NN):
    return lax.dot_general(a, b, dims, preferred_element_type=F32)


def _params(*sem):
    return pltpu.CompilerParams(dimension_semantics=sem or None, vmem_limit_bytes=V7X_VMEM_LIMIT)


def _sigmoid(x):
    return 1.0 / (1.0 + jnp.exp(-x))


def _sum0(x):
    return jnp.sum(x, axis=0, keepdims=True)


def _sum1(x):
    return jnp.sum(x, axis=1, keepdims=True)


def _row(n):
    return pl.BlockSpec((1, n), lambda *_: (0, 0))


def _place():
    x, y, c = lax.axis_index("x"), lax.axis_index("y"), lax.axis_index("c")
    chips = [(1 - x, y), (x, 1 - y), (1 - x, 1 - y)]
    return x, y, c, chips


def _allgather8(v, name):
    r, n = v.shape

    def body(v_ref, out_ref, send_sems, recv_sems, local_sem):
        x, y, c, chips = _place()
        me, sibling = (x, y, c), (x, y, 1 - c)

        def slot(px, py, pc):
            return out_ref.at[4 * px + 2 * py + pc]

        def copy(k, block, to, src=None):
            return pltpu.make_async_remote_copy(
                src_ref=slot(*block) if src is None else src, dst_ref=slot(*block),
                send_sem=send_sems.at[k], recv_sem=recv_sems.at[k], device_id=to, device_id_type=MESH)

        mine = pltpu.make_async_copy(v_ref, slot(*me), local_sem)
        mine.start()
        first = [copy(0, me, sibling, src=v_ref)]
        first += [copy(1 + j, me, (*chip, c), src=v_ref) for j, chip in enumerate(chips)]
        for cp in first:
            cp.start()
        passed = [copy(4 + j, (*chip, c), sibling) for j, chip in enumerate(chips)]
        for j, chip in enumerate(chips):
            copy(1 + j, (*chip, c), me).wait_recv()
            passed[j].start()
        copy(0, sibling, me).wait_recv()
        for j, chip in enumerate(chips):
            copy(4 + j, (*chip, 1 - c), me).wait_recv()
        for cp in first + passed:
            cp.wait_send()
        mine.wait()

    return pl.pallas_call(
        body, name=name, out_shape=jax.ShapeDtypeStruct((8, r, n), v.dtype),
        in_specs=[pl.BlockSpec(memory_space=pltpu.VMEM)], out_specs=pl.BlockSpec(memory_space=pltpu.VMEM),
        scratch_shapes=[pltpu.SemaphoreType.DMA((7,)), pltpu.SemaphoreType.DMA((7,)), pltpu.SemaphoreType.DMA],
    )(v)


class _Carried:
    def __init__(self, operands, results, n_sems, start, finish, aliases=None):
        self.operands, self.results, self.n_sems = operands, results, n_sems
        self.start, self.finish, self.aliases = start, finish, aliases or {}


def _run_alone(carried, name):
    k = len(carried.operands)

    def body(*refs):
        args = (refs[:k], refs[k:k + len(carried.results)], refs[-2], refs[-1])
        carried.start(*args)
        carried.finish(*args)

    outs = pl.pallas_call(
        body, name=name, out_shape=carried.results, in_specs=[ANY] * k, out_specs=[ANY] * len(carried.results),
        input_output_aliases=carried.aliases,
        scratch_shapes=[pltpu.SemaphoreType.DMA((carried.n_sems,)), pltpu.SemaphoreType.DMA((carried.n_sems,))],
    )(*carried.operands)
    return list(outs)


def _call(body, operands, *, name, grid, in_specs, out_specs, out_shape, scratch=(), aliases=None, carried=None):
    params = _params(*(["arbitrary"] * len(grid)))
    n_in, n_out = len(in_specs), len(out_shape)
    if carried is None:
        return pl.pallas_call(body, name=name, grid=grid, in_specs=in_specs, out_specs=out_specs, out_shape=out_shape,
                              scratch_shapes=list(scratch), input_output_aliases=aliases or {}, compiler_params=params)(*operands)
    k_in, k_out = len(carried.operands), len(carried.results)

    def wrapped(*refs):
        ins, outs = refs[:n_in], refs[n_in + k_in:n_in + k_in + n_out]
        comm = (refs[n_in:n_in + k_in], refs[n_in + k_in + n_out:n_in + k_in + n_out + k_out], refs[-2], refs[-1])
        steps = [pl.program_id(a) for a in range(len(grid))]
        first = functools.reduce(jnp.logical_and, [s == 0 for s in steps])
        last = functools.reduce(jnp.logical_and, [s == g - 1 for s, g in zip(steps, grid)])

        @pl.when(first)
        def _():
            carried.start(*comm)

        body(*ins, *outs, *refs[n_in + k_in + n_out + k_out:-2])

        @pl.when(last)
        def _():
            carried.finish(*comm)

    both = dict(aliases or {})
    both.update({n_in + i: n_out + o for i, o in carried.aliases.items()})
    res = pl.pallas_call(
        wrapped, name=name, grid=grid, in_specs=list(in_specs) + [ANY] * k_in, out_specs=list(out_specs) + [ANY] * k_out,
        out_shape=list(out_shape) + list(carried.results), input_output_aliases=both, compiler_params=params,
        scratch_shapes=list(scratch) + [pltpu.SemaphoreType.DMA((carried.n_sems,)), pltpu.SemaphoreType.DMA((carried.n_sems,))],
    )(*operands, *carried.operands)
    return list(res[:n_out]), list(res[n_out:])


def _gather_start(shards):
    ne = len(shards)
    per = 4

    def copies(srcs, dsts, send_sems, recv_sems):
        x, y, c, chips = _place()
        jme = 2 * x + y
        out = []
        for e in range(ne):
            half = srcs[e].shape[2] // 2
            own = pl.ds(pl.multiple_of(c * half, 128), half)
            for k, chip in enumerate(chips):
                out.append(pltpu.make_async_remote_copy(
                    src_ref=srcs[e].at[:, :, own], dst_ref=dsts[e].at[:, jme, :, own], send_sem=send_sems.at[per * e + k],
                    recv_sem=recv_sems.at[per * e + k], device_id=(*chip, c), device_id_type=MESH))
            out.append(pltpu.make_async_remote_copy(
                src_ref=srcs[e], dst_ref=dsts[e].at[:, jme], send_sem=send_sems.at[per * e + 3],
                recv_sem=recv_sems.at[per * e + 3], device_id=(x, y, 1 - c), device_id_type=MESH))
        return out

    def start(*a):
        for cp in copies(*a):
            cp.start()

    def finish(*a):
        for cp in copies(*a):
            cp.wait()

    results = [jax.ShapeDtypeStruct((s.shape[0], 4) + s.shape[1:], s.dtype) for s in shards]
    return _Carried(list(shards), results, per * ne, start, finish)


def _gather_hand_on(bufs):
    ne = len(bufs)

    def copy(e, k, dsts, send_sems, recv_sems, mine):
        x, y, c, chips = _place()
        px, py = chips[k]
        half = dsts[e].shape[3] // 2
        cols = pl.ds(pl.multiple_of((c if mine else 1 - c) * half, 128), half)
        part = dsts[e].at[:, 2 * px + py, :, cols]
        return pltpu.make_async_remote_copy(src_ref=part, dst_ref=part, send_sem=send_sems.at[3 * e + k],
                                            recv_sem=recv_sems.at[3 * e + k], device_id=(x, y, 1 - c), device_id_type=MESH)

    def start(srcs, dsts, send_sems, recv_sems):
        for e in range(ne):
            for k in range(3):
                copy(e, k, dsts, send_sems, recv_sems, True).start()

    def finish(srcs, dsts, send_sems, recv_sems):
        for e in range(ne):
            for k in range(3):
                copy(e, k, dsts, send_sems, recv_sems, True).wait_send()
                copy(e, k, dsts, send_sems, recv_sems, False).wait_recv()

    results = [jax.ShapeDtypeStruct(b.shape, b.dtype) for b in bufs]
    return _Carried(list(bufs), results, 3 * ne, start, finish, aliases={e: e for e in range(ne)})


def _pair_exchange(parts):
    ne = len(parts)

    def copies(srcs, dsts, send_sems, recv_sems):
        x, y, c, _ = _place()
        out = []
        for e in range(ne):
            half = srcs[e].shape[2] // 2
            theirs = pl.ds(pl.multiple_of((1 - c) * half, 128), half)
            out.append(pltpu.make_async_remote_copy(
                src_ref=srcs[e].at[:, :, theirs], dst_ref=dsts[e], send_sem=send_sems.at[e],
                recv_sem=recv_sems.at[e], device_id=(x, y, 1 - c), device_id_type=MESH))
        return out

    def start(*a):
        for cp in copies(*a):
            cp.start()

    def finish(*a):
        for cp in copies(*a):
            cp.wait()

    results = [jax.ShapeDtypeStruct(p.shape[:2] + (p.shape[2] // 2,), p.dtype) for p in parts]
    return _Carried(list(parts), results, ne, start, finish)


def _chip_scatter(sums):
    ne = len(sums)

    def copies(srcs, dsts, send_sems, recv_sems):
        x, y, c, chips = _place()
        return [pltpu.make_async_remote_copy(
                    src_ref=srcs[e].at[2 * px + py], dst_ref=dsts[e].at[k], send_sem=send_sems.at[3 * e + k],
                    recv_sem=recv_sems.at[3 * e + k], device_id=(px, py, c), device_id_type=MESH)
                for e in range(ne) for k, (px, py) in enumerate(chips)]

    def start(*a):
        for cp in copies(*a):
            cp.start()

    def finish(*a):
        for cp in copies(*a):
            cp.wait()

    results = [jax.ShapeDtypeStruct((3,) + s.shape[1:], s.dtype) for s in sums]
    return _Carried(list(sums), results, 3 * ne, start, finish)


def _pair_complete(grads):
    ne = len(grads)

    def copy(e, dsts, send_sems, recv_sems, mine):
        x, y, c, _ = _place()
        half = dsts[e].shape[1] // 2
        cols = pl.ds(pl.multiple_of((c if mine else 1 - c) * half, 128), half)
        return pltpu.make_async_remote_copy(
            src_ref=dsts[e].at[:, cols], dst_ref=dsts[e].at[:, cols], send_sem=send_sems.at[e],
            recv_sem=recv_sems.at[e], device_id=(x, y, 1 - c), device_id_type=MESH)

    def start(srcs, dsts, send_sems, recv_sems):
        for e in range(ne):
            copy(e, dsts, send_sems, recv_sems, True).start()

    def finish(srcs, dsts, send_sems, recv_sems):
        for e in range(ne):
            copy(e, dsts, send_sems, recv_sems, True).wait_send()
            copy(e, dsts, send_sems, recv_sems, False).wait_recv()

    results = [jax.ShapeDtypeStruct(g.shape, g.dtype) for g in grads]
    return _Carried(list(grads), results, ne, start, finish, aliases={e: e for e in range(ne)})


def _pair_sums(parts, theirs, cidx, name):
    ne = len(parts)

    def body(c_ref, *refs):
        for e in range(ne):
            refs[2 * ne + e][...] = (refs[e][...].astype(F32) + refs[ne + e][...].astype(F32)).astype(BF16)

    halves = [(p.shape[1], p.shape[2] // 2) for p in parts]
    gs = pltpu.PrefetchScalarGridSpec(
        num_scalar_prefetch=1, grid=(4,),
        in_specs=[pl.BlockSpec((1, r, h), lambda j, c: (j, 0, c[0])) for r, h in halves]
                 + [pl.BlockSpec((1, r, h), lambda j, c: (j, 0, 0)) for r, h in halves],
        out_specs=[pl.BlockSpec((1, r, h), lambda j, c: (j, 0, 0)) for r, h in halves])
    return list(pl.pallas_call(body, name=name, grid_spec=gs, out_shape=[jax.ShapeDtypeStruct((4, r, h), BF16) for r, h in halves],
                               compiler_params=_params("arbitrary"))(cidx, *parts, *theirs))


def _chip_sums(sums, landed, jc, name):
    ne = len(sums)

    def body(jc_ref, *refs):
        for e in range(ne):
            acc = refs[e][0].astype(F32)
            for k in range(3):
                acc = acc + refs[ne + e][k].astype(F32)
            refs[2 * ne + e][...] = acc

    halves = [sm.shape[1:] for sm in sums]
    gs = pltpu.PrefetchScalarGridSpec(
        num_scalar_prefetch=1, grid=(1,),
        in_specs=[pl.BlockSpec((1, r, h), lambda i, jc: (jc[0], 0, 0)) for r, h in halves]
                 + [pl.BlockSpec((3, r, h), lambda i, jc: (0, 0, 0)) for r, h in halves],
        out_specs=[pl.BlockSpec((r, h), lambda i, jc: (0, jc[1])) for r, h in halves])
    return list(pl.pallas_call(body, name=name, grid_spec=gs, out_shape=[jax.ShapeDtypeStruct((r, 2 * h), F32) for r, h in halves],
                               compiler_params=_params("arbitrary"))(jc, *sums, *landed))


def _rope_tables(pos):
    s = pos.shape[0]
    lane = jnp.arange(128)
    inv = 1.0 / (ROPE_THETA ** ((2 * (lane % 32)).astype(F32) / ROPE))
    keep = (lane < 64).astype(F32)
    sign = jnp.where(lane < 32, -1.0, 1.0).astype(F32) * keep
    consts = jnp.stack([inv.astype(F32), keep, sign])[:, None, :]

    def body(p_ref, k_ref, c_ref, s_ref):
        ang = p_ref[...].astype(F32) * k_ref[0]
        c_ref[...] = jnp.cos(ang) * k_ref[1]
        s_ref[...] = jnp.sin(ang) * k_ref[2]

    tm = min(s, 1024)
    return pl.pallas_call(
        body, name="rope_tables", grid=(s // tm,),
        in_specs=[pl.BlockSpec((tm, 1), lambda i: (i, 0)), pl.BlockSpec((3, 1, 128), lambda i: (0, 0, 0))],
        out_specs=[pl.BlockSpec((tm, 128), lambda i: (i, 0))] * 2,
        out_shape=[jax.ShapeDtypeStruct((s, 128), F32)] * 2, compiler_params=_params("arbitrary"),
    )(pos, consts)


def _modulation(c_all, ada_w, ada_b_shard):
    nl, d, n = ada_w.shape
    tn = 512

    def body(c_ref, w_ref, b_ref, o_ref):
        cv = c_ref[...]
        act = (cv * _sigmoid(cv)).astype(BF16)
        o_ref[...] = _dot(act, w_ref[...].astype(BF16)) + b_ref[...]

    return pl.pallas_call(
        body, name="modulation", grid=(nl, n // tn),
        in_specs=[pl.BlockSpec((8, d), lambda l, j: (0, 0)), pl.BlockSpec((None, d, tn), lambda l, j: (l, 0, j)),
                  pl.BlockSpec((None, 1, tn), lambda l, j: (l, 0, j))],
        out_specs=pl.BlockSpec((None, 8, tn), lambda l, j: (l, 0, j)),
        out_shape=jax.ShapeDtypeStruct((nl, 8, n), F32), compiler_params=_params("arbitrary", "arbitrary"),
    )(c_all, ada_w, ada_b_shard)


W_ROWS = 4992
W_PIECES = ((0, 0, 832), (832, 768, 64), (896, 1856, 2048), (2944, 832, 1024), (3968, 3904, 1024))


def _load_w_in(w_hbm, w_vmem, sems):
    cps = [pltpu.make_async_copy(w_hbm.at[pl.ds(src, n)], w_vmem.at[pl.ds(dst, n)], sems.at[i])
           for i, (dst, src, n) in enumerate(W_PIECES)]
    for cp in cps:
        cp.start()
    for cp in cps:
        cp.wait()


def _inproj_fwd(x, g, scale, shift, w_int, layer, name, carried=None):
    s, d = x.shape
    tm = min(s, 256)

    def body(x_ref, g_ref, sc_ref, sh_ref, w_hbm, hb_ref, za_ref, zkr_ref, zb_ref, w_vmem, sems):
        @pl.when(pl.program_id(0) == 0)
        def _():
            _load_w_in(w_hbm.at[layer], w_vmem, sems)

        xv = x_ref[...]
        rstd = lax.rsqrt(jnp.mean(xv * xv, axis=1, keepdims=True) + EPS)
        h = (xv * rstd) * g_ref[...] * (1.0 + sc_ref[...]) + sh_ref[...]
        hb = h.astype(BF16)
        hb_ref[...] = hb
        za_ref[...] = _dot(hb, w_vmem[0:768], NT)
        zkr_ref[...] = _dot(hb, w_vmem[768:896], NT)
        zb_ref[...] = _dot(hb, w_vmem[896:W_ROWS], NT)

    return _call(
        body, (x, g, scale, shift, w_int), name=name, grid=(s // tm,), carried=carried,
        in_specs=[pl.BlockSpec((tm, d), lambda i: (i, 0)), _row(d), _row(d), _row(d), ANY],
        out_specs=[pl.BlockSpec((tm, d), lambda i: (i, 0)), pl.BlockSpec((tm, 768), lambda i: (i, 0)),
                   pl.BlockSpec((tm, 128), lambda i: (i, 0)), pl.BlockSpec((tm, 4096), lambda i: (i, 0))],
        out_shape=[jax.ShapeDtypeStruct((s, d), BF16), jax.ShapeDtypeStruct((s, 768), F32),
                   jax.ShapeDtypeStruct((s, 128), F32), jax.ShapeDtypeStruct((s, 4096), F32)],
        scratch=[pltpu.VMEM((W_ROWS, d), BF16), pltpu.SemaphoreType.DMA((len(W_PIECES),))])


def _rope(yv, cos, sin):
    return yv * cos + pltpu.roll(yv, 32, axis=1) * sin


def _mla_prep_fwd(za, zkr, cos, sin, wq, wkv, gql, gkvl, gq2, gk2, layer, name, carried=None):
    s = za.shape[0]
    tm = min(s, 256)

    def body(za_ref, zkr_ref, cos_ref, sin_ref, wq_ref, wkv_ref, gql_ref, gkvl_ref, gq_ref, gk_ref, q_ref, k_ref, v_ref):
        zq, zkv = za_ref[:, 0:QL], za_ref[:, QL:QL + KVL]
        qn = (zq * lax.rsqrt(jnp.mean(zq * zq, axis=1, keepdims=True) + EPS) * gql_ref[...]).astype(BF16)
        kvn = (zkv * lax.rsqrt(jnp.mean(zkv * zkv, axis=1, keepdims=True) + EPS) * gkvl_ref[...]).astype(BF16)
        kr = zkr_ref[...]
        kr_ss = 0.5 * _sum1(kr * kr)
        cos, sin = cos_ref[...], sin_ref[...]
        gq, gk = gq_ref[...] * SCORE_SCALE, gk_ref[...]
        qr_all, kvr_all = _dot(qn, wq_ref[...]), _dot(kvn, wkv_ref[...])
        for h in range(N_HEADS):
            qr = qr_all[:, h * HP:(h + 1) * HP]
            n, yv = qr[:, :NOPE], qr[:, NOPE:]
            rstd = lax.rsqrt((_sum1(n * n) + 0.5 * _sum1(yv * yv)) * (1.0 / QK) + EPS)
            q_ref[h, :, 0:NOPE] = (n * rstd * gq[:, :NOPE]).astype(BF16)
            q_ref[h, :, NOPE:HP] = _rope(yv * rstd * gq[:, NOPE:], cos, sin).astype(BF16)
            kvr = kvr_all[:, h * HP:(h + 1) * HP]
            kn, vv = kvr[:, :NOPE], kvr[:, NOPE:]
            rstd = lax.rsqrt((_sum1(kn * kn) + kr_ss) * (1.0 / QK) + EPS)
            k_ref[h, :, 0:NOPE] = (kn * rstd * gk[:, :NOPE]).astype(BF16)
            k_ref[h, :, NOPE:HP] = _rope(kr * rstd * gk[:, NOPE:], cos, sin).astype(BF16)
            v_ref[h] = vv.astype(BF16)

    tile = lambda n: pl.BlockSpec((tm, n), lambda i: (i, 0))
    return _call(
        body, (za, zkr, cos, sin, wq, wkv, gql, gkvl, gq2, gk2), name=name, grid=(s // tm,), carried=carried,
        in_specs=[tile(768), tile(128), tile(128), tile(128),
                  pl.BlockSpec((None, QL, N_HEADS * HP), lambda i: (layer, 0, 0)),
                  pl.BlockSpec((None, KVL, N_HEADS * HP), lambda i: (layer, 0, 0)),
                  _row(QL), _row(KVL), _row(HP), _row(HP)],
        out_specs=[pl.BlockSpec((N_HEADS, tm, HP), lambda i: (0, i, 0)), pl.BlockSpec((N_HEADS, tm, HP), lambda i: (0, i, 0)),
                   pl.BlockSpec((N_HEADS, tm, VD), lambda i: (0, i, 0))],
        out_shape=[jax.ShapeDtypeStruct((N_HEADS, s, HP), BF16), jax.ShapeDtypeStruct((N_HEADS, s, HP), BF16),
                   jax.ShapeDtypeStruct((N_HEADS, s, VD), BF16)])


SCORE_SCALE = 1.0 / math.sqrt(QK)
MASKED = -1e30


def _flash_fwd(q, k, v, name, carried=None):
    s = q.shape[1]
    t = min(s, 1024)

    def body(q_ref, k_ref, v_ref, o_ref, lse_ref):
        i = pl.program_id(1)
        qb = q_ref[...]
        row = lax.broadcasted_iota(jnp.int32, (t, t), 0)
        col = lax.broadcasted_iota(jnp.int32, (t, t), 1)

        def block(j):
            return pl.ds(pl.multiple_of(j * t, t), t)

        def scores(j):
            return _dot(qb, k_ref[block(j), :], NT)

        def update(j, sc, m, l, acc, diagonal):
            if diagonal:
                sc = jnp.where(col <= row, sc, MASKED)
            m_new = jnp.maximum(m, jnp.max(sc, axis=1, keepdims=True))
            p = jnp.exp(sc - m_new)
            alpha = jnp.exp(m - m_new)
            return m_new, alpha * l + _sum1(p), alpha * acc + _dot(p.astype(BF16), v_ref[block(j), :])

        init = (jnp.full((t, 1), MASKED, F32), jnp.zeros((t, 1), F32), jnp.zeros((t, VD), F32))
        carry = lax.fori_loop(0, i, lambda j, cr: update(j, scores(j), *cr, False), init)
        m, l, acc = update(i, scores(i), *carry, True)
        o_ref[...] = acc / l
        lse_ref[...] = m + jnp.log(l)

    return _call(
        body, (q, k, v), name=name, grid=(N_HEADS, s // t), carried=carried,
        in_specs=[pl.BlockSpec((None, t, HP), lambda h, i: (h, i, 0)), pl.BlockSpec((None, s, HP), lambda h, i: (h, 0, 0)),
                  pl.BlockSpec((None, s, VD), lambda h, i: (h, 0, 0))],
        out_specs=[pl.BlockSpec((t, VD), lambda h, i: (i, h)), pl.BlockSpec((None, t, 1), lambda h, i: (h, i, 0))],
        out_shape=[jax.ShapeDtypeStruct((s, N_HEADS * VD), F32), jax.ShapeDtypeStruct((N_HEADS, s, 1), F32)])


CH, RC = 128, 64
DW_SHARD = 256


PH_ROWS_LESS = 8


def _glu(val, gate, bias):
    c = val.shape[1]
    return (val + bias[:, :c]) * _sigmoid(gate + bias[:, c:])


def _make_phases(buf, phases, cc):
    rows = buf.shape[0] - PH_ROWS_LESS
    for b in range(1, 8):
        phases[b - 1] = buf[pl.ds(b, rows), cc:cc + CH]


def _window(buf, phases, cc, shift, r0):
    a, b = divmod(shift, 8)
    if b == 0:
        return buf[r0 + 8 * a:r0 + 8 * a + RC, cc:cc + CH]
    return phases[b - 1, r0 + 8 * a:r0 + 8 * a + RC, :]


def _conv_fwd(zb, glu_b, dw, dwb, lng, lnb, wpw, bpw, layer, name, carried=None):
    s = zb.shape[0]
    dc = dwb.shape[1]
    tm = min(s, 256)
    hb = tm // HALO

    def body(val_ref, gate_ref, valh_ref, gateh_ref, glub_ref, dw_ref, dwb_ref, lng_ref, lnb_ref, wpw_ref, bpw_ref,
             cv_ref, pw_ref, ubuf, uph):
        i = pl.program_id(0)
        bias = glub_ref[...]
        ubuf[HALO:, :] = _glu(val_ref[...], gate_ref[...], bias)
        uh = _glu(valh_ref[...], gateh_ref[...], bias)
        ubuf[0:HALO, :] = jnp.where(i > 0, uh, 0.0)
        for cc in range(0, dc, CH):
            _make_phases(ubuf, uph, cc)
            for r0 in range(0, tm, RC):
                acc = jnp.zeros((RC, CH), F32)
                for j in range(CONV_K):
                    acc = acc + _window(ubuf, uph, cc, HALO - (CONV_K - 1) + j, r0) * dw_ref[j:j + 1, cc:cc + CH]
                cv_ref[r0:r0 + RC, cc:cc + CH] = acc + dwb_ref[:, cc:cc + CH]
        cv = cv_ref[...]
        dv = cv - jnp.mean(cv, axis=1, keepdims=True)
        yl = dv * lax.rsqrt(jnp.mean(dv * dv, axis=1, keepdims=True) + EPS) * lng_ref[...] + lnb_ref[...]
        act = (yl * _sigmoid(yl)).astype(BF16)
        pw_ref[...] = _dot(act, wpw_ref[...]) + bpw_ref[...]

    return _call(
        body, (zb, zb, zb, zb, glu_b, dw, dwb, lng, lnb, wpw, bpw), name=name, grid=(s // tm,), carried=carried,
        in_specs=[pl.BlockSpec((tm, dc), lambda i: (i, 0)), pl.BlockSpec((tm, dc), lambda i: (i, 1)),
                  pl.BlockSpec((HALO, dc), lambda i: (jnp.maximum(i * hb - 1, 0), 0)),
                  pl.BlockSpec((HALO, dc), lambda i: (jnp.maximum(i * hb - 1, 0), 1)),
                  _row(2 * dc), pl.BlockSpec((None, HALO, dc), lambda i: (layer, 0, 0)), _row(dc), _row(dc), _row(dc),
                  pl.BlockSpec((None, dc, dc), lambda i: (layer, 0, 0)), _row(dc)],
        out_specs=[pl.BlockSpec((tm, dc), lambda i: (i, 0))] * 2,
        out_shape=[jax.ShapeDtypeStruct((s, dc), F32)] * 2,
        scratch=[pltpu.VMEM((tm + HALO, dc), F32), pltpu.VMEM((7, tm + HALO - PH_ROWS_LESS, CH), F32)])


def _silu_parts(z):
    sg = _sigmoid(z)
    return z * sg, sg * (1.0 + z * (1.0 - sg))


def _outproj_fwd(x, o, zb, pw, gate, wout, layer, name, carried=None, target=None):
    s, d = x.shape
    dm = o.shape[1]
    tm = min(s, 256)

    def project(x_ref, o_ref, mg_ref, cg_ref, pw_ref, gate_ref, w_ref, mix_ref):
        mg, cg = mg_ref[...], cg_ref[...]
        mix_ref[:, 0:dm] = (o_ref[...] * (mg * _sigmoid(mg))).astype(BF16)
        mix_ref[:, dm:] = (pw_ref[...] * (cg * _sigmoid(cg))).astype(BF16)
        return x_ref[...] + gate_ref[...] * _dot(mix_ref[...], w_ref[...])

    def body(*refs):
        xn_ref, mix_ref = refs[7:]
        xn_ref[...] = project(*refs[:7], mix_ref)

    def body_with_loss(*refs):
        t_ref, l_ref, dx_ref, mix_ref = refs[7:]
        err = project(*refs[:7], mix_ref) - t_ref[...]
        l_ref[...] = 0.5 * jnp.mean(err * err, axis=1, keepdims=True)
        dx_ref[...] = err * (1.0 / d)

    tile = lambda n, j=0: pl.BlockSpec((tm, n), lambda i: (i, j))
    in_specs = [tile(d), tile(dm), tile(dm, 2), tile(dm, 3), tile(dm), _row(d), pl.BlockSpec((None, 2 * dm, d), lambda i: (layer, 0, 0))]
    if target is None:
        return _call(body, (x, o, zb, zb, pw, gate, wout), name=name, grid=(s // tm,), carried=carried, in_specs=in_specs,
                     out_specs=[tile(d), tile(2 * dm)],
                     out_shape=[jax.ShapeDtypeStruct((s, d), F32), jax.ShapeDtypeStruct((s, 2 * dm), BF16)])
    return _call(body_with_loss, (x, o, zb, zb, pw, gate, wout, target), name=name, grid=(s // tm,), carried=carried,
                 in_specs=in_specs + [tile(d)], out_specs=[tile(1), tile(d), tile(2 * dm)],
                 out_shape=[jax.ShapeDtypeStruct((s, 1), F32), jax.ShapeDtypeStruct((s, d), F32), jax.ShapeDtypeStruct((s, 2 * dm), BF16)])


def _grad_tn(a, b, name, carried=None):
    s, n = a.shape
    m = b.shape[1]
    tn, ts = min(n, 1024), min(s, 512)
    nt = s // ts

    def body(a_ref, b_ref, o_ref, acc):
        t = pl.program_id(1)

        @pl.when(t == 0)
        def _():
            acc[...] = jnp.zeros_like(acc)

        acc[...] += _dot(a_ref[...].astype(BF16), b_ref[...].astype(BF16), TN)

        @pl.when(t == nt - 1)
        def _():
            o_ref[...] = acc[...].astype(BF16)

    return _call(
        body, (a, b), name=name, grid=(n // tn, nt), carried=carried,
        in_specs=[pl.BlockSpec((ts, tn), lambda r, t: (t, r)), pl.BlockSpec((ts, m), lambda r, t: (t, 0))],
        out_specs=[pl.BlockSpec((tn, m), lambda r, t: (r, 0))], out_shape=[jax.ShapeDtypeStruct((n, m), BF16)],
        scratch=[pltpu.VMEM((tn, m), F32)])


def _grad_w_out(mixb, dxb, gate, wout, layer, name, carried=None):
    s, n = mixb.shape
    d = dxb.shape[1]
    tn, ts = min(n, 1024), min(s, 512)
    nt = s // ts

    def body(a_ref, b_ref, gate_ref, w_ref, g_ref, dgate_ref, acc):
        t = pl.program_id(1)

        @pl.when(t == 0)
        def _():
            acc[...] = jnp.zeros_like(acc)

        acc[...] += _dot(a_ref[...], b_ref[...], TN)

        @pl.when(t == nt - 1)
        def _():
            m = acc[...]
            dgate_ref[...] = _sum0(m * w_ref[...].astype(F32))
            g_ref[...] = (m * gate_ref[...]).astype(BF16)

    return _call(
        body, (mixb, dxb, gate, wout), name=name, grid=(n // tn, nt), carried=carried,
        in_specs=[pl.BlockSpec((ts, tn), lambda r, t: (t, r)), pl.BlockSpec((ts, d), lambda r, t: (t, 0)), _row(d),
                  pl.BlockSpec((None, tn, d), lambda r, t: (layer, r, 0))],
        out_specs=[pl.BlockSpec((tn, d), lambda r, t: (r, 0)), pl.BlockSpec((None, 1, d), lambda r, t: (r, 0, 0))],
        out_shape=[jax.ShapeDtypeStruct((n, d), BF16), jax.ShapeDtypeStruct((n // tn, 1, d), F32)],
        scratch=[pltpu.VMEM((tn, d), F32)])


def _outproj_bwd(dxo, gate, wout, o, zb, pw, layer, name, carried=None):
    s, d = dxo.shape
    dm = o.shape[1]
    tm = min(s, 256)

    def body(dx_ref, gate_ref, w_ref, o_ref, mg_ref, cg_ref, pw_ref, dxb_ref, do_ref, delta_ref, dzb_ref, dpw_ref):
        dx = dx_ref[...]
        dxb_ref[...] = dx.astype(BF16)
        dmix = _dot((dx * gate_ref[...]).astype(BF16), w_ref[...], NT)
        da, db = dmix[:, :dm], dmix[:, dm:]
        ov = o_ref[...]
        silu_m, dsilu_m = _silu_parts(mg_ref[...])
        do = da * silu_m
        do_ref[...] = do.astype(BF16)
        prod = do * ov
        for h in range(N_HEADS):
            delta_ref[h] = _sum1(prod[:, h * VD:(h + 1) * VD])
        dzb_ref[:, 0:dm] = da * ov * dsilu_m
        silu_c, dsilu_c = _silu_parts(cg_ref[...])
        dpw_ref[...] = db * silu_c
        dzb_ref[:, dm:] = db * pw_ref[...] * dsilu_c

    tile = lambda n, j=0: pl.BlockSpec((tm, n), lambda i: (i, j))
    return _call(
        body, (dxo, gate, wout, o, zb, zb, pw), name=name, grid=(s // tm,), carried=carried,
        in_specs=[tile(d), _row(d), pl.BlockSpec((None, 2 * dm, d), lambda i: (layer, 0, 0)),
                  tile(dm), tile(dm, 2), tile(dm, 3), tile(dm)],
        out_specs=[tile(d), tile(dm), pl.BlockSpec((N_HEADS, tm, 1), lambda i: (0, i, 0)), tile(2 * dm, 1), tile(dm)],
        out_shape=[jax.ShapeDtypeStruct((s, d), BF16), jax.ShapeDtypeStruct((s, dm), BF16),
                   jax.ShapeDtypeStruct((N_HEADS, s, 1), F32), jax.ShapeDtypeStruct((s, 4 * dm), F32),
                   jax.ShapeDtypeStruct((s, dm), F32)])


def _flash_bwd(q, k, v, do, lse, delta, name, carried=None):
    s = q.shape[1]
    t = min(s, 1024)
    nq = s // t

    def body(q_ref, k_ref, v_ref, do_ref, lse_ref, delta_ref, dq_ref, dk_ref, dv_ref):
        j = pl.program_id(1)

        @pl.when(j == 0)
        def _():
            dq_ref[...] = jnp.zeros_like(dq_ref)

        kb, vb = k_ref[...], v_ref[...]
        row = lax.broadcasted_iota(jnp.int32, (t, t), 0)
        col = lax.broadcasted_iota(jnp.int32, (t, t), 1)

        def block(i):
            return pl.ds(pl.multiple_of(i * t, t), t)

        def scores(i):
            at = block(i)
            return _dot(q_ref[at, :], kb, NT), _dot(do_ref[at, :], vb, NT)

        def update(i, sc, dp, dk, dv, diagonal):
            at = block(i)
            p = jnp.exp(sc - lse_ref[at, :])
            if diagonal:
                p = jnp.where(col <= row, p, 0.0)
            dv = dv + _dot(p.astype(BF16), do_ref[at, :], TN)
            ds = (p * (dp - delta_ref[at, :])).astype(BF16)
            dq_ref[at, :] += _dot(ds, kb)
            return dk + _dot(ds, q_ref[at, :], TN), dv

        carry = update(j, *scores(j), jnp.zeros((t, HP), F32), jnp.zeros((t, VD), F32), True)
        dk, dv = lax.fori_loop(j + 1, nq, lambda i, cr: update(i, *scores(i), *cr, False), carry)
        dk_ref[...] = dk
        dv_ref[...] = dv.astype(BF16)

    whole = lambda n: pl.BlockSpec((None, s, n), lambda h, j: (h, 0, 0))
    blk = lambda n: pl.BlockSpec((None, t, n), lambda h, j: (h, j, 0))
    return _call(
        body, (q, k, v, do, lse, delta), name=name, grid=(N_HEADS, nq), carried=carried,
        in_specs=[whole(HP), blk(HP), blk(VD), pl.BlockSpec((s, VD), lambda h, j: (0, h)), whole(1), whole(1)],
        out_specs=[whole(HP), blk(HP), blk(VD)],
        out_shape=[jax.ShapeDtypeStruct((N_HEADS, s, HP), F32), jax.ShapeDtypeStruct((N_HEADS, s, HP), F32),
                   jax.ShapeDtypeStruct((N_HEADS, s, VD), BF16)])


def _mla_prep_bwd(dq, dk, dv, za, zkr, cos, sin, wq, wkv, gql, gkvl, gq2, gk2, layer, name):
    s = za.shape[0]
    tm = min(s, 512)

    def norm_bwd(n, yv, rstd, gain, d_n_out, d_y_out):
        tn_, ty = n * rstd, yv * rstd
        dgain_n, dgain_y = _sum0(d_n_out * tn_), _sum0(d_y_out * ty)
        dtn, dty = d_n_out * gain[:, :NOPE], d_y_out * gain[:, NOPE:]
        a = (_sum1(dtn * n) + _sum1(dty * yv)) * (rstd * rstd * rstd * (1.0 / QK))
        return rstd * dtn - n * a, rstd * dty - (0.5 * yv) * a, dgain_n, dgain_y

    def rope_bwd(d_out, cos, sin):
        return d_out * cos + pltpu.roll(d_out * sin, 128 - 32, axis=1)

    def latent_bwd(z, gain, dn):
        rstd = lax.rsqrt(jnp.mean(z * z, axis=1, keepdims=True) + EPS)
        zh = z * rstd
        dzh = dn * gain
        return rstd * (dzh - zh * jnp.mean(dzh * zh, axis=1, keepdims=True)), _sum0(dn * zh)

    def body(dq_ref, dk_ref, dv_ref, za_ref, zkr_ref, cos_ref, sin_ref, wq_ref, wkv_ref, gql_ref, gkvl_ref, gq_ref, gk_ref,
             dza_ref, dzkr_ref, gwq_ref, gwkv_ref, dgql_ref, dgkvl_ref, dgq_ref, dgk_ref, gwq_acc, gwkv_acc):
        @pl.when(pl.program_id(0) == 0)
        def _():
            for r in (gwq_acc, gwkv_acc, dgql_ref, dgkvl_ref, dgq_ref, dgk_ref):
                r[...] = jnp.zeros_like(r)

        zq, zkv = za_ref[:, 0:QL], za_ref[:, QL:QL + KVL]
        qf = zq * lax.rsqrt(jnp.mean(zq * zq, axis=1, keepdims=True) + EPS) * gql_ref[...]
        kvf = zkv * lax.rsqrt(jnp.mean(zkv * zkv, axis=1, keepdims=True) + EPS) * gkvl_ref[...]
        qn, kvn = qf.astype(BF16), kvf.astype(BF16)
        qn_t, kvn_t = qf.T.astype(BF16), kvf.T.astype(BF16)
        kr = zkr_ref[...]
        kr_ss = 0.5 * _sum1(kr * kr)
        cos, sin = cos_ref[...], sin_ref[...]
        gq, gk = gq_ref[...], gk_ref[...]
        dkr = jnp.zeros((tm, 128), F32)
        qr_all, kvr_all = _dot(qn, wq_ref[...]), _dot(kvn, wkv_ref[...])
        dqr_all, dkvr_all = [], []
        for h in range(N_HEADS):
            qr = qr_all[:, h * HP:(h + 1) * HP]
            n, yv = qr[:, :NOPE], qr[:, NOPE:]
            rstd = lax.rsqrt((_sum1(n * n) + 0.5 * _sum1(yv * yv)) * (1.0 / QK) + EPS)
            dqh = dq_ref[h] * SCORE_SCALE
            dn, dy, dg_n, dg_y = norm_bwd(n, yv, rstd, gq, dqh[:, :NOPE], rope_bwd(dqh[:, NOPE:], cos, sin))
            dgq_ref[:, 0:NOPE] += dg_n
            dgq_ref[:, NOPE:] += dg_y
            dqr_all += [dn.astype(BF16), dy.astype(BF16)]
            kn = kvr_all[:, h * HP:h * HP + NOPE]
            rstd = lax.rsqrt((_sum1(kn * kn) + kr_ss) * (1.0 / QK) + EPS)
            dkh = dk_ref[h]
            dn, dy, dg_n, dg_y = norm_bwd(kn, kr, rstd, gk, dkh[:, :NOPE], rope_bwd(dkh[:, NOPE:], cos, sin))
            dgk_ref[:, 0:NOPE] += dg_n
            dgk_ref[:, NOPE:] += dg_y
            dkr = dkr + dy
            dkvr_all += [dn.astype(BF16), dv_ref[h].astype(BF16)]

        dqr_all, dkvr_all = jnp.concatenate(dqr_all, axis=1), jnp.concatenate(dkvr_all, axis=1)
        gwq_acc[...] += _dot(qn_t, dqr_all)
        gwkv_acc[...] += _dot(kvn_t, dkvr_all)

        @pl.when(pl.program_id(0) == s // tm - 1)
        def _():
            gwq_ref[...] = gwq_acc[...].astype(BF16)
            gwkv_ref[...] = gwkv_acc[...].astype(BF16)

        dzq, dgql = latent_bwd(zq, gql_ref[...], _dot(dqr_all, wq_ref[...], NT))
        dzkv, dgkvl = latent_bwd(zkv, gkvl_ref[...], _dot(dkvr_all, wkv_ref[...], NT))
        dgql_ref[...] += dgql
        dgkvl_ref[...] += dgkvl
        dza_ref[:, 0:QL] = dzq
        dza_ref[:, QL:] = dzkv
        lane = lax.broadcasted_iota(jnp.int32, (tm, 128), 1)
        dzkr_ref[...] = jnp.where(lane < ROPE, dkr + pltpu.roll(dkr, 64, axis=1), 0.0)

    tile = lambda n: pl.BlockSpec((tm, n), lambda i: (i, 0))
    heads = lambda n: pl.BlockSpec((N_HEADS, tm, n), lambda i: (0, i, 0))
    return pl.pallas_call(
        body, name=name, grid=(s // tm,),
        in_specs=[heads(HP), heads(HP), heads(VD), tile(768), tile(128), tile(128), tile(128),
                  pl.BlockSpec((None, QL, N_HEADS * HP), lambda i: (layer, 0, 0)),
                  pl.BlockSpec((None, KVL, N_HEADS * HP), lambda i: (layer, 0, 0)),
                  _row(QL), _row(KVL), _row(HP), _row(HP)],
        out_specs=[tile(768), tile(128), pl.BlockSpec((QL, N_HEADS * HP), lambda i: (0, 0)),
                   pl.BlockSpec((KVL, N_HEADS * HP), lambda i: (0, 0)), _row(QL), _row(KVL), _row(HP), _row(HP)],
        out_shape=[jax.ShapeDtypeStruct((s, 768), F32), jax.ShapeDtypeStruct((s, 128), F32),
                   jax.ShapeDtypeStruct((QL, N_HEADS * HP), BF16), jax.ShapeDtypeStruct((KVL, N_HEADS * HP), BF16),
                   jax.ShapeDtypeStruct((1, QL), F32), jax.ShapeDtypeStruct((1, KVL), F32),
                   jax.ShapeDtypeStruct((1, HP), F32), jax.ShapeDtypeStruct((1, HP), F32)],
        scratch_shapes=[pltpu.VMEM((QL, N_HEADS * HP), F32), pltpu.VMEM((KVL, N_HEADS * HP), F32)],
        compiler_params=_params("arbitrary"),
    )(dq, dk, dv, za, zkr, cos, sin, wq, wkv, gql, gkvl, gq2, gk2)


def _pointwise_bwd(dpw, cv, lng, lnb, wpw, layer, name):
    s, dc = cv.shape
    tm = min(s, 256)

    def body(dpw_ref, cv_ref, lng_ref, lnb_ref, w_ref, dcv_ref, act_ref, dbpw_ref, dlng_ref, dlnb_ref):
        @pl.when(pl.program_id(0) == 0)
        def _():
            for r in (dbpw_ref, dlng_ref, dlnb_ref):
                r[...] = jnp.zeros_like(r)

        cv = cv_ref[...]
        dv = cv - jnp.mean(cv, axis=1, keepdims=True)
        rstd = lax.rsqrt(jnp.mean(dv * dv, axis=1, keepdims=True) + EPS)
        xh = dv * rstd
        yl = xh * lng_ref[...] + lnb_ref[...]
        silu, dsilu = _silu_parts(yl)
        act_ref[...] = silu.astype(BF16)
        dpw = dpw_ref[...]
        dbpw_ref[...] += _sum0(dpw)
        dyl = _dot(dpw.astype(BF16), w_ref[...], NT) * dsilu
        dlng_ref[...] += _sum0(dyl * xh)
        dlnb_ref[...] += _sum0(dyl)
        dxh = dyl * lng_ref[...]
        dcv_ref[...] = rstd * (dxh - jnp.mean(dxh, axis=1, keepdims=True) - xh * jnp.mean(dxh * xh, axis=1, keepdims=True))

    tile = pl.BlockSpec((tm, dc), lambda i: (i, 0))
    return pl.pallas_call(
        body, name=name, grid=(s // tm,),
        in_specs=[tile, tile, _row(dc), _row(dc), pl.BlockSpec((None, dc, dc), lambda i: (layer, 0, 0))],
        out_specs=[tile, tile, _row(dc), _row(dc), _row(dc)],
        out_shape=[jax.ShapeDtypeStruct((s, dc), F32), jax.ShapeDtypeStruct((s, dc), BF16)] + [jax.ShapeDtypeStruct((1, dc), F32)] * 3,
        compiler_params=_params("arbitrary"),
    )(dpw, cv, lng, lnb, wpw)


def _conv_bwd(dcv, zb, dzb, glu_b, dw, layer, name, carried=None):
    s, dc = dcv.shape
    tm = min(s, 256)
    hb = tm // HALO
    last = s // tm - 1

    def body(dcv_ref, dcvn_ref, val_ref, gate_ref, valh_ref, gateh_ref, glub_ref, dw_ref, _, dzb_ref, gdw_ref, ddwb_ref, dglub_ref,
             ubuf, dbuf, gacc, uph, dph):
        i = pl.program_id(0)

        @pl.when(i == 0)
        def _():
            gacc[...] = jnp.zeros_like(gacc)
            ddwb_ref[...] = jnp.zeros_like(ddwb_ref)
            dglub_ref[...] = jnp.zeros_like(dglub_ref)

        bias = glub_ref[...]
        ubuf[HALO:, :] = _glu(val_ref[...], gate_ref[...], bias)
        ubuf[0:HALO, :] = jnp.where(i > 0, _glu(valh_ref[...], gateh_ref[...], bias), 0.0)
        dcv = dcv_ref[...]
        dbuf[0:tm, :] = dcv
        dbuf[tm:, :] = jnp.where(i < last, dcvn_ref[...], 0.0)
        ddwb_ref[...] += _sum0(dcv)
        for cc in range(0, dc, CH):
            _make_phases(ubuf, uph, cc)
            _make_phases(dbuf, dph, cc)
            for r0 in range(0, tm, RC):
                du = jnp.zeros((RC, CH), F32)
                dpiece = dbuf[r0:r0 + RC, cc:cc + CH]
                for j in range(CONV_K):
                    du = du + _window(dbuf, dph, cc, (CONV_K - 1) - j, r0) * dw_ref[j:j + 1, cc:cc + CH]
                    win = _window(ubuf, uph, cc, HALO - (CONV_K - 1) + j, r0)
                    gacc[j, :, cc:cc + CH] += (dpiece * win).reshape(RC // 8, 8, CH).sum(axis=0)
                a = val_ref[r0:r0 + RC, cc:cc + CH] + bias[:, cc:cc + CH]
                sg = _sigmoid(gate_ref[r0:r0 + RC, cc:cc + CH] + bias[:, dc + cc:dc + cc + CH])
                dzb_ref[r0:r0 + RC, cc:cc + CH] = du * sg
                dzb_ref[r0:r0 + RC, dc + cc:dc + cc + CH] = du * a * sg * (1.0 - sg)
        dglub_ref[...] += _sum0(dzb_ref[...])

        @pl.when(i == last)
        def _():
            total = jnp.sum(gacc[...], axis=1)
            for cc in range(0, dc, DW_SHARD):
                gdw_ref[cc // DW_SHARD] = total[:, cc:cc + DW_SHARD]

    return _call(
        body, (dcv, dcv, zb, zb, zb, zb, glu_b, dw, dzb), name=name, grid=(s // tm,), carried=carried, aliases={8: 0},
        in_specs=[pl.BlockSpec((tm, dc), lambda i: (i, 0)),
                  pl.BlockSpec((HALO, dc), lambda i: (jnp.minimum((i + 1) * hb, s // HALO - 1), 0)),
                  pl.BlockSpec((tm, dc), lambda i: (i, 0)), pl.BlockSpec((tm, dc), lambda i: (i, 1)),
                  pl.BlockSpec((HALO, dc), lambda i: (jnp.maximum(i * hb - 1, 0), 0)),
                  pl.BlockSpec((HALO, dc), lambda i: (jnp.maximum(i * hb - 1, 0), 1)),
                  _row(2 * dc), pl.BlockSpec((None, HALO, dc), lambda i: (layer, 0, 0)), ANY],
        out_specs=[pl.BlockSpec((tm, 2 * dc), lambda i: (i, 0)), pl.BlockSpec((4, HALO, DW_SHARD), lambda i: (0, 0, 0)),
                   _row(dc), _row(2 * dc)],
        out_shape=[jax.ShapeDtypeStruct(dzb.shape, F32), jax.ShapeDtypeStruct((4, HALO, DW_SHARD), F32),
                   jax.ShapeDtypeStruct((1, dc), F32), jax.ShapeDtypeStruct((1, 2 * dc), F32)],
        scratch=[pltpu.VMEM((tm + HALO, dc), F32), pltpu.VMEM((tm + HALO, dc), F32), pltpu.VMEM((HALO, 8, dc), F32),
                 pltpu.VMEM((7, tm + HALO - PH_ROWS_LESS, CH), F32), pltpu.VMEM((7, tm + HALO - PH_ROWS_LESS, CH), F32)])


def _inproj_bwd(dza, dzkr, dzb, x, dxo, g, scale, shift, w_int, layer, name, carried=None):
    s, d = x.shape
    tm = min(s, 128)

    def body(dza_ref, dzkr_ref, dzb_ref, x_ref, dxo_ref, g_ref, sc_ref, sh_ref, w_hbm, dx_ref, dsh_ref, dgg_ref, w_vmem, sems):
        @pl.when(pl.program_id(0) == 0)
        def _():
            _load_w_in(w_hbm.at[layer], w_vmem, sems)
            dsh_ref[...] = jnp.zeros_like(dsh_ref)
            dgg_ref[...] = jnp.zeros_like(dgg_ref)

        dh = _dot(dza_ref[...].astype(BF16), w_vmem[0:768])
        dh = dh + _dot(dzkr_ref[...].astype(BF16), w_vmem[768:896])
        dh = dh + _dot(dzb_ref[...].astype(BF16), w_vmem[896:W_ROWS])
        xv = x_ref[...]
        rstd = lax.rsqrt(jnp.mean(xv * xv, axis=1, keepdims=True) + EPS)
        xh = xv * rstd
        dsh_ref[...] += _sum0(dh)
        dgg_ref[...] += _sum0(dh * xh)
        dxh = dh * (g_ref[...] * (1.0 + sc_ref[...]))
        dx_ref[...] = dxo_ref[...] + rstd * (dxh - xh * jnp.mean(dxh * xh, axis=1, keepdims=True))

    tile = lambda n: pl.BlockSpec((tm, n), lambda i: (i, 0))
    return _call(
        body, (dza, dzkr, dzb, x, dxo, g, scale, shift, w_int), name=name, grid=(s // tm,), carried=carried,
        in_specs=[tile(768), tile(128), tile(4096), tile(d), tile(d), _row(d), _row(d), _row(d), ANY],
        out_specs=[tile(d), _row(d), _row(d)],
        out_shape=[jax.ShapeDtypeStruct((s, d), F32), jax.ShapeDtypeStruct((1, d), F32), jax.ShapeDtypeStruct((1, d), F32)],
        scratch=[pltpu.VMEM((W_ROWS, d), BF16), pltpu.SemaphoreType.DMA((len(W_PIECES),))])


def _grad_w_in(dza, dzkr, dzb, hb, name, carried=None):
    s, d = hb.shape
    ts = min(s, 512)
    nt = s // ts
    tiles = ((0, 768), (768, 64), (1856, 1024), (2880, 1024), (832, 1024), (3904, 1024))

    def body(a_ref, kr_ref, b_ref, h_ref, o_hbm, acc, rounded, sem):
        r, t = pl.program_id(0), pl.program_id(1)

        @pl.when(t == 0)
        def _():
            acc[...] = jnp.zeros_like(acc)

        hv = h_ref[...]

        @pl.when(r == 0)
        def _():
            acc[0:768, :] += _dot(a_ref[...].astype(BF16), hv, TN)

        @pl.when(r == 1)
        def _():
            acc[0:128, :] += _dot(kr_ref[...].astype(BF16), hv, TN)

        @pl.when(r >= 2)
        def _():
            acc[...] += _dot(b_ref[...].astype(BF16), hv, TN)

        for tile, (row0, rows) in enumerate(tiles):
            @pl.when((t == nt - 1) & (r == tile))
            def _():
                rounded[0:rows, :] = acc[0:rows, :].astype(BF16)
                cp = pltpu.make_async_copy(rounded.at[pl.ds(0, rows)], o_hbm.at[pl.ds(row0, rows)], sem)
                cp.start()
                cp.wait()

    return _call(
        body, (dza, dzkr, dzb, hb), name=name, grid=(len(tiles), nt), carried=carried,
        in_specs=[pl.BlockSpec((ts, 768), lambda r, t: (jnp.where(r == 0, t, nt - 1), 0)),
                  pl.BlockSpec((ts, 128), lambda r, t: (jnp.where(r == 1, t, jnp.where(r == 0, 0, nt - 1)), 0)),
                  pl.BlockSpec((ts, 1024), lambda r, t: (jnp.where(r >= 2, t, 0), jnp.maximum(r - 2, 0))),
                  pl.BlockSpec((ts, d), lambda r, t: (t, 0))],
        out_specs=[ANY], out_shape=[jax.ShapeDtypeStruct((4928, d), BF16)],
        scratch=[pltpu.VMEM((1024, d), F32), pltpu.VMEM((1024, d), BF16), pltpu.SemaphoreType.DMA])


def _adamw(w, g, m, v):
    m = ADAM_B1 * m + (1.0 - ADAM_B1) * g
    v = ADAM_B2 * v + (1.0 - ADAM_B2) * (g * g)
    m_hat = m / (1.0 - ADAM_B1 ** ADAM_STEP)
    v_hat = v / (1.0 - ADAM_B2 ** ADAM_STEP)
    return -ADAM_LR * (m_hat / (jnp.sqrt(v_hat) + ADAM_EPS) + ADAM_WD * w), m, v


def _adam_update(w, g0, g1, m, v, name):
    _, r, c = w.shape
    fits = [t for t in range(8, r + 1, 8) if r % t == 0 and t * c * 4 <= (1 << 21)]
    tr = max(fits) if fits else r

    def body(w_ref, g0_ref, g1_ref, m_ref, v_ref, g_ref, d_ref, mo_ref, vo_ref):
        g = jnp.where(pl.program_id(0) == 0, g0_ref[...], g1_ref[...])
        g_ref[...] = g
        d_ref[...], mo_ref[...], vo_ref[...] = _adamw(w_ref[...], g, m_ref[...], v_ref[...])

    nt = r // tr
    big = pl.BlockSpec((None, tr, c), lambda l, i: (l, i, 0))
    g_of = lambda layer, idle: pl.BlockSpec((tr, c), lambda l, i: (jnp.where(l == layer, i, idle), 0))
    return pl.pallas_call(
        body, name=name, grid=(2, nt), in_specs=[big, g_of(0, nt - 1), g_of(1, 0), big, big], out_specs=[big] * 4,
        out_shape=[jax.ShapeDtypeStruct(w.shape, F32)] * 4, compiler_params=_params("arbitrary", "arbitrary"),
    )(w, g0, g1, m, v)


def _ada_update(c_all, dmod, w, m, v, carried=None):
    nl, d, n = w.shape
    tr = 256

    def body(c_ref, dm_ref, w_ref, m_ref, v_ref, g_ref, d_ref, mo_ref, vo_ref):
        cv = c_ref[...]
        act = (cv * _sigmoid(cv)).astype(BF16)
        g = _dot(act, dm_ref[...].astype(BF16), TN)
        g_ref[...] = g
        d_ref[...], mo_ref[...], vo_ref[...] = _adamw(w_ref[...], g, m_ref[...], v_ref[...])

    big = pl.BlockSpec((None, tr, n), lambda l, i: (l, i, 0))
    return _call(
        body, (c_all, dmod, w, m, v), name="ada_w_update", grid=(nl, d // tr), carried=carried,
        in_specs=[pl.BlockSpec((8, tr), lambda l, i: (0, i)), pl.BlockSpec((None, 8, n), lambda l, i: (l, 0, 0)), big, big, big],
        out_specs=[big] * 4, out_shape=[jax.ShapeDtypeStruct(w.shape, F32)] * 4)


def _small_update(gathered, w, m, v):
    r = w.shape[0]

    def body(ga_ref, w_ref, m_ref, v_ref, g_ref, d_ref, mo_ref, vo_ref):
        g = ga_ref[0]
        for dev in range(1, 8):
            g = g + ga_ref[dev]
        g_ref[...] = g
        d_ref[...], mo_ref[...], vo_ref[...] = _adamw(w_ref[...], g, m_ref[...], v_ref[...])

    return pl.pallas_call(body, name="small_update", out_shape=[jax.ShapeDtypeStruct((r, 128), F32)] * 4,
                          compiler_params=_params())(gathered, w, m, v)


SMALL = (("ada_b", 6144), ("norm_g", 2048), ("q_lat_g", 512), ("kv_lat_g", 256), ("q_norm_g", 256), ("k_norm_g", 256),
         ("glu_b", 2048), ("dw_b", 1024), ("conv_ln_g", 1024), ("conv_ln_b", 1024), ("b_pw", 1024))


def _pack_small(vals):
    cols = []
    for name, width in SMALL:
        a = vals[name]
        if a.shape[1] < width:
            a = jnp.pad(a, ((0, 0), (0, width - a.shape[1])))
        cols.append(a)
    return jnp.concatenate(cols, axis=1).reshape(-1, 128)


def _unpack_small(packed, shapes):
    flat = packed.reshape(2, -1)
    out, at = {}, 0
    for name, width in SMALL:
        out[name] = flat[:, at:at + shapes[name]]
        at += width
    return out


def _dup_gain(g):
    return jnp.concatenate([g, g[NOPE:]])[None, :]


def _undup(g):
    return jnp.concatenate([g[..., :NOPE], g[..., NOPE:NOPE + ROPE] + g[..., NOPE + ROPE:]], axis=-1)


def kernel(x, c, positions, ada_w, ada_b, norm_g, w_in, q_lat_g, w_q_up, kv_lat_g, w_kv_up, q_norm_g, k_norm_g, glu_b, dw_w, dw_b, conv_ln_g, conv_ln_b, w_pw, b_pw, w_out, loss_target, m_ada_w, m_ada_b, m_norm_g, m_w_in, m_q_lat_g, m_w_q_up, m_kv_lat_g, m_w_kv_up, m_q_norm_g, m_k_norm_g, m_glu_b, m_dw_w, m_dw_b, m_conv_ln_g, m_conv_ln_b, m_w_pw, m_b_pw, m_w_out, v_ada_w, v_ada_b, v_norm_g, v_w_in, v_q_lat_g, v_w_q_up, v_kv_lat_g, v_w_kv_up, v_q_norm_g, v_k_norm_g, v_glu_b, v_dw_w, v_dw_b, v_conv_ln_g, v_conv_ln_b, v_w_pw, v_b_pw, v_w_out):
    nl = 2
    s, d = x.shape[1], x.shape[2]
    xi, yi, ci = lax.axis_index("x"), lax.axis_index("y"), lax.axis_index("c")
    shard = 2 * xi + yi
    me = 4 * xi + 2 * yi + ci
    cidx = jnp.reshape(ci, (1,)).astype(jnp.int32)
    jc = jnp.stack([shard, ci]).astype(jnp.int32)
    x0 = x.reshape(s, d)
    target = loss_target.reshape(s, d)

    c_all = _allgather8(c.reshape(8, d // 8), "gather_c").reshape(8, d)
    n_ada = ada_w.shape[2]
    ada_b_shard = lax.dynamic_slice_in_dim(ada_b, shard * n_ada, n_ada, axis=1)[:, None, :]
    mod_shard = _modulation(c_all, ada_w, ada_b_shard)
    mod_all = _allgather8(mod_shard.reshape(nl * 8, n_ada), "gather_mod")
    mod_rows = lax.dynamic_index_in_dim(mod_all.reshape(4, 2, nl, 8, n_ada)[:, 0], me, axis=2, keepdims=False)
    mod_me = jnp.transpose(mod_rows, (1, 0, 2)).reshape(nl, 3, 1, d)

    tr = lambda a: jnp.transpose(a, (0, 2, 1))
    w_in_t = tr(w_in).astype(BF16)
    wq = w_q_up.reshape(nl, QL, 2, QK)
    wq = jnp.concatenate([wq, wq[..., NOPE:]], axis=-1)
    wq = jnp.transpose(wq, (0, 2, 1, 3)).reshape(nl, 2 * QL, HP).astype(BF16)
    dw_pad = jnp.pad(dw_w, ((0, 0), (0, HALO - CONV_K), (0, 0)))
    local = [w_in_t, wq, w_kv_up.astype(BF16), dw_pad, w_pw.astype(BF16), w_out.astype(BF16)]

    def kernel_layouts(bufs):
        w_in_g, wq_g, wkv_g, dw_g, wpw_g, wout_g = bufs
        heads_side_by_side = lambda a, rows: jnp.transpose(a.reshape(1, -1, rows, a.shape[-1]), (0, 2, 1, 3)).reshape(1, rows, -1)
        return dict(w_in=w_in_g.reshape(1, 4 * w_in_g.shape[2], d), wq=heads_side_by_side(wq_g, QL), wkv=heads_side_by_side(wkv_g, KVL),
                    dw=jnp.transpose(dw_g, (0, 2, 1, 3)).reshape(1, HALO, 4 * dw_g.shape[3]),
                    wpw=wpw_g.reshape(1, 4 * wpw_g.shape[2], wpw_g.shape[3]), wout=wout_g.reshape(1, 4 * wout_g.shape[2], d))

    w_in_all = [_run_alone(_gather_hand_on(_run_alone(_gather_start([local[0][0:1]]), "gather_w_in0")), "gather_w_in0_hand_on"), None]
    others0_start = _gather_start([a[0:1] for a in local[1:]])
    mid1_start, w_out1_start = _gather_start([a[1:2] for a in local[1:5]]), _gather_start([local[5][1:2]])
    w_in1_start = _gather_start([local[0][1:2]])
    wts = [None] * nl

    cos, sin = _rope_tables(positions.reshape(s, 1))
    row = lambda a, l: a[l][None, :]

    saved = []
    xl = x0
    for l in range(nl):
        shift, scale, gate = mod_me[l, 0], mod_me[l, 1], mod_me[l, 2]
        in_args = (xl, row(norm_g, l), scale, shift, w_in_all[l][0].reshape(1, -1, d), 0, f"inproj_fwd{l}")
        if l == 0:
            (hb, za, zkr, zb), landed = _inproj_fwd(*in_args, carried=others0_start)
            others = _run_alone(_gather_hand_on(landed), "gather_others0_hand_on")
        else:
            (hb, za, zkr, zb), others = _inproj_fwd(*in_args, carried=_gather_hand_on(mid1_landed + w_out1_landed))
        w = wts[l] = kernel_layouts(w_in_all[l] + others)
        gains = (row(q_lat_g, l), row(kv_lat_g, l), _dup_gain(q_norm_g[l]), _dup_gain(k_norm_g[l]))
        prep_args = (za, zkr, cos, sin, w["wq"], w["wkv"], *gains, 0, f"mla_prep_fwd{l}")
        conv_args = (zb, row(glu_b, l), w["dw"], row(dw_b, l), row(conv_ln_g, l), row(conv_ln_b, l), w["wpw"], row(b_pw, l), 0)
        out_args = (gate, w["wout"], 0, f"outproj_fwd{l}")
        if l == 0:
            (q, k, v), mid1_landed = _mla_prep_fwd(*prep_args, carried=mid1_start)
            (o, lse), landed = _flash_fwd(q, k, v, f"flash_fwd{l}", carried=w_in1_start)
            (cv, pw), w_in_all[1] = _conv_fwd(*conv_args, f"conv_fwd{l}", carried=_gather_hand_on(landed))
            (xn, mixb), w_out1_landed = _outproj_fwd(xl, o, zb, pw, *out_args, carried=w_out1_start)
        else:
            q, k, v = _mla_prep_fwd(*prep_args)
            o, lse = _flash_fwd(q, k, v, f"flash_fwd{l}")
            cv, pw = _conv_fwd(*conv_args, f"conv_fwd{l}")
            tok_loss, dx, mixb = _outproj_fwd(xl, o, zb, pw, *out_args, target=target)
            xn = None
        saved.append(dict(x=xl, hb=hb, za=za, zkr=zkr, zb=zb, q=q, k=k, v=v, o=o, lse=lse, cv=cv, pw=pw, mixb=mixb, gains=gains))
        xl = xn


    big = [None] * nl
    small = [None] * nl
    shards_of = lambda gs: [g.reshape(4, g.shape[0] // 4, g.shape[1]) for g in gs]
    pair_sums = lambda l, parts, theirs: _pair_sums(parts, theirs, cidx, f"pair_sums{l}")
    chip_sums = lambda l, sums, landed: _chip_sums(sums, landed, jc, f"chip_sums{l}")
    halves = [None] * nl
    for l in reversed(range(nl)):
        sv, w = saved[l], wts[l]
        shift, scale, gate = mod_me[l, 0], mod_me[l, 1], mod_me[l, 2]
        dxb, do, delta, dzb, dpw = _outproj_bwd(dx, gate, w["wout"], sv["o"], sv["zb"], sv["pw"], 0, f"outproj_bwd{l}")
        out_args = (sv["mixb"], dxb, gate, w["wout"], 0, f"grad_w_out{l}")
        attn_args = (sv["q"], sv["k"], sv["v"])
        if l == 0:
            parts = shards_of(big[1])
            (g_out, dgate), theirs = _grad_w_out(*out_args, carried=_pair_exchange(parts))
            sums = pair_sums(1, parts, theirs)
            (dq, dk, dv), landed = _flash_bwd(*attn_args, do, sv["lse"], delta, f"flash_bwd{l}", carried=_chip_scatter(sums))
            halves[1] = chip_sums(1, sums, landed)
        else:
            g_out, dgate = _grad_w_out(*out_args)
            dq, dk, dv = _flash_bwd(*attn_args, do, sv["lse"], delta, f"flash_bwd{l}")
        dza, dzkr, g_q, g_kv, dgql, dgkvl, dgq, dgk = _mla_prep_bwd(
            dq, dk, dv, sv["za"], sv["zkr"], cos, sin, w["wq"], w["wkv"], *sv["gains"], 0, f"mla_prep_bwd{l}")
        dcv, act, dbpw, dlng, dlnb = _pointwise_bwd(dpw, sv["cv"], row(conv_ln_g, l), row(conv_ln_b, l), w["wpw"], 0, f"pointwise_bwd{l}")
        g_pw, = _grad_tn(act, dpw, f"grad_w_pw{l}")
        by_shard = lambda g, n: jnp.transpose(g.reshape(g.shape[0], -1, n), (1, 0, 2)).reshape(-1, n)
        early = [by_shard(g_q, HP), by_shard(g_kv, 512), g_pw, g_out]
        conv_args = (dcv, sv["zb"], dzb, row(glu_b, l), w["dw"], 0, f"conv_bwd{l}")
        in_args = (sv["x"], dx, row(norm_g, l), scale, shift, w["w_in"], 0, f"inproj_bwd{l}")
        if l == 0:
            parts = shards_of(early)
            (dzb, g_dw, ddwb, dglub), theirs = _conv_bwd(*conv_args, carried=_pair_exchange(parts))
            sums = pair_sums("0e", parts, theirs)
            (g_in,), landed = _grad_w_in(dza, dzkr, dzb, sv["hb"], f"grad_w_in{l}", carried=_chip_scatter(sums))
            e_q, e_kv, e_pw, e_out = chip_sums("0e", sums, landed)
            parts = shards_of([g_in, g_dw.reshape(4 * HALO, DW_SHARD)])
            sums = pair_sums("0l", parts, _run_alone(_pair_exchange(parts), "pair_exchange_late0"))
            (dx, dshift, dgg), landed = _inproj_bwd(dza, dzkr, dzb, *in_args, carried=_chip_scatter(sums))
            l_in, l_dw = chip_sums("0l", sums, landed)
            halves[0] = [l_in, e_q, e_kv, l_dw, e_pw, e_out]
        else:
            dzb, g_dw, ddwb, dglub = _conv_bwd(*conv_args)
            dx, dshift, dgg = _inproj_bwd(dza, dzkr, dzb, *in_args)
            g_in, = _grad_w_in(dza, dzkr, dzb, sv["hb"], f"grad_w_in{l}")
        big[l] = [g_in, early[0], early[1], g_dw.reshape(4 * HALO, DW_SHARD), g_pw, g_out]
        small[l] = dict(ada_b=jnp.concatenate([dshift, dgg * row(norm_g, l), jnp.sum(dgate, axis=0)], axis=1), norm_g=dgg * (1.0 + scale),
                        q_lat_g=dgql, kv_lat_g=dgkvl, q_norm_g=_undup(dgq), k_norm_g=_undup(dgk), glu_b=dglub, dw_b=ddwb,
                        conv_ln_g=dlng, conv_ln_b=dlnb, b_pw=dbpw)
    grad_x = dx.reshape(x.shape)

    names = [n for n, _ in SMALL]
    mine = _pack_small({n: jnp.concatenate([small[0][n], small[1][n]], axis=0) for n in names})
    n_small = mine.shape[0]
    mine = jnp.concatenate([mine, jnp.zeros((8, 128), F32).at[0, 0].set(jnp.sum(tok_loss))], axis=0)
    tail = lambda a, fill: jnp.concatenate([a, jnp.full((8, 128), fill, F32)], axis=0)
    gathered = _allgather8(mine, "gather_small")
    weights = dict(ada_b=ada_b, norm_g=norm_g, q_lat_g=q_lat_g, kv_lat_g=kv_lat_g, q_norm_g=q_norm_g, k_norm_g=k_norm_g,
                   glu_b=glu_b, dw_b=dw_b, conv_ln_g=conv_ln_g, conv_ln_b=conv_ln_b, b_pw=b_pw)
    m_small = dict(ada_b=m_ada_b, norm_g=m_norm_g, q_lat_g=m_q_lat_g, kv_lat_g=m_kv_lat_g, q_norm_g=m_q_norm_g, k_norm_g=m_k_norm_g,
                   glu_b=m_glu_b, dw_b=m_dw_b, conv_ln_g=m_conv_ln_g, conv_ln_b=m_conv_ln_b, b_pw=m_b_pw)
    v_small = dict(ada_b=v_ada_b, norm_g=v_norm_g, q_lat_g=v_q_lat_g, kv_lat_g=v_kv_lat_g, q_norm_g=v_q_norm_g, k_norm_g=v_k_norm_g,
                   glu_b=v_glu_b, dw_b=v_dw_b, conv_ln_g=v_conv_ln_g, conv_ln_b=v_conv_ln_b, b_pw=v_b_pw)
    widths = {n: weights[n].shape[1] for n in names}
    v_packed = _pack_small({n: jnp.pad(v_small[n], ((0, 0), (0, dict(SMALL)[n] - widths[n])), constant_values=1.0) for n in names})
    updated = _small_update(gathered, tail(_pack_small(weights), 0.0), tail(_pack_small(m_small), 0.0), tail(v_packed, 1.0))
    loss = updated[0][n_small, 0]
    small_out = [_unpack_small(a[:n_small], widths) for a in updated]

    ada_rows = gathered[:, :n_small].reshape(8, nl, -1)[:, :, :3 * d]
    dmod = lax.dynamic_slice_in_dim(jnp.transpose(ada_rows, (1, 0, 2)), shard * n_ada, n_ada, axis=2)
    ada_out = _ada_update(c_all, dmod, ada_w, m_ada_w, v_ada_w)

    full = _run_alone(_pair_complete(halves[0] + halves[1]), "pair_complete")
    per_layer = [full[l * 6:(l + 1) * 6] for l in range(nl)]

    def natural_q(g):
        return jnp.transpose(_undup(g.reshape(2, QL, HP)), (1, 0, 2)).reshape(QL, 2 * QK)

    grads = [[per_layer[l][0], natural_q(per_layer[l][1]), per_layer[l][2], per_layer[l][3][:CONV_K], per_layer[l][4], per_layer[l][5]]
             for l in range(nl)]
    sharded = (("w_in", tr(w_in), tr(m_w_in), tr(v_w_in)), ("w_q_up", w_q_up, m_w_q_up, v_w_q_up),
               ("w_kv_up", w_kv_up, m_w_kv_up, v_w_kv_up), ("dw_w", dw_w, m_dw_w, v_dw_w),
               ("w_pw", w_pw, m_w_pw, v_w_pw), ("w_out", w_out, m_w_out, v_w_out))
    big_out = {name: _adam_update(w, grads[0][e], grads[1][e], m, v, f"adam_{name}") for e, (name, w, m, v) in enumerate(sharded)}
    big_out["w_in"] = [tr(a) for a in big_out["w_in"]]

    order = ["ada_w", "ada_b", "norm_g", "w_in", "q_lat_g", "w_q_up", "kv_lat_g", "w_kv_up", "q_norm_g", "k_norm_g", "glu_b",
             "dw_w", "dw_b", "conv_ln_g", "conv_ln_b", "w_pw", "b_pw", "w_out"]

    def leaf(kind, name):
        if name == "ada_w":
            return ada_out[kind]
        if name in big_out:
            return big_out[name][kind]
        return small_out[kind][name]

    return (loss, grad_x, *[leaf(kind, name) for kind in range(4) for name in order])
```
